```python
import jax, jax.numpy as jnp
from jax import lax
import numpy as np

D_MODEL = 1024
BATCH = 2
SEQ = 8192
DEPTH = 1

GRID_W = 64
CTX_LEN = 256
HG_HEADS = 8
HG_DK = 128
HG_DV = 128
HG_WIDTH = HG_HEADS * HG_DK
HG_CHUNK = 64
NA_HEADS = 16
NA_HD = 64
NA_WIDTH = NA_HEADS * NA_HD
NA_WIN_R = 8
NA_WIN_C = 16
NA_QCOLS = 16
NA_KCOLS = 32
ROPE_THETA = 10000.0
NEG_INF = -1e30
N_EXPERTS = 64
EXPERT_DIM = 256
TOP_K = 8
N_GROUPS = 8
TOPK_GROUPS = 4
ROUTED_SCALE = 2.5
SHARED_DIM = 256
MOE_BLOCK = 128
LN_EPS = 1e-6
IN_SPLITS = (HG_WIDTH, HG_WIDTH, HG_WIDTH, HG_WIDTH, HG_WIDTH, NA_WIDTH, NA_WIDTH, NA_WIDTH, D_MODEL, D_MODEL)
IN_WIDTH = 5 * HG_WIDTH + 3 * NA_WIDTH + 2 * D_MODEL

kernel_name = "hybrid_hgrn2_natten_moe_dit_block"


def _normalize(x):
    xf = x.astype(jnp.float32)
    mu = jnp.mean(xf, axis=-1, keepdims=True)
    var = jnp.mean(jnp.square(xf - mu), axis=-1, keepdims=True)
    return (xf - mu) * lax.rsqrt(var + LN_EPS)


def layer_norm(x, g, b):
    return (_normalize(x) * g.astype(jnp.float32) + b.astype(jnp.float32)).astype(x.dtype)


def modulate(x, shift, scale):
    return (_normalize(x) * (1.0 + scale.astype(jnp.float32)) + shift.astype(jnp.float32)).astype(x.dtype)


def split_cols(p):
    offs = np.cumsum(IN_SPLITS)[:-1].tolist()
    return jnp.split(p, offs, axis=-1)


def to_heads(t, n_heads):
    b, s, w = t.shape
    return t.reshape(b, s, n_heads, w // n_heads).transpose(0, 2, 1, 3)


def from_heads(t):
    b, h, s, d = t.shape
    return t.transpose(0, 2, 1, 3).reshape(b, s, h * d)


def gla_chunked(q, k, v, log_f, s0):
    b, h, t, dk = k.shape
    dv = v.shape[-1]
    n = t // HG_CHUNK
    ch = lambda u: u.reshape(b, h, n, HG_CHUNK, u.shape[-1])
    k, v, log_f = ch(k), ch(v), ch(log_f)
    a = jnp.cumsum(log_f, axis=3)
    a_last = a[:, :, :, -1:, :]
    u = jnp.einsum('bhnck,bhncv->nbhkv', k * jnp.exp(a_last - a), v)
    decay = jnp.moveaxis(jnp.exp(a_last[:, :, :, 0, :]), 2, 0)

    def step(s, inp):
        d, uc = inp
        return d[..., None] * s + uc, s

    s_final, s_prev = lax.scan(step, s0, (decay, u))
    if q is None:
        return None, s_final
    q = ch(q)
    qa = q * jnp.exp(a)
    kb = k * jnp.exp(-a)
    scores = jnp.einsum('bhnik,bhnjk->bhnij', qa, kb)
    causal = jnp.tril(jnp.ones((HG_CHUNK, HG_CHUNK), dtype=bool))
    scores = jnp.where(causal, scores, 0.0)
    o = jnp.einsum('bhnij,bhnjv->bhniv', scores, v) + jnp.einsum('bhnik,nbhkv->bhniv', qa, s_prev)
    return o.reshape(b, h, t, dv), s_final


def hgrn2_bidir(q, i, f_fwd, f_bwd, lb_fwd, lb_bwd, s_fwd0, s_bwd0, with_output):
    def gates(f_raw, lb):
        lb = lb[None, :, None, :]
        f = lb + (1.0 - lb) * jax.nn.sigmoid(f_raw)
        return 1.0 - f, jnp.log(f)

    rev = lambda t: jnp.flip(t, axis=2)
    k_f, lf_f = gates(f_fwd, lb_fwd)
    k_b, lf_b = gates(f_bwd, lb_bwd)
    qs = jax.nn.silu(q) if with_output else None
    o_f, s_f = gla_chunked(qs, k_f, i, lf_f, s_fwd0)
    o_b, s_b = gla_chunked(rev(qs) if with_output else None, rev(k_b), rev(i), rev(lf_b), s_bwd0)
    o = (o_f + rev(o_b)) if with_output else None
    return o, s_f, s_b


def hgrn2_readout(o, og, g):
    on = o * lax.rsqrt(jnp.mean(jnp.square(o), axis=-1, keepdims=True) + LN_EPS)
    on = from_heads(on) * g.astype(jnp.float32)
    return (on * jax.nn.silu(og.astype(jnp.float32))).astype(og.dtype)


def axial_rope(t):
    s, hd = t.shape[2], t.shape[3]
    half = hd // 2
    pos = jnp.arange(s)
    row = (pos // GRID_W).astype(jnp.float32)
    col = (pos % GRID_W).astype(jnp.float32)
    inv = jnp.power(ROPE_THETA, -jnp.arange(0, half, 2, dtype=jnp.float32) / half)

    def rot(u, p):
        ang = p[:, None] * inv[None, :]
        cos = jnp.cos(ang).astype(u.dtype)
        sin = jnp.sin(ang).astype(u.dtype)
        u1, u2 = jnp.split(u, 2, axis=-1)
        return jnp.concatenate([u1 * cos - u2 * sin, u1 * sin + u2 * cos], axis=-1)

    return jnp.concatenate([rot(t[..., :half], row), rot(t[..., half:], col)], axis=-1)


def na_indices(rows):
    wr = min(NA_WIN_R, rows)
    r = np.arange(rows)
    row_start = np.clip(r - wr // 2, 0, rows - wr)
    key_rows = row_start[:, None] + np.arange(wr)[None, :]
    ng = GRID_W // NA_QCOLS
    g = np.arange(ng)
    blk_start = np.clip(g * NA_QCOLS - NA_WIN_C // 2, 0, GRID_W - NA_KCOLS)
    key_cols = blk_start[:, None] + np.arange(NA_KCOLS)[None, :]
    qcol = g[:, None] * NA_QCOLS + np.arange(NA_QCOLS)[None, :]
    col_start = np.clip(qcol - NA_WIN_C // 2, 0, GRID_W - NA_WIN_C)
    in_win = (key_cols[:, None, :] >= col_start[:, :, None]) & (key_cols[:, None, :] < col_start[:, :, None] + NA_WIN_C)
    mask = np.broadcast_to(in_win[:, :, None, :], (ng, NA_QCOLS, wr, NA_KCOLS)).reshape(ng, NA_QCOLS, wr * NA_KCOLS)
    flat = (key_rows[:, None, :, None] * GRID_W + key_cols[None, :, None, :]).reshape(rows, ng, wr * NA_KCOLS)
    dr = key_rows - r[:, None] + (NA_WIN_R - 1)
    dc = np.clip(key_cols[:, None, :] - qcol[:, :, None] + (NA_WIN_C - 1), 0, 2 * NA_WIN_C - 2)
    return wr, ng, flat, mask, dr, dc


def neighborhood_attention(q_rot, k_rot, v, q, k_ctx, v_ctx, rpb):
    b, h, s, hd = q.shape
    rows = s // GRID_W
    wr, ng, flat, mask, dr, dc = na_indices(rows)
    nk = wr * NA_KCOLS
    idx = jnp.asarray(flat.reshape(-1))
    k_blk = jnp.take(k_rot, idx, axis=2).reshape(b, h, rows, ng, nk, hd)
    v_blk = jnp.take(v, idx, axis=2).reshape(b, h, rows, ng, nk, hd)
    qr = q_rot.reshape(b, h, rows, ng, NA_QCOLS, hd)
    qp = q.reshape(b, h, rows, ng, NA_QCOLS, hd)
    scale = NA_HD ** -0.5
    bias = rpb[:, dr[:, None, None, :, None], dc[None, :, :, None, :]]
    bias = bias.reshape(h, rows, ng, NA_QCOLS, nk).astype(jnp.float32)
    s_win = jnp.einsum('bhrgqd,bhrgkd->bhrgqk', qr, k_blk, preferred_element_type=jnp.float32) * scale + bias
    s_win = jnp.where(jnp.asarray(mask)[None, None, None], s_win, NEG_INF)
    s_ctx = jnp.einsum('bhrgqd,bhcd->bhrgqc', qp, k_ctx, preferred_element_type=jnp.float32) * scale
    p = jax.nn.softmax(jnp.concatenate([s_win, s_ctx], axis=-1), axis=-1).astype(v.dtype)
    o = (jnp.einsum('bhrgqk,bhrgkd->bhrgqd', p[..., :nk], v_blk)
         + jnp.einsum('bhrgqc,bhcd->bhrgqd', p[..., nk:], v_ctx))
    return o.reshape(b, h, s, hd)


def context_attention(q, k, v):
    s = jnp.einsum('bhqd,bhkd->bhqk', q, k, preferred_element_type=jnp.float32) * NA_HD ** -0.5
    p = jax.nn.softmax(s, axis=-1).astype(v.dtype)
    return jnp.einsum('bhqk,bhkd->bhqd', p, v)


def token_mixer(h, hc, w_in, lb_fwd, lb_bwd, hg_norm_g, rpb, w_branch_a, w_branch_b, w_out, ctx_out):
    b = h.shape[0]
    q_hg, ff, fb, i_hg, og, q_na, k_na, v_na, g_a, g_b = split_cols(h @ w_in)
    cq_hg, cff, cfb, ci_hg, cog, cq_na, ck_na, cv_na, cg_a, cg_b = split_cols(hc @ w_in)
    hh = lambda t: to_heads(t, HG_HEADS).astype(jnp.float32)
    s0 = jnp.zeros((b, HG_HEADS, HG_DK, HG_DV), jnp.float32)
    o_c, s_cf, s_cb = hgrn2_bidir(hh(cq_hg), hh(ci_hg), hh(cff), hh(cfb), lb_fwd, lb_bwd, s0, s0, ctx_out)
    o, _, _ = hgrn2_bidir(hh(q_hg), hh(i_hg), hh(ff), hh(fb), lb_fwd, lb_bwd, s_cf, s_cb, True)
    y_hg = hgrn2_readout(o, og, hg_norm_g)
    na = lambda t: to_heads(t, NA_HEADS)
    qn, kn, vn = na(q_na), na(k_na), na(v_na)
    kc, vc = na(ck_na), na(cv_na)
    y_na = from_heads(neighborhood_attention(axial_rope(qn), axial_rope(kn), vn, qn, kc, vc, rpb))
    y = (jax.nn.sigmoid(g_a) * (y_hg @ w_branch_a) + jax.nn.sigmoid(g_b) * (y_na @ w_branch_b)) @ w_out
    if not ctx_out:
        return y, None
    yc_hg = hgrn2_readout(o_c, cog, hg_norm_g)
    yc_na = from_heads(context_attention(na(cq_na), kc, vc))
    yc = (jax.nn.sigmoid(cg_a) * (yc_hg @ w_branch_a) + jax.nn.sigmoid(cg_b) * (yc_na @ w_branch_b)) @ w_out
    return y, yc


def moe_ffn(h, w_router, router_bias, w_e_gate, w_e_up, w_e_down, w_sh_gate, w_sh_up, w_sh_down):
    tok = h.reshape(-1, MOE_BLOCK, h.shape[-1])

    def block(xb):
        scores = jax.nn.sigmoid(jnp.matmul(xb, w_router, preferred_element_type=jnp.float32))
        sel = scores + router_bias.astype(jnp.float32)
        grp = sel.reshape(-1, N_GROUPS, N_EXPERTS // N_GROUPS)
        grp_score = jnp.sum(lax.top_k(grp, 2)[0], axis=-1)
        _, top_g = lax.top_k(grp_score, TOPK_GROUPS)
        gmask = jnp.sum(jax.nn.one_hot(top_g, N_GROUPS, dtype=jnp.float32), axis=1)
        emask = jnp.repeat(gmask, N_EXPERTS // N_GROUPS, axis=-1)
        sel = jnp.where(emask > 0, sel, -jnp.inf)
        _, top_e = lax.top_k(sel, TOP_K)
        w = jnp.take_along_axis(scores, top_e, axis=-1)
        w = w / jnp.sum(w, axis=-1, keepdims=True) * ROUTED_SCALE
        gates = jnp.sum(jax.nn.one_hot(top_e, N_EXPERTS, dtype=jnp.float32) * w[..., None], axis=1)
        a = jnp.einsum('td,edf->tef', xb, w_e_gate)
        u = jnp.einsum('td,edf->tef', xb, w_e_up)
        act = jax.nn.silu(a) * u * gates[..., None].astype(xb.dtype)
        routed = jnp.einsum('tef,efd->td', act, w_e_down)
        shared = (jax.nn.silu(xb @ w_sh_gate) * (xb @ w_sh_up)) @ w_sh_down
        return routed + shared

    return lax.map(block, tok).reshape(h.shape)


def setup_inputs(seed: int = 0) -> dict:
    key = jax.random.key(seed)
    ks = jax.random.split(key, 26)
    f32 = jnp.float32
    beta = (8.0 * DEPTH) ** -0.25
    nrm = lambda k, shape, s: jax.random.normal(k, shape, f32) * s
    col_scale = jnp.ones((IN_WIDTH,), f32)
    col_scale = col_scale.at[3 * HG_WIDTH:4 * HG_WIDTH].set(beta)
    col_scale = col_scale.at[5 * HG_WIDTH + 2 * NA_WIDTH:5 * HG_WIDTH + 3 * NA_WIDTH].set(beta)
    return {
        "x": nrm(ks[0], (BATCH, SEQ, D_MODEL), 1.0),
        "c": nrm(ks[1], (BATCH, D_MODEL), 1.0),
        "ctx": nrm(ks[2], (BATCH, CTX_LEN, D_MODEL), 1.0),
        "c_ctx": nrm(ks[3], (D_MODEL,), 1.0),
        "w_ada": nrm(ks[4], (DEPTH, D_MODEL, 6 * D_MODEL), 0.5 * D_MODEL ** -0.5),
        "b_ada": nrm(ks[5], (DEPTH, 6 * D_MODEL), 0.02),
        "w_in": nrm(ks[6], (DEPTH, D_MODEL, IN_WIDTH), D_MODEL ** -0.5) * col_scale,
        "hg_lb_fwd": nrm(ks[7], (DEPTH + 1, HG_WIDTH), 0.1),
        "hg_lb_bwd": nrm(ks[8], (DEPTH + 1, HG_WIDTH), 0.1),
        "hg_norm_g": 1.0 + nrm(ks[9], (DEPTH, HG_WIDTH), 0.02),
        "na_rpb": nrm(ks[10], (DEPTH, NA_HEADS, 2 * NA_WIN_R - 1, 2 * NA_WIN_C - 1), 0.02),
        "w_branch_a": nrm(ks[11], (DEPTH, HG_WIDTH, D_MODEL), HG_WIDTH ** -0.5),
        "w_branch_b": nrm(ks[12], (DEPTH, NA_WIDTH, D_MODEL), NA_WIDTH ** -0.5),
        "w_out": nrm(ks[13], (DEPTH, D_MODEL, D_MODEL), beta * D_MODEL ** -0.5),
        "ln1_g": 1.0 + nrm(ks[14], (DEPTH, D_MODEL), 0.02),
        "ln1_b": nrm(ks[15], (DEPTH, D_MODEL), 0.02),
        "w_router": nrm(ks[16], (DEPTH, D_MODEL, N_EXPERTS), D_MODEL ** -0.5),
        "router_bias": nrm(ks[17], (DEPTH, N_EXPERTS), 0.01),
        "w_e_gate": nrm(ks[18], (DEPTH, N_EXPERTS, D_MODEL, EXPERT_DIM), D_MODEL ** -0.5),
        "w_e_up": nrm(ks[19], (DEPTH, N_EXPERTS, D_MODEL, EXPERT_DIM), D_MODEL ** -0.5),
        "w_e_down": nrm(ks[20], (DEPTH, N_EXPERTS, EXPERT_DIM, D_MODEL), beta * EXPERT_DIM ** -0.5),
        "w_sh_gate": nrm(ks[21], (DEPTH, D_MODEL, SHARED_DIM), D_MODEL ** -0.5),
        "w_sh_up": nrm(ks[22], (DEPTH, D_MODEL, SHARED_DIM), D_MODEL ** -0.5),
        "w_sh_down": nrm(ks[23], (DEPTH, SHARED_DIM, D_MODEL), beta * SHARED_DIM ** -0.5),
        "ln2_g": 1.0 + nrm(ks[24], (DEPTH, D_MODEL), 0.02),
        "ln2_b": nrm(ks[25], (DEPTH, D_MODEL), 0.02),
    }


def reference(x, c, ctx, c_ctx, w_ada, b_ada, w_in, hg_lb_fwd, hg_lb_bwd, hg_norm_g, na_rpb,
              w_branch_a, w_branch_b, w_out, ln1_g, ln1_b, w_router, router_bias, w_e_gate, w_e_up,
              w_e_down, w_sh_gate, w_sh_up, w_sh_down, ln2_g, ln2_b):
    alpha = (2.0 * DEPTH) ** 0.25
    lb_fwd = jnp.cumsum(jax.nn.softmax(hg_lb_fwd.astype(jnp.float32), axis=0), axis=0)
    lb_bwd = jnp.cumsum(jax.nn.softmax(hg_lb_bwd.astype(jnp.float32), axis=0), axis=0)
    cond = jax.nn.silu(c)
    cond_ctx = jax.nn.silu(c_ctx)
    for l in range(DEPTH):
        ctx_needed = l < DEPTH - 1
        mod = (cond @ w_ada[l] + b_ada[l])[:, None, :]
        mod_c = cond_ctx @ w_ada[l] + b_ada[l]
        sh1, sc1, g1, sh2, sc2, g2 = jnp.split(mod, 6, axis=-1)
        csh1, csc1, cg1, csh2, csc2, cg2 = jnp.split(mod_c, 6, axis=-1)
        h = modulate(x, sh1, sc1)
        hc = modulate(ctx, csh1, csc1)
        y, yc = token_mixer(h, hc, w_in[l], lb_fwd[l].reshape(HG_HEADS, HG_DK),
                            lb_bwd[l].reshape(HG_HEADS, HG_DK), hg_norm_g[l], na_rpb[l],
                            w_branch_a[l], w_branch_b[l], w_out[l], ctx_needed)
        x = layer_norm(alpha * x + g1 * y, ln1_g[l], ln1_b[l])
        y2 = moe_ffn(modulate(x, sh2, sc2), w_router[l], router_bias[l], w_e_gate[l], w_e_up[l],
                     w_e_down[l], w_sh_gate[l], w_sh_up[l], w_sh_down[l])
        x = layer_norm(alpha * x + g2 * y2, ln2_g[l], ln2_b[l])
        if ctx_needed:
            ctx = layer_norm(alpha * ctx + cg1 * yc, ln1_g[l], ln1_b[l])
            yc2 = moe_ffn(modulate(ctx, csh2, csc2), w_router[l], router_bias[l], w_e_gate[l], w_e_up[l],
                          w_e_down[l], w_sh_gate[l], w_sh_up[l], w_sh_down[l])
            ctx = layer_norm(alpha * ctx + cg2 * yc2, ln2_g[l], ln2_b[l])
    return x
```

```python
import functools

import numpy as np
import jax
import jax.numpy as jnp
from jax import lax
from jax.experimental import pallas as pl
from jax.experimental.pallas import tpu as pltpu

F32 = jnp.float32
BF16 = jnp.bfloat16

D_MODEL = 1024
GRID_W = 64
HG_HEADS = 8
HG_DK = 128
HG_CHUNK = 64
NA_HEADS = 16
NA_HD = 64
NA_WIN_R = 8
NA_WIN_C = 16
ROPE_THETA = 10000.0
NEG_INF = -1e30
N_EXPERTS = 64
EXPERT_DIM = 256
TOP_K = 8
N_GROUPS = 8
TOPK_GROUPS = 4
ROUTED_SCALE = 2.5
LN_EPS = 1e-6
N_SECTIONS = 10
SEC_Q, SEC_FF, SEC_FB, SEC_I, SEC_OG, SEC_NQ, SEC_NK, SEC_NV, SEC_GA, SEC_GB = range(10)

VMEM_LIMIT = 56 * 1024 * 1024


def _cparams(sem):
    return pltpu.CompilerParams(dimension_semantics=sem, vmem_limit_bytes=VMEM_LIMIT)


def _normalize(x):
    mu = jnp.mean(x, axis=-1, keepdims=True)
    xc = x - mu
    var = jnp.mean(xc * xc, axis=-1, keepdims=True)
    return xc * lax.rsqrt(var + LN_EPS)


def _silu(x):
    return x * jax.nn.sigmoid(x)


def _dot(a, b):
    return jnp.dot(a, b, preferred_element_type=F32)


def _dot_nt(a, b):
    return lax.dot_general(a, b, (((1,), (1,)), ((), ())), preferred_element_type=F32)


def _dot_tn(a, b):
    return lax.dot_general(a, b, (((0,), (0,)), ((), ())), preferred_element_type=F32)


def _split3(x):
    hi = x.astype(BF16)
    r1 = x - hi.astype(F32)
    mid = r1.astype(BF16)
    lo = (r1 - mid.astype(F32)).astype(BF16)
    return hi, mid, lo


def _ada_kernel(c_ref, w_ref, b_ref, o_ref):
    cond = _silu(c_ref[...])
    o_ref[...] = _dot(cond.astype(BF16), w_ref[...].astype(BF16)) + b_ref[...]


def _ada(cond_rows, w_ada, b_ada):
    r, d = cond_rows.shape
    n = w_ada.shape[1]
    tn = 1024
    return pl.pallas_call(
        _ada_kernel,
        grid=(n // tn,),
        in_specs=[pl.BlockSpec((r, d), lambda j: (0, 0)),
                  pl.BlockSpec((d, tn), lambda j: (0, j)),
                  pl.BlockSpec((1, tn), lambda j: (0, j))],
        out_specs=pl.BlockSpec((r, tn), lambda j: (0, j)),
        out_shape=jax.ShapeDtypeStruct((r, n), F32),
        compiler_params=_cparams(("arbitrary",)),
        name="ada",
    )(cond_rows, w_ada, b_ada.reshape(1, n))


def _inproj_kernel(x_ref, sh_ref, sc_ref, w_ref, o_ref, h_ref):
    @pl.when(pl.program_id(2) == 0)
    def _():
        h = _normalize(x_ref[...]) * (1.0 + sc_ref[...]) + sh_ref[...]
        h_ref[...] = h.astype(BF16)

    o_ref[...] = _dot(h_ref[...], w_ref[...])


def _inproj(x, shift, scale, w_in_bf16):
    b, s, d = x.shape
    tm = min(1024, s)
    nj = w_in_bf16.shape[1] // d
    return pl.pallas_call(
        _inproj_kernel,
        grid=(b, s // tm, nj),
        in_specs=[pl.BlockSpec((None, tm, d), lambda bi, i, j: (bi, i, 0)),
                  pl.BlockSpec((None, 1, d), lambda bi, i, j: (bi, 0, 0)),
                  pl.BlockSpec((None, 1, d), lambda bi, i, j: (bi, 0, 0)),
                  pl.BlockSpec((d, d), lambda bi, i, j: (0, j))],
        out_specs=pl.BlockSpec((None, None, tm, d), lambda bi, i, j: (j, bi, i, 0)),
        out_shape=jax.ShapeDtypeStruct((nj, b, s, d), F32),
        scratch_shapes=[pltpu.VMEM((tm, d), BF16)],
        compiler_params=_cparams(("arbitrary", "arbitrary", "arbitrary")),
        name="inproj",
    )(x, shift, scale, w_in_bf16)


def _hgrn_chunk(q, fraw, v, lb, st_ref, d, tri_bf16, keep, last_row):
    f = lb + (1.0 - lb) * jax.nn.sigmoid(fraw)
    k = 1.0 - f
    lf = jnp.log(f)
    hi, mid, lo = _split3(lf)
    a = _dot(tri_bf16, hi) + _dot(tri_bf16, mid) + _dot(tri_bf16, lo)
    a_last = a[last_row:last_row + 1, :]
    kd = (k * jnp.exp(a_last - a)).astype(BF16)
    decay = jnp.exp(a_last)
    vb = v.astype(BF16)
    if q is not None:
        qa = (_silu(q) * jnp.exp(a)).astype(BF16)
        kb = (k * jnp.exp(-a)).astype(BF16)
    outs = []
    for h in range(HG_HEADS):
        sl = slice(h * HG_DK, (h + 1) * HG_DK)
        st = st_ref[d, h]
        if q is not None:
            sc = jnp.where(keep, _dot_nt(qa[:, sl], kb[:, sl]), 0.0)
            outs.append(_dot(sc.astype(BF16), vb[:, sl]) + _dot_nt(qa[:, sl], st.astype(BF16)))
        st_ref[d, h] = st * decay[:, sl] + _dot_tn(vb[:, sl], kd[:, sl])
    if q is None:
        return None
    return jnp.concatenate(outs, axis=-1)


def _hgrn_kernel(qf_ref, ff_ref, if_ref, qb_ref, fb_ref, ib_ref, cff_ref, cfb_ref, ci_ref,
                 lbf_ref, lbb_ref, of_ref, ob_ref, st_ref, *, n_sub, n_ctx_sub):
    n = pl.program_id(1)
    c = HG_CHUNK
    row = lax.broadcasted_iota(jnp.int32, (c, c), 0)
    col = lax.broadcasted_iota(jnp.int32, (c, c), 1)
    keep_f = col <= row
    keep_b = col >= row
    tri_f = keep_f.astype(F32).astype(BF16)
    tri_b = keep_b.astype(F32).astype(BF16)
    lbf = lbf_ref[...]
    lbb = lbb_ref[...]

    @pl.when(n == 0)
    def _():
        st_ref[...] = jnp.zeros_like(st_ref)

        def body(i, carry):
            r0 = pl.multiple_of(i * c, c)
            _hgrn_chunk(None, cff_ref[pl.ds(r0, c), :], ci_ref[pl.ds(r0, c), :], lbf, st_ref, 0,
                        tri_f, keep_f, c - 1)
            r1 = pl.multiple_of((n_ctx_sub - 1 - i) * c, c)
            _hgrn_chunk(None, cfb_ref[pl.ds(r1, c), :], ci_ref[pl.ds(r1, c), :], lbb, st_ref, 1,
                        tri_b, keep_b, 0)
            return carry

        lax.fori_loop(0, n_ctx_sub, body, 0)

    @pl.when(n > 0)
    def _():
        def body(i, carry):
            r0 = pl.multiple_of(i * c, c)
            of_ref[pl.ds(r0, c), :] = _hgrn_chunk(
                qf_ref[pl.ds(r0, c), :], ff_ref[pl.ds(r0, c), :], if_ref[pl.ds(r0, c), :], lbf, st_ref, 0,
                tri_f, keep_f, c - 1)
            r1 = pl.multiple_of((n_sub - 1 - i) * c, c)
            ob_ref[pl.ds(r1, c), :] = _hgrn_chunk(
                qb_ref[pl.ds(r1, c), :], fb_ref[pl.ds(r1, c), :], ib_ref[pl.ds(r1, c), :], lbb, st_ref, 1,
                tri_b, keep_b, 0)
            return carry

        lax.fori_loop(0, n_sub, body, 0)


def _hgrn(p, pc, lb_fwd, lb_bwd):
    _, b, s, w = p.shape
    ctx_len = pc.shape[2]
    tb = min(256, s)
    nb = s // tb
    fwd = lambda bi, n: jnp.maximum(n - 1, 0)
    bwd = lambda bi, n: nb - 1 - jnp.maximum(n - 1, 0)

    def sec(section, blk):
        return pl.BlockSpec((None, None, tb, w), lambda bi, n: (section, bi, blk(bi, n), 0))

    def csec(section):
        return pl.BlockSpec((None, None, ctx_len, w), lambda bi, n: (section, bi, 0, 0))

    vec = pl.BlockSpec((1, w), lambda bi, n: (0, 0))
    kern = functools.partial(_hgrn_kernel, n_sub=tb // HG_CHUNK, n_ctx_sub=ctx_len // HG_CHUNK)
    return pl.pallas_call(
        kern,
        grid=(b, nb + 1),
        in_specs=[sec(SEC_Q, fwd), sec(SEC_FF, fwd), sec(SEC_I, fwd),
                  sec(SEC_Q, bwd), sec(SEC_FB, bwd), sec(SEC_I, bwd),
                  csec(SEC_FF), csec(SEC_FB), csec(SEC_I), vec, vec],
        out_specs=[pl.BlockSpec((None, tb, w), lambda bi, n: (bi, fwd(bi, n), 0)),
                   pl.BlockSpec((None, tb, w), lambda bi, n: (bi, bwd(bi, n), 0))],
        out_shape=[jax.ShapeDtypeStruct((b, s, w), F32), jax.ShapeDtypeStruct((b, s, w), F32)],
        scratch_shapes=[pltpu.VMEM((2, HG_HEADS, HG_DK, HG_DK), F32)],
        compiler_params=_cparams(("arbitrary", "arbitrary")),
        name="hgrn",
    )(p, p, p, p, p, p, pc, pc, pc, lb_fwd.reshape(1, w), lb_bwd.reshape(1, w))


NA_ROWS_PER_STEP = 8
NA_PREP_ROWS = 512


def _rope(t, cos, sin_signed, first_half):
    w = t.shape[-1]
    partner = jnp.where(first_half, pltpu.roll(t, w - 16, 1), pltpu.roll(t, 16, 1))
    return t * cos + partner * sin_signed


def _natten_kernel(q_ref, k_ref, v_ref, kc_ref, vc_ref, cos_ref, sin_ref, bias_ref, o_ref,
                   krot_s, v_s, kc_s, vc_s, *, rows):
    rblk = pl.program_id(2)
    hd = NA_HD
    lane = lax.broadcasted_iota(jnp.int32, (1, 2 * hd), 1)
    first_half = (lane % 32) < 16
    scale = NA_HD ** -0.5

    @pl.when(rblk == 0)
    def _():
        kc = kc_ref[...].astype(BF16)
        vc = vc_ref[...].astype(BF16)
        for h in range(2):
            sl = slice(h * hd, (h + 1) * hd)
            kc_s[h] = kc[:, sl]
            vc_s[h] = vc[:, sl]

        def prep(i, carry):
            r0 = pl.multiple_of(i * NA_PREP_ROWS, NA_PREP_ROWS)
            rws = pl.ds(r0, NA_PREP_ROWS)
            kr = _rope(k_ref[rws, :], cos_ref[rws, :], sin_ref[rws, :], first_half).astype(BF16)
            vv = v_ref[rws, :].astype(BF16)
            for h in range(2):
                sl = slice(h * hd, (h + 1) * hd)
                krot_s[h, rws, :] = kr[:, sl]
                v_s[h, rws, :] = vv[:, sl]
            return carry

        lax.fori_loop(0, k_ref.shape[0] // NA_PREP_ROWS, prep, 0)

    nkey = NA_WIN_R * GRID_W

    def body(rr, carry):
        r = rblk * NA_ROWS_PER_STEP + rr
        rs = jnp.clip(r - NA_WIN_R // 2, 0, rows - NA_WIN_R)
        variant = r - rs
        q0 = pl.multiple_of(rr * GRID_W, GRID_W)
        t0 = pl.multiple_of(r * GRID_W, GRID_W)
        k0 = pl.multiple_of(rs * GRID_W, GRID_W)
        q = q_ref[pl.ds(q0, GRID_W), :]
        qr = _rope(q, cos_ref[pl.ds(t0, GRID_W), :], sin_ref[pl.ds(t0, GRID_W), :], first_half)
        qb = q.astype(BF16)
        qrb = qr.astype(BF16)
        outs = []
        for h in range(2):
            sl = slice(h * hd, (h + 1) * hd)
            s_win = _dot_nt(qrb[:, sl], krot_s[h, pl.ds(k0, nkey), :]) * scale + bias_ref[h, variant]
            s_ctx = _dot_nt(qb[:, sl], kc_s[h]) * scale
            m = jnp.maximum(jnp.max(s_win, axis=-1, keepdims=True), jnp.max(s_ctx, axis=-1, keepdims=True))
            e_win = jnp.exp(s_win - m)
            e_ctx = jnp.exp(s_ctx - m)
            denom = jnp.sum(e_win, axis=-1, keepdims=True) + jnp.sum(e_ctx, axis=-1, keepdims=True)
            p_win = (e_win / denom).astype(BF16)
            p_ctx = (e_ctx / denom).astype(BF16)
            outs.append(_dot(p_win, v_s[h, pl.ds(k0, nkey), :]) + _dot(p_ctx, vc_s[h]))
        o_ref[pl.ds(q0, GRID_W), :] = jnp.concatenate(outs, axis=-1)
        return carry

    lax.fori_loop(0, NA_ROWS_PER_STEP, body, 0)


def _na_tables(rpb, s):
    half = NA_HD // 2
    pos = jnp.arange(s)
    rowp = (pos // GRID_W).astype(F32)
    colp = (pos % GRID_W).astype(F32)
    inv = jnp.power(ROPE_THETA, -jnp.arange(0, half, 2, dtype=F32) / half)
    ang_r = rowp[:, None] * inv[None, :]
    ang_c = colp[:, None] * inv[None, :]
    cos = jnp.concatenate([jnp.cos(ang_r), jnp.cos(ang_r), jnp.cos(ang_c), jnp.cos(ang_c)], axis=-1)
    sin = jnp.concatenate([-jnp.sin(ang_r), jnp.sin(ang_r), -jnp.sin(ang_c), jnp.sin(ang_c)], axis=-1)
    cos = jnp.concatenate([cos, cos], axis=-1)
    sin = jnp.concatenate([sin, sin], axis=-1)

    qcol = np.arange(GRID_W)
    kcol = np.arange(GRID_W)
    col_start = np.clip(qcol - NA_WIN_C // 2, 0, GRID_W - NA_WIN_C)
    in_win = (kcol[None, :] >= col_start[:, None]) & (kcol[None, :] < col_start[:, None] + NA_WIN_C)
    dc = np.clip(kcol[None, :] - qcol[:, None] + (NA_WIN_C - 1), 0, 2 * NA_WIN_C - 2)
    variant = np.arange(NA_WIN_R)
    j = np.arange(NA_WIN_R)
    dr = (NA_WIN_R - 1) - variant[:, None] + j[None, :]
    bias = rpb.astype(F32)[:, dr[:, None, :, None], dc[None, :, None, :]]
    bias = jnp.where(jnp.asarray(in_win)[None, None, :, None, :], bias, NEG_INF)
    bias = bias.reshape(rpb.shape[0], NA_WIN_R, GRID_W, NA_WIN_R * GRID_W)
    return cos, sin, bias


def _natten(p, pc, cos, sin, bias):
    _, b, s, w = p.shape
    ctx_len = pc.shape[2]
    rows = s // GRID_W
    assert rows >= NA_WIN_R and rows % NA_ROWS_PER_STEP == 0
    tq = NA_ROWS_PER_STEP * GRID_W
    hw = 2 * NA_HD
    nhp = w // hw
    kern = functools.partial(_natten_kernel, rows=rows)
    return pl.pallas_call(
        kern,
        grid=(b, nhp, rows // NA_ROWS_PER_STEP),
        in_specs=[pl.BlockSpec((None, None, tq, hw), lambda bi, hp, r: (SEC_NQ, bi, r, hp)),
                  pl.BlockSpec((None, None, s, hw), lambda bi, hp, r: (SEC_NK, bi, 0, hp)),
                  pl.BlockSpec((None, None, s, hw), lambda bi, hp, r: (SEC_NV, bi, 0, hp)),
                  pl.BlockSpec((None, None, ctx_len, hw), lambda bi, hp, r: (SEC_NK, bi, 0, hp)),
                  pl.BlockSpec((None, None, ctx_len, hw), lambda bi, hp, r: (SEC_NV, bi, 0, hp)),
                  pl.BlockSpec((s, hw), lambda bi, hp, r: (0, 0)),
                  pl.BlockSpec((s, hw), lambda bi, hp, r: (0, 0)),
                  pl.BlockSpec((2, NA_WIN_R, GRID_W, NA_WIN_R * GRID_W), lambda bi, hp, r: (hp, 0, 0, 0))],
        out_specs=pl.BlockSpec((None, tq, hw), lambda bi, hp, r: (bi, r, hp)),
        out_shape=jax.ShapeDtypeStruct((b, s, w), F32),
        scratch_shapes=[pltpu.VMEM((2, s, NA_HD), BF16), pltpu.VMEM((2, s, NA_HD), BF16),
                        pltpu.VMEM((2, ctx_len, NA_HD), BF16), pltpu.VMEM((2, ctx_len, NA_HD), BF16)],
        compiler_params=_cparams(("arbitrary", "arbitrary", "arbitrary")),
        name="natten",
    )(p, p, p, pc, pc, cos, sin, bias)


def _route(logits_t, rbias):
    e, t = logits_t.shape
    gsz = e // N_GROUPS
    scores = jax.nn.sigmoid(logits_t)
    sel = scores + rbias
    neg = -jnp.inf
    sub = lax.broadcasted_iota(jnp.int32, (gsz, t), 0).astype(F32)
    gscore = []
    for g in range(N_GROUPS):
        grp = sel[g * gsz:(g + 1) * gsz, :]
        m1 = jnp.max(grp, axis=0, keepdims=True)
        first = jnp.min(jnp.where(grp == m1, sub, float(gsz)), axis=0, keepdims=True)
        m2 = jnp.max(jnp.where(sub == first, neg, grp), axis=0, keepdims=True)
        gscore.append(m1 + m2)
    masked = []
    for g in range(N_GROUPS):
        rank = jnp.zeros((1, t), F32)
        for g2 in range(N_GROUPS):
            if g2 == g:
                continue
            if g2 < g:
                ahead = gscore[g2] >= gscore[g]
            else:
                ahead = gscore[g2] > gscore[g]
            rank = rank + jnp.where(ahead, 1.0, 0.0)
        masked.append(jnp.where(rank < TOPK_GROUPS, sel[g * gsz:(g + 1) * gsz, :], neg))
    work = jnp.concatenate(masked, axis=0)
    eidx = lax.broadcasted_iota(jnp.int32, (e, t), 0).astype(F32)
    wsel = jnp.zeros((e, t), F32)
    for _ in range(TOP_K):
        m = jnp.max(work, axis=0, keepdims=True)
        first = jnp.min(jnp.where(work == m, eidx, float(e)), axis=0, keepdims=True)
        pick = eidx == first
        wsel = jnp.where(pick, scores, wsel)
        work = jnp.where(pick, neg, work)
    return wsel / jnp.sum(wsel, axis=0, keepdims=True) * ROUTED_SCALE


def _merge_kernel(of_ref, ob_ref, og_ref, yna_ref, ga_ref, gb_ref, x_ref, g1_ref, sh2_ref, sc2_ref,
                  hgg_ref, ln1g_ref, ln1b_ref, wa_ref, wb_ref, wo_ref, wr_ref, rb_ref,
                  x1_ref, h2_ref, gates_ref, *, alpha):
    o = of_ref[...] + ob_ref[...]
    parts = []
    for h in range(HG_HEADS):
        oh = o[:, h * HG_DK:(h + 1) * HG_DK]
        parts.append(oh * lax.rsqrt(jnp.mean(oh * oh, axis=-1, keepdims=True) + LN_EPS))
    y_hg = jnp.concatenate(parts, axis=-1) * hgg_ref[...] * _silu(og_ref[...])
    t = (jax.nn.sigmoid(ga_ref[...]) * _dot(y_hg.astype(BF16), wa_ref[...])
         + jax.nn.sigmoid(gb_ref[...]) * _dot(yna_ref[...].astype(BF16), wb_ref[...]))
    y = _dot(t.astype(BF16), wo_ref[...])
    x1 = _normalize(alpha * x_ref[...] + g1_ref[...] * y) * ln1g_ref[...] + ln1b_ref[...]
    x1_ref[...] = x1
    h2 = _normalize(x1) * (1.0 + sc2_ref[...]) + sh2_ref[...]
    h2_ref[...] = h2.astype(BF16)
    hh, hm, hl = _split3(h2)
    wh, wm, wl = _split3(wr_ref[...])
    logits_t = (_dot_nt(wh, hh) + _dot_nt(wh, hm) + _dot_nt(wm, hh)
                + _dot_nt(wh, hl) + _dot_nt(wl, hh) + _dot_nt(wm, hm))
    gates_ref[...] = _route(logits_t, rb_ref[...]).T


def _merge(o_f, o_b, p, y_na, x, g1, sh2, sc2, hg_norm_g, ln1_g, ln1_b, w_a, w_b, w_o, w_router_t, router_bias,
           alpha):
    b, s, d = x.shape
    tm = min(256, s)
    e = w_router_t.shape[0]
    tok = lambda bi, i: (bi, i, 0)
    blk = pl.BlockSpec((None, tm, d), tok)

    def sec(section):
        return pl.BlockSpec((None, None, tm, d), lambda bi, i: (section, bi, i, 0))

    mod = pl.BlockSpec((None, 1, d), lambda bi, i: (bi, 0, 0))
    vec = pl.BlockSpec((1, d), lambda bi, i: (0, 0))
    mat = pl.BlockSpec((d, d), lambda bi, i: (0, 0))
    return pl.pallas_call(
        functools.partial(_merge_kernel, alpha=alpha),
        grid=(b, s // tm),
        in_specs=[blk, blk, sec(SEC_OG), blk, sec(SEC_GA), sec(SEC_GB), blk, mod, mod, mod,
                  vec, vec, vec, mat, mat, mat,
                  pl.BlockSpec((e, d), lambda bi, i: (0, 0)),
                  pl.BlockSpec((e, 1), lambda bi, i: (0, 0))],
        out_specs=[blk, blk, pl.BlockSpec((None, tm, e), tok)],
        out_shape=[jax.ShapeDtypeStruct((b, s, d), F32), jax.ShapeDtypeStruct((b, s, d), BF16),
                   jax.ShapeDtypeStruct((b, s, e), F32)],
        compiler_params=_cparams(("arbitrary", "arbitrary")),
        name="merge",
    )(o_f, o_b, p, y_na, p, p, x, g1, sh2, sc2, hg_norm_g.reshape(1, d), ln1_g.reshape(1, d),
      ln1_b.reshape(1, d), w_a, w_b, w_o, w_router_t, router_bias.reshape(e, 1))


def _moe_kernel(h_ref, gates_ref, x1_ref, g2_ref, wg_ref, wu_ref, wd_ref, sg_ref, su_ref, sd_ref,
                ln2g_ref, ln2b_ref, o_ref, acc_ref, *, alpha):
    e = pl.program_id(2)
    h = h_ref[...]

    @pl.when(e == 0)
    def _():
        act = _silu(_dot(h, sg_ref[...])) * _dot(h, su_ref[...])
        acc_ref[...] = _dot(act.astype(BF16), sd_ref[...])

    gates = gates_ref[...]
    lane = lax.broadcasted_iota(jnp.int32, gates.shape, 1)
    gcol = jnp.sum(jnp.where(lane == e, gates, 0.0), axis=-1, keepdims=True)
    act = _silu(_dot(h, wg_ref[...])) * _dot(h, wu_ref[...]) * gcol
    acc_ref[...] += _dot(act.astype(BF16), wd_ref[...])

    @pl.when(e == pl.num_programs(2) - 1)
    def _():
        o_ref[...] = (_normalize(alpha * x1_ref[...] + g2_ref[...] * acc_ref[...]) * ln2g_ref[...]
                      + ln2b_ref[...])


def _moe(h2, gates, x1, g2, wg, wu, wd, sg, su, sd, ln2_g, ln2_b, alpha):
    b, s, d = x1.shape
    ne, _, f = wg.shape
    fs = sg.shape[1]
    tm = min(1024, s)
    tok = lambda bi, i, e: (bi, i, 0)
    vec = pl.BlockSpec((1, d), lambda bi, i, e: (0, 0))
    return pl.pallas_call(
        functools.partial(_moe_kernel, alpha=alpha),
        grid=(b, s // tm, ne),
        in_specs=[pl.BlockSpec((None, tm, d), tok),
                  pl.BlockSpec((None, tm, ne), tok),
                  pl.BlockSpec((None, tm, d), tok),
                  pl.BlockSpec((None, 1, d), lambda bi, i, e: (bi, 0, 0)),
                  pl.BlockSpec((None, d, f), lambda bi, i, e: (e, 0, 0)),
                  pl.BlockSpec((None, d, f), lambda bi, i, e: (e, 0, 0)),
                  pl.BlockSpec((None, f, d), lambda bi, i, e: (e, 0, 0)),
                  pl.BlockSpec((d, fs), lambda bi, i, e: (0, 0)),
                  pl.BlockSpec((d, fs), lambda bi, i, e: (0, 0)),
                  pl.BlockSpec((fs, d), lambda bi, i, e: (0, 0)),
                  vec, vec],
        out_specs=pl.BlockSpec((None, tm, d), tok),
        out_shape=jax.ShapeDtypeStruct((b, s, d), F32),
        scratch_shapes=[pltpu.VMEM((tm, d), F32)],
        compiler_params=_cparams(("arbitrary", "arbitrary", "arbitrary")),
        name="moe",
    )(h2, gates, x1, g2, wg, wu, wd, sg, su, sd, ln2_g.reshape(1, d), ln2_b.reshape(1, d))


def kernel(x, c, ctx, c_ctx, w_ada, b_ada, w_in, hg_lb_fwd, hg_lb_bwd, hg_norm_g, na_rpb, w_branch_a, w_branch_b, w_out, ln1_g, ln1_b, w_router, router_bias, w_e_gate, w_e_up, w_e_down, w_sh_gate, w_sh_up, w_sh_down, ln2_g, ln2_b):
    depth = w_ada.shape[0]
    assert depth == 1, "single-layer block"
    b, s, d = x.shape
    alpha = (2.0 * depth) ** 0.25
    l = 0
    lb_fwd = jnp.cumsum(jax.nn.softmax(hg_lb_fwd.astype(F32), axis=0), axis=0)[l]
    lb_bwd = jnp.cumsum(jax.nn.softmax(hg_lb_bwd.astype(F32), axis=0), axis=0)[l]

    cond_rows = jnp.concatenate([c, c_ctx[None, :], jnp.zeros((8 - b - 1, d), F32)], axis=0)
    mod = _ada(cond_rows, w_ada[l], b_ada[l])
    sh1, sc1, g1, sh2, sc2, g2 = [m[:b, None, :] for m in jnp.split(mod, 6, axis=-1)]
    csh1, csc1 = [jnp.broadcast_to(m[b:b + 1, None, :], (b, 1, d)) for m in jnp.split(mod, 6, axis=-1)[:2]]

    w_in_b = w_in[l].astype(BF16)
    p = _inproj(x, sh1, sc1, w_in_b)
    pc = _inproj(ctx, csh1, csc1, w_in_b)

    o_f, o_b = _hgrn(p, pc, lb_fwd, lb_bwd)
    cos, sin, bias = _na_tables(na_rpb[l], s)
    y_na = _natten(p, pc, cos, sin, bias)

    x1, h2, gates = _merge(o_f, o_b, p, y_na, x, g1, sh2, sc2, hg_norm_g[l], ln1_g[l], ln1_b[l],
                           w_branch_a[l].astype(BF16), w_branch_b[l].astype(BF16), w_out[l].astype(BF16),
                           w_router[l].T, router_bias[l], alpha)

    return _moe(h2, gates, x1, g2, w_e_gate[l].astype(BF16), w_e_up[l].astype(BF16), w_e_down[l].astype(BF16),
                w_sh_gate[l].astype(BF16), w_sh_up[l].astype(BF16), w_sh_down[l].astype(BF16),
                ln2_g[l], ln2_b[l], alpha)
```

```python
import functools

import numpy as np
import jax
import jax.numpy as jnp
from jax import lax
from jax.experimental import pallas as pl
from jax.experimental.pallas import tpu as pltpu

F32 = jnp.float32
BF16 = jnp.bfloat16

D_MODEL = 1024
GRID_W = 64
HG_HEADS = 8
HG_DK = 128
HG_CHUNK = 64
NA_HEADS = 16
NA_HD = 64
NA_WIN_R = 8
NA_WIN_C = 16
ROPE_THETA = 10000.0
NEG_INF = -1e30
N_EXPERTS = 64
EXPERT_DIM = 256
TOP_K = 8
N_GROUPS = 8
TOPK_GROUPS = 4
ROUTED_SCALE = 2.5
LN_EPS = 1e-6
N_SECTIONS = 10
SEC_Q, SEC_FF, SEC_FB, SEC_I, SEC_OG, SEC_NQ, SEC_NK, SEC_NV, SEC_GA, SEC_GB = range(10)

VMEM_LIMIT = 56 * 1024 * 1024


def _cparams(sem):
    return pltpu.CompilerParams(dimension_semantics=sem, vmem_limit_bytes=VMEM_LIMIT)


def _normalize(x):
    mu = jnp.mean(x, axis=-1, keepdims=True)
    xc = x - mu
    var = jnp.mean(xc * xc, axis=-1, keepdims=True)
    return xc * lax.rsqrt(var + LN_EPS)


def _silu(x):
    return x * jax.nn.sigmoid(x)


def _dot(a, b):
    return jnp.dot(a, b, preferred_element_type=F32)


def _dot_nt(a, b):
    return lax.dot_general(a, b, (((1,), (1,)), ((), ())), preferred_element_type=F32)


def _dot_tn(a, b):
    return lax.dot_general(a, b, (((0,), (0,)), ((), ())), preferred_element_type=F32)


def _split3(x):
    hi = x.astype(BF16)
    r1 = x - hi.astype(F32)
    mid = r1.astype(BF16)
    lo = (r1 - mid.astype(F32)).astype(BF16)
    return hi, mid, lo


def _ada_kernel(c_ref, w_ref, b_ref, o_ref):
    cond = _silu(c_ref[...])
    o_ref[...] = _dot(cond.astype(BF16), w_ref[...].astype(BF16)) + b_ref[...]


def _ada(cond_rows, w_ada, b_ada):
    r, d = cond_rows.shape
    n = w_ada.shape[1]
    tn = 1024
    return pl.pallas_call(
        _ada_kernel,
        grid=(n // tn,),
        in_specs=[pl.BlockSpec((r, d), lambda j: (0, 0)),
                  pl.BlockSpec((d, tn), lambda j: (0, j)),
                  pl.BlockSpec((1, tn), lambda j: (0, j))],
        out_specs=pl.BlockSpec((r, tn), lambda j: (0, j)),
        out_shape=jax.ShapeDtypeStruct((r, n), F32),
        compiler_params=_cparams(("arbitrary",)),
        name="ada",
    )(cond_rows, w_ada, b_ada.reshape(1, n))


def _inproj_kernel(x_ref, sh_ref, sc_ref, w_ref, o_ref, h_ref):
    @pl.when(pl.program_id(2) == 0)
    def _():
        h = _normalize(x_ref[...]) * (1.0 + sc_ref[...]) + sh_ref[...]
        h_ref[...] = h.astype(BF16)

    o_ref[...] = _dot(h_ref[...], w_ref[...])


def _inproj(x, shift, scale, w_in_bf16):
    b, s, d = x.shape
    tm = min(1024, s)
    nj = w_in_bf16.shape[1] // d
    return pl.pallas_call(
        _inproj_kernel,
        grid=(b, s // tm, nj),
        in_specs=[pl.BlockSpec((None, tm, d), lambda bi, i, j: (bi, i, 0)),
                  pl.BlockSpec((None, 1, d), lambda bi, i, j: (bi, 0, 0)),
                  pl.BlockSpec((None, 1, d), lambda bi, i, j: (bi, 0, 0)),
                  pl.BlockSpec((d, d), lambda bi, i, j: (0, j))],
        out_specs=pl.BlockSpec((None, None, tm, d), lambda bi, i, j: (j, bi, i, 0)),
        out_shape=jax.ShapeDtypeStruct((nj, b, s, d), F32),
        scratch_shapes=[pltpu.VMEM((tm, d), BF16)],
        compiler_params=_cparams(("arbitrary", "arbitrary", "arbitrary")),
        name="inproj",
    )(x, shift, scale, w_in_bf16)


def _hgrn_chunk(q, fraw, v, lb, st_ref, d, tri_bf16, keep, last_row):
    f = lb + (1.0 - lb) * jax.nn.sigmoid(fraw)
    k = 1.0 - f
    lf = jnp.log(f)
    hi, mid, lo = _split3(lf)
    a = _dot(tri_bf16, hi) + _dot(tri_bf16, mid) + _dot(tri_bf16, lo)
    a_last = a[last_row:last_row + 1, :]
    kd = (k * jnp.exp(a_last - a)).astype(BF16)
    decay = jnp.exp(a_last)
    vb = v.astype(BF16)
    if q is not None:
        qa = (_silu(q) * jnp.exp(a)).astype(BF16)
        kb = (k * jnp.exp(-a)).astype(BF16)
    outs = []
    for h in range(HG_HEADS):
        sl = slice(h * HG_DK, (h + 1) * HG_DK)
        st = st_ref[d, h]
        if q is not None:
            sc = jnp.where(keep, _dot_nt(qa[:, sl], kb[:, sl]), 0.0)
            outs.append(_dot(sc.astype(BF16), vb[:, sl]) + _dot_nt(qa[:, sl], st.astype(BF16)))
        st_ref[d, h] = st * decay[:, sl] + _dot_tn(vb[:, sl], kd[:, sl])
    if q is None:
        return None
    return jnp.concatenate(outs, axis=-1)


def _hgrn_kernel(qf_ref, ff_ref, if_ref, qb_ref, fb_ref, ib_ref, cff_ref, cfb_ref, ci_ref,
                 lbf_ref, lbb_ref, of_ref, ob_ref, st_ref, *, n_sub, n_ctx_sub):
    n = pl.program_id(1)
    c = HG_CHUNK
    row = lax.broadcasted_iota(jnp.int32, (c, c), 0)
    col = lax.broadcasted_iota(jnp.int32, (c, c), 1)
    keep_f = col <= row
    keep_b = col >= row
    tri_f = keep_f.astype(F32).astype(BF16)
    tri_b = keep_b.astype(F32).astype(BF16)
    lbf = lbf_ref[...]
    lbb = lbb_ref[...]

    @pl.when(n == 0)
    def _():
        st_ref[...] = jnp.zeros_like(st_ref)

        def body(i, carry):
            r0 = pl.multiple_of(i * c, c)
            _hgrn_chunk(None, cff_ref[pl.ds(r0, c), :], ci_ref[pl.ds(r0, c), :], lbf, st_ref, 0,
                        tri_f, keep_f, c - 1)
            r1 = pl.multiple_of((n_ctx_sub - 1 - i) * c, c)
            _hgrn_chunk(None, cfb_ref[pl.ds(r1, c), :], ci_ref[pl.ds(r1, c), :], lbb, st_ref, 1,
                        tri_b, keep_b, 0)
            return carry

        lax.fori_loop(0, n_ctx_sub, body, 0)

    @pl.when(n > 0)
    def _():
        def body(i, carry):
            r0 = pl.multiple_of(i * c, c)
            of_ref[pl.ds(r0, c), :] = _hgrn_chunk(
                qf_ref[pl.ds(r0, c), :], ff_ref[pl.ds(r0, c), :], if_ref[pl.ds(r0, c), :], lbf, st_ref, 0,
                tri_f, keep_f, c - 1)
            r1 = pl.multiple_of((n_sub - 1 - i) * c, c)
            ob_ref[pl.ds(r1, c), :] = _hgrn_chunk(
                qb_ref[pl.ds(r1, c), :], fb_ref[pl.ds(r1, c), :], ib_ref[pl.ds(r1, c), :], lbb, st_ref, 1,
                tri_b, keep_b, 0)
            return carry

        lax.fori_loop(0, n_sub, body, 0)


def _hgrn(p, pc, lb_fwd, lb_bwd):
    _, b, s, w = p.shape
    ctx_len = pc.shape[2]
    tb = min(256, s)
    nb = s // tb
    fwd = lambda bi, n: jnp.maximum(n - 1, 0)
    bwd = lambda bi, n: nb - 1 - jnp.maximum(n - 1, 0)

    def sec(section, blk):
        return pl.BlockSpec((None, None, tb, w), lambda bi, n: (section, bi, blk(bi, n), 0))

    def csec(section):
        return pl.BlockSpec((None, None, ctx_len, w), lambda bi, n: (section, bi, 0, 0))

    vec = pl.BlockSpec((1, w), lambda bi, n: (0, 0))
    kern = functools.partial(_hgrn_kernel, n_sub=tb // HG_CHUNK, n_ctx_sub=ctx_len // HG_CHUNK)
    return pl.pallas_call(
        kern,
        grid=(b, nb + 1),
        in_specs=[sec(SEC_Q, fwd), sec(SEC_FF, fwd), sec(SEC_I, fwd),
                  sec(SEC_Q, bwd), sec(SEC_FB, bwd), sec(SEC_I, bwd),
                  csec(SEC_FF), csec(SEC_FB), csec(SEC_I), vec, vec],
        out_specs=[pl.BlockSpec((None, tb, w), lambda bi, n: (bi, fwd(bi, n), 0)),
                   pl.BlockSpec((None, tb, w), lambda bi, n: (bi, bwd(bi, n), 0))],
        out_shape=[jax.ShapeDtypeStruct((b, s, w), F32), jax.ShapeDtypeStruct((b, s, w), F32)],
        scratch_shapes=[pltpu.VMEM((2, HG_HEADS, HG_DK, HG_DK), F32)],
        compiler_params=_cparams(("arbitrary", "arbitrary")),
        name="hgrn",
    )(p, p, p, p, p, p, pc, pc, pc, lb_fwd.reshape(1, w), lb_bwd.reshape(1, w))


NA_ROWS_PER_STEP = 8
NA_PREP_ROWS = 512
NA_KEY_TILE = 128
NA_SPAN = (NA_WIN_R + 2) * GRID_W


def _rope(t, cos, sin_signed, first_half):
    w = t.shape[-1]
    partner = jnp.where(first_half, pltpu.roll(t, w - 16, 1), pltpu.roll(t, 16, 1))
    return t * cos + partner * sin_signed


def _fold_lanes(op, *arrays):
    tiles = [a[:, c:c + 128] for a in arrays for c in range(0, a.shape[-1], 128)]
    acc = tiles[0]
    for t in tiles[1:]:
        acc = op(acc, t)
    return acc


def _natten_kernel(q_ref, k_ref, v_ref, kc_ref, vc_ref, cos_ref, sin_ref, t2_ref, o_ref,
                   kt_s, v_s, kc_s, vc_s, bias_s, *, rows):
    rblk = pl.program_id(2)
    hd = NA_HD
    lane = lax.broadcasted_iota(jnp.int32, (1, 2 * hd), 1)
    first_half = (lane % 32) < 16
    scale = NA_HD ** -0.5

    @pl.when(rblk == 0)
    def _():
        kc = kc_ref[...].astype(BF16)
        vc = vc_ref[...].astype(BF16)
        qi = lax.broadcasted_iota(jnp.int32, (GRID_W, GRID_W), 0)
        ki = lax.broadcasted_iota(jnp.int32, (GRID_W, GRID_W), 1)
        cstart = jnp.clip(qi - NA_WIN_C // 2, 0, GRID_W - NA_WIN_C)
        in_win = (ki >= cstart) & (ki < cstart + NA_WIN_C)
        masked = jnp.full((GRID_W, GRID_W), NEG_INF, F32)
        s_len = k_ref.shape[0]
        for h in range(2):
            sl = slice(h * hd, (h + 1) * hd)
            kc_s[h] = kc[:, sl]
            vc_s[h] = vc[:, sl]
            kt_s[h, s_len // NA_KEY_TILE] = jnp.zeros((hd, NA_KEY_TILE), BF16)
            v_s[h, s_len:s_len + NA_KEY_TILE, :] = jnp.zeros((NA_KEY_TILE, hd), BF16)
            tiles = [jnp.where(in_win, t2_ref[h, dr], NEG_INF) for dr in range(2 * NA_WIN_R - 1)]
            for bidx in range(NA_WIN_R + 1):
                v, par = (bidx, 0) if bidx < NA_WIN_R else (NA_WIN_R // 2, 1)
                for piece in range(NA_SPAN // GRID_W):
                    j = piece - par
                    tile = tiles[NA_WIN_R - 1 - v + j] if 0 <= j < NA_WIN_R else masked
                    bias_s[h, bidx, :, piece * GRID_W:(piece + 1) * GRID_W] = tile

        def prep(i, carry):
            r0 = pl.multiple_of(i * NA_PREP_ROWS, NA_PREP_ROWS)
            rws = pl.ds(r0, NA_PREP_ROWS)
            kr = _rope(k_ref[rws, :], cos_ref[rws, :], sin_ref[rws, :], first_half)
            krt = kr.T.astype(BF16)
            vv = v_ref[rws, :].astype(BF16)
            for h in range(2):
                sl = slice(h * hd, (h + 1) * hd)
                for c in range(NA_PREP_ROWS // NA_KEY_TILE):
                    kt_s[h, i * (NA_PREP_ROWS // NA_KEY_TILE) + c] = krt[sl, c * NA_KEY_TILE:(c + 1) * NA_KEY_TILE]
                v_s[h, rws, :] = vv[:, sl]
            return carry

        lax.fori_loop(0, s_len // NA_PREP_ROWS, prep, 0)

    tq = NA_ROWS_PER_STEP * GRID_W
    t0 = pl.multiple_of(rblk * tq, tq)
    q = q_ref[...] * scale
    qr = _rope(q, cos_ref[pl.ds(t0, tq), :], sin_ref[pl.ds(t0, tq), :], first_half)
    qb = q.astype(BF16)
    qrb = qr.astype(BF16)
    rws = [slice(rr * GRID_W, (rr + 1) * GRID_W) for rr in range(NA_ROWS_PER_STEP)]
    tile0, bidx = [], []
    for rr in range(NA_ROWS_PER_STEP):
        r = rblk * NA_ROWS_PER_STEP + rr
        rs = jnp.clip(r - NA_WIN_R // 2, 0, rows - NA_WIN_R)
        tile0.append(lax.shift_right_logical(rs, 1))
        bidx.append(jnp.where((rs & 1) == 1, NA_WIN_R, r - rs))

    def scores(h):
        sl = slice(h * hd, (h + 1) * hd)
        qrb_h = qrb[:, sl]
        s_ctx_all = _dot_nt(qb[:, sl], kc_s[h])
        s_win = []
        for rr in range(NA_ROWS_PER_STEP):
            kt = kt_s[h, pl.ds(tile0[rr], NA_SPAN // NA_KEY_TILE)]
            kt = jnp.concatenate([kt[c] for c in range(NA_SPAN // NA_KEY_TILE)], axis=-1)
            s_win.append(_dot(qrb_h[rws[rr]], kt))
        return s_win, s_ctx_all

    def softmax(h, s_win, s_ctx_all):
        e_win, e_ctx, rinv = [], [], []
        for rr in range(NA_ROWS_PER_STEP):
            sw = s_win[rr] + bias_s[h, bidx[rr]]
            sc = s_ctx_all[rws[rr]]
            m = jnp.max(_fold_lanes(jnp.maximum, sw, sc), axis=-1, keepdims=True)
            ew = jnp.exp(sw - m)
            ec = jnp.exp(sc - m)
            rinv.append(1.0 / jnp.sum(_fold_lanes(jnp.add, ew, ec), axis=-1, keepdims=True))
            e_win.append(ew.astype(BF16))
            e_ctx.append(ec.astype(BF16))
        return e_win, e_ctx, rinv

    def values(h, e_win, e_ctx, rinv):
        o_win = []
        for rr in range(NA_ROWS_PER_STEP):
            k0 = pl.multiple_of(tile0[rr] * NA_KEY_TILE, NA_KEY_TILE)
            o_win.append(_dot(e_win[rr], v_s[h, pl.ds(k0, NA_SPAN), :]))
        o_ctx = _dot(jnp.concatenate(e_ctx, axis=0), vc_s[h])
        return (jnp.concatenate(o_win, axis=0) + o_ctx) * jnp.concatenate(rinv, axis=0)

    s0 = scores(0)
    s1 = scores(1)
    p0 = softmax(0, *s0)
    o0 = values(0, *p0)
    p1 = softmax(1, *s1)
    o1 = values(1, *p1)
    o_ref[...] = jnp.concatenate([o0, o1], axis=-1)


def _na_tables(rpb, s):
    half = NA_HD // 2
    pos = jnp.arange(s)
    rowp = (pos // GRID_W).astype(F32)
    colp = (pos % GRID_W).astype(F32)
    inv = jnp.power(ROPE_THETA, -jnp.arange(0, half, 2, dtype=F32) / half)
    ang_r = rowp[:, None] * inv[None, :]
    ang_c = colp[:, None] * inv[None, :]
    cos = jnp.concatenate([jnp.cos(ang_r), jnp.cos(ang_r), jnp.cos(ang_c), jnp.cos(ang_c)], axis=-1)
    sin = jnp.concatenate([-jnp.sin(ang_r), jnp.sin(ang_r), -jnp.sin(ang_c), jnp.sin(ang_c)], axis=-1)
    cos = jnp.concatenate([cos, cos], axis=-1)
    sin = jnp.concatenate([sin, sin], axis=-1)

    pad = GRID_W - NA_WIN_C
    rp = jnp.pad(rpb.astype(F32), ((0, 0), (0, 0), (pad, pad)), mode="edge")
    t2 = jnp.stack([rp[:, :, GRID_W - 1 - qc:2 * GRID_W - 1 - qc] for qc in range(GRID_W)], axis=2)
    return cos, sin, t2


def _natten(p, pc, cos, sin, t2):
    _, b, s, w = p.shape
    ctx_len = pc.shape[2]
    rows = s // GRID_W
    assert rows >= NA_WIN_R and rows % NA_ROWS_PER_STEP == 0
    tq = NA_ROWS_PER_STEP * GRID_W
    hw = 2 * NA_HD
    nhp = w // hw
    kern = functools.partial(_natten_kernel, rows=rows)
    return pl.pallas_call(
        kern,
        grid=(b, nhp, rows // NA_ROWS_PER_STEP),
        in_specs=[pl.BlockSpec((None, None, tq, hw), lambda bi, hp, r: (SEC_NQ, bi, r, hp)),
                  pl.BlockSpec((None, None, s, hw), lambda bi, hp, r: (SEC_NK, bi, 0, hp)),
                  pl.BlockSpec((None, None, s, hw), lambda bi, hp, r: (SEC_NV, bi, 0, hp)),
                  pl.BlockSpec((None, None, ctx_len, hw), lambda bi, hp, r: (SEC_NK, bi, 0, hp)),
                  pl.BlockSpec((None, None, ctx_len, hw), lambda bi, hp, r: (SEC_NV, bi, 0, hp)),
                  pl.BlockSpec((s, hw), lambda bi, hp, r: (0, 0)),
                  pl.BlockSpec((s, hw), lambda bi, hp, r: (0, 0)),
                  pl.BlockSpec((2, 2 * NA_WIN_R - 1, GRID_W, GRID_W), lambda bi, hp, r: (hp, 0, 0, 0))],
        out_specs=pl.BlockSpec((None, tq, hw), lambda bi, hp, r: (bi, r, hp)),
        out_shape=jax.ShapeDtypeStruct((b, s, w), F32),
        scratch_shapes=[pltpu.VMEM((2, s // NA_KEY_TILE + 1, NA_HD, NA_KEY_TILE), BF16),
                        pltpu.VMEM((2, s + NA_KEY_TILE, NA_HD), BF16),
                        pltpu.VMEM((2, ctx_len, NA_HD), BF16), pltpu.VMEM((2, ctx_len, NA_HD), BF16),
                        pltpu.VMEM((2, NA_WIN_R + 1, GRID_W, NA_SPAN), F32)],
        compiler_params=_cparams(("arbitrary", "arbitrary", "arbitrary")),
        name="natten",
    )(p, p, p, pc, pc, cos, sin, t2)


def _route(logits_t, rbias):
    e, t = logits_t.shape
    gsz = e // N_GROUPS
    scores = jax.nn.sigmoid(logits_t)
    sel = scores + rbias
    neg = -jnp.inf
    sub = lax.broadcasted_iota(jnp.int32, (gsz, t), 0).astype(F32)
    gscore = []
    for g in range(N_GROUPS):
        grp = sel[g * gsz:(g + 1) * gsz, :]
        m1 = jnp.max(grp, axis=0, keepdims=True)
        first = jnp.min(jnp.where(grp == m1, sub, float(gsz)), axis=0, keepdims=True)
        m2 = jnp.max(jnp.where(sub == first, neg, grp), axis=0, keepdims=True)
        gscore.append(m1 + m2)
    masked = []
    for g in range(N_GROUPS):
        rank = jnp.zeros((1, t), F32)
        for g2 in range(N_GROUPS):
            if g2 == g:
                continue
            if g2 < g:
                ahead = gscore[g2] >= gscore[g]
            else:
                ahead = gscore[g2] > gscore[g]
            rank = rank + jnp.where(ahead, 1.0, 0.0)
        masked.append(jnp.where(rank < TOPK_GROUPS, sel[g * gsz:(g + 1) * gsz, :], neg))
    work = jnp.concatenate(masked, axis=0)
    eidx = lax.broadcasted_iota(jnp.int32, (e, t), 0).astype(F32)
    wsel = jnp.zeros((e, t), F32)
    for _ in range(TOP_K):
        m = jnp.max(work, axis=0, keepdims=True)
        first = jnp.min(jnp.where(work == m, eidx, float(e)), axis=0, keepdims=True)
        pick = eidx == first
        wsel = jnp.where(pick, scores, wsel)
        work = jnp.where(pick, neg, work)
    return wsel / jnp.sum(wsel, axis=0, keepdims=True) * ROUTED_SCALE


def _merge_kernel(of_ref, ob_ref, og_ref, yna_ref, ga_ref, gb_ref, x_ref, g1_ref, sh2_ref, sc2_ref,
                  hgg_ref, ln1g_ref, ln1b_ref, wa_ref, wb_ref, wo_ref, wr_ref, rb_ref,
                  x1_ref, h2_ref, gates_ref, *, alpha):
    o = of_ref[...] + ob_ref[...]
    parts = []
    for h in range(HG_HEADS):
        oh = o[:, h * HG_DK:(h + 1) * HG_DK]
        parts.append(oh * lax.rsqrt(jnp.mean(oh * oh, axis=-1, keepdims=True) + LN_EPS))
    y_hg = jnp.concatenate(parts, axis=-1) * hgg_ref[...] * _silu(og_ref[...])
    t = (jax.nn.sigmoid(ga_ref[...]) * _dot(y_hg.astype(BF16), wa_ref[...])
         + jax.nn.sigmoid(gb_ref[...]) * _dot(yna_ref[...].astype(BF16), wb_ref[...]))
    y = _dot(t.astype(BF16), wo_ref[...])
    x1 = _normalize(alpha * x_ref[...] + g1_ref[...] * y) * ln1g_ref[...] + ln1b_ref[...]
    x1_ref[...] = x1
    h2 = _normalize(x1) * (1.0 + sc2_ref[...]) + sh2_ref[...]
    h2_ref[...] = h2.astype(BF16)
    hh, hm, hl = _split3(h2)
    wh, wm, wl = _split3(wr_ref[...])
    logits_t = (_dot_nt(wh, hh) + _dot_nt(wh, hm) + _dot_nt(wm, hh)
                + _dot_nt(wh, hl) + _dot_nt(wl, hh) + _dot_nt(wm, hm))
    gates_ref[...] = _route(logits_t, rb_ref[...]).T


def _merge(o_f, o_b, p, y_na, x, g1, sh2, sc2, hg_norm_g, ln1_g, ln1_b, w_a, w_b, w_o, w_router_t, router_bias,
           alpha):
    b, s, d = x.shape
    tm = min(256, s)
    e = w_router_t.shape[0]
    tok = lambda bi, i: (bi, i, 0)
    blk = pl.BlockSpec((None, tm, d), tok)

    def sec(section):
        return pl.BlockSpec((None, None, tm, d), lambda bi, i: (section, bi, i, 0))

    mod = pl.BlockSpec((None, 1, d), lambda bi, i: (bi, 0, 0))
    vec = pl.BlockSpec((1, d), lambda bi, i: (0, 0))
    mat = pl.BlockSpec((d, d), lambda bi, i: (0, 0))
    return pl.pallas_call(
        functools.partial(_merge_kernel, alpha=alpha),
        grid=(b, s // tm),
        in_specs=[blk, blk, sec(SEC_OG), blk, sec(SEC_GA), sec(SEC_GB), blk, mod, mod, mod,
                  vec, vec, vec, mat, mat, mat,
                  pl.BlockSpec((e, d), lambda bi, i: (0, 0)),
                  pl.BlockSpec((e, 1), lambda bi, i: (0, 0))],
        out_specs=[blk, blk, pl.BlockSpec((None, tm, e), tok)],
        out_shape=[jax.ShapeDtypeStruct((b, s, d), F32), jax.ShapeDtypeStruct((b, s, d), BF16),
                   jax.ShapeDtypeStruct((b, s, e), F32)],
        compiler_params=_cparams(("arbitrary", "arbitrary")),
        name="merge",
    )(o_f, o_b, p, y_na, p, p, x, g1, sh2, sc2, hg_norm_g.reshape(1, d), ln1_g.reshape(1, d),
      ln1_b.reshape(1, d), w_a, w_b, w_o, w_router_t, router_bias.reshape(e, 1))


def _moe_kernel(h_ref, gates_ref, x1_ref, g2_ref, wg_ref, wu_ref, wd_ref, sg_ref, su_ref, sd_ref,
                ln2g_ref, ln2b_ref, o_ref, acc_ref, *, alpha):
    e = pl.program_id(2)
    h = h_ref[...]

    @pl.when(e == 0)
    def _():
        act = _silu(_dot(h, sg_ref[...])) * _dot(h, su_ref[...])
        acc_ref[...] = _dot(act.astype(BF16), sd_ref[...])

    gates = gates_ref[...]
    lane = lax.broadcasted_iota(jnp.int32, gates.shape, 1)
    gcol = jnp.sum(jnp.where(lane == e, gates, 0.0), axis=-1, keepdims=True)
    act = _silu(_dot(h, wg_ref[...])) * _dot(h, wu_ref[...]) * gcol
    acc_ref[...] += _dot(act.astype(BF16), wd_ref[...])

    @pl.when(e == pl.num_programs(2) - 1)
    def _():
        o_ref[...] = (_normalize(alpha * x1_ref[...] + g2_ref[...] * acc_ref[...]) * ln2g_ref[...]
                      + ln2b_ref[...])


def _moe(h2, gates, x1, g2, wg, wu, wd, sg, su, sd, ln2_g, ln2_b, alpha):
    b, s, d = x1.shape
    ne, _, f = wg.shape
    fs = sg.shape[1]
    tm = min(1024, s)
    tok = lambda bi, i, e: (bi, i, 0)
    vec = pl.BlockSpec((1, d), lambda bi, i, e: (0, 0))
    return pl.pallas_call(
        functools.partial(_moe_kernel, alpha=alpha),
        grid=(b, s // tm, ne),
        in_specs=[pl.BlockSpec((None, tm, d), tok),
                  pl.BlockSpec((None, tm, ne), tok),
                  pl.BlockSpec((None, tm, d), tok),
                  pl.BlockSpec((None, 1, d), lambda bi, i, e: (bi, 0, 0)),
                  pl.BlockSpec((None, d, f), lambda bi, i, e: (e, 0, 0)),
                  pl.BlockSpec((None, d, f), lambda bi, i, e: (e, 0, 0)),
                  pl.BlockSpec((None, f, d), lambda bi, i, e: (e, 0, 0)),
                  pl.BlockSpec((d, fs), lambda bi, i, e: (0, 0)),
                  pl.BlockSpec((d, fs), lambda bi, i, e: (0, 0)),
                  pl.BlockSpec((fs, d), lambda bi, i, e: (0, 0)),
                  vec, vec],
        out_specs=pl.BlockSpec((None, tm, d), tok),
        out_shape=jax.ShapeDtypeStruct((b, s, d), F32),
        scratch_shapes=[pltpu.VMEM((tm, d), F32)],
        compiler_params=_cparams(("arbitrary", "arbitrary", "arbitrary")),
        name="moe",
    )(h2, gates, x1, g2, wg, wu, wd, sg, su, sd, ln2_g.reshape(1, d), ln2_b.reshape(1, d))


def kernel(x, c, ctx, c_ctx, w_ada, b_ada, w_in, hg_lb_fwd, hg_lb_bwd, hg_norm_g, na_rpb, w_branch_a, w_branch_b, w_out, ln1_g, ln1_b, w_router, router_bias, w_e_gate, w_e_up, w_e_down, w_sh_gate, w_sh_up, w_sh_down, ln2_g, ln2_b):
    depth = w_ada.shape[0]
    assert depth == 1, "single-layer block"
    b, s, d = x.shape
    alpha = (2.0 * depth) ** 0.25
    l = 0
    lb_fwd = jnp.cumsum(jax.nn.softmax(hg_lb_fwd.astype(F32), axis=0), axis=0)[l]
    lb_bwd = jnp.cumsum(jax.nn.softmax(hg_lb_bwd.astype(F32), axis=0), axis=0)[l]

    cond_rows = jnp.concatenate([c, c_ctx[None, :], jnp.zeros((8 - b - 1, d), F32)], axis=0)
    mod = _ada(cond_rows, w_ada[l], b_ada[l])
    sh1, sc1, g1, sh2, sc2, g2 = [m[:b, None, :] for m in jnp.split(mod, 6, axis=-1)]
    csh1, csc1 = [jnp.broadcast_to(m[b:b + 1, None, :], (b, 1, d)) for m in jnp.split(mod, 6, axis=-1)[:2]]

    w_in_b = w_in[l].astype(BF16)
    p = _inproj(x, sh1, sc1, w_in_b)
    pc = _inproj(ctx, csh1, csc1, w_in_b)

    o_f, o_b = _hgrn(p, pc, lb_fwd, lb_bwd)
    cos, sin, bias = _na_tables(na_rpb[l], s)
    y_na = _natten(p, pc, cos, sin, bias)

    x1, h2, gates = _merge(o_f, o_b, p, y_na, x, g1, sh2, sc2, hg_norm_g[l], ln1_g[l], ln1_b[l],
                           w_branch_a[l].astype(BF16), w_branch_b[l].astype(BF16), w_out[l].astype(BF16),
                           w_router[l].T, router_bias[l], alpha)

    return _moe(h2, gates, x1, g2, w_e_gate[l].astype(BF16), w_e_up[l].astype(BF16), w_e_down[l].astype(BF16),
                w_sh_gate[l].astype(BF16), w_sh_up[l].astype(BF16), w_sh_down[l].astype(BF16),
                ln2_g[l], ln2_b[l], alpha)
```

```python
import functools

import numpy as np
import jax
import jax.numpy as jnp
from jax import lax
from jax.experimental import pallas as pl
from jax.experimental.pallas import tpu as pltpu

F32 = jnp.float32
BF16 = jnp.bfloat16

D_MODEL = 1024
GRID_W = 64
HG_HEADS = 8
HG_DK = 128
HG_CHUNK = 64
NA_HEADS = 16
NA_HD = 64
NA_WIN_R = 8
NA_WIN_C = 16
ROPE_THETA = 10000.0
NEG_INF = -1e30
N_EXPERTS = 64
EXPERT_DIM = 256
TOP_K = 8
N_GROUPS = 8
TOPK_GROUPS = 4
ROUTED_SCALE = 2.5
LN_EPS = 1e-6
N_SECTIONS = 10
SEC_Q, SEC_FF, SEC_FB, SEC_I, SEC_OG, SEC_NQ, SEC_NK, SEC_NV, SEC_GA, SEC_GB = range(10)

VMEM_LIMIT = 56 * 1024 * 1024


def _cparams(sem):
    return pltpu.CompilerParams(dimension_semantics=sem, vmem_limit_bytes=VMEM_LIMIT)


def _normalize(x):
    mu = jnp.mean(x, axis=-1, keepdims=True)
    xc = x - mu
    var = jnp.mean(xc * xc, axis=-1, keepdims=True)
    return xc * lax.rsqrt(var + LN_EPS)


def _silu(x):
    return x * jax.nn.sigmoid(x)


def _dot(a, b):
    return jnp.dot(a, b, preferred_element_type=F32)


def _dot_nt(a, b):
    return lax.dot_general(a, b, (((1,), (1,)), ((), ())), preferred_element_type=F32)


def _dot_tn(a, b):
    return lax.dot_general(a, b, (((0,), (0,)), ((), ())), preferred_element_type=F32)


def _split3(x):
    hi = x.astype(BF16)
    r1 = x - hi.astype(F32)
    mid = r1.astype(BF16)
    lo = (r1 - mid.astype(F32)).astype(BF16)
    return hi, mid, lo


def _ada_kernel(c_ref, w_ref, b_ref, o_ref):
    cond = _silu(c_ref[...])
    o_ref[...] = _dot(cond.astype(BF16), w_ref[...].astype(BF16)) + b_ref[...]


def _ada(cond_rows, w_ada, b_ada):
    r, d = cond_rows.shape
    n = w_ada.shape[1]
    tn = 1024
    return pl.pallas_call(
        _ada_kernel,
        grid=(n // tn,),
        in_specs=[pl.BlockSpec((r, d), lambda j: (0, 0)),
                  pl.BlockSpec((d, tn), lambda j: (0, j)),
                  pl.BlockSpec((1, tn), lambda j: (0, j))],
        out_specs=pl.BlockSpec((r, tn), lambda j: (0, j)),
        out_shape=jax.ShapeDtypeStruct((r, n), F32),
        compiler_params=_cparams(("arbitrary",)),
        name="ada",
    )(cond_rows, w_ada, b_ada.reshape(1, n))


def _inproj_kernel(x_ref, sh_ref, sc_ref, w_ref, o_ref, h_ref):
    @pl.when(pl.program_id(2) == 0)
    def _():
        h = _normalize(x_ref[...]) * (1.0 + sc_ref[...]) + sh_ref[...]
        h_ref[...] = h.astype(BF16)

    o_ref[...] = _dot(h_ref[...], w_ref[...])


def _inproj(x, shift, scale, w_in_bf16):
    b, s, d = x.shape
    tm = min(1024, s)
    nj = w_in_bf16.shape[1] // d
    return pl.pallas_call(
        _inproj_kernel,
        grid=(b, s // tm, nj),
        in_specs=[pl.BlockSpec((None, tm, d), lambda bi, i, j: (bi, i, 0)),
                  pl.BlockSpec((None, 1, d), lambda bi, i, j: (bi, 0, 0)),
                  pl.BlockSpec((None, 1, d), lambda bi, i, j: (bi, 0, 0)),
                  pl.BlockSpec((d, d), lambda bi, i, j: (0, j))],
        out_specs=pl.BlockSpec((None, None, tm, d), lambda bi, i, j: (j, bi, i, 0)),
        out_shape=jax.ShapeDtypeStruct((nj, b, s, d), F32),
        scratch_shapes=[pltpu.VMEM((tm, d), BF16)],
        compiler_params=_cparams(("arbitrary", "arbitrary", "arbitrary")),
        name="inproj",
    )(x, shift, scale, w_in_bf16)


def _hgrn_chunk(q, fraw, v, lb, st_ref, d, tri_bf16, keep, last_row):
    f = lb + (1.0 - lb) * jax.nn.sigmoid(fraw)
    k = 1.0 - f
    lf = jnp.log(f)
    hi, mid, lo = _split3(lf)
    a = _dot(tri_bf16, hi) + _dot(tri_bf16, mid) + _dot(tri_bf16, lo)
    a_last = a[last_row:last_row + 1, :]
    kd = (k * jnp.exp(a_last - a)).astype(BF16)
    decay = jnp.exp(a_last)
    vb = v.astype(BF16)
    if q is not None:
        qa = (_silu(q) * jnp.exp(a)).astype(BF16)
        kb = (k * jnp.exp(-a)).astype(BF16)
    outs = []
    for h in range(HG_HEADS):
        sl = slice(h * HG_DK, (h + 1) * HG_DK)
        st = st_ref[d, h]
        if q is not None:
            sc = jnp.where(keep, _dot_nt(qa[:, sl], kb[:, sl]), 0.0)
            outs.append(_dot(sc.astype(BF16), vb[:, sl]) + _dot_nt(qa[:, sl], st.astype(BF16)))
        st_ref[d, h] = st * decay[:, sl] + _dot_tn(vb[:, sl], kd[:, sl])
    if q is None:
        return None
    return jnp.concatenate(outs, axis=-1)


def _hgrn_kernel(qf_ref, ff_ref, if_ref, qb_ref, fb_ref, ib_ref, cff_ref, cfb_ref, ci_ref,
                 lbf_ref, lbb_ref, of_ref, ob_ref, st_ref, *, n_sub, n_ctx_sub):
    n = pl.program_id(1)
    c = HG_CHUNK
    row = lax.broadcasted_iota(jnp.int32, (c, c), 0)
    col = lax.broadcasted_iota(jnp.int32, (c, c), 1)
    keep_f = col <= row
    keep_b = col >= row
    tri_f = keep_f.astype(F32).astype(BF16)
    tri_b = keep_b.astype(F32).astype(BF16)
    lbf = lbf_ref[...]
    lbb = lbb_ref[...]

    @pl.when(n == 0)
    def _():
        st_ref[...] = jnp.zeros_like(st_ref)

        def body(i, carry):
            r0 = pl.multiple_of(i * c, c)
            _hgrn_chunk(None, cff_ref[pl.ds(r0, c), :], ci_ref[pl.ds(r0, c), :], lbf, st_ref, 0,
                        tri_f, keep_f, c - 1)
            r1 = pl.multiple_of((n_ctx_sub - 1 - i) * c, c)
            _hgrn_chunk(None, cfb_ref[pl.ds(r1, c), :], ci_ref[pl.ds(r1, c), :], lbb, st_ref, 1,
                        tri_b, keep_b, 0)
            return carry

        lax.fori_loop(0, n_ctx_sub, body, 0)

    @pl.when(n > 0)
    def _():
        def body(i, carry):
            r0 = pl.multiple_of(i * c, c)
            of_ref[pl.ds(r0, c), :] = _hgrn_chunk(
                qf_ref[pl.ds(r0, c), :], ff_ref[pl.ds(r0, c), :], if_ref[pl.ds(r0, c), :], lbf, st_ref, 0,
                tri_f, keep_f, c - 1)
            r1 = pl.multiple_of((n_sub - 1 - i) * c, c)
            ob_ref[pl.ds(r1, c), :] = _hgrn_chunk(
                qb_ref[pl.ds(r1, c), :], fb_ref[pl.ds(r1, c), :], ib_ref[pl.ds(r1, c), :], lbb, st_ref, 1,
                tri_b, keep_b, 0)
            return carry

        lax.fori_loop(0, n_sub, body, 0)


def _hgrn(p, pc, lb_fwd, lb_bwd):
    _, b, s, w = p.shape
    ctx_len = pc.shape[2]
    tb = min(256, s)
    nb = s // tb
    fwd = lambda bi, n: jnp.maximum(n - 1, 0)
    bwd = lambda bi, n: nb - 1 - jnp.maximum(n - 1, 0)

    def sec(section, blk):
        return pl.BlockSpec((None, None, tb, w), lambda bi, n: (section, bi, blk(bi, n), 0))

    def csec(section):
        return pl.BlockSpec((None, None, ctx_len, w), lambda bi, n: (section, bi, 0, 0))

    vec = pl.BlockSpec((1, w), lambda bi, n: (0, 0))
    kern = functools.partial(_hgrn_kernel, n_sub=tb // HG_CHUNK, n_ctx_sub=ctx_len // HG_CHUNK)
    return pl.pallas_call(
        kern,
        grid=(b, nb + 1),
        in_specs=[sec(SEC_Q, fwd), sec(SEC_FF, fwd), sec(SEC_I, fwd),
                  sec(SEC_Q, bwd), sec(SEC_FB, bwd), sec(SEC_I, bwd),
                  csec(SEC_FF), csec(SEC_FB), csec(SEC_I), vec, vec],
        out_specs=[pl.BlockSpec((None, tb, w), lambda bi, n: (bi, fwd(bi, n), 0)),
                   pl.BlockSpec((None, tb, w), lambda bi, n: (bi, bwd(bi, n), 0))],
        out_shape=[jax.ShapeDtypeStruct((b, s, w), F32), jax.ShapeDtypeStruct((b, s, w), F32)],
        scratch_shapes=[pltpu.VMEM((2, HG_HEADS, HG_DK, HG_DK), F32)],
        compiler_params=_cparams(("arbitrary", "arbitrary")),
        name="hgrn",
    )(p, p, p, p, p, p, pc, pc, pc, lb_fwd.reshape(1, w), lb_bwd.reshape(1, w))


NA_ROWS_PER_STEP = 8
NA_PREP_ROWS = 512
NA_KEY_TILE = 128
NA_SPAN = (NA_WIN_R + 2) * GRID_W


def _rope(t, cos, sin_signed, first_half):
    w = t.shape[-1]
    partner = jnp.where(first_half, pltpu.roll(t, w - 16, 1), pltpu.roll(t, 16, 1))
    return t * cos + partner * sin_signed


def _fold_lanes(op, *arrays):
    tiles = [a[:, c:c + 128] for a in arrays for c in range(0, a.shape[-1], 128)]
    acc = tiles[0]
    for t in tiles[1:]:
        acc = op(acc, t)
    return acc


def _natten_kernel(q_ref, k_ref, v_ref, kc_ref, vc_ref, cos_ref, sin_ref, t2_ref, o_ref,
                   kt_s, v_s, kc_s, vc_s, bias_s, *, rows):
    rblk = pl.program_id(2)
    hd = NA_HD
    lane = lax.broadcasted_iota(jnp.int32, (1, 2 * hd), 1)
    first_half = (lane % 32) < 16
    scale = NA_HD ** -0.5

    @pl.when(rblk == 0)
    def _():
        kc = kc_ref[...].astype(BF16)
        vc = vc_ref[...].astype(BF16)
        qi = lax.broadcasted_iota(jnp.int32, (GRID_W, GRID_W), 0)
        ki = lax.broadcasted_iota(jnp.int32, (GRID_W, GRID_W), 1)
        cstart = jnp.clip(qi - NA_WIN_C // 2, 0, GRID_W - NA_WIN_C)
        in_win = (ki >= cstart) & (ki < cstart + NA_WIN_C)
        masked = jnp.full((GRID_W, GRID_W), NEG_INF, F32)
        s_len = k_ref.shape[0]
        for h in range(2):
            sl = slice(h * hd, (h + 1) * hd)
            kc_s[h] = kc[:, sl]
            vc_s[h] = vc[:, sl]
            kt_s[h, s_len // NA_KEY_TILE] = jnp.zeros((hd, NA_KEY_TILE), BF16)
            v_s[h, s_len:s_len + NA_KEY_TILE, :] = jnp.zeros((NA_KEY_TILE, hd), BF16)
            tiles = [jnp.where(in_win, t2_ref[h, dr], NEG_INF) for dr in range(2 * NA_WIN_R - 1)]
            for bidx in range(NA_WIN_R + 1):
                v, par = (bidx, 0) if bidx < NA_WIN_R else (NA_WIN_R // 2, 1)
                for piece in range(NA_SPAN // GRID_W):
                    j = piece - par
                    tile = tiles[NA_WIN_R - 1 - v + j] if 0 <= j < NA_WIN_R else masked
                    bias_s[h, bidx, :, piece * GRID_W:(piece + 1) * GRID_W] = tile

        def prep(i, carry):
            r0 = pl.multiple_of(i * NA_PREP_ROWS, NA_PREP_ROWS)
            rws = pl.ds(r0, NA_PREP_ROWS)
            kr = _rope(k_ref[rws, :], cos_ref[rws, :], sin_ref[rws, :], first_half)
            krt = kr.T.astype(BF16)
            vv = v_ref[rws, :].astype(BF16)
            for h in range(2):
                sl = slice(h * hd, (h + 1) * hd)
                for c in range(NA_PREP_ROWS // NA_KEY_TILE):
                    kt_s[h, i * (NA_PREP_ROWS // NA_KEY_TILE) + c] = krt[sl, c * NA_KEY_TILE:(c + 1) * NA_KEY_TILE]
                v_s[h, rws, :] = vv[:, sl]
            return carry

        lax.fori_loop(0, s_len // NA_PREP_ROWS, prep, 0)

    tq = NA_ROWS_PER_STEP * GRID_W
    t0 = pl.multiple_of(rblk * tq, tq)
    q = q_ref[...] * scale
    qr = _rope(q, cos_ref[pl.ds(t0, tq), :], sin_ref[pl.ds(t0, tq), :], first_half)
    qb = q.astype(BF16)
    qrb = qr.astype(BF16)
    rws = [slice(rr * GRID_W, (rr + 1) * GRID_W) for rr in range(NA_ROWS_PER_STEP)]
    tile0, bidx = [], []
    for rr in range(NA_ROWS_PER_STEP):
        r = rblk * NA_ROWS_PER_STEP + rr
        rs = jnp.clip(r - NA_WIN_R // 2, 0, rows - NA_WIN_R)
        tile0.append(lax.shift_right_logical(rs, 1))
        bidx.append(jnp.where((rs & 1) == 1, NA_WIN_R, r - rs))

    def scores(h):
        sl = slice(h * hd, (h + 1) * hd)
        qrb_h = qrb[:, sl]
        s_ctx_all = _dot_nt(qb[:, sl], kc_s[h])
        s_win = []
        for rr in range(NA_ROWS_PER_STEP):
            kt = kt_s[h, pl.ds(tile0[rr], NA_SPAN // NA_KEY_TILE)]
            kt = jnp.concatenate([kt[c] for c in range(NA_SPAN // NA_KEY_TILE)], axis=-1)
            s_win.append(_dot(qrb_h[rws[rr]], kt))
        return s_win, s_ctx_all

    def softmax(h, s_win, s_ctx_all):
        e_win, e_ctx, rinv = [], [], []
        for rr in range(NA_ROWS_PER_STEP):
            sw = s_win[rr] + bias_s[h, bidx[rr]]
            sc = s_ctx_all[rws[rr]]
            m = jnp.max(_fold_lanes(jnp.maximum, sw, sc), axis=-1, keepdims=True)
            ew = jnp.exp(sw - m)
            ec = jnp.exp(sc - m)
            rinv.append(1.0 / jnp.sum(_fold_lanes(jnp.add, ew, ec), axis=-1, keepdims=True))
            e_win.append(ew.astype(BF16))
            e_ctx.append(ec.astype(BF16))
        return e_win, e_ctx, rinv

    def values(h, e_win, e_ctx, rinv):
        o_win = []
        for rr in range(NA_ROWS_PER_STEP):
            k0 = pl.multiple_of(tile0[rr] * NA_KEY_TILE, NA_KEY_TILE)
            o_win.append(_dot(e_win[rr], v_s[h, pl.ds(k0, NA_SPAN), :]))
        o_ctx = _dot(jnp.concatenate(e_ctx, axis=0), vc_s[h])
        return (jnp.concatenate(o_win, axis=0) + o_ctx) * jnp.concatenate(rinv, axis=0)

    s0 = scores(0)
    s1 = scores(1)
    p0 = softmax(0, *s0)
    o0 = values(0, *p0)
    p1 = softmax(1, *s1)
    o1 = values(1, *p1)
    o_ref[...] = jnp.concatenate([o0, o1], axis=-1)


def _na_tables(rpb, s):
    half = NA_HD // 2
    pos = jnp.arange(s)
    rowp = (pos // GRID_W).astype(F32)
    colp = (pos % GRID_W).astype(F32)
    inv = jnp.power(ROPE_THETA, -jnp.arange(0, half, 2, dtype=F32) / half)
    ang_r = rowp[:, None] * inv[None, :]
    ang_c = colp[:, None] * inv[None, :]
    cos = jnp.concatenate([jnp.cos(ang_r), jnp.cos(ang_r), jnp.cos(ang_c), jnp.cos(ang_c)], axis=-1)
    sin = jnp.concatenate([-jnp.sin(ang_r), jnp.sin(ang_r), -jnp.sin(ang_c), jnp.sin(ang_c)], axis=-1)
    cos = jnp.concatenate([cos, cos], axis=-1)
    sin = jnp.concatenate([sin, sin], axis=-1)

    pad = GRID_W - NA_WIN_C
    rp = jnp.pad(rpb.astype(F32), ((0, 0), (0, 0), (pad, pad)), mode="edge")
    t2 = jnp.stack([rp[:, :, GRID_W - 1 - qc:2 * GRID_W - 1 - qc] for qc in range(GRID_W)], axis=2)
    return cos, sin, t2


def _natten(p, pc, cos, sin, t2):
    _, b, s, w = p.shape
    ctx_len = pc.shape[2]
    rows = s // GRID_W
    assert rows >= NA_WIN_R and rows % NA_ROWS_PER_STEP == 0
    tq = NA_ROWS_PER_STEP * GRID_W
    hw = 2 * NA_HD
    nhp = w // hw
    kern = functools.partial(_natten_kernel, rows=rows)
    return pl.pallas_call(
        kern,
        grid=(b, nhp, rows // NA_ROWS_PER_STEP),
        in_specs=[pl.BlockSpec((None, None, tq, hw), lambda bi, hp, r: (SEC_NQ, bi, r, hp)),
                  pl.BlockSpec((None, None, s, hw), lambda bi, hp, r: (SEC_NK, bi, 0, hp)),
                  pl.BlockSpec((None, None, s, hw), lambda bi, hp, r: (SEC_NV, bi, 0, hp)),
                  pl.BlockSpec((None, None, ctx_len, hw), lambda bi, hp, r: (SEC_NK, bi, 0, hp)),
                  pl.BlockSpec((None, None, ctx_len, hw), lambda bi, hp, r: (SEC_NV, bi, 0, hp)),
                  pl.BlockSpec((s, hw), lambda bi, hp, r: (0, 0)),
                  pl.BlockSpec((s, hw), lambda bi, hp, r: (0, 0)),
                  pl.BlockSpec((2, 2 * NA_WIN_R - 1, GRID_W, GRID_W), lambda bi, hp, r: (hp, 0, 0, 0))],
        out_specs=pl.BlockSpec((None, tq, hw), lambda bi, hp, r: (bi, r, hp)),
        out_shape=jax.ShapeDtypeStruct((b, s, w), F32),
        scratch_shapes=[pltpu.VMEM((2, s // NA_KEY_TILE + 1, NA_HD, NA_KEY_TILE), BF16),
                        pltpu.VMEM((2, s + NA_KEY_TILE, NA_HD), BF16),
                        pltpu.VMEM((2, ctx_len, NA_HD), BF16), pltpu.VMEM((2, ctx_len, NA_HD), BF16),
                        pltpu.VMEM((2, NA_WIN_R + 1, GRID_W, NA_SPAN), F32)],
        compiler_params=_cparams(("arbitrary", "arbitrary", "arbitrary")),
        name="natten",
    )(p, p, p, pc, pc, cos, sin, t2)


def _route(logits_t, rbias):
    e, t = logits_t.shape
    gsz = e // N_GROUPS
    scores = jax.nn.sigmoid(logits_t)
    sel = scores + rbias
    neg = -jnp.inf
    sub = lax.broadcasted_iota(jnp.int32, (gsz, t), 0).astype(F32)
    gscore = []
    for g in range(N_GROUPS):
        grp = sel[g * gsz:(g + 1) * gsz, :]
        m1 = jnp.max(grp, axis=0, keepdims=True)
        first = jnp.min(jnp.where(grp == m1, sub, float(gsz)), axis=0, keepdims=True)
        m2 = jnp.max(jnp.where(sub == first, neg, grp), axis=0, keepdims=True)
        gscore.append(m1 + m2)
    masked = []
    for g in range(N_GROUPS):
        rank = jnp.zeros((1, t), F32)
        for g2 in range(N_GROUPS):
            if g2 == g:
                continue
            if g2 < g:
                ahead = gscore[g2] >= gscore[g]
            else:
                ahead = gscore[g2] > gscore[g]
            rank = rank + jnp.where(ahead, 1.0, 0.0)
        masked.append(jnp.where(rank < TOPK_GROUPS, sel[g * gsz:(g + 1) * gsz, :], neg))
    work = jnp.concatenate(masked, axis=0)
    eidx = lax.broadcasted_iota(jnp.int32, (e, t), 0).astype(F32)
    idxs, ws = [], []
    chosen = jnp.zeros((e, t), F32)
    for _ in range(TOP_K):
        m = jnp.max(work, axis=0, keepdims=True)
        first = jnp.min(jnp.where(work == m, eidx, float(e)), axis=0, keepdims=True)
        pick = eidx == first
        idxs.append(first)
        ws.append(jnp.sum(jnp.where(pick, scores, 0.0), axis=0, keepdims=True))
        chosen = jnp.where(pick, 1.0, chosen)
        work = jnp.where(pick, neg, work)
    w = jnp.concatenate(ws, axis=0)
    w = w / jnp.sum(w, axis=0, keepdims=True) * ROUTED_SCALE
    return jnp.concatenate(idxs, axis=0).astype(jnp.int32), w, chosen


def _merge_kernel(of_ref, ob_ref, og_ref, yna_ref, ga_ref, gb_ref, x_ref, g1_ref, sh2_ref, sc2_ref,
                  hgg_ref, ln1g_ref, ln1b_ref, wa_ref, wb_ref, wo_ref, wr_ref, rb_ref,
                  x1_ref, h2_ref, topi_ref, topw_ref, cnt_ref, *, alpha):
    o = of_ref[...] + ob_ref[...]
    parts = []
    for h in range(HG_HEADS):
        oh = o[:, h * HG_DK:(h + 1) * HG_DK]
        parts.append(oh * lax.rsqrt(jnp.mean(oh * oh, axis=-1, keepdims=True) + LN_EPS))
    y_hg = jnp.concatenate(parts, axis=-1) * hgg_ref[...] * _silu(og_ref[...])
    t = (jax.nn.sigmoid(ga_ref[...]) * _dot(y_hg.astype(BF16), wa_ref[...])
         + jax.nn.sigmoid(gb_ref[...]) * _dot(yna_ref[...].astype(BF16), wb_ref[...]))
    y = _dot(t.astype(BF16), wo_ref[...])
    x1 = _normalize(alpha * x_ref[...] + g1_ref[...] * y) * ln1g_ref[...] + ln1b_ref[...]
    x1_ref[...] = x1
    h2 = _normalize(x1) * (1.0 + sc2_ref[...]) + sh2_ref[...]
    h2_ref[...] = h2
    hh, hm, hl = _split3(h2)
    wh, wm, wl = _split3(wr_ref[...])
    logits_t = (_dot_nt(wh, hh) + _dot_nt(wh, hm) + _dot_nt(wm, hh)
                + _dot_nt(wh, hl) + _dot_nt(wl, hh) + _dot_nt(wm, hm))
    topi, topw, chosen = _route(logits_t, rb_ref[...])
    topi_ref[...] = topi
    topw_ref[...] = topw

    @pl.when((pl.program_id(0) == 0) & (pl.program_id(1) == 0))
    def _():
        cnt_ref[...] = jnp.zeros_like(cnt_ref)

    cnt_ref[...] += jnp.sum(chosen, axis=1, keepdims=True)


def _merge(o_f, o_b, p, y_na, x, g1, sh2, sc2, hg_norm_g, ln1_g, ln1_b, w_a, w_b, w_o, w_router_t, router_bias,
           alpha):
    b, s, d = x.shape
    tm = min(256, s)
    e = w_router_t.shape[0]
    tok = lambda bi, i: (bi, i, 0)
    blk = pl.BlockSpec((None, tm, d), tok)

    def sec(section):
        return pl.BlockSpec((None, None, tm, d), lambda bi, i: (section, bi, i, 0))

    mod = pl.BlockSpec((None, 1, d), lambda bi, i: (bi, 0, 0))
    vec = pl.BlockSpec((1, d), lambda bi, i: (0, 0))
    mat = pl.BlockSpec((d, d), lambda bi, i: (0, 0))
    return pl.pallas_call(
        functools.partial(_merge_kernel, alpha=alpha),
        grid=(b, s // tm),
        in_specs=[blk, blk, sec(SEC_OG), blk, sec(SEC_GA), sec(SEC_GB), blk, mod, mod, mod,
                  vec, vec, vec, mat, mat, mat,
                  pl.BlockSpec((e, d), lambda bi, i: (0, 0)),
                  pl.BlockSpec((e, 1), lambda bi, i: (0, 0))],
        out_specs=[blk, blk,
                   pl.BlockSpec((None, TOP_K, tm), lambda bi, i: (bi, 0, i)),
                   pl.BlockSpec((None, TOP_K, tm), lambda bi, i: (bi, 0, i)),
                   pl.BlockSpec((e, 128), lambda bi, i: (0, 0))],
        out_shape=[jax.ShapeDtypeStruct((b, s, d), F32), jax.ShapeDtypeStruct((b, s, d), F32),
                   jax.ShapeDtypeStruct((b, TOP_K, s), jnp.int32), jax.ShapeDtypeStruct((b, TOP_K, s), F32),
                   jax.ShapeDtypeStruct((e, 128), F32)],
        compiler_params=_cparams(("arbitrary", "arbitrary")),
        name="merge",
    )(o_f, o_b, p, y_na, p, p, x, g1, sh2, sc2, hg_norm_g.reshape(1, d), ln1_g.reshape(1, d),
      ln1_b.reshape(1, d), w_a, w_b, w_o, w_router_t, router_bias.reshape(e, 1))


MOE_TILE = 256
MOE_TOK = 256


def _plan_kernel(topi_ref, off_ref, dest_ref, carry_ref):
    @pl.when(pl.program_id(0) == 0)
    def _():
        carry_ref[...] = jnp.zeros_like(carry_ref)

    topi = topi_ref[...]
    tok = topi.shape[1]
    eidx = lax.broadcasted_iota(jnp.int32, (N_EXPERTS, tok), 0)
    hits = [eidx == topi[k:k + 1, :] for k in range(TOP_K)]
    m = jnp.zeros((N_EXPERTS, tok), F32)
    for hit in hits:
        m = jnp.where(hit, 1.0, m)
    before = (lax.broadcasted_iota(jnp.int32, (tok, tok), 0)
              < lax.broadcasted_iota(jnp.int32, (tok, tok), 1)).astype(F32).astype(BF16)
    row = off_ref[...] + carry_ref[...] + _dot(m.astype(BF16), before)
    dest = [jnp.sum(jnp.where(hit, row, 0.0), axis=0, keepdims=True) for hit in hits]
    dest_ref[...] = jnp.concatenate(dest, axis=0).astype(jnp.int32)
    carry_ref[...] += jnp.sum(m, axis=1, keepdims=True)


def _plan(topi, seg_off):
    b, k, s = topi.shape
    per_b = s // MOE_TOK
    blk = pl.BlockSpec((None, k, MOE_TOK), lambda i: (i // per_b, 0, i % per_b))
    return pl.pallas_call(
        _plan_kernel,
        grid=(b * per_b,),
        in_specs=[blk, pl.BlockSpec((N_EXPERTS, 1), lambda i: (0, 0))],
        out_specs=blk,
        out_shape=jax.ShapeDtypeStruct((b, k, s), jnp.int32),
        scratch_shapes=[pltpu.VMEM((N_EXPERTS, 1), F32)],
        compiler_params=_cparams(("arbitrary",)),
        name="plan",
    )(topi, seg_off.astype(F32).reshape(N_EXPERTS, 1))


def _row_copy(src_ref, src_row, dst_ref, dst_row, sem):
    return pltpu.make_async_copy(src_ref.at[pl.ds(src_row, 1), :], dst_ref.at[pl.ds(dst_row, 1), :], sem)


def _dispatch_kernel(cnt_ref, off_ref, dest_ref, h_ref, xs_ref, zero_ref, sem):
    tok = h_ref.shape[0]

    def issue(t, carry):
        for k in range(TOP_K):
            _row_copy(h_ref, t, xs_ref, dest_ref[k, t], sem).start()
        return carry

    def drain(t, carry):
        for k in range(TOP_K):
            _row_copy(h_ref, 0, xs_ref, 0, sem).wait()
        return carry

    lax.fori_loop(0, tok, issue, 0)

    @pl.when(pl.program_id(0) == pl.num_programs(0) - 1)
    def _():
        zero_ref[...] = jnp.zeros_like(zero_ref)

        def per_expert(e, carry):
            end = off_ref[e] + cnt_ref[e]
            npad = lax.rem(MOE_TILE - lax.rem(cnt_ref[e], MOE_TILE), MOE_TILE)

            def pad_issue(j, c):
                _row_copy(zero_ref, 0, xs_ref, end + j, sem).start()
                return c

            def pad_drain(j, c):
                _row_copy(zero_ref, 0, xs_ref, 0, sem).wait()
                return c

            lax.fori_loop(0, npad, pad_issue, 0)
            lax.fori_loop(0, npad, pad_drain, 0)
            return carry

        lax.fori_loop(0, N_EXPERTS, per_expert, 0)

    lax.fori_loop(0, tok, drain, 0)


def _dispatch(h2, dest, cnt, seg_off, n_rows):
    t, d = h2.shape
    b, k, s = dest.shape
    per_b = s // MOE_TOK
    grid_spec = pltpu.PrefetchScalarGridSpec(
        num_scalar_prefetch=2,
        grid=(t // MOE_TOK,),
        in_specs=[pl.BlockSpec((None, k, MOE_TOK), lambda i, c, o: (i // per_b, 0, i % per_b),
                               memory_space=pltpu.SMEM),
                  pl.BlockSpec((MOE_TOK, d), lambda i, c, o: (i, 0))],
        out_specs=pl.BlockSpec(memory_space=pl.ANY),
        scratch_shapes=[pltpu.VMEM((8, d), F32), pltpu.SemaphoreType.DMA(())],
    )
    return pl.pallas_call(
        _dispatch_kernel,
        grid_spec=grid_spec,
        out_shape=jax.ShapeDtypeStruct((n_rows, d), F32),
        compiler_params=_cparams(("arbitrary",)),
        name="dispatch",
    )(cnt, seg_off, dest, h2)


def _experts_kernel(te_ref, tb_ref, nt_ref, xs_ref, wg_ref, wu_ref, wd_ref, ys_ref):
    @pl.when(pl.program_id(0) < nt_ref[0])
    def _():
        x = xs_ref[...].astype(BF16)
        act = _silu(_dot(x, wg_ref[...])) * _dot(x, wu_ref[...])
        ys_ref[...] = _dot(act.astype(BF16), wd_ref[...])


def _experts(xs, tile_expert, tile_block, n_tiles, wg, wu, wd):
    n_rows, d = xs.shape
    f = wg.shape[2]
    rows = pl.BlockSpec((MOE_TILE, d), lambda i, te, tb, nt: (tb[i], 0))
    grid_spec = pltpu.PrefetchScalarGridSpec(
        num_scalar_prefetch=3,
        grid=(n_rows // MOE_TILE,),
        in_specs=[rows,
                  pl.BlockSpec((None, d, f), lambda i, te, tb, nt: (te[i], 0, 0)),
                  pl.BlockSpec((None, d, f), lambda i, te, tb, nt: (te[i], 0, 0)),
                  pl.BlockSpec((None, f, d), lambda i, te, tb, nt: (te[i], 0, 0))],
        out_specs=rows,
    )
    return pl.pallas_call(
        _experts_kernel,
        grid_spec=grid_spec,
        out_shape=jax.ShapeDtypeStruct((n_rows, d), F32),
        compiler_params=_cparams(("arbitrary",)),
        name="experts",
    )(tile_expert, tile_block, n_tiles, xs, wg, wu, wd)


def _combine_kernel(dest_ref, topw_ref, h_ref, x1_ref, g2_ref, sg_ref, su_ref, sd_ref, ln2g_ref, ln2b_ref,
                    ys_ref, o_ref, buf_ref, sem, *, alpha):
    tok = h_ref.shape[0]

    def issue(t, carry):
        for k in range(TOP_K):
            _row_copy(ys_ref, dest_ref[k, t], buf_ref.at[k], t, sem).start()
        return carry

    def drain(t, carry):
        for k in range(TOP_K):
            _row_copy(ys_ref, 0, buf_ref.at[0], 0, sem).wait()
        return carry

    lax.fori_loop(0, tok, issue, 0)
    h = h_ref[...].astype(BF16)
    act = _silu(_dot(h, sg_ref[...])) * _dot(h, su_ref[...])
    y = _dot(act.astype(BF16), sd_ref[...])
    w = topw_ref[...].T
    lax.fori_loop(0, tok, drain, 0)
    for k in range(TOP_K):
        y = y + w[:, k:k + 1] * buf_ref[k]
    o_ref[...] = _normalize(alpha * x1_ref[...] + g2_ref[...] * y) * ln2g_ref[...] + ln2b_ref[...]


def _combine(ys, dest, topw, h2, x1, g2, sg, su, sd, ln2_g, ln2_b, alpha):
    t, d = h2.shape
    b, k, s = dest.shape
    per_b = s // MOE_TOK
    fs = sg.shape[1]
    tk = lambda i: (i // per_b, 0, i % per_b)
    rows = pl.BlockSpec((MOE_TOK, d), lambda i: (i, 0))
    vec = pl.BlockSpec((1, d), lambda i: (0, 0))
    return pl.pallas_call(
        functools.partial(_combine_kernel, alpha=alpha),
        grid=(t // MOE_TOK,),
        in_specs=[pl.BlockSpec((None, k, MOE_TOK), tk, memory_space=pltpu.SMEM),
                  pl.BlockSpec((None, k, MOE_TOK), tk),
                  rows, rows,
                  pl.BlockSpec((None, 1, d), lambda i: (i // per_b, 0, 0)),
                  pl.BlockSpec((d, fs), lambda i: (0, 0)),
                  pl.BlockSpec((d, fs), lambda i: (0, 0)),
                  pl.BlockSpec((fs, d), lambda i: (0, 0)),
                  vec, vec,
                  pl.BlockSpec(memory_space=pl.ANY)],
        out_specs=rows,
        out_shape=jax.ShapeDtypeStruct((t, d), F32),
        scratch_shapes=[pltpu.VMEM((k, MOE_TOK, d), F32), pltpu.SemaphoreType.DMA(())],
        compiler_params=_cparams(("arbitrary",)),
        name="combine",
    )(dest, topw, h2, x1, g2, sg, su, sd, ln2_g.reshape(1, d), ln2_b.reshape(1, d), ys)


def _moe(h2, topi, topw, cnt, x1, g2, wg, wu, wd, sg, su, sd, ln2_g, ln2_b, alpha):
    b, s, d = x1.shape
    t = b * s
    cnt = cnt[:, 0].astype(jnp.int32)
    tiles_e = (cnt + (MOE_TILE - 1)) // MOE_TILE
    tiles_cum = jnp.cumsum(tiles_e)
    seg_off = (tiles_cum - tiles_e) * MOE_TILE
    n_tiles_max = t * TOP_K // MOE_TILE + N_EXPERTS
    tile_block = jnp.minimum(jnp.arange(n_tiles_max, dtype=jnp.int32), tiles_cum[-1] - 1)
    tile_expert = jnp.sum((tiles_cum[None, :] <= tile_block[:, None]).astype(jnp.int32), axis=1)
    n_tiles = tiles_cum[-1:].astype(jnp.int32)

    dest = _plan(topi, seg_off)
    h2f = h2.reshape(t, d)
    xs = _dispatch(h2f, dest, cnt, seg_off.astype(jnp.int32), n_tiles_max * MOE_TILE)
    ys = _experts(xs, tile_expert, tile_block, n_tiles, wg, wu, wd)
    out = _combine(ys, dest, topw, h2f, x1.reshape(t, d), g2, sg, su, sd, ln2_g, ln2_b, alpha)
    return out.reshape(b, s, d)


def kernel(x, c, ctx, c_ctx, w_ada, b_ada, w_in, hg_lb_fwd, hg_lb_bwd, hg_norm_g, na_rpb, w_branch_a, w_branch_b, w_out, ln1_g, ln1_b, w_router, router_bias, w_e_gate, w_e_up, w_e_down, w_sh_gate, w_sh_up, w_sh_down, ln2_g, ln2_b):
    depth = w_ada.shape[0]
    assert depth == 1, "single-layer block"
    b, s, d = x.shape
    alpha = (2.0 * depth) ** 0.25
    l = 0
    lb_fwd = jnp.cumsum(jax.nn.softmax(hg_lb_fwd.astype(F32), axis=0), axis=0)[l]
    lb_bwd = jnp.cumsum(jax.nn.softmax(hg_lb_bwd.astype(F32), axis=0), axis=0)[l]

    cond_rows = jnp.concatenate([c, c_ctx[None, :], jnp.zeros((8 - b - 1, d), F32)], axis=0)
    mod = _ada(cond_rows, w_ada[l], b_ada[l])
    sh1, sc1, g1, sh2, sc2, g2 = [m[:b, None, :] for m in jnp.split(mod, 6, axis=-1)]
    csh1, csc1 = [jnp.broadcast_to(m[b:b + 1, None, :], (b, 1, d)) for m in jnp.split(mod, 6, axis=-1)[:2]]

    w_in_b = w_in[l].astype(BF16)
    p = _inproj(x, sh1, sc1, w_in_b)
    pc = _inproj(ctx, csh1, csc1, w_in_b)

    o_f, o_b = _hgrn(p, pc, lb_fwd, lb_bwd)
    cos, sin, bias = _na_tables(na_rpb[l], s)
    y_na = _natten(p, pc, cos, sin, bias)

    x1, h2, topi, topw, cnt = _merge(o_f, o_b, p, y_na, x, g1, sh2, sc2, hg_norm_g[l], ln1_g[l], ln1_b[l],
                                     w_branch_a[l].astype(BF16), w_branch_b[l].astype(BF16),
                                     w_out[l].astype(BF16), w_router[l].T, router_bias[l], alpha)

    return _moe(h2, topi, topw, cnt, x1, g2,
                w_e_gate[l].astype(BF16), w_e_up[l].astype(BF16), w_e_down[l].astype(BF16),
                w_sh_gate[l].astype(BF16), w_sh_up[l].astype(BF16), w_sh_down[l].astype(BF16),
                ln2_g[l], ln2_b[l], alpha)
```

```python
import functools

import numpy as np
import jax
import jax.numpy as jnp
from jax import lax
from jax.experimental import pallas as pl
from jax.experimental.pallas import tpu as pltpu

F32 = jnp.float32
BF16 = jnp.bfloat16

D_MODEL = 1024
GRID_W = 64
HG_HEADS = 8
HG_DK = 128
HG_CHUNK = 64
NA_HEADS = 16
NA_HD = 64
NA_WIN_R = 8
NA_WIN_C = 16
ROPE_THETA = 10000.0
NEG_INF = -1e30
N_EXPERTS = 64
EXPERT_DIM = 256
TOP_K = 8
N_GROUPS = 8
TOPK_GROUPS = 4
ROUTED_SCALE = 2.5
LN_EPS = 1e-6
N_SECTIONS = 10
SEC_Q, SEC_FF, SEC_FB, SEC_I, SEC_OG, SEC_NQ, SEC_NK, SEC_NV, SEC_GA, SEC_GB = range(10)

VMEM_LIMIT = 56 * 1024 * 1024


def _cparams(sem):
    return pltpu.CompilerParams(dimension_semantics=sem, vmem_limit_bytes=VMEM_LIMIT)


def _normalize(x):
    mu = jnp.mean(x, axis=-1, keepdims=True)
    xc = x - mu
    var = jnp.mean(xc * xc, axis=-1, keepdims=True)
    return xc * lax.rsqrt(var + LN_EPS)


def _silu(x):
    return x * jax.nn.sigmoid(x)


def _dot(a, b):
    return jnp.dot(a, b, preferred_element_type=F32)


def _dot_nt(a, b):
    return lax.dot_general(a, b, (((1,), (1,)), ((), ())), preferred_element_type=F32)


def _dot_tn(a, b):
    return lax.dot_general(a, b, (((0,), (0,)), ((), ())), preferred_element_type=F32)


PACK_SUBLANES = 4
U32 = jnp.uint32


def _store_packed(ref, x):
    n, d = x.shape
    half = d // 2
    lo = lax.bitcast_convert_type(x[:, :half].astype(BF16).astype(F32), U32) >> 16
    hi = lax.bitcast_convert_type(x[:, half:].astype(BF16).astype(F32), U32) & jnp.uint32(0xFFFF0000)
    w = lo | hi
    for c in range(PACK_SUBLANES):
        ref[pl.ds(c, n, stride=PACK_SUBLANES), :] = w[:, c * 128:(c + 1) * 128]


def _load_packed(ref, n):
    w = jnp.concatenate([ref[pl.ds(c, n, stride=PACK_SUBLANES), :] for c in range(PACK_SUBLANES)], axis=-1)
    lo = lax.bitcast_convert_type(w << 16, F32)
    hi = lax.bitcast_convert_type(w & jnp.uint32(0xFFFF0000), F32)
    return jnp.concatenate([lo, hi], axis=-1)


def _split3(x):
    hi = x.astype(BF16)
    r1 = x - hi.astype(F32)
    mid = r1.astype(BF16)
    lo = (r1 - mid.astype(F32)).astype(BF16)
    return hi, mid, lo


def _ada_kernel(c_ref, w_ref, b_ref, o_ref):
    cond = _silu(c_ref[...])
    o_ref[...] = _dot(cond.astype(BF16), w_ref[...].astype(BF16)) + b_ref[...]


def _ada(cond_rows, w_ada, b_ada):
    r, d = cond_rows.shape
    n = w_ada.shape[1]
    tn = 1024
    return pl.pallas_call(
        _ada_kernel,
        grid=(n // tn,),
        in_specs=[pl.BlockSpec((r, d), lambda j: (0, 0)),
                  pl.BlockSpec((d, tn), lambda j: (0, j)),
                  pl.BlockSpec((1, tn), lambda j: (0, j))],
        out_specs=pl.BlockSpec((r, tn), lambda j: (0, j)),
        out_shape=jax.ShapeDtypeStruct((r, n), F32),
        compiler_params=_cparams(("arbitrary",)),
        name="ada",
    )(cond_rows, w_ada, b_ada.reshape(1, n))


def _inproj_kernel(x_ref, sh_ref, sc_ref, w_ref, o_ref, h_ref):
    @pl.when(pl.program_id(2) == 0)
    def _():
        h = _normalize(x_ref[...]) * (1.0 + sc_ref[...]) + sh_ref[...]
        h_ref[...] = h.astype(BF16)

    o_ref[...] = _dot(h_ref[...], w_ref[...])


def _inproj(x, shift, scale, w_in_bf16):
    b, s, d = x.shape
    tm = min(1024, s)
    nj = w_in_bf16.shape[1] // d
    return pl.pallas_call(
        _inproj_kernel,
        grid=(b, s // tm, nj),
        in_specs=[pl.BlockSpec((None, tm, d), lambda bi, i, j: (bi, i, 0)),
                  pl.BlockSpec((None, 1, d), lambda bi, i, j: (bi, 0, 0)),
                  pl.BlockSpec((None, 1, d), lambda bi, i, j: (bi, 0, 0)),
                  pl.BlockSpec((d, d), lambda bi, i, j: (0, j))],
        out_specs=pl.BlockSpec((None, None, tm, d), lambda bi, i, j: (j, bi, i, 0)),
        out_shape=jax.ShapeDtypeStruct((nj, b, s, d), F32),
        scratch_shapes=[pltpu.VMEM((tm, d), BF16)],
        compiler_params=_cparams(("arbitrary", "arbitrary", "arbitrary")),
        name="inproj",
    )(x, shift, scale, w_in_bf16)


def _hgrn_chunk(q, fraw, v, lb, st_ref, d, tri_bf16, keep, last_row):
    f = lb + (1.0 - lb) * jax.nn.sigmoid(fraw)
    k = 1.0 - f
    lf = jnp.log(f)
    hi, mid, lo = _split3(lf)
    a = _dot(tri_bf16, hi) + _dot(tri_bf16, mid) + _dot(tri_bf16, lo)
    a_last = a[last_row:last_row + 1, :]
    kd = (k * jnp.exp(a_last - a)).astype(BF16)
    decay = jnp.exp(a_last)
    vb = v.astype(BF16)
    if q is not None:
        qa = (_silu(q) * jnp.exp(a)).astype(BF16)
        kb = (k * jnp.exp(-a)).astype(BF16)
    outs = []
    for h in range(HG_HEADS):
        sl = slice(h * HG_DK, (h + 1) * HG_DK)
        st = st_ref[d, h]
        if q is not None:
            sc = jnp.where(keep, _dot_nt(qa[:, sl], kb[:, sl]), 0.0)
            outs.append(_dot(sc.astype(BF16), vb[:, sl]) + _dot_nt(qa[:, sl], st.astype(BF16)))
        st_ref[d, h] = st * decay[:, sl] + _dot_tn(vb[:, sl], kd[:, sl])
    if q is None:
        return None
    return jnp.concatenate(outs, axis=-1)


def _hgrn_kernel(qf_ref, ff_ref, if_ref, qb_ref, fb_ref, ib_ref, cff_ref, cfb_ref, ci_ref,
                 lbf_ref, lbb_ref, of_ref, ob_ref, st_ref, *, n_sub, n_ctx_sub):
    n = pl.program_id(1)
    c = HG_CHUNK
    row = lax.broadcasted_iota(jnp.int32, (c, c), 0)
    col = lax.broadcasted_iota(jnp.int32, (c, c), 1)
    keep_f = col <= row
    keep_b = col >= row
    tri_f = keep_f.astype(F32).astype(BF16)
    tri_b = keep_b.astype(F32).astype(BF16)
    lbf = lbf_ref[...]
    lbb = lbb_ref[...]

    @pl.when(n == 0)
    def _():
        st_ref[...] = jnp.zeros_like(st_ref)

        def body(i, carry):
            r0 = pl.multiple_of(i * c, c)
            _hgrn_chunk(None, cff_ref[pl.ds(r0, c), :], ci_ref[pl.ds(r0, c), :], lbf, st_ref, 0,
                        tri_f, keep_f, c - 1)
            r1 = pl.multiple_of((n_ctx_sub - 1 - i) * c, c)
            _hgrn_chunk(None, cfb_ref[pl.ds(r1, c), :], ci_ref[pl.ds(r1, c), :], lbb, st_ref, 1,
                        tri_b, keep_b, 0)
            return carry

        lax.fori_loop(0, n_ctx_sub, body, 0)

    @pl.when(n > 0)
    def _():
        def body(i, carry):
            r0 = pl.multiple_of(i * c, c)
            of_ref[pl.ds(r0, c), :] = _hgrn_chunk(
                qf_ref[pl.ds(r0, c), :], ff_ref[pl.ds(r0, c), :], if_ref[pl.ds(r0, c), :], lbf, st_ref, 0,
                tri_f, keep_f, c - 1)
            r1 = pl.multiple_of((n_sub - 1 - i) * c, c)
            ob_ref[pl.ds(r1, c), :] = _hgrn_chunk(
                qb_ref[pl.ds(r1, c), :], fb_ref[pl.ds(r1, c), :], ib_ref[pl.ds(r1, c), :], lbb, st_ref, 1,
                tri_b, keep_b, 0)
            return carry

        lax.fori_loop(0, n_sub, body, 0)


def _hgrn(p, pc, lb_fwd, lb_bwd):
    _, b, s, w = p.shape
    ctx_len = pc.shape[2]
    tb = min(256, s)
    nb = s // tb
    fwd = lambda bi, n: jnp.maximum(n - 1, 0)
    bwd = lambda bi, n: nb - 1 - jnp.maximum(n - 1, 0)

    def sec(section, blk):
        return pl.BlockSpec((None, None, tb, w), lambda bi, n: (section, bi, blk(bi, n), 0))

    def csec(section):
        return pl.BlockSpec((None, None, ctx_len, w), lambda bi, n: (section, bi, 0, 0))

    vec = pl.BlockSpec((1, w), lambda bi, n: (0, 0))
    kern = functools.partial(_hgrn_kernel, n_sub=tb // HG_CHUNK, n_ctx_sub=ctx_len // HG_CHUNK)
    return pl.pallas_call(
        kern,
        grid=(b, nb + 1),
        in_specs=[sec(SEC_Q, fwd), sec(SEC_FF, fwd), sec(SEC_I, fwd),
                  sec(SEC_Q, bwd), sec(SEC_FB, bwd), sec(SEC_I, bwd),
                  csec(SEC_FF), csec(SEC_FB), csec(SEC_I), vec, vec],
        out_specs=[pl.BlockSpec((None, tb, w), lambda bi, n: (bi, fwd(bi, n), 0)),
                   pl.BlockSpec((None, tb, w), lambda bi, n: (bi, bwd(bi, n), 0))],
        out_shape=[jax.ShapeDtypeStruct((b, s, w), F32), jax.ShapeDtypeStruct((b, s, w), F32)],
        scratch_shapes=[pltpu.VMEM((2, HG_HEADS, HG_DK, HG_DK), F32)],
        compiler_params=_cparams(("arbitrary", "arbitrary")),
        name="hgrn",
    )(p, p, p, p, p, p, pc, pc, pc, lb_fwd.reshape(1, w), lb_bwd.reshape(1, w))


NA_ROWS_PER_STEP = 8
NA_PREP_ROWS = 512
NA_KEY_TILE = 128
NA_SPAN = (NA_WIN_R + 2) * GRID_W


def _rope(t, cos, sin_signed, first_half):
    w = t.shape[-1]
    partner = jnp.where(first_half, pltpu.roll(t, w - 16, 1), pltpu.roll(t, 16, 1))
    return t * cos + partner * sin_signed


def _fold_lanes(op, *arrays):
    tiles = [a[:, c:c + 128] for a in arrays for c in range(0, a.shape[-1], 128)]
    acc = tiles[0]
    for t in tiles[1:]:
        acc = op(acc, t)
    return acc


def _natten_kernel(q_ref, k_ref, v_ref, kc_ref, vc_ref, cos_ref, sin_ref, t2_ref, o_ref,
                   kt_s, v_s, kc_s, vc_s, bias_s, *, rows):
    rblk = pl.program_id(2)
    hd = NA_HD
    lane = lax.broadcasted_iota(jnp.int32, (1, 2 * hd), 1)
    first_half = (lane % 32) < 16
    scale = NA_HD ** -0.5

    @pl.when(rblk == 0)
    def _():
        kc = kc_ref[...].astype(BF16)
        vc = vc_ref[...].astype(BF16)
        qi = lax.broadcasted_iota(jnp.int32, (GRID_W, GRID_W), 0)
        ki = lax.broadcasted_iota(jnp.int32, (GRID_W, GRID_W), 1)
        cstart = jnp.clip(qi - NA_WIN_C // 2, 0, GRID_W - NA_WIN_C)
        in_win = (ki >= cstart) & (ki < cstart + NA_WIN_C)
        masked = jnp.full((GRID_W, GRID_W), NEG_INF, F32)
        s_len = k_ref.shape[0]
        for h in range(2):
            sl = slice(h * hd, (h + 1) * hd)
            kc_s[h] = kc[:, sl]
            vc_s[h] = vc[:, sl]
            kt_s[h, s_len // NA_KEY_TILE] = jnp.zeros((hd, NA_KEY_TILE), BF16)
            v_s[h, s_len:s_len + NA_KEY_TILE, :] = jnp.zeros((NA_KEY_TILE, hd), BF16)
            tiles = [jnp.where(in_win, t2_ref[h, dr], NEG_INF) for dr in range(2 * NA_WIN_R - 1)]
            for bidx in range(NA_WIN_R + 1):
                v, par = (bidx, 0) if bidx < NA_WIN_R else (NA_WIN_R // 2, 1)
                for piece in range(NA_SPAN // GRID_W):
                    j = piece - par
                    tile = tiles[NA_WIN_R - 1 - v + j] if 0 <= j < NA_WIN_R else masked
                    bias_s[h, bidx, :, piece * GRID_W:(piece + 1) * GRID_W] = tile

        def prep(i, carry):
            r0 = pl.multiple_of(i * NA_PREP_ROWS, NA_PREP_ROWS)
            rws = pl.ds(r0, NA_PREP_ROWS)
            kr = _rope(k_ref[rws, :], cos_ref[rws, :], sin_ref[rws, :], first_half)
            krt = kr.T.astype(BF16)
            vv = v_ref[rws, :].astype(BF16)
            for h in range(2):
                sl = slice(h * hd, (h + 1) * hd)
                for c in range(NA_PREP_ROWS // NA_KEY_TILE):
                    kt_s[h, i * (NA_PREP_ROWS // NA_KEY_TILE) + c] = krt[sl, c * NA_KEY_TILE:(c + 1) * NA_KEY_TILE]
                v_s[h, rws, :] = vv[:, sl]
            return carry

        lax.fori_loop(0, s_len // NA_PREP_ROWS, prep, 0)

    tq = NA_ROWS_PER_STEP * GRID_W
    t0 = pl.multiple_of(rblk * tq, tq)
    q = q_ref[...] * scale
    qr = _rope(q, cos_ref[pl.ds(t0, tq), :], sin_ref[pl.ds(t0, tq), :], first_half)
    qb = q.astype(BF16)
    qrb = qr.astype(BF16)
    rws = [slice(rr * GRID_W, (rr + 1) * GRID_W) for rr in range(NA_ROWS_PER_STEP)]
    tile0, bidx = [], []
    for rr in range(NA_ROWS_PER_STEP):
        r = rblk * NA_ROWS_PER_STEP + rr
        rs = jnp.clip(r - NA_WIN_R // 2, 0, rows - NA_WIN_R)
        tile0.append(lax.shift_right_logical(rs, 1))
        bidx.append(jnp.where((rs & 1) == 1, NA_WIN_R, r - rs))

    def scores(h):
        sl = slice(h * hd, (h + 1) * hd)
        qrb_h = qrb[:, sl]
        s_ctx_all = _dot_nt(qb[:, sl], kc_s[h])
        s_win = []
        for rr in range(NA_ROWS_PER_STEP):
            kt = kt_s[h, pl.ds(tile0[rr], NA_SPAN // NA_KEY_TILE)]
            kt = jnp.concatenate([kt[c] for c in range(NA_SPAN // NA_KEY_TILE)], axis=-1)
            s_win.append(_dot(qrb_h[rws[rr]], kt))
        return s_win, s_ctx_all

    def softmax(h, s_win, s_ctx_all):
        e_win, e_ctx, rinv = [], [], []
        for rr in range(NA_ROWS_PER_STEP):
            sw = s_win[rr] + bias_s[h, bidx[rr]]
            sc = s_ctx_all[rws[rr]]
            m = jnp.max(_fold_lanes(jnp.maximum, sw, sc), axis=-1, keepdims=True)
            ew = jnp.exp(sw - m)
            ec = jnp.exp(sc - m)
            rinv.append(1.0 / jnp.sum(_fold_lanes(jnp.add, ew, ec), axis=-1, keepdims=True))
            e_win.append(ew.astype(BF16))
            e_ctx.append(ec.astype(BF16))
        return e_win, e_ctx, rinv

    def values(h, e_win, e_ctx, rinv):
        o_win = []
        for rr in range(NA_ROWS_PER_STEP):
            k0 = pl.multiple_of(tile0[rr] * NA_KEY_TILE, NA_KEY_TILE)
            o_win.append(_dot(e_win[rr], v_s[h, pl.ds(k0, NA_SPAN), :]))
        o_ctx = _dot(jnp.concatenate(e_ctx, axis=0), vc_s[h])
        return (jnp.concatenate(o_win, axis=0) + o_ctx) * jnp.concatenate(rinv, axis=0)

    s0 = scores(0)
    s1 = scores(1)
    p0 = softmax(0, *s0)
    o0 = values(0, *p0)
    p1 = softmax(1, *s1)
    o1 = values(1, *p1)
    o_ref[...] = jnp.concatenate([o0, o1], axis=-1)


def _na_tables(rpb, s):
    half = NA_HD // 2
    pos = jnp.arange(s)
    rowp = (pos // GRID_W).astype(F32)
    colp = (pos % GRID_W).astype(F32)
    inv = jnp.power(ROPE_THETA, -jnp.arange(0, half, 2, dtype=F32) / half)
    ang_r = rowp[:, None] * inv[None, :]
    ang_c = colp[:, None] * inv[None, :]
    cos = jnp.concatenate([jnp.cos(ang_r), jnp.cos(ang_r), jnp.cos(ang_c), jnp.cos(ang_c)], axis=-1)
    sin = jnp.concatenate([-jnp.sin(ang_r), jnp.sin(ang_r), -jnp.sin(ang_c), jnp.sin(ang_c)], axis=-1)
    cos = jnp.concatenate([cos, cos], axis=-1)
    sin = jnp.concatenate([sin, sin], axis=-1)

    pad = GRID_W - NA_WIN_C
    rp = jnp.pad(rpb.astype(F32), ((0, 0), (0, 0), (pad, pad)), mode="edge")
    t2 = jnp.stack([rp[:, :, GRID_W - 1 - qc:2 * GRID_W - 1 - qc] for qc in range(GRID_W)], axis=2)
    return cos, sin, t2


def _natten(p, pc, cos, sin, t2):
    _, b, s, w = p.shape
    ctx_len = pc.shape[2]
    rows = s // GRID_W
    assert rows >= NA_WIN_R and rows % NA_ROWS_PER_STEP == 0
    tq = NA_ROWS_PER_STEP * GRID_W
    hw = 2 * NA_HD
    nhp = w // hw
    kern = functools.partial(_natten_kernel, rows=rows)
    return pl.pallas_call(
        kern,
        grid=(b, nhp, rows // NA_ROWS_PER_STEP),
        in_specs=[pl.BlockSpec((None, None, tq, hw), lambda bi, hp, r: (SEC_NQ, bi, r, hp)),
                  pl.BlockSpec((None, None, s, hw), lambda bi, hp, r: (SEC_NK, bi, 0, hp)),
                  pl.BlockSpec((None, None, s, hw), lambda bi, hp, r: (SEC_NV, bi, 0, hp)),
                  pl.BlockSpec((None, None, ctx_len, hw), lambda bi, hp, r: (SEC_NK, bi, 0, hp)),
                  pl.BlockSpec((None, None, ctx_len, hw), lambda bi, hp, r: (SEC_NV, bi, 0, hp)),
                  pl.BlockSpec((s, hw), lambda bi, hp, r: (0, 0)),
                  pl.BlockSpec((s, hw), lambda bi, hp, r: (0, 0)),
                  pl.BlockSpec((2, 2 * NA_WIN_R - 1, GRID_W, GRID_W), lambda bi, hp, r: (hp, 0, 0, 0))],
        out_specs=pl.BlockSpec((None, tq, hw), lambda bi, hp, r: (bi, r, hp)),
        out_shape=jax.ShapeDtypeStruct((b, s, w), F32),
        scratch_shapes=[pltpu.VMEM((2, s // NA_KEY_TILE + 1, NA_HD, NA_KEY_TILE), BF16),
                        pltpu.VMEM((2, s + NA_KEY_TILE, NA_HD), BF16),
                        pltpu.VMEM((2, ctx_len, NA_HD), BF16), pltpu.VMEM((2, ctx_len, NA_HD), BF16),
                        pltpu.VMEM((2, NA_WIN_R + 1, GRID_W, NA_SPAN), F32)],
        compiler_params=_cparams(("arbitrary", "arbitrary", "arbitrary")),
        name="natten",
    )(p, p, p, pc, pc, cos, sin, t2)


def _route(logits_t, rbias):
    e, t = logits_t.shape
    gsz = e // N_GROUPS
    scores = jax.nn.sigmoid(logits_t)
    sel = scores + rbias
    neg = -jnp.inf
    sub = lax.broadcasted_iota(jnp.int32, (gsz, t), 0).astype(F32)
    gscore = []
    for g in range(N_GROUPS):
        grp = sel[g * gsz:(g + 1) * gsz, :]
        m1 = jnp.max(grp, axis=0, keepdims=True)
        first = jnp.min(jnp.where(grp == m1, sub, float(gsz)), axis=0, keepdims=True)
        m2 = jnp.max(jnp.where(sub == first, neg, grp), axis=0, keepdims=True)
        gscore.append(m1 + m2)
    masked = []
    for g in range(N_GROUPS):
        rank = jnp.zeros((1, t), F32)
        for g2 in range(N_GROUPS):
            if g2 == g:
                continue
            if g2 < g:
                ahead = gscore[g2] >= gscore[g]
            else:
                ahead = gscore[g2] > gscore[g]
            rank = rank + jnp.where(ahead, 1.0, 0.0)
        masked.append(jnp.where(rank < TOPK_GROUPS, sel[g * gsz:(g + 1) * gsz, :], neg))
    work = jnp.concatenate(masked, axis=0)
    eidx = lax.broadcasted_iota(jnp.int32, (e, t), 0).astype(F32)
    idxs, ws = [], []
    chosen = jnp.zeros((e, t), F32)
    for _ in range(TOP_K):
        m = jnp.max(work, axis=0, keepdims=True)
        first = jnp.min(jnp.where(work == m, eidx, float(e)), axis=0, keepdims=True)
        pick = eidx == first
        idxs.append(first)
        ws.append(jnp.sum(jnp.where(pick, scores, 0.0), axis=0, keepdims=True))
        chosen = jnp.where(pick, 1.0, chosen)
        work = jnp.where(pick, neg, work)
    w = jnp.concatenate(ws, axis=0)
    w = w / jnp.sum(w, axis=0, keepdims=True) * ROUTED_SCALE
    return jnp.concatenate(idxs, axis=0).astype(jnp.int32), w, chosen


def _merge_kernel(of_ref, ob_ref, og_ref, yna_ref, ga_ref, gb_ref, x_ref, g1_ref, sh2_ref, sc2_ref,
                  hgg_ref, ln1g_ref, ln1b_ref, wa_ref, wb_ref, wo_ref, wr_ref, rb_ref,
                  x1_ref, h2_ref, topi_ref, topw_ref, cnt_ref, *, alpha):
    o = of_ref[...] + ob_ref[...]
    parts = []
    for h in range(HG_HEADS):
        oh = o[:, h * HG_DK:(h + 1) * HG_DK]
        parts.append(oh * lax.rsqrt(jnp.mean(oh * oh, axis=-1, keepdims=True) + LN_EPS))
    y_hg = jnp.concatenate(parts, axis=-1) * hgg_ref[...] * _silu(og_ref[...])
    t = (jax.nn.sigmoid(ga_ref[...]) * _dot(y_hg.astype(BF16), wa_ref[...])
         + jax.nn.sigmoid(gb_ref[...]) * _dot(yna_ref[...].astype(BF16), wb_ref[...]))
    y = _dot(t.astype(BF16), wo_ref[...])
    x1 = _normalize(alpha * x_ref[...] + g1_ref[...] * y) * ln1g_ref[...] + ln1b_ref[...]
    x1_ref[...] = x1
    h2 = _normalize(x1) * (1.0 + sc2_ref[...]) + sh2_ref[...]
    _store_packed(h2_ref, h2)
    hh, hm, hl = _split3(h2)
    wh, wm, wl = _split3(wr_ref[...])
    logits_t = (_dot_nt(wh, hh) + _dot_nt(wh, hm) + _dot_nt(wm, hh)
                + _dot_nt(wh, hl) + _dot_nt(wl, hh) + _dot_nt(wm, hm))
    topi, topw, chosen = _route(logits_t, rb_ref[...])
    topi_ref[...] = topi
    topw_ref[...] = topw

    @pl.when((pl.program_id(0) == 0) & (pl.program_id(1) == 0))
    def _():
        cnt_ref[...] = jnp.zeros_like(cnt_ref)

    cnt_ref[...] += jnp.sum(chosen, axis=1, keepdims=True)


def _merge(o_f, o_b, p, y_na, x, g1, sh2, sc2, hg_norm_g, ln1_g, ln1_b, w_a, w_b, w_o, w_router_t, router_bias,
           alpha):
    b, s, d = x.shape
    tm = min(256, s)
    e = w_router_t.shape[0]
    tok = lambda bi, i: (bi, i, 0)
    blk = pl.BlockSpec((None, tm, d), tok)

    def sec(section):
        return pl.BlockSpec((None, None, tm, d), lambda bi, i: (section, bi, i, 0))

    mod = pl.BlockSpec((None, 1, d), lambda bi, i: (bi, 0, 0))
    vec = pl.BlockSpec((1, d), lambda bi, i: (0, 0))
    mat = pl.BlockSpec((d, d), lambda bi, i: (0, 0))
    return pl.pallas_call(
        functools.partial(_merge_kernel, alpha=alpha),
        grid=(b, s // tm),
        in_specs=[blk, blk, sec(SEC_OG), blk, sec(SEC_GA), sec(SEC_GB), blk, mod, mod, mod,
                  vec, vec, vec, mat, mat, mat,
                  pl.BlockSpec((e, d), lambda bi, i: (0, 0)),
                  pl.BlockSpec((e, 1), lambda bi, i: (0, 0))],
        out_specs=[blk,
                   pl.BlockSpec((tm * PACK_SUBLANES, 128), lambda bi, i: (bi * (s // tm) + i, 0)),
                   pl.BlockSpec((None, TOP_K, tm), lambda bi, i: (bi, 0, i)),
                   pl.BlockSpec((None, TOP_K, tm), lambda bi, i: (bi, 0, i)),
                   pl.BlockSpec((e, 128), lambda bi, i: (0, 0))],
        out_shape=[jax.ShapeDtypeStruct((b, s, d), F32),
                   jax.ShapeDtypeStruct((b * s * PACK_SUBLANES, 128), U32),
                   jax.ShapeDtypeStruct((b, TOP_K, s), jnp.int32), jax.ShapeDtypeStruct((b, TOP_K, s), F32),
                   jax.ShapeDtypeStruct((e, 128), F32)],
        compiler_params=_cparams(("arbitrary", "arbitrary")),
        name="merge",
    )(o_f, o_b, p, y_na, p, p, x, g1, sh2, sc2, hg_norm_g.reshape(1, d), ln1_g.reshape(1, d),
      ln1_b.reshape(1, d), w_a, w_b, w_o, w_router_t, router_bias.reshape(e, 1))


MOE_TILE = 256
MOE_TOK = 256


def _plan_kernel(topi_ref, off_ref, dest_ref, carry_ref):
    @pl.when(pl.program_id(0) == 0)
    def _():
        carry_ref[...] = jnp.zeros_like(carry_ref)

    topi = topi_ref[...]
    tok = topi.shape[1]
    eidx = lax.broadcasted_iota(jnp.int32, (N_EXPERTS, tok), 0)
    hits = [eidx == topi[k:k + 1, :] for k in range(TOP_K)]
    m = jnp.zeros((N_EXPERTS, tok), F32)
    for hit in hits:
        m = jnp.where(hit, 1.0, m)
    before = (lax.broadcasted_iota(jnp.int32, (tok, tok), 0)
              < lax.broadcasted_iota(jnp.int32, (tok, tok), 1)).astype(F32).astype(BF16)
    row = off_ref[...] + carry_ref[...] + _dot(m.astype(BF16), before)
    dest = [jnp.sum(jnp.where(hit, row, 0.0), axis=0, keepdims=True) for hit in hits]
    dest_ref[...] = jnp.concatenate(dest, axis=0).astype(jnp.int32)
    carry_ref[...] += jnp.sum(m, axis=1, keepdims=True)


def _plan(topi, seg_off):
    b, k, s = topi.shape
    per_b = s // MOE_TOK
    blk = pl.BlockSpec((None, k, MOE_TOK), lambda i: (i // per_b, 0, i % per_b))
    return pl.pallas_call(
        _plan_kernel,
        grid=(b * per_b,),
        in_specs=[blk, pl.BlockSpec((N_EXPERTS, 1), lambda i: (0, 0))],
        out_specs=blk,
        out_shape=jax.ShapeDtypeStruct((b, k, s), jnp.int32),
        scratch_shapes=[pltpu.VMEM((N_EXPERTS, 1), F32)],
        compiler_params=_cparams(("arbitrary",)),
        name="plan",
    )(topi, seg_off.astype(F32).reshape(N_EXPERTS, 1))


def _row_copy(src_ref, src_row, dst_ref, dst_row, sem):
    src = pl.ds(pl.multiple_of(src_row * PACK_SUBLANES, PACK_SUBLANES), PACK_SUBLANES)
    dst = pl.ds(pl.multiple_of(dst_row * PACK_SUBLANES, PACK_SUBLANES), PACK_SUBLANES)
    return pltpu.make_async_copy(src_ref.at[src, :], dst_ref.at[dst, :], sem)


def _dispatch_kernel(cnt_ref, off_ref, dest_ref, h_ref, xs_ref, zero_ref, sem):
    tok = h_ref.shape[0] // PACK_SUBLANES

    def issue(t, carry):
        for k in range(TOP_K):
            _row_copy(h_ref, t, xs_ref, dest_ref[k, t], sem).start(priority=k % 2)
        return carry

    def drain(t, carry):
        for k in range(TOP_K):
            _row_copy(h_ref, 0, xs_ref, 0, sem).wait()
        return carry

    lax.fori_loop(0, tok, issue, 0)

    @pl.when(pl.program_id(0) == pl.num_programs(0) - 1)
    def _():
        zero_ref[...] = jnp.zeros_like(zero_ref)

        def per_expert(e, carry):
            end = off_ref[e] + cnt_ref[e]
            npad = lax.rem(MOE_TILE - lax.rem(cnt_ref[e], MOE_TILE), MOE_TILE)

            def pad_issue(j, c):
                _row_copy(zero_ref, 0, xs_ref, end + j, sem).start()
                return c

            def pad_drain(j, c):
                _row_copy(zero_ref, 0, xs_ref, 0, sem).wait()
                return c

            lax.fori_loop(0, npad, pad_issue, 0)
            lax.fori_loop(0, npad, pad_drain, 0)
            return carry

        lax.fori_loop(0, N_EXPERTS, per_expert, 0)

    lax.fori_loop(0, tok, drain, 0)


def _dispatch(h2p, dest, cnt, seg_off, n_rows):
    b, k, s = dest.shape
    per_b = s // MOE_TOK
    grid_spec = pltpu.PrefetchScalarGridSpec(
        num_scalar_prefetch=2,
        grid=(b * per_b,),
        in_specs=[pl.BlockSpec((None, k, MOE_TOK), lambda i, c, o: (i // per_b, 0, i % per_b),
                               memory_space=pltpu.SMEM),
                  pl.BlockSpec((MOE_TOK * PACK_SUBLANES, 128), lambda i, c, o: (i, 0))],
        out_specs=pl.BlockSpec(memory_space=pl.ANY),
        scratch_shapes=[pltpu.VMEM((8, 128), U32), pltpu.SemaphoreType.DMA(())],
    )
    return pl.pallas_call(
        _dispatch_kernel,
        grid_spec=grid_spec,
        out_shape=jax.ShapeDtypeStruct((n_rows * PACK_SUBLANES, 128), U32),
        compiler_params=_cparams(("arbitrary",)),
        name="dispatch",
    )(cnt, seg_off, dest, h2p)


def _experts_kernel(te_ref, tb_ref, nt_ref, xs_ref, wg_ref, wu_ref, wd_ref, ys_ref):
    @pl.when(pl.program_id(0) < nt_ref[0])
    def _():
        x = _load_packed(xs_ref, MOE_TILE).astype(BF16)
        act = _silu(_dot(x, wg_ref[...])) * _dot(x, wu_ref[...])
        _store_packed(ys_ref, _dot(act.astype(BF16), wd_ref[...]))


def _experts(xs, tile_expert, tile_block, n_tiles, wg, wu, wd):
    d, f = wg.shape[1], wg.shape[2]
    rows = pl.BlockSpec((MOE_TILE * PACK_SUBLANES, 128), lambda i, te, tb, nt: (tb[i], 0))
    grid_spec = pltpu.PrefetchScalarGridSpec(
        num_scalar_prefetch=3,
        grid=(xs.shape[0] // (MOE_TILE * PACK_SUBLANES),),
        in_specs=[rows,
                  pl.BlockSpec((None, d, f), lambda i, te, tb, nt: (te[i], 0, 0)),
                  pl.BlockSpec((None, d, f), lambda i, te, tb, nt: (te[i], 0, 0)),
                  pl.BlockSpec((None, f, d), lambda i, te, tb, nt: (te[i], 0, 0))],
        out_specs=rows,
    )
    return pl.pallas_call(
        _experts_kernel,
        grid_spec=grid_spec,
        out_shape=jax.ShapeDtypeStruct(xs.shape, U32),
        compiler_params=_cparams(("arbitrary",)),
        name="experts",
    )(tile_expert, tile_block, n_tiles, xs, wg, wu, wd)


def _combine_kernel(dest_ref, topw_ref, h_ref, x1_ref, g2_ref, sg_ref, su_ref, sd_ref, ln2g_ref, ln2b_ref,
                    ys_ref, o_ref, buf_ref, sem, *, alpha):
    tok = x1_ref.shape[0]

    def issue(t, carry):
        for k in range(TOP_K):
            _row_copy(ys_ref, dest_ref[k, t], buf_ref.at[k], t, sem).start(priority=k % 2)
        return carry

    def drain(t, carry):
        for k in range(TOP_K):
            _row_copy(ys_ref, 0, buf_ref.at[0], 0, sem).wait()
        return carry

    lax.fori_loop(0, tok, issue, 0)
    h = _load_packed(h_ref, tok).astype(BF16)
    act = _silu(_dot(h, sg_ref[...])) * _dot(h, su_ref[...])
    y = _dot(act.astype(BF16), sd_ref[...])
    w = topw_ref[...].T
    lax.fori_loop(0, tok, drain, 0)
    for k in range(TOP_K):
        y = y + w[:, k:k + 1] * _load_packed(buf_ref.at[k], tok)
    o_ref[...] = _normalize(alpha * x1_ref[...] + g2_ref[...] * y) * ln2g_ref[...] + ln2b_ref[...]


def _combine(ys, dest, topw, h2p, x1, g2, sg, su, sd, ln2_g, ln2_b, alpha):
    t, d = x1.shape
    b, k, s = dest.shape
    per_b = s // MOE_TOK
    fs = sg.shape[1]
    tk = lambda i: (i // per_b, 0, i % per_b)
    rows = pl.BlockSpec((MOE_TOK, d), lambda i: (i, 0))
    packed = pl.BlockSpec((MOE_TOK * PACK_SUBLANES, 128), lambda i: (i, 0))
    vec = pl.BlockSpec((1, d), lambda i: (0, 0))
    return pl.pallas_call(
        functools.partial(_combine_kernel, alpha=alpha),
        grid=(t // MOE_TOK,),
        in_specs=[pl.BlockSpec((None, k, MOE_TOK), tk, memory_space=pltpu.SMEM),
                  pl.BlockSpec((None, k, MOE_TOK), tk),
                  packed, rows,
                  pl.BlockSpec((None, 1, d), lambda i: (i // per_b, 0, 0)),
                  pl.BlockSpec((d, fs), lambda i: (0, 0)),
                  pl.BlockSpec((d, fs), lambda i: (0, 0)),
                  pl.BlockSpec((fs, d), lambda i: (0, 0)),
                  vec, vec,
                  pl.BlockSpec(memory_space=pl.ANY)],
        out_specs=rows,
        out_shape=jax.ShapeDtypeStruct((t, d), F32),
        scratch_shapes=[pltpu.VMEM((k, MOE_TOK * PACK_SUBLANES, 128), U32), pltpu.SemaphoreType.DMA(())],
        compiler_params=_cparams(("arbitrary",)),
        name="combine",
    )(dest, topw, h2p, x1, g2, sg, su, sd, ln2_g.reshape(1, d), ln2_b.reshape(1, d), ys)


def _moe(h2p, topi, topw, cnt, x1, g2, wg, wu, wd, sg, su, sd, ln2_g, ln2_b, alpha):
    b, s, d = x1.shape
    t = b * s
    cnt = cnt[:, 0].astype(jnp.int32)
    tiles_e = (cnt + (MOE_TILE - 1)) // MOE_TILE
    tiles_cum = jnp.cumsum(tiles_e)
    seg_off = (tiles_cum - tiles_e) * MOE_TILE
    n_tiles_max = t * TOP_K // MOE_TILE + N_EXPERTS
    tile_block = jnp.minimum(jnp.arange(n_tiles_max, dtype=jnp.int32), tiles_cum[-1] - 1)
    tile_expert = jnp.sum((tiles_cum[None, :] <= tile_block[:, None]).astype(jnp.int32), axis=1)
    n_tiles = tiles_cum[-1:].astype(jnp.int32)

    dest = _plan(topi, seg_off)
    xs = _dispatch(h2p, dest, cnt, seg_off.astype(jnp.int32), n_tiles_max * MOE_TILE)
    ys = _experts(xs, tile_expert, tile_block, n_tiles, wg, wu, wd)
    out = _combine(ys, dest, topw, h2p, x1.reshape(t, d), g2, sg, su, sd, ln2_g, ln2_b, alpha)
    return out.reshape(b, s, d)


def kernel(x, c, ctx, c_ctx, w_ada, b_ada, w_in, hg_lb_fwd, hg_lb_bwd, hg_norm_g, na_rpb, w_branch_a, w_branch_b, w_out, ln1_g, ln1_b, w_router, router_bias, w_e_gate, w_e_up, w_e_down, w_sh_gate, w_sh_up, w_sh_down, ln2_g, ln2_b):
    depth = w_ada.shape[0]
    assert depth == 1, "single-layer block"
    b, s, d = x.shape
    alpha = (2.0 * depth) ** 0.25
    l = 0
    lb_fwd = jnp.cumsum(jax.nn.softmax(hg_lb_fwd.astype(F32), axis=0), axis=0)[l]
    lb_bwd = jnp.cumsum(jax.nn.softmax(hg_lb_bwd.astype(F32), axis=0), axis=0)[l]

    cond_rows = jnp.concatenate([c, c_ctx[None, :], jnp.zeros((8 - b - 1, d), F32)], axis=0)
    mod = _ada(cond_rows, w_ada[l], b_ada[l])
    sh1, sc1, g1, sh2, sc2, g2 = [m[:b, None, :] for m in jnp.split(mod, 6, axis=-1)]
    csh1, csc1 = [jnp.broadcast_to(m[b:b + 1, None, :], (b, 1, d)) for m in jnp.split(mod, 6, axis=-1)[:2]]

    w_in_b = w_in[l].astype(BF16)
    p = _inproj(x, sh1, sc1, w_in_b)
    pc = _inproj(ctx, csh1, csc1, w_in_b)

    o_f, o_b = _hgrn(p, pc, lb_fwd, lb_bwd)
    cos, sin, bias = _na_tables(na_rpb[l], s)
    y_na = _natten(p, pc, cos, sin, bias)

    x1, h2, topi, topw, cnt = _merge(o_f, o_b, p, y_na, x, g1, sh2, sc2, hg_norm_g[l], ln1_g[l], ln1_b[l],
                                     w_branch_a[l].astype(BF16), w_branch_b[l].astype(BF16),
                                     w_out[l].astype(BF16), w_router[l].T, router_bias[l], alpha)

    return _moe(h2, topi, topw, cnt, x1, g2,
                w_e_gate[l].astype(BF16), w_e_up[l].astype(BF16), w_e_down[l].astype(BF16),
                w_sh_gate[l].astype(BF16), w_sh_up[l].astype(BF16), w_sh_down[l].astype(BF16),
                ln2_g[l], ln2_b[l], alpha)
```

```python
import functools

import numpy as np
import jax
import jax.numpy as jnp
from jax import lax
from jax.experimental import pallas as pl
from jax.experimental.pallas import tpu as pltpu

F32 = jnp.float32
BF16 = jnp.bfloat16

D_MODEL = 1024
GRID_W = 64
HG_HEADS = 8
HG_DK = 128
HG_CHUNK = 64
NA_HEADS = 16
NA_HD = 64
NA_WIN_R = 8
NA_WIN_C = 16
ROPE_THETA = 10000.0
NEG_INF = -1e30
N_EXPERTS = 64
EXPERT_DIM = 256
TOP_K = 8
N_GROUPS = 8
TOPK_GROUPS = 4
ROUTED_SCALE = 2.5
LN_EPS = 1e-6
N_SECTIONS = 10
SEC_Q, SEC_FF, SEC_FB, SEC_I, SEC_OG, SEC_NQ, SEC_NK, SEC_NV, SEC_GA, SEC_GB = range(10)

VMEM_LIMIT = 56 * 1024 * 1024


def _cparams(sem):
    return pltpu.CompilerParams(dimension_semantics=sem, vmem_limit_bytes=VMEM_LIMIT)


def _normalize(x):
    mu = jnp.mean(x, axis=-1, keepdims=True)
    xc = x - mu
    var = jnp.mean(xc * xc, axis=-1, keepdims=True)
    return xc * lax.rsqrt(var + LN_EPS)


def _silu(x):
    return x * jax.nn.sigmoid(x)


def _dot(a, b):
    return jnp.dot(a, b, preferred_element_type=F32)


def _dot_nt(a, b):
    return lax.dot_general(a, b, (((1,), (1,)), ((), ())), preferred_element_type=F32)


def _dot_tn(a, b):
    return lax.dot_general(a, b, (((0,), (0,)), ((), ())), preferred_element_type=F32)


PACK_SUBLANES = 4
U32 = jnp.uint32


def _store_packed(ref, x):
    n, d = x.shape
    half = d // 2
    lo = lax.bitcast_convert_type(x[:, :half].astype(BF16).astype(F32), U32) >> 16
    hi = lax.bitcast_convert_type(x[:, half:].astype(BF16).astype(F32), U32) & jnp.uint32(0xFFFF0000)
    w = lo | hi
    for c in range(PACK_SUBLANES):
        ref[pl.ds(c, n, stride=PACK_SUBLANES), :] = w[:, c * 128:(c + 1) * 128]


def _load_packed(ref, n):
    w = jnp.concatenate([ref[pl.ds(c, n, stride=PACK_SUBLANES), :] for c in range(PACK_SUBLANES)], axis=-1)
    lo = lax.bitcast_convert_type(w << 16, F32)
    hi = lax.bitcast_convert_type(w & jnp.uint32(0xFFFF0000), F32)
    return jnp.concatenate([lo, hi], axis=-1)


def _split3(x):
    hi = x.astype(BF16)
    r1 = x - hi.astype(F32)
    mid = r1.astype(BF16)
    lo = (r1 - mid.astype(F32)).astype(BF16)
    return hi, mid, lo


def _ada_kernel(c_ref, w_ref, b_ref, o_ref):
    cond = _silu(c_ref[...])
    o_ref[...] = _dot(cond.astype(BF16), w_ref[...].astype(BF16)) + b_ref[...]


def _ada(cond_rows, w_ada, b_ada):
    r, d = cond_rows.shape
    n = w_ada.shape[1]
    tn = 1024
    return pl.pallas_call(
        _ada_kernel,
        grid=(n // tn,),
        in_specs=[pl.BlockSpec((r, d), lambda j: (0, 0)),
                  pl.BlockSpec((d, tn), lambda j: (0, j)),
                  pl.BlockSpec((1, tn), lambda j: (0, j))],
        out_specs=pl.BlockSpec((r, tn), lambda j: (0, j)),
        out_shape=jax.ShapeDtypeStruct((r, n), F32),
        compiler_params=_cparams(("arbitrary",)),
        name="ada",
    )(cond_rows, w_ada, b_ada.reshape(1, n))


def _inproj_kernel(x_ref, sh_ref, sc_ref, w_ref, o_ref, h_ref):
    @pl.when(pl.program_id(2) == 0)
    def _():
        h = _normalize(x_ref[...]) * (1.0 + sc_ref[...]) + sh_ref[...]
        h_ref[...] = h.astype(BF16)

    o_ref[...] = _dot(h_ref[...], w_ref[...])


def _inproj(x, shift, scale, w_in_bf16):
    b, s, d = x.shape
    tm = min(1024, s)
    nj = w_in_bf16.shape[1] // d
    return pl.pallas_call(
        _inproj_kernel,
        grid=(b, s // tm, nj),
        in_specs=[pl.BlockSpec((None, tm, d), lambda bi, i, j: (bi, i, 0)),
                  pl.BlockSpec((None, 1, d), lambda bi, i, j: (bi, 0, 0)),
                  pl.BlockSpec((None, 1, d), lambda bi, i, j: (bi, 0, 0)),
                  pl.BlockSpec((d, d), lambda bi, i, j: (0, j))],
        out_specs=pl.BlockSpec((None, None, tm, d), lambda bi, i, j: (j, bi, i, 0)),
        out_shape=jax.ShapeDtypeStruct((nj, b, s, d), F32),
        scratch_shapes=[pltpu.VMEM((tm, d), BF16)],
        compiler_params=_cparams(("arbitrary", "arbitrary", "arbitrary")),
        name="inproj",
    )(x, shift, scale, w_in_bf16)


def _hgrn_gates(q, fraw, v, lb, tri_bf16, last_row):
    f = lb + (1.0 - lb) * jax.nn.sigmoid(fraw)
    k = 1.0 - f
    lf = jnp.log(f)
    hi, mid, lo = _split3(lf)
    a = _dot(tri_bf16, hi) + _dot(tri_bf16, mid) + _dot(tri_bf16, lo)
    a_last = a[last_row:last_row + 1, :]
    kd = (k * jnp.exp(a_last - a)).astype(BF16)
    decay = jnp.exp(a_last)
    qa = kb = None
    if q is not None:
        qa = (_silu(q) * jnp.exp(a)).astype(BF16)
        kb = (k * jnp.exp(-a)).astype(BF16)
    return qa, kb, kd, v.astype(BF16), decay


def _hgrn_chunks(chunks, st_ref):
    first = []
    for d, ((qa, kb, kd, vb, decay), keep) in enumerate(chunks):
        for h in range(HG_HEADS):
            sl = slice(h * HG_DK, (h + 1) * HG_DK)
            st = st_ref[d, h]
            if qa is not None:
                first.append((_dot_nt(qa[:, sl], kb[:, sl]), _dot_nt(qa[:, sl], st.astype(BF16))))
            st_ref[d, h] = st * decay[:, sl] + _dot_tn(vb[:, sl], kd[:, sl])
    results = []
    for d, ((qa, kb, kd, vb, decay), keep) in enumerate(chunks):
        if qa is None:
            results.append(None)
            continue
        outs = []
        for h in range(HG_HEADS):
            sl = slice(h * HG_DK, (h + 1) * HG_DK)
            s_qk, o_state = first.pop(0)
            outs.append(_dot(jnp.where(keep, s_qk, 0.0).astype(BF16), vb[:, sl]) + o_state)
        results.append(jnp.concatenate(outs, axis=-1))
    return results


def _hgrn_kernel(qf_ref, ff_ref, if_ref, qb_ref, fb_ref, ib_ref, cff_ref, cfb_ref, ci_ref,
                 lbf_ref, lbb_ref, of_ref, ob_ref, st_ref, *, n_sub, n_ctx_sub):
    n = pl.program_id(1)
    c = HG_CHUNK
    row = lax.broadcasted_iota(jnp.int32, (c, c), 0)
    col = lax.broadcasted_iota(jnp.int32, (c, c), 1)
    keep_f = col <= row
    keep_b = col >= row
    tri_f = keep_f.astype(F32).astype(BF16)
    tri_b = keep_b.astype(F32).astype(BF16)
    lbf = lbf_ref[...]
    lbb = lbb_ref[...]

    @pl.when(n == 0)
    def _():
        st_ref[...] = jnp.zeros_like(st_ref)

        def body(i, carry):
            r0 = pl.multiple_of(i * c, c)
            r1 = pl.multiple_of((n_ctx_sub - 1 - i) * c, c)
            gf = _hgrn_gates(None, cff_ref[pl.ds(r0, c), :], ci_ref[pl.ds(r0, c), :], lbf, tri_f, c - 1)
            gb = _hgrn_gates(None, cfb_ref[pl.ds(r1, c), :], ci_ref[pl.ds(r1, c), :], lbb, tri_b, 0)
            _hgrn_chunks([(gf, keep_f), (gb, keep_b)], st_ref)
            return carry

        lax.fori_loop(0, n_ctx_sub, body, 0)

    @pl.when(n > 0)
    def _():
        def body(i, carry):
            r0 = pl.multiple_of(i * c, c)
            r1 = pl.multiple_of((n_sub - 1 - i) * c, c)
            gf = _hgrn_gates(qf_ref[pl.ds(r0, c), :], ff_ref[pl.ds(r0, c), :], if_ref[pl.ds(r0, c), :],
                             lbf, tri_f, c - 1)
            gb = _hgrn_gates(qb_ref[pl.ds(r1, c), :], fb_ref[pl.ds(r1, c), :], ib_ref[pl.ds(r1, c), :],
                             lbb, tri_b, 0)
            o_f, o_b = _hgrn_chunks([(gf, keep_f), (gb, keep_b)], st_ref)
            of_ref[pl.ds(r0, c), :] = o_f
            ob_ref[pl.ds(r1, c), :] = o_b
            return carry

        lax.fori_loop(0, n_sub, body, 0)


def _hgrn(p, pc, lb_fwd, lb_bwd):
    _, b, s, w = p.shape
    ctx_len = pc.shape[2]
    tb = min(256, s)
    nb = s // tb
    fwd = lambda bi, n: jnp.maximum(n - 1, 0)
    bwd = lambda bi, n: nb - 1 - jnp.maximum(n - 1, 0)

    def sec(section, blk):
        return pl.BlockSpec((None, None, tb, w), lambda bi, n: (section, bi, blk(bi, n), 0))

    def csec(section):
        return pl.BlockSpec((None, None, ctx_len, w), lambda bi, n: (section, bi, 0, 0))

    vec = pl.BlockSpec((1, w), lambda bi, n: (0, 0))
    kern = functools.partial(_hgrn_kernel, n_sub=tb // HG_CHUNK, n_ctx_sub=ctx_len // HG_CHUNK)
    return pl.pallas_call(
        kern,
        grid=(b, nb + 1),
        in_specs=[sec(SEC_Q, fwd), sec(SEC_FF, fwd), sec(SEC_I, fwd),
                  sec(SEC_Q, bwd), sec(SEC_FB, bwd), sec(SEC_I, bwd),
                  csec(SEC_FF), csec(SEC_FB), csec(SEC_I), vec, vec],
        out_specs=[pl.BlockSpec((None, tb, w), lambda bi, n: (bi, fwd(bi, n), 0)),
                   pl.BlockSpec((None, tb, w), lambda bi, n: (bi, bwd(bi, n), 0))],
        out_shape=[jax.ShapeDtypeStruct((b, s, w), F32), jax.ShapeDtypeStruct((b, s, w), F32)],
        scratch_shapes=[pltpu.VMEM((2, HG_HEADS, HG_DK, HG_DK), F32)],
        compiler_params=_cparams(("arbitrary", "arbitrary")),
        name="hgrn",
    )(p, p, p, p, p, p, pc, pc, pc, lb_fwd.reshape(1, w), lb_bwd.reshape(1, w))


NA_ROWS_PER_STEP = 8
NA_PREP_ROWS = 512
NA_KEY_TILE = 128
NA_SPAN = (NA_WIN_R + 2) * GRID_W


def _rope(t, cos, sin_signed, first_half):
    w = t.shape[-1]
    partner = jnp.where(first_half, pltpu.roll(t, w - 16, 1), pltpu.roll(t, 16, 1))
    return t * cos + partner * sin_signed


def _fold_lanes(op, *arrays):
    tiles = [a[:, c:c + 128] for a in arrays for c in range(0, a.shape[-1], 128)]
    acc = tiles[0]
    for t in tiles[1:]:
        acc = op(acc, t)
    return acc


def _natten_kernel(q_ref, k_ref, v_ref, kc_ref, vc_ref, cos_ref, sin_ref, t2_ref, o_ref,
                   kt_s, v_s, kc_s, vc_s, bias_s, *, rows):
    rblk = pl.program_id(2)
    hd = NA_HD
    lane = lax.broadcasted_iota(jnp.int32, (1, 2 * hd), 1)
    first_half = (lane % 32) < 16
    scale = NA_HD ** -0.5

    @pl.when(rblk == 0)
    def _():
        kc = kc_ref[...].astype(BF16)
        vc = vc_ref[...].astype(BF16)
        qi = lax.broadcasted_iota(jnp.int32, (GRID_W, GRID_W), 0)
        ki = lax.broadcasted_iota(jnp.int32, (GRID_W, GRID_W), 1)
        cstart = jnp.clip(qi - NA_WIN_C // 2, 0, GRID_W - NA_WIN_C)
        in_win = (ki >= cstart) & (ki < cstart + NA_WIN_C)
        masked = jnp.full((GRID_W, GRID_W), NEG_INF, F32)
        s_len = k_ref.shape[0]
        for h in range(2):
            sl = slice(h * hd, (h + 1) * hd)
            kc_s[h] = kc[:, sl]
            vc_s[h] = vc[:, sl]
            kt_s[h, s_len // NA_KEY_TILE] = jnp.zeros((hd, NA_KEY_TILE), BF16)
            v_s[h, s_len:s_len + NA_KEY_TILE, :] = jnp.zeros((NA_KEY_TILE, hd), BF16)
            tiles = [jnp.where(in_win, t2_ref[h, dr], NEG_INF) for dr in range(2 * NA_WIN_R - 1)]
            for bidx in range(NA_WIN_R + 1):
                v, par = (bidx, 0) if bidx < NA_WIN_R else (NA_WIN_R // 2, 1)
                for piece in range(NA_SPAN // GRID_W):
                    j = piece - par
                    tile = tiles[NA_WIN_R - 1 - v + j] if 0 <= j < NA_WIN_R else masked
                    bias_s[h, bidx, :, piece * GRID_W:(piece + 1) * GRID_W] = tile

        def prep(i, carry):
            r0 = pl.multiple_of(i * NA_PREP_ROWS, NA_PREP_ROWS)
            rws = pl.ds(r0, NA_PREP_ROWS)
            kr = _rope(k_ref[rws, :], cos_ref[rws, :], sin_ref[rws, :], first_half)
            krt = kr.T.astype(BF16)
            vv = v_ref[rws, :].astype(BF16)
            for h in range(2):
                sl = slice(h * hd, (h + 1) * hd)
                for c in range(NA_PREP_ROWS // NA_KEY_TILE):
                    kt_s[h, i * (NA_PREP_ROWS // NA_KEY_TILE) + c] = krt[sl, c * NA_KEY_TILE:(c + 1) * NA_KEY_TILE]
                v_s[h, rws, :] = vv[:, sl]
            return carry

        lax.fori_loop(0, s_len // NA_PREP_ROWS, prep, 0)

    tq = NA_ROWS_PER_STEP * GRID_W
    t0 = pl.multiple_of(rblk * tq, tq)
    q = q_ref[...] * scale
    qr = _rope(q, cos_ref[pl.ds(t0, tq), :], sin_ref[pl.ds(t0, tq), :], first_half)
    qb = q.astype(BF16)
    qrb = qr.astype(BF16)
    rws = [slice(rr * GRID_W, (rr + 1) * GRID_W) for rr in range(NA_ROWS_PER_STEP)]
    tile0, bidx = [], []
    for rr in range(NA_ROWS_PER_STEP):
        r = rblk * NA_ROWS_PER_STEP + rr
        rs = jnp.clip(r - NA_WIN_R // 2, 0, rows - NA_WIN_R)
        tile0.append(lax.shift_right_logical(rs, 1))
        bidx.append(jnp.where((rs & 1) == 1, NA_WIN_R, r - rs))

    def scores(h):
        sl = slice(h * hd, (h + 1) * hd)
        qrb_h = qrb[:, sl]
        s_ctx_all = _dot_nt(qb[:, sl], kc_s[h])
        s_win = []
        for rr in range(NA_ROWS_PER_STEP):
            kt = kt_s[h, pl.ds(tile0[rr], NA_SPAN // NA_KEY_TILE)]
            kt = jnp.concatenate([kt[c] for c in range(NA_SPAN // NA_KEY_TILE)], axis=-1)
            s_win.append(_dot(qrb_h[rws[rr]], kt))
        return s_win, s_ctx_all

    def softmax(h, s_win, s_ctx_all):
        e_win, e_ctx, rinv = [], [], []
        for rr in range(NA_ROWS_PER_STEP):
            sw = s_win[rr] + bias_s[h, bidx[rr]]
            sc = s_ctx_all[rws[rr]]
            m = jnp.max(_fold_lanes(jnp.maximum, sw, sc), axis=-1, keepdims=True)
            ew = jnp.exp(sw - m)
            ec = jnp.exp(sc - m)
            rinv.append(1.0 / jnp.sum(_fold_lanes(jnp.add, ew, ec), axis=-1, keepdims=True))
            e_win.append(ew.astype(BF16))
            e_ctx.append(ec.astype(BF16))
        return e_win, e_ctx, rinv

    def values(h, e_win, e_ctx, rinv):
        o_win = []
        for rr in range(NA_ROWS_PER_STEP):
            k0 = pl.multiple_of(tile0[rr] * NA_KEY_TILE, NA_KEY_TILE)
            o_win.append(_dot(e_win[rr], v_s[h, pl.ds(k0, NA_SPAN), :]))
        o_ctx = _dot(jnp.concatenate(e_ctx, axis=0), vc_s[h])
        return (jnp.concatenate(o_win, axis=0) + o_ctx) * jnp.concatenate(rinv, axis=0)

    s0 = scores(0)
    s1 = scores(1)
    p0 = softmax(0, *s0)
    o0 = values(0, *p0)
    p1 = softmax(1, *s1)
    o1 = values(1, *p1)
    o_ref[...] = jnp.concatenate([o0, o1], axis=-1)


def _na_tables(rpb, s):
    half = NA_HD // 2
    pos = jnp.arange(s)
    rowp = (pos // GRID_W).astype(F32)
    colp = (pos % GRID_W).astype(F32)
    inv = jnp.power(ROPE_THETA, -jnp.arange(0, half, 2, dtype=F32) / half)
    ang_r = rowp[:, None] * inv[None, :]
    ang_c = colp[:, None] * inv[None, :]
    cos = jnp.concatenate([jnp.cos(ang_r), jnp.cos(ang_r), jnp.cos(ang_c), jnp.cos(ang_c)], axis=-1)
    sin = jnp.concatenate([-jnp.sin(ang_r), jnp.sin(ang_r), -jnp.sin(ang_c), jnp.sin(ang_c)], axis=-1)
    cos = jnp.concatenate([cos, cos], axis=-1)
    sin = jnp.concatenate([sin, sin], axis=-1)

    pad = GRID_W - NA_WIN_C
    rp = jnp.pad(rpb.astype(F32), ((0, 0), (0, 0), (pad, pad)), mode="edge")
    t2 = jnp.stack([rp[:, :, GRID_W - 1 - qc:2 * GRID_W - 1 - qc] for qc in range(GRID_W)], axis=2)
    return cos, sin, t2


def _natten(p, pc, cos, sin, t2):
    _, b, s, w = p.shape
    ctx_len = pc.shape[2]
    rows = s // GRID_W
    assert rows >= NA_WIN_R and rows % NA_ROWS_PER_STEP == 0
    tq = NA_ROWS_PER_STEP * GRID_W
    hw = 2 * NA_HD
    nhp = w // hw
    kern = functools.partial(_natten_kernel, rows=rows)
    return pl.pallas_call(
        kern,
        grid=(b, nhp, rows // NA_ROWS_PER_STEP),
        in_specs=[pl.BlockSpec((None, None, tq, hw), lambda bi, hp, r: (SEC_NQ, bi, r, hp)),
                  pl.BlockSpec((None, None, s, hw), lambda bi, hp, r: (SEC_NK, bi, 0, hp)),
                  pl.BlockSpec((None, None, s, hw), lambda bi, hp, r: (SEC_NV, bi, 0, hp)),
                  pl.BlockSpec((None, None, ctx_len, hw), lambda bi, hp, r: (SEC_NK, bi, 0, hp)),
                  pl.BlockSpec((None, None, ctx_len, hw), lambda bi, hp, r: (SEC_NV, bi, 0, hp)),
                  pl.BlockSpec((s, hw), lambda bi, hp, r: (0, 0)),
                  pl.BlockSpec((s, hw), lambda bi, hp, r: (0, 0)),
                  pl.BlockSpec((2, 2 * NA_WIN_R - 1, GRID_W, GRID_W), lambda bi, hp, r: (hp, 0, 0, 0))],
        out_specs=pl.BlockSpec((None, tq, hw), lambda bi, hp, r: (bi, r, hp)),
        out_shape=jax.ShapeDtypeStruct((b, s, w), F32),
        scratch_shapes=[pltpu.VMEM((2, s // NA_KEY_TILE + 1, NA_HD, NA_KEY_TILE), BF16),
                        pltpu.VMEM((2, s + NA_KEY_TILE, NA_HD), BF16),
                        pltpu.VMEM((2, ctx_len, NA_HD), BF16), pltpu.VMEM((2, ctx_len, NA_HD), BF16),
                        pltpu.VMEM((2, NA_WIN_R + 1, GRID_W, NA_SPAN), F32)],
        compiler_params=_cparams(("arbitrary", "arbitrary", "arbitrary")),
        name="natten",
    )(p, p, p, pc, pc, cos, sin, t2)


def _route(logits_t, rbias):
    e, t = logits_t.shape
    gsz = e // N_GROUPS
    scores = jax.nn.sigmoid(logits_t)
    sel = scores + rbias
    neg = -jnp.inf
    sub = lax.broadcasted_iota(jnp.int32, (gsz, t), 0).astype(F32)
    gscore = []
    for g in range(N_GROUPS):
        grp = sel[g * gsz:(g + 1) * gsz, :]
        m1 = jnp.max(grp, axis=0, keepdims=True)
        first = jnp.min(jnp.where(grp == m1, sub, float(gsz)), axis=0, keepdims=True)
        m2 = jnp.max(jnp.where(sub == first, neg, grp), axis=0, keepdims=True)
        gscore.append(m1 + m2)
    masked = []
    for g in range(N_GROUPS):
        rank = jnp.zeros((1, t), F32)
        for g2 in range(N_GROUPS):
            if g2 == g:
                continue
            if g2 < g:
                ahead = gscore[g2] >= gscore[g]
            else:
                ahead = gscore[g2] > gscore[g]
            rank = rank + jnp.where(ahead, 1.0, 0.0)
        masked.append(jnp.where(rank < TOPK_GROUPS, sel[g * gsz:(g + 1) * gsz, :], neg))
    work = jnp.concatenate(masked, axis=0)
    eidx = lax.broadcasted_iota(jnp.int32, (e, t), 0).astype(F32)
    idxs, ws = [], []
    chosen = jnp.zeros((e, t), F32)
    for _ in range(TOP_K):
        m = jnp.max(work, axis=0, keepdims=True)
        first = jnp.min(jnp.where(work == m, eidx, float(e)), axis=0, keepdims=True)
        pick = eidx == first
        idxs.append(first)
        ws.append(jnp.sum(jnp.where(pick, scores, 0.0), axis=0, keepdims=True))
        chosen = jnp.where(pick, 1.0, chosen)
        work = jnp.where(pick, neg, work)
    w = jnp.concatenate(ws, axis=0)
    w = w / jnp.sum(w, axis=0, keepdims=True) * ROUTED_SCALE
    return jnp.concatenate(idxs, axis=0).astype(jnp.int32), w, chosen


MERGE_TOK = 512
MERGE_SUB = 256


def _merge_kernel(of_ref, ob_ref, og_ref, yna_ref, ga_ref, gb_ref, x_ref, g1_ref, sh2_ref, sc2_ref,
                  hgg_ref, ln1g_ref, ln1b_ref, wa_ref, wb_ref, wo_ref, wr_ref, rb_ref,
                  x1_ref, h2_ref, topi_ref, topw_ref, cnt_ref, *, alpha):
    tm = x_ref.shape[0]
    subs = [slice(i * MERGE_SUB, (i + 1) * MERGE_SUB) for i in range(tm // MERGE_SUB)]

    def branches(rows):
        o = of_ref[rows, :] + ob_ref[rows, :]
        parts = []
        for h in range(HG_HEADS):
            oh = o[:, h * HG_DK:(h + 1) * HG_DK]
            parts.append(oh * lax.rsqrt(jnp.mean(oh * oh, axis=-1, keepdims=True) + LN_EPS))
        y_hg = jnp.concatenate(parts, axis=-1) * hgg_ref[...] * _silu(og_ref[rows, :])
        return _dot(y_hg.astype(BF16), wa_ref[...]), _dot(yna_ref[rows, :].astype(BF16), wb_ref[...])

    def out_proj(rows, ya, yb):
        t = jax.nn.sigmoid(ga_ref[rows, :]) * ya + jax.nn.sigmoid(gb_ref[rows, :]) * yb
        return _dot(t.astype(BF16), wo_ref[...])

    def norms_router(rows, i, y):
        x1 = _normalize(alpha * x_ref[rows, :] + g1_ref[...] * y) * ln1g_ref[...] + ln1b_ref[...]
        x1_ref[rows, :] = x1
        h2 = _normalize(x1) * (1.0 + sc2_ref[...]) + sh2_ref[...]
        _store_packed(h2_ref.at[pl.ds(i * MERGE_SUB * PACK_SUBLANES, MERGE_SUB * PACK_SUBLANES), :], h2)
        hh, hm, hl = _split3(h2)
        wh, wm, wl = _split3(wr_ref[...])
        return (_dot_nt(wh, hh) + _dot_nt(wh, hm) + _dot_nt(wm, hh)
                + _dot_nt(wh, hl) + _dot_nt(wl, hh) + _dot_nt(wm, hm))

    ab = [branches(rows) for rows in subs]
    ys = [out_proj(rows, *ab[i]) for i, rows in enumerate(subs)]
    logits = [norms_router(rows, i, ys[i]) for i, rows in enumerate(subs)]

    @pl.when((pl.program_id(0) == 0) & (pl.program_id(1) == 0))
    def _():
        cnt_ref[...] = jnp.zeros_like(cnt_ref)

    for i, rows in enumerate(subs):
        topi, topw, chosen = _route(logits[i], rb_ref[...])
        topi_ref[:, rows] = topi
        topw_ref[:, rows] = topw
        cnt_ref[...] += jnp.sum(chosen, axis=1, keepdims=True)


def _merge(o_f, o_b, p, y_na, x, g1, sh2, sc2, hg_norm_g, ln1_g, ln1_b, w_a, w_b, w_o, w_router_t, router_bias,
           alpha):
    b, s, d = x.shape
    tm = min(MERGE_TOK, s)
    e = w_router_t.shape[0]
    tok = lambda bi, i: (bi, i, 0)
    blk = pl.BlockSpec((None, tm, d), tok)

    def sec(section):
        return pl.BlockSpec((None, None, tm, d), lambda bi, i: (section, bi, i, 0))

    mod = pl.BlockSpec((None, 1, d), lambda bi, i: (bi, 0, 0))
    vec = pl.BlockSpec((1, d), lambda bi, i: (0, 0))
    mat = pl.BlockSpec((d, d), lambda bi, i: (0, 0), pipeline_mode=pl.Buffered(1))
    return pl.pallas_call(
        functools.partial(_merge_kernel, alpha=alpha),
        grid=(b, s // tm),
        in_specs=[blk, blk, sec(SEC_OG), blk, sec(SEC_GA), sec(SEC_GB), blk, mod, mod, mod,
                  vec, vec, vec, mat, mat, mat,
                  pl.BlockSpec((e, d), lambda bi, i: (0, 0)),
                  pl.BlockSpec((e, 1), lambda bi, i: (0, 0))],
        out_specs=[blk,
                   pl.BlockSpec((tm * PACK_SUBLANES, 128), lambda bi, i: (bi * (s // tm) + i, 0)),
                   pl.BlockSpec((None, TOP_K, tm), lambda bi, i: (bi, 0, i)),
                   pl.BlockSpec((None, TOP_K, tm), lambda bi, i: (bi, 0, i)),
                   pl.BlockSpec((e, 128), lambda bi, i: (0, 0))],
        out_shape=[jax.ShapeDtypeStruct((b, s, d), F32),
                   jax.ShapeDtypeStruct((b * s * PACK_SUBLANES, 128), U32),
                   jax.ShapeDtypeStruct((b, TOP_K, s), jnp.int32), jax.ShapeDtypeStruct((b, TOP_K, s), F32),
                   jax.ShapeDtypeStruct((e, 128), F32)],
        compiler_params=_cparams(("arbitrary", "arbitrary")),
        name="merge",
    )(o_f, o_b, p, y_na, p, p, x, g1, sh2, sc2, hg_norm_g.reshape(1, d), ln1_g.reshape(1, d),
      ln1_b.reshape(1, d), w_a, w_b, w_o, w_router_t, router_bias.reshape(e, 1))


MOE_TILE = 256
MOE_TOK = 256


def _plan_kernel(topi_ref, off_ref, dest_ref, carry_ref):
    @pl.when(pl.program_id(0) == 0)
    def _():
        carry_ref[...] = jnp.zeros_like(carry_ref)

    topi = topi_ref[...]
    tok = topi.shape[1]
    eidx = lax.broadcasted_iota(jnp.int32, (N_EXPERTS, tok), 0)
    hits = [eidx == topi[k:k + 1, :] for k in range(TOP_K)]
    m = jnp.zeros((N_EXPERTS, tok), F32)
    for hit in hits:
        m = jnp.where(hit, 1.0, m)
    before = (lax.broadcasted_iota(jnp.int32, (tok, tok), 0)
              < lax.broadcasted_iota(jnp.int32, (tok, tok), 1)).astype(F32).astype(BF16)
    row = off_ref[...] + carry_ref[...] + _dot(m.astype(BF16), before)
    dest = [jnp.sum(jnp.where(hit, row, 0.0), axis=0, keepdims=True) for hit in hits]
    dest_ref[...] = jnp.concatenate(dest, axis=0).astype(jnp.int32)
    carry_ref[...] += jnp.sum(m, axis=1, keepdims=True)


def _plan(topi, seg_off):
    b, k, s = topi.shape
    per_b = s // MOE_TOK
    blk = pl.BlockSpec((None, k, MOE_TOK), lambda i: (i // per_b, 0, i % per_b))
    return pl.pallas_call(
        _plan_kernel,
        grid=(b * per_b,),
        in_specs=[blk, pl.BlockSpec((N_EXPERTS, 1), lambda i: (0, 0))],
        out_specs=blk,
        out_shape=jax.ShapeDtypeStruct((b, k, s), jnp.int32),
        scratch_shapes=[pltpu.VMEM((N_EXPERTS, 1), F32)],
        compiler_params=_cparams(("arbitrary",)),
        name="plan",
    )(topi, seg_off.astype(F32).reshape(N_EXPERTS, 1))


def _row_copy(src_ref, src_row, dst_ref, dst_row, sem):
    src = pl.ds(pl.multiple_of(src_row * PACK_SUBLANES, PACK_SUBLANES), PACK_SUBLANES)
    dst = pl.ds(pl.multiple_of(dst_row * PACK_SUBLANES, PACK_SUBLANES), PACK_SUBLANES)
    return pltpu.make_async_copy(src_ref.at[src, :], dst_ref.at[dst, :], sem)


def _dispatch_kernel(cnt_ref, off_ref, dest_ref, h_ref, xs_ref, zero_ref, sem):
    tok = h_ref.shape[0] // PACK_SUBLANES

    def issue(t, carry):
        for k in range(TOP_K):
            _row_copy(h_ref, t, xs_ref, dest_ref[k, t], sem).start(priority=k % 2)
        return carry

    def drain(t, carry):
        for k in range(TOP_K):
            _row_copy(h_ref, 0, xs_ref, 0, sem).wait()
        return carry

    lax.fori_loop(0, tok, issue, 0)

    @pl.when(pl.program_id(0) == pl.num_programs(0) - 1)
    def _():
        zero_ref[...] = jnp.zeros_like(zero_ref)

        def per_expert(e, carry):
            end = off_ref[e] + cnt_ref[e]
            npad = lax.rem(MOE_TILE - lax.rem(cnt_ref[e], MOE_TILE), MOE_TILE)

            def pad_issue(j, c):
                _row_copy(zero_ref, 0, xs_ref, end + j, sem).start()
                return c

            def pad_drain(j, c):
                _row_copy(zero_ref, 0, xs_ref, 0, sem).wait()
                return c

            lax.fori_loop(0, npad, pad_issue, 0)
            lax.fori_loop(0, npad, pad_drain, 0)
            return carry

        lax.fori_loop(0, N_EXPERTS, per_expert, 0)

    lax.fori_loop(0, tok, drain, 0)


def _dispatch(h2p, dest, cnt, seg_off, n_rows):
    b, k, s = dest.shape
    per_b = s // MOE_TOK
    grid_spec = pltpu.PrefetchScalarGridSpec(
        num_scalar_prefetch=2,
        grid=(b * per_b,),
        in_specs=[pl.BlockSpec((None, k, MOE_TOK), lambda i, c, o: (i // per_b, 0, i % per_b),
                               memory_space=pltpu.SMEM),
                  pl.BlockSpec((MOE_TOK * PACK_SUBLANES, 128), lambda i, c, o: (i, 0))],
        out_specs=pl.BlockSpec(memory_space=pl.ANY),
        scratch_shapes=[pltpu.VMEM((8, 128), U32), pltpu.SemaphoreType.DMA(())],
    )
    return pl.pallas_call(
        _dispatch_kernel,
        grid_spec=grid_spec,
        out_shape=jax.ShapeDtypeStruct((n_rows * PACK_SUBLANES, 128), U32),
        compiler_params=_cparams(("arbitrary",)),
        name="dispatch",
    )(cnt, seg_off, dest, h2p)


def _experts_kernel(te_ref, tb_ref, nt_ref, xs_ref, wg_ref, wu_ref, wd_ref, ys_ref):
    @pl.when(pl.program_id(0) < nt_ref[0])
    def _():
        x = _load_packed(xs_ref, MOE_TILE).astype(BF16)
        act = _silu(_dot(x, wg_ref[...])) * _dot(x, wu_ref[...])
        _store_packed(ys_ref, _dot(act.astype(BF16), wd_ref[...]))


def _experts(xs, tile_expert, tile_block, n_tiles, wg, wu, wd):
    d, f = wg.shape[1], wg.shape[2]
    rows = pl.BlockSpec((MOE_TILE * PACK_SUBLANES, 128), lambda i, te, tb, nt: (tb[i], 0))
    grid_spec = pltpu.PrefetchScalarGridSpec(
        num_scalar_prefetch=3,
        grid=(xs.shape[0] // (MOE_TILE * PACK_SUBLANES),),
        in_specs=[rows,
                  pl.BlockSpec((None, d, f), lambda i, te, tb, nt: (te[i], 0, 0)),
                  pl.BlockSpec((None, d, f), lambda i, te, tb, nt: (te[i], 0, 0)),
                  pl.BlockSpec((None, f, d), lambda i, te, tb, nt: (te[i], 0, 0))],
        out_specs=rows,
    )
    return pl.pallas_call(
        _experts_kernel,
        grid_spec=grid_spec,
        out_shape=jax.ShapeDtypeStruct(xs.shape, U32),
        compiler_params=_cparams(("arbitrary",)),
        name="experts",
    )(tile_expert, tile_block, n_tiles, xs, wg, wu, wd)


def _combine_kernel(dest_ref, topw_ref, h_ref, x1_ref, g2_ref, sg_ref, su_ref, sd_ref, ln2g_ref, ln2b_ref,
                    ys_ref, o_ref, buf_ref, sem, *, alpha):
    tok = x1_ref.shape[0]

    def issue(t, carry):
        for k in range(TOP_K):
            _row_copy(ys_ref, dest_ref[k, t], buf_ref.at[k], t, sem).start(priority=k % 2)
        return carry

    def drain(t, carry):
        for k in range(TOP_K):
            _row_copy(ys_ref, 0, buf_ref.at[0], 0, sem).wait()
        return carry

    lax.fori_loop(0, tok, issue, 0)
    h = _load_packed(h_ref, tok).astype(BF16)
    act = _silu(_dot(h, sg_ref[...])) * _dot(h, su_ref[...])
    y = _dot(act.astype(BF16), sd_ref[...])
    w = topw_ref[...].T
    lax.fori_loop(0, tok, drain, 0)
    for k in range(TOP_K):
        y = y + w[:, k:k + 1] * _load_packed(buf_ref.at[k], tok)
    o_ref[...] = _normalize(alpha * x1_ref[...] + g2_ref[...] * y) * ln2g_ref[...] + ln2b_ref[...]


def _combine(ys, dest, topw, h2p, x1, g2, sg, su, sd, ln2_g, ln2_b, alpha):
    t, d = x1.shape
    b, k, s = dest.shape
    per_b = s // MOE_TOK
    fs = sg.shape[1]
    tk = lambda i: (i // per_b, 0, i % per_b)
    rows = pl.BlockSpec((MOE_TOK, d), lambda i: (i, 0))
    packed = pl.BlockSpec((MOE_TOK * PACK_SUBLANES, 128), lambda i: (i, 0))
    vec = pl.BlockSpec((1, d), lambda i: (0, 0))
    return pl.pallas_call(
        functools.partial(_combine_kernel, alpha=alpha),
        grid=(t // MOE_TOK,),
        in_specs=[pl.BlockSpec((None, k, MOE_TOK), tk, memory_space=pltpu.SMEM),
                  pl.BlockSpec((None, k, MOE_TOK), tk),
                  packed, rows,
                  pl.BlockSpec((None, 1, d), lambda i: (i // per_b, 0, 0)),
                  pl.BlockSpec((d, fs), lambda i: (0, 0)),
                  pl.BlockSpec((d, fs), lambda i: (0, 0)),
                  pl.BlockSpec((fs, d), lambda i: (0, 0)),
                  vec, vec,
                  pl.BlockSpec(memory_space=pl.ANY)],
        out_specs=rows,
        out_shape=jax.ShapeDtypeStruct((t, d), F32),
        scratch_shapes=[pltpu.VMEM((k, MOE_TOK * PACK_SUBLANES, 128), U32), pltpu.SemaphoreType.DMA(())],
        compiler_params=_cparams(("arbitrary",)),
        name="combine",
    )(dest, topw, h2p, x1, g2, sg, su, sd, ln2_g.reshape(1, d), ln2_b.reshape(1, d), ys)


def _moe(h2p, topi, topw, cnt, x1, g2, wg, wu, wd, sg, su, sd, ln2_g, ln2_b, alpha):
    b, s, d = x1.shape
    t = b * s
    cnt = cnt[:, 0].astype(jnp.int32)
    tiles_e = (cnt + (MOE_TILE - 1)) // MOE_TILE
    tiles_cum = jnp.cumsum(tiles_e)
    seg_off = (tiles_cum - tiles_e) * MOE_TILE
    n_tiles_max = t * TOP_K // MOE_TILE + N_EXPERTS
    tile_block = jnp.minimum(jnp.arange(n_tiles_max, dtype=jnp.int32), tiles_cum[-1] - 1)
    tile_expert = jnp.sum((tiles_cum[None, :] <= tile_block[:, None]).astype(jnp.int32), axis=1)
    n_tiles = tiles_cum[-1:].astype(jnp.int32)

    dest = _plan(topi, seg_off)
    xs = _dispatch(h2p, dest, cnt, seg_off.astype(jnp.int32), n_tiles_max * MOE_TILE)
    ys = _experts(xs, tile_expert, tile_block, n_tiles, wg, wu, wd)
    out = _combine(ys, dest, topw, h2p, x1.reshape(t, d), g2, sg, su, sd, ln2_g, ln2_b, alpha)
    return out.reshape(b, s, d)


def kernel(x, c, ctx, c_ctx, w_ada, b_ada, w_in, hg_lb_fwd, hg_lb_bwd, hg_norm_g, na_rpb, w_branch_a, w_branch_b, w_out, ln1_g, ln1_b, w_router, router_bias, w_e_gate, w_e_up, w_e_down, w_sh_gate, w_sh_up, w_sh_down, ln2_g, ln2_b):
    depth = w_ada.shape[0]
    assert depth == 1, "single-layer block"
    b, s, d = x.shape
    alpha = (2.0 * depth) ** 0.25
    l = 0
    lb_fwd = jnp.cumsum(jax.nn.softmax(hg_lb_fwd.astype(F32), axis=0), axis=0)[l]
    lb_bwd = jnp.cumsum(jax.nn.softmax(hg_lb_bwd.astype(F32), axis=0), axis=0)[l]

    cond_rows = jnp.concatenate([c, c_ctx[None, :], jnp.zeros((8 - b - 1, d), F32)], axis=0)
    mod = _ada(cond_rows, w_ada[l], b_ada[l])
    sh1, sc1, g1, sh2, sc2, g2 = [m[:b, None, :] for m in jnp.split(mod, 6, axis=-1)]
    csh1, csc1 = [jnp.broadcast_to(m[b:b + 1, None, :], (b, 1, d)) for m in jnp.split(mod, 6, axis=-1)[:2]]

    w_in_b = w_in[l].astype(BF16)
    p = _inproj(x, sh1, sc1, w_in_b)
    pc = _inproj(ctx, csh1, csc1, w_in_b)

    o_f, o_b = _hgrn(p, pc, lb_fwd, lb_bwd)
    cos, sin, bias = _na_tables(na_rpb[l], s)
    y_na = _natten(p, pc, cos, sin, bias)

    x1, h2, topi, topw, cnt = _merge(o_f, o_b, p, y_na, x, g1, sh2, sc2, hg_norm_g[l], ln1_g[l], ln1_b[l],
                                     w_branch_a[l].astype(BF16), w_branch_b[l].astype(BF16),
                                     w_out[l].astype(BF16), w_router[l].T, router_bias[l], alpha)

    return _moe(h2, topi, topw, cnt, x1, g2,
                w_e_gate[l].astype(BF16), w_e_up[l].astype(BF16), w_e_down[l].astype(BF16),
                w_sh_gate[l].astype(BF16), w_sh_up[l].astype(BF16), w_sh_down[l].astype(BF16),
                ln2_g[l], ln2_b[l], alpha)
```

```python
import functools

import numpy as np
import jax
import jax.numpy as jnp
from jax import lax
from jax.experimental import pallas as pl
from jax.experimental.pallas import tpu as pltpu

F32 = jnp.float32
BF16 = jnp.bfloat16

D_MODEL = 1024
GRID_W = 64
HG_HEADS = 8
HG_DK = 128
HG_CHUNK = 64
NA_HEADS = 16
NA_HD = 64
NA_WIN_R = 8
NA_WIN_C = 16
ROPE_THETA = 10000.0
NEG_INF = -1e30
N_EXPERTS = 64
EXPERT_DIM = 256
TOP_K = 8
N_GROUPS = 8
TOPK_GROUPS = 4
ROUTED_SCALE = 2.5
LN_EPS = 1e-6
N_SECTIONS = 10
SEC_Q, SEC_FF, SEC_FB, SEC_I, SEC_OG, SEC_NQ, SEC_NK, SEC_NV, SEC_GA, SEC_GB = range(10)

VMEM_LIMIT = 56 * 1024 * 1024


def _cparams(sem):
    return pltpu.CompilerParams(dimension_semantics=sem, vmem_limit_bytes=VMEM_LIMIT)


def _normalize(x):
    mu = jnp.mean(x, axis=-1, keepdims=True)
    xc = x - mu
    var = jnp.mean(xc * xc, axis=-1, keepdims=True)
    return xc * lax.rsqrt(var + LN_EPS)


def _silu(x):
    return x * jax.nn.sigmoid(x)


def _dot(a, b):
    return jnp.dot(a, b, preferred_element_type=F32)


def _dot_nt(a, b):
    return lax.dot_general(a, b, (((1,), (1,)), ((), ())), preferred_element_type=F32)


def _dot_tn(a, b):
    return lax.dot_general(a, b, (((0,), (0,)), ((), ())), preferred_element_type=F32)


PACK_SUBLANES = 4
U32 = jnp.uint32


def _store_packed(ref, x):
    n, d = x.shape
    half = d // 2
    lo = lax.bitcast_convert_type(x[:, :half].astype(BF16).astype(F32), U32) >> 16
    hi = lax.bitcast_convert_type(x[:, half:].astype(BF16).astype(F32), U32) & jnp.uint32(0xFFFF0000)
    w = lo | hi
    for c in range(PACK_SUBLANES):
        ref[pl.ds(c, n, stride=PACK_SUBLANES), :] = w[:, c * 128:(c + 1) * 128]


def _load_packed(ref, n):
    w = jnp.concatenate([ref[pl.ds(c, n, stride=PACK_SUBLANES), :] for c in range(PACK_SUBLANES)], axis=-1)
    lo = lax.bitcast_convert_type(w << 16, F32)
    hi = lax.bitcast_convert_type(w & jnp.uint32(0xFFFF0000), F32)
    return jnp.concatenate([lo, hi], axis=-1)


def _split3(x):
    hi = x.astype(BF16)
    r1 = x - hi.astype(F32)
    mid = r1.astype(BF16)
    lo = (r1 - mid.astype(F32)).astype(BF16)
    return hi, mid, lo


def _ada_kernel(c_ref, w_ref, b_ref, o_ref):
    cond = _silu(c_ref[...])
    o_ref[...] = _dot(cond.astype(BF16), w_ref[...].astype(BF16)) + b_ref[...]


def _ada(cond_rows, w_ada, b_ada):
    r, d = cond_rows.shape
    n = w_ada.shape[1]
    tn = 1024
    return pl.pallas_call(
        _ada_kernel,
        grid=(n // tn,),
        in_specs=[pl.BlockSpec((r, d), lambda j: (0, 0)),
                  pl.BlockSpec((d, tn), lambda j: (0, j)),
                  pl.BlockSpec((1, tn), lambda j: (0, j))],
        out_specs=pl.BlockSpec((r, tn), lambda j: (0, j)),
        out_shape=jax.ShapeDtypeStruct((r, n), F32),
        compiler_params=_cparams(("arbitrary",)),
        name="ada",
    )(cond_rows, w_ada, b_ada.reshape(1, n))


def _inproj_kernel(x_ref, sh_ref, sc_ref, w_ref, o_ref, h_ref):
    @pl.when(pl.program_id(2) == 0)
    def _():
        h = _normalize(x_ref[...]) * (1.0 + sc_ref[...]) + sh_ref[...]
        h_ref[...] = h.astype(BF16)

    o_ref[...] = _dot(h_ref[...], w_ref[...].astype(BF16))


def _inproj(x, shift, scale, w_in_bf16):
    b, s, d = x.shape
    tm = min(1024, s)
    nj = w_in_bf16.shape[1] // d
    return pl.pallas_call(
        _inproj_kernel,
        grid=(b, s // tm, nj),
        in_specs=[pl.BlockSpec((None, tm, d), lambda bi, i, j: (bi, i, 0)),
                  pl.BlockSpec((None, 1, d), lambda bi, i, j: (bi, 0, 0)),
                  pl.BlockSpec((None, 1, d), lambda bi, i, j: (bi, 0, 0)),
                  pl.BlockSpec((d, d), lambda bi, i, j: (0, j))],
        out_specs=pl.BlockSpec((None, None, tm, d), lambda bi, i, j: (j, bi, i, 0)),
        out_shape=jax.ShapeDtypeStruct((nj, b, s, d), F32),
        scratch_shapes=[pltpu.VMEM((tm, d), BF16)],
        compiler_params=_cparams(("arbitrary", "arbitrary", "arbitrary")),
        name="inproj",
    )(x, shift, scale, w_in_bf16)


def _hgrn_gates(q, fraw, v, lb, tri_bf16, last_row):
    f = lb + (1.0 - lb) * jax.nn.sigmoid(fraw)
    k = 1.0 - f
    lf = jnp.log(f)
    hi, mid, lo = _split3(lf)
    a = _dot(tri_bf16, hi) + _dot(tri_bf16, mid) + _dot(tri_bf16, lo)
    a_last = a[last_row:last_row + 1, :]
    kd = (k * jnp.exp(a_last - a)).astype(BF16)
    decay = jnp.exp(a_last)
    qa = kb = None
    if q is not None:
        qa = (_silu(q) * jnp.exp(a)).astype(BF16)
        kb = (k * jnp.exp(-a)).astype(BF16)
    return qa, kb, kd, v.astype(BF16), decay


def _hgrn_chunks(chunks, st_ref):
    first = []
    for d, ((qa, kb, kd, vb, decay), keep) in enumerate(chunks):
        for h in range(HG_HEADS):
            sl = slice(h * HG_DK, (h + 1) * HG_DK)
            st = st_ref[d, h]
            if qa is not None:
                first.append((_dot_nt(qa[:, sl], kb[:, sl]), _dot_nt(qa[:, sl], st.astype(BF16))))
            st_ref[d, h] = st * decay[:, sl] + _dot_tn(vb[:, sl], kd[:, sl])
    results = []
    for d, ((qa, kb, kd, vb, decay), keep) in enumerate(chunks):
        if qa is None:
            results.append(None)
            continue
        outs = []
        for h in range(HG_HEADS):
            sl = slice(h * HG_DK, (h + 1) * HG_DK)
            s_qk, o_state = first.pop(0)
            outs.append(_dot(jnp.where(keep, s_qk, 0.0).astype(BF16), vb[:, sl]) + o_state)
        results.append(jnp.concatenate(outs, axis=-1))
    return results


def _hgrn_kernel(qf_ref, ff_ref, if_ref, qb_ref, fb_ref, ib_ref, cff_ref, cfb_ref, ci_ref,
                 lbf_ref, lbb_ref, of_ref, ob_ref, st_ref, *, n_sub, n_ctx_sub):
    n = pl.program_id(1)
    c = HG_CHUNK
    row = lax.broadcasted_iota(jnp.int32, (c, c), 0)
    col = lax.broadcasted_iota(jnp.int32, (c, c), 1)
    keep_f = col <= row
    keep_b = col >= row
    tri_f = keep_f.astype(F32).astype(BF16)
    tri_b = keep_b.astype(F32).astype(BF16)
    lbf = lbf_ref[...]
    lbb = lbb_ref[...]

    @pl.when(n == 0)
    def _():
        st_ref[...] = jnp.zeros_like(st_ref)

        def body(i, carry):
            r0 = pl.multiple_of(i * c, c)
            r1 = pl.multiple_of((n_ctx_sub - 1 - i) * c, c)
            gf = _hgrn_gates(None, cff_ref[pl.ds(r0, c), :], ci_ref[pl.ds(r0, c), :], lbf, tri_f, c - 1)
            gb = _hgrn_gates(None, cfb_ref[pl.ds(r1, c), :], ci_ref[pl.ds(r1, c), :], lbb, tri_b, 0)
            _hgrn_chunks([(gf, keep_f), (gb, keep_b)], st_ref)
            return carry

        lax.fori_loop(0, n_ctx_sub, body, 0)

    @pl.when(n > 0)
    def _():
        def body(i, carry):
            r0 = pl.multiple_of(i * c, c)
            r1 = pl.multiple_of((n_sub - 1 - i) * c, c)
            gf = _hgrn_gates(qf_ref[pl.ds(r0, c), :], ff_ref[pl.ds(r0, c), :], if_ref[pl.ds(r0, c), :],
                             lbf, tri_f, c - 1)
            gb = _hgrn_gates(qb_ref[pl.ds(r1, c), :], fb_ref[pl.ds(r1, c), :], ib_ref[pl.ds(r1, c), :],
                             lbb, tri_b, 0)
            o_f, o_b = _hgrn_chunks([(gf, keep_f), (gb, keep_b)], st_ref)
            of_ref[pl.ds(r0, c), :] = o_f
            ob_ref[pl.ds(r1, c), :] = o_b
            return carry

        lax.fori_loop(0, n_sub, body, 0)


def _hgrn(p, pc, lb_fwd, lb_bwd):
    _, b, s, w = p.shape
    ctx_len = pc.shape[2]
    tb = min(256, s)
    nb = s // tb
    fwd = lambda bi, n: jnp.maximum(n - 1, 0)
    bwd = lambda bi, n: nb - 1 - jnp.maximum(n - 1, 0)

    def sec(section, blk):
        return pl.BlockSpec((None, None, tb, w), lambda bi, n: (section, bi, blk(bi, n), 0))

    def csec(section):
        return pl.BlockSpec((None, None, ctx_len, w), lambda bi, n: (section, bi, 0, 0))

    vec = pl.BlockSpec((1, w), lambda bi, n: (0, 0))
    kern = functools.partial(_hgrn_kernel, n_sub=tb // HG_CHUNK, n_ctx_sub=ctx_len // HG_CHUNK)
    return pl.pallas_call(
        kern,
        grid=(b, nb + 1),
        in_specs=[sec(SEC_Q, fwd), sec(SEC_FF, fwd), sec(SEC_I, fwd),
                  sec(SEC_Q, bwd), sec(SEC_FB, bwd), sec(SEC_I, bwd),
                  csec(SEC_FF), csec(SEC_FB), csec(SEC_I), vec, vec],
        out_specs=[pl.BlockSpec((None, tb, w), lambda bi, n: (bi, fwd(bi, n), 0)),
                   pl.BlockSpec((None, tb, w), lambda bi, n: (bi, bwd(bi, n), 0))],
        out_shape=[jax.ShapeDtypeStruct((b, s, w), F32), jax.ShapeDtypeStruct((b, s, w), F32)],
        scratch_shapes=[pltpu.VMEM((2, HG_HEADS, HG_DK, HG_DK), F32)],
        compiler_params=_cparams(("arbitrary", "arbitrary")),
        name="hgrn",
    )(p, p, p, p, p, p, pc, pc, pc, lb_fwd.reshape(1, w), lb_bwd.reshape(1, w))


NA_ROWS_PER_STEP = 8
NA_PREP_ROWS = 512
NA_KEY_TILE = 128
NA_SPAN = (NA_WIN_R + 2) * GRID_W


def _rope(t, cos, sin_signed, first_half):
    w = t.shape[-1]
    partner = jnp.where(first_half, pltpu.roll(t, w - 16, 1), pltpu.roll(t, 16, 1))
    return t * cos + partner * sin_signed


def _fold_lanes(op, *arrays):
    tiles = [a[:, c:c + 128] for a in arrays for c in range(0, a.shape[-1], 128)]
    acc = tiles[0]
    for t in tiles[1:]:
        acc = op(acc, t)
    return acc


def _rope_tables(rowtab_ref, coltab_ref, row0, n_rows, row_lane):
    out = []
    for i in range(2):
        rt = rowtab_ref[i, pl.ds(row0, n_rows), :]
        by_row = jnp.concatenate([jnp.broadcast_to(rt[r:r + 1, :], (GRID_W, rt.shape[1])) for r in range(n_rows)],
                                 axis=0)
        by_col = jnp.concatenate([coltab_ref[i]] * n_rows, axis=0)
        out.append(jnp.where(row_lane, by_row, by_col))
    return out


def _natten_kernel(q_ref, k_ref, v_ref, kc_ref, vc_ref, rowtab_ref, coltab_ref, t2_ref, o_ref,
                   kt_s, v_s, kc_s, vc_s, bias_s, *, rows):
    rblk = pl.program_id(2)
    hd = NA_HD
    lane = lax.broadcasted_iota(jnp.int32, (1, 2 * hd), 1)
    first_half = (lane % 32) < 16
    row_lane = (lane % hd) < hd // 2
    scale = NA_HD ** -0.5

    def values_and_ones(v_pair, h):
        vh = v_pair if h == 0 else pltpu.roll(v_pair, hd, 1)
        return jnp.where(lane < hd, vh, jnp.where(lane == hd, 1.0, 0.0)).astype(BF16)

    @pl.when(rblk == 0)
    def _():
        kc = kc_ref[...].astype(BF16)
        qi = lax.broadcasted_iota(jnp.int32, (GRID_W, GRID_W), 0)
        ki = lax.broadcasted_iota(jnp.int32, (GRID_W, GRID_W), 1)
        cstart = jnp.clip(qi - NA_WIN_C // 2, 0, GRID_W - NA_WIN_C)
        in_win = (ki >= cstart) & (ki < cstart + NA_WIN_C)
        masked = jnp.full((GRID_W, GRID_W), NEG_INF, F32)
        s_len = k_ref.shape[0]
        for h in range(2):
            sl = slice(h * hd, (h + 1) * hd)
            kc_s[h] = kc[:, sl]
            vc_s[h] = values_and_ones(vc_ref[...], h)
            kt_s[h, s_len // NA_KEY_TILE] = jnp.zeros((hd, NA_KEY_TILE), BF16)
            v_s[h, s_len:s_len + NA_KEY_TILE, :] = jnp.zeros((NA_KEY_TILE, 2 * hd), BF16)
            tiles = [jnp.where(in_win, t2_ref[h, dr], NEG_INF) for dr in range(2 * NA_WIN_R - 1)]
            for bidx in range(NA_WIN_R + 1):
                v, par = (bidx, 0) if bidx < NA_WIN_R else (NA_WIN_R // 2, 1)
                for piece in range(NA_SPAN // GRID_W):
                    j = piece - par
                    tile = tiles[NA_WIN_R - 1 - v + j] if 0 <= j < NA_WIN_R else masked
                    bias_s[h, bidx, :, piece * GRID_W:(piece + 1) * GRID_W] = tile

        def prep(i, carry):
            r0 = pl.multiple_of(i * NA_PREP_ROWS, NA_PREP_ROWS)
            rws = pl.ds(r0, NA_PREP_ROWS)
            cos, sin = _rope_tables(rowtab_ref, coltab_ref, i * (NA_PREP_ROWS // GRID_W), NA_PREP_ROWS // GRID_W,
                                    row_lane)
            kr = _rope(k_ref[rws, :], cos, sin, first_half)
            krt = kr.T.astype(BF16)
            vv = v_ref[rws, :]
            for h in range(2):
                sl = slice(h * hd, (h + 1) * hd)
                for c in range(NA_PREP_ROWS // NA_KEY_TILE):
                    kt_s[h, i * (NA_PREP_ROWS // NA_KEY_TILE) + c] = krt[sl, c * NA_KEY_TILE:(c + 1) * NA_KEY_TILE]
                v_s[h, rws, :] = values_and_ones(vv, h)
            return carry

        lax.fori_loop(0, s_len // NA_PREP_ROWS, prep, 0)

    tq = NA_ROWS_PER_STEP * GRID_W
    q = q_ref[...] * scale
    cos, sin = _rope_tables(rowtab_ref, coltab_ref, rblk * NA_ROWS_PER_STEP, NA_ROWS_PER_STEP, row_lane)
    qr = _rope(q, cos, sin, first_half)
    qb = q.astype(BF16)
    qrb = qr.astype(BF16)
    rws = [slice(rr * GRID_W, (rr + 1) * GRID_W) for rr in range(NA_ROWS_PER_STEP)]
    tile0, bidx = [], []
    for rr in range(NA_ROWS_PER_STEP):
        r = rblk * NA_ROWS_PER_STEP + rr
        rs = jnp.clip(r - NA_WIN_R // 2, 0, rows - NA_WIN_R)
        tile0.append(lax.shift_right_logical(rs, 1))
        bidx.append(jnp.where((rs & 1) == 1, NA_WIN_R, r - rs))

    def scores(h):
        sl = slice(h * hd, (h + 1) * hd)
        qrb_h = qrb[:, sl]
        s_ctx_all = _dot_nt(qb[:, sl], kc_s[h])
        s_win = []
        for rr in range(NA_ROWS_PER_STEP):
            kt = kt_s[h, pl.ds(tile0[rr], NA_SPAN // NA_KEY_TILE)]
            kt = jnp.concatenate([kt[c] for c in range(NA_SPAN // NA_KEY_TILE)], axis=-1)
            s_win.append(_dot(qrb_h[rws[rr]], kt))
        return s_win, s_ctx_all

    def softmax(h, s_win, s_ctx_all):
        e_win, e_ctx = [], []
        for rr in range(NA_ROWS_PER_STEP):
            sw = s_win[rr] + bias_s[h, bidx[rr]]
            sc = s_ctx_all[rws[rr]]
            m = jnp.max(_fold_lanes(jnp.maximum, sw, sc), axis=-1, keepdims=True)
            e_win.append(jnp.exp(sw - m).astype(BF16))
            e_ctx.append(jnp.exp(sc - m).astype(BF16))
        return e_win, e_ctx

    def values(h, e_win, e_ctx):
        o_win = []
        for rr in range(NA_ROWS_PER_STEP):
            k0 = pl.multiple_of(tile0[rr] * NA_KEY_TILE, NA_KEY_TILE)
            o_win.append(_dot(e_win[rr], v_s[h, pl.ds(k0, NA_SPAN), :]))
        o = jnp.concatenate(o_win, axis=0) + _dot(jnp.concatenate(e_ctx, axis=0), vc_s[h])
        return o[:, :hd] * (1.0 / o[:, hd:hd + 1])

    s0 = scores(0)
    s1 = scores(1)
    p0 = softmax(0, *s0)
    o0 = values(0, *p0)
    p1 = softmax(1, *s1)
    o1 = values(1, *p1)
    o_ref[...] = jnp.concatenate([o0, o1], axis=-1)


def _na_tables(rpb, s):
    half = NA_HD // 2
    inv = jnp.power(ROPE_THETA, -jnp.arange(0, half, 2, dtype=F32) / half)

    def tables(n):
        ang = jnp.arange(n, dtype=F32)[:, None] * inv[None, :]
        reps = 2 * NA_HD // half
        return jnp.stack([jnp.tile(jnp.cos(ang), (1, 2 * reps)),
                          jnp.tile(jnp.concatenate([-jnp.sin(ang), jnp.sin(ang)], axis=-1), (1, reps))])

    rowtab, coltab = tables(s // GRID_W), tables(GRID_W)

    pad = GRID_W - NA_WIN_C
    rp = jnp.pad(rpb.astype(F32), ((0, 0), (0, 0), (pad, pad)), mode="edge")
    t2 = jnp.stack([rp[:, :, GRID_W - 1 - qc:2 * GRID_W - 1 - qc] for qc in range(GRID_W)], axis=2)
    return rowtab, coltab, t2


def _natten(p, pc, rowtab, coltab, t2):
    _, b, s, w = p.shape
    ctx_len = pc.shape[2]
    rows = s // GRID_W
    assert rows >= NA_WIN_R and rows % NA_ROWS_PER_STEP == 0
    tq = NA_ROWS_PER_STEP * GRID_W
    hw = 2 * NA_HD
    nhp = w // hw
    kern = functools.partial(_natten_kernel, rows=rows)
    return pl.pallas_call(
        kern,
        grid=(b, nhp, rows // NA_ROWS_PER_STEP),
        in_specs=[pl.BlockSpec((None, None, tq, hw), lambda bi, hp, r: (SEC_NQ, bi, r, hp)),
                  pl.BlockSpec((None, None, s, hw), lambda bi, hp, r: (SEC_NK, bi, 0, hp)),
                  pl.BlockSpec((None, None, s, hw), lambda bi, hp, r: (SEC_NV, bi, 0, hp)),
                  pl.BlockSpec((None, None, ctx_len, hw), lambda bi, hp, r: (SEC_NK, bi, 0, hp)),
                  pl.BlockSpec((None, None, ctx_len, hw), lambda bi, hp, r: (SEC_NV, bi, 0, hp)),
                  pl.BlockSpec((2, rows, hw), lambda bi, hp, r: (0, 0, 0)),
                  pl.BlockSpec((2, GRID_W, hw), lambda bi, hp, r: (0, 0, 0)),
                  pl.BlockSpec((2, 2 * NA_WIN_R - 1, GRID_W, GRID_W), lambda bi, hp, r: (hp, 0, 0, 0))],
        out_specs=pl.BlockSpec((None, tq, hw), lambda bi, hp, r: (bi, r, hp)),
        out_shape=jax.ShapeDtypeStruct((b, s, w), F32),
        scratch_shapes=[pltpu.VMEM((2, s // NA_KEY_TILE + 1, NA_HD, NA_KEY_TILE), BF16),
                        pltpu.VMEM((2, s + NA_KEY_TILE, hw), BF16),
                        pltpu.VMEM((2, ctx_len, NA_HD), BF16), pltpu.VMEM((2, ctx_len, hw), BF16),
                        pltpu.VMEM((2, NA_WIN_R + 1, GRID_W, NA_SPAN), F32)],
        compiler_params=_cparams(("arbitrary", "arbitrary", "arbitrary")),
        name="natten",
    )(p, p, p, pc, pc, rowtab, coltab, t2)


def _route(logits_t, rbias):
    e, t = logits_t.shape
    gsz = e // N_GROUPS
    scores = jax.nn.sigmoid(logits_t)
    sel = scores + rbias
    neg = -jnp.inf
    sub = lax.broadcasted_iota(jnp.int32, (gsz, t), 0).astype(F32)
    gscore = []
    for g in range(N_GROUPS):
        grp = sel[g * gsz:(g + 1) * gsz, :]
        m1 = jnp.max(grp, axis=0, keepdims=True)
        first = jnp.min(jnp.where(grp == m1, sub, float(gsz)), axis=0, keepdims=True)
        m2 = jnp.max(jnp.where(sub == first, neg, grp), axis=0, keepdims=True)
        gscore.append(m1 + m2)
    masked = []
    for g in range(N_GROUPS):
        rank = jnp.zeros((1, t), F32)
        for g2 in range(N_GROUPS):
            if g2 == g:
                continue
            if g2 < g:
                ahead = gscore[g2] >= gscore[g]
            else:
                ahead = gscore[g2] > gscore[g]
            rank = rank + jnp.where(ahead, 1.0, 0.0)
        masked.append(jnp.where(rank < TOPK_GROUPS, sel[g * gsz:(g + 1) * gsz, :], neg))
    work = jnp.concatenate(masked, axis=0)
    eidx = lax.broadcasted_iota(jnp.int32, (e, t), 0).astype(F32)
    idxs, ws = [], []
    chosen = jnp.zeros((e, t), F32)
    for _ in range(TOP_K):
        m = jnp.max(work, axis=0, keepdims=True)
        first = jnp.min(jnp.where(work == m, eidx, float(e)), axis=0, keepdims=True)
        pick = eidx == first
        idxs.append(first)
        ws.append(jnp.sum(jnp.where(pick, scores, 0.0), axis=0, keepdims=True))
        chosen = jnp.where(pick, 1.0, chosen)
        work = jnp.where(pick, neg, work)
    w = jnp.concatenate(ws, axis=0)
    w = w / jnp.sum(w, axis=0, keepdims=True) * ROUTED_SCALE
    return jnp.concatenate(idxs, axis=0).astype(jnp.int32), w, chosen


MERGE_TOK = 512
MERGE_SUB = 256


def _merge_kernel(of_ref, ob_ref, og_ref, yna_ref, ga_ref, gb_ref, x_ref, g1_ref, sh2_ref, sc2_ref,
                  hgg_ref, ln1g_ref, ln1b_ref, wa_ref, wb_ref, wo_ref, wr_ref, rb_ref,
                  x1_ref, h2_ref, topi_ref, topw_ref, cnt_ref, *, alpha):
    tm = x_ref.shape[0]
    subs = [slice(i * MERGE_SUB, (i + 1) * MERGE_SUB) for i in range(tm // MERGE_SUB)]

    def branches(rows):
        o = of_ref[rows, :] + ob_ref[rows, :]
        parts = []
        for h in range(HG_HEADS):
            oh = o[:, h * HG_DK:(h + 1) * HG_DK]
            parts.append(oh * lax.rsqrt(jnp.mean(oh * oh, axis=-1, keepdims=True) + LN_EPS))
        y_hg = jnp.concatenate(parts, axis=-1) * hgg_ref[...] * _silu(og_ref[rows, :])
        return _dot(y_hg.astype(BF16), wa_ref[...]), _dot(yna_ref[rows, :].astype(BF16), wb_ref[...])

    def out_proj(rows, ya, yb):
        t = jax.nn.sigmoid(ga_ref[rows, :]) * ya + jax.nn.sigmoid(gb_ref[rows, :]) * yb
        return _dot(t.astype(BF16), wo_ref[...])

    def norms_router(rows, i, y):
        x1 = _normalize(alpha * x_ref[rows, :] + g1_ref[...] * y) * ln1g_ref[...] + ln1b_ref[...]
        x1_ref[rows, :] = x1
        h2 = _normalize(x1) * (1.0 + sc2_ref[...]) + sh2_ref[...]
        _store_packed(h2_ref.at[pl.ds(i * MERGE_SUB * PACK_SUBLANES, MERGE_SUB * PACK_SUBLANES), :], h2)
        hh, hm, hl = _split3(h2)
        wh, wm, wl = _split3(wr_ref[...])
        return (_dot_nt(wh, hh) + _dot_nt(wh, hm) + _dot_nt(wm, hh)
                + _dot_nt(wh, hl) + _dot_nt(wl, hh) + _dot_nt(wm, hm))

    ab = [branches(rows) for rows in subs]
    ys = [out_proj(rows, *ab[i]) for i, rows in enumerate(subs)]
    logits = [norms_router(rows, i, ys[i]) for i, rows in enumerate(subs)]

    @pl.when((pl.program_id(0) == 0) & (pl.program_id(1) == 0))
    def _():
        cnt_ref[...] = jnp.zeros_like(cnt_ref)

    for i, rows in enumerate(subs):
        topi, topw, chosen = _route(logits[i], rb_ref[...])
        topi_ref[:, rows] = topi
        topw_ref[:, rows] = topw
        cnt_ref[...] += jnp.sum(chosen, axis=1, keepdims=True)


def _merge(o_f, o_b, p, y_na, x, g1, sh2, sc2, hg_norm_g, ln1_g, ln1_b, w_a, w_b, w_o, w_router_t, router_bias,
           alpha):
    b, s, d = x.shape
    tm = min(MERGE_TOK, s)
    e = w_router_t.shape[0]
    tok = lambda bi, i: (bi, i, 0)
    blk = pl.BlockSpec((None, tm, d), tok)

    def sec(section):
        return pl.BlockSpec((None, None, tm, d), lambda bi, i: (section, bi, i, 0))

    mod = pl.BlockSpec((None, 1, d), lambda bi, i: (bi, 0, 0))
    vec = pl.BlockSpec((1, d), lambda bi, i: (0, 0))
    mat = pl.BlockSpec((d, d), lambda bi, i: (0, 0), pipeline_mode=pl.Buffered(1))
    return pl.pallas_call(
        functools.partial(_merge_kernel, alpha=alpha),
        grid=(b, s // tm),
        in_specs=[blk, blk, sec(SEC_OG), blk, sec(SEC_GA), sec(SEC_GB), blk, mod, mod, mod,
                  vec, vec, vec, mat, mat, mat,
                  pl.BlockSpec((e, d), lambda bi, i: (0, 0)),
                  pl.BlockSpec((e, 1), lambda bi, i: (0, 0))],
        out_specs=[blk,
                   pl.BlockSpec((tm * PACK_SUBLANES, 128), lambda bi, i: (bi * (s // tm) + i, 0)),
                   pl.BlockSpec((None, TOP_K, tm), lambda bi, i: (bi, 0, i)),
                   pl.BlockSpec((None, TOP_K, tm), lambda bi, i: (bi, 0, i)),
                   pl.BlockSpec((e, 128), lambda bi, i: (0, 0))],
        out_shape=[jax.ShapeDtypeStruct((b, s, d), F32),
                   jax.ShapeDtypeStruct((b * s * PACK_SUBLANES, 128), U32),
                   jax.ShapeDtypeStruct((b, TOP_K, s), jnp.int32), jax.ShapeDtypeStruct((b, TOP_K, s), F32),
                   jax.ShapeDtypeStruct((e, 128), F32)],
        compiler_params=_cparams(("arbitrary", "arbitrary")),
        name="merge",
    )(o_f, o_b, p, y_na, p, p, x, g1, sh2, sc2, hg_norm_g.reshape(1, d), ln1_g.reshape(1, d),
      ln1_b.reshape(1, d), w_a, w_b, w_o, w_router_t, router_bias.reshape(e, 1))


MOE_TILE = 512
MOE_TOK = 256


def _plan_kernel(topi_ref, off_ref, dest_ref, carry_ref):
    @pl.when(pl.program_id(0) == 0)
    def _():
        carry_ref[...] = jnp.zeros_like(carry_ref)

    topi = topi_ref[...]
    tok = topi.shape[1]
    eidx = lax.broadcasted_iota(jnp.int32, (N_EXPERTS, tok), 0)
    hits = [eidx == topi[k:k + 1, :] for k in range(TOP_K)]
    m = jnp.zeros((N_EXPERTS, tok), F32)
    for hit in hits:
        m = jnp.where(hit, 1.0, m)
    before = (lax.broadcasted_iota(jnp.int32, (tok, tok), 0)
              < lax.broadcasted_iota(jnp.int32, (tok, tok), 1)).astype(F32).astype(BF16)
    row = off_ref[...] + carry_ref[...] + _dot(m.astype(BF16), before)
    dest = [jnp.sum(jnp.where(hit, row, 0.0), axis=0, keepdims=True) for hit in hits]
    dest_ref[...] = jnp.concatenate(dest, axis=0).astype(jnp.int32)
    carry_ref[...] += jnp.sum(m, axis=1, keepdims=True)


def _plan(topi, seg_off):
    b, k, s = topi.shape
    per_b = s // MOE_TOK
    blk = pl.BlockSpec((None, k, MOE_TOK), lambda i: (i // per_b, 0, i % per_b))
    return pl.pallas_call(
        _plan_kernel,
        grid=(b * per_b,),
        in_specs=[blk, pl.BlockSpec((N_EXPERTS, 1), lambda i: (0, 0))],
        out_specs=blk,
        out_shape=jax.ShapeDtypeStruct((b, k, s), jnp.int32),
        scratch_shapes=[pltpu.VMEM((N_EXPERTS, 1), F32)],
        compiler_params=_cparams(("arbitrary",)),
        name="plan",
    )(topi, seg_off.astype(F32).reshape(N_EXPERTS, 1))


def _row_copy(src_ref, src_row, dst_ref, dst_row, sem):
    src = pl.ds(pl.multiple_of(src_row * PACK_SUBLANES, PACK_SUBLANES), PACK_SUBLANES)
    dst = pl.ds(pl.multiple_of(dst_row * PACK_SUBLANES, PACK_SUBLANES), PACK_SUBLANES)
    return pltpu.make_async_copy(src_ref.at[src, :], dst_ref.at[dst, :], sem)


def _dispatch_kernel(cnt_ref, off_ref, dest_ref, h_ref, xs_ref, zero_ref, sem):
    tok = h_ref.shape[0] // PACK_SUBLANES

    def issue(t, carry):
        for k in range(TOP_K):
            _row_copy(h_ref, t, xs_ref, dest_ref[k, t], sem).start(priority=k % 2)
        return carry

    def drain(t, carry):
        for k in range(TOP_K):
            _row_copy(h_ref, 0, xs_ref, 0, sem).wait()
        return carry

    lax.fori_loop(0, tok, issue, 0)

    @pl.when(pl.program_id(0) == pl.num_programs(0) - 1)
    def _():
        zero_ref[...] = jnp.zeros_like(zero_ref)

        def per_expert(e, carry):
            end = off_ref[e] + cnt_ref[e]
            npad = lax.rem(MOE_TILE - lax.rem(cnt_ref[e], MOE_TILE), MOE_TILE)

            def pad_issue(j, c):
                _row_copy(zero_ref, 0, xs_ref, end + j, sem).start()
                return c

            def pad_drain(j, c):
                _row_copy(zero_ref, 0, xs_ref, 0, sem).wait()
                return c

            lax.fori_loop(0, npad, pad_issue, 0)
            lax.fori_loop(0, npad, pad_drain, 0)
            return carry

        lax.fori_loop(0, N_EXPERTS, per_expert, 0)

    lax.fori_loop(0, tok, drain, 0)


def _dispatch(h2p, dest, cnt, seg_off, n_rows):
    b, k, s = dest.shape
    per_b = s // MOE_TOK
    grid_spec = pltpu.PrefetchScalarGridSpec(
        num_scalar_prefetch=2,
        grid=(b * per_b,),
        in_specs=[pl.BlockSpec((None, k, MOE_TOK), lambda i, c, o: (i // per_b, 0, i % per_b),
                               memory_space=pltpu.SMEM),
                  pl.BlockSpec((MOE_TOK * PACK_SUBLANES, 128), lambda i, c, o: (i, 0))],
        out_specs=pl.BlockSpec(memory_space=pl.ANY),
        scratch_shapes=[pltpu.VMEM((8, 128), U32), pltpu.SemaphoreType.DMA(())],
    )
    return pl.pallas_call(
        _dispatch_kernel,
        grid_spec=grid_spec,
        out_shape=jax.ShapeDtypeStruct((n_rows * PACK_SUBLANES, 128), U32),
        compiler_params=_cparams(("arbitrary",)),
        name="dispatch",
    )(cnt, seg_off, dest, h2p)


def _experts_kernel(te_ref, tb_ref, nt_ref, xs_ref, wg_ref, wu_ref, wd_ref, ys_ref):
    @pl.when(pl.program_id(0) < nt_ref[0])
    def _():
        x = _load_packed(xs_ref, MOE_TILE).astype(BF16)
        act = _silu(_dot(x, wg_ref[...].astype(BF16))) * _dot(x, wu_ref[...].astype(BF16))
        _store_packed(ys_ref, _dot(act.astype(BF16), wd_ref[...].astype(BF16)))


def _experts(xs, tile_expert, tile_block, n_tiles, wg, wu, wd):
    d, f = wg.shape[1], wg.shape[2]
    rows = pl.BlockSpec((MOE_TILE * PACK_SUBLANES, 128), lambda i, te, tb, nt: (tb[i], 0))
    grid_spec = pltpu.PrefetchScalarGridSpec(
        num_scalar_prefetch=3,
        grid=(xs.shape[0] // (MOE_TILE * PACK_SUBLANES),),
        in_specs=[rows,
                  pl.BlockSpec((None, d, f), lambda i, te, tb, nt: (te[i], 0, 0)),
                  pl.BlockSpec((None, d, f), lambda i, te, tb, nt: (te[i], 0, 0)),
                  pl.BlockSpec((None, f, d), lambda i, te, tb, nt: (te[i], 0, 0))],
        out_specs=rows,
    )
    return pl.pallas_call(
        _experts_kernel,
        grid_spec=grid_spec,
        out_shape=jax.ShapeDtypeStruct(xs.shape, U32),
        compiler_params=_cparams(("arbitrary",)),
        name="experts",
    )(tile_expert, tile_block, n_tiles, xs, wg, wu, wd)


def _combine_kernel(dest_ref, topw_ref, h_ref, x1_ref, g2_ref, sg_ref, su_ref, sd_ref, ln2g_ref, ln2b_ref,
                    ys_ref, o_ref, buf_ref, sem, *, alpha):
    tok = x1_ref.shape[0]

    def issue(t, carry):
        for k in range(TOP_K):
            _row_copy(ys_ref, dest_ref[k, t], buf_ref.at[k], t, sem).start(priority=k % 2)
        return carry

    def drain(t, carry):
        for k in range(TOP_K):
            _row_copy(ys_ref, 0, buf_ref.at[0], 0, sem).wait()
        return carry

    lax.fori_loop(0, tok, issue, 0)
    h = _load_packed(h_ref, tok).astype(BF16)
    act = _silu(_dot(h, sg_ref[...])) * _dot(h, su_ref[...])
    y = _dot(act.astype(BF16), sd_ref[...])
    w = topw_ref[...].T
    lax.fori_loop(0, tok, drain, 0)
    for k in range(TOP_K):
        y = y + w[:, k:k + 1] * _load_packed(buf_ref.at[k], tok)
    o_ref[...] = _normalize(alpha * x1_ref[...] + g2_ref[...] * y) * ln2g_ref[...] + ln2b_ref[...]


def _combine(ys, dest, topw, h2p, x1, g2, sg, su, sd, ln2_g, ln2_b, alpha):
    t, d = x1.shape
    b, k, s = dest.shape
    per_b = s // MOE_TOK
    fs = sg.shape[1]
    tk = lambda i: (i // per_b, 0, i % per_b)
    rows = pl.BlockSpec((MOE_TOK, d), lambda i: (i, 0))
    packed = pl.BlockSpec((MOE_TOK * PACK_SUBLANES, 128), lambda i: (i, 0))
    vec = pl.BlockSpec((1, d), lambda i: (0, 0))
    return pl.pallas_call(
        functools.partial(_combine_kernel, alpha=alpha),
        grid=(t // MOE_TOK,),
        in_specs=[pl.BlockSpec((None, k, MOE_TOK), tk, memory_space=pltpu.SMEM),
                  pl.BlockSpec((None, k, MOE_TOK), tk),
                  packed, rows,
                  pl.BlockSpec((None, 1, d), lambda i: (i // per_b, 0, 0)),
                  pl.BlockSpec((d, fs), lambda i: (0, 0)),
                  pl.BlockSpec((d, fs), lambda i: (0, 0)),
                  pl.BlockSpec((fs, d), lambda i: (0, 0)),
                  vec, vec,
                  pl.BlockSpec(memory_space=pl.ANY)],
        out_specs=rows,
        out_shape=jax.ShapeDtypeStruct((t, d), F32),
        scratch_shapes=[pltpu.VMEM((k, MOE_TOK * PACK_SUBLANES, 128), U32), pltpu.SemaphoreType.DMA(())],
        compiler_params=_cparams(("arbitrary",)),
        name="combine",
    )(dest, topw, h2p, x1, g2, sg, su, sd, ln2_g.reshape(1, d), ln2_b.reshape(1, d), ys)


def _moe(h2p, topi, topw, cnt, x1, g2, wg, wu, wd, sg, su, sd, ln2_g, ln2_b, alpha):
    b, s, d = x1.shape
    t = b * s
    cnt = cnt[:, 0].astype(jnp.int32)
    tiles_e = (cnt + (MOE_TILE - 1)) // MOE_TILE
    tiles_cum = jnp.cumsum(tiles_e)
    seg_off = (tiles_cum - tiles_e) * MOE_TILE
    n_tiles_max = t * TOP_K // MOE_TILE + N_EXPERTS
    tile_block = jnp.minimum(jnp.arange(n_tiles_max, dtype=jnp.int32), tiles_cum[-1] - 1)
    tile_expert = jnp.sum((tiles_cum[None, :] <= tile_block[:, None]).astype(jnp.int32), axis=1)
    n_tiles = tiles_cum[-1:].astype(jnp.int32)

    dest = _plan(topi, seg_off)
    xs = _dispatch(h2p, dest, cnt, seg_off.astype(jnp.int32), n_tiles_max * MOE_TILE)
    ys = _experts(xs, tile_expert, tile_block, n_tiles, wg, wu, wd)
    out = _combine(ys, dest, topw, h2p, x1.reshape(t, d), g2, sg, su, sd, ln2_g, ln2_b, alpha)
    return out.reshape(b, s, d)


def kernel(x, c, ctx, c_ctx, w_ada, b_ada, w_in, hg_lb_fwd, hg_lb_bwd, hg_norm_g, na_rpb, w_branch_a, w_branch_b, w_out, ln1_g, ln1_b, w_router, router_bias, w_e_gate, w_e_up, w_e_down, w_sh_gate, w_sh_up, w_sh_down, ln2_g, ln2_b):
    depth = w_ada.shape[0]
    assert depth == 1, "single-layer block"
    b, s, d = x.shape
    alpha = (2.0 * depth) ** 0.25
    l = 0
    lb_fwd = jnp.cumsum(jax.nn.softmax(hg_lb_fwd.astype(F32), axis=0), axis=0)[l]
    lb_bwd = jnp.cumsum(jax.nn.softmax(hg_lb_bwd.astype(F32), axis=0), axis=0)[l]

    cond_rows = jnp.concatenate([c, c_ctx[None, :], jnp.zeros((8 - b - 1, d), F32)], axis=0)
    mod = _ada(cond_rows, w_ada[l], b_ada[l])
    sh1, sc1, g1, sh2, sc2, g2 = [m[:b, None, :] for m in jnp.split(mod, 6, axis=-1)]
    csh1, csc1 = [jnp.broadcast_to(m[b:b + 1, None, :], (b, 1, d)) for m in jnp.split(mod, 6, axis=-1)[:2]]

    w_in_b = w_in[l]
    p = _inproj(x, sh1, sc1, w_in_b)
    pc = _inproj(ctx, csh1, csc1, w_in_b)

    o_f, o_b = _hgrn(p, pc, lb_fwd, lb_bwd)
    y_na = _natten(p, pc, *_na_tables(na_rpb[l], s))

    x1, h2, topi, topw, cnt = _merge(o_f, o_b, p, y_na, x, g1, sh2, sc2, hg_norm_g[l], ln1_g[l], ln1_b[l],
                                     w_branch_a[l].astype(BF16), w_branch_b[l].astype(BF16),
                                     w_out[l].astype(BF16), w_router[l].T, router_bias[l], alpha)

    return _moe(h2, topi, topw, cnt, x1, g2,
                w_e_gate[l], w_e_up[l], w_e_down[l],
                w_sh_gate[l].astype(BF16), w_sh_up[l].astype(BF16), w_sh_down[l].astype(BF16),
                ln2_g[l], ln2_b[l], alpha)
```

```python
import functools

import numpy as np
import jax
import jax.numpy as jnp
from jax import lax
from jax.experimental import pallas as pl
from jax.experimental.pallas import tpu as pltpu

F32 = jnp.float32
BF16 = jnp.bfloat16

D_MODEL = 1024
GRID_W = 64
HG_HEADS = 8
HG_DK = 128
HG_CHUNK = 64
NA_HEADS = 16
NA_HD = 64
NA_WIN_R = 8
NA_WIN_C = 16
ROPE_THETA = 10000.0
NEG_INF = -1e30
N_EXPERTS = 64
EXPERT_DIM = 256
TOP_K = 8
N_GROUPS = 8
TOPK_GROUPS = 4
ROUTED_SCALE = 2.5
LN_EPS = 1e-6
N_SECTIONS = 10
SEC_Q, SEC_FF, SEC_FB, SEC_I, SEC_OG, SEC_NQ, SEC_NK, SEC_NV, SEC_GA, SEC_GB = range(10)

VMEM_LIMIT = 56 * 1024 * 1024


def _cparams(sem):
    return pltpu.CompilerParams(dimension_semantics=sem, vmem_limit_bytes=VMEM_LIMIT)


def _normalize(x):
    mu = jnp.mean(x, axis=-1, keepdims=True)
    xc = x - mu
    var = jnp.mean(xc * xc, axis=-1, keepdims=True)
    return xc * lax.rsqrt(var + LN_EPS)


def _silu(x):
    return x * jax.nn.sigmoid(x)


def _dot(a, b):
    return jnp.dot(a, b, preferred_element_type=F32)


def _dot_nt(a, b):
    return lax.dot_general(a, b, (((1,), (1,)), ((), ())), preferred_element_type=F32)


def _dot_tn(a, b):
    return lax.dot_general(a, b, (((0,), (0,)), ((), ())), preferred_element_type=F32)


PACK_SUBLANES = 4
U32 = jnp.uint32


def _store_packed(ref, x):
    n, d = x.shape
    half = d // 2
    lo = lax.bitcast_convert_type(x[:, :half].astype(BF16).astype(F32), U32) >> 16
    hi = lax.bitcast_convert_type(x[:, half:].astype(BF16).astype(F32), U32) & jnp.uint32(0xFFFF0000)
    w = lo | hi
    for c in range(PACK_SUBLANES):
        ref[pl.ds(c, n, stride=PACK_SUBLANES), :] = w[:, c * 128:(c + 1) * 128]


def _load_packed(ref, n):
    w = jnp.concatenate([ref[pl.ds(c, n, stride=PACK_SUBLANES), :] for c in range(PACK_SUBLANES)], axis=-1)
    lo = lax.bitcast_convert_type(w << 16, F32)
    hi = lax.bitcast_convert_type(w & jnp.uint32(0xFFFF0000), F32)
    return jnp.concatenate([lo, hi], axis=-1)


def _split3(x):
    hi = x.astype(BF16)
    r1 = x - hi.astype(F32)
    mid = r1.astype(BF16)
    lo = (r1 - mid.astype(F32)).astype(BF16)
    return hi, mid, lo


def _ada_kernel(c_ref, w_ref, b_ref, o_ref):
    cond = _silu(c_ref[...])
    o_ref[...] = _dot(cond.astype(BF16), w_ref[...].astype(BF16)) + b_ref[...]


def _ada(cond_rows, w_ada, b_ada):
    r, d = cond_rows.shape
    n = w_ada.shape[1]
    tn = 1024
    return pl.pallas_call(
        _ada_kernel,
        grid=(n // tn,),
        in_specs=[pl.BlockSpec((r, d), lambda j: (0, 0)),
                  pl.BlockSpec((d, tn), lambda j: (0, j)),
                  pl.BlockSpec((1, tn), lambda j: (0, j))],
        out_specs=pl.BlockSpec((r, tn), lambda j: (0, j)),
        out_shape=jax.ShapeDtypeStruct((r, n), F32),
        compiler_params=_cparams(("arbitrary",)),
        name="ada",
    )(cond_rows, w_ada, b_ada.reshape(1, n))


def _inproj_kernel(x_ref, sh_ref, sc_ref, w_ref, o_ref, h_ref):
    @pl.when(pl.program_id(2) == 0)
    def _():
        h = _normalize(x_ref[...]) * (1.0 + sc_ref[...]) + sh_ref[...]
        h_ref[...] = h.astype(BF16)

    o_ref[...] = _dot(h_ref[...], w_ref[...])


def _inproj(x, shift, scale, w_in_bf16):
    b, s, d = x.shape
    tm = min(1024, s)
    nj = w_in_bf16.shape[1] // d
    return pl.pallas_call(
        _inproj_kernel,
        grid=(b, s // tm, nj),
        in_specs=[pl.BlockSpec((None, tm, d), lambda bi, i, j: (bi, i, 0)),
                  pl.BlockSpec((None, 1, d), lambda bi, i, j: (bi, 0, 0)),
                  pl.BlockSpec((None, 1, d), lambda bi, i, j: (bi, 0, 0)),
                  pl.BlockSpec((d, d), lambda bi, i, j: (0, j))],
        out_specs=pl.BlockSpec((None, None, tm, d), lambda bi, i, j: (j, bi, i, 0)),
        out_shape=jax.ShapeDtypeStruct((nj, b, s, d), F32),
        scratch_shapes=[pltpu.VMEM((tm, d), BF16)],
        compiler_params=_cparams(("arbitrary", "arbitrary", "arbitrary")),
        name="inproj",
    )(x, shift, scale, w_in_bf16)


def _hgrn_gates(q, fraw, v, lb, tri_bf16, last_row):
    f = lb + (1.0 - lb) * jax.nn.sigmoid(fraw)
    k = 1.0 - f
    lf = jnp.log(f)
    hi, mid, lo = _split3(lf)
    a = _dot(tri_bf16, hi) + _dot(tri_bf16, mid) + _dot(tri_bf16, lo)
    a_last = a[last_row:last_row + 1, :]
    kd = (k * jnp.exp(a_last - a)).astype(BF16)
    decay = jnp.exp(a_last)
    qa = kb = None
    if q is not None:
        qa = (_silu(q) * jnp.exp(a)).astype(BF16)
        kb = (k * jnp.exp(-a)).astype(BF16)
    return qa, kb, kd, v.astype(BF16), decay


def _hgrn_chunks(chunks, st_ref):
    first = []
    for d, ((qa, kb, kd, vb, decay), keep) in enumerate(chunks):
        for h in range(HG_HEADS):
            sl = slice(h * HG_DK, (h + 1) * HG_DK)
            st = st_ref[d, h]
            if qa is not None:
                first.append((_dot_nt(qa[:, sl], kb[:, sl]), _dot_nt(qa[:, sl], st.astype(BF16))))
            st_ref[d, h] = st * decay[:, sl] + _dot_tn(vb[:, sl], kd[:, sl])
    results = []
    for d, ((qa, kb, kd, vb, decay), keep) in enumerate(chunks):
        if qa is None:
            results.append(None)
            continue
        outs = []
        for h in range(HG_HEADS):
            sl = slice(h * HG_DK, (h + 1) * HG_DK)
            s_qk, o_state = first.pop(0)
            outs.append(_dot(jnp.where(keep, s_qk, 0.0).astype(BF16), vb[:, sl]) + o_state)
        results.append(jnp.concatenate(outs, axis=-1))
    return results


def _hgrn_kernel(qf_ref, ff_ref, if_ref, qb_ref, fb_ref, ib_ref, cff_ref, cfb_ref, ci_ref,
                 lbf_ref, lbb_ref, of_ref, ob_ref, st_ref, *, n_sub, n_ctx_sub):
    n = pl.program_id(1)
    c = HG_CHUNK
    row = lax.broadcasted_iota(jnp.int32, (c, c), 0)
    col = lax.broadcasted_iota(jnp.int32, (c, c), 1)
    keep_f = col <= row
    keep_b = col >= row
    tri_f = keep_f.astype(F32).astype(BF16)
    tri_b = keep_b.astype(F32).astype(BF16)
    lbf = lbf_ref[...]
    lbb = lbb_ref[...]

    @pl.when(n == 0)
    def _():
        st_ref[...] = jnp.zeros_like(st_ref)

        def body(i, carry):
            r0 = pl.multiple_of(i * c, c)
            r1 = pl.multiple_of((n_ctx_sub - 1 - i) * c, c)
            gf = _hgrn_gates(None, cff_ref[pl.ds(r0, c), :], ci_ref[pl.ds(r0, c), :], lbf, tri_f, c - 1)
            gb = _hgrn_gates(None, cfb_ref[pl.ds(r1, c), :], ci_ref[pl.ds(r1, c), :], lbb, tri_b, 0)
            _hgrn_chunks([(gf, keep_f), (gb, keep_b)], st_ref)
            return carry

        lax.fori_loop(0, n_ctx_sub, body, 0)

    @pl.when(n > 0)
    def _():
        def body(i, carry):
            r0 = pl.multiple_of(i * c, c)
            r1 = pl.multiple_of((n_sub - 1 - i) * c, c)
            gf = _hgrn_gates(qf_ref[pl.ds(r0, c), :], ff_ref[pl.ds(r0, c), :], if_ref[pl.ds(r0, c), :],
                             lbf, tri_f, c - 1)
            gb = _hgrn_gates(qb_ref[pl.ds(r1, c), :], fb_ref[pl.ds(r1, c), :], ib_ref[pl.ds(r1, c), :],
                             lbb, tri_b, 0)
            o_f, o_b = _hgrn_chunks([(gf, keep_f), (gb, keep_b)], st_ref)
            of_ref[pl.ds(r0, c), :] = o_f
            ob_ref[pl.ds(r1, c), :] = o_b
            return carry

        lax.fori_loop(0, n_sub, body, 0)


def _hgrn(p, pc, lb_fwd, lb_bwd):
    _, b, s, w = p.shape
    ctx_len = pc.shape[2]
    tb = min(256, s)
    nb = s // tb
    fwd = lambda bi, n: jnp.maximum(n - 1, 0)
    bwd = lambda bi, n: nb - 1 - jnp.maximum(n - 1, 0)

    def sec(section, blk):
        return pl.BlockSpec((None, None, tb, w), lambda bi, n: (section, bi, blk(bi, n), 0))

    def csec(section):
        return pl.BlockSpec((None, None, ctx_len, w), lambda bi, n: (section, bi, 0, 0))

    vec = pl.BlockSpec((1, w), lambda bi, n: (0, 0))
    kern = functools.partial(_hgrn_kernel, n_sub=tb // HG_CHUNK, n_ctx_sub=ctx_len // HG_CHUNK)
    return pl.pallas_call(
        kern,
        grid=(b, nb + 1),
        in_specs=[sec(SEC_Q, fwd), sec(SEC_FF, fwd), sec(SEC_I, fwd),
                  sec(SEC_Q, bwd), sec(SEC_FB, bwd), sec(SEC_I, bwd),
                  csec(SEC_FF), csec(SEC_FB), csec(SEC_I), vec, vec],
        out_specs=[pl.BlockSpec((None, tb, w), lambda bi, n: (bi, fwd(bi, n), 0)),
                   pl.BlockSpec((None, tb, w), lambda bi, n: (bi, bwd(bi, n), 0))],
        out_shape=[jax.ShapeDtypeStruct((b, s, w), F32), jax.ShapeDtypeStruct((b, s, w), F32)],
        scratch_shapes=[pltpu.VMEM((2, HG_HEADS, HG_DK, HG_DK), F32)],
        compiler_params=_cparams(("arbitrary", "arbitrary")),
        name="hgrn",
    )(p, p, p, p, p, p, pc, pc, pc, lb_fwd.reshape(1, w), lb_bwd.reshape(1, w))


NA_ROWS_PER_STEP = 8
NA_PREP_ROWS = 512
NA_KEY_TILE = 128
NA_SPAN = (NA_WIN_R + 2) * GRID_W


def _rope(t, cos, sin_signed, first_half):
    w = t.shape[-1]
    partner = jnp.where(first_half, pltpu.roll(t, w - 16, 1), pltpu.roll(t, 16, 1))
    return t * cos + partner * sin_signed


def _fold_lanes(op, *arrays):
    tiles = [a[:, c:c + 128] for a in arrays for c in range(0, a.shape[-1], 128)]
    acc = tiles[0]
    for t in tiles[1:]:
        acc = op(acc, t)
    return acc


def _rope_tables(rowtab_ref, coltab_ref, row0, n_rows, row_lane):
    out = []
    for i in range(2):
        rt = rowtab_ref[i, pl.ds(row0, n_rows), :]
        by_row = jnp.concatenate([jnp.broadcast_to(rt[r:r + 1, :], (GRID_W, rt.shape[1])) for r in range(n_rows)],
                                 axis=0)
        by_col = jnp.concatenate([coltab_ref[i]] * n_rows, axis=0)
        out.append(jnp.where(row_lane, by_row, by_col))
    return out


def _natten_kernel(q_ref, k_ref, v_ref, kc_ref, vc_ref, rowtab_ref, coltab_ref, t2_ref, o_ref,
                   kt_s, v_s, kc_s, vc_s, bias_s, *, rows):
    rblk = pl.program_id(2)
    hd = NA_HD
    lane = lax.broadcasted_iota(jnp.int32, (1, 2 * hd), 1)
    first_half = (lane % 32) < 16
    row_lane = (lane % hd) < hd // 2
    scale = NA_HD ** -0.5

    def values_and_ones(v_pair, h):
        vh = v_pair if h == 0 else pltpu.roll(v_pair, hd, 1)
        return jnp.where(lane < hd, vh, jnp.where(lane == hd, 1.0, 0.0)).astype(BF16)

    @pl.when(rblk == 0)
    def _():
        kc = kc_ref[...].astype(BF16)
        qi = lax.broadcasted_iota(jnp.int32, (GRID_W, GRID_W), 0)
        ki = lax.broadcasted_iota(jnp.int32, (GRID_W, GRID_W), 1)
        cstart = jnp.clip(qi - NA_WIN_C // 2, 0, GRID_W - NA_WIN_C)
        in_win = (ki >= cstart) & (ki < cstart + NA_WIN_C)
        masked = jnp.full((GRID_W, GRID_W), NEG_INF, F32)
        s_len = k_ref.shape[0]
        for h in range(2):
            sl = slice(h * hd, (h + 1) * hd)
            kc_s[h] = kc[:, sl]
            vc_s[h] = values_and_ones(vc_ref[...], h)
            kt_s[h, s_len // NA_KEY_TILE] = jnp.zeros((hd, NA_KEY_TILE), BF16)
            v_s[h, s_len:s_len + NA_KEY_TILE, :] = jnp.zeros((NA_KEY_TILE, 2 * hd), BF16)
            tiles = [jnp.where(in_win, t2_ref[h, dr], NEG_INF) for dr in range(2 * NA_WIN_R - 1)]
            for bidx in range(NA_WIN_R + 1):
                v, par = (bidx, 0) if bidx < NA_WIN_R else (NA_WIN_R // 2, 1)
                for piece in range(NA_SPAN // GRID_W):
                    j = piece - par
                    tile = tiles[NA_WIN_R - 1 - v + j] if 0 <= j < NA_WIN_R else masked
                    bias_s[h, bidx, :, piece * GRID_W:(piece + 1) * GRID_W] = tile

        def prep(i, carry):
            r0 = pl.multiple_of(i * NA_PREP_ROWS, NA_PREP_ROWS)
            rws = pl.ds(r0, NA_PREP_ROWS)
            cos, sin = _rope_tables(rowtab_ref, coltab_ref, i * (NA_PREP_ROWS // GRID_W), NA_PREP_ROWS // GRID_W,
                                    row_lane)
            kr = _rope(k_ref[rws, :], cos, sin, first_half)
            krt = kr.T.astype(BF16)
            vv = v_ref[rws, :]
            for h in range(2):
                sl = slice(h * hd, (h + 1) * hd)
                for c in range(NA_PREP_ROWS // NA_KEY_TILE):
                    kt_s[h, i * (NA_PREP_ROWS // NA_KEY_TILE) + c] = krt[sl, c * NA_KEY_TILE:(c + 1) * NA_KEY_TILE]
                v_s[h, rws, :] = values_and_ones(vv, h)
            return carry

        lax.fori_loop(0, s_len // NA_PREP_ROWS, prep, 0)

    tq = NA_ROWS_PER_STEP * GRID_W
    q = q_ref[...] * scale
    cos, sin = _rope_tables(rowtab_ref, coltab_ref, rblk * NA_ROWS_PER_STEP, NA_ROWS_PER_STEP, row_lane)
    qr = _rope(q, cos, sin, first_half)
    qb = q.astype(BF16)
    qrb = qr.astype(BF16)
    rws = [slice(rr * GRID_W, (rr + 1) * GRID_W) for rr in range(NA_ROWS_PER_STEP)]
    tile0, bidx = [], []
    for rr in range(NA_ROWS_PER_STEP):
        r = rblk * NA_ROWS_PER_STEP + rr
        rs = jnp.clip(r - NA_WIN_R // 2, 0, rows - NA_WIN_R)
        tile0.append(lax.shift_right_logical(rs, 1))
        bidx.append(jnp.where((rs & 1) == 1, NA_WIN_R, r - rs))

    def scores(h):
        sl = slice(h * hd, (h + 1) * hd)
        qrb_h = qrb[:, sl]
        s_ctx_all = _dot_nt(qb[:, sl], kc_s[h])
        s_win = []
        for rr in range(NA_ROWS_PER_STEP):
            kt = kt_s[h, pl.ds(tile0[rr], NA_SPAN // NA_KEY_TILE)]
            kt = jnp.concatenate([kt[c] for c in range(NA_SPAN // NA_KEY_TILE)], axis=-1)
            s_win.append(_dot(qrb_h[rws[rr]], kt))
        return s_win, s_ctx_all

    def softmax(h, s_win, s_ctx_all):
        e_win, e_ctx = [], []
        for rr in range(NA_ROWS_PER_STEP):
            sw = s_win[rr] + bias_s[h, bidx[rr]]
            sc = s_ctx_all[rws[rr]]
            m = jnp.max(_fold_lanes(jnp.maximum, sw, sc), axis=-1, keepdims=True)
            e_win.append(jnp.exp(sw - m).astype(BF16))
            e_ctx.append(jnp.exp(sc - m).astype(BF16))
        return e_win, e_ctx

    def values(h, e_win, e_ctx):
        o_win = []
        for rr in range(NA_ROWS_PER_STEP):
            k0 = pl.multiple_of(tile0[rr] * NA_KEY_TILE, NA_KEY_TILE)
            o_win.append(_dot(e_win[rr], v_s[h, pl.ds(k0, NA_SPAN), :]))
        o = jnp.concatenate(o_win, axis=0) + _dot(jnp.concatenate(e_ctx, axis=0), vc_s[h])
        return o[:, :hd] * (1.0 / o[:, hd:hd + 1])

    s0 = scores(0)
    s1 = scores(1)
    p0 = softmax(0, *s0)
    o0 = values(0, *p0)
    p1 = softmax(1, *s1)
    o1 = values(1, *p1)
    o_ref[...] = jnp.concatenate([o0, o1], axis=-1)


def _na_tables(rpb, s):
    half = NA_HD // 2
    inv = jnp.power(ROPE_THETA, -jnp.arange(0, half, 2, dtype=F32) / half)

    def tables(n):
        ang = jnp.arange(n, dtype=F32)[:, None] * inv[None, :]
        reps = 2 * NA_HD // half
        return jnp.stack([jnp.tile(jnp.cos(ang), (1, 2 * reps)),
                          jnp.tile(jnp.concatenate([-jnp.sin(ang), jnp.sin(ang)], axis=-1), (1, reps))])

    rowtab, coltab = tables(s // GRID_W), tables(GRID_W)

    pad = GRID_W - NA_WIN_C
    rp = jnp.pad(rpb.astype(F32), ((0, 0), (0, 0), (pad, pad)), mode="edge")
    t2 = jnp.stack([rp[:, :, GRID_W - 1 - qc:2 * GRID_W - 1 - qc] for qc in range(GRID_W)], axis=2)
    return rowtab, coltab, t2


def _natten(p, pc, rowtab, coltab, t2):
    _, b, s, w = p.shape
    ctx_len = pc.shape[2]
    rows = s // GRID_W
    assert rows >= NA_WIN_R and rows % NA_ROWS_PER_STEP == 0
    tq = NA_ROWS_PER_STEP * GRID_W
    hw = 2 * NA_HD
    nhp = w // hw
    kern = functools.partial(_natten_kernel, rows=rows)
    return pl.pallas_call(
        kern,
        grid=(b, nhp, rows // NA_ROWS_PER_STEP),
        in_specs=[pl.BlockSpec((None, None, tq, hw), lambda bi, hp, r: (SEC_NQ, bi, r, hp)),
                  pl.BlockSpec((None, None, s, hw), lambda bi, hp, r: (SEC_NK, bi, 0, hp)),
                  pl.BlockSpec((None, None, s, hw), lambda bi, hp, r: (SEC_NV, bi, 0, hp)),
                  pl.BlockSpec((None, None, ctx_len, hw), lambda bi, hp, r: (SEC_NK, bi, 0, hp)),
                  pl.BlockSpec((None, None, ctx_len, hw), lambda bi, hp, r: (SEC_NV, bi, 0, hp)),
                  pl.BlockSpec((2, rows, hw), lambda bi, hp, r: (0, 0, 0)),
                  pl.BlockSpec((2, GRID_W, hw), lambda bi, hp, r: (0, 0, 0)),
                  pl.BlockSpec((2, 2 * NA_WIN_R - 1, GRID_W, GRID_W), lambda bi, hp, r: (hp, 0, 0, 0))],
        out_specs=pl.BlockSpec((None, tq, hw), lambda bi, hp, r: (bi, r, hp)),
        out_shape=jax.ShapeDtypeStruct((b, s, w), F32),
        scratch_shapes=[pltpu.VMEM((2, s // NA_KEY_TILE + 1, NA_HD, NA_KEY_TILE), BF16),
                        pltpu.VMEM((2, s + NA_KEY_TILE, hw), BF16),
                        pltpu.VMEM((2, ctx_len, NA_HD), BF16), pltpu.VMEM((2, ctx_len, hw), BF16),
                        pltpu.VMEM((2, NA_WIN_R + 1, GRID_W, NA_SPAN), F32)],
        compiler_params=_cparams(("arbitrary", "arbitrary", "arbitrary")),
        name="natten",
    )(p, p, p, pc, pc, rowtab, coltab, t2)


def _route(logits_t, rbias):
    e, t = logits_t.shape
    gsz = e // N_GROUPS
    scores = jax.nn.sigmoid(logits_t)
    sel = scores + rbias
    neg = -jnp.inf
    sub = lax.broadcasted_iota(jnp.int32, (gsz, t), 0).astype(F32)
    gscore = []
    for g in range(N_GROUPS):
        grp = sel[g * gsz:(g + 1) * gsz, :]
        m1 = jnp.max(grp, axis=0, keepdims=True)
        first = jnp.min(jnp.where(grp == m1, sub, float(gsz)), axis=0, keepdims=True)
        m2 = jnp.max(jnp.where(sub == first, neg, grp), axis=0, keepdims=True)
        gscore.append(m1 + m2)
    masked = []
    for g in range(N_GROUPS):
        rank = jnp.zeros((1, t), F32)
        for g2 in range(N_GROUPS):
            if g2 == g:
                continue
            if g2 < g:
                ahead = gscore[g2] >= gscore[g]
            else:
                ahead = gscore[g2] > gscore[g]
            rank = rank + jnp.where(ahead, 1.0, 0.0)
        masked.append(jnp.where(rank < TOPK_GROUPS, sel[g * gsz:(g + 1) * gsz, :], neg))
    work = jnp.concatenate(masked, axis=0)
    eidx = lax.broadcasted_iota(jnp.int32, (e, t), 0).astype(F32)
    idxs, ws = [], []
    chosen = jnp.zeros((e, t), F32)
    for _ in range(TOP_K):
        m = jnp.max(work, axis=0, keepdims=True)
        first = jnp.min(jnp.where(work == m, eidx, float(e)), axis=0, keepdims=True)
        pick = eidx == first
        idxs.append(first)
        ws.append(jnp.sum(jnp.where(pick, scores, 0.0), axis=0, keepdims=True))
        chosen = jnp.where(pick, 1.0, chosen)
        work = jnp.where(pick, neg, work)
    w = jnp.concatenate(ws, axis=0)
    w = w / jnp.sum(w, axis=0, keepdims=True) * ROUTED_SCALE
    return jnp.concatenate(idxs, axis=0).astype(jnp.int32), w, chosen


MERGE_TOK = 512
MERGE_SUB = 256


def _merge_kernel(of_ref, ob_ref, og_ref, yna_ref, ga_ref, gb_ref, x_ref, g1_ref, sh2_ref, sc2_ref,
                  hgg_ref, ln1g_ref, ln1b_ref, wa_ref, wb_ref, wo_ref, wr_ref, rb_ref,
                  x1_ref, h2_ref, topi_ref, topw_ref, cnt_ref, *, alpha):
    tm = x_ref.shape[0]
    subs = [slice(i * MERGE_SUB, (i + 1) * MERGE_SUB) for i in range(tm // MERGE_SUB)]

    def branches(rows):
        o = of_ref[rows, :] + ob_ref[rows, :]
        parts = []
        for h in range(HG_HEADS):
            oh = o[:, h * HG_DK:(h + 1) * HG_DK]
            parts.append(oh * lax.rsqrt(jnp.mean(oh * oh, axis=-1, keepdims=True) + LN_EPS))
        y_hg = jnp.concatenate(parts, axis=-1) * hgg_ref[...] * _silu(og_ref[rows, :])
        return _dot(y_hg.astype(BF16), wa_ref[...]), _dot(yna_ref[rows, :].astype(BF16), wb_ref[...])

    def out_proj(rows, ya, yb):
        t = jax.nn.sigmoid(ga_ref[rows, :]) * ya + jax.nn.sigmoid(gb_ref[rows, :]) * yb
        return _dot(t.astype(BF16), wo_ref[...])

    def norms_router(rows, i, y):
        x1 = _normalize(alpha * x_ref[rows, :] + g1_ref[...] * y) * ln1g_ref[...] + ln1b_ref[...]
        x1_ref[rows, :] = x1
        h2 = _normalize(x1) * (1.0 + sc2_ref[...]) + sh2_ref[...]
        _store_packed(h2_ref.at[pl.ds(i * MERGE_SUB * PACK_SUBLANES, MERGE_SUB * PACK_SUBLANES), :], h2)
        hh, hm, hl = _split3(h2)
        wh, wm, wl = _split3(wr_ref[...])
        return (_dot_nt(wh, hh) + _dot_nt(wh, hm) + _dot_nt(wm, hh)
                + _dot_nt(wh, hl) + _dot_nt(wl, hh) + _dot_nt(wm, hm))

    ab = [branches(rows) for rows in subs]
    ys = [out_proj(rows, *ab[i]) for i, rows in enumerate(subs)]
    logits = [norms_router(rows, i, ys[i]) for i, rows in enumerate(subs)]

    @pl.when((pl.program_id(0) == 0) & (pl.program_id(1) == 0))
    def _():
        cnt_ref[...] = jnp.zeros_like(cnt_ref)

    for i, rows in enumerate(subs):
        topi, topw, chosen = _route(logits[i], rb_ref[...])
        topi_ref[:, rows] = topi
        topw_ref[:, rows] = topw
        cnt_ref[...] += jnp.sum(chosen, axis=1, keepdims=True)


def _merge(o_f, o_b, p, y_na, x, g1, sh2, sc2, hg_norm_g, ln1_g, ln1_b, w_a, w_b, w_o, w_router_t, router_bias,
           alpha):
    b, s, d = x.shape
    tm = min(MERGE_TOK, s)
    e = w_router_t.shape[0]
    tok = lambda bi, i: (bi, i, 0)
    blk = pl.BlockSpec((None, tm, d), tok)

    def sec(section):
        return pl.BlockSpec((None, None, tm, d), lambda bi, i: (section, bi, i, 0))

    mod = pl.BlockSpec((None, 1, d), lambda bi, i: (bi, 0, 0))
    vec = pl.BlockSpec((1, d), lambda bi, i: (0, 0))
    mat = pl.BlockSpec((d, d), lambda bi, i: (0, 0), pipeline_mode=pl.Buffered(1))
    return pl.pallas_call(
        functools.partial(_merge_kernel, alpha=alpha),
        grid=(b, s // tm),
        in_specs=[blk, blk, sec(SEC_OG), blk, sec(SEC_GA), sec(SEC_GB), blk, mod, mod, mod,
                  vec, vec, vec, mat, mat, mat,
                  pl.BlockSpec((e, d), lambda bi, i: (0, 0)),
                  pl.BlockSpec((e, 1), lambda bi, i: (0, 0))],
        out_specs=[blk,
                   pl.BlockSpec((tm * PACK_SUBLANES, 128), lambda bi, i: (bi * (s // tm) + i, 0)),
                   pl.BlockSpec((None, TOP_K, tm), lambda bi, i: (bi, 0, i)),
                   pl.BlockSpec((None, TOP_K, tm), lambda bi, i: (bi, 0, i)),
                   pl.BlockSpec((e, 128), lambda bi, i: (0, 0))],
        out_shape=[jax.ShapeDtypeStruct((b, s, d), F32),
                   jax.ShapeDtypeStruct((b * s * PACK_SUBLANES, 128), U32),
                   jax.ShapeDtypeStruct((b, TOP_K, s), jnp.int32), jax.ShapeDtypeStruct((b, TOP_K, s), F32),
                   jax.ShapeDtypeStruct((e, 128), F32)],
        compiler_params=_cparams(("arbitrary", "arbitrary")),
        name="merge",
    )(o_f, o_b, p, y_na, p, p, x, g1, sh2, sc2, hg_norm_g.reshape(1, d), ln1_g.reshape(1, d),
      ln1_b.reshape(1, d), w_a, w_b, w_o, w_router_t, router_bias.reshape(e, 1))


MOE_TILE = 512
MOE_TOK = 256


def _plan_kernel(topi_ref, off_ref, dest_ref, carry_ref):
    @pl.when(pl.program_id(0) == 0)
    def _():
        carry_ref[...] = jnp.zeros_like(carry_ref)

    topi = topi_ref[...]
    tok = topi.shape[1]
    eidx = lax.broadcasted_iota(jnp.int32, (N_EXPERTS, tok), 0)
    hits = [eidx == topi[k:k + 1, :] for k in range(TOP_K)]
    m = jnp.zeros((N_EXPERTS, tok), F32)
    for hit in hits:
        m = jnp.where(hit, 1.0, m)
    before = (lax.broadcasted_iota(jnp.int32, (tok, tok), 0)
              < lax.broadcasted_iota(jnp.int32, (tok, tok), 1)).astype(F32).astype(BF16)
    row = off_ref[...] + carry_ref[...] + _dot(m.astype(BF16), before)
    dest = [jnp.sum(jnp.where(hit, row, 0.0), axis=0, keepdims=True) for hit in hits]
    dest_ref[...] = jnp.concatenate(dest, axis=0).astype(jnp.int32)
    carry_ref[...] += jnp.sum(m, axis=1, keepdims=True)


def _plan(topi, seg_off):
    b, k, s = topi.shape
    per_b = s // MOE_TOK
    blk = pl.BlockSpec((None, k, MOE_TOK), lambda i: (i // per_b, 0, i % per_b))
    return pl.pallas_call(
        _plan_kernel,
        grid=(b * per_b,),
        in_specs=[blk, pl.BlockSpec((N_EXPERTS, 1), lambda i: (0, 0))],
        out_specs=blk,
        out_shape=jax.ShapeDtypeStruct((b, k, s), jnp.int32),
        scratch_shapes=[pltpu.VMEM((N_EXPERTS, 1), F32)],
        compiler_params=_cparams(("arbitrary",)),
        name="plan",
    )(topi, seg_off.astype(F32).reshape(N_EXPERTS, 1))


def _row_copy(src_ref, src_row, dst_ref, dst_row, sem):
    src = pl.ds(pl.multiple_of(src_row * PACK_SUBLANES, PACK_SUBLANES), PACK_SUBLANES)
    dst = pl.ds(pl.multiple_of(dst_row * PACK_SUBLANES, PACK_SUBLANES), PACK_SUBLANES)
    return pltpu.make_async_copy(src_ref.at[src, :], dst_ref.at[dst, :], sem)


def _dispatch_kernel(cnt_ref, off_ref, dest_ref, h_ref, xs_ref, zero_ref, sem, pad_sem):
    tok = h_ref.shape[0] // PACK_SUBLANES

    def issue(t, carry):
        for k in range(TOP_K):
            _row_copy(h_ref, t, xs_ref, dest_ref[k, t], sem).start(priority=k % 2)
        return carry

    def drain(t, carry):
        for k in range(TOP_K):
            _row_copy(h_ref, 0, xs_ref, 0, sem).wait()
        return carry

    lax.fori_loop(0, tok, issue, 0)

    @pl.when(pl.program_id(0) == pl.num_programs(0) - 1)
    def _():
        zero_ref[...] = jnp.zeros_like(zero_ref)

        def per_expert(e, carry):
            end = off_ref[e] + cnt_ref[e]
            npad = lax.rem(MOE_TILE - lax.rem(cnt_ref[e], MOE_TILE), MOE_TILE)
            done = jnp.int32(0)
            chunk = MOE_TILE // 2
            while chunk >= 1:
                n_sub = chunk * PACK_SUBLANES
                first = pl.multiple_of((end + done) * PACK_SUBLANES, PACK_SUBLANES)
                fill = pltpu.make_async_copy(zero_ref.at[pl.ds(0, n_sub), :], xs_ref.at[pl.ds(first, n_sub), :],
                                             pad_sem)
                take = (npad & chunk) != 0

                @pl.when(take)
                def _():
                    fill.start()
                    fill.wait()

                done = done + jnp.where(take, chunk, 0)
                chunk //= 2
            return carry

        lax.fori_loop(0, N_EXPERTS, per_expert, 0)

    lax.fori_loop(0, tok, drain, 0)


def _dispatch(h2p, dest, cnt, seg_off, n_rows):
    b, k, s = dest.shape
    per_b = s // MOE_TOK
    grid_spec = pltpu.PrefetchScalarGridSpec(
        num_scalar_prefetch=2,
        grid=(b * per_b,),
        in_specs=[pl.BlockSpec((None, k, MOE_TOK), lambda i, c, o: (i // per_b, 0, i % per_b),
                               memory_space=pltpu.SMEM),
                  pl.BlockSpec((MOE_TOK * PACK_SUBLANES, 128), lambda i, c, o: (i, 0))],
        out_specs=pl.BlockSpec(memory_space=pl.ANY),
        scratch_shapes=[pltpu.VMEM((MOE_TILE // 2 * PACK_SUBLANES, 128), U32),
                        pltpu.SemaphoreType.DMA(()), pltpu.SemaphoreType.DMA(())],
    )
    return pl.pallas_call(
        _dispatch_kernel,
        grid_spec=grid_spec,
        out_shape=jax.ShapeDtypeStruct((n_rows * PACK_SUBLANES, 128), U32),
        compiler_params=_cparams(("arbitrary",)),
        name="dispatch",
    )(cnt, seg_off, dest, h2p)


def _experts_kernel(te_ref, tb_ref, nt_ref, xs_ref, wg_ref, wu_ref, wd_ref, ys_ref):
    @pl.when(pl.program_id(0) < nt_ref[0])
    def _():
        x = _load_packed(xs_ref, MOE_TILE).astype(BF16)
        act = _silu(_dot(x, wg_ref[...].astype(BF16))) * _dot(x, wu_ref[...].astype(BF16))
        _store_packed(ys_ref, _dot(act.astype(BF16), wd_ref[...].astype(BF16)))


def _experts(xs, tile_expert, tile_block, n_tiles, wg, wu, wd):
    d, f = wg.shape[1], wg.shape[2]
    rows = pl.BlockSpec((MOE_TILE * PACK_SUBLANES, 128), lambda i, te, tb, nt: (tb[i], 0))
    grid_spec = pltpu.PrefetchScalarGridSpec(
        num_scalar_prefetch=3,
        grid=(xs.shape[0] // (MOE_TILE * PACK_SUBLANES),),
        in_specs=[rows,
                  pl.BlockSpec((None, d, f), lambda i, te, tb, nt: (te[i], 0, 0)),
                  pl.BlockSpec((None, d, f), lambda i, te, tb, nt: (te[i], 0, 0)),
                  pl.BlockSpec((None, f, d), lambda i, te, tb, nt: (te[i], 0, 0))],
        out_specs=rows,
    )
    return pl.pallas_call(
        _experts_kernel,
        grid_spec=grid_spec,
        out_shape=jax.ShapeDtypeStruct(xs.shape, U32),
        compiler_params=_cparams(("arbitrary",)),
        name="experts",
    )(tile_expert, tile_block, n_tiles, xs, wg, wu, wd)


def _combine_kernel(dest_ref, topw_ref, h_ref, x1_ref, g2_ref, sg_ref, su_ref, sd_ref, ln2g_ref, ln2b_ref,
                    ys_ref, o_ref, buf_ref, sem, *, alpha):
    tok = x1_ref.shape[0]

    def issue(t, carry):
        for k in range(TOP_K):
            _row_copy(ys_ref, dest_ref[k, t], buf_ref.at[k], t, sem).start(priority=k % 2)
        return carry

    def drain(t, carry):
        for k in range(TOP_K):
            _row_copy(ys_ref, 0, buf_ref.at[0], 0, sem).wait()
        return carry

    lax.fori_loop(0, tok, issue, 0)
    h = _load_packed(h_ref, tok).astype(BF16)
    act = _silu(_dot(h, sg_ref[...])) * _dot(h, su_ref[...])
    y = _dot(act.astype(BF16), sd_ref[...])
    w = topw_ref[...].T
    lax.fori_loop(0, tok, drain, 0)
    for k in range(TOP_K):
        y = y + w[:, k:k + 1] * _load_packed(buf_ref.at[k], tok)
    o_ref[...] = _normalize(alpha * x1_ref[...] + g2_ref[...] * y) * ln2g_ref[...] + ln2b_ref[...]


def _combine(ys, dest, topw, h2p, x1, g2, sg, su, sd, ln2_g, ln2_b, alpha):
    t, d = x1.shape
    b, k, s = dest.shape
    per_b = s // MOE_TOK
    fs = sg.shape[1]
    tk = lambda i: (i // per_b, 0, i % per_b)
    rows = pl.BlockSpec((MOE_TOK, d), lambda i: (i, 0))
    packed = pl.BlockSpec((MOE_TOK * PACK_SUBLANES, 128), lambda i: (i, 0))
    vec = pl.BlockSpec((1, d), lambda i: (0, 0))
    return pl.pallas_call(
        functools.partial(_combine_kernel, alpha=alpha),
        grid=(t // MOE_TOK,),
        in_specs=[pl.BlockSpec((None, k, MOE_TOK), tk, memory_space=pltpu.SMEM),
                  pl.BlockSpec((None, k, MOE_TOK), tk),
                  packed, rows,
                  pl.BlockSpec((None, 1, d), lambda i: (i // per_b, 0, 0)),
                  pl.BlockSpec((d, fs), lambda i: (0, 0)),
                  pl.BlockSpec((d, fs), lambda i: (0, 0)),
                  pl.BlockSpec((fs, d), lambda i: (0, 0)),
                  vec, vec,
                  pl.BlockSpec(memory_space=pl.ANY)],
        out_specs=rows,
        out_shape=jax.ShapeDtypeStruct((t, d), F32),
        scratch_shapes=[pltpu.VMEM((k, MOE_TOK * PACK_SUBLANES, 128), U32), pltpu.SemaphoreType.DMA(())],
        compiler_params=_cparams(("arbitrary",)),
        name="combine",
    )(dest, topw, h2p, x1, g2, sg, su, sd, ln2_g.reshape(1, d), ln2_b.reshape(1, d), ys)


def _moe(h2p, topi, topw, cnt, x1, g2, wg, wu, wd, sg, su, sd, ln2_g, ln2_b, alpha):
    b, s, d = x1.shape
    t = b * s
    cnt = cnt[:, 0].astype(jnp.int32)
    tiles_e = (cnt + (MOE_TILE - 1)) // MOE_TILE
    tiles_cum = jnp.cumsum(tiles_e)
    seg_off = (tiles_cum - tiles_e) * MOE_TILE
    n_tiles_max = t * TOP_K // MOE_TILE + N_EXPERTS
    tile_block = jnp.minimum(jnp.arange(n_tiles_max, dtype=jnp.int32), tiles_cum[-1] - 1)
    tile_expert = jnp.sum((tiles_cum[None, :] <= tile_block[:, None]).astype(jnp.int32), axis=1)
    n_tiles = tiles_cum[-1:].astype(jnp.int32)

    dest = _plan(topi, seg_off)
    xs = _dispatch(h2p, dest, cnt, seg_off.astype(jnp.int32), n_tiles_max * MOE_TILE)
    ys = _experts(xs, tile_expert, tile_block, n_tiles, wg, wu, wd)
    out = _combine(ys, dest, topw, h2p, x1.reshape(t, d), g2, sg, su, sd, ln2_g, ln2_b, alpha)
    return out.reshape(b, s, d)


def kernel(x, c, ctx, c_ctx, w_ada, b_ada, w_in, hg_lb_fwd, hg_lb_bwd, hg_norm_g, na_rpb, w_branch_a, w_branch_b, w_out, ln1_g, ln1_b, w_router, router_bias, w_e_gate, w_e_up, w_e_down, w_sh_gate, w_sh_up, w_sh_down, ln2_g, ln2_b):
    depth = w_ada.shape[0]
    assert depth == 1, "single-layer block"
    b, s, d = x.shape
    alpha = (2.0 * depth) ** 0.25
    l = 0
    lb_fwd = jnp.cumsum(jax.nn.softmax(hg_lb_fwd.astype(F32), axis=0), axis=0)[l]
    lb_bwd = jnp.cumsum(jax.nn.softmax(hg_lb_bwd.astype(F32), axis=0), axis=0)[l]

    cond_rows = jnp.concatenate([c, c_ctx[None, :], jnp.zeros((8 - b - 1, d), F32)], axis=0)
    mod = _ada(cond_rows, w_ada[l], b_ada[l])
    sh1, sc1, g1, sh2, sc2, g2 = [m[:b, None, :] for m in jnp.split(mod, 6, axis=-1)]
    csh1, csc1 = [jnp.broadcast_to(m[b:b + 1, None, :], (b, 1, d)) for m in jnp.split(mod, 6, axis=-1)[:2]]

    w_in_b = w_in[l].astype(BF16)
    p = _inproj(x, sh1, sc1, w_in_b)
    pc = _inproj(ctx, csh1, csc1, w_in_b)

    o_f, o_b = _hgrn(p, pc, lb_fwd, lb_bwd)
    y_na = _natten(p, pc, *_na_tables(na_rpb[l], s))

    x1, h2, topi, topw, cnt = _merge(o_f, o_b, p, y_na, x, g1, sh2, sc2, hg_norm_g[l], ln1_g[l], ln1_b[l],
                                     w_branch_a[l].astype(BF16), w_branch_b[l].astype(BF16),
                                     w_out[l].astype(BF16), w_router[l].T, router_bias[l], alpha)

    return _moe(h2, topi, topw, cnt, x1, g2,
                w_e_gate[l], w_e_up[l], w_e_down[l],
                w_sh_gate[l].astype(BF16), w_sh_up[l].astype(BF16), w_sh_down[l].astype(BF16),
                ln2_g[l], ln2_b[l], alpha)
```

```python
import functools

import numpy as np
import jax
import jax.numpy as jnp
from jax import lax
from jax.experimental import pallas as pl
from jax.experimental.pallas import tpu as pltpu
from jax.experimental.pallas import tpu_sc as plsc

F32 = jnp.float32
BF16 = jnp.bfloat16

D_MODEL = 1024
GRID_W = 64
HG_HEADS = 8
HG_DK = 128
HG_CHUNK = 64
NA_HEADS = 16
NA_HD = 64
NA_WIN_R = 8
NA_WIN_C = 16
ROPE_THETA = 10000.0
NEG_INF = -1e30
N_EXPERTS = 64
EXPERT_DIM = 256
TOP_K = 8
N_GROUPS = 8
TOPK_GROUPS = 4
ROUTED_SCALE = 2.5
LN_EPS = 1e-6
N_SECTIONS = 10
SEC_Q, SEC_FF, SEC_FB, SEC_I, SEC_OG, SEC_NQ, SEC_NK, SEC_NV, SEC_GA, SEC_GB = range(10)

VMEM_LIMIT = 56 * 1024 * 1024


def _cparams(sem):
    return pltpu.CompilerParams(dimension_semantics=sem, vmem_limit_bytes=VMEM_LIMIT)


def _normalize(x):
    mu = jnp.mean(x, axis=-1, keepdims=True)
    xc = x - mu
    var = jnp.mean(xc * xc, axis=-1, keepdims=True)
    return xc * lax.rsqrt(var + LN_EPS)


def _silu(x):
    return x * jax.nn.sigmoid(x)


def _dot(a, b):
    return jnp.dot(a, b, preferred_element_type=F32)


def _dot_nt(a, b):
    return lax.dot_general(a, b, (((1,), (1,)), ((), ())), preferred_element_type=F32)


def _dot_tn(a, b):
    return lax.dot_general(a, b, (((0,), (0,)), ((), ())), preferred_element_type=F32)


PACK_SUBLANES = 4
U32 = jnp.uint32


def _pack_words(x):
    half = x.shape[1] // 2
    lo = lax.bitcast_convert_type(x[:, :half].astype(BF16).astype(F32), U32) >> 16
    hi = lax.bitcast_convert_type(x[:, half:].astype(BF16).astype(F32), U32) & jnp.uint32(0xFFFF0000)
    return lo | hi


def _unpack_words(w):
    lo = lax.bitcast_convert_type(w << 16, F32)
    hi = lax.bitcast_convert_type(w & jnp.uint32(0xFFFF0000), F32)
    return jnp.concatenate([lo, hi], axis=-1)


def _store_packed(ref, x):
    w = _pack_words(x)
    for c in range(PACK_SUBLANES):
        ref[pl.ds(c, x.shape[0], stride=PACK_SUBLANES), :] = w[:, c * 128:(c + 1) * 128]


def _load_packed(ref, n):
    return _unpack_words(
        jnp.concatenate([ref[pl.ds(c, n, stride=PACK_SUBLANES), :] for c in range(PACK_SUBLANES)], axis=-1))


def _split3(x):
    hi = x.astype(BF16)
    r1 = x - hi.astype(F32)
    mid = r1.astype(BF16)
    lo = (r1 - mid.astype(F32)).astype(BF16)
    return hi, mid, lo


def _ada_kernel(c_ref, w_ref, b_ref, o_ref):
    cond = _silu(c_ref[...])
    o_ref[...] = _dot(cond.astype(BF16), w_ref[...].astype(BF16)) + b_ref[...]


def _ada(cond_rows, w_ada, b_ada):
    r, d = cond_rows.shape
    n = w_ada.shape[1]
    tn = 1024
    return pl.pallas_call(
        _ada_kernel,
        grid=(n // tn,),
        in_specs=[pl.BlockSpec((r, d), lambda j: (0, 0)),
                  pl.BlockSpec((d, tn), lambda j: (0, j)),
                  pl.BlockSpec((1, tn), lambda j: (0, j))],
        out_specs=pl.BlockSpec((r, tn), lambda j: (0, j)),
        out_shape=jax.ShapeDtypeStruct((r, n), F32),
        compiler_params=_cparams(("arbitrary",)),
        name="ada",
    )(cond_rows, w_ada, b_ada.reshape(1, n))


def _inproj_kernel(x_ref, sh_ref, sc_ref, w_ref, o_ref, h_ref):
    @pl.when(pl.program_id(2) == 0)
    def _():
        h = _normalize(x_ref[...]) * (1.0 + sc_ref[...]) + sh_ref[...]
        h_ref[...] = h.astype(BF16)

    o_ref[...] = _dot(h_ref[...], w_ref[...])


def _inproj(x, shift, scale, w_in_bf16):
    b, s, d = x.shape
    tm = min(1024, s)
    nj = w_in_bf16.shape[1] // d
    return pl.pallas_call(
        _inproj_kernel,
        grid=(b, s // tm, nj),
        in_specs=[pl.BlockSpec((None, tm, d), lambda bi, i, j: (bi, i, 0)),
                  pl.BlockSpec((None, 1, d), lambda bi, i, j: (bi, 0, 0)),
                  pl.BlockSpec((None, 1, d), lambda bi, i, j: (bi, 0, 0)),
                  pl.BlockSpec((d, d), lambda bi, i, j: (0, j))],
        out_specs=pl.BlockSpec((None, None, tm, d), lambda bi, i, j: (j, bi, i, 0)),
        out_shape=jax.ShapeDtypeStruct((nj, b, s, d), F32),
        scratch_shapes=[pltpu.VMEM((tm, d), BF16)],
        compiler_params=_cparams(("arbitrary", "arbitrary", "arbitrary")),
        name="inproj",
    )(x, shift, scale, w_in_bf16)


def _hgrn_gates(q, fraw, v, lb, tri_bf16, last_row):
    f = lb + (1.0 - lb) * jax.nn.sigmoid(fraw)
    k = 1.0 - f
    lf = jnp.log(f)
    hi, mid, lo = _split3(lf)
    a = _dot(tri_bf16, hi) + _dot(tri_bf16, mid) + _dot(tri_bf16, lo)
    a_last = a[last_row:last_row + 1, :]
    kd = (k * jnp.exp(a_last - a)).astype(BF16)
    decay = jnp.exp(a_last)
    qa = kb = None
    if q is not None:
        qa = (_silu(q) * jnp.exp(a)).astype(BF16)
        kb = (k * jnp.exp(-a)).astype(BF16)
    return qa, kb, kd, v.astype(BF16), decay


def _hgrn_chunks(chunks, st_ref):
    first = []
    for d, ((qa, kb, kd, vb, decay), keep) in enumerate(chunks):
        for h in range(HG_HEADS):
            sl = slice(h * HG_DK, (h + 1) * HG_DK)
            st = st_ref[d, h]
            if qa is not None:
                first.append((_dot_nt(qa[:, sl], kb[:, sl]), _dot_nt(qa[:, sl], st.astype(BF16))))
            st_ref[d, h] = st * decay[:, sl] + _dot_tn(vb[:, sl], kd[:, sl])
    results = []
    for d, ((qa, kb, kd, vb, decay), keep) in enumerate(chunks):
        if qa is None:
            results.append(None)
            continue
        outs = []
        for h in range(HG_HEADS):
            sl = slice(h * HG_DK, (h + 1) * HG_DK)
            s_qk, o_state = first.pop(0)
            outs.append(_dot(jnp.where(keep, s_qk, 0.0).astype(BF16), vb[:, sl]) + o_state)
        results.append(jnp.concatenate(outs, axis=-1))
    return results


def _hgrn_kernel(qf_ref, ff_ref, if_ref, qb_ref, fb_ref, ib_ref, cff_ref, cfb_ref, ci_ref,
                 lbf_ref, lbb_ref, of_ref, ob_ref, st_ref, *, n_sub, n_ctx_sub):
    n = pl.program_id(1)
    c = HG_CHUNK
    row = lax.broadcasted_iota(jnp.int32, (c, c), 0)
    col = lax.broadcasted_iota(jnp.int32, (c, c), 1)
    keep_f = col <= row
    keep_b = col >= row
    tri_f = keep_f.astype(F32).astype(BF16)
    tri_b = keep_b.astype(F32).astype(BF16)
    lbf = lbf_ref[...]
    lbb = lbb_ref[...]

    @pl.when(n == 0)
    def _():
        st_ref[...] = jnp.zeros_like(st_ref)

        def body(i, carry):
            r0 = pl.multiple_of(i * c, c)
            r1 = pl.multiple_of((n_ctx_sub - 1 - i) * c, c)
            gf = _hgrn_gates(None, cff_ref[pl.ds(r0, c), :], ci_ref[pl.ds(r0, c), :], lbf, tri_f, c - 1)
            gb = _hgrn_gates(None, cfb_ref[pl.ds(r1, c), :], ci_ref[pl.ds(r1, c), :], lbb, tri_b, 0)
            _hgrn_chunks([(gf, keep_f), (gb, keep_b)], st_ref)
            return carry

        lax.fori_loop(0, n_ctx_sub, body, 0)

    @pl.when(n > 0)
    def _():
        def body(i, carry):
            r0 = pl.multiple_of(i * c, c)
            r1 = pl.multiple_of((n_sub - 1 - i) * c, c)
            gf = _hgrn_gates(qf_ref[pl.ds(r0, c), :], ff_ref[pl.ds(r0, c), :], if_ref[pl.ds(r0, c), :],
                             lbf, tri_f, c - 1)
            gb = _hgrn_gates(qb_ref[pl.ds(r1, c), :], fb_ref[pl.ds(r1, c), :], ib_ref[pl.ds(r1, c), :],
                             lbb, tri_b, 0)
            o_f, o_b = _hgrn_chunks([(gf, keep_f), (gb, keep_b)], st_ref)
            of_ref[pl.ds(r0, c), :] = o_f
            ob_ref[pl.ds(r1, c), :] = o_b
            return carry

        lax.fori_loop(0, n_sub, body, 0)


def _hgrn(p, pc, lb_fwd, lb_bwd):
    _, b, s, w = p.shape
    ctx_len = pc.shape[2]
    tb = min(256, s)
    nb = s // tb
    fwd = lambda bi, n: jnp.maximum(n - 1, 0)
    bwd = lambda bi, n: nb - 1 - jnp.maximum(n - 1, 0)

    def sec(section, blk):
        return pl.BlockSpec((None, None, tb, w), lambda bi, n: (section, bi, blk(bi, n), 0))

    def csec(section):
        return pl.BlockSpec((None, None, ctx_len, w), lambda bi, n: (section, bi, 0, 0))

    vec = pl.BlockSpec((1, w), lambda bi, n: (0, 0))
    kern = functools.partial(_hgrn_kernel, n_sub=tb // HG_CHUNK, n_ctx_sub=ctx_len // HG_CHUNK)
    return pl.pallas_call(
        kern,
        grid=(b, nb + 1),
        in_specs=[sec(SEC_Q, fwd), sec(SEC_FF, fwd), sec(SEC_I, fwd),
                  sec(SEC_Q, bwd), sec(SEC_FB, bwd), sec(SEC_I, bwd),
                  csec(SEC_FF), csec(SEC_FB), csec(SEC_I), vec, vec],
        out_specs=[pl.BlockSpec((None, tb, w), lambda bi, n: (bi, fwd(bi, n), 0)),
                   pl.BlockSpec((None, tb, w), lambda bi, n: (bi, bwd(bi, n), 0))],
        out_shape=[jax.ShapeDtypeStruct((b, s, w), F32), jax.ShapeDtypeStruct((b, s, w), F32)],
        scratch_shapes=[pltpu.VMEM((2, HG_HEADS, HG_DK, HG_DK), F32)],
        compiler_params=_cparams(("arbitrary", "arbitrary")),
        name="hgrn",
    )(p, p, p, p, p, p, pc, pc, pc, lb_fwd.reshape(1, w), lb_bwd.reshape(1, w))


NA_ROWS_PER_STEP = 8
NA_PREP_ROWS = 512
NA_KEY_TILE = 128
NA_SPAN = (NA_WIN_R + 2) * GRID_W


def _rope(t, cos, sin_signed, first_half):
    w = t.shape[-1]
    partner = jnp.where(first_half, pltpu.roll(t, w - 16, 1), pltpu.roll(t, 16, 1))
    return t * cos + partner * sin_signed


def _fold_lanes(op, *arrays):
    tiles = [a[:, c:c + 128] for a in arrays for c in range(0, a.shape[-1], 128)]
    acc = tiles[0]
    for t in tiles[1:]:
        acc = op(acc, t)
    return acc


def _rope_tables(rowtab_ref, coltab_ref, row0, n_rows, row_lane):
    out = []
    for i in range(2):
        rt = rowtab_ref[i, pl.ds(row0, n_rows), :]
        by_row = jnp.concatenate([jnp.broadcast_to(rt[r:r + 1, :], (GRID_W, rt.shape[1])) for r in range(n_rows)],
                                 axis=0)
        by_col = jnp.concatenate([coltab_ref[i]] * n_rows, axis=0)
        out.append(jnp.where(row_lane, by_row, by_col))
    return out


def _natten_kernel(q_ref, k_ref, v_ref, kc_ref, vc_ref, rowtab_ref, coltab_ref, t2_ref, o_ref,
                   kt_s, v_s, kc_s, vc_s, bias_s, *, rows):
    rblk = pl.program_id(2)
    hd = NA_HD
    lane = lax.broadcasted_iota(jnp.int32, (1, 2 * hd), 1)
    first_half = (lane % 32) < 16
    row_lane = (lane % hd) < hd // 2
    scale = NA_HD ** -0.5

    def values_and_ones(v_pair, h):
        vh = v_pair if h == 0 else pltpu.roll(v_pair, hd, 1)
        return jnp.where(lane < hd, vh, jnp.where(lane == hd, 1.0, 0.0)).astype(BF16)

    @pl.when(rblk == 0)
    def _():
        kc = kc_ref[...].astype(BF16)
        qi = lax.broadcasted_iota(jnp.int32, (GRID_W, GRID_W), 0)
        ki = lax.broadcasted_iota(jnp.int32, (GRID_W, GRID_W), 1)
        cstart = jnp.clip(qi - NA_WIN_C // 2, 0, GRID_W - NA_WIN_C)
        in_win = (ki >= cstart) & (ki < cstart + NA_WIN_C)
        masked = jnp.full((GRID_W, GRID_W), NEG_INF, F32)
        s_len = k_ref.shape[0]
        for h in range(2):
            sl = slice(h * hd, (h + 1) * hd)
            kc_s[h] = kc[:, sl]
            vc_s[h] = values_and_ones(vc_ref[...], h)
            kt_s[h, s_len // NA_KEY_TILE] = jnp.zeros((hd, NA_KEY_TILE), BF16)
            v_s[h, s_len:s_len + NA_KEY_TILE, :] = jnp.zeros((NA_KEY_TILE, 2 * hd), BF16)
            tiles = [jnp.where(in_win, t2_ref[h, dr], NEG_INF) for dr in range(2 * NA_WIN_R - 1)]
            for bidx in range(NA_WIN_R + 1):
                v, par = (bidx, 0) if bidx < NA_WIN_R else (NA_WIN_R // 2, 1)
                for piece in range(NA_SPAN // GRID_W):
                    j = piece - par
                    tile = tiles[NA_WIN_R - 1 - v + j] if 0 <= j < NA_WIN_R else masked
                    bias_s[h, bidx, :, piece * GRID_W:(piece + 1) * GRID_W] = tile

        def prep(i, carry):
            r0 = pl.multiple_of(i * NA_PREP_ROWS, NA_PREP_ROWS)
            rws = pl.ds(r0, NA_PREP_ROWS)
            cos, sin = _rope_tables(rowtab_ref, coltab_ref, i * (NA_PREP_ROWS // GRID_W), NA_PREP_ROWS // GRID_W,
                                    row_lane)
            kr = _rope(k_ref[rws, :], cos, sin, first_half)
            krt = kr.T.astype(BF16)
            vv = v_ref[rws, :]
            for h in range(2):
                sl = slice(h * hd, (h + 1) * hd)
                for c in range(NA_PREP_ROWS // NA_KEY_TILE):
                    kt_s[h, i * (NA_PREP_ROWS // NA_KEY_TILE) + c] = krt[sl, c * NA_KEY_TILE:(c + 1) * NA_KEY_TILE]
                v_s[h, rws, :] = values_and_ones(vv, h)
            return carry

        lax.fori_loop(0, s_len // NA_PREP_ROWS, prep, 0)

    tq = NA_ROWS_PER_STEP * GRID_W
    q = q_ref[...] * scale
    cos, sin = _rope_tables(rowtab_ref, coltab_ref, rblk * NA_ROWS_PER_STEP, NA_ROWS_PER_STEP, row_lane)
    qr = _rope(q, cos, sin, first_half)
    qb = q.astype(BF16)
    qrb = qr.astype(BF16)
    rws = [slice(rr * GRID_W, (rr + 1) * GRID_W) for rr in range(NA_ROWS_PER_STEP)]
    tile0, bidx = [], []
    for rr in range(NA_ROWS_PER_STEP):
        r = rblk * NA_ROWS_PER_STEP + rr
        rs = jnp.clip(r - NA_WIN_R // 2, 0, rows - NA_WIN_R)
        tile0.append(lax.shift_right_logical(rs, 1))
        bidx.append(jnp.where((rs & 1) == 1, NA_WIN_R, r - rs))

    def scores(h):
        sl = slice(h * hd, (h + 1) * hd)
        qrb_h = qrb[:, sl]
        s_ctx_all = _dot_nt(qb[:, sl], kc_s[h])
        s_win = []
        for rr in range(NA_ROWS_PER_STEP):
            kt = kt_s[h, pl.ds(tile0[rr], NA_SPAN // NA_KEY_TILE)]
            kt = jnp.concatenate([kt[c] for c in range(NA_SPAN // NA_KEY_TILE)], axis=-1)
            s_win.append(_dot(qrb_h[rws[rr]], kt))
        return s_win, s_ctx_all

    def softmax(h, s_win, s_ctx_all):
        e_win, e_ctx = [], []
        for rr in range(NA_ROWS_PER_STEP):
            sw = s_win[rr] + bias_s[h, bidx[rr]]
            sc = s_ctx_all[rws[rr]]
            m = jnp.max(_fold_lanes(jnp.maximum, sw, sc), axis=-1, keepdims=True)
            e_win.append(jnp.exp(sw - m).astype(BF16))
            e_ctx.append(jnp.exp(sc - m).astype(BF16))
        return e_win, e_ctx

    def values(h, e_win, e_ctx):
        o_win = []
        for rr in range(NA_ROWS_PER_STEP):
            k0 = pl.multiple_of(tile0[rr] * NA_KEY_TILE, NA_KEY_TILE)
            o_win.append(_dot(e_win[rr], v_s[h, pl.ds(k0, NA_SPAN), :]))
        o = jnp.concatenate(o_win, axis=0) + _dot(jnp.concatenate(e_ctx, axis=0), vc_s[h])
        return o[:, :hd] * (1.0 / o[:, hd:hd + 1])

    s0 = scores(0)
    s1 = scores(1)
    p0 = softmax(0, *s0)
    o0 = values(0, *p0)
    p1 = softmax(1, *s1)
    o1 = values(1, *p1)
    o_ref[...] = jnp.concatenate([o0, o1], axis=-1)


def _na_tables(rpb, s):
    half = NA_HD // 2
    inv = jnp.power(ROPE_THETA, -jnp.arange(0, half, 2, dtype=F32) / half)

    def tables(n):
        ang = jnp.arange(n, dtype=F32)[:, None] * inv[None, :]
        reps = 2 * NA_HD // half
        return jnp.stack([jnp.tile(jnp.cos(ang), (1, 2 * reps)),
                          jnp.tile(jnp.concatenate([-jnp.sin(ang), jnp.sin(ang)], axis=-1), (1, reps))])

    rowtab, coltab = tables(s // GRID_W), tables(GRID_W)

    pad = GRID_W - NA_WIN_C
    rp = jnp.pad(rpb.astype(F32), ((0, 0), (0, 0), (pad, pad)), mode="edge")
    t2 = jnp.stack([rp[:, :, GRID_W - 1 - qc:2 * GRID_W - 1 - qc] for qc in range(GRID_W)], axis=2)
    return rowtab, coltab, t2


def _natten(p, pc, rowtab, coltab, t2):
    _, b, s, w = p.shape
    ctx_len = pc.shape[2]
    rows = s // GRID_W
    assert rows >= NA_WIN_R and rows % NA_ROWS_PER_STEP == 0
    tq = NA_ROWS_PER_STEP * GRID_W
    hw = 2 * NA_HD
    nhp = w // hw
    kern = functools.partial(_natten_kernel, rows=rows)
    return pl.pallas_call(
        kern,
        grid=(b, nhp, rows // NA_ROWS_PER_STEP),
        in_specs=[pl.BlockSpec((None, None, tq, hw), lambda bi, hp, r: (SEC_NQ, bi, r, hp)),
                  pl.BlockSpec((None, None, s, hw), lambda bi, hp, r: (SEC_NK, bi, 0, hp)),
                  pl.BlockSpec((None, None, s, hw), lambda bi, hp, r: (SEC_NV, bi, 0, hp)),
                  pl.BlockSpec((None, None, ctx_len, hw), lambda bi, hp, r: (SEC_NK, bi, 0, hp)),
                  pl.BlockSpec((None, None, ctx_len, hw), lambda bi, hp, r: (SEC_NV, bi, 0, hp)),
                  pl.BlockSpec((2, rows, hw), lambda bi, hp, r: (0, 0, 0)),
                  pl.BlockSpec((2, GRID_W, hw), lambda bi, hp, r: (0, 0, 0)),
                  pl.BlockSpec((2, 2 * NA_WIN_R - 1, GRID_W, GRID_W), lambda bi, hp, r: (hp, 0, 0, 0))],
        out_specs=pl.BlockSpec((None, tq, hw), lambda bi, hp, r: (bi, r, hp)),
        out_shape=jax.ShapeDtypeStruct((b, s, w), F32),
        scratch_shapes=[pltpu.VMEM((2, s // NA_KEY_TILE + 1, NA_HD, NA_KEY_TILE), BF16),
                        pltpu.VMEM((2, s + NA_KEY_TILE, hw), BF16),
                        pltpu.VMEM((2, ctx_len, NA_HD), BF16), pltpu.VMEM((2, ctx_len, hw), BF16),
                        pltpu.VMEM((2, NA_WIN_R + 1, GRID_W, NA_SPAN), F32)],
        compiler_params=_cparams(("arbitrary", "arbitrary", "arbitrary")),
        name="natten",
    )(p, p, p, pc, pc, rowtab, coltab, t2)


def _route(logits_t, rbias):
    e, t = logits_t.shape
    gsz = e // N_GROUPS
    scores = jax.nn.sigmoid(logits_t)
    sel = scores + rbias
    neg = -jnp.inf
    sub = lax.broadcasted_iota(jnp.int32, (gsz, t), 0).astype(F32)
    gscore = []
    for g in range(N_GROUPS):
        grp = sel[g * gsz:(g + 1) * gsz, :]
        m1 = jnp.max(grp, axis=0, keepdims=True)
        first = jnp.min(jnp.where(grp == m1, sub, float(gsz)), axis=0, keepdims=True)
        m2 = jnp.max(jnp.where(sub == first, neg, grp), axis=0, keepdims=True)
        gscore.append(m1 + m2)
    masked = []
    for g in range(N_GROUPS):
        rank = jnp.zeros((1, t), F32)
        for g2 in range(N_GROUPS):
            if g2 == g:
                continue
            if g2 < g:
                ahead = gscore[g2] >= gscore[g]
            else:
                ahead = gscore[g2] > gscore[g]
            rank = rank + jnp.where(ahead, 1.0, 0.0)
        masked.append(jnp.where(rank < TOPK_GROUPS, sel[g * gsz:(g + 1) * gsz, :], neg))
    work = jnp.concatenate(masked, axis=0)
    eidx = lax.broadcasted_iota(jnp.int32, (e, t), 0).astype(F32)
    idxs, ws = [], []
    chosen = jnp.zeros((e, t), F32)
    for _ in range(TOP_K):
        m = jnp.max(work, axis=0, keepdims=True)
        first = jnp.min(jnp.where(work == m, eidx, float(e)), axis=0, keepdims=True)
        pick = eidx == first
        idxs.append(first)
        ws.append(jnp.sum(jnp.where(pick, scores, 0.0), axis=0, keepdims=True))
        chosen = jnp.where(pick, 1.0, chosen)
        work = jnp.where(pick, neg, work)
    w = jnp.concatenate(ws, axis=0)
    w = w / jnp.sum(w, axis=0, keepdims=True) * ROUTED_SCALE
    return jnp.concatenate(idxs, axis=0).astype(jnp.int32), w, chosen


MERGE_TOK = 512
MERGE_SUB = 256


def _merge_kernel(of_ref, ob_ref, og_ref, yna_ref, ga_ref, gb_ref, x_ref, g1_ref, sh2_ref, sc2_ref,
                  hgg_ref, ln1g_ref, ln1b_ref, wa_ref, wb_ref, wo_ref, wr_ref, rb_ref,
                  x1_ref, h2_ref, topi_ref, topw_ref, cnt_ref, *, alpha):
    tm = x_ref.shape[0]
    subs = [slice(i * MERGE_SUB, (i + 1) * MERGE_SUB) for i in range(tm // MERGE_SUB)]

    def branches(rows):
        o = of_ref[rows, :] + ob_ref[rows, :]
        parts = []
        for h in range(HG_HEADS):
            oh = o[:, h * HG_DK:(h + 1) * HG_DK]
            parts.append(oh * lax.rsqrt(jnp.mean(oh * oh, axis=-1, keepdims=True) + LN_EPS))
        y_hg = jnp.concatenate(parts, axis=-1) * hgg_ref[...] * _silu(og_ref[rows, :])
        return _dot(y_hg.astype(BF16), wa_ref[...]), _dot(yna_ref[rows, :].astype(BF16), wb_ref[...])

    def out_proj(rows, ya, yb):
        t = jax.nn.sigmoid(ga_ref[rows, :]) * ya + jax.nn.sigmoid(gb_ref[rows, :]) * yb
        return _dot(t.astype(BF16), wo_ref[...])

    def norms_router(rows, i, y):
        x1 = _normalize(alpha * x_ref[rows, :] + g1_ref[...] * y) * ln1g_ref[...] + ln1b_ref[...]
        x1_ref[rows, :] = x1
        h2 = _normalize(x1) * (1.0 + sc2_ref[...]) + sh2_ref[...]
        _store_packed(h2_ref.at[pl.ds(i * MERGE_SUB * PACK_SUBLANES, MERGE_SUB * PACK_SUBLANES), :], h2)
        hh, hm, hl = _split3(h2)
        wh, wm, wl = _split3(wr_ref[...])
        return (_dot_nt(wh, hh) + _dot_nt(wh, hm) + _dot_nt(wm, hh)
                + _dot_nt(wh, hl) + _dot_nt(wl, hh) + _dot_nt(wm, hm))

    ab = [branches(rows) for rows in subs]
    ys = [out_proj(rows, *ab[i]) for i, rows in enumerate(subs)]
    logits = [norms_router(rows, i, ys[i]) for i, rows in enumerate(subs)]

    @pl.when((pl.program_id(0) == 0) & (pl.program_id(1) == 0))
    def _():
        cnt_ref[...] = jnp.zeros_like(cnt_ref)

    for i, rows in enumerate(subs):
        topi, topw, chosen = _route(logits[i], rb_ref[...])
        topi_ref[:, rows] = topi
        topw_ref[:, rows] = topw
        cnt_ref[...] += jnp.sum(chosen, axis=1, keepdims=True)


def _merge(o_f, o_b, p, y_na, x, g1, sh2, sc2, hg_norm_g, ln1_g, ln1_b, w_a, w_b, w_o, w_router_t, router_bias,
           alpha):
    b, s, d = x.shape
    tm = min(MERGE_TOK, s)
    e = w_router_t.shape[0]
    tok = lambda bi, i: (bi, i, 0)
    blk = pl.BlockSpec((None, tm, d), tok)

    def sec(section):
        return pl.BlockSpec((None, None, tm, d), lambda bi, i: (section, bi, i, 0))

    mod = pl.BlockSpec((None, 1, d), lambda bi, i: (bi, 0, 0))
    vec = pl.BlockSpec((1, d), lambda bi, i: (0, 0))
    mat = pl.BlockSpec((d, d), lambda bi, i: (0, 0), pipeline_mode=pl.Buffered(1))
    return pl.pallas_call(
        functools.partial(_merge_kernel, alpha=alpha),
        grid=(b, s // tm),
        in_specs=[blk, blk, sec(SEC_OG), blk, sec(SEC_GA), sec(SEC_GB), blk, mod, mod, mod,
                  vec, vec, vec, mat, mat, mat,
                  pl.BlockSpec((e, d), lambda bi, i: (0, 0)),
                  pl.BlockSpec((e, 1), lambda bi, i: (0, 0))],
        out_specs=[blk,
                   pl.BlockSpec((tm * PACK_SUBLANES, 128), lambda bi, i: (bi * (s // tm) + i, 0)),
                   pl.BlockSpec((None, TOP_K, tm), lambda bi, i: (bi, 0, i)),
                   pl.BlockSpec((None, TOP_K, tm), lambda bi, i: (bi, 0, i)),
                   pl.BlockSpec((e, 128), lambda bi, i: (0, 0))],
        out_shape=[jax.ShapeDtypeStruct((b, s, d), F32),
                   jax.ShapeDtypeStruct((b * s * PACK_SUBLANES, 128), U32),
                   jax.ShapeDtypeStruct((b, TOP_K, s), jnp.int32), jax.ShapeDtypeStruct((b, TOP_K, s), F32),
                   jax.ShapeDtypeStruct((e, 128), F32)],
        compiler_params=_cparams(("arbitrary", "arbitrary")),
        name="merge",
    )(o_f, o_b, p, y_na, p, p, x, g1, sh2, sc2, hg_norm_g.reshape(1, d), ln1_g.reshape(1, d),
      ln1_b.reshape(1, d), w_a, w_b, w_o, w_router_t, router_bias.reshape(e, 1))


MOE_TILE = 512
MOE_TOK = 256


def _plan_kernel(topi_ref, off_ref, dest_ref, carry_ref):
    @pl.when(pl.program_id(0) == 0)
    def _():
        carry_ref[...] = jnp.zeros_like(carry_ref)

    topi = topi_ref[...]
    tok = topi.shape[1]
    eidx = lax.broadcasted_iota(jnp.int32, (N_EXPERTS, tok), 0)
    hits = [eidx == topi[k:k + 1, :] for k in range(TOP_K)]
    m = jnp.zeros((N_EXPERTS, tok), F32)
    for hit in hits:
        m = jnp.where(hit, 1.0, m)
    before = (lax.broadcasted_iota(jnp.int32, (tok, tok), 0)
              < lax.broadcasted_iota(jnp.int32, (tok, tok), 1)).astype(F32).astype(BF16)
    row = off_ref[...] + carry_ref[...] + _dot(m.astype(BF16), before)
    dest = [jnp.sum(jnp.where(hit, row, 0.0), axis=0, keepdims=True) for hit in hits]
    dest_ref[...] = jnp.concatenate(dest, axis=0).astype(jnp.int32)
    carry_ref[...] += jnp.sum(m, axis=1, keepdims=True)


def _plan(topi, seg_off):
    b, k, s = topi.shape
    per_b = s // MOE_TOK
    blk = pl.BlockSpec((None, k, MOE_TOK), lambda i: (i // per_b, 0, i % per_b))
    return pl.pallas_call(
        _plan_kernel,
        grid=(b * per_b,),
        in_specs=[blk, pl.BlockSpec((N_EXPERTS, 1), lambda i: (0, 0))],
        out_specs=blk,
        out_shape=jax.ShapeDtypeStruct((b, k, s), jnp.int32),
        scratch_shapes=[pltpu.VMEM((N_EXPERTS, 1), F32)],
        compiler_params=_cparams(("arbitrary",)),
        name="plan",
    )(topi, seg_off.astype(F32).reshape(N_EXPERTS, 1))


def _row_copy(src_ref, src_row, dst_ref, dst_row, sem):
    src = pl.ds(pl.multiple_of(src_row * PACK_SUBLANES, PACK_SUBLANES), PACK_SUBLANES)
    dst = pl.ds(pl.multiple_of(dst_row * PACK_SUBLANES, PACK_SUBLANES), PACK_SUBLANES)
    return pltpu.make_async_copy(src_ref.at[src, :], dst_ref.at[dst, :], sem)


def _dispatch_kernel(cnt_ref, off_ref, dest_ref, h_ref, xs_ref, zero_ref, sem, pad_sem):
    tok = h_ref.shape[0] // PACK_SUBLANES

    def issue(t, carry):
        for k in range(TOP_K):
            _row_copy(h_ref, t, xs_ref, dest_ref[k, t], sem).start(priority=k % 2)
        return carry

    def drain(t, carry):
        for k in range(TOP_K):
            _row_copy(h_ref, 0, xs_ref, 0, sem).wait()
        return carry

    lax.fori_loop(0, tok, issue, 0)

    @pl.when(pl.program_id(0) == pl.num_programs(0) - 1)
    def _():
        zero_ref[...] = jnp.zeros_like(zero_ref)

        def per_expert(e, carry):
            end = off_ref[e] + cnt_ref[e]
            npad = lax.rem(MOE_TILE - lax.rem(cnt_ref[e], MOE_TILE), MOE_TILE)
            done = jnp.int32(0)
            chunk = MOE_TILE // 2
            while chunk >= 1:
                n_sub = chunk * PACK_SUBLANES
                first = pl.multiple_of((end + done) * PACK_SUBLANES, PACK_SUBLANES)
                fill = pltpu.make_async_copy(zero_ref.at[pl.ds(0, n_sub), :], xs_ref.at[pl.ds(first, n_sub), :],
                                             pad_sem)
                take = (npad & chunk) != 0

                @pl.when(take)
                def _():
                    fill.start()
                    fill.wait()

                done = done + jnp.where(take, chunk, 0)
                chunk //= 2
            return carry

        lax.fori_loop(0, N_EXPERTS, per_expert, 0)

    lax.fori_loop(0, tok, drain, 0)


def _dispatch(h2p, dest, cnt, seg_off, n_rows):
    b, k, s = dest.shape
    per_b = s // MOE_TOK
    grid_spec = pltpu.PrefetchScalarGridSpec(
        num_scalar_prefetch=2,
        grid=(b * per_b,),
        in_specs=[pl.BlockSpec((None, k, MOE_TOK), lambda i, c, o: (i // per_b, 0, i % per_b),
                               memory_space=pltpu.SMEM),
                  pl.BlockSpec((MOE_TOK * PACK_SUBLANES, 128), lambda i, c, o: (i, 0))],
        out_specs=pl.BlockSpec(memory_space=pl.ANY),
        scratch_shapes=[pltpu.VMEM((MOE_TILE // 2 * PACK_SUBLANES, 128), U32),
                        pltpu.SemaphoreType.DMA(()), pltpu.SemaphoreType.DMA(())],
    )
    return pl.pallas_call(
        _dispatch_kernel,
        grid_spec=grid_spec,
        out_shape=jax.ShapeDtypeStruct((n_rows * PACK_SUBLANES, 128), U32),
        compiler_params=_cparams(("arbitrary",)),
        name="dispatch",
    )(cnt, seg_off, dest, h2p)


def _experts_kernel(te_ref, tb_ref, nt_ref, xs_ref, wg_ref, wu_ref, wd_ref, ys_ref):
    @pl.when(pl.program_id(0) < nt_ref[0])
    def _():
        x = _load_packed(xs_ref, MOE_TILE).astype(BF16)
        act = _silu(_dot(x, wg_ref[...].astype(BF16))) * _dot(x, wu_ref[...].astype(BF16))
        ys_ref[...] = _pack_words(_dot(act.astype(BF16), wd_ref[...].astype(BF16)))


def _experts(xs, tile_expert, tile_block, n_tiles, wg, wu, wd):
    d, f = wg.shape[1], wg.shape[2]
    n_rows = xs.shape[0] // PACK_SUBLANES
    rows = pl.BlockSpec((MOE_TILE * PACK_SUBLANES, 128), lambda i, te, tb, nt: (tb[i], 0))
    grid_spec = pltpu.PrefetchScalarGridSpec(
        num_scalar_prefetch=3,
        grid=(xs.shape[0] // (MOE_TILE * PACK_SUBLANES),),
        in_specs=[rows,
                  pl.BlockSpec((None, d, f), lambda i, te, tb, nt: (te[i], 0, 0)),
                  pl.BlockSpec((None, d, f), lambda i, te, tb, nt: (te[i], 0, 0)),
                  pl.BlockSpec((None, f, d), lambda i, te, tb, nt: (te[i], 0, 0))],
        out_specs=pl.BlockSpec((MOE_TILE, d // 2), lambda i, te, tb, nt: (tb[i], 0)),
    )
    return pl.pallas_call(
        _experts_kernel,
        grid_spec=grid_spec,
        out_shape=jax.ShapeDtypeStruct((n_rows, d // 2), U32),
        compiler_params=_cparams(("arbitrary",)),
        name="experts",
    )(tile_expert, tile_block, n_tiles, xs, wg, wu, wd)


SC_GATHER_WINDOW = 128


def _gather_rows(table, idx):
    m = idx.shape[0]
    w = table.shape[1]
    info = plsc.get_sparse_core_info()
    n_workers = info.num_cores * info.num_subcores
    per_worker = m // n_workers
    assert per_worker * n_workers == m and per_worker % SC_GATHER_WINDOW == 0
    mesh = plsc.VectorSubcoreMesh(core_axis_name="core", subcore_axis_name="subcore")

    @functools.partial(
        pl.kernel, mesh=mesh, out_type=jax.ShapeDtypeStruct((m, w), table.dtype),
        scratch_types=[pltpu.VMEM((SC_GATHER_WINDOW,), jnp.int32), pltpu.VMEM((SC_GATHER_WINDOW, w), table.dtype),
                       pltpu.SemaphoreType.DMA])
    def gather(table_hbm, idx_hbm, out_hbm, idx_v, rows_v, sem):
        worker = lax.axis_index("subcore") * info.num_cores + lax.axis_index("core")

        @pl.loop(0, per_worker // SC_GATHER_WINDOW)
        def _(step):
            base = pl.multiple_of(worker * per_worker + step * SC_GATHER_WINDOW, SC_GATHER_WINDOW)
            pltpu.sync_copy(idx_hbm.at[pl.ds(base, SC_GATHER_WINDOW)], idx_v)
            pltpu.async_copy(table_hbm.at[idx_v], rows_v, sem).wait()
            pltpu.sync_copy(rows_v, out_hbm.at[pl.ds(base, SC_GATHER_WINDOW)])

    return gather(table, idx)


def _combine_kernel(rows_ref, topw_ref, h_ref, x1_ref, g2_ref, sg_ref, su_ref, sd_ref, ln2g_ref, ln2b_ref,
                    o_ref, *, alpha):
    tok = x1_ref.shape[0]
    h = _load_packed(h_ref, tok).astype(BF16)
    act = _silu(_dot(h, sg_ref[...])) * _dot(h, su_ref[...])
    y = _dot(act.astype(BF16), sd_ref[...])
    w = topw_ref[...].T
    for k in range(TOP_K):
        y = y + w[:, k:k + 1] * _unpack_words(rows_ref[k])
    o_ref[...] = _normalize(alpha * x1_ref[...] + g2_ref[...] * y) * ln2g_ref[...] + ln2b_ref[...]


def _combine(gathered, topw, h2p, x1, g2, sg, su, sd, ln2_g, ln2_b, alpha):
    t, d = x1.shape
    b, k, s = topw.shape
    per_b = s // MOE_TOK
    fs = sg.shape[1]
    rows = pl.BlockSpec((MOE_TOK, d), lambda i: (i, 0))
    packed = pl.BlockSpec((MOE_TOK * PACK_SUBLANES, 128), lambda i: (i, 0))
    vec = pl.BlockSpec((1, d), lambda i: (0, 0))
    return pl.pallas_call(
        functools.partial(_combine_kernel, alpha=alpha),
        grid=(t // MOE_TOK,),
        in_specs=[pl.BlockSpec((k, MOE_TOK, d // 2), lambda i: (0, i, 0)),
                  pl.BlockSpec((None, k, MOE_TOK), lambda i: (i // per_b, 0, i % per_b)),
                  packed, rows,
                  pl.BlockSpec((None, 1, d), lambda i: (i // per_b, 0, 0)),
                  pl.BlockSpec((d, fs), lambda i: (0, 0)),
                  pl.BlockSpec((d, fs), lambda i: (0, 0)),
                  pl.BlockSpec((fs, d), lambda i: (0, 0)),
                  vec, vec],
        out_specs=rows,
        out_shape=jax.ShapeDtypeStruct((t, d), F32),
        compiler_params=_cparams(("arbitrary",)),
        name="combine",
    )(gathered, topw, h2p, x1, g2, sg, su, sd, ln2_g.reshape(1, d), ln2_b.reshape(1, d))


def _moe(h2p, topi, topw, cnt, x1, g2, wg, wu, wd, sg, su, sd, ln2_g, ln2_b, alpha):
    b, s, d = x1.shape
    t = b * s
    cnt = cnt[:, 0].astype(jnp.int32)
    tiles_e = (cnt + (MOE_TILE - 1)) // MOE_TILE
    tiles_cum = jnp.cumsum(tiles_e)
    seg_off = (tiles_cum - tiles_e) * MOE_TILE
    n_tiles_max = t * TOP_K // MOE_TILE + N_EXPERTS
    tile_block = jnp.minimum(jnp.arange(n_tiles_max, dtype=jnp.int32), tiles_cum[-1] - 1)
    tile_expert = jnp.sum((tiles_cum[None, :] <= tile_block[:, None]).astype(jnp.int32), axis=1)
    n_tiles = tiles_cum[-1:].astype(jnp.int32)

    dest = _plan(topi, seg_off)
    xs = _dispatch(h2p, dest, cnt, seg_off.astype(jnp.int32), n_tiles_max * MOE_TILE)
    ys = _experts(xs, tile_expert, tile_block, n_tiles, wg, wu, wd)
    gathered = _gather_rows(ys, jnp.transpose(dest, (1, 0, 2)).reshape(TOP_K * t))
    out = _combine(gathered.reshape(TOP_K, t, d // 2), topw, h2p, x1.reshape(t, d), g2, sg, su, sd,
                   ln2_g, ln2_b, alpha)
    return out.reshape(b, s, d)


def kernel(x, c, ctx, c_ctx, w_ada, b_ada, w_in, hg_lb_fwd, hg_lb_bwd, hg_norm_g, na_rpb, w_branch_a, w_branch_b, w_out, ln1_g, ln1_b, w_router, router_bias, w_e_gate, w_e_up, w_e_down, w_sh_gate, w_sh_up, w_sh_down, ln2_g, ln2_b):
    depth = w_ada.shape[0]
    assert depth == 1, "single-layer block"
    b, s, d = x.shape
    alpha = (2.0 * depth) ** 0.25
    l = 0
    lb_fwd = jnp.cumsum(jax.nn.softmax(hg_lb_fwd.astype(F32), axis=0), axis=0)[l]
    lb_bwd = jnp.cumsum(jax.nn.softmax(hg_lb_bwd.astype(F32), axis=0), axis=0)[l]

    cond_rows = jnp.concatenate([c, c_ctx[None, :], jnp.zeros((8 - b - 1, d), F32)], axis=0)
    mod = _ada(cond_rows, w_ada[l], b_ada[l])
    sh1, sc1, g1, sh2, sc2, g2 = [m[:b, None, :] for m in jnp.split(mod, 6, axis=-1)]
    csh1, csc1 = [jnp.broadcast_to(m[b:b + 1, None, :], (b, 1, d)) for m in jnp.split(mod, 6, axis=-1)[:2]]

    w_in_b = w_in[l].astype(BF16)
    p = _inproj(x, sh1, sc1, w_in_b)
    pc = _inproj(ctx, csh1, csc1, w_in_b)

    o_f, o_b = _hgrn(p, pc, lb_fwd, lb_bwd)
    y_na = _natten(p, pc, *_na_tables(na_rpb[l], s))

    x1, h2, topi, topw, cnt = _merge(o_f, o_b, p, y_na, x, g1, sh2, sc2, hg_norm_g[l], ln1_g[l], ln1_b[l],
                                     w_branch_a[l].astype(BF16), w_branch_b[l].astype(BF16),
                                     w_out[l].astype(BF16), w_router[l].T, router_bias[l], alpha)

    return _moe(h2, topi, topw, cnt, x1, g2,
                w_e_gate[l], w_e_up[l], w_e_down[l],
                w_sh_gate[l].astype(BF16), w_sh_up[l].astype(BF16), w_sh_down[l].astype(BF16),
                ln2_g[l], ln2_b[l], alpha)
```

```python
import functools

import numpy as np
import jax
import jax.numpy as jnp
from jax import lax
from jax.experimental import pallas as pl
from jax.experimental.pallas import tpu as pltpu
from jax.experimental.pallas import tpu_sc as plsc

F32 = jnp.float32
BF16 = jnp.bfloat16

D_MODEL = 1024
GRID_W = 64
HG_HEADS = 8
HG_DK = 128
HG_CHUNK = 64
NA_HEADS = 16
NA_HD = 64
NA_WIN_R = 8
NA_WIN_C = 16
ROPE_THETA = 10000.0
NEG_INF = -1e30
N_EXPERTS = 64
EXPERT_DIM = 256
TOP_K = 8
N_GROUPS = 8
TOPK_GROUPS = 4
ROUTED_SCALE = 2.5
LN_EPS = 1e-6
N_SECTIONS = 10
SEC_Q, SEC_FF, SEC_FB, SEC_I, SEC_OG, SEC_NQ, SEC_NK, SEC_NV, SEC_GA, SEC_GB = range(10)

VMEM_LIMIT = 56 * 1024 * 1024


def _cparams(sem):
    return pltpu.CompilerParams(dimension_semantics=sem, vmem_limit_bytes=VMEM_LIMIT)


def _normalize(x):
    mu = jnp.mean(x, axis=-1, keepdims=True)
    xc = x - mu
    var = jnp.mean(xc * xc, axis=-1, keepdims=True)
    return xc * lax.rsqrt(var + LN_EPS)


def _silu(x):
    return x * jax.nn.sigmoid(x)


def _dot(a, b):
    return jnp.dot(a, b, preferred_element_type=F32)


def _dot_nt(a, b):
    return lax.dot_general(a, b, (((1,), (1,)), ((), ())), preferred_element_type=F32)


def _dot_tn(a, b):
    return lax.dot_general(a, b, (((0,), (0,)), ((), ())), preferred_element_type=F32)


U32 = jnp.uint32


def _pack_words(x):
    half = x.shape[1] // 2
    lo = lax.bitcast_convert_type(x[:, :half].astype(BF16).astype(F32), U32) >> 16
    hi = lax.bitcast_convert_type(x[:, half:].astype(BF16).astype(F32), U32) & jnp.uint32(0xFFFF0000)
    return lo | hi


def _unpack_words(w):
    lo = lax.bitcast_convert_type(w << 16, F32)
    hi = lax.bitcast_convert_type(w & jnp.uint32(0xFFFF0000), F32)
    return jnp.concatenate([lo, hi], axis=-1)


def _split3(x):
    hi = x.astype(BF16)
    r1 = x - hi.astype(F32)
    mid = r1.astype(BF16)
    lo = (r1 - mid.astype(F32)).astype(BF16)
    return hi, mid, lo


def _ada_kernel(c_ref, w_ref, b_ref, o_ref):
    cond = _silu(c_ref[...])
    o_ref[...] = _dot(cond.astype(BF16), w_ref[...].astype(BF16)) + b_ref[...]


def _ada(cond_rows, w_ada, b_ada):
    r, d = cond_rows.shape
    n = w_ada.shape[1]
    tn = 1024
    return pl.pallas_call(
        _ada_kernel,
        grid=(n // tn,),
        in_specs=[pl.BlockSpec((r, d), lambda j: (0, 0)),
                  pl.BlockSpec((d, tn), lambda j: (0, j)),
                  pl.BlockSpec((1, tn), lambda j: (0, j))],
        out_specs=pl.BlockSpec((r, tn), lambda j: (0, j)),
        out_shape=jax.ShapeDtypeStruct((r, n), F32),
        compiler_params=_cparams(("arbitrary",)),
        name="ada",
    )(cond_rows, w_ada, b_ada.reshape(1, n))


def _inproj_kernel(x_ref, sh_ref, sc_ref, w_ref, o_ref, h_ref):
    @pl.when(pl.program_id(2) == 0)
    def _():
        h = _normalize(x_ref[...]) * (1.0 + sc_ref[...]) + sh_ref[...]
        h_ref[...] = h.astype(BF16)

    o_ref[...] = _dot(h_ref[...], w_ref[...])


def _inproj(x, shift, scale, w_in_bf16):
    b, s, d = x.shape
    tm = min(1024, s)
    nj = w_in_bf16.shape[1] // d
    return pl.pallas_call(
        _inproj_kernel,
        grid=(b, s // tm, nj),
        in_specs=[pl.BlockSpec((None, tm, d), lambda bi, i, j: (bi, i, 0)),
                  pl.BlockSpec((None, 1, d), lambda bi, i, j: (bi, 0, 0)),
                  pl.BlockSpec((None, 1, d), lambda bi, i, j: (bi, 0, 0)),
                  pl.BlockSpec((d, d), lambda bi, i, j: (0, j))],
        out_specs=pl.BlockSpec((None, None, tm, d), lambda bi, i, j: (j, bi, i, 0)),
        out_shape=jax.ShapeDtypeStruct((nj, b, s, d), F32),
        scratch_shapes=[pltpu.VMEM((tm, d), BF16)],
        compiler_params=_cparams(("arbitrary", "arbitrary", "arbitrary")),
        name="inproj",
    )(x, shift, scale, w_in_bf16)


def _hgrn_gates(q, fraw, v, lb, tri_bf16, last_row):
    f = lb + (1.0 - lb) * jax.nn.sigmoid(fraw)
    k = 1.0 - f
    lf = jnp.log(f)
    hi, mid, lo = _split3(lf)
    a = _dot(tri_bf16, hi) + _dot(tri_bf16, mid) + _dot(tri_bf16, lo)
    a_last = a[last_row:last_row + 1, :]
    kd = (k * jnp.exp(a_last - a)).astype(BF16)
    decay = jnp.exp(a_last)
    qa = kb = None
    if q is not None:
        qa = (_silu(q) * jnp.exp(a)).astype(BF16)
        kb = (k * jnp.exp(-a)).astype(BF16)
    return qa, kb, kd, v.astype(BF16), decay


def _hgrn_chunks(chunks, st_ref):
    first = []
    for d, ((qa, kb, kd, vb, decay), keep) in enumerate(chunks):
        for h in range(HG_HEADS):
            sl = slice(h * HG_DK, (h + 1) * HG_DK)
            st = st_ref[d, h]
            if qa is not None:
                first.append((_dot_nt(qa[:, sl], kb[:, sl]), _dot_nt(qa[:, sl], st.astype(BF16))))
            st_ref[d, h] = st * decay[:, sl] + _dot_tn(vb[:, sl], kd[:, sl])
    results = []
    for d, ((qa, kb, kd, vb, decay), keep) in enumerate(chunks):
        if qa is None:
            results.append(None)
            continue
        outs = []
        for h in range(HG_HEADS):
            sl = slice(h * HG_DK, (h + 1) * HG_DK)
            s_qk, o_state = first.pop(0)
            outs.append(_dot(jnp.where(keep, s_qk, 0.0).astype(BF16), vb[:, sl]) + o_state)
        results.append(jnp.concatenate(outs, axis=-1))
    return results


def _hgrn_kernel(qf_ref, ff_ref, if_ref, qb_ref, fb_ref, ib_ref, cff_ref, cfb_ref, ci_ref,
                 lbf_ref, lbb_ref, of_ref, ob_ref, st_ref, *, n_sub, n_ctx_sub):
    n = pl.program_id(1)
    c = HG_CHUNK
    row = lax.broadcasted_iota(jnp.int32, (c, c), 0)
    col = lax.broadcasted_iota(jnp.int32, (c, c), 1)
    keep_f = col <= row
    keep_b = col >= row
    tri_f = keep_f.astype(F32).astype(BF16)
    tri_b = keep_b.astype(F32).astype(BF16)
    lbf = lbf_ref[...]
    lbb = lbb_ref[...]

    @pl.when(n == 0)
    def _():
        st_ref[...] = jnp.zeros_like(st_ref)

        def body(i, carry):
            r0 = pl.multiple_of(i * c, c)
            r1 = pl.multiple_of((n_ctx_sub - 1 - i) * c, c)
            gf = _hgrn_gates(None, cff_ref[pl.ds(r0, c), :], ci_ref[pl.ds(r0, c), :], lbf, tri_f, c - 1)
            gb = _hgrn_gates(None, cfb_ref[pl.ds(r1, c), :], ci_ref[pl.ds(r1, c), :], lbb, tri_b, 0)
            _hgrn_chunks([(gf, keep_f), (gb, keep_b)], st_ref)
            return carry

        lax.fori_loop(0, n_ctx_sub, body, 0)

    @pl.when(n > 0)
    def _():
        def body(i, carry):
            r0 = pl.multiple_of(i * c, c)
            r1 = pl.multiple_of((n_sub - 1 - i) * c, c)
            gf = _hgrn_gates(qf_ref[pl.ds(r0, c), :], ff_ref[pl.ds(r0, c), :], if_ref[pl.ds(r0, c), :],
                             lbf, tri_f, c - 1)
            gb = _hgrn_gates(qb_ref[pl.ds(r1, c), :], fb_ref[pl.ds(r1, c), :], ib_ref[pl.ds(r1, c), :],
                             lbb, tri_b, 0)
            o_f, o_b = _hgrn_chunks([(gf, keep_f), (gb, keep_b)], st_ref)
            of_ref[pl.ds(r0, c), :] = o_f
            ob_ref[pl.ds(r1, c), :] = o_b
            return carry

        lax.fori_loop(0, n_sub, body, 0)


def _hgrn(p, pc, lb_fwd, lb_bwd):
    _, b, s, w = p.shape
    ctx_len = pc.shape[2]
    tb = min(256, s)
    nb = s // tb
    fwd = lambda bi, n: jnp.maximum(n - 1, 0)
    bwd = lambda bi, n: nb - 1 - jnp.maximum(n - 1, 0)

    def sec(section, blk):
        return pl.BlockSpec((None, None, tb, w), lambda bi, n: (section, bi, blk(bi, n), 0))

    def csec(section):
        return pl.BlockSpec((None, None, ctx_len, w), lambda bi, n: (section, bi, 0, 0))

    vec = pl.BlockSpec((1, w), lambda bi, n: (0, 0))
    kern = functools.partial(_hgrn_kernel, n_sub=tb // HG_CHUNK, n_ctx_sub=ctx_len // HG_CHUNK)
    return pl.pallas_call(
        kern,
        grid=(b, nb + 1),
        in_specs=[sec(SEC_Q, fwd), sec(SEC_FF, fwd), sec(SEC_I, fwd),
                  sec(SEC_Q, bwd), sec(SEC_FB, bwd), sec(SEC_I, bwd),
                  csec(SEC_FF), csec(SEC_FB), csec(SEC_I), vec, vec],
        out_specs=[pl.BlockSpec((None, tb, w), lambda bi, n: (bi, fwd(bi, n), 0)),
                   pl.BlockSpec((None, tb, w), lambda bi, n: (bi, bwd(bi, n), 0))],
        out_shape=[jax.ShapeDtypeStruct((b, s, w), F32), jax.ShapeDtypeStruct((b, s, w), F32)],
        scratch_shapes=[pltpu.VMEM((2, HG_HEADS, HG_DK, HG_DK), F32)],
        compiler_params=_cparams(("arbitrary", "arbitrary")),
        name="hgrn",
    )(p, p, p, p, p, p, pc, pc, pc, lb_fwd.reshape(1, w), lb_bwd.reshape(1, w))


NA_ROWS_PER_STEP = 8
NA_PREP_ROWS = 512
NA_KEY_TILE = 128
NA_SPAN = (NA_WIN_R + 2) * GRID_W


def _rope(t, cos, sin_signed, first_half):
    w = t.shape[-1]
    partner = jnp.where(first_half, pltpu.roll(t, w - 16, 1), pltpu.roll(t, 16, 1))
    return t * cos + partner * sin_signed


def _fold_lanes(op, *arrays):
    tiles = [a[:, c:c + 128] for a in arrays for c in range(0, a.shape[-1], 128)]
    acc = tiles[0]
    for t in tiles[1:]:
        acc = op(acc, t)
    return acc


def _rope_tables(rowtab_ref, coltab_ref, row0, n_rows, row_lane):
    out = []
    for i in range(2):
        rt = rowtab_ref[i, pl.ds(row0, n_rows), :]
        by_row = jnp.concatenate([jnp.broadcast_to(rt[r:r + 1, :], (GRID_W, rt.shape[1])) for r in range(n_rows)],
                                 axis=0)
        by_col = jnp.concatenate([coltab_ref[i]] * n_rows, axis=0)
        out.append(jnp.where(row_lane, by_row, by_col))
    return out


def _natten_kernel(q_ref, k_ref, v_ref, kc_ref, vc_ref, rowtab_ref, coltab_ref, t2_ref, o_ref,
                   kt_s, v_s, kc_s, vc_s, bias_s, *, rows):
    rblk = pl.program_id(2)
    hd = NA_HD
    lane = lax.broadcasted_iota(jnp.int32, (1, 2 * hd), 1)
    first_half = (lane % 32) < 16
    row_lane = (lane % hd) < hd // 2
    scale = NA_HD ** -0.5

    def values_and_ones(v_pair, h):
        vh = v_pair if h == 0 else pltpu.roll(v_pair, hd, 1)
        return jnp.where(lane < hd, vh, jnp.where(lane == hd, 1.0, 0.0)).astype(BF16)

    @pl.when(rblk == 0)
    def _():
        kc = kc_ref[...].astype(BF16)
        qi = lax.broadcasted_iota(jnp.int32, (GRID_W, GRID_W), 0)
        ki = lax.broadcasted_iota(jnp.int32, (GRID_W, GRID_W), 1)
        cstart = jnp.clip(qi - NA_WIN_C // 2, 0, GRID_W - NA_WIN_C)
        in_win = (ki >= cstart) & (ki < cstart + NA_WIN_C)
        masked = jnp.full((GRID_W, GRID_W), NEG_INF, F32)
        s_len = k_ref.shape[0]
        for h in range(2):
            sl = slice(h * hd, (h + 1) * hd)
            kc_s[h] = kc[:, sl]
            vc_s[h] = values_and_ones(vc_ref[...], h)
            kt_s[h, s_len // NA_KEY_TILE] = jnp.zeros((hd, NA_KEY_TILE), BF16)
            v_s[h, s_len:s_len + NA_KEY_TILE, :] = jnp.zeros((NA_KEY_TILE, 2 * hd), BF16)
            tiles = [jnp.where(in_win, t2_ref[h, dr], NEG_INF) for dr in range(2 * NA_WIN_R - 1)]
            for bidx in range(NA_WIN_R + 1):
                v, par = (bidx, 0) if bidx < NA_WIN_R else (NA_WIN_R // 2, 1)
                for piece in range(NA_SPAN // GRID_W):
                    j = piece - par
                    tile = tiles[NA_WIN_R - 1 - v + j] if 0 <= j < NA_WIN_R else masked
                    bias_s[h, bidx, :, piece * GRID_W:(piece + 1) * GRID_W] = tile

        def prep(i, carry):
            r0 = pl.multiple_of(i * NA_PREP_ROWS, NA_PREP_ROWS)
            rws = pl.ds(r0, NA_PREP_ROWS)
            cos, sin = _rope_tables(rowtab_ref, coltab_ref, i * (NA_PREP_ROWS // GRID_W), NA_PREP_ROWS // GRID_W,
                                    row_lane)
            kr = _rope(k_ref[rws, :], cos, sin, first_half)
            krt = kr.T.astype(BF16)
            vv = v_ref[rws, :]
            for h in range(2):
                sl = slice(h * hd, (h + 1) * hd)
                for c in range(NA_PREP_ROWS // NA_KEY_TILE):
                    kt_s[h, i * (NA_PREP_ROWS // NA_KEY_TILE) + c] = krt[sl, c * NA_KEY_TILE:(c + 1) * NA_KEY_TILE]
                v_s[h, rws, :] = values_and_ones(vv, h)
            return carry

        lax.fori_loop(0, s_len // NA_PREP_ROWS, prep, 0)

    tq = NA_ROWS_PER_STEP * GRID_W
    q = q_ref[...] * scale
    cos, sin = _rope_tables(rowtab_ref, coltab_ref, rblk * NA_ROWS_PER_STEP, NA_ROWS_PER_STEP, row_lane)
    qr = _rope(q, cos, sin, first_half)
    qb = q.astype(BF16)
    qrb = qr.astype(BF16)
    rws = [slice(rr * GRID_W, (rr + 1) * GRID_W) for rr in range(NA_ROWS_PER_STEP)]
    tile0, bidx = [], []
    for rr in range(NA_ROWS_PER_STEP):
        r = rblk * NA_ROWS_PER_STEP + rr
        rs = jnp.clip(r - NA_WIN_R // 2, 0, rows - NA_WIN_R)
        tile0.append(lax.shift_right_logical(rs, 1))
        bidx.append(jnp.where((rs & 1) == 1, NA_WIN_R, r - rs))

    def scores(h):
        sl = slice(h * hd, (h + 1) * hd)
        qrb_h = qrb[:, sl]
        s_ctx_all = _dot_nt(qb[:, sl], kc_s[h])
        s_win = []
        for rr in range(NA_ROWS_PER_STEP):
            kt = kt_s[h, pl.ds(tile0[rr], NA_SPAN // NA_KEY_TILE)]
            kt = jnp.concatenate([kt[c] for c in range(NA_SPAN // NA_KEY_TILE)], axis=-1)
            s_win.append(_dot(qrb_h[rws[rr]], kt))
        return s_win, s_ctx_all

    def softmax(h, s_win, s_ctx_all):
        e_win, e_ctx = [], []
        for rr in range(NA_ROWS_PER_STEP):
            sw = s_win[rr] + bias_s[h, bidx[rr]]
            sc = s_ctx_all[rws[rr]]
            m = jnp.max(_fold_lanes(jnp.maximum, sw, sc), axis=-1, keepdims=True)
            e_win.append(jnp.exp(sw - m).astype(BF16))
            e_ctx.append(jnp.exp(sc - m).astype(BF16))
        return e_win, e_ctx

    def values(h, e_win, e_ctx):
        o_win = []
        for rr in range(NA_ROWS_PER_STEP):
            k0 = pl.multiple_of(tile0[rr] * NA_KEY_TILE, NA_KEY_TILE)
            o_win.append(_dot(e_win[rr], v_s[h, pl.ds(k0, NA_SPAN), :]))
        o = jnp.concatenate(o_win, axis=0) + _dot(jnp.concatenate(e_ctx, axis=0), vc_s[h])
        return o[:, :hd] * (1.0 / o[:, hd:hd + 1])

    s0 = scores(0)
    s1 = scores(1)
    p0 = softmax(0, *s0)
    o0 = values(0, *p0)
    p1 = softmax(1, *s1)
    o1 = values(1, *p1)
    o_ref[...] = jnp.concatenate([o0, o1], axis=-1)


def _na_tables(rpb, s):
    half = NA_HD // 2
    inv = jnp.power(ROPE_THETA, -jnp.arange(0, half, 2, dtype=F32) / half)

    def tables(n):
        ang = jnp.arange(n, dtype=F32)[:, None] * inv[None, :]
        reps = 2 * NA_HD // half
        return jnp.stack([jnp.tile(jnp.cos(ang), (1, 2 * reps)),
                          jnp.tile(jnp.concatenate([-jnp.sin(ang), jnp.sin(ang)], axis=-1), (1, reps))])

    rowtab, coltab = tables(s // GRID_W), tables(GRID_W)

    pad = GRID_W - NA_WIN_C
    rp = jnp.pad(rpb.astype(F32), ((0, 0), (0, 0), (pad, pad)), mode="edge")
    t2 = jnp.stack([rp[:, :, GRID_W - 1 - qc:2 * GRID_W - 1 - qc] for qc in range(GRID_W)], axis=2)
    return rowtab, coltab, t2


def _natten(p, pc, rowtab, coltab, t2):
    _, b, s, w = p.shape
    ctx_len = pc.shape[2]
    rows = s // GRID_W
    assert rows >= NA_WIN_R and rows % NA_ROWS_PER_STEP == 0
    tq = NA_ROWS_PER_STEP * GRID_W
    hw = 2 * NA_HD
    nhp = w // hw
    kern = functools.partial(_natten_kernel, rows=rows)
    return pl.pallas_call(
        kern,
        grid=(b, nhp, rows // NA_ROWS_PER_STEP),
        in_specs=[pl.BlockSpec((None, None, tq, hw), lambda bi, hp, r: (SEC_NQ, bi, r, hp)),
                  pl.BlockSpec((None, None, s, hw), lambda bi, hp, r: (SEC_NK, bi, 0, hp)),
                  pl.BlockSpec((None, None, s, hw), lambda bi, hp, r: (SEC_NV, bi, 0, hp)),
                  pl.BlockSpec((None, None, ctx_len, hw), lambda bi, hp, r: (SEC_NK, bi, 0, hp)),
                  pl.BlockSpec((None, None, ctx_len, hw), lambda bi, hp, r: (SEC_NV, bi, 0, hp)),
                  pl.BlockSpec((2, rows, hw), lambda bi, hp, r: (0, 0, 0)),
                  pl.BlockSpec((2, GRID_W, hw), lambda bi, hp, r: (0, 0, 0)),
                  pl.BlockSpec((2, 2 * NA_WIN_R - 1, GRID_W, GRID_W), lambda bi, hp, r: (hp, 0, 0, 0))],
        out_specs=pl.BlockSpec((None, tq, hw), lambda bi, hp, r: (bi, r, hp)),
        out_shape=jax.ShapeDtypeStruct((b, s, w), F32),
        scratch_shapes=[pltpu.VMEM((2, s // NA_KEY_TILE + 1, NA_HD, NA_KEY_TILE), BF16),
                        pltpu.VMEM((2, s + NA_KEY_TILE, hw), BF16),
                        pltpu.VMEM((2, ctx_len, NA_HD), BF16), pltpu.VMEM((2, ctx_len, hw), BF16),
                        pltpu.VMEM((2, NA_WIN_R + 1, GRID_W, NA_SPAN), F32)],
        compiler_params=_cparams(("arbitrary", "arbitrary", "arbitrary")),
        name="natten",
    )(p, p, p, pc, pc, rowtab, coltab, t2)


def _route(logits_t, rbias):
    e, t = logits_t.shape
    gsz = e // N_GROUPS
    scores = jax.nn.sigmoid(logits_t)
    sel = scores + rbias
    neg = -jnp.inf
    sub = lax.broadcasted_iota(jnp.int32, (gsz, t), 0).astype(F32)
    gscore = []
    for g in range(N_GROUPS):
        grp = sel[g * gsz:(g + 1) * gsz, :]
        m1 = jnp.max(grp, axis=0, keepdims=True)
        first = jnp.min(jnp.where(grp == m1, sub, float(gsz)), axis=0, keepdims=True)
        m2 = jnp.max(jnp.where(sub == first, neg, grp), axis=0, keepdims=True)
        gscore.append(m1 + m2)
    masked = []
    for g in range(N_GROUPS):
        rank = jnp.zeros((1, t), F32)
        for g2 in range(N_GROUPS):
            if g2 == g:
                continue
            if g2 < g:
                ahead = gscore[g2] >= gscore[g]
            else:
                ahead = gscore[g2] > gscore[g]
            rank = rank + jnp.where(ahead, 1.0, 0.0)
        masked.append(jnp.where(rank < TOPK_GROUPS, sel[g * gsz:(g + 1) * gsz, :], neg))
    work = jnp.concatenate(masked, axis=0)
    eidx = lax.broadcasted_iota(jnp.int32, (e, t), 0).astype(F32)
    idxs, ws = [], []
    chosen = jnp.zeros((e, t), F32)
    for _ in range(TOP_K):
        m = jnp.max(work, axis=0, keepdims=True)
        first = jnp.min(jnp.where(work == m, eidx, float(e)), axis=0, keepdims=True)
        pick = eidx == first
        idxs.append(first)
        ws.append(jnp.sum(jnp.where(pick, scores, 0.0), axis=0, keepdims=True))
        chosen = jnp.where(pick, 1.0, chosen)
        work = jnp.where(pick, neg, work)
    w = jnp.concatenate(ws, axis=0)
    w = w / jnp.sum(w, axis=0, keepdims=True) * ROUTED_SCALE
    return jnp.concatenate(idxs, axis=0).astype(jnp.int32), w, chosen


MERGE_TOK = 512
MERGE_SUB = 256


def _merge_kernel(of_ref, ob_ref, og_ref, yna_ref, ga_ref, gb_ref, x_ref, g1_ref, sh2_ref, sc2_ref,
                  hgg_ref, ln1g_ref, ln1b_ref, wa_ref, wb_ref, wo_ref, wr_ref, rb_ref,
                  x1_ref, h2_ref, topi_ref, topw_ref, cnt_ref, *, alpha):
    tm = x_ref.shape[0]
    subs = [slice(i * MERGE_SUB, (i + 1) * MERGE_SUB) for i in range(tm // MERGE_SUB)]

    def branches(rows):
        o = of_ref[rows, :] + ob_ref[rows, :]
        parts = []
        for h in range(HG_HEADS):
            oh = o[:, h * HG_DK:(h + 1) * HG_DK]
            parts.append(oh * lax.rsqrt(jnp.mean(oh * oh, axis=-1, keepdims=True) + LN_EPS))
        y_hg = jnp.concatenate(parts, axis=-1) * hgg_ref[...] * _silu(og_ref[rows, :])
        return _dot(y_hg.astype(BF16), wa_ref[...]), _dot(yna_ref[rows, :].astype(BF16), wb_ref[...])

    def out_proj(rows, ya, yb):
        t = jax.nn.sigmoid(ga_ref[rows, :]) * ya + jax.nn.sigmoid(gb_ref[rows, :]) * yb
        return _dot(t.astype(BF16), wo_ref[...])

    def norms_router(rows, i, y):
        x1 = _normalize(alpha * x_ref[rows, :] + g1_ref[...] * y) * ln1g_ref[...] + ln1b_ref[...]
        x1_ref[rows, :] = x1
        h2 = _normalize(x1) * (1.0 + sc2_ref[...]) + sh2_ref[...]
        h2_ref[rows, :] = _pack_words(h2)
        hh, hm, hl = _split3(h2)
        wh, wm, wl = _split3(wr_ref[...])
        return (_dot_nt(wh, hh) + _dot_nt(wh, hm) + _dot_nt(wm, hh)
                + _dot_nt(wh, hl) + _dot_nt(wl, hh) + _dot_nt(wm, hm))

    ab = [branches(rows) for rows in subs]
    ys = [out_proj(rows, *ab[i]) for i, rows in enumerate(subs)]
    logits = [norms_router(rows, i, ys[i]) for i, rows in enumerate(subs)]

    @pl.when((pl.program_id(0) == 0) & (pl.program_id(1) == 0))
    def _():
        cnt_ref[...] = jnp.zeros_like(cnt_ref)

    for i, rows in enumerate(subs):
        topi, topw, chosen = _route(logits[i], rb_ref[...])
        topi_ref[:, rows] = topi
        topw_ref[:, rows] = topw
        cnt_ref[...] += jnp.sum(chosen, axis=1, keepdims=True)


def _merge(o_f, o_b, p, y_na, x, g1, sh2, sc2, hg_norm_g, ln1_g, ln1_b, w_a, w_b, w_o, w_router_t, router_bias,
           alpha):
    b, s, d = x.shape
    tm = min(MERGE_TOK, s)
    e = w_router_t.shape[0]
    tok = lambda bi, i: (bi, i, 0)
    blk = pl.BlockSpec((None, tm, d), tok)

    def sec(section):
        return pl.BlockSpec((None, None, tm, d), lambda bi, i: (section, bi, i, 0))

    mod = pl.BlockSpec((None, 1, d), lambda bi, i: (bi, 0, 0))
    vec = pl.BlockSpec((1, d), lambda bi, i: (0, 0))
    mat = pl.BlockSpec((d, d), lambda bi, i: (0, 0), pipeline_mode=pl.Buffered(1))
    return pl.pallas_call(
        functools.partial(_merge_kernel, alpha=alpha),
        grid=(b, s // tm),
        in_specs=[blk, blk, sec(SEC_OG), blk, sec(SEC_GA), sec(SEC_GB), blk, mod, mod, mod,
                  vec, vec, vec, mat, mat, mat,
                  pl.BlockSpec((e, d), lambda bi, i: (0, 0)),
                  pl.BlockSpec((e, 1), lambda bi, i: (0, 0))],
        out_specs=[blk,
                   pl.BlockSpec((tm, d // 2), lambda bi, i: (bi * (s // tm) + i, 0)),
                   pl.BlockSpec((None, TOP_K, tm), lambda bi, i: (bi, 0, i)),
                   pl.BlockSpec((None, TOP_K, tm), lambda bi, i: (bi, 0, i)),
                   pl.BlockSpec((e, 128), lambda bi, i: (0, 0))],
        out_shape=[jax.ShapeDtypeStruct((b, s, d), F32),
                   jax.ShapeDtypeStruct((b * s, d // 2), U32),
                   jax.ShapeDtypeStruct((b, TOP_K, s), jnp.int32), jax.ShapeDtypeStruct((b, TOP_K, s), F32),
                   jax.ShapeDtypeStruct((e, 128), F32)],
        compiler_params=_cparams(("arbitrary", "arbitrary")),
        name="merge",
    )(o_f, o_b, p, y_na, p, p, x, g1, sh2, sc2, hg_norm_g.reshape(1, d), ln1_g.reshape(1, d),
      ln1_b.reshape(1, d), w_a, w_b, w_o, w_router_t, router_bias.reshape(e, 1))


MOE_TILE = 512
MOE_TOK = 256


def _plan_kernel(topi_ref, off_ref, dest_ref, carry_ref):
    @pl.when(pl.program_id(0) == 0)
    def _():
        carry_ref[...] = jnp.zeros_like(carry_ref)

    topi = topi_ref[...]
    tok = topi.shape[1]
    eidx = lax.broadcasted_iota(jnp.int32, (N_EXPERTS, tok), 0)
    hits = [eidx == topi[k:k + 1, :] for k in range(TOP_K)]
    m = jnp.zeros((N_EXPERTS, tok), F32)
    for hit in hits:
        m = jnp.where(hit, 1.0, m)
    before = (lax.broadcasted_iota(jnp.int32, (tok, tok), 0)
              < lax.broadcasted_iota(jnp.int32, (tok, tok), 1)).astype(F32).astype(BF16)
    row = off_ref[...] + carry_ref[...] + _dot(m.astype(BF16), before)
    dest = [jnp.sum(jnp.where(hit, row, 0.0), axis=0, keepdims=True) for hit in hits]
    dest_ref[...] = jnp.concatenate(dest, axis=0).astype(jnp.int32)
    carry_ref[...] += jnp.sum(m, axis=1, keepdims=True)


def _plan(topi, seg_off):
    b, k, s = topi.shape
    per_b = s // MOE_TOK
    blk = pl.BlockSpec((None, k, MOE_TOK), lambda i: (i // per_b, 0, i % per_b))
    return pl.pallas_call(
        _plan_kernel,
        grid=(b * per_b,),
        in_specs=[blk, pl.BlockSpec((N_EXPERTS, 1), lambda i: (0, 0))],
        out_specs=blk,
        out_shape=jax.ShapeDtypeStruct((b, k, s), jnp.int32),
        scratch_shapes=[pltpu.VMEM((N_EXPERTS, 1), F32)],
        compiler_params=_cparams(("arbitrary",)),
        name="plan",
    )(topi, seg_off.astype(F32).reshape(N_EXPERTS, 1))


SC_WINDOW = 128


def _sc_workers():
    info = plsc.get_sparse_core_info()
    return info.num_cores, info.num_cores * info.num_subcores


def _scatter_rows(src, idx, zero_idx, n_rows):
    t, w = src.shape
    m, mz = idx.shape[0], zero_idx.shape[0]
    n_cores, n_workers = _sc_workers()
    per_worker, per_worker_z = m // n_workers, mz // n_workers
    assert m % t == 0 and t % SC_WINDOW == 0
    assert per_worker * n_workers == m and per_worker % SC_WINDOW == 0
    assert per_worker_z * n_workers == mz and per_worker_z % SC_WINDOW == 0
    mesh = plsc.VectorSubcoreMesh(core_axis_name="core", subcore_axis_name="subcore")

    @functools.partial(
        pl.kernel, mesh=mesh, out_type=jax.ShapeDtypeStruct((n_rows, w), src.dtype),
        scratch_types=[pltpu.VMEM((SC_WINDOW,), jnp.int32), pltpu.VMEM((SC_WINDOW, w), src.dtype),
                       pltpu.SemaphoreType.DMA])
    def scatter(src_hbm, idx_hbm, zeros_hbm, zero_idx_hbm, out_hbm, idx_v, rows_v, sem):
        worker = lax.axis_index("subcore") * n_cores + lax.axis_index("core")

        @pl.loop(0, per_worker // SC_WINDOW)
        def _(step):
            base = pl.multiple_of(worker * per_worker + step * SC_WINDOW, SC_WINDOW)
            first = pl.multiple_of(lax.rem(base, t), SC_WINDOW)
            pltpu.sync_copy(idx_hbm.at[pl.ds(base, SC_WINDOW)], idx_v)
            pltpu.sync_copy(src_hbm.at[pl.ds(first, SC_WINDOW)], rows_v)
            pltpu.async_copy(rows_v, out_hbm.at[idx_v], sem).wait()

        pltpu.sync_copy(zeros_hbm, rows_v)

        @pl.loop(0, per_worker_z // SC_WINDOW)
        def _(step):
            base = pl.multiple_of(worker * per_worker_z + step * SC_WINDOW, SC_WINDOW)
            pltpu.sync_copy(zero_idx_hbm.at[pl.ds(base, SC_WINDOW)], idx_v)
            pltpu.async_copy(rows_v, out_hbm.at[idx_v], sem).wait()

    return scatter(src, idx, jnp.zeros((SC_WINDOW, w), src.dtype), zero_idx)


def _experts_kernel(te_ref, tb_ref, nt_ref, xs_ref, wg_ref, wu_ref, wd_ref, ys_ref):
    @pl.when(pl.program_id(0) < nt_ref[0])
    def _():
        x = _unpack_words(xs_ref[...]).astype(BF16)
        act = _silu(_dot(x, wg_ref[...].astype(BF16))) * _dot(x, wu_ref[...].astype(BF16))
        ys_ref[...] = _pack_words(_dot(act.astype(BF16), wd_ref[...].astype(BF16)))


def _experts(xs, tile_expert, tile_block, n_tiles, wg, wu, wd):
    d, f = wg.shape[1], wg.shape[2]
    rows = pl.BlockSpec((MOE_TILE, d // 2), lambda i, te, tb, nt: (tb[i], 0))
    grid_spec = pltpu.PrefetchScalarGridSpec(
        num_scalar_prefetch=3,
        grid=(xs.shape[0] // MOE_TILE,),
        in_specs=[rows,
                  pl.BlockSpec((None, d, f), lambda i, te, tb, nt: (te[i], 0, 0)),
                  pl.BlockSpec((None, d, f), lambda i, te, tb, nt: (te[i], 0, 0)),
                  pl.BlockSpec((None, f, d), lambda i, te, tb, nt: (te[i], 0, 0))],
        out_specs=rows,
    )
    return pl.pallas_call(
        _experts_kernel,
        grid_spec=grid_spec,
        out_shape=jax.ShapeDtypeStruct(xs.shape, U32),
        compiler_params=_cparams(("arbitrary",)),
        name="experts",
    )(tile_expert, tile_block, n_tiles, xs, wg, wu, wd)


def _gather_rows(table, idx):
    m = idx.shape[0]
    w = table.shape[1]
    n_cores, n_workers = _sc_workers()
    per_worker = m // n_workers
    assert per_worker * n_workers == m and per_worker % SC_WINDOW == 0
    mesh = plsc.VectorSubcoreMesh(core_axis_name="core", subcore_axis_name="subcore")

    @functools.partial(
        pl.kernel, mesh=mesh, out_type=jax.ShapeDtypeStruct((m, w), table.dtype),
        scratch_types=[pltpu.VMEM((SC_WINDOW,), jnp.int32), pltpu.VMEM((SC_WINDOW, w), table.dtype),
                       pltpu.SemaphoreType.DMA])
    def gather(table_hbm, idx_hbm, out_hbm, idx_v, rows_v, sem):
        worker = lax.axis_index("subcore") * n_cores + lax.axis_index("core")

        @pl.loop(0, per_worker // SC_WINDOW)
        def _(step):
            base = pl.multiple_of(worker * per_worker + step * SC_WINDOW, SC_WINDOW)
            pltpu.sync_copy(idx_hbm.at[pl.ds(base, SC_WINDOW)], idx_v)
            pltpu.async_copy(table_hbm.at[idx_v], rows_v, sem).wait()
            pltpu.sync_copy(rows_v, out_hbm.at[pl.ds(base, SC_WINDOW)])

    return gather(table, idx)


def _combine_kernel(rows_ref, topw_ref, h_ref, x1_ref, g2_ref, sg_ref, su_ref, sd_ref, ln2g_ref, ln2b_ref,
                    o_ref, *, alpha):
    tok = x1_ref.shape[0]
    h = _unpack_words(h_ref[...]).astype(BF16)
    act = _silu(_dot(h, sg_ref[...])) * _dot(h, su_ref[...])
    y = _dot(act.astype(BF16), sd_ref[...])
    w = topw_ref[...].T
    for k in range(TOP_K):
        y = y + w[:, k:k + 1] * _unpack_words(rows_ref[k])
    o_ref[...] = _normalize(alpha * x1_ref[...] + g2_ref[...] * y) * ln2g_ref[...] + ln2b_ref[...]


def _combine(gathered, topw, h2p, x1, g2, sg, su, sd, ln2_g, ln2_b, alpha):
    t, d = x1.shape
    b, k, s = topw.shape
    per_b = s // MOE_TOK
    fs = sg.shape[1]
    rows = pl.BlockSpec((MOE_TOK, d), lambda i: (i, 0))
    packed = pl.BlockSpec((MOE_TOK, d // 2), lambda i: (i, 0))
    vec = pl.BlockSpec((1, d), lambda i: (0, 0))
    return pl.pallas_call(
        functools.partial(_combine_kernel, alpha=alpha),
        grid=(t // MOE_TOK,),
        in_specs=[pl.BlockSpec((k, MOE_TOK, d // 2), lambda i: (0, i, 0)),
                  pl.BlockSpec((None, k, MOE_TOK), lambda i: (i // per_b, 0, i % per_b)),
                  packed, rows,
                  pl.BlockSpec((None, 1, d), lambda i: (i // per_b, 0, 0)),
                  pl.BlockSpec((d, fs), lambda i: (0, 0)),
                  pl.BlockSpec((d, fs), lambda i: (0, 0)),
                  pl.BlockSpec((fs, d), lambda i: (0, 0)),
                  vec, vec],
        out_specs=rows,
        out_shape=jax.ShapeDtypeStruct((t, d), F32),
        compiler_params=_cparams(("arbitrary",)),
        name="combine",
    )(gathered, topw, h2p, x1, g2, sg, su, sd, ln2_g.reshape(1, d), ln2_b.reshape(1, d))


def _moe(h2p, topi, topw, cnt, x1, g2, wg, wu, wd, sg, su, sd, ln2_g, ln2_b, alpha):
    b, s, d = x1.shape
    t = b * s
    cnt = cnt[:, 0].astype(jnp.int32)
    tiles_e = (cnt + (MOE_TILE - 1)) // MOE_TILE
    tiles_cum = jnp.cumsum(tiles_e)
    seg_off = (tiles_cum - tiles_e) * MOE_TILE
    n_tiles_max = t * TOP_K // MOE_TILE + N_EXPERTS
    tile_block = jnp.minimum(jnp.arange(n_tiles_max, dtype=jnp.int32), tiles_cum[-1] - 1)
    tile_expert = jnp.sum((tiles_cum[None, :] <= tile_block[:, None]).astype(jnp.int32), axis=1)
    n_tiles = tiles_cum[-1:].astype(jnp.int32)

    dest = jnp.transpose(_plan(topi, seg_off), (1, 0, 2)).reshape(TOP_K * t)
    j = jnp.arange(MOE_TILE, dtype=jnp.int32)[None, :]
    n_pad = (tiles_e * MOE_TILE - cnt)[:, None]
    spare = (n_tiles_max - 1) * MOE_TILE + j
    zero_idx = jnp.where(j < n_pad, (seg_off + cnt)[:, None] + j, spare).reshape(N_EXPERTS * MOE_TILE)
    xs = _scatter_rows(h2p, dest, zero_idx.astype(jnp.int32), n_tiles_max * MOE_TILE)
    ys = _experts(xs, tile_expert, tile_block, n_tiles, wg, wu, wd)
    gathered = _gather_rows(ys, dest)
    out = _combine(gathered.reshape(TOP_K, t, d // 2), topw, h2p, x1.reshape(t, d), g2, sg, su, sd,
                   ln2_g, ln2_b, alpha)
    return out.reshape(b, s, d)


def kernel(x, c, ctx, c_ctx, w_ada, b_ada, w_in, hg_lb_fwd, hg_lb_bwd, hg_norm_g, na_rpb, w_branch_a, w_branch_b, w_out, ln1_g, ln1_b, w_router, router_bias, w_e_gate, w_e_up, w_e_down, w_sh_gate, w_sh_up, w_sh_down, ln2_g, ln2_b):
    depth = w_ada.shape[0]
    assert depth == 1, "single-layer block"
    b, s, d = x.shape
    alpha = (2.0 * depth) ** 0.25
    l = 0
    lb_fwd = jnp.cumsum(jax.nn.softmax(hg_lb_fwd.astype(F32), axis=0), axis=0)[l]
    lb_bwd = jnp.cumsum(jax.nn.softmax(hg_lb_bwd.astype(F32), axis=0), axis=0)[l]

    cond_rows = jnp.concatenate([c, c_ctx[None, :], jnp.zeros((8 - b - 1, d), F32)], axis=0)
    mod = _ada(cond_rows, w_ada[l], b_ada[l])
    sh1, sc1, g1, sh2, sc2, g2 = [m[:b, None, :] for m in jnp.split(mod, 6, axis=-1)]
    csh1, csc1 = [jnp.broadcast_to(m[b:b + 1, None, :], (b, 1, d)) for m in jnp.split(mod, 6, axis=-1)[:2]]

    w_in_b = w_in[l].astype(BF16)
    p = _inproj(x, sh1, sc1, w_in_b)
    pc = _inproj(ctx, csh1, csc1, w_in_b)

    o_f, o_b = _hgrn(p, pc, lb_fwd, lb_bwd)
    y_na = _natten(p, pc, *_na_tables(na_rpb[l], s))

    x1, h2, topi, topw, cnt = _merge(o_f, o_b, p, y_na, x, g1, sh2, sc2, hg_norm_g[l], ln1_g[l], ln1_b[l],
                                     w_branch_a[l].astype(BF16), w_branch_b[l].astype(BF16),
                                     w_out[l].astype(BF16), w_router[l].T, router_bias[l], alpha)

    return _moe(h2, topi, topw, cnt, x1, g2,
                w_e_gate[l], w_e_up[l], w_e_down[l],
                w_sh_gate[l].astype(BF16), w_sh_up[l].astype(BF16), w_sh_down[l].astype(BF16),
                ln2_g[l], ln2_b[l], alpha)
```

```python
import functools

import numpy as np
import jax
import jax.numpy as jnp
from jax import lax
from jax.experimental import pallas as pl
from jax.experimental.pallas import tpu as pltpu
from jax.experimental.pallas import tpu_sc as plsc

F32 = jnp.float32
BF16 = jnp.bfloat16

D_MODEL = 1024
GRID_W = 64
HG_HEADS = 8
HG_DK = 128
HG_CHUNK = 64
NA_HEADS = 16
NA_HD = 64
NA_WIN_R = 8
NA_WIN_C = 16
ROPE_THETA = 10000.0
NEG_INF = -1e30
N_EXPERTS = 64
EXPERT_DIM = 256
TOP_K = 8
N_GROUPS = 8
TOPK_GROUPS = 4
ROUTED_SCALE = 2.5
LN_EPS = 1e-6
N_SECTIONS = 10
SEC_Q, SEC_FF, SEC_FB, SEC_I, SEC_OG, SEC_NQ, SEC_NK, SEC_NV, SEC_GA, SEC_GB = range(10)

VMEM_LIMIT = 56 * 1024 * 1024


def _cparams(sem):
    return pltpu.CompilerParams(dimension_semantics=sem, vmem_limit_bytes=VMEM_LIMIT)


def _normalize(x):
    mu = jnp.mean(x, axis=-1, keepdims=True)
    xc = x - mu
    var = jnp.mean(xc * xc, axis=-1, keepdims=True)
    return xc * lax.rsqrt(var + LN_EPS)


def _silu(x):
    return x * jax.nn.sigmoid(x)


def _dot(a, b):
    return jnp.dot(a, b, preferred_element_type=F32)


def _dot_nt(a, b):
    return lax.dot_general(a, b, (((1,), (1,)), ((), ())), preferred_element_type=F32)


def _dot_tn(a, b):
    return lax.dot_general(a, b, (((0,), (0,)), ((), ())), preferred_element_type=F32)


U32 = jnp.uint32


def _pack_words(x):
    half = x.shape[1] // 2
    lo = lax.bitcast_convert_type(x[:, :half].astype(BF16).astype(F32), U32) >> 16
    hi = lax.bitcast_convert_type(x[:, half:].astype(BF16).astype(F32), U32) & jnp.uint32(0xFFFF0000)
    return lo | hi


def _unpack_words(w):
    lo = lax.bitcast_convert_type(w << 16, F32)
    hi = lax.bitcast_convert_type(w & jnp.uint32(0xFFFF0000), F32)
    return jnp.concatenate([lo, hi], axis=-1)


def _split3(x):
    hi = x.astype(BF16)
    r1 = x - hi.astype(F32)
    mid = r1.astype(BF16)
    lo = (r1 - mid.astype(F32)).astype(BF16)
    return hi, mid, lo


def _ada_kernel(c_ref, w_ref, b_ref, o_ref):
    cond = _silu(c_ref[...])
    o_ref[...] = _dot(cond.astype(BF16), w_ref[...].astype(BF16)) + b_ref[...]


def _ada(cond_rows, w_ada, b_ada):
    r, d = cond_rows.shape
    n = w_ada.shape[1]
    tn = 1024
    return pl.pallas_call(
        _ada_kernel,
        grid=(n // tn,),
        in_specs=[pl.BlockSpec((r, d), lambda j: (0, 0)),
                  pl.BlockSpec((d, tn), lambda j: (0, j)),
                  pl.BlockSpec((1, tn), lambda j: (0, j))],
        out_specs=pl.BlockSpec((r, tn), lambda j: (0, j)),
        out_shape=jax.ShapeDtypeStruct((r, n), F32),
        compiler_params=_cparams(("arbitrary",)),
        name="ada",
    )(cond_rows, w_ada, b_ada.reshape(1, n))


def _inproj_kernel(x_ref, sh_ref, sc_ref, w_ref, o_ref, h_ref):
    @pl.when(pl.program_id(2) == 0)
    def _():
        h = _normalize(x_ref[...]) * (1.0 + sc_ref[...]) + sh_ref[...]
        h_ref[...] = h.astype(BF16)

    o_ref[...] = _dot(h_ref[...], w_ref[...])


def _inproj(x, shift, scale, w_in_bf16):
    b, s, d = x.shape
    tm = min(1024, s)
    nj = w_in_bf16.shape[1] // d
    return pl.pallas_call(
        _inproj_kernel,
        grid=(b, s // tm, nj),
        in_specs=[pl.BlockSpec((None, tm, d), lambda bi, i, j: (bi, i, 0)),
                  pl.BlockSpec((None, 1, d), lambda bi, i, j: (bi, 0, 0)),
                  pl.BlockSpec((None, 1, d), lambda bi, i, j: (bi, 0, 0)),
                  pl.BlockSpec((d, d), lambda bi, i, j: (0, j))],
        out_specs=pl.BlockSpec((None, None, tm, d), lambda bi, i, j: (j, bi, i, 0)),
        out_shape=jax.ShapeDtypeStruct((nj, b, s, d), F32),
        scratch_shapes=[pltpu.VMEM((tm, d), BF16)],
        compiler_params=_cparams(("arbitrary", "arbitrary", "arbitrary")),
        name="inproj",
    )(x, shift, scale, w_in_bf16)


def _hgrn_gates(q, fraw, v, lb, tri_bf16, last_row):
    f = lb + (1.0 - lb) * jax.nn.sigmoid(fraw)
    k = 1.0 - f
    lf = jnp.log(f)
    hi, mid, lo = _split3(lf)
    a = _dot(tri_bf16, hi) + _dot(tri_bf16, mid) + _dot(tri_bf16, lo)
    a_last = a[last_row:last_row + 1, :]
    kd = (k * jnp.exp(a_last - a)).astype(BF16)
    decay = jnp.exp(a_last)
    qa = kb = None
    if q is not None:
        qa = (_silu(q) * jnp.exp(a)).astype(BF16)
        kb = (k * jnp.exp(-a)).astype(BF16)
    return qa, kb, kd, v.astype(BF16), decay


def _hgrn_chunks(chunks, st_ref):
    first = []
    for d, ((qa, kb, kd, vb, decay), keep) in enumerate(chunks):
        for h in range(HG_HEADS):
            sl = slice(h * HG_DK, (h + 1) * HG_DK)
            st = st_ref[d, h]
            if qa is not None:
                first.append((_dot_nt(qa[:, sl], kb[:, sl]), _dot_nt(qa[:, sl], st.astype(BF16))))
            st_ref[d, h] = st * decay[:, sl] + _dot_tn(vb[:, sl], kd[:, sl])
    results = []
    for d, ((qa, kb, kd, vb, decay), keep) in enumerate(chunks):
        if qa is None:
            results.append(None)
            continue
        outs = []
        for h in range(HG_HEADS):
            sl = slice(h * HG_DK, (h + 1) * HG_DK)
            s_qk, o_state = first.pop(0)
            outs.append(_dot(jnp.where(keep, s_qk, 0.0).astype(BF16), vb[:, sl]) + o_state)
        results.append(jnp.concatenate(outs, axis=-1))
    return results


def _hgrn_kernel(qf_ref, ff_ref, if_ref, qb_ref, fb_ref, ib_ref, cff_ref, cfb_ref, ci_ref,
                 lbf_ref, lbb_ref, of_ref, ob_ref, st_ref, *, n_sub, n_ctx_sub):
    n = pl.program_id(1)
    c = HG_CHUNK
    row = lax.broadcasted_iota(jnp.int32, (c, c), 0)
    col = lax.broadcasted_iota(jnp.int32, (c, c), 1)
    keep_f = col <= row
    keep_b = col >= row
    tri_f = keep_f.astype(F32).astype(BF16)
    tri_b = keep_b.astype(F32).astype(BF16)
    lbf = lbf_ref[...]
    lbb = lbb_ref[...]

    @pl.when(n == 0)
    def _():
        st_ref[...] = jnp.zeros_like(st_ref)

        def body(i, carry):
            r0 = pl.multiple_of(i * c, c)
            r1 = pl.multiple_of((n_ctx_sub - 1 - i) * c, c)
            gf = _hgrn_gates(None, cff_ref[pl.ds(r0, c), :], ci_ref[pl.ds(r0, c), :], lbf, tri_f, c - 1)
            gb = _hgrn_gates(None, cfb_ref[pl.ds(r1, c), :], ci_ref[pl.ds(r1, c), :], lbb, tri_b, 0)
            _hgrn_chunks([(gf, keep_f), (gb, keep_b)], st_ref)
            return carry

        lax.fori_loop(0, n_ctx_sub, body, 0)

    @pl.when(n > 0)
    def _():
        def body(i, carry):
            r0 = pl.multiple_of(i * c, c)
            r1 = pl.multiple_of((n_sub - 1 - i) * c, c)
            gf = _hgrn_gates(qf_ref[pl.ds(r0, c), :], ff_ref[pl.ds(r0, c), :], if_ref[pl.ds(r0, c), :],
                             lbf, tri_f, c - 1)
            gb = _hgrn_gates(qb_ref[pl.ds(r1, c), :], fb_ref[pl.ds(r1, c), :], ib_ref[pl.ds(r1, c), :],
                             lbb, tri_b, 0)
            o_f, o_b = _hgrn_chunks([(gf, keep_f), (gb, keep_b)], st_ref)
            of_ref[pl.ds(r0, c), :] = o_f
            ob_ref[pl.ds(r1, c), :] = o_b
            return carry

        lax.fori_loop(0, n_sub, body, 0)


def _hgrn(p, pc, lb_fwd, lb_bwd):
    _, b, s, w = p.shape
    ctx_len = pc.shape[2]
    tb = min(256, s)
    nb = s // tb
    fwd = lambda bi, n: jnp.maximum(n - 1, 0)
    bwd = lambda bi, n: nb - 1 - jnp.maximum(n - 1, 0)

    def sec(section, blk):
        return pl.BlockSpec((None, None, tb, w), lambda bi, n: (section, bi, blk(bi, n), 0))

    def csec(section):
        return pl.BlockSpec((None, None, ctx_len, w), lambda bi, n: (section, bi, 0, 0))

    vec = pl.BlockSpec((1, w), lambda bi, n: (0, 0))
    kern = functools.partial(_hgrn_kernel, n_sub=tb // HG_CHUNK, n_ctx_sub=ctx_len // HG_CHUNK)
    return pl.pallas_call(
        kern,
        grid=(b, nb + 1),
        in_specs=[sec(SEC_Q, fwd), sec(SEC_FF, fwd), sec(SEC_I, fwd),
                  sec(SEC_Q, bwd), sec(SEC_FB, bwd), sec(SEC_I, bwd),
                  csec(SEC_FF), csec(SEC_FB), csec(SEC_I), vec, vec],
        out_specs=[pl.BlockSpec((None, tb, w), lambda bi, n: (bi, fwd(bi, n), 0)),
                   pl.BlockSpec((None, tb, w), lambda bi, n: (bi, bwd(bi, n), 0))],
        out_shape=[jax.ShapeDtypeStruct((b, s, w), F32), jax.ShapeDtypeStruct((b, s, w), F32)],
        scratch_shapes=[pltpu.VMEM((2, HG_HEADS, HG_DK, HG_DK), F32)],
        compiler_params=_cparams(("arbitrary", "arbitrary")),
        name="hgrn",
    )(p, p, p, p, p, p, pc, pc, pc, lb_fwd.reshape(1, w), lb_bwd.reshape(1, w))


NA_ROWS_PER_STEP = 8
NA_PREP_ROWS = 512
NA_KEY_TILE = 128
NA_SPAN = (NA_WIN_R + 2) * GRID_W


def _rope(t, cos, sin_signed, first_half):
    w = t.shape[-1]
    partner = jnp.where(first_half, pltpu.roll(t, w - 16, 1), pltpu.roll(t, 16, 1))
    return t * cos + partner * sin_signed


def _fold_lanes(op, *arrays):
    tiles = [a[:, c:c + 128] for a in arrays for c in range(0, a.shape[-1], 128)]
    acc = tiles[0]
    for t in tiles[1:]:
        acc = op(acc, t)
    return acc


def _rope_tables(rowtab_ref, coltab_ref, row0, n_rows, row_lane):
    out = []
    for i in range(2):
        rt = rowtab_ref[i, pl.ds(row0, n_rows), :]
        by_row = jnp.concatenate([jnp.broadcast_to(rt[r:r + 1, :], (GRID_W, rt.shape[1])) for r in range(n_rows)],
                                 axis=0)
        by_col = jnp.concatenate([coltab_ref[i]] * n_rows, axis=0)
        out.append(jnp.where(row_lane, by_row, by_col))
    return out


def _natten_kernel(q_ref, k_ref, v_ref, kc_ref, vc_ref, rowtab_ref, coltab_ref, t2_ref, o_ref,
                   kt_s, v_s, kc_s, vc_s, bias_s, *, rows):
    rblk = pl.program_id(2)
    hd = NA_HD
    lane = lax.broadcasted_iota(jnp.int32, (1, 2 * hd), 1)
    first_half = (lane % 32) < 16
    row_lane = (lane % hd) < hd // 2
    scale = NA_HD ** -0.5

    def values_and_ones(v_pair, h):
        vh = v_pair if h == 0 else pltpu.roll(v_pair, hd, 1)
        return jnp.where(lane < hd, vh, jnp.where(lane == hd, 1.0, 0.0)).astype(BF16)

    @pl.when(rblk == 0)
    def _():
        kc = kc_ref[...].astype(BF16)
        qi = lax.broadcasted_iota(jnp.int32, (GRID_W, GRID_W), 0)
        ki = lax.broadcasted_iota(jnp.int32, (GRID_W, GRID_W), 1)
        cstart = jnp.clip(qi - NA_WIN_C // 2, 0, GRID_W - NA_WIN_C)
        in_win = (ki >= cstart) & (ki < cstart + NA_WIN_C)
        masked = jnp.full((GRID_W, GRID_W), NEG_INF, F32)
        s_len = k_ref.shape[0]
        for h in range(2):
            sl = slice(h * hd, (h + 1) * hd)
            kc_s[h] = kc[:, sl]
            vc_s[h] = values_and_ones(vc_ref[...], h)
            kt_s[h, s_len // NA_KEY_TILE] = jnp.zeros((hd, NA_KEY_TILE), BF16)
            v_s[h, s_len:s_len + NA_KEY_TILE, :] = jnp.zeros((NA_KEY_TILE, 2 * hd), BF16)
            tiles = [jnp.where(in_win, t2_ref[h, dr], NEG_INF) for dr in range(2 * NA_WIN_R - 1)]
            for bidx in range(NA_WIN_R + 1):
                v, par = (bidx, 0) if bidx < NA_WIN_R else (NA_WIN_R // 2, 1)
                for piece in range(NA_SPAN // GRID_W):
                    j = piece - par
                    tile = tiles[NA_WIN_R - 1 - v + j] if 0 <= j < NA_WIN_R else masked
                    bias_s[h, bidx, :, piece * GRID_W:(piece + 1) * GRID_W] = tile

        def prep(i, carry):
            r0 = pl.multiple_of(i * NA_PREP_ROWS, NA_PREP_ROWS)
            rws = pl.ds(r0, NA_PREP_ROWS)
            cos, sin = _rope_tables(rowtab_ref, coltab_ref, i * (NA_PREP_ROWS // GRID_W), NA_PREP_ROWS // GRID_W,
                                    row_lane)
            kr = _rope(k_ref[rws, :], cos, sin, first_half)
            krt = kr.T.astype(BF16)
            vv = v_ref[rws, :]
            for h in range(2):
                sl = slice(h * hd, (h + 1) * hd)
                for c in range(NA_PREP_ROWS // NA_KEY_TILE):
                    kt_s[h, i * (NA_PREP_ROWS // NA_KEY_TILE) + c] = krt[sl, c * NA_KEY_TILE:(c + 1) * NA_KEY_TILE]
                v_s[h, rws, :] = values_and_ones(vv, h)
            return carry

        lax.fori_loop(0, s_len // NA_PREP_ROWS, prep, 0)

    tq = NA_ROWS_PER_STEP * GRID_W
    q = q_ref[...] * scale
    cos, sin = _rope_tables(rowtab_ref, coltab_ref, rblk * NA_ROWS_PER_STEP, NA_ROWS_PER_STEP, row_lane)
    qr = _rope(q, cos, sin, first_half)
    qb = q.astype(BF16)
    qrb = qr.astype(BF16)
    rws = [slice(rr * GRID_W, (rr + 1) * GRID_W) for rr in range(NA_ROWS_PER_STEP)]
    tile0, bidx = [], []
    for rr in range(NA_ROWS_PER_STEP):
        r = rblk * NA_ROWS_PER_STEP + rr
        rs = jnp.clip(r - NA_WIN_R // 2, 0, rows - NA_WIN_R)
        tile0.append(lax.shift_right_logical(rs, 1))
        bidx.append(jnp.where((rs & 1) == 1, NA_WIN_R, r - rs))

    def scores(h):
        sl = slice(h * hd, (h + 1) * hd)
        qrb_h = qrb[:, sl]
        s_ctx_all = _dot_nt(qb[:, sl], kc_s[h])
        s_win = []
        for rr in range(NA_ROWS_PER_STEP):
            kt = kt_s[h, pl.ds(tile0[rr], NA_SPAN // NA_KEY_TILE)]
            kt = jnp.concatenate([kt[c] for c in range(NA_SPAN // NA_KEY_TILE)], axis=-1)
            s_win.append(_dot(qrb_h[rws[rr]], kt))
        return s_win, s_ctx_all

    def softmax(h, s_win, s_ctx_all):
        e_win, e_ctx = [], []
        for rr in range(NA_ROWS_PER_STEP):
            sw = s_win[rr] + bias_s[h, bidx[rr]]
            sc = s_ctx_all[rws[rr]]
            m = jnp.max(_fold_lanes(jnp.maximum, sw, sc), axis=-1, keepdims=True)
            e_win.append(jnp.exp(sw - m).astype(BF16))
            e_ctx.append(jnp.exp(sc - m).astype(BF16))
        return e_win, e_ctx

    def values(h, e_win, e_ctx):
        o_win = []
        for rr in range(NA_ROWS_PER_STEP):
            k0 = pl.multiple_of(tile0[rr] * NA_KEY_TILE, NA_KEY_TILE)
            o_win.append(_dot(e_win[rr], v_s[h, pl.ds(k0, NA_SPAN), :]))
        o = jnp.concatenate(o_win, axis=0) + _dot(jnp.concatenate(e_ctx, axis=0), vc_s[h])
        return o[:, :hd] * (1.0 / o[:, hd:hd + 1])

    s0 = scores(0)
    s1 = scores(1)
    p0 = softmax(0, *s0)
    o0 = values(0, *p0)
    p1 = softmax(1, *s1)
    o1 = values(1, *p1)
    o_ref[...] = jnp.concatenate([o0, o1], axis=-1)


def _na_tables(rpb, s):
    half = NA_HD // 2
    inv = jnp.power(ROPE_THETA, -jnp.arange(0, half, 2, dtype=F32) / half)

    def tables(n):
        ang = jnp.arange(n, dtype=F32)[:, None] * inv[None, :]
        reps = 2 * NA_HD // half
        return jnp.stack([jnp.tile(jnp.cos(ang), (1, 2 * reps)),
                          jnp.tile(jnp.concatenate([-jnp.sin(ang), jnp.sin(ang)], axis=-1), (1, reps))])

    rowtab, coltab = tables(s // GRID_W), tables(GRID_W)

    pad = GRID_W - NA_WIN_C
    rp = jnp.pad(rpb.astype(F32), ((0, 0), (0, 0), (pad, pad)), mode="edge")
    t2 = jnp.stack([rp[:, :, GRID_W - 1 - qc:2 * GRID_W - 1 - qc] for qc in range(GRID_W)], axis=2)
    return rowtab, coltab, t2


def _natten(p, pc, rowtab, coltab, t2):
    _, b, s, w = p.shape
    ctx_len = pc.shape[2]
    rows = s // GRID_W
    assert rows >= NA_WIN_R and rows % NA_ROWS_PER_STEP == 0
    tq = NA_ROWS_PER_STEP * GRID_W
    hw = 2 * NA_HD
    nhp = w // hw
    kern = functools.partial(_natten_kernel, rows=rows)
    return pl.pallas_call(
        kern,
        grid=(b, nhp, rows // NA_ROWS_PER_STEP),
        in_specs=[pl.BlockSpec((None, None, tq, hw), lambda bi, hp, r: (SEC_NQ, bi, r, hp)),
                  pl.BlockSpec((None, None, s, hw), lambda bi, hp, r: (SEC_NK, bi, 0, hp)),
                  pl.BlockSpec((None, None, s, hw), lambda bi, hp, r: (SEC_NV, bi, 0, hp)),
                  pl.BlockSpec((None, None, ctx_len, hw), lambda bi, hp, r: (SEC_NK, bi, 0, hp)),
                  pl.BlockSpec((None, None, ctx_len, hw), lambda bi, hp, r: (SEC_NV, bi, 0, hp)),
                  pl.BlockSpec((2, rows, hw), lambda bi, hp, r: (0, 0, 0)),
                  pl.BlockSpec((2, GRID_W, hw), lambda bi, hp, r: (0, 0, 0)),
                  pl.BlockSpec((2, 2 * NA_WIN_R - 1, GRID_W, GRID_W), lambda bi, hp, r: (hp, 0, 0, 0))],
        out_specs=pl.BlockSpec((None, tq, hw), lambda bi, hp, r: (bi, r, hp)),
        out_shape=jax.ShapeDtypeStruct((b, s, w), F32),
        scratch_shapes=[pltpu.VMEM((2, s // NA_KEY_TILE + 1, NA_HD, NA_KEY_TILE), BF16),
                        pltpu.VMEM((2, s + NA_KEY_TILE, hw), BF16),
                        pltpu.VMEM((2, ctx_len, NA_HD), BF16), pltpu.VMEM((2, ctx_len, hw), BF16),
                        pltpu.VMEM((2, NA_WIN_R + 1, GRID_W, NA_SPAN), F32)],
        compiler_params=_cparams(("arbitrary", "arbitrary", "arbitrary")),
        name="natten",
    )(p, p, p, pc, pc, rowtab, coltab, t2)


def _route(logits_t, rbias):
    e, t = logits_t.shape
    gsz = e // N_GROUPS
    scores = jax.nn.sigmoid(logits_t)
    sel = scores + rbias
    neg = -jnp.inf
    sub = lax.broadcasted_iota(jnp.int32, (gsz, t), 0).astype(F32)
    gscore = []
    for g in range(N_GROUPS):
        grp = sel[g * gsz:(g + 1) * gsz, :]
        m1 = jnp.max(grp, axis=0, keepdims=True)
        first = jnp.min(jnp.where(grp == m1, sub, float(gsz)), axis=0, keepdims=True)
        m2 = jnp.max(jnp.where(sub == first, neg, grp), axis=0, keepdims=True)
        gscore.append(m1 + m2)
    masked = []
    for g in range(N_GROUPS):
        rank = jnp.zeros((1, t), F32)
        for g2 in range(N_GROUPS):
            if g2 == g:
                continue
            if g2 < g:
                ahead = gscore[g2] >= gscore[g]
            else:
                ahead = gscore[g2] > gscore[g]
            rank = rank + jnp.where(ahead, 1.0, 0.0)
        masked.append(jnp.where(rank < TOPK_GROUPS, sel[g * gsz:(g + 1) * gsz, :], neg))
    work = jnp.concatenate(masked, axis=0)
    eidx = lax.broadcasted_iota(jnp.int32, (e, t), 0).astype(F32)
    idxs, ws = [], []
    chosen = jnp.zeros((e, t), F32)
    for _ in range(TOP_K):
        m = jnp.max(work, axis=0, keepdims=True)
        first = jnp.min(jnp.where(work == m, eidx, float(e)), axis=0, keepdims=True)
        pick = eidx == first
        idxs.append(first)
        ws.append(jnp.sum(jnp.where(pick, scores, 0.0), axis=0, keepdims=True))
        chosen = jnp.where(pick, 1.0, chosen)
        work = jnp.where(pick, neg, work)
    w = jnp.concatenate(ws, axis=0)
    w = w / jnp.sum(w, axis=0, keepdims=True) * ROUTED_SCALE
    return jnp.concatenate(idxs, axis=0).astype(jnp.int32), w, chosen


MERGE_TOK = 512
MERGE_SUB = 256


def _merge_kernel(of_ref, ob_ref, og_ref, yna_ref, ga_ref, gb_ref, x_ref, g1_ref, sh2_ref, sc2_ref,
                  hgg_ref, ln1g_ref, ln1b_ref, wa_ref, wb_ref, wo_ref, wr_ref, rb_ref,
                  x1_ref, h2_ref, topi_ref, topw_ref, cnt_ref, *, alpha):
    tm = x_ref.shape[0]
    subs = [slice(i * MERGE_SUB, (i + 1) * MERGE_SUB) for i in range(tm // MERGE_SUB)]

    def branches(rows):
        o = of_ref[rows, :] + ob_ref[rows, :]
        parts = []
        for h in range(HG_HEADS):
            oh = o[:, h * HG_DK:(h + 1) * HG_DK]
            parts.append(oh * lax.rsqrt(jnp.mean(oh * oh, axis=-1, keepdims=True) + LN_EPS))
        y_hg = jnp.concatenate(parts, axis=-1) * hgg_ref[...] * _silu(og_ref[rows, :])
        return _dot(y_hg.astype(BF16), wa_ref[...]), _dot(yna_ref[rows, :].astype(BF16), wb_ref[...])

    def out_proj(rows, ya, yb):
        t = jax.nn.sigmoid(ga_ref[rows, :]) * ya + jax.nn.sigmoid(gb_ref[rows, :]) * yb
        return _dot(t.astype(BF16), wo_ref[...])

    def norms_router(rows, i, y):
        x1 = _normalize(alpha * x_ref[rows, :] + g1_ref[...] * y) * ln1g_ref[...] + ln1b_ref[...]
        x1_ref[rows, :] = x1
        h2 = _normalize(x1) * (1.0 + sc2_ref[...]) + sh2_ref[...]
        h2_ref[rows, :] = _pack_words(h2)
        hh, hm, hl = _split3(h2)
        wh, wm, wl = _split3(wr_ref[...])
        return (_dot_nt(wh, hh) + _dot_nt(wh, hm) + _dot_nt(wm, hh)
                + _dot_nt(wh, hl) + _dot_nt(wl, hh) + _dot_nt(wm, hm))

    ab = [branches(rows) for rows in subs]
    ys = [out_proj(rows, *ab[i]) for i, rows in enumerate(subs)]
    logits = [norms_router(rows, i, ys[i]) for i, rows in enumerate(subs)]

    @pl.when(pl.program_id(1) == 0)
    def _():
        cnt_ref[...] = jnp.zeros_like(cnt_ref)

    for i, rows in enumerate(subs):
        topi, topw, chosen = _route(logits[i], rb_ref[...])
        topi_ref[:, rows] = topi
        topw_ref[:, rows] = topw
        cnt_ref[...] += jnp.sum(chosen, axis=1, keepdims=True)


def _merge(o_f, o_b, p, y_na, x, g1, sh2, sc2, hg_norm_g, ln1_g, ln1_b, w_a, w_b, w_o, w_router_t, router_bias,
           alpha):
    b, s, d = x.shape
    tm = min(MERGE_TOK, s)
    e = w_router_t.shape[0]
    tok = lambda bi, i: (bi, i, 0)
    blk = pl.BlockSpec((None, tm, d), tok)

    def sec(section):
        return pl.BlockSpec((None, None, tm, d), lambda bi, i: (section, bi, i, 0))

    mod = pl.BlockSpec((None, 1, d), lambda bi, i: (bi, 0, 0))
    vec = pl.BlockSpec((1, d), lambda bi, i: (0, 0))
    mat = pl.BlockSpec((d, d), lambda bi, i: (0, 0), pipeline_mode=pl.Buffered(1))
    return pl.pallas_call(
        functools.partial(_merge_kernel, alpha=alpha),
        grid=(b, s // tm),
        in_specs=[blk, blk, sec(SEC_OG), blk, sec(SEC_GA), sec(SEC_GB), blk, mod, mod, mod,
                  vec, vec, vec, mat, mat, mat,
                  pl.BlockSpec((e, d), lambda bi, i: (0, 0)),
                  pl.BlockSpec((e, 1), lambda bi, i: (0, 0))],
        out_specs=[blk,
                   pl.BlockSpec((tm, d // 2), lambda bi, i: (bi * (s // tm) + i, 0)),
                   pl.BlockSpec((None, TOP_K, tm), lambda bi, i: (bi, 0, i)),
                   pl.BlockSpec((None, TOP_K, tm), lambda bi, i: (bi, 0, i)),
                   pl.BlockSpec((None, e, 128), lambda bi, i: (bi, 0, 0))],
        out_shape=[jax.ShapeDtypeStruct((b, s, d), F32),
                   jax.ShapeDtypeStruct((b * s, d // 2), U32),
                   jax.ShapeDtypeStruct((b, TOP_K, s), jnp.int32), jax.ShapeDtypeStruct((b, TOP_K, s), F32),
                   jax.ShapeDtypeStruct((b, e, 128), F32)],
        compiler_params=_cparams(("arbitrary", "arbitrary")),
        name="merge",
    )(o_f, o_b, p, y_na, p, p, x, g1, sh2, sc2, hg_norm_g.reshape(1, d), ln1_g.reshape(1, d),
      ln1_b.reshape(1, d), w_a, w_b, w_o, w_router_t, router_bias.reshape(e, 1))


MOE_TILE = 512
MOE_TOK = 256


def _plan_kernel(topi_ref, off_ref, dest_ref, carry_ref):
    @pl.when(pl.program_id(0) == 0)
    def _():
        carry_ref[...] = jnp.zeros_like(carry_ref)

    topi = topi_ref[...]
    tok = topi.shape[1]
    eidx = lax.broadcasted_iota(jnp.int32, (N_EXPERTS, tok), 0)
    hits = [eidx == topi[k:k + 1, :] for k in range(TOP_K)]
    m = jnp.zeros((N_EXPERTS, tok), F32)
    for hit in hits:
        m = jnp.where(hit, 1.0, m)
    before = (lax.broadcasted_iota(jnp.int32, (tok, tok), 0)
              < lax.broadcasted_iota(jnp.int32, (tok, tok), 1)).astype(F32).astype(BF16)
    row = off_ref[...] + carry_ref[...] + _dot(m.astype(BF16), before)
    dest = [jnp.sum(jnp.where(hit, row, 0.0), axis=0, keepdims=True) for hit in hits]
    dest_ref[...] = jnp.concatenate(dest, axis=0).astype(jnp.int32)
    carry_ref[...] += jnp.sum(m, axis=1, keepdims=True)


def _plan(topi, seg_off, batch):
    _, k, s = topi.shape
    return pl.pallas_call(
        _plan_kernel,
        grid=(s // MOE_TOK,),
        in_specs=[pl.BlockSpec((None, k, MOE_TOK), lambda i: (batch, 0, i)),
                  pl.BlockSpec((N_EXPERTS, 1), lambda i: (0, 0))],
        out_specs=pl.BlockSpec((k, MOE_TOK), lambda i: (0, i)),
        out_shape=jax.ShapeDtypeStruct((k, s), jnp.int32),
        scratch_shapes=[pltpu.VMEM((N_EXPERTS, 1), F32)],
        compiler_params=_cparams(("arbitrary",)),
        name="plan",
    )(topi, seg_off.astype(F32).reshape(N_EXPERTS, 1))


SC_WINDOW = 128


def _sc_workers():
    info = plsc.get_sparse_core_info()
    return info.num_cores, info.num_cores * info.num_subcores


def _scatter_rows(src, src_first, t, idx, zero_idx, n_rows):
    w = src.shape[1]
    m, mz = idx.shape[0], zero_idx.shape[0]
    n_cores, n_workers = _sc_workers()
    per_worker, per_worker_z = m // n_workers, mz // n_workers
    assert m % t == 0 and t % SC_WINDOW == 0
    assert per_worker * n_workers == m and per_worker % SC_WINDOW == 0
    assert per_worker_z * n_workers == mz and per_worker_z % SC_WINDOW == 0
    mesh = plsc.VectorSubcoreMesh(core_axis_name="core", subcore_axis_name="subcore")

    @functools.partial(
        pl.kernel, mesh=mesh, out_type=jax.ShapeDtypeStruct((n_rows, w), src.dtype),
        scratch_types=[pltpu.VMEM((SC_WINDOW,), jnp.int32), pltpu.VMEM((SC_WINDOW, w), src.dtype),
                       pltpu.SemaphoreType.DMA])
    def scatter(src_hbm, idx_hbm, zeros_hbm, zero_idx_hbm, out_hbm, idx_v, rows_v, sem):
        worker = lax.axis_index("subcore") * n_cores + lax.axis_index("core")

        @pl.loop(0, per_worker // SC_WINDOW)
        def _(step):
            base = pl.multiple_of(worker * per_worker + step * SC_WINDOW, SC_WINDOW)
            first = pl.multiple_of(src_first + lax.rem(base, t), SC_WINDOW)
            pltpu.sync_copy(idx_hbm.at[pl.ds(base, SC_WINDOW)], idx_v)
            pltpu.sync_copy(src_hbm.at[pl.ds(first, SC_WINDOW)], rows_v)
            pltpu.async_copy(rows_v, out_hbm.at[idx_v], sem).wait()

        pltpu.sync_copy(zeros_hbm, rows_v)

        @pl.loop(0, per_worker_z // SC_WINDOW)
        def _(step):
            base = pl.multiple_of(worker * per_worker_z + step * SC_WINDOW, SC_WINDOW)
            pltpu.sync_copy(zero_idx_hbm.at[pl.ds(base, SC_WINDOW)], idx_v)
            pltpu.async_copy(rows_v, out_hbm.at[idx_v], sem).wait()

    return scatter(src, idx, jnp.zeros((SC_WINDOW, w), src.dtype), zero_idx)


def _experts_kernel(te_ref, tb_ref, nt_ref, xs_ref, wg_ref, wu_ref, wd_ref, ys_ref):
    @pl.when(pl.program_id(0) < nt_ref[0])
    def _():
        x = _unpack_words(xs_ref[...]).astype(BF16)
        act = _silu(_dot(x, wg_ref[...].astype(BF16))) * _dot(x, wu_ref[...].astype(BF16))
        ys_ref[...] = _pack_words(_dot(act.astype(BF16), wd_ref[...].astype(BF16)))


def _experts(xs, tile_expert, tile_block, n_tiles, wg, wu, wd):
    d, f = wg.shape[1], wg.shape[2]
    rows = pl.BlockSpec((MOE_TILE, d // 2), lambda i, te, tb, nt: (tb[i], 0))
    grid_spec = pltpu.PrefetchScalarGridSpec(
        num_scalar_prefetch=3,
        grid=(xs.shape[0] // MOE_TILE,),
        in_specs=[rows,
                  pl.BlockSpec((None, d, f), lambda i, te, tb, nt: (te[i], 0, 0)),
                  pl.BlockSpec((None, d, f), lambda i, te, tb, nt: (te[i], 0, 0)),
                  pl.BlockSpec((None, f, d), lambda i, te, tb, nt: (te[i], 0, 0))],
        out_specs=rows,
    )
    return pl.pallas_call(
        _experts_kernel,
        grid_spec=grid_spec,
        out_shape=jax.ShapeDtypeStruct(xs.shape, U32),
        compiler_params=_cparams(("arbitrary",)),
        name="experts",
    )(tile_expert, tile_block, n_tiles, xs, wg, wu, wd)


def _gather_rows(table, idx):
    m = idx.shape[0]
    w = table.shape[1]
    n_cores, n_workers = _sc_workers()
    per_worker = m // n_workers
    assert per_worker * n_workers == m and per_worker % SC_WINDOW == 0
    mesh = plsc.VectorSubcoreMesh(core_axis_name="core", subcore_axis_name="subcore")

    @functools.partial(
        pl.kernel, mesh=mesh, out_type=jax.ShapeDtypeStruct((m, w), table.dtype),
        scratch_types=[pltpu.VMEM((SC_WINDOW,), jnp.int32), pltpu.VMEM((SC_WINDOW, w), table.dtype),
                       pltpu.SemaphoreType.DMA])
    def gather(table_hbm, idx_hbm, out_hbm, idx_v, rows_v, sem):
        worker = lax.axis_index("subcore") * n_cores + lax.axis_index("core")

        @pl.loop(0, per_worker // SC_WINDOW)
        def _(step):
            base = pl.multiple_of(worker * per_worker + step * SC_WINDOW, SC_WINDOW)
            pltpu.sync_copy(idx_hbm.at[pl.ds(base, SC_WINDOW)], idx_v)
            pltpu.async_copy(table_hbm.at[idx_v], rows_v, sem).wait()
            pltpu.sync_copy(rows_v, out_hbm.at[pl.ds(base, SC_WINDOW)])

    return gather(table, idx)


def _combine_kernel(rows_ref, topw_ref, h_ref, x1_ref, g2_ref, sg_ref, su_ref, sd_ref, ln2g_ref, ln2b_ref,
                    *rest, alpha):
    o_ref = rest[-1]
    h = _unpack_words(h_ref[...]).astype(BF16)
    act = _silu(_dot(h, sg_ref[...])) * _dot(h, su_ref[...])
    y = _dot(act.astype(BF16), sd_ref[...])
    w = topw_ref[...].T
    for k in range(TOP_K):
        y = y + w[:, k:k + 1] * _unpack_words(rows_ref[k])
    o_ref[...] = _normalize(alpha * x1_ref[...] + g2_ref[...] * y) * ln2g_ref[...] + ln2b_ref[...]


def _combine(gathered, topw, h2p, x1, g2, sg, su, sd, ln2_g, ln2_b, alpha, batch, out_prev):
    t, d = x1.shape
    _, k, s = topw.shape
    per_b = s // MOE_TOK
    fs = sg.shape[1]
    rows = pl.BlockSpec((MOE_TOK, d), lambda i: (batch * per_b + i, 0))
    packed = pl.BlockSpec((MOE_TOK, d // 2), lambda i: (batch * per_b + i, 0))
    vec = pl.BlockSpec((1, d), lambda i: (0, 0))
    in_specs = [pl.BlockSpec((k, MOE_TOK, d // 2), lambda i: (0, i, 0)),
                pl.BlockSpec((None, k, MOE_TOK), lambda i: (batch, 0, i)),
                packed, rows,
                pl.BlockSpec((None, 1, d), lambda i: (batch, 0, 0)),
                pl.BlockSpec((d, fs), lambda i: (0, 0)),
                pl.BlockSpec((d, fs), lambda i: (0, 0)),
                pl.BlockSpec((fs, d), lambda i: (0, 0)),
                vec, vec]
    args = [gathered, topw, h2p, x1, g2, sg, su, sd, ln2_g.reshape(1, d), ln2_b.reshape(1, d)]
    aliases = {}
    if out_prev is not None:
        in_specs.append(pl.BlockSpec(memory_space=pl.ANY))
        args.append(out_prev)
        aliases = {len(args) - 1: 0}
    return pl.pallas_call(
        functools.partial(_combine_kernel, alpha=alpha),
        grid=(per_b,),
        in_specs=in_specs,
        out_specs=rows,
        out_shape=jax.ShapeDtypeStruct((t, d), F32),
        input_output_aliases=aliases,
        compiler_params=_cparams(("arbitrary",)),
        name="combine",
    )(*args)


def _moe(h2p, topi, topw, cnt, x1, g2, wg, wu, wd, sg, su, sd, ln2_g, ln2_b, alpha):
    b, s, d = x1.shape
    x1 = x1.reshape(b * s, d)
    n_tiles_max = s * TOP_K // MOE_TILE + N_EXPERTS
    out = None
    for bi in range(b):
        cnt_b = cnt[bi, :, 0].astype(jnp.int32)
        tiles_e = (cnt_b + (MOE_TILE - 1)) // MOE_TILE
        tiles_cum = jnp.cumsum(tiles_e)
        seg_off = (tiles_cum - tiles_e) * MOE_TILE
        tile_block = jnp.minimum(jnp.arange(n_tiles_max, dtype=jnp.int32), tiles_cum[-1] - 1)
        tile_expert = jnp.sum((tiles_cum[None, :] <= tile_block[:, None]).astype(jnp.int32), axis=1)
        n_tiles = tiles_cum[-1:].astype(jnp.int32)

        dest = _plan(topi, seg_off, bi).reshape(TOP_K * s)
        j = jnp.arange(MOE_TILE, dtype=jnp.int32)[None, :]
        n_pad = (tiles_e * MOE_TILE - cnt_b)[:, None]
        spare = (n_tiles_max - 1) * MOE_TILE + j
        zero_idx = jnp.where(j < n_pad, (seg_off + cnt_b)[:, None] + j, spare).reshape(N_EXPERTS * MOE_TILE)
        xs = _scatter_rows(h2p, bi * s, s, dest, zero_idx.astype(jnp.int32), n_tiles_max * MOE_TILE)
        ys = _experts(xs, tile_expert, tile_block, n_tiles, wg, wu, wd)
        gathered = _gather_rows(ys, dest).reshape(TOP_K, s, d // 2)
        out = _combine(gathered, topw, h2p, x1, g2, sg, su, sd, ln2_g, ln2_b, alpha, bi, out)
    return out.reshape(b, s, d)


def kernel(x, c, ctx, c_ctx, w_ada, b_ada, w_in, hg_lb_fwd, hg_lb_bwd, hg_norm_g, na_rpb, w_branch_a, w_branch_b, w_out, ln1_g, ln1_b, w_router, router_bias, w_e_gate, w_e_up, w_e_down, w_sh_gate, w_sh_up, w_sh_down, ln2_g, ln2_b):
    depth = w_ada.shape[0]
    assert depth == 1, "single-layer block"
    b, s, d = x.shape
    alpha = (2.0 * depth) ** 0.25
    l = 0
    lb_fwd = jnp.cumsum(jax.nn.softmax(hg_lb_fwd.astype(F32), axis=0), axis=0)[l]
    lb_bwd = jnp.cumsum(jax.nn.softmax(hg_lb_bwd.astype(F32), axis=0), axis=0)[l]

    cond_rows = jnp.concatenate([c, c_ctx[None, :], jnp.zeros((8 - b - 1, d), F32)], axis=0)
    mod = _ada(cond_rows, w_ada[l], b_ada[l])
    sh1, sc1, g1, sh2, sc2, g2 = [m[:b, None, :] for m in jnp.split(mod, 6, axis=-1)]
    csh1, csc1 = [jnp.broadcast_to(m[b:b + 1, None, :], (b, 1, d)) for m in jnp.split(mod, 6, axis=-1)[:2]]

    w_in_b = w_in[l].astype(BF16)
    p = _inproj(x, sh1, sc1, w_in_b)
    pc = _inproj(ctx, csh1, csc1, w_in_b)

    o_f, o_b = _hgrn(p, pc, lb_fwd, lb_bwd)
    y_na = _natten(p, pc, *_na_tables(na_rpb[l], s))

    x1, h2, topi, topw, cnt = _merge(o_f, o_b, p, y_na, x, g1, sh2, sc2, hg_norm_g[l], ln1_g[l], ln1_b[l],
                                     w_branch_a[l].astype(BF16), w_branch_b[l].astype(BF16),
                                     w_out[l].astype(BF16), w_router[l].T, router_bias[l], alpha)

    return _moe(h2, topi, topw, cnt, x1, g2,
                w_e_gate[l], w_e_up[l], w_e_down[l],
                w_sh_gate[l].astype(BF16), w_sh_up[l].astype(BF16), w_sh_down[l].astype(BF16),
                ln2_g[l], ln2_b[l], alpha)
```

```python
import functools

import numpy as np
import jax
import jax.numpy as jnp
from jax import lax
from jax.experimental import pallas as pl
from jax.experimental.pallas import tpu as pltpu
from jax.experimental.pallas import tpu_sc as plsc

F32 = jnp.float32
BF16 = jnp.bfloat16

D_MODEL = 1024
GRID_W = 64
HG_HEADS = 8
HG_DK = 128
HG_CHUNK = 64
NA_HEADS = 16
NA_HD = 64
NA_WIN_R = 8
NA_WIN_C = 16
ROPE_THETA = 10000.0
NEG_INF = -1e30
N_EXPERTS = 64
EXPERT_DIM = 256
TOP_K = 8
N_GROUPS = 8
TOPK_GROUPS = 4
ROUTED_SCALE = 2.5
LN_EPS = 1e-6
N_SECTIONS = 10
SEC_Q, SEC_FF, SEC_FB, SEC_I, SEC_OG, SEC_NQ, SEC_NK, SEC_NV, SEC_GA, SEC_GB = range(10)

VMEM_LIMIT = 56 * 1024 * 1024


def _cparams(sem):
    return pltpu.CompilerParams(dimension_semantics=sem, vmem_limit_bytes=VMEM_LIMIT)


def _normalize(x):
    mu = jnp.mean(x, axis=-1, keepdims=True)
    xc = x - mu
    var = jnp.mean(xc * xc, axis=-1, keepdims=True)
    return xc * lax.rsqrt(var + LN_EPS)


def _silu(x):
    return x * jax.nn.sigmoid(x)


def _dot(a, b):
    return jnp.dot(a, b, preferred_element_type=F32)


def _dot_nt(a, b):
    return lax.dot_general(a, b, (((1,), (1,)), ((), ())), preferred_element_type=F32)


def _dot_tn(a, b):
    return lax.dot_general(a, b, (((0,), (0,)), ((), ())), preferred_element_type=F32)


U32 = jnp.uint32


def _pack_words(x):
    half = x.shape[1] // 2
    lo = lax.bitcast_convert_type(x[:, :half].astype(BF16).astype(F32), U32) >> 16
    hi = lax.bitcast_convert_type(x[:, half:].astype(BF16).astype(F32), U32) & jnp.uint32(0xFFFF0000)
    return lo | hi


def _unpack_words(w):
    lo = lax.bitcast_convert_type(w << 16, F32)
    hi = lax.bitcast_convert_type(w & jnp.uint32(0xFFFF0000), F32)
    return jnp.concatenate([lo, hi], axis=-1)


def _split3(x):
    hi = x.astype(BF16)
    r1 = x - hi.astype(F32)
    mid = r1.astype(BF16)
    lo = (r1 - mid.astype(F32)).astype(BF16)
    return hi, mid, lo


def _ada_kernel(c_ref, w_ref, b_ref, o_ref):
    cond = _silu(c_ref[...])
    o_ref[...] = _dot(cond.astype(BF16), w_ref[...].astype(BF16)) + b_ref[...]


def _ada(cond_rows, w_ada, b_ada):
    r, d = cond_rows.shape
    n = w_ada.shape[1]
    tn = 1024
    return pl.pallas_call(
        _ada_kernel,
        grid=(n // tn,),
        in_specs=[pl.BlockSpec((r, d), lambda j: (0, 0)),
                  pl.BlockSpec((d, tn), lambda j: (0, j)),
                  pl.BlockSpec((1, tn), lambda j: (0, j))],
        out_specs=pl.BlockSpec((r, tn), lambda j: (0, j)),
        out_shape=jax.ShapeDtypeStruct((r, n), F32),
        compiler_params=_cparams(("arbitrary",)),
        name="ada",
    )(cond_rows, w_ada, b_ada.reshape(1, n))


INPROJ_TOK = 2048


def _inproj_kernel(x_ref, sh_ref, sc_ref, w_ref, o_ref, h_ref):
    @pl.when(pl.program_id(2) == 0)
    def _():
        h = _normalize(x_ref[...]) * (1.0 + sc_ref[...]) + sh_ref[...]
        h_ref[...] = h.astype(BF16)

    o_ref[...] = _dot(h_ref[...], w_ref[...])


def _inproj(x, shift, scale, w_in_bf16):
    b, s, d = x.shape
    tm = min(INPROJ_TOK, s)
    nj = w_in_bf16.shape[1] // d
    return pl.pallas_call(
        _inproj_kernel,
        grid=(b, s // tm, nj),
        in_specs=[pl.BlockSpec((None, tm, d), lambda bi, i, j: (bi, i, 0)),
                  pl.BlockSpec((None, 1, d), lambda bi, i, j: (bi, 0, 0)),
                  pl.BlockSpec((None, 1, d), lambda bi, i, j: (bi, 0, 0)),
                  pl.BlockSpec((d, d), lambda bi, i, j: (0, j))],
        out_specs=pl.BlockSpec((None, None, tm, d), lambda bi, i, j: (j, bi, i, 0)),
        out_shape=jax.ShapeDtypeStruct((nj, b, s, d), F32),
        scratch_shapes=[pltpu.VMEM((tm, d), BF16)],
        compiler_params=_cparams(("arbitrary", "arbitrary", "arbitrary")),
        name="inproj",
    )(x, shift, scale, w_in_bf16)


def _hgrn_gates(q, fraw, v, lb, tri_bf16, last_row):
    f = lb + (1.0 - lb) * jax.nn.sigmoid(fraw)
    k = 1.0 - f
    lf = jnp.log(f)
    hi, mid, lo = _split3(lf)
    a = _dot(tri_bf16, hi) + _dot(tri_bf16, mid) + _dot(tri_bf16, lo)
    a_last = a[last_row:last_row + 1, :]
    kd = (k * jnp.exp(a_last - a)).astype(BF16)
    decay = jnp.exp(a_last)
    qa = kb = None
    if q is not None:
        qa = (_silu(q) * jnp.exp(a)).astype(BF16)
        kb = (k * jnp.exp(-a)).astype(BF16)
    return qa, kb, kd, v.astype(BF16), decay


def _hgrn_chunks(chunks, st_ref):
    first = []
    for d, ((qa, kb, kd, vb, decay), keep) in enumerate(chunks):
        for h in range(HG_HEADS):
            sl = slice(h * HG_DK, (h + 1) * HG_DK)
            st = st_ref[d, h]
            if qa is not None:
                first.append((_dot_nt(qa[:, sl], kb[:, sl]), _dot_nt(qa[:, sl], st.astype(BF16))))
            st_ref[d, h] = st * decay[:, sl] + _dot_tn(vb[:, sl], kd[:, sl])
    results = []
    for d, ((qa, kb, kd, vb, decay), keep) in enumerate(chunks):
        if qa is None:
            results.append(None)
            continue
        outs = []
        for h in range(HG_HEADS):
            sl = slice(h * HG_DK, (h + 1) * HG_DK)
            s_qk, o_state = first.pop(0)
            outs.append(_dot(jnp.where(keep, s_qk, 0.0).astype(BF16), vb[:, sl]) + o_state)
        results.append(jnp.concatenate(outs, axis=-1))
    return results


def _hgrn_kernel(qf_ref, ff_ref, if_ref, qb_ref, fb_ref, ib_ref, cff_ref, cfb_ref, ci_ref,
                 lbf_ref, lbb_ref, of_ref, ob_ref, st_ref, *, n_sub, n_ctx_sub):
    n = pl.program_id(1)
    c = HG_CHUNK
    row = lax.broadcasted_iota(jnp.int32, (c, c), 0)
    col = lax.broadcasted_iota(jnp.int32, (c, c), 1)
    keep_f = col <= row
    keep_b = col >= row
    tri_f = keep_f.astype(F32).astype(BF16)
    tri_b = keep_b.astype(F32).astype(BF16)
    lbf = lbf_ref[...]
    lbb = lbb_ref[...]

    @pl.when(n == 0)
    def _():
        st_ref[...] = jnp.zeros_like(st_ref)

        def body(i, carry):
            r0 = pl.multiple_of(i * c, c)
            r1 = pl.multiple_of((n_ctx_sub - 1 - i) * c, c)
            gf = _hgrn_gates(None, cff_ref[pl.ds(r0, c), :], ci_ref[pl.ds(r0, c), :], lbf, tri_f, c - 1)
            gb = _hgrn_gates(None, cfb_ref[pl.ds(r1, c), :], ci_ref[pl.ds(r1, c), :], lbb, tri_b, 0)
            _hgrn_chunks([(gf, keep_f), (gb, keep_b)], st_ref)
            return carry

        lax.fori_loop(0, n_ctx_sub, body, 0)

    @pl.when(n > 0)
    def _():
        def body(i, carry):
            r0 = pl.multiple_of(i * c, c)
            r1 = pl.multiple_of((n_sub - 1 - i) * c, c)
            gf = _hgrn_gates(qf_ref[pl.ds(r0, c), :], ff_ref[pl.ds(r0, c), :], if_ref[pl.ds(r0, c), :],
                             lbf, tri_f, c - 1)
            gb = _hgrn_gates(qb_ref[pl.ds(r1, c), :], fb_ref[pl.ds(r1, c), :], ib_ref[pl.ds(r1, c), :],
                             lbb, tri_b, 0)
            o_f, o_b = _hgrn_chunks([(gf, keep_f), (gb, keep_b)], st_ref)
            of_ref[pl.ds(r0, c), :] = o_f
            ob_ref[pl.ds(r1, c), :] = o_b
            return carry

        lax.fori_loop(0, n_sub, body, 0)


def _hgrn(p, pc, lb_fwd, lb_bwd):
    _, b, s, w = p.shape
    ctx_len = pc.shape[2]
    tb = min(256, s)
    nb = s // tb
    fwd = lambda bi, n: jnp.maximum(n - 1, 0)
    bwd = lambda bi, n: nb - 1 - jnp.maximum(n - 1, 0)

    def sec(section, blk):
        return pl.BlockSpec((None, None, tb, w), lambda bi, n: (section, bi, blk(bi, n), 0))

    def csec(section):
        return pl.BlockSpec((None, None, ctx_len, w), lambda bi, n: (section, bi, 0, 0))

    vec = pl.BlockSpec((1, w), lambda bi, n: (0, 0))
    kern = functools.partial(_hgrn_kernel, n_sub=tb // HG_CHUNK, n_ctx_sub=ctx_len // HG_CHUNK)
    return pl.pallas_call(
        kern,
        grid=(b, nb + 1),
        in_specs=[sec(SEC_Q, fwd), sec(SEC_FF, fwd), sec(SEC_I, fwd),
                  sec(SEC_Q, bwd), sec(SEC_FB, bwd), sec(SEC_I, bwd),
                  csec(SEC_FF), csec(SEC_FB), csec(SEC_I), vec, vec],
        out_specs=[pl.BlockSpec((None, tb, w), lambda bi, n: (bi, fwd(bi, n), 0)),
                   pl.BlockSpec((None, tb, w), lambda bi, n: (bi, bwd(bi, n), 0))],
        out_shape=[jax.ShapeDtypeStruct((b, s, w), F32), jax.ShapeDtypeStruct((b, s, w), F32)],
        scratch_shapes=[pltpu.VMEM((2, HG_HEADS, HG_DK, HG_DK), F32)],
        compiler_params=_cparams(("arbitrary", "arbitrary")),
        name="hgrn",
    )(p, p, p, p, p, p, pc, pc, pc, lb_fwd.reshape(1, w), lb_bwd.reshape(1, w))


NA_ROWS_PER_STEP = 16
NA_PREP_ROWS = 512
NA_KEY_TILE = 128
NA_SPAN = (NA_WIN_R + 2) * GRID_W


def _rope(t, cos, sin_signed, first_half):
    w = t.shape[-1]
    partner = jnp.where(first_half, pltpu.roll(t, w - 16, 1), pltpu.roll(t, 16, 1))
    return t * cos + partner * sin_signed


def _fold_lanes(op, *arrays):
    tiles = [a[:, c:c + 128] for a in arrays for c in range(0, a.shape[-1], 128)]
    acc = tiles[0]
    for t in tiles[1:]:
        acc = op(acc, t)
    return acc


def _rope_tables(rowtab_ref, coltab_ref, row0, n_rows, row_lane):
    out = []
    for i in range(2):
        rt = rowtab_ref[i, pl.ds(row0, n_rows), :]
        by_row = jnp.concatenate([jnp.broadcast_to(rt[r:r + 1, :], (GRID_W, rt.shape[1])) for r in range(n_rows)],
                                 axis=0)
        by_col = jnp.concatenate([coltab_ref[i]] * n_rows, axis=0)
        out.append(jnp.where(row_lane, by_row, by_col))
    return out


def _natten_kernel(q_ref, k_ref, v_ref, kc_ref, vc_ref, rowtab_ref, coltab_ref, t2_ref, o_ref,
                   kt_s, v_s, kc_s, vc_s, bias_s, *, rows):
    rblk = pl.program_id(2)
    hd = NA_HD
    lane = lax.broadcasted_iota(jnp.int32, (1, 2 * hd), 1)
    first_half = (lane % 32) < 16
    row_lane = (lane % hd) < hd // 2
    scale = NA_HD ** -0.5

    def values_and_ones(v_pair, h):
        vh = v_pair if h == 0 else pltpu.roll(v_pair, hd, 1)
        return jnp.where(lane < hd, vh, jnp.where(lane == hd, 1.0, 0.0)).astype(BF16)

    @pl.when(rblk == 0)
    def _():
        kc = kc_ref[...].astype(BF16)
        qi = lax.broadcasted_iota(jnp.int32, (GRID_W, GRID_W), 0)
        ki = lax.broadcasted_iota(jnp.int32, (GRID_W, GRID_W), 1)
        cstart = jnp.clip(qi - NA_WIN_C // 2, 0, GRID_W - NA_WIN_C)
        in_win = (ki >= cstart) & (ki < cstart + NA_WIN_C)
        masked = jnp.full((GRID_W, GRID_W), NEG_INF, F32)
        s_len = k_ref.shape[0]
        for h in range(2):
            sl = slice(h * hd, (h + 1) * hd)
            kc_s[h] = kc[:, sl]
            vc_s[h] = values_and_ones(vc_ref[...], h)
            kt_s[h, s_len // NA_KEY_TILE] = jnp.zeros((hd, NA_KEY_TILE), BF16)
            v_s[h, s_len:s_len + NA_KEY_TILE, :] = jnp.zeros((NA_KEY_TILE, 2 * hd), BF16)
            tiles = [jnp.where(in_win, t2_ref[h, dr], NEG_INF) for dr in range(2 * NA_WIN_R - 1)]
            for bidx in range(NA_WIN_R + 1):
                v, par = (bidx, 0) if bidx < NA_WIN_R else (NA_WIN_R // 2, 1)
                for piece in range(NA_SPAN // GRID_W):
                    j = piece - par
                    tile = tiles[NA_WIN_R - 1 - v + j] if 0 <= j < NA_WIN_R else masked
                    bias_s[h, bidx, :, piece * GRID_W:(piece + 1) * GRID_W] = tile

        def prep(i, carry):
            r0 = pl.multiple_of(i * NA_PREP_ROWS, NA_PREP_ROWS)
            rws = pl.ds(r0, NA_PREP_ROWS)
            cos, sin = _rope_tables(rowtab_ref, coltab_ref, i * (NA_PREP_ROWS // GRID_W), NA_PREP_ROWS // GRID_W,
                                    row_lane)
            kr = _rope(k_ref[rws, :], cos, sin, first_half)
            krt = kr.T.astype(BF16)
            vv = v_ref[rws, :]
            for h in range(2):
                sl = slice(h * hd, (h + 1) * hd)
                for c in range(NA_PREP_ROWS // NA_KEY_TILE):
                    kt_s[h, i * (NA_PREP_ROWS // NA_KEY_TILE) + c] = krt[sl, c * NA_KEY_TILE:(c + 1) * NA_KEY_TILE]
                v_s[h, rws, :] = values_and_ones(vv, h)
            return carry

        lax.fori_loop(0, s_len // NA_PREP_ROWS, prep, 0)

    tq = NA_ROWS_PER_STEP * GRID_W
    q = q_ref[...] * scale
    cos, sin = _rope_tables(rowtab_ref, coltab_ref, rblk * NA_ROWS_PER_STEP, NA_ROWS_PER_STEP, row_lane)
    qr = _rope(q, cos, sin, first_half)
    qb = q.astype(BF16)
    qrb = qr.astype(BF16)
    rws = [slice(rr * GRID_W, (rr + 1) * GRID_W) for rr in range(NA_ROWS_PER_STEP)]
    tile0, bidx = [], []
    for rr in range(NA_ROWS_PER_STEP):
        r = rblk * NA_ROWS_PER_STEP + rr
        rs = jnp.clip(r - NA_WIN_R // 2, 0, rows - NA_WIN_R)
        tile0.append(lax.shift_right_logical(rs, 1))
        bidx.append(jnp.where((rs & 1) == 1, NA_WIN_R, r - rs))

    def scores(h):
        sl = slice(h * hd, (h + 1) * hd)
        qrb_h = qrb[:, sl]
        s_ctx_all = _dot_nt(qb[:, sl], kc_s[h])
        s_win = []
        for rr in range(NA_ROWS_PER_STEP):
            kt = kt_s[h, pl.ds(tile0[rr], NA_SPAN // NA_KEY_TILE)]
            kt = jnp.concatenate([kt[c] for c in range(NA_SPAN // NA_KEY_TILE)], axis=-1)
            s_win.append(_dot(qrb_h[rws[rr]], kt))
        return s_win, s_ctx_all

    def softmax(h, s_win, s_ctx_all):
        e_win, e_ctx = [], []
        for rr in range(NA_ROWS_PER_STEP):
            sw = s_win[rr] + bias_s[h, bidx[rr]]
            sc = s_ctx_all[rws[rr]]
            m = jnp.max(_fold_lanes(jnp.maximum, sw, sc), axis=-1, keepdims=True)
            e_win.append(jnp.exp(sw - m).astype(BF16))
            e_ctx.append(jnp.exp(sc - m).astype(BF16))
        return e_win, e_ctx

    def values(h, e_win, e_ctx):
        o_win = []
        for rr in range(NA_ROWS_PER_STEP):
            k0 = pl.multiple_of(tile0[rr] * NA_KEY_TILE, NA_KEY_TILE)
            o_win.append(_dot(e_win[rr], v_s[h, pl.ds(k0, NA_SPAN), :]))
        o = jnp.concatenate(o_win, axis=0) + _dot(jnp.concatenate(e_ctx, axis=0), vc_s[h])
        return o[:, :hd] * (1.0 / o[:, hd:hd + 1])

    s0 = scores(0)
    s1 = scores(1)
    p0 = softmax(0, *s0)
    o0 = values(0, *p0)
    p1 = softmax(1, *s1)
    o1 = values(1, *p1)
    o_ref[...] = jnp.concatenate([o0, o1], axis=-1)


def _na_tables(rpb, s):
    half = NA_HD // 2
    inv = jnp.power(ROPE_THETA, -jnp.arange(0, half, 2, dtype=F32) / half)

    def tables(n):
        ang = jnp.arange(n, dtype=F32)[:, None] * inv[None, :]
        reps = 2 * NA_HD // half
        return jnp.stack([jnp.tile(jnp.cos(ang), (1, 2 * reps)),
                          jnp.tile(jnp.concatenate([-jnp.sin(ang), jnp.sin(ang)], axis=-1), (1, reps))])

    rowtab, coltab = tables(s // GRID_W), tables(GRID_W)

    pad = GRID_W - NA_WIN_C
    rp = jnp.pad(rpb.astype(F32), ((0, 0), (0, 0), (pad, pad)), mode="edge")
    t2 = jnp.stack([rp[:, :, GRID_W - 1 - qc:2 * GRID_W - 1 - qc] for qc in range(GRID_W)], axis=2)
    return rowtab, coltab, t2


def _natten(p, pc, rowtab, coltab, t2):
    _, b, s, w = p.shape
    ctx_len = pc.shape[2]
    rows = s // GRID_W
    assert rows >= NA_WIN_R and rows % NA_ROWS_PER_STEP == 0
    tq = NA_ROWS_PER_STEP * GRID_W
    hw = 2 * NA_HD
    nhp = w // hw
    kern = functools.partial(_natten_kernel, rows=rows)
    return pl.pallas_call(
        kern,
        grid=(b, nhp, rows // NA_ROWS_PER_STEP),
        in_specs=[pl.BlockSpec((None, None, tq, hw), lambda bi, hp, r: (SEC_NQ, bi, r, hp)),
                  pl.BlockSpec((None, None, s, hw), lambda bi, hp, r: (SEC_NK, bi, 0, hp)),
                  pl.BlockSpec((None, None, s, hw), lambda bi, hp, r: (SEC_NV, bi, 0, hp)),
                  pl.BlockSpec((None, None, ctx_len, hw), lambda bi, hp, r: (SEC_NK, bi, 0, hp)),
                  pl.BlockSpec((None, None, ctx_len, hw), lambda bi, hp, r: (SEC_NV, bi, 0, hp)),
                  pl.BlockSpec((2, rows, hw), lambda bi, hp, r: (0, 0, 0)),
                  pl.BlockSpec((2, GRID_W, hw), lambda bi, hp, r: (0, 0, 0)),
                  pl.BlockSpec((2, 2 * NA_WIN_R - 1, GRID_W, GRID_W), lambda bi, hp, r: (hp, 0, 0, 0))],
        out_specs=pl.BlockSpec((None, tq, hw), lambda bi, hp, r: (bi, r, hp)),
        out_shape=jax.ShapeDtypeStruct((b, s, w), F32),
        scratch_shapes=[pltpu.VMEM((2, s // NA_KEY_TILE + 1, NA_HD, NA_KEY_TILE), BF16),
                        pltpu.VMEM((2, s + NA_KEY_TILE, hw), BF16),
                        pltpu.VMEM((2, ctx_len, NA_HD), BF16), pltpu.VMEM((2, ctx_len, hw), BF16),
                        pltpu.VMEM((2, NA_WIN_R + 1, GRID_W, NA_SPAN), F32)],
        compiler_params=_cparams(("arbitrary", "arbitrary", "arbitrary")),
        name="natten",
    )(p, p, p, pc, pc, rowtab, coltab, t2)


def _route(logits_t, rbias):
    e, t = logits_t.shape
    gsz = e // N_GROUPS
    scores = jax.nn.sigmoid(logits_t)
    sel = scores + rbias
    neg = -jnp.inf
    sub = lax.broadcasted_iota(jnp.int32, (gsz, t), 0).astype(F32)
    gscore = []
    for g in range(N_GROUPS):
        grp = sel[g * gsz:(g + 1) * gsz, :]
        m1 = jnp.max(grp, axis=0, keepdims=True)
        first = jnp.min(jnp.where(grp == m1, sub, float(gsz)), axis=0, keepdims=True)
        m2 = jnp.max(jnp.where(sub == first, neg, grp), axis=0, keepdims=True)
        gscore.append(m1 + m2)
    masked = []
    for g in range(N_GROUPS):
        rank = jnp.zeros((1, t), F32)
        for g2 in range(N_GROUPS):
            if g2 == g:
                continue
            if g2 < g:
                ahead = gscore[g2] >= gscore[g]
            else:
                ahead = gscore[g2] > gscore[g]
            rank = rank + jnp.where(ahead, 1.0, 0.0)
        masked.append(jnp.where(rank < TOPK_GROUPS, sel[g * gsz:(g + 1) * gsz, :], neg))
    work = jnp.concatenate(masked, axis=0)
    eidx = lax.broadcasted_iota(jnp.int32, (e, t), 0).astype(F32)
    idxs, ws = [], []
    chosen = jnp.zeros((e, t), F32)
    for _ in range(TOP_K):
        m = jnp.max(work, axis=0, keepdims=True)
        first = jnp.min(jnp.where(work == m, eidx, float(e)), axis=0, keepdims=True)
        pick = eidx == first
        idxs.append(first)
        ws.append(jnp.sum(jnp.where(pick, scores, 0.0), axis=0, keepdims=True))
        chosen = jnp.where(pick, 1.0, chosen)
        work = jnp.where(pick, neg, work)
    w = jnp.concatenate(ws, axis=0)
    w = w / jnp.sum(w, axis=0, keepdims=True) * ROUTED_SCALE
    return jnp.concatenate(idxs, axis=0).astype(jnp.int32), w, chosen


MERGE_TOK = 512
MERGE_SUB = 256


def _merge_kernel(of_ref, ob_ref, og_ref, yna_ref, ga_ref, gb_ref, x_ref, g1_ref, sh2_ref, sc2_ref,
                  hgg_ref, ln1g_ref, ln1b_ref, wa_ref, wb_ref, wo_ref, wr_ref, rb_ref,
                  x1_ref, h2_ref, topi_ref, topw_ref, cnt_ref, *, alpha):
    tm = x_ref.shape[0]
    subs = [slice(i * MERGE_SUB, (i + 1) * MERGE_SUB) for i in range(tm // MERGE_SUB)]

    def branches(rows):
        o = of_ref[rows, :] + ob_ref[rows, :]
        parts = []
        for h in range(HG_HEADS):
            oh = o[:, h * HG_DK:(h + 1) * HG_DK]
            parts.append(oh * lax.rsqrt(jnp.mean(oh * oh, axis=-1, keepdims=True) + LN_EPS))
        y_hg = jnp.concatenate(parts, axis=-1) * hgg_ref[...] * _silu(og_ref[rows, :])
        return _dot(y_hg.astype(BF16), wa_ref[...]), _dot(yna_ref[rows, :].astype(BF16), wb_ref[...])

    def out_proj(rows, ya, yb):
        t = jax.nn.sigmoid(ga_ref[rows, :]) * ya + jax.nn.sigmoid(gb_ref[rows, :]) * yb
        return _dot(t.astype(BF16), wo_ref[...])

    def norms_router(rows, i, y):
        x1 = _normalize(alpha * x_ref[rows, :] + g1_ref[...] * y) * ln1g_ref[...] + ln1b_ref[...]
        x1_ref[rows, :] = x1
        h2 = _normalize(x1) * (1.0 + sc2_ref[...]) + sh2_ref[...]
        h2_ref[rows, :] = _pack_words(h2)
        hh, hm, hl = _split3(h2)
        wh, wm, wl = _split3(wr_ref[...])
        return (_dot_nt(wh, hh) + _dot_nt(wh, hm) + _dot_nt(wm, hh)
                + _dot_nt(wh, hl) + _dot_nt(wl, hh) + _dot_nt(wm, hm))

    ab = [branches(rows) for rows in subs]
    ys = [out_proj(rows, *ab[i]) for i, rows in enumerate(subs)]
    logits = [norms_router(rows, i, ys[i]) for i, rows in enumerate(subs)]

    @pl.when((pl.program_id(0) == 0) & (pl.program_id(1) == 0))
    def _():
        cnt_ref[...] = jnp.zeros_like(cnt_ref)

    for i, rows in enumerate(subs):
        topi, topw, chosen = _route(logits[i], rb_ref[...])
        topi_ref[:, rows] = topi
        topw_ref[:, rows] = topw
        cnt_ref[...] += jnp.sum(chosen, axis=1, keepdims=True)


def _merge(o_f, o_b, p, y_na, x, g1, sh2, sc2, hg_norm_g, ln1_g, ln1_b, w_a, w_b, w_o, w_router_t, router_bias,
           alpha):
    b, s, d = x.shape
    tm = min(MERGE_TOK, s)
    e = w_router_t.shape[0]
    tok = lambda bi, i: (bi, i, 0)
    blk = pl.BlockSpec((None, tm, d), tok)

    def sec(section):
        return pl.BlockSpec((None, None, tm, d), lambda bi, i: (section, bi, i, 0))

    mod = pl.BlockSpec((None, 1, d), lambda bi, i: (bi, 0, 0))
    vec = pl.BlockSpec((1, d), lambda bi, i: (0, 0))
    mat = pl.BlockSpec((d, d), lambda bi, i: (0, 0), pipeline_mode=pl.Buffered(1))
    return pl.pallas_call(
        functools.partial(_merge_kernel, alpha=alpha),
        grid=(b, s // tm),
        in_specs=[blk, blk, sec(SEC_OG), blk, sec(SEC_GA), sec(SEC_GB), blk, mod, mod, mod,
                  vec, vec, vec, mat, mat, mat,
                  pl.BlockSpec((e, d), lambda bi, i: (0, 0)),
                  pl.BlockSpec((e, 1), lambda bi, i: (0, 0))],
        out_specs=[blk,
                   pl.BlockSpec((tm, d // 2), lambda bi, i: (bi * (s // tm) + i, 0)),
                   pl.BlockSpec((None, TOP_K, tm), lambda bi, i: (bi, 0, i)),
                   pl.BlockSpec((None, TOP_K, tm), lambda bi, i: (bi, 0, i)),
                   pl.BlockSpec((e, 128), lambda bi, i: (0, 0))],
        out_shape=[jax.ShapeDtypeStruct((b, s, d), F32),
                   jax.ShapeDtypeStruct((b * s, d // 2), U32),
                   jax.ShapeDtypeStruct((b, TOP_K, s), jnp.int32), jax.ShapeDtypeStruct((b, TOP_K, s), F32),
                   jax.ShapeDtypeStruct((e, 128), F32)],
        compiler_params=_cparams(("arbitrary", "arbitrary")),
        name="merge",
    )(o_f, o_b, p, y_na, p, p, x, g1, sh2, sc2, hg_norm_g.reshape(1, d), ln1_g.reshape(1, d),
      ln1_b.reshape(1, d), w_a, w_b, w_o, w_router_t, router_bias.reshape(e, 1))


MOE_TILE = 512
MOE_TOK = 512


def _plan_kernel(topi_ref, off_ref, dest_ref, carry_ref):
    @pl.when(pl.program_id(0) == 0)
    def _():
        carry_ref[...] = jnp.zeros_like(carry_ref)

    topi = topi_ref[...]
    tok = topi.shape[1]
    eidx = lax.broadcasted_iota(jnp.int32, (N_EXPERTS, tok), 0)
    hits = [eidx == topi[k:k + 1, :] for k in range(TOP_K)]
    m = jnp.zeros((N_EXPERTS, tok), F32)
    for hit in hits:
        m = jnp.where(hit, 1.0, m)
    before = (lax.broadcasted_iota(jnp.int32, (tok, tok), 0)
              < lax.broadcasted_iota(jnp.int32, (tok, tok), 1)).astype(F32).astype(BF16)
    row = off_ref[...] + carry_ref[...] + _dot(m.astype(BF16), before)
    dest = [jnp.sum(jnp.where(hit, row, 0.0), axis=0, keepdims=True) for hit in hits]
    dest_ref[...] = jnp.concatenate(dest, axis=0).astype(jnp.int32)
    carry_ref[...] += jnp.sum(m, axis=1, keepdims=True)


def _plan(topi, seg_off):
    b, k, s = topi.shape
    per_b = s // MOE_TOK
    blk = pl.BlockSpec((None, k, MOE_TOK), lambda i: (i // per_b, 0, i % per_b))
    return pl.pallas_call(
        _plan_kernel,
        grid=(b * per_b,),
        in_specs=[blk, pl.BlockSpec((N_EXPERTS, 1), lambda i: (0, 0))],
        out_specs=blk,
        out_shape=jax.ShapeDtypeStruct((b, k, s), jnp.int32),
        scratch_shapes=[pltpu.VMEM((N_EXPERTS, 1), F32)],
        compiler_params=_cparams(("arbitrary",)),
        name="plan",
    )(topi, seg_off.astype(F32).reshape(N_EXPERTS, 1))


SC_WINDOW = 128


def _sc_workers():
    info = plsc.get_sparse_core_info()
    return info.num_cores, info.num_cores * info.num_subcores


def _scatter_rows(src, idx, zero_idx, n_rows):
    t, w = src.shape
    m, mz = idx.shape[0], zero_idx.shape[0]
    n_cores, n_workers = _sc_workers()
    per_worker, per_worker_z = m // n_workers, mz // n_workers
    assert m % t == 0 and t % SC_WINDOW == 0
    assert per_worker * n_workers == m and per_worker % SC_WINDOW == 0
    assert per_worker_z * n_workers == mz and per_worker_z % SC_WINDOW == 0
    mesh = plsc.VectorSubcoreMesh(core_axis_name="core", subcore_axis_name="subcore")

    @functools.partial(
        pl.kernel, mesh=mesh, out_type=jax.ShapeDtypeStruct((n_rows, w), src.dtype),
        scratch_types=[pltpu.VMEM((SC_WINDOW,), jnp.int32), pltpu.VMEM((SC_WINDOW, w), src.dtype),
                       pltpu.SemaphoreType.DMA])
    def scatter(src_hbm, idx_hbm, zeros_hbm, zero_idx_hbm, out_hbm, idx_v, rows_v, sem):
        worker = lax.axis_index("subcore") * n_cores + lax.axis_index("core")

        @pl.loop(0, per_worker // SC_WINDOW)
        def _(step):
            base = pl.multiple_of(worker * per_worker + step * SC_WINDOW, SC_WINDOW)
            first = pl.multiple_of(lax.rem(base, t), SC_WINDOW)
            pltpu.sync_copy(idx_hbm.at[pl.ds(base, SC_WINDOW)], idx_v)
            pltpu.sync_copy(src_hbm.at[pl.ds(first, SC_WINDOW)], rows_v)
            pltpu.async_copy(rows_v, out_hbm.at[idx_v], sem).wait()

        pltpu.sync_copy(zeros_hbm, rows_v)

        @pl.loop(0, per_worker_z // SC_WINDOW)
        def _(step):
            base = pl.multiple_of(worker * per_worker_z + step * SC_WINDOW, SC_WINDOW)
            pltpu.sync_copy(zero_idx_hbm.at[pl.ds(base, SC_WINDOW)], idx_v)
            pltpu.async_copy(rows_v, out_hbm.at[idx_v], sem).wait()

    return scatter(src, idx, jnp.zeros((SC_WINDOW, w), src.dtype), zero_idx)


def _experts_kernel(te_ref, tb_ref, nt_ref, xs_ref, wg_ref, wu_ref, wd_ref, ys_ref):
    @pl.when(pl.program_id(0) < nt_ref[0])
    def _():
        x = _unpack_words(xs_ref[...]).astype(BF16)
        act = _silu(_dot(x, wg_ref[...].astype(BF16))) * _dot(x, wu_ref[...].astype(BF16))
        ys_ref[...] = _pack_words(_dot(act.astype(BF16), wd_ref[...].astype(BF16)))


def _experts(xs, tile_expert, tile_block, n_tiles, wg, wu, wd):
    d, f = wg.shape[1], wg.shape[2]
    rows = pl.BlockSpec((MOE_TILE, d // 2), lambda i, te, tb, nt: (tb[i], 0))
    grid_spec = pltpu.PrefetchScalarGridSpec(
        num_scalar_prefetch=3,
        grid=(xs.shape[0] // MOE_TILE,),
        in_specs=[rows,
                  pl.BlockSpec((None, d, f), lambda i, te, tb, nt: (te[i], 0, 0)),
                  pl.BlockSpec((None, d, f), lambda i, te, tb, nt: (te[i], 0, 0)),
                  pl.BlockSpec((None, f, d), lambda i, te, tb, nt: (te[i], 0, 0))],
        out_specs=rows,
    )
    return pl.pallas_call(
        _experts_kernel,
        grid_spec=grid_spec,
        out_shape=jax.ShapeDtypeStruct(xs.shape, U32),
        compiler_params=_cparams(("arbitrary",)),
        name="experts",
    )(tile_expert, tile_block, n_tiles, xs, wg, wu, wd)


def _gather_rows(table, idx):
    m = idx.shape[0]
    w = table.shape[1]
    n_cores, n_workers = _sc_workers()
    per_worker = m // n_workers
    assert per_worker * n_workers == m and per_worker % SC_WINDOW == 0
    mesh = plsc.VectorSubcoreMesh(core_axis_name="core", subcore_axis_name="subcore")

    @functools.partial(
        pl.kernel, mesh=mesh, out_type=jax.ShapeDtypeStruct((m, w), table.dtype),
        scratch_types=[pltpu.VMEM((SC_WINDOW,), jnp.int32), pltpu.VMEM((SC_WINDOW, w), table.dtype),
                       pltpu.SemaphoreType.DMA])
    def gather(table_hbm, idx_hbm, out_hbm, idx_v, rows_v, sem):
        worker = lax.axis_index("subcore") * n_cores + lax.axis_index("core")

        @pl.loop(0, per_worker // SC_WINDOW)
        def _(step):
            base = pl.multiple_of(worker * per_worker + step * SC_WINDOW, SC_WINDOW)
            pltpu.sync_copy(idx_hbm.at[pl.ds(base, SC_WINDOW)], idx_v)
            pltpu.async_copy(table_hbm.at[idx_v], rows_v, sem).wait()
            pltpu.sync_copy(rows_v, out_hbm.at[pl.ds(base, SC_WINDOW)])

    return gather(table, idx)


def _combine_kernel(rows_ref, topw_ref, h_ref, x1_ref, g2_ref, sg_ref, su_ref, sd_ref, ln2g_ref, ln2b_ref,
                    o_ref, *, alpha):
    h = _unpack_words(h_ref[...]).astype(BF16)
    act = _silu(_dot(h, sg_ref[...])) * _dot(h, su_ref[...])
    y = _dot(act.astype(BF16), sd_ref[...])
    w = topw_ref[...].T
    for k in range(TOP_K):
        y = y + w[:, k:k + 1] * _unpack_words(rows_ref[k])
    o_ref[...] = _normalize(alpha * x1_ref[...] + g2_ref[...] * y) * ln2g_ref[...] + ln2b_ref[...]


def _combine(gathered, topw, h2p, x1, g2, sg, su, sd, ln2_g, ln2_b, alpha):
    t, d = x1.shape
    b, k, s = topw.shape
    per_b = s // MOE_TOK
    fs = sg.shape[1]
    rows = pl.BlockSpec((MOE_TOK, d), lambda i: (i, 0))
    packed = pl.BlockSpec((MOE_TOK, d // 2), lambda i: (i, 0))
    vec = pl.BlockSpec((1, d), lambda i: (0, 0))
    return pl.pallas_call(
        functools.partial(_combine_kernel, alpha=alpha),
        grid=(t // MOE_TOK,),
        in_specs=[pl.BlockSpec((k, MOE_TOK, d // 2), lambda i: (0, i, 0)),
                  pl.BlockSpec((None, k, MOE_TOK), lambda i: (i // per_b, 0, i % per_b)),
                  packed, rows,
                  pl.BlockSpec((None, 1, d), lambda i: (i // per_b, 0, 0)),
                  pl.BlockSpec((d, fs), lambda i: (0, 0)),
                  pl.BlockSpec((d, fs), lambda i: (0, 0)),
                  pl.BlockSpec((fs, d), lambda i: (0, 0)),
                  vec, vec],
        out_specs=rows,
        out_shape=jax.ShapeDtypeStruct((t, d), F32),
        compiler_params=_cparams(("arbitrary",)),
        name="combine",
    )(gathered, topw, h2p, x1, g2, sg, su, sd, ln2_g.reshape(1, d), ln2_b.reshape(1, d))


def _moe(h2p, topi, topw, cnt, x1, g2, wg, wu, wd, sg, su, sd, ln2_g, ln2_b, alpha):
    b, s, d = x1.shape
    t = b * s
    cnt = cnt[:, 0].astype(jnp.int32)
    tiles_e = (cnt + (MOE_TILE - 1)) // MOE_TILE
    tiles_cum = jnp.cumsum(tiles_e)
    seg_off = (tiles_cum - tiles_e) * MOE_TILE
    n_tiles_max = t * TOP_K // MOE_TILE + N_EXPERTS
    tile_block = jnp.minimum(jnp.arange(n_tiles_max, dtype=jnp.int32), tiles_cum[-1] - 1)
    tile_expert = jnp.sum((tiles_cum[None, :] <= tile_block[:, None]).astype(jnp.int32), axis=1)
    n_tiles = tiles_cum[-1:].astype(jnp.int32)

    dest = jnp.transpose(_plan(topi, seg_off), (1, 0, 2)).reshape(TOP_K * t)
    j = jnp.arange(MOE_TILE, dtype=jnp.int32)[None, :]
    n_pad = (tiles_e * MOE_TILE - cnt)[:, None]
    spare = (n_tiles_max - 1) * MOE_TILE + j
    zero_idx = jnp.where(j < n_pad, (seg_off + cnt)[:, None] + j, spare).reshape(N_EXPERTS * MOE_TILE)
    xs = _scatter_rows(h2p, dest, zero_idx.astype(jnp.int32), n_tiles_max * MOE_TILE)
    ys = _experts(xs, tile_expert, tile_block, n_tiles, wg, wu, wd)
    gathered = _gather_rows(ys, dest)
    out = _combine(gathered.reshape(TOP_K, t, d // 2), topw, h2p, x1.reshape(t, d), g2, sg, su, sd,
                   ln2_g, ln2_b, alpha)
    return out.reshape(b, s, d)


def kernel(x, c, ctx, c_ctx, w_ada, b_ada, w_in, hg_lb_fwd, hg_lb_bwd, hg_norm_g, na_rpb, w_branch_a, w_branch_b, w_out, ln1_g, ln1_b, w_router, router_bias, w_e_gate, w_e_up, w_e_down, w_sh_gate, w_sh_up, w_sh_down, ln2_g, ln2_b):
    depth = w_ada.shape[0]
    assert depth == 1, "single-layer block"
    b, s, d = x.shape
    alpha = (2.0 * depth) ** 0.25
    l = 0
    lb_fwd = jnp.cumsum(jax.nn.softmax(hg_lb_fwd.astype(F32), axis=0), axis=0)[l]
    lb_bwd = jnp.cumsum(jax.nn.softmax(hg_lb_bwd.astype(F32), axis=0), axis=0)[l]

    cond_rows = jnp.concatenate([c, c_ctx[None, :], jnp.zeros((8 - b - 1, d), F32)], axis=0)
    mod = _ada(cond_rows, w_ada[l], b_ada[l])
    sh1, sc1, g1, sh2, sc2, g2 = [m[:b, None, :] for m in jnp.split(mod, 6, axis=-1)]
    csh1, csc1 = [jnp.broadcast_to(m[b:b + 1, None, :], (b, 1, d)) for m in jnp.split(mod, 6, axis=-1)[:2]]

    w_in_b = w_in[l].astype(BF16)
    p = _inproj(x, sh1, sc1, w_in_b)
    pc = _inproj(ctx, csh1, csc1, w_in_b)

    o_f, o_b = _hgrn(p, pc, lb_fwd, lb_bwd)
    y_na = _natten(p, pc, *_na_tables(na_rpb[l], s))

    x1, h2, topi, topw, cnt = _merge(o_f, o_b, p, y_na, x, g1, sh2, sc2, hg_norm_g[l], ln1_g[l], ln1_b[l],
                                     w_branch_a[l].astype(BF16), w_branch_b[l].astype(BF16),
                                     w_out[l].astype(BF16), w_router[l].T, router_bias[l], alpha)

    return _moe(h2, topi, topw, cnt, x1, g2,
                w_e_gate[l], w_e_up[l], w_e_down[l],
                w_sh_gate[l].astype(BF16), w_sh_up[l].astype(BF16), w_sh_down[l].astype(BF16),
                ln2_g[l], ln2_b[l], alpha)
```

```python
import functools

import numpy as np
import jax
import jax.numpy as jnp
from jax import lax
from jax.experimental import pallas as pl
from jax.experimental.pallas import tpu as pltpu
from jax.experimental.pallas import tpu_sc as plsc

F32 = jnp.float32
BF16 = jnp.bfloat16

D_MODEL = 1024
GRID_W = 64
HG_HEADS = 8
HG_DK = 128
HG_CHUNK = 64
NA_HEADS = 16
NA_HD = 64
NA_WIN_R = 8
NA_WIN_C = 16
ROPE_THETA = 10000.0
NEG_INF = -1e30
N_EXPERTS = 64
EXPERT_DIM = 256
TOP_K = 8
N_GROUPS = 8
TOPK_GROUPS = 4
ROUTED_SCALE = 2.5
LN_EPS = 1e-6
N_SECTIONS = 10
SEC_Q, SEC_FF, SEC_FB, SEC_I, SEC_OG, SEC_NQ, SEC_NK, SEC_NV, SEC_GA, SEC_GB = range(10)

VMEM_LIMIT = 56 * 1024 * 1024


def _cparams(sem):
    return pltpu.CompilerParams(dimension_semantics=sem, vmem_limit_bytes=VMEM_LIMIT)


def _normalize(x):
    mu = jnp.mean(x, axis=-1, keepdims=True)
    xc = x - mu
    var = jnp.mean(xc * xc, axis=-1, keepdims=True)
    return xc * lax.rsqrt(var + LN_EPS)


def _silu(x):
    return x * jax.nn.sigmoid(x)


def _dot(a, b):
    return jnp.dot(a, b, preferred_element_type=F32)


def _dot_nt(a, b):
    return lax.dot_general(a, b, (((1,), (1,)), ((), ())), preferred_element_type=F32)


def _dot_tn(a, b):
    return lax.dot_general(a, b, (((0,), (0,)), ((), ())), preferred_element_type=F32)


U32 = jnp.uint32


def _pack_words(x):
    half = x.shape[1] // 2
    lo = lax.bitcast_convert_type(x[:, :half].astype(BF16).astype(F32), U32) >> 16
    hi = lax.bitcast_convert_type(x[:, half:].astype(BF16).astype(F32), U32) & jnp.uint32(0xFFFF0000)
    return lo | hi


def _unpack_words(w):
    lo = lax.bitcast_convert_type(w << 16, F32)
    hi = lax.bitcast_convert_type(w & jnp.uint32(0xFFFF0000), F32)
    return jnp.concatenate([lo, hi], axis=-1)


def _split3(x):
    hi = x.astype(BF16)
    r1 = x - hi.astype(F32)
    mid = r1.astype(BF16)
    lo = (r1 - mid.astype(F32)).astype(BF16)
    return hi, mid, lo


def _ada_kernel(c_ref, w_ref, b_ref, o_ref):
    cond = _silu(c_ref[...])
    o_ref[...] = _dot(cond.astype(BF16), w_ref[...].astype(BF16)) + b_ref[...]


def _ada(cond_rows, w_ada, b_ada):
    r, d = cond_rows.shape
    n = w_ada.shape[1]
    tn = 1024
    return pl.pallas_call(
        _ada_kernel,
        grid=(n // tn,),
        in_specs=[pl.BlockSpec((r, d), lambda j: (0, 0)),
                  pl.BlockSpec((d, tn), lambda j: (0, j)),
                  pl.BlockSpec((1, tn), lambda j: (0, j))],
        out_specs=pl.BlockSpec((r, tn), lambda j: (0, j)),
        out_shape=jax.ShapeDtypeStruct((r, n), F32),
        compiler_params=_cparams(("arbitrary",)),
        name="ada",
    )(cond_rows, w_ada, b_ada.reshape(1, n))


INPROJ_TOK = 2048


def _inproj_kernel(x_ref, sh_ref, sc_ref, w_ref, o_ref, h_ref):
    @pl.when(pl.program_id(2) == 0)
    def _():
        h = _normalize(x_ref[...]) * (1.0 + sc_ref[...]) + sh_ref[...]
        h_ref[...] = h.astype(BF16)

    o_ref[...] = _dot(h_ref[...], w_ref[...])


def _inproj(x, shift, scale, w_in_bf16):
    b, s, d = x.shape
    tm = min(INPROJ_TOK, s)
    nj = w_in_bf16.shape[1] // d
    return pl.pallas_call(
        _inproj_kernel,
        grid=(b, s // tm, nj),
        in_specs=[pl.BlockSpec((None, tm, d), lambda bi, i, j: (bi, i, 0)),
                  pl.BlockSpec((None, 1, d), lambda bi, i, j: (bi, 0, 0)),
                  pl.BlockSpec((None, 1, d), lambda bi, i, j: (bi, 0, 0)),
                  pl.BlockSpec((d, d), lambda bi, i, j: (0, j))],
        out_specs=pl.BlockSpec((None, None, tm, d), lambda bi, i, j: (j, bi, i, 0)),
        out_shape=jax.ShapeDtypeStruct((nj, b, s, d), F32),
        scratch_shapes=[pltpu.VMEM((tm, d), BF16)],
        compiler_params=_cparams(("arbitrary", "arbitrary", "arbitrary")),
        name="inproj",
    )(x, shift, scale, w_in_bf16)


def _hgrn_gates(q, fraw, v, lb, tri_bf16, last_row):
    f = lb + (1.0 - lb) * jax.nn.sigmoid(fraw)
    k = 1.0 - f
    lf = jnp.log(f)
    hi, mid, lo = _split3(lf)
    a = _dot(tri_bf16, hi) + _dot(tri_bf16, mid) + _dot(tri_bf16, lo)
    a_last = a[last_row:last_row + 1, :]
    kd = (k * jnp.exp(a_last - a)).astype(BF16)
    decay = jnp.exp(a_last)
    qa = kb = None
    if q is not None:
        qa = (_silu(q) * jnp.exp(a)).astype(BF16)
        kb = (k * jnp.exp(-a)).astype(BF16)
    return qa, kb, kd, v.astype(BF16), decay


def _hgrn_chunks(chunks, st_ref):
    first = []
    for d, ((qa, kb, kd, vb, decay), keep) in enumerate(chunks):
        for h in range(HG_HEADS):
            sl = slice(h * HG_DK, (h + 1) * HG_DK)
            st = st_ref[d, h]
            if qa is not None:
                first.append((_dot_nt(qa[:, sl], kb[:, sl]), _dot_nt(qa[:, sl], st.astype(BF16))))
            st_ref[d, h] = st * decay[:, sl] + _dot_tn(vb[:, sl], kd[:, sl])
    results = []
    for d, ((qa, kb, kd, vb, decay), keep) in enumerate(chunks):
        if qa is None:
            results.append(None)
            continue
        outs = []
        for h in range(HG_HEADS):
            sl = slice(h * HG_DK, (h + 1) * HG_DK)
            s_qk, o_state = first.pop(0)
            outs.append(_dot(jnp.where(keep, s_qk, 0.0).astype(BF16), vb[:, sl]) + o_state)
        results.append(jnp.concatenate(outs, axis=-1))
    return results


def _hgrn_kernel(qf_ref, ff_ref, if_ref, qb_ref, fb_ref, ib_ref, cff_ref, cfb_ref, ci_ref,
                 lbf_ref, lbb_ref, of_ref, ob_ref, st_ref, *, n_sub, n_ctx_sub):
    n = pl.program_id(1)
    c = HG_CHUNK
    row = lax.broadcasted_iota(jnp.int32, (c, c), 0)
    col = lax.broadcasted_iota(jnp.int32, (c, c), 1)
    keep_f = col <= row
    keep_b = col >= row
    tri_f = keep_f.astype(F32).astype(BF16)
    tri_b = keep_b.astype(F32).astype(BF16)
    lbf = lbf_ref[...]
    lbb = lbb_ref[...]

    @pl.when(n == 0)
    def _():
        st_ref[...] = jnp.zeros_like(st_ref)

        def body(i, carry):
            r0 = pl.multiple_of(i * c, c)
            r1 = pl.multiple_of((n_ctx_sub - 1 - i) * c, c)
            gf = _hgrn_gates(None, cff_ref[pl.ds(r0, c), :], ci_ref[pl.ds(r0, c), :], lbf, tri_f, c - 1)
            gb = _hgrn_gates(None, cfb_ref[pl.ds(r1, c), :], ci_ref[pl.ds(r1, c), :], lbb, tri_b, 0)
            _hgrn_chunks([(gf, keep_f), (gb, keep_b)], st_ref)
            return carry

        lax.fori_loop(0, n_ctx_sub, body, 0)

    @pl.when(n > 0)
    def _():
        def body(i, carry):
            r0 = pl.multiple_of(i * c, c)
            r1 = pl.multiple_of((n_sub - 1 - i) * c, c)
            gf = _hgrn_gates(qf_ref[pl.ds(r0, c), :], ff_ref[pl.ds(r0, c), :], if_ref[pl.ds(r0, c), :],
                             lbf, tri_f, c - 1)
            gb = _hgrn_gates(qb_ref[pl.ds(r1, c), :], fb_ref[pl.ds(r1, c), :], ib_ref[pl.ds(r1, c), :],
                             lbb, tri_b, 0)
            o_f, o_b = _hgrn_chunks([(gf, keep_f), (gb, keep_b)], st_ref)
            of_ref[pl.ds(r0, c), :] = o_f
            ob_ref[pl.ds(r1, c), :] = o_b
            return carry

        lax.fori_loop(0, n_sub, body, 0, unroll=2)


def _hgrn(p, pc, lb_fwd, lb_bwd):
    _, b, s, w = p.shape
    ctx_len = pc.shape[2]
    tb = min(256, s)
    nb = s // tb
    fwd = lambda bi, n: jnp.maximum(n - 1, 0)
    bwd = lambda bi, n: nb - 1 - jnp.maximum(n - 1, 0)

    def sec(section, blk):
        return pl.BlockSpec((None, None, tb, w), lambda bi, n: (section, bi, blk(bi, n), 0))

    def csec(section):
        return pl.BlockSpec((None, None, ctx_len, w), lambda bi, n: (section, bi, 0, 0))

    vec = pl.BlockSpec((1, w), lambda bi, n: (0, 0))
    kern = functools.partial(_hgrn_kernel, n_sub=tb // HG_CHUNK, n_ctx_sub=ctx_len // HG_CHUNK)
    return pl.pallas_call(
        kern,
        grid=(b, nb + 1),
        in_specs=[sec(SEC_Q, fwd), sec(SEC_FF, fwd), sec(SEC_I, fwd),
                  sec(SEC_Q, bwd), sec(SEC_FB, bwd), sec(SEC_I, bwd),
                  csec(SEC_FF), csec(SEC_FB), csec(SEC_I), vec, vec],
        out_specs=[pl.BlockSpec((None, tb, w), lambda bi, n: (bi, fwd(bi, n), 0)),
                   pl.BlockSpec((None, tb, w), lambda bi, n: (bi, bwd(bi, n), 0))],
        out_shape=[jax.ShapeDtypeStruct((b, s, w), F32), jax.ShapeDtypeStruct((b, s, w), F32)],
        scratch_shapes=[pltpu.VMEM((2, HG_HEADS, HG_DK, HG_DK), F32)],
        compiler_params=_cparams(("arbitrary", "arbitrary")),
        name="hgrn",
    )(p, p, p, p, p, p, pc, pc, pc, lb_fwd.reshape(1, w), lb_bwd.reshape(1, w))


NA_ROWS_PER_STEP = 32
NA_PREP_ROWS = 512
NA_KEY_TILE = 128
NA_SPAN = (NA_WIN_R + 2) * GRID_W


def _rope(t, cos, sin_signed, first_half):
    w = t.shape[-1]
    partner = jnp.where(first_half, pltpu.roll(t, w - 16, 1), pltpu.roll(t, 16, 1))
    return t * cos + partner * sin_signed


def _fold_lanes(op, *arrays):
    tiles = [a[:, c:c + 128] for a in arrays for c in range(0, a.shape[-1], 128)]
    acc = tiles[0]
    for t in tiles[1:]:
        acc = op(acc, t)
    return acc


def _rope_tables(rowtab_ref, coltab_ref, row0, n_rows, row_lane):
    out = []
    for i in range(2):
        rt = rowtab_ref[i, pl.ds(row0, n_rows), :]
        by_row = jnp.concatenate([jnp.broadcast_to(rt[r:r + 1, :], (GRID_W, rt.shape[1])) for r in range(n_rows)],
                                 axis=0)
        by_col = jnp.concatenate([coltab_ref[i]] * n_rows, axis=0)
        out.append(jnp.where(row_lane, by_row, by_col))
    return out


def _natten_kernel(q_ref, k_ref, v_ref, kc_ref, vc_ref, rowtab_ref, coltab_ref, t2_ref, o_ref,
                   kt_s, v_s, kc_s, vc_s, bias_s, *, rows):
    rblk = pl.program_id(2)
    hd = NA_HD
    lane = lax.broadcasted_iota(jnp.int32, (1, 2 * hd), 1)
    first_half = (lane % 32) < 16
    row_lane = (lane % hd) < hd // 2
    scale = NA_HD ** -0.5

    def values_and_ones(v_pair, h):
        vh = v_pair if h == 0 else pltpu.roll(v_pair, hd, 1)
        return jnp.where(lane < hd, vh, jnp.where(lane == hd, 1.0, 0.0)).astype(BF16)

    @pl.when(rblk == 0)
    def _():
        kc = kc_ref[...].astype(BF16)
        qi = lax.broadcasted_iota(jnp.int32, (GRID_W, GRID_W), 0)
        ki = lax.broadcasted_iota(jnp.int32, (GRID_W, GRID_W), 1)
        cstart = jnp.clip(qi - NA_WIN_C // 2, 0, GRID_W - NA_WIN_C)
        in_win = (ki >= cstart) & (ki < cstart + NA_WIN_C)
        masked = jnp.full((GRID_W, GRID_W), NEG_INF, F32)
        s_len = k_ref.shape[0]
        for h in range(2):
            sl = slice(h * hd, (h + 1) * hd)
            kc_s[h] = kc[:, sl]
            vc_s[h] = values_and_ones(vc_ref[...], h)
            kt_s[h, s_len // NA_KEY_TILE] = jnp.zeros((hd, NA_KEY_TILE), BF16)
            v_s[h, s_len:s_len + NA_KEY_TILE, :] = jnp.zeros((NA_KEY_TILE, 2 * hd), BF16)
            tiles = [jnp.where(in_win, t2_ref[h, dr], NEG_INF) for dr in range(2 * NA_WIN_R - 1)]
            for bidx in range(NA_WIN_R + 1):
                v, par = (bidx, 0) if bidx < NA_WIN_R else (NA_WIN_R // 2, 1)
                for piece in range(NA_SPAN // GRID_W):
                    j = piece - par
                    tile = tiles[NA_WIN_R - 1 - v + j] if 0 <= j < NA_WIN_R else masked
                    bias_s[h, bidx, :, piece * GRID_W:(piece + 1) * GRID_W] = tile

        def prep(i, carry):
            r0 = pl.multiple_of(i * NA_PREP_ROWS, NA_PREP_ROWS)
            rws = pl.ds(r0, NA_PREP_ROWS)
            cos, sin = _rope_tables(rowtab_ref, coltab_ref, i * (NA_PREP_ROWS // GRID_W), NA_PREP_ROWS // GRID_W,
                                    row_lane)
            kr = _rope(k_ref[rws, :], cos, sin, first_half)
            krt = kr.T.astype(BF16)
            vv = v_ref[rws, :]
            for h in range(2):
                sl = slice(h * hd, (h + 1) * hd)
                for c in range(NA_PREP_ROWS // NA_KEY_TILE):
                    kt_s[h, i * (NA_PREP_ROWS // NA_KEY_TILE) + c] = krt[sl, c * NA_KEY_TILE:(c + 1) * NA_KEY_TILE]
                v_s[h, rws, :] = values_and_ones(vv, h)
            return carry

        lax.fori_loop(0, s_len // NA_PREP_ROWS, prep, 0)

    tq = NA_ROWS_PER_STEP * GRID_W
    q = q_ref[...] * scale
    cos, sin = _rope_tables(rowtab_ref, coltab_ref, rblk * NA_ROWS_PER_STEP, NA_ROWS_PER_STEP, row_lane)
    qr = _rope(q, cos, sin, first_half)
    qb = q.astype(BF16)
    qrb = qr.astype(BF16)
    rws = [slice(rr * GRID_W, (rr + 1) * GRID_W) for rr in range(NA_ROWS_PER_STEP)]
    tile0, bidx = [], []
    for rr in range(NA_ROWS_PER_STEP):
        r = rblk * NA_ROWS_PER_STEP + rr
        rs = jnp.clip(r - NA_WIN_R // 2, 0, rows - NA_WIN_R)
        tile0.append(lax.shift_right_logical(rs, 1))
        bidx.append(jnp.where((rs & 1) == 1, NA_WIN_R, r - rs))

    def scores(h):
        sl = slice(h * hd, (h + 1) * hd)
        qrb_h = qrb[:, sl]
        s_ctx_all = _dot_nt(qb[:, sl], kc_s[h])
        s_win = []
        for rr in range(NA_ROWS_PER_STEP):
            kt = kt_s[h, pl.ds(tile0[rr], NA_SPAN // NA_KEY_TILE)]
            kt = jnp.concatenate([kt[c] for c in range(NA_SPAN // NA_KEY_TILE)], axis=-1)
            s_win.append(_dot(qrb_h[rws[rr]], kt))
        return s_win, s_ctx_all

    def softmax(h, s_win, s_ctx_all):
        e_win, e_ctx = [], []
        for rr in range(NA_ROWS_PER_STEP):
            sw = s_win[rr] + bias_s[h, bidx[rr]]
            sc = s_ctx_all[rws[rr]]
            m = jnp.max(_fold_lanes(jnp.maximum, sw, sc), axis=-1, keepdims=True)
            e_win.append(jnp.exp(sw - m).astype(BF16))
            e_ctx.append(jnp.exp(sc - m).astype(BF16))
        return e_win, e_ctx

    def values(h, e_win, e_ctx):
        o_win = []
        for rr in range(NA_ROWS_PER_STEP):
            k0 = pl.multiple_of(tile0[rr] * NA_KEY_TILE, NA_KEY_TILE)
            o_win.append(_dot(e_win[rr], v_s[h, pl.ds(k0, NA_SPAN), :]))
        o = jnp.concatenate(o_win, axis=0) + _dot(jnp.concatenate(e_ctx, axis=0), vc_s[h])
        return o[:, :hd] * (1.0 / o[:, hd:hd + 1])

    s0 = scores(0)
    s1 = scores(1)
    p0 = softmax(0, *s0)
    o0 = values(0, *p0)
    p1 = softmax(1, *s1)
    o1 = values(1, *p1)
    o_ref[...] = jnp.concatenate([o0, o1], axis=-1)


def _na_tables(rpb, s):
    half = NA_HD // 2
    inv = jnp.power(ROPE_THETA, -jnp.arange(0, half, 2, dtype=F32) / half)

    def tables(n):
        ang = jnp.arange(n, dtype=F32)[:, None] * inv[None, :]
        reps = 2 * NA_HD // half
        return jnp.stack([jnp.tile(jnp.cos(ang), (1, 2 * reps)),
                          jnp.tile(jnp.concatenate([-jnp.sin(ang), jnp.sin(ang)], axis=-1), (1, reps))])

    rowtab, coltab = tables(s // GRID_W), tables(GRID_W)

    pad = GRID_W - NA_WIN_C
    rp = jnp.pad(rpb.astype(F32), ((0, 0), (0, 0), (pad, pad)), mode="edge")
    t2 = jnp.stack([rp[:, :, GRID_W - 1 - qc:2 * GRID_W - 1 - qc] for qc in range(GRID_W)], axis=2)
    return rowtab, coltab, t2


def _natten(p, pc, rowtab, coltab, t2):
    _, b, s, w = p.shape
    ctx_len = pc.shape[2]
    rows = s // GRID_W
    assert rows >= NA_WIN_R and rows % NA_ROWS_PER_STEP == 0
    tq = NA_ROWS_PER_STEP * GRID_W
    hw = 2 * NA_HD
    nhp = w // hw
    kern = functools.partial(_natten_kernel, rows=rows)
    return pl.pallas_call(
        kern,
        grid=(b, nhp, rows // NA_ROWS_PER_STEP),
        in_specs=[pl.BlockSpec((None, None, tq, hw), lambda bi, hp, r: (SEC_NQ, bi, r, hp)),
                  pl.BlockSpec((None, None, s, hw), lambda bi, hp, r: (SEC_NK, bi, 0, hp)),
                  pl.BlockSpec((None, None, s, hw), lambda bi, hp, r: (SEC_NV, bi, 0, hp)),
                  pl.BlockSpec((None, None, ctx_len, hw), lambda bi, hp, r: (SEC_NK, bi, 0, hp)),
                  pl.BlockSpec((None, None, ctx_len, hw), lambda bi, hp, r: (SEC_NV, bi, 0, hp)),
                  pl.BlockSpec((2, rows, hw), lambda bi, hp, r: (0, 0, 0)),
                  pl.BlockSpec((2, GRID_W, hw), lambda bi, hp, r: (0, 0, 0)),
                  pl.BlockSpec((2, 2 * NA_WIN_R - 1, GRID_W, GRID_W), lambda bi, hp, r: (hp, 0, 0, 0))],
        out_specs=pl.BlockSpec((None, tq, hw), lambda bi, hp, r: (bi, r, hp)),
        out_shape=jax.ShapeDtypeStruct((b, s, w), F32),
        scratch_shapes=[pltpu.VMEM((2, s // NA_KEY_TILE + 1, NA_HD, NA_KEY_TILE), BF16),
                        pltpu.VMEM((2, s + NA_KEY_TILE, hw), BF16),
                        pltpu.VMEM((2, ctx_len, NA_HD), BF16), pltpu.VMEM((2, ctx_len, hw), BF16),
                        pltpu.VMEM((2, NA_WIN_R + 1, GRID_W, NA_SPAN), F32)],
        compiler_params=_cparams(("arbitrary", "arbitrary", "arbitrary")),
        name="natten",
    )(p, p, p, pc, pc, rowtab, coltab, t2)


def _route(logits_t, rbias):
    e, t = logits_t.shape
    gsz = e // N_GROUPS
    scores = jax.nn.sigmoid(logits_t)
    sel = scores + rbias
    neg = -jnp.inf
    sub = lax.broadcasted_iota(jnp.int32, (gsz, t), 0).astype(F32)
    gscore = []
    for g in range(N_GROUPS):
        grp = sel[g * gsz:(g + 1) * gsz, :]
        m1 = jnp.max(grp, axis=0, keepdims=True)
        first = jnp.min(jnp.where(grp == m1, sub, float(gsz)), axis=0, keepdims=True)
        m2 = jnp.max(jnp.where(sub == first, neg, grp), axis=0, keepdims=True)
        gscore.append(m1 + m2)
    masked = []
    for g in range(N_GROUPS):
        rank = jnp.zeros((1, t), F32)
        for g2 in range(N_GROUPS):
            if g2 == g:
                continue
            if g2 < g:
                ahead = gscore[g2] >= gscore[g]
            else:
                ahead = gscore[g2] > gscore[g]
            rank = rank + jnp.where(ahead, 1.0, 0.0)
        masked.append(jnp.where(rank < TOPK_GROUPS, sel[g * gsz:(g + 1) * gsz, :], neg))
    work = jnp.concatenate(masked, axis=0)
    eidx = lax.broadcasted_iota(jnp.int32, (e, t), 0).astype(F32)
    idxs, ws = [], []
    chosen = jnp.zeros((e, t), F32)
    for _ in range(TOP_K):
        m = jnp.max(work, axis=0, keepdims=True)
        first = jnp.min(jnp.where(work == m, eidx, float(e)), axis=0, keepdims=True)
        pick = eidx == first
        idxs.append(first)
        ws.append(jnp.sum(jnp.where(pick, scores, 0.0), axis=0, keepdims=True))
        chosen = jnp.where(pick, 1.0, chosen)
        work = jnp.where(pick, neg, work)
    w = jnp.concatenate(ws, axis=0)
    w = w / jnp.sum(w, axis=0, keepdims=True) * ROUTED_SCALE
    return jnp.concatenate(idxs, axis=0).astype(jnp.int32), w, chosen


MERGE_TOK = 512
MERGE_SUB = 256


def _merge_kernel(of_ref, ob_ref, og_ref, yna_ref, ga_ref, gb_ref, x_ref, g1_ref, sh2_ref, sc2_ref,
                  hgg_ref, ln1g_ref, ln1b_ref, wa_ref, wb_ref, wo_ref, wr_ref, rb_ref,
                  x1_ref, h2_ref, topi_ref, topw_ref, cnt_ref, *, alpha):
    tm = x_ref.shape[0]
    subs = [slice(i * MERGE_SUB, (i + 1) * MERGE_SUB) for i in range(tm // MERGE_SUB)]

    def branches(rows):
        o = of_ref[rows, :] + ob_ref[rows, :]
        parts = []
        for h in range(HG_HEADS):
            oh = o[:, h * HG_DK:(h + 1) * HG_DK]
            parts.append(oh * lax.rsqrt(jnp.mean(oh * oh, axis=-1, keepdims=True) + LN_EPS))
        y_hg = jnp.concatenate(parts, axis=-1) * hgg_ref[...] * _silu(og_ref[rows, :])
        return _dot(y_hg.astype(BF16), wa_ref[...]), _dot(yna_ref[rows, :].astype(BF16), wb_ref[...])

    def out_proj(rows, ya, yb):
        t = jax.nn.sigmoid(ga_ref[rows, :]) * ya + jax.nn.sigmoid(gb_ref[rows, :]) * yb
        return _dot(t.astype(BF16), wo_ref[...])

    def norms_router(rows, i, y):
        x1 = _normalize(alpha * x_ref[rows, :] + g1_ref[...] * y) * ln1g_ref[...] + ln1b_ref[...]
        x1_ref[rows, :] = x1
        h2 = _normalize(x1) * (1.0 + sc2_ref[...]) + sh2_ref[...]
        h2_ref[rows, :] = _pack_words(h2)
        hh, hm, hl = _split3(h2)
        wh, wm, wl = _split3(wr_ref[...])
        return (_dot_nt(wh, hh) + _dot_nt(wh, hm) + _dot_nt(wm, hh)
                + _dot_nt(wh, hl) + _dot_nt(wl, hh) + _dot_nt(wm, hm))

    ab = [branches(rows) for rows in subs]
    ys = [out_proj(rows, *ab[i]) for i, rows in enumerate(subs)]
    logits = [norms_router(rows, i, ys[i]) for i, rows in enumerate(subs)]

    @pl.when((pl.program_id(0) == 0) & (pl.program_id(1) == 0))
    def _():
        cnt_ref[...] = jnp.zeros_like(cnt_ref)

    for i, rows in enumerate(subs):
        topi, topw, chosen = _route(logits[i], rb_ref[...])
        topi_ref[:, rows] = topi
        topw_ref[:, rows] = topw
        cnt_ref[...] += jnp.sum(chosen, axis=1, keepdims=True)


def _merge(o_f, o_b, p, y_na, x, g1, sh2, sc2, hg_norm_g, ln1_g, ln1_b, w_a, w_b, w_o, w_router_t, router_bias,
           alpha):
    b, s, d = x.shape
    tm = min(MERGE_TOK, s)
    e = w_router_t.shape[0]
    tok = lambda bi, i: (bi, i, 0)
    blk = pl.BlockSpec((None, tm, d), tok)

    def sec(section):
        return pl.BlockSpec((None, None, tm, d), lambda bi, i: (section, bi, i, 0))

    mod = pl.BlockSpec((None, 1, d), lambda bi, i: (bi, 0, 0))
    vec = pl.BlockSpec((1, d), lambda bi, i: (0, 0))
    mat = pl.BlockSpec((d, d), lambda bi, i: (0, 0), pipeline_mode=pl.Buffered(1))
    return pl.pallas_call(
        functools.partial(_merge_kernel, alpha=alpha),
        grid=(b, s // tm),
        in_specs=[blk, blk, sec(SEC_OG), blk, sec(SEC_GA), sec(SEC_GB), blk, mod, mod, mod,
                  vec, vec, vec, mat, mat, mat,
                  pl.BlockSpec((e, d), lambda bi, i: (0, 0)),
                  pl.BlockSpec((e, 1), lambda bi, i: (0, 0))],
        out_specs=[blk,
                   pl.BlockSpec((tm, d // 2), lambda bi, i: (bi * (s // tm) + i, 0)),
                   pl.BlockSpec((None, TOP_K, tm), lambda bi, i: (bi, 0, i)),
                   pl.BlockSpec((None, TOP_K, tm), lambda bi, i: (bi, 0, i)),
                   pl.BlockSpec((e, 128), lambda bi, i: (0, 0))],
        out_shape=[jax.ShapeDtypeStruct((b, s, d), F32),
                   jax.ShapeDtypeStruct((b * s, d // 2), U32),
                   jax.ShapeDtypeStruct((b, TOP_K, s), jnp.int32), jax.ShapeDtypeStruct((b, TOP_K, s), F32),
                   jax.ShapeDtypeStruct((e, 128), F32)],
        compiler_params=_cparams(("arbitrary", "arbitrary")),
        name="merge",
    )(o_f, o_b, p, y_na, p, p, x, g1, sh2, sc2, hg_norm_g.reshape(1, d), ln1_g.reshape(1, d),
      ln1_b.reshape(1, d), w_a, w_b, w_o, w_router_t, router_bias.reshape(e, 1))


MOE_TILE = 512
MOE_TOK = 512


def _plan_kernel(topi_ref, off_ref, dest_ref, carry_ref):
    @pl.when(pl.program_id(0) == 0)
    def _():
        carry_ref[...] = jnp.zeros_like(carry_ref)

    topi = topi_ref[...]
    tok = topi.shape[1]
    eidx = lax.broadcasted_iota(jnp.int32, (N_EXPERTS, tok), 0)
    hits = [eidx == topi[k:k + 1, :] for k in range(TOP_K)]
    m = jnp.zeros((N_EXPERTS, tok), F32)
    for hit in hits:
        m = jnp.where(hit, 1.0, m)
    before = (lax.broadcasted_iota(jnp.int32, (tok, tok), 0)
              < lax.broadcasted_iota(jnp.int32, (tok, tok), 1)).astype(F32).astype(BF16)
    row = off_ref[...] + carry_ref[...] + _dot(m.astype(BF16), before)
    dest = [jnp.sum(jnp.where(hit, row, 0.0), axis=0, keepdims=True) for hit in hits]
    dest_ref[...] = jnp.concatenate(dest, axis=0).astype(jnp.int32)
    carry_ref[...] += jnp.sum(m, axis=1, keepdims=True)


def _plan(topi, seg_off):
    b, k, s = topi.shape
    per_b = s // MOE_TOK
    blk = pl.BlockSpec((None, k, MOE_TOK), lambda i: (i // per_b, 0, i % per_b))
    return pl.pallas_call(
        _plan_kernel,
        grid=(b * per_b,),
        in_specs=[blk, pl.BlockSpec((N_EXPERTS, 1), lambda i: (0, 0))],
        out_specs=blk,
        out_shape=jax.ShapeDtypeStruct((b, k, s), jnp.int32),
        scratch_shapes=[pltpu.VMEM((N_EXPERTS, 1), F32)],
        compiler_params=_cparams(("arbitrary",)),
        name="plan",
    )(topi, seg_off.astype(F32).reshape(N_EXPERTS, 1))


SC_WINDOW = 128


def _sc_workers():
    info = plsc.get_sparse_core_info()
    return info.num_cores, info.num_cores * info.num_subcores


def _scatter_rows(src, idx, zero_idx, n_rows):
    t, w = src.shape
    m, mz = idx.shape[0], zero_idx.shape[0]
    n_cores, n_workers = _sc_workers()
    per_worker, per_worker_z = m // n_workers, mz // n_workers
    assert m % t == 0 and t % SC_WINDOW == 0
    assert per_worker * n_workers == m and per_worker % SC_WINDOW == 0
    assert per_worker_z * n_workers == mz and per_worker_z % SC_WINDOW == 0
    mesh = plsc.VectorSubcoreMesh(core_axis_name="core", subcore_axis_name="subcore")

    @functools.partial(
        pl.kernel, mesh=mesh, out_type=jax.ShapeDtypeStruct((n_rows, w), src.dtype),
        scratch_types=[pltpu.VMEM((SC_WINDOW,), jnp.int32), pltpu.VMEM((SC_WINDOW, w), src.dtype),
                       pltpu.SemaphoreType.DMA])
    def scatter(src_hbm, idx_hbm, zeros_hbm, zero_idx_hbm, out_hbm, idx_v, rows_v, sem):
        worker = lax.axis_index("subcore") * n_cores + lax.axis_index("core")

        @pl.loop(0, per_worker // SC_WINDOW)
        def _(step):
            base = pl.multiple_of(worker * per_worker + step * SC_WINDOW, SC_WINDOW)
            first = pl.multiple_of(lax.rem(base, t), SC_WINDOW)
            pltpu.sync_copy(idx_hbm.at[pl.ds(base, SC_WINDOW)], idx_v)
            pltpu.sync_copy(src_hbm.at[pl.ds(first, SC_WINDOW)], rows_v)
            pltpu.async_copy(rows_v, out_hbm.at[idx_v], sem).wait()

        pltpu.sync_copy(zeros_hbm, rows_v)

        @pl.loop(0, per_worker_z // SC_WINDOW)
        def _(step):
            base = pl.multiple_of(worker * per_worker_z + step * SC_WINDOW, SC_WINDOW)
            pltpu.sync_copy(zero_idx_hbm.at[pl.ds(base, SC_WINDOW)], idx_v)
            pltpu.async_copy(rows_v, out_hbm.at[idx_v], sem).wait()

    return scatter(src, idx, jnp.zeros((SC_WINDOW, w), src.dtype), zero_idx)


def _experts_kernel(te_ref, tb_ref, nt_ref, xs_ref, wg_ref, wu_ref, wd_ref, ys_ref):
    @pl.when(pl.program_id(0) < nt_ref[0])
    def _():
        x = _unpack_words(xs_ref[...]).astype(BF16)
        act = _silu(_dot(x, wg_ref[...].astype(BF16))) * _dot(x, wu_ref[...].astype(BF16))
        ys_ref[...] = _pack_words(_dot(act.astype(BF16), wd_ref[...].astype(BF16)))


def _experts(xs, tile_expert, tile_block, n_tiles, wg, wu, wd):
    d, f = wg.shape[1], wg.shape[2]
    rows = pl.BlockSpec((MOE_TILE, d // 2), lambda i, te, tb, nt: (tb[i], 0))
    grid_spec = pltpu.PrefetchScalarGridSpec(
        num_scalar_prefetch=3,
        grid=(xs.shape[0] // MOE_TILE,),
        in_specs=[rows,
                  pl.BlockSpec((None, d, f), lambda i, te, tb, nt: (te[i], 0, 0)),
                  pl.BlockSpec((None, d, f), lambda i, te, tb, nt: (te[i], 0, 0)),
                  pl.BlockSpec((None, f, d), lambda i, te, tb, nt: (te[i], 0, 0))],
        out_specs=rows,
    )
    return pl.pallas_call(
        _experts_kernel,
        grid_spec=grid_spec,
        out_shape=jax.ShapeDtypeStruct(xs.shape, U32),
        compiler_params=_cparams(("arbitrary",)),
        name="experts",
    )(tile_expert, tile_block, n_tiles, xs, wg, wu, wd)


def _gather_rows(table, idx):
    m = idx.shape[0]
    w = table.shape[1]
    n_cores, n_workers = _sc_workers()
    per_worker = m // n_workers
    assert per_worker * n_workers == m and per_worker % SC_WINDOW == 0
    mesh = plsc.VectorSubcoreMesh(core_axis_name="core", subcore_axis_name="subcore")

    @functools.partial(
        pl.kernel, mesh=mesh, out_type=jax.ShapeDtypeStruct((m, w), table.dtype),
        scratch_types=[pltpu.VMEM((SC_WINDOW,), jnp.int32), pltpu.VMEM((SC_WINDOW, w), table.dtype),
                       pltpu.SemaphoreType.DMA])
    def gather(table_hbm, idx_hbm, out_hbm, idx_v, rows_v, sem):
        worker = lax.axis_index("subcore") * n_cores + lax.axis_index("core")

        @pl.loop(0, per_worker // SC_WINDOW)
        def _(step):
            base = pl.multiple_of(worker * per_worker + step * SC_WINDOW, SC_WINDOW)
            pltpu.sync_copy(idx_hbm.at[pl.ds(base, SC_WINDOW)], idx_v)
            pltpu.async_copy(table_hbm.at[idx_v], rows_v, sem).wait()
            pltpu.sync_copy(rows_v, out_hbm.at[pl.ds(base, SC_WINDOW)])

    return gather(table, idx)


def _combine_kernel(rows_ref, topw_ref, h_ref, x1_ref, g2_ref, sg_ref, su_ref, sd_ref, ln2g_ref, ln2b_ref,
                    o_ref, *, alpha):
    h = _unpack_words(h_ref[...]).astype(BF16)
    act = _silu(_dot(h, sg_ref[...])) * _dot(h, su_ref[...])
    y = _dot(act.astype(BF16), sd_ref[...])
    w = topw_ref[...].T
    for k in range(TOP_K):
        y = y + w[:, k:k + 1] * _unpack_words(rows_ref[k])
    o_ref[...] = _normalize(alpha * x1_ref[...] + g2_ref[...] * y) * ln2g_ref[...] + ln2b_ref[...]


def _combine(gathered, topw, h2p, x1, g2, sg, su, sd, ln2_g, ln2_b, alpha):
    t, d = x1.shape
    b, k, s = topw.shape
    per_b = s // MOE_TOK
    fs = sg.shape[1]
    rows = pl.BlockSpec((MOE_TOK, d), lambda i: (i, 0))
    packed = pl.BlockSpec((MOE_TOK, d // 2), lambda i: (i, 0))
    vec = pl.BlockSpec((1, d), lambda i: (0, 0))
    return pl.pallas_call(
        functools.partial(_combine_kernel, alpha=alpha),
        grid=(t // MOE_TOK,),
        in_specs=[pl.BlockSpec((k, MOE_TOK, d // 2), lambda i: (0, i, 0)),
                  pl.BlockSpec((None, k, MOE_TOK), lambda i: (i // per_b, 0, i % per_b)),
                  packed, rows,
                  pl.BlockSpec((None, 1, d), lambda i: (i // per_b, 0, 0)),
                  pl.BlockSpec((d, fs), lambda i: (0, 0)),
                  pl.BlockSpec((d, fs), lambda i: (0, 0)),
                  pl.BlockSpec((fs, d), lambda i: (0, 0)),
                  vec, vec],
        out_specs=rows,
        out_shape=jax.ShapeDtypeStruct((t, d), F32),
        compiler_params=_cparams(("arbitrary",)),
        name="combine",
    )(gathered, topw, h2p, x1, g2, sg, su, sd, ln2_g.reshape(1, d), ln2_b.reshape(1, d))


def _moe(h2p, topi, topw, cnt, x1, g2, wg, wu, wd, sg, su, sd, ln2_g, ln2_b, alpha):
    b, s, d = x1.shape
    t = b * s
    cnt = cnt[:, 0].astype(jnp.int32)
    tiles_e = (cnt + (MOE_TILE - 1)) // MOE_TILE
    tiles_cum = jnp.cumsum(tiles_e)
    seg_off = (tiles_cum - tiles_e) * MOE_TILE
    n_tiles_max = t * TOP_K // MOE_TILE + N_EXPERTS
    tile_block = jnp.minimum(jnp.arange(n_tiles_max, dtype=jnp.int32), tiles_cum[-1] - 1)
    tile_expert = jnp.sum((tiles_cum[None, :] <= tile_block[:, None]).astype(jnp.int32), axis=1)
    n_tiles = tiles_cum[-1:].astype(jnp.int32)

    dest = jnp.transpose(_plan(topi, seg_off), (1, 0, 2)).reshape(TOP_K * t)
    j = jnp.arange(MOE_TILE, dtype=jnp.int32)[None, :]
    n_pad = (tiles_e * MOE_TILE - cnt)[:, None]
    spare = (n_tiles_max - 1) * MOE_TILE + j
    zero_idx = jnp.where(j < n_pad, (seg_off + cnt)[:, None] + j, spare).reshape(N_EXPERTS * MOE_TILE)
    xs = _scatter_rows(h2p, dest, zero_idx.astype(jnp.int32), n_tiles_max * MOE_TILE)
    ys = _experts(xs, tile_expert, tile_block, n_tiles, wg, wu, wd)
    gathered = _gather_rows(ys, dest)
    out = _combine(gathered.reshape(TOP_K, t, d // 2), topw, h2p, x1.reshape(t, d), g2, sg, su, sd,
                   ln2_g, ln2_b, alpha)
    return out.reshape(b, s, d)


def kernel(x, c, ctx, c_ctx, w_ada, b_ada, w_in, hg_lb_fwd, hg_lb_bwd, hg_norm_g, na_rpb, w_branch_a, w_branch_b, w_out, ln1_g, ln1_b, w_router, router_bias, w_e_gate, w_e_up, w_e_down, w_sh_gate, w_sh_up, w_sh_down, ln2_g, ln2_b):
    depth = w_ada.shape[0]
    assert depth == 1, "single-layer block"
    b, s, d = x.shape
    alpha = (2.0 * depth) ** 0.25
    l = 0
    lb_fwd = jnp.cumsum(jax.nn.softmax(hg_lb_fwd.astype(F32), axis=0), axis=0)[l]
    lb_bwd = jnp.cumsum(jax.nn.softmax(hg_lb_bwd.astype(F32), axis=0), axis=0)[l]

    cond_rows = jnp.concatenate([c, c_ctx[None, :], jnp.zeros((8 - b - 1, d), F32)], axis=0)
    mod = _ada(cond_rows, w_ada[l], b_ada[l])
    sh1, sc1, g1, sh2, sc2, g2 = [m[:b, None, :] for m in jnp.split(mod, 6, axis=-1)]
    csh1, csc1 = [jnp.broadcast_to(m[b:b + 1, None, :], (b, 1, d)) for m in jnp.split(mod, 6, axis=-1)[:2]]

    w_in_b = w_in[l].astype(BF16)
    p = _inproj(x, sh1, sc1, w_in_b)
    pc = _inproj(ctx, csh1, csc1, w_in_b)

    o_f, o_b = _hgrn(p, pc, lb_fwd, lb_bwd)
    y_na = _natten(p, pc, *_na_tables(na_rpb[l], s))

    x1, h2, topi, topw, cnt = _merge(o_f, o_b, p, y_na, x, g1, sh2, sc2, hg_norm_g[l], ln1_g[l], ln1_b[l],
                                     w_branch_a[l].astype(BF16), w_branch_b[l].astype(BF16),
                                     w_out[l].astype(BF16), w_router[l].T, router_bias[l], alpha)

    return _moe(h2, topi, topw, cnt, x1, g2,
                w_e_gate[l], w_e_up[l], w_e_down[l],
                w_sh_gate[l].astype(BF16), w_sh_up[l].astype(BF16), w_sh_down[l].astype(BF16),
                ln2_g[l], ln2_b[l], alpha)
```

```python
import functools

import numpy as np
import jax
import jax.numpy as jnp
from jax import lax
from jax.experimental import pallas as pl
from jax.experimental.pallas import tpu as pltpu
from jax.experimental.pallas import tpu_sc as plsc

F32 = jnp.float32
BF16 = jnp.bfloat16

D_MODEL = 1024
GRID_W = 64
HG_HEADS = 8
HG_DK = 128
HG_CHUNK = 64
NA_HEADS = 16
NA_HD = 64
NA_WIN_R = 8
NA_WIN_C = 16
ROPE_THETA = 10000.0
NEG_INF = -1e30
N_EXPERTS = 64
EXPERT_DIM = 256
TOP_K = 8
N_GROUPS = 8
TOPK_GROUPS = 4
ROUTED_SCALE = 2.5
LN_EPS = 1e-6
N_SECTIONS = 10
SEC_Q, SEC_FF, SEC_FB, SEC_I, SEC_OG, SEC_NQ, SEC_NK, SEC_NV, SEC_GA, SEC_GB = range(10)

VMEM_LIMIT = 56 * 1024 * 1024


def _cparams(sem):
    return pltpu.CompilerParams(dimension_semantics=sem, vmem_limit_bytes=VMEM_LIMIT)


def _normalize(x):
    mu = jnp.mean(x, axis=-1, keepdims=True)
    xc = x - mu
    var = jnp.mean(xc * xc, axis=-1, keepdims=True)
    return xc * lax.rsqrt(var + LN_EPS)


def _silu(x):
    return x * jax.nn.sigmoid(x)


def _dot(a, b):
    return jnp.dot(a, b, preferred_element_type=F32)


def _dot_nt(a, b):
    return lax.dot_general(a, b, (((1,), (1,)), ((), ())), preferred_element_type=F32)


def _dot_tn(a, b):
    return lax.dot_general(a, b, (((0,), (0,)), ((), ())), preferred_element_type=F32)


U32 = jnp.uint32


def _pack_words(x):
    half = x.shape[1] // 2
    lo = lax.bitcast_convert_type(x[:, :half].astype(BF16).astype(F32), U32) >> 16
    hi = lax.bitcast_convert_type(x[:, half:].astype(BF16).astype(F32), U32) & jnp.uint32(0xFFFF0000)
    return lo | hi


def _unpack_words(w):
    lo = lax.bitcast_convert_type(w << 16, F32)
    hi = lax.bitcast_convert_type(w & jnp.uint32(0xFFFF0000), F32)
    return jnp.concatenate([lo, hi], axis=-1)


def _split3(x):
    hi = x.astype(BF16)
    r1 = x - hi.astype(F32)
    mid = r1.astype(BF16)
    lo = (r1 - mid.astype(F32)).astype(BF16)
    return hi, mid, lo


def _ada_kernel(c_ref, w_ref, b_ref, o_ref):
    cond = _silu(c_ref[...])
    o_ref[...] = _dot(cond.astype(BF16), w_ref[...].astype(BF16)) + b_ref[...]


def _ada(cond_rows, w_ada, b_ada):
    r, d = cond_rows.shape
    n = w_ada.shape[1]
    tn = 1024
    return pl.pallas_call(
        _ada_kernel,
        grid=(n // tn,),
        in_specs=[pl.BlockSpec((r, d), lambda j: (0, 0)),
                  pl.BlockSpec((d, tn), lambda j: (0, j)),
                  pl.BlockSpec((1, tn), lambda j: (0, j))],
        out_specs=pl.BlockSpec((r, tn), lambda j: (0, j)),
        out_shape=jax.ShapeDtypeStruct((r, n), F32),
        compiler_params=_cparams(("arbitrary",)),
        name="ada",
    )(cond_rows, w_ada, b_ada.reshape(1, n))


INPROJ_TOK = 2048


def _inproj_kernel(x_ref, sh_ref, sc_ref, w_ref, o_ref, h_ref):
    @pl.when(pl.program_id(2) == 0)
    def _():
        h = _normalize(x_ref[...]) * (1.0 + sc_ref[...]) + sh_ref[...]
        h_ref[...] = h.astype(BF16)

    o_ref[...] = _dot(h_ref[...], w_ref[...])


def _inproj(x, shift, scale, w_in_bf16):
    b, s, d = x.shape
    tm = min(INPROJ_TOK, s)
    nj = w_in_bf16.shape[1] // d
    return pl.pallas_call(
        _inproj_kernel,
        grid=(b, s // tm, nj),
        in_specs=[pl.BlockSpec((None, tm, d), lambda bi, i, j: (bi, i, 0)),
                  pl.BlockSpec((None, 1, d), lambda bi, i, j: (bi, 0, 0)),
                  pl.BlockSpec((None, 1, d), lambda bi, i, j: (bi, 0, 0)),
                  pl.BlockSpec((d, d), lambda bi, i, j: (0, j))],
        out_specs=pl.BlockSpec((None, None, tm, d), lambda bi, i, j: (j, bi, i, 0)),
        out_shape=jax.ShapeDtypeStruct((nj, b, s, d), F32),
        scratch_shapes=[pltpu.VMEM((tm, d), BF16)],
        compiler_params=_cparams(("arbitrary", "arbitrary", "arbitrary")),
        name="inproj",
    )(x, shift, scale, w_in_bf16)


def _hgrn_gates(q, fraw, v, lb, tri_bf16, last_row):
    f = lb + (1.0 - lb) * jax.nn.sigmoid(fraw)
    k = 1.0 - f
    lf = jnp.log(f)
    hi, mid, lo = _split3(lf)
    a = _dot(tri_bf16, hi) + _dot(tri_bf16, mid) + _dot(tri_bf16, lo)
    a_last = a[last_row:last_row + 1, :]
    kd = (k * jnp.exp(a_last - a)).astype(BF16)
    decay = jnp.exp(a_last)
    qa = kb = None
    if q is not None:
        qa = (_silu(q) * jnp.exp(a)).astype(BF16)
        kb = (k * jnp.exp(-a)).astype(BF16)
    return qa, kb, kd, v.astype(BF16), decay


def _hgrn_chunks(chunks, st_ref):
    first = []
    for d, ((qa, kb, kd, vb, decay), keep) in enumerate(chunks):
        for h in range(HG_HEADS):
            sl = slice(h * HG_DK, (h + 1) * HG_DK)
            st = st_ref[d, h]
            if qa is not None:
                first.append((_dot_nt(qa[:, sl], kb[:, sl]), _dot_nt(qa[:, sl], st.astype(BF16))))
            st_ref[d, h] = st * decay[:, sl] + _dot_tn(vb[:, sl], kd[:, sl])
    results = []
    for d, ((qa, kb, kd, vb, decay), keep) in enumerate(chunks):
        if qa is None:
            results.append(None)
            continue
        outs = []
        for h in range(HG_HEADS):
            sl = slice(h * HG_DK, (h + 1) * HG_DK)
            s_qk, o_state = first.pop(0)
            outs.append(_dot(jnp.where(keep, s_qk, 0.0).astype(BF16), vb[:, sl]) + o_state)
        results.append(jnp.concatenate(outs, axis=-1))
    return results


def _hgrn_kernel(qf_ref, ff_ref, if_ref, qb_ref, fb_ref, ib_ref, cff_ref, cfb_ref, ci_ref,
                 lbf_ref, lbb_ref, of_ref, ob_ref, st_ref, *, n_sub, n_ctx_sub):
    n = pl.program_id(1)
    c = HG_CHUNK
    row = lax.broadcasted_iota(jnp.int32, (c, c), 0)
    col = lax.broadcasted_iota(jnp.int32, (c, c), 1)
    keep_f = col <= row
    keep_b = col >= row
    tri_f = keep_f.astype(F32).astype(BF16)
    tri_b = keep_b.astype(F32).astype(BF16)
    lbf = lbf_ref[...]
    lbb = lbb_ref[...]

    @pl.when(n == 0)
    def _():
        st_ref[...] = jnp.zeros_like(st_ref)

        def body(i, carry):
            r0 = pl.multiple_of(i * c, c)
            r1 = pl.multiple_of((n_ctx_sub - 1 - i) * c, c)
            gf = _hgrn_gates(None, cff_ref[pl.ds(r0, c), :], ci_ref[pl.ds(r0, c), :], lbf, tri_f, c - 1)
            gb = _hgrn_gates(None, cfb_ref[pl.ds(r1, c), :], ci_ref[pl.ds(r1, c), :], lbb, tri_b, 0)
            _hgrn_chunks([(gf, keep_f), (gb, keep_b)], st_ref)
            return carry

        lax.fori_loop(0, n_ctx_sub, body, 0)

    @pl.when(n > 0)
    def _():
        def body(i, carry):
            r0 = pl.multiple_of(i * c, c)
            r1 = pl.multiple_of((n_sub - 1 - i) * c, c)
            gf = _hgrn_gates(qf_ref[pl.ds(r0, c), :], ff_ref[pl.ds(r0, c), :], if_ref[pl.ds(r0, c), :],
                             lbf, tri_f, c - 1)
            gb = _hgrn_gates(qb_ref[pl.ds(r1, c), :], fb_ref[pl.ds(r1, c), :], ib_ref[pl.ds(r1, c), :],
                             lbb, tri_b, 0)
            o_f, o_b = _hgrn_chunks([(gf, keep_f), (gb, keep_b)], st_ref)
            of_ref[pl.ds(r0, c), :] = o_f
            ob_ref[pl.ds(r1, c), :] = o_b
            return carry

        lax.fori_loop(0, n_sub, body, 0, unroll=True)


def _hgrn(p, pc, lb_fwd, lb_bwd):
    _, b, s, w = p.shape
    ctx_len = pc.shape[2]
    tb = min(256, s)
    nb = s // tb
    fwd = lambda bi, n: jnp.maximum(n - 1, 0)
    bwd = lambda bi, n: nb - 1 - jnp.maximum(n - 1, 0)

    def sec(section, blk):
        return pl.BlockSpec((None, None, tb, w), lambda bi, n: (section, bi, blk(bi, n), 0))

    def csec(section):
        return pl.BlockSpec((None, None, ctx_len, w), lambda bi, n: (section, bi, 0, 0))

    vec = pl.BlockSpec((1, w), lambda bi, n: (0, 0))
    kern = functools.partial(_hgrn_kernel, n_sub=tb // HG_CHUNK, n_ctx_sub=ctx_len // HG_CHUNK)
    return pl.pallas_call(
        kern,
        grid=(b, nb + 1),
        in_specs=[sec(SEC_Q, fwd), sec(SEC_FF, fwd), sec(SEC_I, fwd),
                  sec(SEC_Q, bwd), sec(SEC_FB, bwd), sec(SEC_I, bwd),
                  csec(SEC_FF), csec(SEC_FB), csec(SEC_I), vec, vec],
        out_specs=[pl.BlockSpec((None, tb, w), lambda bi, n: (bi, fwd(bi, n), 0)),
                   pl.BlockSpec((None, tb, w), lambda bi, n: (bi, bwd(bi, n), 0))],
        out_shape=[jax.ShapeDtypeStruct((b, s, w), F32), jax.ShapeDtypeStruct((b, s, w), F32)],
        scratch_shapes=[pltpu.VMEM((2, HG_HEADS, HG_DK, HG_DK), F32)],
        compiler_params=_cparams(("arbitrary", "arbitrary")),
        name="hgrn",
    )(p, p, p, p, p, p, pc, pc, pc, lb_fwd.reshape(1, w), lb_bwd.reshape(1, w))


NA_ROWS_PER_STEP = 32
NA_PREP_ROWS = 512
NA_KEY_TILE = 128
NA_SPAN = (NA_WIN_R + 2) * GRID_W


def _rope(t, cos, sin_signed, first_half):
    w = t.shape[-1]
    partner = jnp.where(first_half, pltpu.roll(t, w - 16, 1), pltpu.roll(t, 16, 1))
    return t * cos + partner * sin_signed


def _fold_lanes(op, *arrays):
    tiles = [a[:, c:c + 128] for a in arrays for c in range(0, a.shape[-1], 128)]
    acc = tiles[0]
    for t in tiles[1:]:
        acc = op(acc, t)
    return acc


def _rope_tables(rowtab_ref, coltab_ref, row0, n_rows, row_lane):
    out = []
    for i in range(2):
        rt = rowtab_ref[i, pl.ds(row0, n_rows), :]
        by_row = jnp.concatenate([jnp.broadcast_to(rt[r:r + 1, :], (GRID_W, rt.shape[1])) for r in range(n_rows)],
                                 axis=0)
        by_col = jnp.concatenate([coltab_ref[i]] * n_rows, axis=0)
        out.append(jnp.where(row_lane, by_row, by_col))
    return out


def _natten_kernel(q_ref, k_ref, v_ref, kc_ref, vc_ref, rowtab_ref, coltab_ref, t2_ref, o_ref,
                   kt_s, v_s, kc_s, vc_s, bias_s, *, rows):
    rblk = pl.program_id(2)
    hd = NA_HD
    lane = lax.broadcasted_iota(jnp.int32, (1, 2 * hd), 1)
    first_half = (lane % 32) < 16
    row_lane = (lane % hd) < hd // 2
    scale = NA_HD ** -0.5

    def values_and_ones(v_pair, h):
        vh = v_pair if h == 0 else pltpu.roll(v_pair, hd, 1)
        return jnp.where(lane < hd, vh, jnp.where(lane == hd, 1.0, 0.0)).astype(BF16)

    @pl.when(rblk == 0)
    def _():
        kc = kc_ref[...].astype(BF16)
        qi = lax.broadcasted_iota(jnp.int32, (GRID_W, GRID_W), 0)
        ki = lax.broadcasted_iota(jnp.int32, (GRID_W, GRID_W), 1)
        cstart = jnp.clip(qi - NA_WIN_C // 2, 0, GRID_W - NA_WIN_C)
        in_win = (ki >= cstart) & (ki < cstart + NA_WIN_C)
        masked = jnp.full((GRID_W, GRID_W), NEG_INF, F32)
        s_len = k_ref.shape[0]
        for h in range(2):
            sl = slice(h * hd, (h + 1) * hd)
            kc_s[h] = kc[:, sl]
            vc_s[h] = values_and_ones(vc_ref[...], h)
            kt_s[h, s_len // NA_KEY_TILE] = jnp.zeros((hd, NA_KEY_TILE), BF16)
            v_s[h, s_len:s_len + NA_KEY_TILE, :] = jnp.zeros((NA_KEY_TILE, 2 * hd), BF16)
            tiles = [jnp.where(in_win, t2_ref[h, dr], NEG_INF) for dr in range(2 * NA_WIN_R - 1)]
            for bidx in range(NA_WIN_R + 1):
                v, par = (bidx, 0) if bidx < NA_WIN_R else (NA_WIN_R // 2, 1)
                for piece in range(NA_SPAN // GRID_W):
                    j = piece - par
                    tile = tiles[NA_WIN_R - 1 - v + j] if 0 <= j < NA_WIN_R else masked
                    bias_s[h, bidx, :, piece * GRID_W:(piece + 1) * GRID_W] = tile

        eye = (lax.broadcasted_iota(jnp.int32, (2 * hd, 2 * hd), 0)
               == lax.broadcasted_iota(jnp.int32, (2 * hd, 2 * hd), 1)).astype(F32).astype(BF16)

        def prep(i, carry):
            r0 = pl.multiple_of(i * NA_PREP_ROWS, NA_PREP_ROWS)
            rws = pl.ds(r0, NA_PREP_ROWS)
            cos, sin = _rope_tables(rowtab_ref, coltab_ref, i * (NA_PREP_ROWS // GRID_W), NA_PREP_ROWS // GRID_W,
                                    row_lane)
            kr = _rope(k_ref[rws, :], cos, sin, first_half)
            krt = _dot_nt(eye, kr.astype(BF16)).astype(BF16)
            vv = v_ref[rws, :]
            for h in range(2):
                sl = slice(h * hd, (h + 1) * hd)
                for c in range(NA_PREP_ROWS // NA_KEY_TILE):
                    kt_s[h, i * (NA_PREP_ROWS // NA_KEY_TILE) + c] = krt[sl, c * NA_KEY_TILE:(c + 1) * NA_KEY_TILE]
                v_s[h, rws, :] = values_and_ones(vv, h)
            return carry

        lax.fori_loop(0, s_len // NA_PREP_ROWS, prep, 0, unroll=4)

    tq = NA_ROWS_PER_STEP * GRID_W
    q = q_ref[...] * scale
    cos, sin = _rope_tables(rowtab_ref, coltab_ref, rblk * NA_ROWS_PER_STEP, NA_ROWS_PER_STEP, row_lane)
    qr = _rope(q, cos, sin, first_half)
    qb = q.astype(BF16)
    qrb = qr.astype(BF16)
    rws = [slice(rr * GRID_W, (rr + 1) * GRID_W) for rr in range(NA_ROWS_PER_STEP)]
    tile0, bidx = [], []
    for rr in range(NA_ROWS_PER_STEP):
        r = rblk * NA_ROWS_PER_STEP + rr
        rs = jnp.clip(r - NA_WIN_R // 2, 0, rows - NA_WIN_R)
        tile0.append(lax.shift_right_logical(rs, 1))
        bidx.append(jnp.where((rs & 1) == 1, NA_WIN_R, r - rs))

    def scores(h):
        sl = slice(h * hd, (h + 1) * hd)
        qrb_h = qrb[:, sl]
        s_ctx_all = _dot_nt(qb[:, sl], kc_s[h])
        s_win = []
        for rr in range(NA_ROWS_PER_STEP):
            kt = kt_s[h, pl.ds(tile0[rr], NA_SPAN // NA_KEY_TILE)]
            kt = jnp.concatenate([kt[c] for c in range(NA_SPAN // NA_KEY_TILE)], axis=-1)
            s_win.append(_dot(qrb_h[rws[rr]], kt))
        return s_win, s_ctx_all

    def softmax(h, s_win, s_ctx_all):
        e_win, e_ctx = [], []
        for rr in range(NA_ROWS_PER_STEP):
            sw = s_win[rr] + bias_s[h, bidx[rr]]
            sc = s_ctx_all[rws[rr]]
            m = jnp.max(_fold_lanes(jnp.maximum, sw, sc), axis=-1, keepdims=True)
            e_win.append(jnp.exp(sw - m).astype(BF16))
            e_ctx.append(jnp.exp(sc - m).astype(BF16))
        return e_win, e_ctx

    def values(h, e_win, e_ctx):
        o_win = []
        for rr in range(NA_ROWS_PER_STEP):
            k0 = pl.multiple_of(tile0[rr] * NA_KEY_TILE, NA_KEY_TILE)
            o_win.append(_dot(e_win[rr], v_s[h, pl.ds(k0, NA_SPAN), :]))
        o = jnp.concatenate(o_win, axis=0) + _dot(jnp.concatenate(e_ctx, axis=0), vc_s[h])
        return o[:, :hd] * (1.0 / o[:, hd:hd + 1])

    s0 = scores(0)
    s1 = scores(1)
    p0 = softmax(0, *s0)
    o0 = values(0, *p0)
    p1 = softmax(1, *s1)
    o1 = values(1, *p1)
    o_ref[...] = jnp.concatenate([o0, o1], axis=-1)


def _na_tables(rpb, s):
    half = NA_HD // 2
    inv = jnp.power(ROPE_THETA, -jnp.arange(0, half, 2, dtype=F32) / half)

    def tables(n):
        ang = jnp.arange(n, dtype=F32)[:, None] * inv[None, :]
        reps = 2 * NA_HD // half
        return jnp.stack([jnp.tile(jnp.cos(ang), (1, 2 * reps)),
                          jnp.tile(jnp.concatenate([-jnp.sin(ang), jnp.sin(ang)], axis=-1), (1, reps))])

    rowtab, coltab = tables(s // GRID_W), tables(GRID_W)

    pad = GRID_W - NA_WIN_C
    rp = jnp.pad(rpb.astype(F32), ((0, 0), (0, 0), (pad, pad)), mode="edge")
    t2 = jnp.stack([rp[:, :, GRID_W - 1 - qc:2 * GRID_W - 1 - qc] for qc in range(GRID_W)], axis=2)
    return rowtab, coltab, t2


def _natten(p, pc, rowtab, coltab, t2):
    _, b, s, w = p.shape
    ctx_len = pc.shape[2]
    rows = s // GRID_W
    assert rows >= NA_WIN_R and rows % NA_ROWS_PER_STEP == 0
    tq = NA_ROWS_PER_STEP * GRID_W
    hw = 2 * NA_HD
    nhp = w // hw
    kern = functools.partial(_natten_kernel, rows=rows)
    return pl.pallas_call(
        kern,
        grid=(b, nhp, rows // NA_ROWS_PER_STEP),
        in_specs=[pl.BlockSpec((None, None, tq, hw), lambda bi, hp, r: (SEC_NQ, bi, r, hp)),
                  pl.BlockSpec((None, None, s, hw), lambda bi, hp, r: (SEC_NK, bi, 0, hp)),
                  pl.BlockSpec((None, None, s, hw), lambda bi, hp, r: (SEC_NV, bi, 0, hp)),
                  pl.BlockSpec((None, None, ctx_len, hw), lambda bi, hp, r: (SEC_NK, bi, 0, hp)),
                  pl.BlockSpec((None, None, ctx_len, hw), lambda bi, hp, r: (SEC_NV, bi, 0, hp)),
                  pl.BlockSpec((2, rows, hw), lambda bi, hp, r: (0, 0, 0)),
                  pl.BlockSpec((2, GRID_W, hw), lambda bi, hp, r: (0, 0, 0)),
                  pl.BlockSpec((2, 2 * NA_WIN_R - 1, GRID_W, GRID_W), lambda bi, hp, r: (hp, 0, 0, 0))],
        out_specs=pl.BlockSpec((None, tq, hw), lambda bi, hp, r: (bi, r, hp)),
        out_shape=jax.ShapeDtypeStruct((b, s, w), F32),
        scratch_shapes=[pltpu.VMEM((2, s // NA_KEY_TILE + 1, NA_HD, NA_KEY_TILE), BF16),
                        pltpu.VMEM((2, s + NA_KEY_TILE, hw), BF16),
                        pltpu.VMEM((2, ctx_len, NA_HD), BF16), pltpu.VMEM((2, ctx_len, hw), BF16),
                        pltpu.VMEM((2, NA_WIN_R + 1, GRID_W, NA_SPAN), F32)],
        compiler_params=_cparams(("arbitrary", "arbitrary", "arbitrary")),
        name="natten",
    )(p, p, p, pc, pc, rowtab, coltab, t2)


def _route(logits_t, rbias):
    e, t = logits_t.shape
    gsz = e // N_GROUPS
    scores = jax.nn.sigmoid(logits_t)
    sel = scores + rbias
    neg = -jnp.inf
    sub = lax.broadcasted_iota(jnp.int32, (gsz, t), 0).astype(F32)
    gscore = []
    for g in range(N_GROUPS):
        grp = sel[g * gsz:(g + 1) * gsz, :]
        m1 = jnp.max(grp, axis=0, keepdims=True)
        first = jnp.min(jnp.where(grp == m1, sub, float(gsz)), axis=0, keepdims=True)
        m2 = jnp.max(jnp.where(sub == first, neg, grp), axis=0, keepdims=True)
        gscore.append(m1 + m2)
    masked = []
    for g in range(N_GROUPS):
        rank = jnp.zeros((1, t), F32)
        for g2 in range(N_GROUPS):
            if g2 == g:
                continue
            if g2 < g:
                ahead = gscore[g2] >= gscore[g]
            else:
                ahead = gscore[g2] > gscore[g]
            rank = rank + jnp.where(ahead, 1.0, 0.0)
        masked.append(jnp.where(rank < TOPK_GROUPS, sel[g * gsz:(g + 1) * gsz, :], neg))
    work = jnp.concatenate(masked, axis=0)
    eidx = lax.broadcasted_iota(jnp.int32, (e, t), 0).astype(F32)
    idxs, ws = [], []
    chosen = jnp.zeros((e, t), F32)
    for _ in range(TOP_K):
        m = jnp.max(work, axis=0, keepdims=True)
        first = jnp.min(jnp.where(work == m, eidx, float(e)), axis=0, keepdims=True)
        pick = eidx == first
        idxs.append(first)
        ws.append(jnp.sum(jnp.where(pick, scores, 0.0), axis=0, keepdims=True))
        chosen = jnp.where(pick, 1.0, chosen)
        work = jnp.where(pick, neg, work)
    w = jnp.concatenate(ws, axis=0)
    w = w / jnp.sum(w, axis=0, keepdims=True) * ROUTED_SCALE
    return jnp.concatenate(idxs, axis=0).astype(jnp.int32), w, chosen


MERGE_TOK = 512
MERGE_SUB = 256


def _merge_kernel(of_ref, ob_ref, og_ref, yna_ref, ga_ref, gb_ref, x_ref, g1_ref, sh2_ref, sc2_ref,
                  hgg_ref, ln1g_ref, ln1b_ref, wa_ref, wb_ref, wo_ref, wr_ref, rb_ref,
                  x1_ref, h2_ref, topi_ref, topw_ref, cnt_ref, *, alpha):
    tm = x_ref.shape[0]
    subs = [slice(i * MERGE_SUB, (i + 1) * MERGE_SUB) for i in range(tm // MERGE_SUB)]

    def branches(rows):
        o = of_ref[rows, :] + ob_ref[rows, :]
        parts = []
        for h in range(HG_HEADS):
            oh = o[:, h * HG_DK:(h + 1) * HG_DK]
            parts.append(oh * lax.rsqrt(jnp.mean(oh * oh, axis=-1, keepdims=True) + LN_EPS))
        y_hg = jnp.concatenate(parts, axis=-1) * hgg_ref[...] * _silu(og_ref[rows, :])
        return _dot(y_hg.astype(BF16), wa_ref[...]), _dot(yna_ref[rows, :].astype(BF16), wb_ref[...])

    def out_proj(rows, ya, yb):
        t = jax.nn.sigmoid(ga_ref[rows, :]) * ya + jax.nn.sigmoid(gb_ref[rows, :]) * yb
        return _dot(t.astype(BF16), wo_ref[...])

    def norms_router(rows, i, y):
        x1 = _normalize(alpha * x_ref[rows, :] + g1_ref[...] * y) * ln1g_ref[...] + ln1b_ref[...]
        x1_ref[rows, :] = x1
        h2 = _normalize(x1) * (1.0 + sc2_ref[...]) + sh2_ref[...]
        h2_ref[rows, :] = _pack_words(h2)
        hh, hm, hl = _split3(h2)
        wh, wm, wl = _split3(wr_ref[...])
        return (_dot_nt(wh, hh) + _dot_nt(wh, hm) + _dot_nt(wm, hh)
                + _dot_nt(wh, hl) + _dot_nt(wl, hh) + _dot_nt(wm, hm))

    ab = [branches(rows) for rows in subs]
    ys = [out_proj(rows, *ab[i]) for i, rows in enumerate(subs)]
    logits = [norms_router(rows, i, ys[i]) for i, rows in enumerate(subs)]

    @pl.when((pl.program_id(0) == 0) & (pl.program_id(1) == 0))
    def _():
        cnt_ref[...] = jnp.zeros_like(cnt_ref)

    for i, rows in enumerate(subs):
        topi, topw, chosen = _route(logits[i], rb_ref[...])
        topi_ref[:, rows] = topi
        topw_ref[:, rows] = topw
        cnt_ref[...] += jnp.sum(chosen, axis=1, keepdims=True)


def _merge(o_f, o_b, p, y_na, x, g1, sh2, sc2, hg_norm_g, ln1_g, ln1_b, w_a, w_b, w_o, w_router_t, router_bias,
           alpha):
    b, s, d = x.shape
    tm = min(MERGE_TOK, s)
    e = w_router_t.shape[0]
    tok = lambda bi, i: (bi, i, 0)
    blk = pl.BlockSpec((None, tm, d), tok)

    def sec(section):
        return pl.BlockSpec((None, None, tm, d), lambda bi, i: (section, bi, i, 0))

    mod = pl.BlockSpec((None, 1, d), lambda bi, i: (bi, 0, 0))
    vec = pl.BlockSpec((1, d), lambda bi, i: (0, 0))
    mat = pl.BlockSpec((d, d), lambda bi, i: (0, 0), pipeline_mode=pl.Buffered(1))
    return pl.pallas_call(
        functools.partial(_merge_kernel, alpha=alpha),
        grid=(b, s // tm),
        in_specs=[blk, blk, sec(SEC_OG), blk, sec(SEC_GA), sec(SEC_GB), blk, mod, mod, mod,
                  vec, vec, vec, mat, mat, mat,
                  pl.BlockSpec((e, d), lambda bi, i: (0, 0)),
                  pl.BlockSpec((e, 1), lambda bi, i: (0, 0))],
        out_specs=[blk,
                   pl.BlockSpec((tm, d // 2), lambda bi, i: (bi * (s // tm) + i, 0)),
                   pl.BlockSpec((None, TOP_K, tm), lambda bi, i: (bi, 0, i)),
                   pl.BlockSpec((None, TOP_K, tm), lambda bi, i: (bi, 0, i)),
                   pl.BlockSpec((e, 128), lambda bi, i: (0, 0))],
        out_shape=[jax.ShapeDtypeStruct((b, s, d), F32),
                   jax.ShapeDtypeStruct((b * s, d // 2), U32),
                   jax.ShapeDtypeStruct((b, TOP_K, s), jnp.int32), jax.ShapeDtypeStruct((b, TOP_K, s), F32),
                   jax.ShapeDtypeStruct((e, 128), F32)],
        compiler_params=_cparams(("arbitrary", "arbitrary")),
        name="merge",
    )(o_f, o_b, p, y_na, p, p, x, g1, sh2, sc2, hg_norm_g.reshape(1, d), ln1_g.reshape(1, d),
      ln1_b.reshape(1, d), w_a, w_b, w_o, w_router_t, router_bias.reshape(e, 1))


MOE_TILE = 512
MOE_TOK = 512


def _plan_kernel(topi_ref, off_ref, dest_ref, carry_ref):
    @pl.when(pl.program_id(0) == 0)
    def _():
        carry_ref[...] = jnp.zeros_like(carry_ref)

    topi = topi_ref[...]
    tok = topi.shape[1]
    eidx = lax.broadcasted_iota(jnp.int32, (N_EXPERTS, tok), 0)
    hits = [eidx == topi[k:k + 1, :] for k in range(TOP_K)]
    m = jnp.zeros((N_EXPERTS, tok), F32)
    for hit in hits:
        m = jnp.where(hit, 1.0, m)
    before = (lax.broadcasted_iota(jnp.int32, (tok, tok), 0)
              < lax.broadcasted_iota(jnp.int32, (tok, tok), 1)).astype(F32).astype(BF16)
    row = off_ref[...] + carry_ref[...] + _dot(m.astype(BF16), before)
    dest = [jnp.sum(jnp.where(hit, row, 0.0), axis=0, keepdims=True) for hit in hits]
    dest_ref[...] = jnp.concatenate(dest, axis=0).astype(jnp.int32)
    carry_ref[...] += jnp.sum(m, axis=1, keepdims=True)


def _plan(topi, seg_off):
    b, k, s = topi.shape
    per_b = s // MOE_TOK
    blk = pl.BlockSpec((None, k, MOE_TOK), lambda i: (i // per_b, 0, i % per_b))
    return pl.pallas_call(
        _plan_kernel,
        grid=(b * per_b,),
        in_specs=[blk, pl.BlockSpec((N_EXPERTS, 1), lambda i: (0, 0))],
        out_specs=blk,
        out_shape=jax.ShapeDtypeStruct((b, k, s), jnp.int32),
        scratch_shapes=[pltpu.VMEM((N_EXPERTS, 1), F32)],
        compiler_params=_cparams(("arbitrary",)),
        name="plan",
    )(topi, seg_off.astype(F32).reshape(N_EXPERTS, 1))


SC_WINDOW = 128


def _sc_workers():
    info = plsc.get_sparse_core_info()
    return info.num_cores, info.num_cores * info.num_subcores


def _scatter_rows(src, idx, zero_idx, n_rows):
    t, w = src.shape
    m, mz = idx.shape[0], zero_idx.shape[0]
    n_cores, n_workers = _sc_workers()
    per_worker, per_worker_z = m // n_workers, mz // n_workers
    assert m % t == 0 and t % SC_WINDOW == 0
    assert per_worker * n_workers == m and per_worker % SC_WINDOW == 0
    assert per_worker_z * n_workers == mz and per_worker_z % SC_WINDOW == 0
    mesh = plsc.VectorSubcoreMesh(core_axis_name="core", subcore_axis_name="subcore")

    @functools.partial(
        pl.kernel, mesh=mesh, out_type=jax.ShapeDtypeStruct((n_rows, w), src.dtype),
        scratch_types=[pltpu.VMEM((SC_WINDOW,), jnp.int32), pltpu.VMEM((SC_WINDOW, w), src.dtype),
                       pltpu.SemaphoreType.DMA])
    def scatter(src_hbm, idx_hbm, zeros_hbm, zero_idx_hbm, out_hbm, idx_v, rows_v, sem):
        worker = lax.axis_index("subcore") * n_cores + lax.axis_index("core")

        @pl.loop(0, per_worker // SC_WINDOW)
        def _(step):
            base = pl.multiple_of(worker * per_worker + step * SC_WINDOW, SC_WINDOW)
            first = pl.multiple_of(lax.rem(base, t), SC_WINDOW)
            pltpu.sync_copy(idx_hbm.at[pl.ds(base, SC_WINDOW)], idx_v)
            pltpu.sync_copy(src_hbm.at[pl.ds(first, SC_WINDOW)], rows_v)
            pltpu.async_copy(rows_v, out_hbm.at[idx_v], sem).wait()

        pltpu.sync_copy(zeros_hbm, rows_v)

        @pl.loop(0, per_worker_z // SC_WINDOW)
        def _(step):
            base = pl.multiple_of(worker * per_worker_z + step * SC_WINDOW, SC_WINDOW)
            pltpu.sync_copy(zero_idx_hbm.at[pl.ds(base, SC_WINDOW)], idx_v)
            pltpu.async_copy(rows_v, out_hbm.at[idx_v], sem).wait()

    return scatter(src, idx, jnp.zeros((SC_WINDOW, w), src.dtype), zero_idx)


def _experts_kernel(te_ref, tb_ref, nt_ref, xs_ref, wg_ref, wu_ref, wd_ref, ys_ref):
    @pl.when(pl.program_id(0) < nt_ref[0])
    def _():
        x = _unpack_words(xs_ref[...]).astype(BF16)
        act = _silu(_dot(x, wg_ref[...].astype(BF16))) * _dot(x, wu_ref[...].astype(BF16))
        ys_ref[...] = _pack_words(_dot(act.astype(BF16), wd_ref[...].astype(BF16)))


def _experts(xs, tile_expert, tile_block, n_tiles, wg, wu, wd):
    d, f = wg.shape[1], wg.shape[2]
    rows = pl.BlockSpec((MOE_TILE, d // 2), lambda i, te, tb, nt: (tb[i], 0))
    grid_spec = pltpu.PrefetchScalarGridSpec(
        num_scalar_prefetch=3,
        grid=(xs.shape[0] // MOE_TILE,),
        in_specs=[rows,
                  pl.BlockSpec((None, d, f), lambda i, te, tb, nt: (te[i], 0, 0)),
                  pl.BlockSpec((None, d, f), lambda i, te, tb, nt: (te[i], 0, 0)),
                  pl.BlockSpec((None, f, d), lambda i, te, tb, nt: (te[i], 0, 0))],
        out_specs=rows,
    )
    return pl.pallas_call(
        _experts_kernel,
        grid_spec=grid_spec,
        out_shape=jax.ShapeDtypeStruct(xs.shape, U32),
        compiler_params=_cparams(("arbitrary",)),
        name="experts",
    )(tile_expert, tile_block, n_tiles, xs, wg, wu, wd)


def _gather_rows(table, idx):
    m = idx.shape[0]
    w = table.shape[1]
    n_cores, n_workers = _sc_workers()
    per_worker = m // n_workers
    assert per_worker * n_workers == m and per_worker % SC_WINDOW == 0
    mesh = plsc.VectorSubcoreMesh(core_axis_name="core", subcore_axis_name="subcore")

    @functools.partial(
        pl.kernel, mesh=mesh, out_type=jax.ShapeDtypeStruct((m, w), table.dtype),
        scratch_types=[pltpu.VMEM((SC_WINDOW,), jnp.int32), pltpu.VMEM((SC_WINDOW, w), table.dtype),
                       pltpu.SemaphoreType.DMA])
    def gather(table_hbm, idx_hbm, out_hbm, idx_v, rows_v, sem):
        worker = lax.axis_index("subcore") * n_cores + lax.axis_index("core")

        @pl.loop(0, per_worker // SC_WINDOW)
        def _(step):
            base = pl.multiple_of(worker * per_worker + step * SC_WINDOW, SC_WINDOW)
            pltpu.sync_copy(idx_hbm.at[pl.ds(base, SC_WINDOW)], idx_v)
            pltpu.async_copy(table_hbm.at[idx_v], rows_v, sem).wait()
            pltpu.sync_copy(rows_v, out_hbm.at[pl.ds(base, SC_WINDOW)])

    return gather(table, idx)


def _combine_kernel(rows_ref, topw_ref, h_ref, x1_ref, g2_ref, sg_ref, su_ref, sd_ref, ln2g_ref, ln2b_ref,
                    o_ref, *, alpha):
    h = _unpack_words(h_ref[...]).astype(BF16)
    act = _silu(_dot(h, sg_ref[...])) * _dot(h, su_ref[...])
    y = _dot(act.astype(BF16), sd_ref[...])
    w = topw_ref[...].T
    for k in range(TOP_K):
        y = y + w[:, k:k + 1] * _unpack_words(rows_ref[k])
    o_ref[...] = _normalize(alpha * x1_ref[...] + g2_ref[...] * y) * ln2g_ref[...] + ln2b_ref[...]


def _combine(gathered, topw, h2p, x1, g2, sg, su, sd, ln2_g, ln2_b, alpha):
    t, d = x1.shape
    b, k, s = topw.shape
    per_b = s // MOE_TOK
    fs = sg.shape[1]
    rows = pl.BlockSpec((MOE_TOK, d), lambda i: (i, 0))
    packed = pl.BlockSpec((MOE_TOK, d // 2), lambda i: (i, 0))
    vec = pl.BlockSpec((1, d), lambda i: (0, 0))
    return pl.pallas_call(
        functools.partial(_combine_kernel, alpha=alpha),
        grid=(t // MOE_TOK,),
        in_specs=[pl.BlockSpec((k, MOE_TOK, d // 2), lambda i: (0, i, 0)),
                  pl.BlockSpec((None, k, MOE_TOK), lambda i: (i // per_b, 0, i % per_b)),
                  packed, rows,
                  pl.BlockSpec((None, 1, d), lambda i: (i // per_b, 0, 0)),
                  pl.BlockSpec((d, fs), lambda i: (0, 0)),
                  pl.BlockSpec((d, fs), lambda i: (0, 0)),
                  pl.BlockSpec((fs, d), lambda i: (0, 0)),
                  vec, vec],
        out_specs=rows,
        out_shape=jax.ShapeDtypeStruct((t, d), F32),
        compiler_params=_cparams(("arbitrary",)),
        name="combine",
    )(gathered, topw, h2p, x1, g2, sg, su, sd, ln2_g.reshape(1, d), ln2_b.reshape(1, d))


def _moe(h2p, topi, topw, cnt, x1, g2, wg, wu, wd, sg, su, sd, ln2_g, ln2_b, alpha):
    b, s, d = x1.shape
    t = b * s
    cnt = cnt[:, 0].astype(jnp.int32)
    tiles_e = (cnt + (MOE_TILE - 1)) // MOE_TILE
    tiles_cum = jnp.cumsum(tiles_e)
    seg_off = (tiles_cum - tiles_e) * MOE_TILE
    n_tiles_max = t * TOP_K // MOE_TILE + N_EXPERTS
    tile_block = jnp.minimum(jnp.arange(n_tiles_max, dtype=jnp.int32), tiles_cum[-1] - 1)
    tile_expert = jnp.sum((tiles_cum[None, :] <= tile_block[:, None]).astype(jnp.int32), axis=1)
    n_tiles = tiles_cum[-1:].astype(jnp.int32)

    dest = jnp.transpose(_plan(topi, seg_off), (1, 0, 2)).reshape(TOP_K * t)
    j = jnp.arange(MOE_TILE, dtype=jnp.int32)[None, :]
    n_pad = (tiles_e * MOE_TILE - cnt)[:, None]
    spare = (n_tiles_max - 1) * MOE_TILE + j
    zero_idx = jnp.where(j < n_pad, (seg_off + cnt)[:, None] + j, spare).reshape(N_EXPERTS * MOE_TILE)
    xs = _scatter_rows(h2p, dest, zero_idx.astype(jnp.int32), n_tiles_max * MOE_TILE)
    ys = _experts(xs, tile_expert, tile_block, n_tiles, wg, wu, wd)
    gathered = _gather_rows(ys, dest)
    out = _combine(gathered.reshape(TOP_K, t, d // 2), topw, h2p, x1.reshape(t, d), g2, sg, su, sd,
                   ln2_g, ln2_b, alpha)
    return out.reshape(b, s, d)


def kernel(x, c, ctx, c_ctx, w_ada, b_ada, w_in, hg_lb_fwd, hg_lb_bwd, hg_norm_g, na_rpb, w_branch_a, w_branch_b, w_out, ln1_g, ln1_b, w_router, router_bias, w_e_gate, w_e_up, w_e_down, w_sh_gate, w_sh_up, w_sh_down, ln2_g, ln2_b):
    depth = w_ada.shape[0]
    assert depth == 1, "single-layer block"
    b, s, d = x.shape
    alpha = (2.0 * depth) ** 0.25
    l = 0
    lb_fwd = jnp.cumsum(jax.nn.softmax(hg_lb_fwd.astype(F32), axis=0), axis=0)[l]
    lb_bwd = jnp.cumsum(jax.nn.softmax(hg_lb_bwd.astype(F32), axis=0), axis=0)[l]

    cond_rows = jnp.concatenate([c, c_ctx[None, :], jnp.zeros((8 - b - 1, d), F32)], axis=0)
    mod = _ada(cond_rows, w_ada[l], b_ada[l])
    sh1, sc1, g1, sh2, sc2, g2 = [m[:b, None, :] for m in jnp.split(mod, 6, axis=-1)]
    csh1, csc1 = [jnp.broadcast_to(m[b:b + 1, None, :], (b, 1, d)) for m in jnp.split(mod, 6, axis=-1)[:2]]

    w_in_b = w_in[l].astype(BF16)
    p = _inproj(x, sh1, sc1, w_in_b)
    pc = _inproj(ctx, csh1, csc1, w_in_b)

    o_f, o_b = _hgrn(p, pc, lb_fwd, lb_bwd)
    y_na = _natten(p, pc, *_na_tables(na_rpb[l], s))

    x1, h2, topi, topw, cnt = _merge(o_f, o_b, p, y_na, x, g1, sh2, sc2, hg_norm_g[l], ln1_g[l], ln1_b[l],
                                     w_branch_a[l].astype(BF16), w_branch_b[l].astype(BF16),
                                     w_out[l].astype(BF16), w_router[l].T, router_bias[l], alpha)

    return _moe(h2, topi, topw, cnt, x1, g2,
                w_e_gate[l], w_e_up[l], w_e_down[l],
                w_sh_gate[l].astype(BF16), w_sh_up[l].astype(BF16), w_sh_down[l].astype(BF16),
                ln2_g[l], ln2_b[l], alpha)
```

```python
import functools

import numpy as np
import jax
import jax.numpy as jnp
from jax import lax
from jax.experimental import pallas as pl
from jax.experimental.pallas import tpu as pltpu
from jax.experimental.pallas import tpu_sc as plsc

F32 = jnp.float32
BF16 = jnp.bfloat16

D_MODEL = 1024
GRID_W = 64
HG_HEADS = 8
HG_DK = 128
HG_CHUNK = 64
NA_HEADS = 16
NA_HD = 64
NA_WIN_R = 8
NA_WIN_C = 16
ROPE_THETA = 10000.0
NEG_INF = -1e30
N_EXPERTS = 64
EXPERT_DIM = 256
TOP_K = 8
N_GROUPS = 8
TOPK_GROUPS = 4
ROUTED_SCALE = 2.5
LN_EPS = 1e-6
N_SECTIONS = 10
SEC_Q, SEC_FF, SEC_FB, SEC_I, SEC_OG, SEC_NQ, SEC_NK, SEC_NV, SEC_GA, SEC_GB = range(10)

VMEM_LIMIT = 56 * 1024 * 1024


def _cparams(sem):
    return pltpu.CompilerParams(dimension_semantics=sem, vmem_limit_bytes=VMEM_LIMIT)


def _normalize(x):
    mu = jnp.mean(x, axis=-1, keepdims=True)
    xc = x - mu
    var = jnp.mean(xc * xc, axis=-1, keepdims=True)
    return xc * lax.rsqrt(var + LN_EPS)


def _silu(x):
    return x * jax.nn.sigmoid(x)


def _dot(a, b):
    return jnp.dot(a, b, preferred_element_type=F32)


def _dot_nt(a, b):
    return lax.dot_general(a, b, (((1,), (1,)), ((), ())), preferred_element_type=F32)


def _dot_tn(a, b):
    return lax.dot_general(a, b, (((0,), (0,)), ((), ())), preferred_element_type=F32)


U32 = jnp.uint32


def _pack_words(x):
    half = x.shape[1] // 2
    lo = lax.bitcast_convert_type(x[:, :half].astype(BF16).astype(F32), U32) >> 16
    hi = lax.bitcast_convert_type(x[:, half:].astype(BF16).astype(F32), U32) & jnp.uint32(0xFFFF0000)
    return lo | hi


def _unpack_words(w):
    lo = lax.bitcast_convert_type(w << 16, F32)
    hi = lax.bitcast_convert_type(w & jnp.uint32(0xFFFF0000), F32)
    return jnp.concatenate([lo, hi], axis=-1)


def _split3(x):
    hi = x.astype(BF16)
    r1 = x - hi.astype(F32)
    mid = r1.astype(BF16)
    lo = (r1 - mid.astype(F32)).astype(BF16)
    return hi, mid, lo


def _ada_kernel(c_ref, w_ref, b_ref, o_ref):
    cond = _silu(c_ref[...])
    o_ref[...] = _dot(cond.astype(BF16), w_ref[...].astype(BF16)) + b_ref[...]


def _ada(cond_rows, w_ada, b_ada):
    r, d = cond_rows.shape
    n = w_ada.shape[1]
    tn = 1024
    return pl.pallas_call(
        _ada_kernel,
        grid=(n // tn,),
        in_specs=[pl.BlockSpec((r, d), lambda j: (0, 0)),
                  pl.BlockSpec((d, tn), lambda j: (0, j)),
                  pl.BlockSpec((1, tn), lambda j: (0, j))],
        out_specs=pl.BlockSpec((r, tn), lambda j: (0, j)),
        out_shape=jax.ShapeDtypeStruct((r, n), F32),
        compiler_params=_cparams(("arbitrary",)),
        name="ada",
    )(cond_rows, w_ada, b_ada.reshape(1, n))


INPROJ_TOK = 2048


def _inproj_kernel(x_ref, sh_ref, sc_ref, w_ref, o_ref, h_ref):
    @pl.when(pl.program_id(2) == 0)
    def _():
        h = _normalize(x_ref[...]) * (1.0 + sc_ref[...]) + sh_ref[...]
        h_ref[...] = h.astype(BF16)

    o_ref[...] = _dot(h_ref[...], w_ref[...])


def _inproj(x, shift, scale, w_in_bf16):
    b, s, d = x.shape
    tm = min(INPROJ_TOK, s)
    nj = w_in_bf16.shape[1] // d
    return pl.pallas_call(
        _inproj_kernel,
        grid=(b, s // tm, nj),
        in_specs=[pl.BlockSpec((None, tm, d), lambda bi, i, j: (bi, i, 0)),
                  pl.BlockSpec((None, 1, d), lambda bi, i, j: (bi, 0, 0)),
                  pl.BlockSpec((None, 1, d), lambda bi, i, j: (bi, 0, 0)),
                  pl.BlockSpec((d, d), lambda bi, i, j: (0, j))],
        out_specs=pl.BlockSpec((None, None, tm, d), lambda bi, i, j: (j, bi, i, 0)),
        out_shape=jax.ShapeDtypeStruct((nj, b, s, d), F32),
        scratch_shapes=[pltpu.VMEM((tm, d), BF16)],
        compiler_params=_cparams(("arbitrary", "arbitrary", "arbitrary")),
        name="inproj",
    )(x, shift, scale, w_in_bf16)


def _hgrn_gates(q, fraw, v, lb, tri_bf16, last_row):
    f = lb + (1.0 - lb) * jax.nn.sigmoid(fraw)
    k = 1.0 - f
    lf = jnp.log(f)
    hi, mid, lo = _split3(lf)
    a = _dot(tri_bf16, hi) + _dot(tri_bf16, mid) + _dot(tri_bf16, lo)
    a_last = a[last_row:last_row + 1, :]
    kd = (k * jnp.exp(a_last - a)).astype(BF16)
    decay = jnp.exp(a_last)
    qa = kb = None
    if q is not None:
        qa = (_silu(q) * jnp.exp(a)).astype(BF16)
        kb = (k * jnp.exp(-a)).astype(BF16)
    return qa, kb, kd, v.astype(BF16), decay


def _hgrn_chunks(chunks, st_ref):
    first = []
    for d, ((qa, kb, kd, vb, decay), keep) in enumerate(chunks):
        for h in range(HG_HEADS):
            sl = slice(h * HG_DK, (h + 1) * HG_DK)
            st = st_ref[d, h]
            if qa is not None:
                first.append((_dot_nt(qa[:, sl], kb[:, sl]), _dot_nt(qa[:, sl], st.astype(BF16))))
            st_ref[d, h] = st * decay[:, sl] + _dot_tn(vb[:, sl], kd[:, sl])
    results = []
    for d, ((qa, kb, kd, vb, decay), keep) in enumerate(chunks):
        if qa is None:
            results.append(None)
            continue
        outs = []
        for h in range(HG_HEADS):
            sl = slice(h * HG_DK, (h + 1) * HG_DK)
            s_qk, o_state = first.pop(0)
            outs.append(_dot(jnp.where(keep, s_qk, 0.0).astype(BF16), vb[:, sl]) + o_state)
        results.append(jnp.concatenate(outs, axis=-1))
    return results


def _hgrn_kernel(qf_ref, ff_ref, if_ref, qb_ref, fb_ref, ib_ref, cff_ref, cfb_ref, ci_ref,
                 lbf_ref, lbb_ref, of_ref, ob_ref, st_ref, *, n_sub, n_ctx_sub):
    n = pl.program_id(1)
    c = HG_CHUNK
    row = lax.broadcasted_iota(jnp.int32, (c, c), 0)
    col = lax.broadcasted_iota(jnp.int32, (c, c), 1)
    keep_f = col <= row
    keep_b = col >= row
    tri_f = keep_f.astype(F32).astype(BF16)
    tri_b = keep_b.astype(F32).astype(BF16)
    lbf = lbf_ref[...]
    lbb = lbb_ref[...]

    @pl.when(n == 0)
    def _():
        st_ref[...] = jnp.zeros_like(st_ref)

        def body(i, carry):
            r0 = pl.multiple_of(i * c, c)
            r1 = pl.multiple_of((n_ctx_sub - 1 - i) * c, c)
            gf = _hgrn_gates(None, cff_ref[pl.ds(r0, c), :], ci_ref[pl.ds(r0, c), :], lbf, tri_f, c - 1)
            gb = _hgrn_gates(None, cfb_ref[pl.ds(r1, c), :], ci_ref[pl.ds(r1, c), :], lbb, tri_b, 0)
            _hgrn_chunks([(gf, keep_f), (gb, keep_b)], st_ref)
            return carry

        lax.fori_loop(0, n_ctx_sub, body, 0)

    @pl.when(n > 0)
    def _():
        def body(i, carry):
            r0 = pl.multiple_of(i * c, c)
            r1 = pl.multiple_of((n_sub - 1 - i) * c, c)
            gf = _hgrn_gates(qf_ref[pl.ds(r0, c), :], ff_ref[pl.ds(r0, c), :], if_ref[pl.ds(r0, c), :],
                             lbf, tri_f, c - 1)
            gb = _hgrn_gates(qb_ref[pl.ds(r1, c), :], fb_ref[pl.ds(r1, c), :], ib_ref[pl.ds(r1, c), :],
                             lbb, tri_b, 0)
            o_f, o_b = _hgrn_chunks([(gf, keep_f), (gb, keep_b)], st_ref)
            of_ref[pl.ds(r0, c), :] = o_f
            ob_ref[pl.ds(r1, c), :] = o_b
            return carry

        lax.fori_loop(0, n_sub, body, 0, unroll=True)


def _hgrn(p, pc, lb_fwd, lb_bwd):
    _, b, s, w = p.shape
    ctx_len = pc.shape[2]
    tb = min(256, s)
    nb = s // tb
    fwd = lambda bi, n: jnp.maximum(n - 1, 0)
    bwd = lambda bi, n: nb - 1 - jnp.maximum(n - 1, 0)

    def sec(section, blk):
        return pl.BlockSpec((None, None, tb, w), lambda bi, n: (section, bi, blk(bi, n), 0))

    def csec(section):
        return pl.BlockSpec((None, None, ctx_len, w), lambda bi, n: (section, bi, 0, 0))

    vec = pl.BlockSpec((1, w), lambda bi, n: (0, 0))
    kern = functools.partial(_hgrn_kernel, n_sub=tb // HG_CHUNK, n_ctx_sub=ctx_len // HG_CHUNK)
    return pl.pallas_call(
        kern,
        grid=(b, nb + 1),
        in_specs=[sec(SEC_Q, fwd), sec(SEC_FF, fwd), sec(SEC_I, fwd),
                  sec(SEC_Q, bwd), sec(SEC_FB, bwd), sec(SEC_I, bwd),
                  csec(SEC_FF), csec(SEC_FB), csec(SEC_I), vec, vec],
        out_specs=[pl.BlockSpec((None, tb, w), lambda bi, n: (bi, fwd(bi, n), 0)),
                   pl.BlockSpec((None, tb, w), lambda bi, n: (bi, bwd(bi, n), 0))],
        out_shape=[jax.ShapeDtypeStruct((b, s, w), F32), jax.ShapeDtypeStruct((b, s, w), F32)],
        scratch_shapes=[pltpu.VMEM((2, HG_HEADS, HG_DK, HG_DK), F32)],
        compiler_params=_cparams(("arbitrary", "arbitrary")),
        name="hgrn",
    )(p, p, p, p, p, p, pc, pc, pc, lb_fwd.reshape(1, w), lb_bwd.reshape(1, w))


NA_ROWS_PER_STEP = 32
NA_PREP_ROWS = 512
NA_KEY_TILE = 128
NA_SPAN = (NA_WIN_R + 2) * GRID_W


def _rope(t, cos, sin_signed, first_half):
    w = t.shape[-1]
    partner = jnp.where(first_half, pltpu.roll(t, w - 16, 1), pltpu.roll(t, 16, 1))
    return t * cos + partner * sin_signed


def _fold_lanes(op, *arrays):
    tiles = [a[:, c:c + 128] for a in arrays for c in range(0, a.shape[-1], 128)]
    acc = tiles[0]
    for t in tiles[1:]:
        acc = op(acc, t)
    return acc


def _rope_tables(rowtab_ref, coltab_ref, row0, n_rows, row_lane):
    out = []
    for i in range(2):
        rt = rowtab_ref[i, pl.ds(row0, n_rows), :]
        by_row = jnp.concatenate([jnp.broadcast_to(rt[r:r + 1, :], (GRID_W, rt.shape[1])) for r in range(n_rows)],
                                 axis=0)
        by_col = jnp.concatenate([coltab_ref[i]] * n_rows, axis=0)
        out.append(jnp.where(row_lane, by_row, by_col))
    return out


def _natten_kernel(q_ref, k_ref, v_ref, kc_ref, vc_ref, rowtab_ref, coltab_ref, t2_ref, o_ref,
                   kt_s, v_s, kc_s, vc_s, bias_s, *, rows):
    rblk = pl.program_id(2)
    hd = NA_HD
    lane = lax.broadcasted_iota(jnp.int32, (1, 2 * hd), 1)
    first_half = (lane % 32) < 16
    row_lane = (lane % hd) < hd // 2
    scale = NA_HD ** -0.5

    def values_and_ones(v_pair, h):
        vh = v_pair if h == 0 else pltpu.roll(v_pair, hd, 1)
        return jnp.where(lane < hd, vh, jnp.where(lane == hd, 1.0, 0.0)).astype(BF16)

    @pl.when(rblk == 0)
    def _():
        kc = kc_ref[...].astype(BF16)
        qi = lax.broadcasted_iota(jnp.int32, (GRID_W, GRID_W), 0)
        ki = lax.broadcasted_iota(jnp.int32, (GRID_W, GRID_W), 1)
        cstart = jnp.clip(qi - NA_WIN_C // 2, 0, GRID_W - NA_WIN_C)
        in_win = (ki >= cstart) & (ki < cstart + NA_WIN_C)
        masked = jnp.full((GRID_W, GRID_W), NEG_INF, F32)
        s_len = k_ref.shape[0]
        for h in range(2):
            sl = slice(h * hd, (h + 1) * hd)
            kc_s[h] = kc[:, sl]
            vc_s[h] = values_and_ones(vc_ref[...], h)
            kt_s[h, s_len // NA_KEY_TILE] = jnp.zeros((hd, NA_KEY_TILE), BF16)
            v_s[h, s_len:s_len + NA_KEY_TILE, :] = jnp.zeros((NA_KEY_TILE, 2 * hd), BF16)
            tiles = [jnp.where(in_win, t2_ref[h, dr], NEG_INF) for dr in range(2 * NA_WIN_R - 1)]
            for bidx in range(NA_WIN_R + 1):
                v, par = (bidx, 0) if bidx < NA_WIN_R else (NA_WIN_R // 2, 1)
                for piece in range(NA_SPAN // GRID_W):
                    j = piece - par
                    tile = tiles[NA_WIN_R - 1 - v + j] if 0 <= j < NA_WIN_R else masked
                    bias_s[h, bidx, :, piece * GRID_W:(piece + 1) * GRID_W] = tile

        eye = (lax.broadcasted_iota(jnp.int32, (2 * hd, 2 * hd), 0)
               == lax.broadcasted_iota(jnp.int32, (2 * hd, 2 * hd), 1)).astype(F32).astype(BF16)

        def prep(i, carry):
            r0 = pl.multiple_of(i * NA_PREP_ROWS, NA_PREP_ROWS)
            rws = pl.ds(r0, NA_PREP_ROWS)
            cos, sin = _rope_tables(rowtab_ref, coltab_ref, i * (NA_PREP_ROWS // GRID_W), NA_PREP_ROWS // GRID_W,
                                    row_lane)
            kr = _rope(k_ref[rws, :], cos, sin, first_half)
            krt = _dot_nt(eye, kr.astype(BF16)).astype(BF16)
            vv = v_ref[rws, :]
            for h in range(2):
                sl = slice(h * hd, (h + 1) * hd)
                for c in range(NA_PREP_ROWS // NA_KEY_TILE):
                    kt_s[h, i * (NA_PREP_ROWS // NA_KEY_TILE) + c] = krt[sl, c * NA_KEY_TILE:(c + 1) * NA_KEY_TILE]
                v_s[h, rws, :] = values_and_ones(vv, h)
            return carry

        lax.fori_loop(0, s_len // NA_PREP_ROWS, prep, 0, unroll=4)

    tq = NA_ROWS_PER_STEP * GRID_W
    q = q_ref[...] * scale
    cos, sin = _rope_tables(rowtab_ref, coltab_ref, rblk * NA_ROWS_PER_STEP, NA_ROWS_PER_STEP, row_lane)
    qr = _rope(q, cos, sin, first_half)
    qb = q.astype(BF16)
    qrb = qr.astype(BF16)
    rws = [slice(rr * GRID_W, (rr + 1) * GRID_W) for rr in range(NA_ROWS_PER_STEP)]
    tile0, bidx = [], []
    for rr in range(NA_ROWS_PER_STEP):
        r = rblk * NA_ROWS_PER_STEP + rr
        rs = jnp.clip(r - NA_WIN_R // 2, 0, rows - NA_WIN_R)
        tile0.append(lax.shift_right_logical(rs, 1))
        bidx.append(jnp.where((rs & 1) == 1, NA_WIN_R, r - rs))

    def scores(h):
        sl = slice(h * hd, (h + 1) * hd)
        qrb_h = qrb[:, sl]
        s_ctx_all = _dot_nt(qb[:, sl], kc_s[h])
        s_win = []
        for rr in range(NA_ROWS_PER_STEP):
            kt = kt_s[h, pl.ds(tile0[rr], NA_SPAN // NA_KEY_TILE)]
            kt = jnp.concatenate([kt[c] for c in range(NA_SPAN // NA_KEY_TILE)], axis=-1)
            s_win.append(_dot(qrb_h[rws[rr]], kt))
        return s_win, s_ctx_all

    def softmax(h, s_win, s_ctx_all):
        e_win, e_ctx = [], []
        for rr in range(NA_ROWS_PER_STEP):
            sw = s_win[rr] + bias_s[h, bidx[rr]]
            sc = s_ctx_all[rws[rr]]
            m = jnp.max(_fold_lanes(jnp.maximum, sw, sc), axis=-1, keepdims=True)
            e_win.append(jnp.exp(sw - m).astype(BF16))
            e_ctx.append(jnp.exp(sc - m).astype(BF16))
        return e_win, e_ctx

    def values(h, e_win, e_ctx):
        o_win = []
        for rr in range(NA_ROWS_PER_STEP):
            k0 = pl.multiple_of(tile0[rr] * NA_KEY_TILE, NA_KEY_TILE)
            o_win.append(_dot(e_win[rr], v_s[h, pl.ds(k0, NA_SPAN), :]))
        o = jnp.concatenate(o_win, axis=0) + _dot(jnp.concatenate(e_ctx, axis=0), vc_s[h])
        return o[:, :hd] * (1.0 / o[:, hd:hd + 1])

    s0 = scores(0)
    s1 = scores(1)
    p0 = softmax(0, *s0)
    o0 = values(0, *p0)
    p1 = softmax(1, *s1)
    o1 = values(1, *p1)
    o_ref[...] = jnp.concatenate([o0, o1], axis=-1)


def _na_tables(rpb, s):
    half = NA_HD // 2
    inv = jnp.power(ROPE_THETA, -jnp.arange(0, half, 2, dtype=F32) / half)

    def tables(n):
        ang = jnp.arange(n, dtype=F32)[:, None] * inv[None, :]
        reps = 2 * NA_HD // half
        return jnp.stack([jnp.tile(jnp.cos(ang), (1, 2 * reps)),
                          jnp.tile(jnp.concatenate([-jnp.sin(ang), jnp.sin(ang)], axis=-1), (1, reps))])

    rowtab, coltab = tables(s // GRID_W), tables(GRID_W)

    pad = GRID_W - NA_WIN_C
    rp = jnp.pad(rpb.astype(F32), ((0, 0), (0, 0), (pad, pad)), mode="edge")
    t2 = jnp.stack([rp[:, :, GRID_W - 1 - qc:2 * GRID_W - 1 - qc] for qc in range(GRID_W)], axis=2)
    return rowtab, coltab, t2


def _natten(p, pc, rowtab, coltab, t2):
    _, b, s, w = p.shape
    ctx_len = pc.shape[2]
    rows = s // GRID_W
    assert rows >= NA_WIN_R and rows % NA_ROWS_PER_STEP == 0
    tq = NA_ROWS_PER_STEP * GRID_W
    hw = 2 * NA_HD
    nhp = w // hw
    kern = functools.partial(_natten_kernel, rows=rows)
    return pl.pallas_call(
        kern,
        grid=(b, nhp, rows // NA_ROWS_PER_STEP),
        in_specs=[pl.BlockSpec((None, None, tq, hw), lambda bi, hp, r: (SEC_NQ, bi, r, hp)),
                  pl.BlockSpec((None, None, s, hw), lambda bi, hp, r: (SEC_NK, bi, 0, hp)),
                  pl.BlockSpec((None, None, s, hw), lambda bi, hp, r: (SEC_NV, bi, 0, hp)),
                  pl.BlockSpec((None, None, ctx_len, hw), lambda bi, hp, r: (SEC_NK, bi, 0, hp)),
                  pl.BlockSpec((None, None, ctx_len, hw), lambda bi, hp, r: (SEC_NV, bi, 0, hp)),
                  pl.BlockSpec((2, rows, hw), lambda bi, hp, r: (0, 0, 0)),
                  pl.BlockSpec((2, GRID_W, hw), lambda bi, hp, r: (0, 0, 0)),
                  pl.BlockSpec((2, 2 * NA_WIN_R - 1, GRID_W, GRID_W), lambda bi, hp, r: (hp, 0, 0, 0))],
        out_specs=pl.BlockSpec((None, tq, hw), lambda bi, hp, r: (bi, r, hp)),
        out_shape=jax.ShapeDtypeStruct((b, s, w), F32),
        scratch_shapes=[pltpu.VMEM((2, s // NA_KEY_TILE + 1, NA_HD, NA_KEY_TILE), BF16),
                        pltpu.VMEM((2, s + NA_KEY_TILE, hw), BF16),
                        pltpu.VMEM((2, ctx_len, NA_HD), BF16), pltpu.VMEM((2, ctx_len, hw), BF16),
                        pltpu.VMEM((2, NA_WIN_R + 1, GRID_W, NA_SPAN), F32)],
        compiler_params=_cparams(("arbitrary", "arbitrary", "arbitrary")),
        name="natten",
    )(p, p, p, pc, pc, rowtab, coltab, t2)


def _route(logits_t, rbias):
    e, t = logits_t.shape
    gsz = e // N_GROUPS
    scores = jax.nn.sigmoid(logits_t)
    sel = scores + rbias
    neg = -jnp.inf
    sub = lax.broadcasted_iota(jnp.int32, (gsz, t), 0).astype(F32)
    gscore = []
    for g in range(N_GROUPS):
        grp = sel[g * gsz:(g + 1) * gsz, :]
        m1 = jnp.max(grp, axis=0, keepdims=True)
        first = jnp.min(jnp.where(grp == m1, sub, float(gsz)), axis=0, keepdims=True)
        m2 = jnp.max(jnp.where(sub == first, neg, grp), axis=0, keepdims=True)
        gscore.append(m1 + m2)
    masked = []
    for g in range(N_GROUPS):
        rank = jnp.zeros((1, t), F32)
        for g2 in range(N_GROUPS):
            if g2 == g:
                continue
            if g2 < g:
                ahead = gscore[g2] >= gscore[g]
            else:
                ahead = gscore[g2] > gscore[g]
            rank = rank + jnp.where(ahead, 1.0, 0.0)
        masked.append(jnp.where(rank < TOPK_GROUPS, sel[g * gsz:(g + 1) * gsz, :], neg))
    work = jnp.concatenate(masked, axis=0)
    eidx = lax.broadcasted_iota(jnp.int32, (e, t), 0).astype(F32)
    idxs, ws = [], []
    chosen = jnp.zeros((e, t), F32)
    for _ in range(TOP_K):
        m = jnp.max(work, axis=0, keepdims=True)
        first = jnp.min(jnp.where(work == m, eidx, float(e)), axis=0, keepdims=True)
        pick = eidx == first
        idxs.append(first)
        ws.append(jnp.sum(jnp.where(pick, scores, 0.0), axis=0, keepdims=True))
        chosen = jnp.where(pick, 1.0, chosen)
        work = jnp.where(pick, neg, work)
    w = jnp.concatenate(ws, axis=0)
    w = w / jnp.sum(w, axis=0, keepdims=True) * ROUTED_SCALE
    return jnp.concatenate(idxs, axis=0).astype(jnp.int32), w, chosen


MERGE_TOK = 512
MERGE_SUB = 256


def _merge_kernel(of_ref, ob_ref, og_ref, yna_ref, ga_ref, gb_ref, x_ref, g1_ref, sh2_ref, sc2_ref,
                  hgg_ref, ln1g_ref, ln1b_ref, wa_ref, wb_ref, wo_ref, wr_ref, rb_ref,
                  x1_ref, h2_ref, topi_ref, topw_ref, cnt_ref, *, alpha):
    tm = x_ref.shape[0]
    subs = [slice(i * MERGE_SUB, (i + 1) * MERGE_SUB) for i in range(tm // MERGE_SUB)]

    def branches(rows):
        o = of_ref[rows, :] + ob_ref[rows, :]
        parts = []
        for h in range(HG_HEADS):
            oh = o[:, h * HG_DK:(h + 1) * HG_DK]
            parts.append(oh * lax.rsqrt(jnp.mean(oh * oh, axis=-1, keepdims=True) + LN_EPS))
        y_hg = jnp.concatenate(parts, axis=-1) * hgg_ref[...] * _silu(og_ref[rows, :])
        return _dot(y_hg.astype(BF16), wa_ref[...]), _dot(yna_ref[rows, :].astype(BF16), wb_ref[...])

    def out_proj(rows, ya, yb):
        t = jax.nn.sigmoid(ga_ref[rows, :]) * ya + jax.nn.sigmoid(gb_ref[rows, :]) * yb
        return _dot(t.astype(BF16), wo_ref[...])

    def norms_router(rows, i, y):
        x1 = _normalize(alpha * x_ref[rows, :] + g1_ref[...] * y) * ln1g_ref[...] + ln1b_ref[...]
        x1_ref[rows, :] = x1
        h2 = _normalize(x1) * (1.0 + sc2_ref[...]) + sh2_ref[...]
        h2_ref[rows, :] = _pack_words(h2)
        hh, hm, hl = _split3(h2)
        wh, wm, wl = _split3(wr_ref[...])
        return (_dot_nt(wh, hh) + _dot_nt(wh, hm) + _dot_nt(wm, hh)
                + _dot_nt(wh, hl) + _dot_nt(wl, hh) + _dot_nt(wm, hm))

    ab = [branches(rows) for rows in subs]
    ys = [out_proj(rows, *ab[i]) for i, rows in enumerate(subs)]
    logits = [norms_router(rows, i, ys[i]) for i, rows in enumerate(subs)]

    @pl.when((pl.program_id(0) == 0) & (pl.program_id(1) == 0))
    def _():
        cnt_ref[...] = jnp.zeros_like(cnt_ref)

    for i, rows in enumerate(subs):
        topi, topw, chosen = _route(logits[i], rb_ref[...])
        topi_ref[:, rows] = topi
        topw_ref[:, rows] = topw
        cnt_ref[...] += jnp.sum(chosen, axis=1, keepdims=True)


def _merge(o_f, o_b, p, y_na, x, g1, sh2, sc2, hg_norm_g, ln1_g, ln1_b, w_a, w_b, w_o, w_router_t, router_bias,
           alpha):
    b, s, d = x.shape
    tm = min(MERGE_TOK, s)
    e = w_router_t.shape[0]
    tok = lambda bi, i: (bi, i, 0)
    blk = pl.BlockSpec((None, tm, d), tok)

    def sec(section):
        return pl.BlockSpec((None, None, tm, d), lambda bi, i: (section, bi, i, 0))

    mod = pl.BlockSpec((None, 1, d), lambda bi, i: (bi, 0, 0))
    vec = pl.BlockSpec((1, d), lambda bi, i: (0, 0))
    mat = pl.BlockSpec((d, d), lambda bi, i: (0, 0), pipeline_mode=pl.Buffered(1))
    return pl.pallas_call(
        functools.partial(_merge_kernel, alpha=alpha),
        grid=(b, s // tm),
        in_specs=[blk, blk, sec(SEC_OG), blk, sec(SEC_GA), sec(SEC_GB), blk, mod, mod, mod,
                  vec, vec, vec, mat, mat, mat,
                  pl.BlockSpec((e, d), lambda bi, i: (0, 0)),
                  pl.BlockSpec((e, 1), lambda bi, i: (0, 0))],
        out_specs=[blk,
                   pl.BlockSpec((tm, d // 2), lambda bi, i: (bi * (s // tm) + i, 0)),
                   pl.BlockSpec((None, TOP_K, tm), lambda bi, i: (bi, 0, i)),
                   pl.BlockSpec((None, TOP_K, tm), lambda bi, i: (bi, 0, i)),
                   pl.BlockSpec((e, 128), lambda bi, i: (0, 0))],
        out_shape=[jax.ShapeDtypeStruct((b, s, d), F32),
                   jax.ShapeDtypeStruct((b * s, d // 2), U32),
                   jax.ShapeDtypeStruct((b, TOP_K, s), jnp.int32), jax.ShapeDtypeStruct((b, TOP_K, s), F32),
                   jax.ShapeDtypeStruct((e, 128), F32)],
        compiler_params=_cparams(("arbitrary", "arbitrary")),
        name="merge",
    )(o_f, o_b, p, y_na, p, p, x, g1, sh2, sc2, hg_norm_g.reshape(1, d), ln1_g.reshape(1, d),
      ln1_b.reshape(1, d), w_a, w_b, w_o, w_router_t, router_bias.reshape(e, 1))


MOE_TILE = 512
MOE_TOK = 512


def _plan_kernel(topi_ref, off_ref, dest_ref, carry_ref):
    @pl.when(pl.program_id(0) == 0)
    def _():
        carry_ref[...] = jnp.zeros_like(carry_ref)

    topi = topi_ref[...]
    tok = topi.shape[1]
    eidx = lax.broadcasted_iota(jnp.int32, (N_EXPERTS, tok), 0)
    hits = [eidx == topi[k:k + 1, :] for k in range(TOP_K)]
    m = jnp.zeros((N_EXPERTS, tok), F32)
    for hit in hits:
        m = jnp.where(hit, 1.0, m)
    before = (lax.broadcasted_iota(jnp.int32, (tok, tok), 0)
              < lax.broadcasted_iota(jnp.int32, (tok, tok), 1)).astype(F32).astype(BF16)
    row = off_ref[...] + carry_ref[...] + _dot(m.astype(BF16), before)
    dest = [jnp.sum(jnp.where(hit, row, 0.0), axis=0, keepdims=True) for hit in hits]
    dest_ref[...] = jnp.concatenate(dest, axis=0).astype(jnp.int32)
    carry_ref[...] += jnp.sum(m, axis=1, keepdims=True)


def _plan(topi, seg_off):
    b, k, s = topi.shape
    per_b = s // MOE_TOK
    blk = pl.BlockSpec((None, k, MOE_TOK), lambda i: (i // per_b, 0, i % per_b))
    return pl.pallas_call(
        _plan_kernel,
        grid=(b * per_b,),
        in_specs=[blk, pl.BlockSpec((N_EXPERTS, 1), lambda i: (0, 0))],
        out_specs=blk,
        out_shape=jax.ShapeDtypeStruct((b, k, s), jnp.int32),
        scratch_shapes=[pltpu.VMEM((N_EXPERTS, 1), F32)],
        compiler_params=_cparams(("arbitrary",)),
        name="plan",
    )(topi, seg_off.astype(F32).reshape(N_EXPERTS, 1))


SC_WINDOW = 128


def _sc_workers():
    info = plsc.get_sparse_core_info()
    return info.num_cores, info.num_cores * info.num_subcores


def _scatter_rows(src, idx, zero_idx, n_rows):
    t, w = src.shape
    m, mz = idx.shape[0], zero_idx.shape[0]
    n_cores, n_workers = _sc_workers()
    per_worker, per_worker_z = m // n_workers, mz // n_workers
    assert m % t == 0 and t % SC_WINDOW == 0
    assert per_worker * n_workers == m and per_worker % SC_WINDOW == 0
    assert per_worker_z * n_workers == mz and per_worker_z % SC_WINDOW == 0
    mesh = plsc.VectorSubcoreMesh(core_axis_name="core", subcore_axis_name="subcore")

    @functools.partial(
        pl.kernel, mesh=mesh, out_type=jax.ShapeDtypeStruct((n_rows, w), src.dtype),
        scratch_types=[pltpu.VMEM((SC_WINDOW,), jnp.int32), pltpu.VMEM((SC_WINDOW, w), src.dtype),
                       pltpu.SemaphoreType.DMA])
    def scatter(src_hbm, idx_hbm, zeros_hbm, zero_idx_hbm, out_hbm, idx_v, rows_v, sem):
        worker = lax.axis_index("subcore") * n_cores + lax.axis_index("core")

        @pl.loop(0, per_worker // SC_WINDOW)
        def _(step):
            base = pl.multiple_of(worker * per_worker + step * SC_WINDOW, SC_WINDOW)
            first = pl.multiple_of(lax.rem(base, t), SC_WINDOW)
            pltpu.sync_copy(idx_hbm.at[pl.ds(base, SC_WINDOW)], idx_v)
            pltpu.sync_copy(src_hbm.at[pl.ds(first, SC_WINDOW)], rows_v)
            pltpu.async_copy(rows_v, out_hbm.at[idx_v], sem).wait()

        pltpu.sync_copy(zeros_hbm, rows_v)

        @pl.loop(0, per_worker_z // SC_WINDOW)
        def _(step):
            base = pl.multiple_of(worker * per_worker_z + step * SC_WINDOW, SC_WINDOW)
            pltpu.sync_copy(zero_idx_hbm.at[pl.ds(base, SC_WINDOW)], idx_v)
            pltpu.async_copy(rows_v, out_hbm.at[idx_v], sem).wait()

    return scatter(src, idx, jnp.zeros((SC_WINDOW, w), src.dtype), zero_idx)


EXPERT_RING = 3


def _experts_kernel(te_ref, tb_ref, nt_ref, xs_ref, wg_ref, wu_ref, wd_ref, ys_ref, xbuf, sem):
    i = pl.program_id(0)
    n_tiles = nt_ref[0]

    def tile_copy(j):
        slot = lax.rem(j, EXPERT_RING)
        row0 = pl.multiple_of(tb_ref[j] * MOE_TILE, MOE_TILE)
        return pltpu.make_async_copy(xs_ref.at[pl.ds(row0, MOE_TILE), :], xbuf.at[slot], sem.at[slot])

    @pl.when(i == 0)
    def _():
        for j in range(EXPERT_RING - 1):
            @pl.when(j < n_tiles)
            def _():
                tile_copy(j).start()

    ahead = i + (EXPERT_RING - 1)

    @pl.when(ahead < n_tiles)
    def _():
        tile_copy(ahead).start()

    @pl.when(i < n_tiles)
    def _():
        tile_copy(i).wait()
        x = _unpack_words(xbuf[lax.rem(i, EXPERT_RING)]).astype(BF16)
        act = _silu(_dot(x, wg_ref[...].astype(BF16))) * _dot(x, wu_ref[...].astype(BF16))
        ys_ref[...] = _pack_words(_dot(act.astype(BF16), wd_ref[...].astype(BF16)))


def _experts(xs, tile_expert, tile_block, n_tiles, wg, wu, wd):
    d, f = wg.shape[1], wg.shape[2]
    grid_spec = pltpu.PrefetchScalarGridSpec(
        num_scalar_prefetch=3,
        grid=(xs.shape[0] // MOE_TILE,),
        in_specs=[pl.BlockSpec(memory_space=pl.ANY),
                  pl.BlockSpec((None, d, f), lambda i, te, tb, nt: (te[i], 0, 0)),
                  pl.BlockSpec((None, d, f), lambda i, te, tb, nt: (te[i], 0, 0)),
                  pl.BlockSpec((None, f, d), lambda i, te, tb, nt: (te[i], 0, 0))],
        out_specs=pl.BlockSpec((MOE_TILE, d // 2), lambda i, te, tb, nt: (tb[i], 0)),
        scratch_shapes=[pltpu.VMEM((EXPERT_RING, MOE_TILE, d // 2), U32), pltpu.SemaphoreType.DMA((EXPERT_RING,))],
    )
    return pl.pallas_call(
        _experts_kernel,
        grid_spec=grid_spec,
        out_shape=jax.ShapeDtypeStruct(xs.shape, U32),
        compiler_params=_cparams(("arbitrary",)),
        name="experts",
    )(tile_expert, tile_block, n_tiles, xs, wg, wu, wd)


def _gather_rows(table, idx):
    m = idx.shape[0]
    w = table.shape[1]
    n_cores, n_workers = _sc_workers()
    per_worker = m // n_workers
    assert per_worker * n_workers == m and per_worker % SC_WINDOW == 0
    mesh = plsc.VectorSubcoreMesh(core_axis_name="core", subcore_axis_name="subcore")

    @functools.partial(
        pl.kernel, mesh=mesh, out_type=jax.ShapeDtypeStruct((m, w), table.dtype),
        scratch_types=[pltpu.VMEM((SC_WINDOW,), jnp.int32), pltpu.VMEM((SC_WINDOW, w), table.dtype),
                       pltpu.SemaphoreType.DMA])
    def gather(table_hbm, idx_hbm, out_hbm, idx_v, rows_v, sem):
        worker = lax.axis_index("subcore") * n_cores + lax.axis_index("core")

        @pl.loop(0, per_worker // SC_WINDOW)
        def _(step):
            base = pl.multiple_of(worker * per_worker + step * SC_WINDOW, SC_WINDOW)
            pltpu.sync_copy(idx_hbm.at[pl.ds(base, SC_WINDOW)], idx_v)
            pltpu.async_copy(table_hbm.at[idx_v], rows_v, sem).wait()
            pltpu.sync_copy(rows_v, out_hbm.at[pl.ds(base, SC_WINDOW)])

    return gather(table, idx)


def _combine_kernel(rows_ref, topw_ref, h_ref, x1_ref, g2_ref, sg_ref, su_ref, sd_ref, ln2g_ref, ln2b_ref,
                    o_ref, *, alpha):
    h = _unpack_words(h_ref[...]).astype(BF16)
    act = _silu(_dot(h, sg_ref[...])) * _dot(h, su_ref[...])
    y = _dot(act.astype(BF16), sd_ref[...])
    w = topw_ref[...].T
    for k in range(TOP_K):
        y = y + w[:, k:k + 1] * _unpack_words(rows_ref[k])
    o_ref[...] = _normalize(alpha * x1_ref[...] + g2_ref[...] * y) * ln2g_ref[...] + ln2b_ref[...]


def _combine(gathered, topw, h2p, x1, g2, sg, su, sd, ln2_g, ln2_b, alpha):
    t, d = x1.shape
    b, k, s = topw.shape
    per_b = s // MOE_TOK
    fs = sg.shape[1]
    rows = pl.BlockSpec((MOE_TOK, d), lambda i: (i, 0))
    packed = pl.BlockSpec((MOE_TOK, d // 2), lambda i: (i, 0))
    vec = pl.BlockSpec((1, d), lambda i: (0, 0))
    return pl.pallas_call(
        functools.partial(_combine_kernel, alpha=alpha),
        grid=(t // MOE_TOK,),
        in_specs=[pl.BlockSpec((k, MOE_TOK, d // 2), lambda i: (0, i, 0)),
                  pl.BlockSpec((None, k, MOE_TOK), lambda i: (i // per_b, 0, i % per_b)),
                  packed, rows,
                  pl.BlockSpec((None, 1, d), lambda i: (i // per_b, 0, 0)),
                  pl.BlockSpec((d, fs), lambda i: (0, 0)),
                  pl.BlockSpec((d, fs), lambda i: (0, 0)),
                  pl.BlockSpec((fs, d), lambda i: (0, 0)),
                  vec, vec],
        out_specs=rows,
        out_shape=jax.ShapeDtypeStruct((t, d), F32),
        compiler_params=_cparams(("arbitrary",)),
        name="combine",
    )(gathered, topw, h2p, x1, g2, sg, su, sd, ln2_g.reshape(1, d), ln2_b.reshape(1, d))


def _moe(h2p, topi, topw, cnt, x1, g2, wg, wu, wd, sg, su, sd, ln2_g, ln2_b, alpha):
    b, s, d = x1.shape
    t = b * s
    cnt = cnt[:, 0].astype(jnp.int32)
    tiles_e = (cnt + (MOE_TILE - 1)) // MOE_TILE
    tiles_cum = jnp.cumsum(tiles_e)
    seg_off = (tiles_cum - tiles_e) * MOE_TILE
    n_tiles_max = t * TOP_K // MOE_TILE + N_EXPERTS
    tile_block = jnp.minimum(jnp.arange(n_tiles_max, dtype=jnp.int32), tiles_cum[-1] - 1)
    tile_expert = jnp.sum((tiles_cum[None, :] <= tile_block[:, None]).astype(jnp.int32), axis=1)
    n_tiles = tiles_cum[-1:].astype(jnp.int32)

    dest = jnp.transpose(_plan(topi, seg_off), (1, 0, 2)).reshape(TOP_K * t)
    j = jnp.arange(MOE_TILE, dtype=jnp.int32)[None, :]
    n_pad = (tiles_e * MOE_TILE - cnt)[:, None]
    spare = (n_tiles_max - 1) * MOE_TILE + j
    zero_idx = jnp.where(j < n_pad, (seg_off + cnt)[:, None] + j, spare).reshape(N_EXPERTS * MOE_TILE)
    xs = _scatter_rows(h2p, dest, zero_idx.astype(jnp.int32), n_tiles_max * MOE_TILE)
    ys = _experts(xs, tile_expert, tile_block, n_tiles, wg, wu, wd)
    gathered = _gather_rows(ys, dest)
    out = _combine(gathered.reshape(TOP_K, t, d // 2), topw, h2p, x1.reshape(t, d), g2, sg, su, sd,
                   ln2_g, ln2_b, alpha)
    return out.reshape(b, s, d)


def kernel(x, c, ctx, c_ctx, w_ada, b_ada, w_in, hg_lb_fwd, hg_lb_bwd, hg_norm_g, na_rpb, w_branch_a, w_branch_b, w_out, ln1_g, ln1_b, w_router, router_bias, w_e_gate, w_e_up, w_e_down, w_sh_gate, w_sh_up, w_sh_down, ln2_g, ln2_b):
    depth = w_ada.shape[0]
    assert depth == 1, "single-layer block"
    b, s, d = x.shape
    alpha = (2.0 * depth) ** 0.25
    l = 0
    lb_fwd = jnp.cumsum(jax.nn.softmax(hg_lb_fwd.astype(F32), axis=0), axis=0)[l]
    lb_bwd = jnp.cumsum(jax.nn.softmax(hg_lb_bwd.astype(F32), axis=0), axis=0)[l]

    cond_rows = jnp.concatenate([c, c_ctx[None, :], jnp.zeros((8 - b - 1, d), F32)], axis=0)
    mod = _ada(cond_rows, w_ada[l], b_ada[l])
    sh1, sc1, g1, sh2, sc2, g2 = [m[:b, None, :] for m in jnp.split(mod, 6, axis=-1)]
    csh1, csc1 = [jnp.broadcast_to(m[b:b + 1, None, :], (b, 1, d)) for m in jnp.split(mod, 6, axis=-1)[:2]]

    w_in_b = w_in[l].astype(BF16)
    p = _inproj(x, sh1, sc1, w_in_b)
    pc = _inproj(ctx, csh1, csc1, w_in_b)

    o_f, o_b = _hgrn(p, pc, lb_fwd, lb_bwd)
    y_na = _natten(p, pc, *_na_tables(na_rpb[l], s))

    x1, h2, topi, topw, cnt = _merge(o_f, o_b, p, y_na, x, g1, sh2, sc2, hg_norm_g[l], ln1_g[l], ln1_b[l],
                                     w_branch_a[l].astype(BF16), w_branch_b[l].astype(BF16),
                                     w_out[l].astype(BF16), w_router[l].T, router_bias[l], alpha)

    return _moe(h2, topi, topw, cnt, x1, g2,
                w_e_gate[l], w_e_up[l], w_e_down[l],
                w_sh_gate[l].astype(BF16), w_sh_up[l].astype(BF16), w_sh_down[l].astype(BF16),
                ln2_g[l], ln2_b[l], alpha)
```

```python
import functools

import numpy as np
import jax
import jax.numpy as jnp
from jax import lax
from jax.experimental import pallas as pl
from jax.experimental.pallas import tpu as pltpu
from jax.experimental.pallas import tpu_sc as plsc

F32 = jnp.float32
BF16 = jnp.bfloat16

D_MODEL = 1024
GRID_W = 64
HG_HEADS = 8
HG_DK = 128
HG_CHUNK = 64
NA_HEADS = 16
NA_HD = 64
NA_WIN_R = 8
NA_WIN_C = 16
ROPE_THETA = 10000.0
NEG_INF = -1e30
N_EXPERTS = 64
EXPERT_DIM = 256
TOP_K = 8
N_GROUPS = 8
TOPK_GROUPS = 4
ROUTED_SCALE = 2.5
LN_EPS = 1e-6
N_SECTIONS = 10
SEC_Q, SEC_FF, SEC_FB, SEC_I, SEC_OG, SEC_NQ, SEC_NK, SEC_NV, SEC_GA, SEC_GB = range(10)

VMEM_LIMIT = 56 * 1024 * 1024


def _cparams(sem):
    return pltpu.CompilerParams(dimension_semantics=sem, vmem_limit_bytes=VMEM_LIMIT)


def _normalize(x):
    mu = jnp.mean(x, axis=-1, keepdims=True)
    xc = x - mu
    var = jnp.mean(xc * xc, axis=-1, keepdims=True)
    return xc * lax.rsqrt(var + LN_EPS)


def _silu(x):
    return x * jax.nn.sigmoid(x)


def _dot(a, b):
    return jnp.dot(a, b, preferred_element_type=F32)


def _dot_nt(a, b):
    return lax.dot_general(a, b, (((1,), (1,)), ((), ())), preferred_element_type=F32)


def _dot_tn(a, b):
    return lax.dot_general(a, b, (((0,), (0,)), ((), ())), preferred_element_type=F32)


U32 = jnp.uint32


def _pack_words(x):
    half = x.shape[1] // 2
    lo = lax.bitcast_convert_type(x[:, :half].astype(BF16).astype(F32), U32) >> 16
    hi = lax.bitcast_convert_type(x[:, half:].astype(BF16).astype(F32), U32) & jnp.uint32(0xFFFF0000)
    return lo | hi


def _unpack_words(w):
    lo = lax.bitcast_convert_type(w << 16, F32)
    hi = lax.bitcast_convert_type(w & jnp.uint32(0xFFFF0000), F32)
    return jnp.concatenate([lo, hi], axis=-1)


def _split3(x):
    hi = x.astype(BF16)
    r1 = x - hi.astype(F32)
    mid = r1.astype(BF16)
    lo = (r1 - mid.astype(F32)).astype(BF16)
    return hi, mid, lo


def _ada_kernel(c_ref, w_ref, b_ref, o_ref):
    cond = _silu(c_ref[...])
    o_ref[...] = _dot(cond.astype(BF16), w_ref[...].astype(BF16)) + b_ref[...]


def _ada(cond_rows, w_ada, b_ada):
    r, d = cond_rows.shape
    n = w_ada.shape[1]
    tn = 1024
    return pl.pallas_call(
        _ada_kernel,
        grid=(n // tn,),
        in_specs=[pl.BlockSpec((r, d), lambda j: (0, 0)),
                  pl.BlockSpec((d, tn), lambda j: (0, j)),
                  pl.BlockSpec((1, tn), lambda j: (0, j))],
        out_specs=pl.BlockSpec((r, tn), lambda j: (0, j)),
        out_shape=jax.ShapeDtypeStruct((r, n), F32),
        compiler_params=_cparams(("arbitrary",)),
        name="ada",
    )(cond_rows, w_ada, b_ada.reshape(1, n))


INPROJ_TOK = 2048


def _inproj_kernel(x_ref, sh_ref, sc_ref, w_ref, o_ref, h_ref):
    @pl.when(pl.program_id(2) == 0)
    def _():
        h = _normalize(x_ref[...]) * (1.0 + sc_ref[...]) + sh_ref[...]
        h_ref[...] = h.astype(BF16)

    o_ref[...] = _dot(h_ref[...], w_ref[...])


def _inproj(x, shift, scale, w_in_bf16):
    b, s, d = x.shape
    tm = min(INPROJ_TOK, s)
    nj = w_in_bf16.shape[1] // d
    return pl.pallas_call(
        _inproj_kernel,
        grid=(b, s // tm, nj),
        in_specs=[pl.BlockSpec((None, tm, d), lambda bi, i, j: (bi, i, 0)),
                  pl.BlockSpec((None, 1, d), lambda bi, i, j: (bi, 0, 0)),
                  pl.BlockSpec((None, 1, d), lambda bi, i, j: (bi, 0, 0)),
                  pl.BlockSpec((d, d), lambda bi, i, j: (0, j))],
        out_specs=pl.BlockSpec((None, None, tm, d), lambda bi, i, j: (j, bi, i, 0)),
        out_shape=jax.ShapeDtypeStruct((nj, b, s, d), F32),
        scratch_shapes=[pltpu.VMEM((tm, d), BF16)],
        compiler_params=_cparams(("arbitrary", "arbitrary", "arbitrary")),
        name="inproj",
    )(x, shift, scale, w_in_bf16)


def _hgrn_gates(q, fraw, v, lb, tri_bf16, last_row):
    f = lb + (1.0 - lb) * jax.nn.sigmoid(fraw)
    k = 1.0 - f
    lf = jnp.log(f)
    hi, mid, lo = _split3(lf)
    a = _dot(tri_bf16, hi) + _dot(tri_bf16, mid) + _dot(tri_bf16, lo)
    a_last = a[last_row:last_row + 1, :]
    kd = (k * jnp.exp(a_last - a)).astype(BF16)
    decay = jnp.exp(a_last)
    qa = kb = None
    if q is not None:
        qa = (_silu(q) * jnp.exp(a)).astype(BF16)
        kb = (k * jnp.exp(-a)).astype(BF16)
    return qa, kb, kd, v.astype(BF16), decay


def _hgrn_chunks(chunks, st_ref):
    first = []
    for d, ((qa, kb, kd, vb, decay), keep) in enumerate(chunks):
        for h in range(HG_HEADS):
            sl = slice(h * HG_DK, (h + 1) * HG_DK)
            st = st_ref[d, h]
            if qa is not None:
                first.append((_dot_nt(qa[:, sl], kb[:, sl]), _dot_nt(qa[:, sl], st.astype(BF16))))
            st_ref[d, h] = st * decay[:, sl] + _dot_tn(vb[:, sl], kd[:, sl])
    results = []
    for d, ((qa, kb, kd, vb, decay), keep) in enumerate(chunks):
        if qa is None:
            results.append(None)
            continue
        outs = []
        for h in range(HG_HEADS):
            sl = slice(h * HG_DK, (h + 1) * HG_DK)
            s_qk, o_state = first.pop(0)
            outs.append(_dot(jnp.where(keep, s_qk, 0.0).astype(BF16), vb[:, sl]) + o_state)
        results.append(jnp.concatenate(outs, axis=-1))
    return results


def _hgrn_kernel(qf_ref, ff_ref, if_ref, qb_ref, fb_ref, ib_ref, cff_ref, cfb_ref, ci_ref,
                 lbf_ref, lbb_ref, of_ref, ob_ref, st_ref, *, n_sub, n_ctx_sub):
    n = pl.program_id(1)
    c = HG_CHUNK
    row = lax.broadcasted_iota(jnp.int32, (c, c), 0)
    col = lax.broadcasted_iota(jnp.int32, (c, c), 1)
    keep_f = col <= row
    keep_b = col >= row
    tri_f = keep_f.astype(F32).astype(BF16)
    tri_b = keep_b.astype(F32).astype(BF16)
    lbf = lbf_ref[...]
    lbb = lbb_ref[...]

    @pl.when(n == 0)
    def _():
        st_ref[...] = jnp.zeros_like(st_ref)

        def body(i, carry):
            r0 = pl.multiple_of(i * c, c)
            r1 = pl.multiple_of((n_ctx_sub - 1 - i) * c, c)
            gf = _hgrn_gates(None, cff_ref[pl.ds(r0, c), :], ci_ref[pl.ds(r0, c), :], lbf, tri_f, c - 1)
            gb = _hgrn_gates(None, cfb_ref[pl.ds(r1, c), :], ci_ref[pl.ds(r1, c), :], lbb, tri_b, 0)
            _hgrn_chunks([(gf, keep_f), (gb, keep_b)], st_ref)
            return carry

        lax.fori_loop(0, n_ctx_sub, body, 0)

    @pl.when(n > 0)
    def _():
        def body(i, carry):
            r0 = pl.multiple_of(i * c, c)
            r1 = pl.multiple_of((n_sub - 1 - i) * c, c)
            gf = _hgrn_gates(qf_ref[pl.ds(r0, c), :], ff_ref[pl.ds(r0, c), :], if_ref[pl.ds(r0, c), :],
                             lbf, tri_f, c - 1)
            gb = _hgrn_gates(qb_ref[pl.ds(r1, c), :], fb_ref[pl.ds(r1, c), :], ib_ref[pl.ds(r1, c), :],
                             lbb, tri_b, 0)
            o_f, o_b = _hgrn_chunks([(gf, keep_f), (gb, keep_b)], st_ref)
            of_ref[pl.ds(r0, c), :] = o_f
            ob_ref[pl.ds(r1, c), :] = o_b
            return carry

        lax.fori_loop(0, n_sub, body, 0, unroll=True)


def _hgrn(p, pc, lb_fwd, lb_bwd):
    _, b, s, w = p.shape
    ctx_len = pc.shape[2]
    tb = min(256, s)
    nb = s // tb
    fwd = lambda bi, n: jnp.maximum(n - 1, 0)
    bwd = lambda bi, n: nb - 1 - jnp.maximum(n - 1, 0)

    def sec(section, blk):
        return pl.BlockSpec((None, None, tb, w), lambda bi, n: (section, bi, blk(bi, n), 0))

    def csec(section):
        return pl.BlockSpec((None, None, ctx_len, w), lambda bi, n: (section, bi, 0, 0))

    vec = pl.BlockSpec((1, w), lambda bi, n: (0, 0))
    kern = functools.partial(_hgrn_kernel, n_sub=tb // HG_CHUNK, n_ctx_sub=ctx_len // HG_CHUNK)
    return pl.pallas_call(
        kern,
        grid=(b, nb + 1),
        in_specs=[sec(SEC_Q, fwd), sec(SEC_FF, fwd), sec(SEC_I, fwd),
                  sec(SEC_Q, bwd), sec(SEC_FB, bwd), sec(SEC_I, bwd),
                  csec(SEC_FF), csec(SEC_FB), csec(SEC_I), vec, vec],
        out_specs=[pl.BlockSpec((None, tb, w), lambda bi, n: (bi, fwd(bi, n), 0)),
                   pl.BlockSpec((None, tb, w), lambda bi, n: (bi, bwd(bi, n), 0))],
        out_shape=[jax.ShapeDtypeStruct((b, s, w), F32), jax.ShapeDtypeStruct((b, s, w), F32)],
        scratch_shapes=[pltpu.VMEM((2, HG_HEADS, HG_DK, HG_DK), F32)],
        compiler_params=_cparams(("arbitrary", "arbitrary")),
        name="hgrn",
    )(p, p, p, p, p, p, pc, pc, pc, lb_fwd.reshape(1, w), lb_bwd.reshape(1, w))


NA_ROWS_PER_STEP = 32
NA_PREP_ROWS = 512
NA_KEY_TILE = 128
NA_SPAN = (NA_WIN_R + 2) * GRID_W


def _rope(t, cos, sin_signed, first_half):
    w = t.shape[-1]
    partner = jnp.where(first_half, pltpu.roll(t, w - 16, 1), pltpu.roll(t, 16, 1))
    return t * cos + partner * sin_signed


def _fold_lanes(op, *arrays):
    tiles = [a[:, c:c + 128] for a in arrays for c in range(0, a.shape[-1], 128)]
    acc = tiles[0]
    for t in tiles[1:]:
        acc = op(acc, t)
    return acc


def _rope_tables(rowtab_ref, coltab_ref, row0, n_rows, row_lane):
    out = []
    for i in range(2):
        rt = rowtab_ref[i, pl.ds(row0, n_rows), :]
        by_row = jnp.concatenate([jnp.broadcast_to(rt[r:r + 1, :], (GRID_W, rt.shape[1])) for r in range(n_rows)],
                                 axis=0)
        by_col = jnp.concatenate([coltab_ref[i]] * n_rows, axis=0)
        out.append(jnp.where(row_lane, by_row, by_col))
    return out


def _natten_kernel(q_ref, k_ref, v_ref, kc_ref, vc_ref, rowtab_ref, coltab_ref, t2_ref, o_ref,
                   kt_s, v_s, kc_s, vc_s, bias_s, *, rows):
    rblk = pl.program_id(2)
    hd = NA_HD
    lane = lax.broadcasted_iota(jnp.int32, (1, 2 * hd), 1)
    first_half = (lane % 32) < 16
    row_lane = (lane % hd) < hd // 2
    scale = NA_HD ** -0.5

    def values_and_ones(v_pair, h):
        vh = v_pair if h == 0 else pltpu.roll(v_pair, hd, 1)
        return jnp.where(lane < hd, vh, jnp.where(lane == hd, 1.0, 0.0)).astype(BF16)

    @pl.when(rblk == 0)
    def _():
        kc = kc_ref[...].astype(BF16)
        qi = lax.broadcasted_iota(jnp.int32, (GRID_W, GRID_W), 0)
        ki = lax.broadcasted_iota(jnp.int32, (GRID_W, GRID_W), 1)
        cstart = jnp.clip(qi - NA_WIN_C // 2, 0, GRID_W - NA_WIN_C)
        in_win = (ki >= cstart) & (ki < cstart + NA_WIN_C)
        masked = jnp.full((GRID_W, GRID_W), NEG_INF, F32)
        s_len = k_ref.shape[0]
        for h in range(2):
            sl = slice(h * hd, (h + 1) * hd)
            kc_s[h] = kc[:, sl]
            vc_s[h] = values_and_ones(vc_ref[...], h)
            kt_s[h, s_len // NA_KEY_TILE] = jnp.zeros((hd, NA_KEY_TILE), BF16)
            v_s[h, s_len:s_len + NA_KEY_TILE, :] = jnp.zeros((NA_KEY_TILE, 2 * hd), BF16)
            tiles = [jnp.where(in_win, t2_ref[h, dr], NEG_INF) for dr in range(2 * NA_WIN_R - 1)]
            for bidx in range(NA_WIN_R + 1):
                v, par = (bidx, 0) if bidx < NA_WIN_R else (NA_WIN_R // 2, 1)
                for piece in range(NA_SPAN // GRID_W):
                    j = piece - par
                    tile = tiles[NA_WIN_R - 1 - v + j] if 0 <= j < NA_WIN_R else masked
                    bias_s[h, bidx, :, piece * GRID_W:(piece + 1) * GRID_W] = tile

        eye = (lax.broadcasted_iota(jnp.int32, (2 * hd, 2 * hd), 0)
               == lax.broadcasted_iota(jnp.int32, (2 * hd, 2 * hd), 1)).astype(F32).astype(BF16)

        def prep(i, carry):
            r0 = pl.multiple_of(i * NA_PREP_ROWS, NA_PREP_ROWS)
            rws = pl.ds(r0, NA_PREP_ROWS)
            cos, sin = _rope_tables(rowtab_ref, coltab_ref, i * (NA_PREP_ROWS // GRID_W), NA_PREP_ROWS // GRID_W,
                                    row_lane)
            kr = _rope(k_ref[rws, :], cos, sin, first_half)
            krt = _dot_nt(eye, kr.astype(BF16)).astype(BF16)
            vv = v_ref[rws, :]
            for h in range(2):
                sl = slice(h * hd, (h + 1) * hd)
                for c in range(NA_PREP_ROWS // NA_KEY_TILE):
                    kt_s[h, i * (NA_PREP_ROWS // NA_KEY_TILE) + c] = krt[sl, c * NA_KEY_TILE:(c + 1) * NA_KEY_TILE]
                v_s[h, rws, :] = values_and_ones(vv, h)
            return carry

        lax.fori_loop(0, s_len // NA_PREP_ROWS, prep, 0, unroll=4)

    tq = NA_ROWS_PER_STEP * GRID_W
    q = q_ref[...] * scale
    cos, sin = _rope_tables(rowtab_ref, coltab_ref, rblk * NA_ROWS_PER_STEP, NA_ROWS_PER_STEP, row_lane)
    qr = _rope(q, cos, sin, first_half)
    qb = q.astype(BF16)
    qrb = qr.astype(BF16)
    rws = [slice(rr * GRID_W, (rr + 1) * GRID_W) for rr in range(NA_ROWS_PER_STEP)]
    tile0, bidx = [], []
    for rr in range(NA_ROWS_PER_STEP):
        r = rblk * NA_ROWS_PER_STEP + rr
        rs = jnp.clip(r - NA_WIN_R // 2, 0, rows - NA_WIN_R)
        tile0.append(lax.shift_right_logical(rs, 1))
        bidx.append(jnp.where((rs & 1) == 1, NA_WIN_R, r - rs))

    def scores(h):
        sl = slice(h * hd, (h + 1) * hd)
        qrb_h = qrb[:, sl]
        s_ctx_all = _dot_nt(qb[:, sl], kc_s[h])
        s_win = []
        for rr in range(NA_ROWS_PER_STEP):
            kt = kt_s[h, pl.ds(tile0[rr], NA_SPAN // NA_KEY_TILE)]
            kt = jnp.concatenate([kt[c] for c in range(NA_SPAN // NA_KEY_TILE)], axis=-1)
            s_win.append(_dot(qrb_h[rws[rr]], kt))
        return s_win, s_ctx_all

    def softmax(h, s_win, s_ctx_all):
        e_win, e_ctx = [], []
        for rr in range(NA_ROWS_PER_STEP):
            sw = s_win[rr] + bias_s[h, bidx[rr]]
            sc = s_ctx_all[rws[rr]]
            m = jnp.max(_fold_lanes(jnp.maximum, sw, sc), axis=-1, keepdims=True)
            e_win.append(jnp.exp(sw - m).astype(BF16))
            e_ctx.append(jnp.exp(sc - m).astype(BF16))
        return e_win, e_ctx

    def values(h, e_win, e_ctx):
        o_win = []
        for rr in range(NA_ROWS_PER_STEP):
            k0 = pl.multiple_of(tile0[rr] * NA_KEY_TILE, NA_KEY_TILE)
            o_win.append(_dot(e_win[rr], v_s[h, pl.ds(k0, NA_SPAN), :]))
        o = jnp.concatenate(o_win, axis=0) + _dot(jnp.concatenate(e_ctx, axis=0), vc_s[h])
        return o[:, :hd] * (1.0 / o[:, hd:hd + 1])

    s0 = scores(0)
    s1 = scores(1)
    p0 = softmax(0, *s0)
    o0 = values(0, *p0)
    p1 = softmax(1, *s1)
    o1 = values(1, *p1)
    o_ref[...] = jnp.concatenate([o0, o1], axis=-1)


def _na_tables(rpb, s):
    half = NA_HD // 2
    inv = jnp.power(ROPE_THETA, -jnp.arange(0, half, 2, dtype=F32) / half)

    def tables(n):
        ang = jnp.arange(n, dtype=F32)[:, None] * inv[None, :]
        reps = 2 * NA_HD // half
        return jnp.stack([jnp.tile(jnp.cos(ang), (1, 2 * reps)),
                          jnp.tile(jnp.concatenate([-jnp.sin(ang), jnp.sin(ang)], axis=-1), (1, reps))])

    rowtab, coltab = tables(s // GRID_W), tables(GRID_W)

    pad = GRID_W - NA_WIN_C
    rp = jnp.pad(rpb.astype(F32), ((0, 0), (0, 0), (pad, pad)), mode="edge")
    t2 = jnp.stack([rp[:, :, GRID_W - 1 - qc:2 * GRID_W - 1 - qc] for qc in range(GRID_W)], axis=2)
    return rowtab, coltab, t2


def _natten(p, pc, rowtab, coltab, t2):
    _, b, s, w = p.shape
    ctx_len = pc.shape[2]
    rows = s // GRID_W
    assert rows >= NA_WIN_R and rows % NA_ROWS_PER_STEP == 0
    tq = NA_ROWS_PER_STEP * GRID_W
    hw = 2 * NA_HD
    nhp = w // hw
    kern = functools.partial(_natten_kernel, rows=rows)
    return pl.pallas_call(
        kern,
        grid=(b, nhp, rows // NA_ROWS_PER_STEP),
        in_specs=[pl.BlockSpec((None, None, tq, hw), lambda bi, hp, r: (SEC_NQ, bi, r, hp)),
                  pl.BlockSpec((None, None, s, hw), lambda bi, hp, r: (SEC_NK, bi, 0, hp)),
                  pl.BlockSpec((None, None, s, hw), lambda bi, hp, r: (SEC_NV, bi, 0, hp)),
                  pl.BlockSpec((None, None, ctx_len, hw), lambda bi, hp, r: (SEC_NK, bi, 0, hp)),
                  pl.BlockSpec((None, None, ctx_len, hw), lambda bi, hp, r: (SEC_NV, bi, 0, hp)),
                  pl.BlockSpec((2, rows, hw), lambda bi, hp, r: (0, 0, 0)),
                  pl.BlockSpec((2, GRID_W, hw), lambda bi, hp, r: (0, 0, 0)),
                  pl.BlockSpec((2, 2 * NA_WIN_R - 1, GRID_W, GRID_W), lambda bi, hp, r: (hp, 0, 0, 0))],
        out_specs=pl.BlockSpec((None, tq, hw), lambda bi, hp, r: (bi, r, hp)),
        out_shape=jax.ShapeDtypeStruct((b, s, w), F32),
        scratch_shapes=[pltpu.VMEM((2, s // NA_KEY_TILE + 1, NA_HD, NA_KEY_TILE), BF16),
                        pltpu.VMEM((2, s + NA_KEY_TILE, hw), BF16),
                        pltpu.VMEM((2, ctx_len, NA_HD), BF16), pltpu.VMEM((2, ctx_len, hw), BF16),
                        pltpu.VMEM((2, NA_WIN_R + 1, GRID_W, NA_SPAN), F32)],
        compiler_params=_cparams(("arbitrary", "arbitrary", "arbitrary")),
        name="natten",
    )(p, p, p, pc, pc, rowtab, coltab, t2)


def _route(logits_t, rbias):
    e, t = logits_t.shape
    gsz = e // N_GROUPS
    scores = jax.nn.sigmoid(logits_t)
    sel = scores + rbias
    neg = -jnp.inf
    sub = lax.broadcasted_iota(jnp.int32, (gsz, t), 0).astype(F32)
    gscore = []
    for g in range(N_GROUPS):
        grp = sel[g * gsz:(g + 1) * gsz, :]
        m1 = jnp.max(grp, axis=0, keepdims=True)
        first = jnp.min(jnp.where(grp == m1, sub, float(gsz)), axis=0, keepdims=True)
        m2 = jnp.max(jnp.where(sub == first, neg, grp), axis=0, keepdims=True)
        gscore.append(m1 + m2)
    masked = []
    for g in range(N_GROUPS):
        rank = jnp.zeros((1, t), F32)
        for g2 in range(N_GROUPS):
            if g2 == g:
                continue
            if g2 < g:
                ahead = gscore[g2] >= gscore[g]
            else:
                ahead = gscore[g2] > gscore[g]
            rank = rank + jnp.where(ahead, 1.0, 0.0)
        masked.append(jnp.where(rank < TOPK_GROUPS, sel[g * gsz:(g + 1) * gsz, :], neg))
    work = jnp.concatenate(masked, axis=0)
    eidx = lax.broadcasted_iota(jnp.int32, (e, t), 0).astype(F32)
    idxs, ws = [], []
    chosen = jnp.zeros((e, t), F32)
    for _ in range(TOP_K):
        m = jnp.max(work, axis=0, keepdims=True)
        first = jnp.min(jnp.where(work == m, eidx, float(e)), axis=0, keepdims=True)
        pick = eidx == first
        idxs.append(first)
        ws.append(jnp.sum(jnp.where(pick, scores, 0.0), axis=0, keepdims=True))
        chosen = jnp.where(pick, 1.0, chosen)
        work = jnp.where(pick, neg, work)
    w = jnp.concatenate(ws, axis=0)
    w = w / jnp.sum(w, axis=0, keepdims=True) * ROUTED_SCALE
    return jnp.concatenate(idxs, axis=0).astype(jnp.int32), w, chosen


MERGE_TOK = 512
MERGE_SUB = 256


def _merge_kernel(of_ref, ob_ref, og_ref, yna_ref, ga_ref, gb_ref, x_ref, g1_ref, sh2_ref, sc2_ref,
                  hgg_ref, ln1g_ref, ln1b_ref, wa_ref, wb_ref, wo_ref, wr_ref, rb_ref,
                  x1_ref, h2_ref, topi_ref, topw_ref, cnt_ref, *, alpha):
    tm = x_ref.shape[0]
    subs = [slice(i * MERGE_SUB, (i + 1) * MERGE_SUB) for i in range(tm // MERGE_SUB)]

    def branches(rows):
        o = of_ref[rows, :] + ob_ref[rows, :]
        parts = []
        for h in range(HG_HEADS):
            oh = o[:, h * HG_DK:(h + 1) * HG_DK]
            parts.append(oh * lax.rsqrt(jnp.mean(oh * oh, axis=-1, keepdims=True) + LN_EPS))
        y_hg = jnp.concatenate(parts, axis=-1) * hgg_ref[...] * _silu(og_ref[rows, :])
        return _dot(y_hg.astype(BF16), wa_ref[...]), _dot(yna_ref[rows, :].astype(BF16), wb_ref[...])

    def out_proj(rows, ya, yb):
        t = jax.nn.sigmoid(ga_ref[rows, :]) * ya + jax.nn.sigmoid(gb_ref[rows, :]) * yb
        return _dot(t.astype(BF16), wo_ref[...])

    def norms_router(rows, i, y):
        x1 = _normalize(alpha * x_ref[rows, :] + g1_ref[...] * y) * ln1g_ref[...] + ln1b_ref[...]
        x1_ref[rows, :] = x1
        h2 = _normalize(x1) * (1.0 + sc2_ref[...]) + sh2_ref[...]
        h2_ref[rows, :] = _pack_words(h2)
        hh, hm, hl = _split3(h2)
        wh, wm, wl = _split3(wr_ref[...])
        return (_dot_nt(wh, hh) + _dot_nt(wh, hm) + _dot_nt(wm, hh)
                + _dot_nt(wh, hl) + _dot_nt(wl, hh) + _dot_nt(wm, hm))

    ab = [branches(rows) for rows in subs]
    ys = [out_proj(rows, *ab[i]) for i, rows in enumerate(subs)]
    logits = [norms_router(rows, i, ys[i]) for i, rows in enumerate(subs)]

    @pl.when((pl.program_id(0) == 0) & (pl.program_id(1) == 0))
    def _():
        cnt_ref[...] = jnp.zeros_like(cnt_ref)

    for i, rows in enumerate(subs):
        topi, topw, chosen = _route(logits[i], rb_ref[...])
        topi_ref[:, rows] = topi
        topw_ref[:, rows] = topw
        cnt_ref[...] += jnp.sum(chosen, axis=1, keepdims=True)


def _merge(o_f, o_b, p, y_na, x, g1, sh2, sc2, hg_norm_g, ln1_g, ln1_b, w_a, w_b, w_o, w_router_t, router_bias,
           alpha):
    b, s, d = x.shape
    tm = min(MERGE_TOK, s)
    e = w_router_t.shape[0]
    tok = lambda bi, i: (bi, i, 0)
    blk = pl.BlockSpec((None, tm, d), tok)

    def sec(section):
        return pl.BlockSpec((None, None, tm, d), lambda bi, i: (section, bi, i, 0))

    mod = pl.BlockSpec((None, 1, d), lambda bi, i: (bi, 0, 0))
    vec = pl.BlockSpec((1, d), lambda bi, i: (0, 0))
    mat = pl.BlockSpec((d, d), lambda bi, i: (0, 0), pipeline_mode=pl.Buffered(1))
    return pl.pallas_call(
        functools.partial(_merge_kernel, alpha=alpha),
        grid=(b, s // tm),
        in_specs=[blk, blk, sec(SEC_OG), blk, sec(SEC_GA), sec(SEC_GB), blk, mod, mod, mod,
                  vec, vec, vec, mat, mat, mat,
                  pl.BlockSpec((e, d), lambda bi, i: (0, 0)),
                  pl.BlockSpec((e, 1), lambda bi, i: (0, 0))],
        out_specs=[blk,
                   pl.BlockSpec((tm, d // 2), lambda bi, i: (bi * (s // tm) + i, 0)),
                   pl.BlockSpec((None, TOP_K, tm), lambda bi, i: (bi, 0, i)),
                   pl.BlockSpec((None, TOP_K, tm), lambda bi, i: (bi, 0, i)),
                   pl.BlockSpec((e, 128), lambda bi, i: (0, 0))],
        out_shape=[jax.ShapeDtypeStruct((b, s, d), F32),
                   jax.ShapeDtypeStruct((b * s, d // 2), U32),
                   jax.ShapeDtypeStruct((b, TOP_K, s), jnp.int32), jax.ShapeDtypeStruct((b, TOP_K, s), F32),
                   jax.ShapeDtypeStruct((e, 128), F32)],
        compiler_params=_cparams(("arbitrary", "arbitrary")),
        name="merge",
    )(o_f, o_b, p, y_na, p, p, x, g1, sh2, sc2, hg_norm_g.reshape(1, d), ln1_g.reshape(1, d),
      ln1_b.reshape(1, d), w_a, w_b, w_o, w_router_t, router_bias.reshape(e, 1))


MOE_TILE = 512
MOE_TOK = 512


def _plan_kernel(topi_ref, off_ref, dest_ref, carry_ref):
    @pl.when(pl.program_id(0) == 0)
    def _():
        carry_ref[...] = jnp.zeros_like(carry_ref)

    topi = topi_ref[...]
    tok = topi.shape[1]
    eidx = lax.broadcasted_iota(jnp.int32, (N_EXPERTS, tok), 0)
    hits = [eidx == topi[k:k + 1, :] for k in range(TOP_K)]
    m = jnp.zeros((N_EXPERTS, tok), F32)
    for hit in hits:
        m = jnp.where(hit, 1.0, m)
    before = (lax.broadcasted_iota(jnp.int32, (tok, tok), 0)
              < lax.broadcasted_iota(jnp.int32, (tok, tok), 1)).astype(F32).astype(BF16)
    row = off_ref[...] + carry_ref[...] + _dot(m.astype(BF16), before)
    dest = [jnp.sum(jnp.where(hit, row, 0.0), axis=0, keepdims=True) for hit in hits]
    dest_ref[...] = jnp.concatenate(dest, axis=0).astype(jnp.int32)
    carry_ref[...] += jnp.sum(m, axis=1, keepdims=True)


def _plan(topi, seg_off):
    b, k, s = topi.shape
    per_b = s // MOE_TOK
    blk = pl.BlockSpec((None, k, MOE_TOK), lambda i: (i // per_b, 0, i % per_b))
    return pl.pallas_call(
        _plan_kernel,
        grid=(b * per_b,),
        in_specs=[blk, pl.BlockSpec((N_EXPERTS, 1), lambda i: (0, 0))],
        out_specs=blk,
        out_shape=jax.ShapeDtypeStruct((b, k, s), jnp.int32),
        scratch_shapes=[pltpu.VMEM((N_EXPERTS, 1), F32)],
        compiler_params=_cparams(("arbitrary",)),
        name="plan",
    )(topi, seg_off.astype(F32).reshape(N_EXPERTS, 1))


SC_WINDOW = 128


def _sc_workers():
    info = plsc.get_sparse_core_info()
    return info.num_cores, info.num_cores * info.num_subcores


def _scatter_rows(src, idx, zero_idx, n_rows):
    t, w = src.shape
    m, mz = idx.shape[0], zero_idx.shape[0]
    n_cores, n_workers = _sc_workers()
    per_worker, per_worker_z = t // n_workers, mz // n_workers
    assert m % t == 0
    assert per_worker * n_workers == t and per_worker % SC_WINDOW == 0
    assert per_worker_z * n_workers == mz and per_worker_z % SC_WINDOW == 0
    mesh = plsc.VectorSubcoreMesh(core_axis_name="core", subcore_axis_name="subcore")

    @functools.partial(
        pl.kernel, mesh=mesh, out_type=jax.ShapeDtypeStruct((n_rows, w), src.dtype),
        scratch_types=[pltpu.VMEM((SC_WINDOW,), jnp.int32), pltpu.VMEM((SC_WINDOW, w), src.dtype),
                       pltpu.SemaphoreType.DMA])
    def scatter(src_hbm, idx_hbm, zeros_hbm, zero_idx_hbm, out_hbm, idx_v, rows_v, sem):
        worker = lax.axis_index("subcore") * n_cores + lax.axis_index("core")

        @pl.loop(0, per_worker // SC_WINDOW)
        def _(step):
            first = pl.multiple_of(worker * per_worker + step * SC_WINDOW, SC_WINDOW)
            pltpu.sync_copy(src_hbm.at[pl.ds(first, SC_WINDOW)], rows_v)
            for copy in range(m // t):
                pltpu.sync_copy(idx_hbm.at[pl.ds(copy * t + first, SC_WINDOW)], idx_v)
                pltpu.async_copy(rows_v, out_hbm.at[idx_v], sem).wait()

        pltpu.sync_copy(zeros_hbm, rows_v)

        @pl.loop(0, per_worker_z // SC_WINDOW)
        def _(step):
            base = pl.multiple_of(worker * per_worker_z + step * SC_WINDOW, SC_WINDOW)
            pltpu.sync_copy(zero_idx_hbm.at[pl.ds(base, SC_WINDOW)], idx_v)
            pltpu.async_copy(rows_v, out_hbm.at[idx_v], sem).wait()

    return scatter(src, idx, jnp.zeros((SC_WINDOW, w), src.dtype), zero_idx)


EXPERT_RING = 3


def _experts_kernel(te_ref, tb_ref, nt_ref, xs_ref, wg_ref, wu_ref, wd_ref, ys_ref, xbuf, sem):
    i = pl.program_id(0)
    n_tiles = nt_ref[0]

    def tile_copy(j):
        slot = lax.rem(j, EXPERT_RING)
        row0 = pl.multiple_of(tb_ref[j] * MOE_TILE, MOE_TILE)
        return pltpu.make_async_copy(xs_ref.at[pl.ds(row0, MOE_TILE), :], xbuf.at[slot], sem.at[slot])

    @pl.when(i == 0)
    def _():
        for j in range(EXPERT_RING - 1):
            @pl.when(j < n_tiles)
            def _():
                tile_copy(j).start()

    ahead = i + (EXPERT_RING - 1)

    @pl.when(ahead < n_tiles)
    def _():
        tile_copy(ahead).start()

    @pl.when(i < n_tiles)
    def _():
        tile_copy(i).wait()
        x = _unpack_words(xbuf[lax.rem(i, EXPERT_RING)]).astype(BF16)
        act = _silu(_dot(x, wg_ref[...].astype(BF16))) * _dot(x, wu_ref[...].astype(BF16))
        ys_ref[...] = _pack_words(_dot(act.astype(BF16), wd_ref[...].astype(BF16)))


def _experts(xs, tile_expert, tile_block, n_tiles, wg, wu, wd):
    d, f = wg.shape[1], wg.shape[2]
    grid_spec = pltpu.PrefetchScalarGridSpec(
        num_scalar_prefetch=3,
        grid=(xs.shape[0] // MOE_TILE,),
        in_specs=[pl.BlockSpec(memory_space=pl.ANY),
                  pl.BlockSpec((None, d, f), lambda i, te, tb, nt: (te[i], 0, 0)),
                  pl.BlockSpec((None, d, f), lambda i, te, tb, nt: (te[i], 0, 0)),
                  pl.BlockSpec((None, f, d), lambda i, te, tb, nt: (te[i], 0, 0))],
        out_specs=pl.BlockSpec((MOE_TILE, d // 2), lambda i, te, tb, nt: (tb[i], 0)),
        scratch_shapes=[pltpu.VMEM((EXPERT_RING, MOE_TILE, d // 2), U32), pltpu.SemaphoreType.DMA((EXPERT_RING,))],
    )
    return pl.pallas_call(
        _experts_kernel,
        grid_spec=grid_spec,
        out_shape=jax.ShapeDtypeStruct(xs.shape, U32),
        compiler_params=_cparams(("arbitrary",)),
        name="experts",
    )(tile_expert, tile_block, n_tiles, xs, wg, wu, wd)


def _gather_rows(table, idx):
    m = idx.shape[0]
    w = table.shape[1]
    n_cores, n_workers = _sc_workers()
    per_worker = m // n_workers
    assert per_worker * n_workers == m and per_worker % SC_WINDOW == 0
    mesh = plsc.VectorSubcoreMesh(core_axis_name="core", subcore_axis_name="subcore")

    @functools.partial(
        pl.kernel, mesh=mesh, out_type=jax.ShapeDtypeStruct((m, w), table.dtype),
        scratch_types=[pltpu.VMEM((SC_WINDOW,), jnp.int32), pltpu.VMEM((SC_WINDOW, w), table.dtype),
                       pltpu.SemaphoreType.DMA])
    def gather(table_hbm, idx_hbm, out_hbm, idx_v, rows_v, sem):
        worker = lax.axis_index("subcore") * n_cores + lax.axis_index("core")

        @pl.loop(0, per_worker // SC_WINDOW)
        def _(step):
            base = pl.multiple_of(worker * per_worker + step * SC_WINDOW, SC_WINDOW)
            pltpu.sync_copy(idx_hbm.at[pl.ds(base, SC_WINDOW)], idx_v)
            pltpu.async_copy(table_hbm.at[idx_v], rows_v, sem).wait()
            pltpu.sync_copy(rows_v, out_hbm.at[pl.ds(base, SC_WINDOW)])

    return gather(table, idx)


def _combine_kernel(rows_ref, topw_ref, h_ref, x1_ref, g2_ref, sg_ref, su_ref, sd_ref, ln2g_ref, ln2b_ref,
                    o_ref, *, alpha):
    h = _unpack_words(h_ref[...]).astype(BF16)
    act = _silu(_dot(h, sg_ref[...])) * _dot(h, su_ref[...])
    y = _dot(act.astype(BF16), sd_ref[...])
    w = topw_ref[...].T
    for k in range(TOP_K):
        y = y + w[:, k:k + 1] * _unpack_words(rows_ref[k])
    o_ref[...] = _normalize(alpha * x1_ref[...] + g2_ref[...] * y) * ln2g_ref[...] + ln2b_ref[...]


def _combine(gathered, topw, h2p, x1, g2, sg, su, sd, ln2_g, ln2_b, alpha):
    t, d = x1.shape
    b, k, s = topw.shape
    per_b = s // MOE_TOK
    fs = sg.shape[1]
    rows = pl.BlockSpec((MOE_TOK, d), lambda i: (i, 0))
    packed = pl.BlockSpec((MOE_TOK, d // 2), lambda i: (i, 0))
    vec = pl.BlockSpec((1, d), lambda i: (0, 0))
    return pl.pallas_call(
        functools.partial(_combine_kernel, alpha=alpha),
        grid=(t // MOE_TOK,),
        in_specs=[pl.BlockSpec((k, MOE_TOK, d // 2), lambda i: (0, i, 0)),
                  pl.BlockSpec((None, k, MOE_TOK), lambda i: (i // per_b, 0, i % per_b)),
                  packed, rows,
                  pl.BlockSpec((None, 1, d), lambda i: (i // per_b, 0, 0)),
                  pl.BlockSpec((d, fs), lambda i: (0, 0)),
                  pl.BlockSpec((d, fs), lambda i: (0, 0)),
                  pl.BlockSpec((fs, d), lambda i: (0, 0)),
                  vec, vec],
        out_specs=rows,
        out_shape=jax.ShapeDtypeStruct((t, d), F32),
        compiler_params=_cparams(("arbitrary",)),
        name="combine",
    )(gathered, topw, h2p, x1, g2, sg, su, sd, ln2_g.reshape(1, d), ln2_b.reshape(1, d))


def _moe(h2p, topi, topw, cnt, x1, g2, wg, wu, wd, sg, su, sd, ln2_g, ln2_b, alpha):
    b, s, d = x1.shape
    t = b * s
    cnt = cnt[:, 0].astype(jnp.int32)
    tiles_e = (cnt + (MOE_TILE - 1)) // MOE_TILE
    tiles_cum = jnp.cumsum(tiles_e)
    seg_off = (tiles_cum - tiles_e) * MOE_TILE
    n_tiles_max = t * TOP_K // MOE_TILE + N_EXPERTS
    tile_block = jnp.minimum(jnp.arange(n_tiles_max, dtype=jnp.int32), tiles_cum[-1] - 1)
    tile_expert = jnp.sum((tiles_cum[None, :] <= tile_block[:, None]).astype(jnp.int32), axis=1)
    n_tiles = tiles_cum[-1:].astype(jnp.int32)

    dest = jnp.transpose(_plan(topi, seg_off), (1, 0, 2)).reshape(TOP_K * t)
    j = jnp.arange(MOE_TILE, dtype=jnp.int32)[None, :]
    n_pad = (tiles_e * MOE_TILE - cnt)[:, None]
    spare = (n_tiles_max - 1) * MOE_TILE + j
    zero_idx = jnp.where(j < n_pad, (seg_off + cnt)[:, None] + j, spare).reshape(N_EXPERTS * MOE_TILE)
    xs = _scatter_rows(h2p, dest, zero_idx.astype(jnp.int32), n_tiles_max * MOE_TILE)
    ys = _experts(xs, tile_expert, tile_block, n_tiles, wg, wu, wd)
    gathered = _gather_rows(ys, dest)
    out = _combine(gathered.reshape(TOP_K, t, d // 2), topw, h2p, x1.reshape(t, d), g2, sg, su, sd,
                   ln2_g, ln2_b, alpha)
    return out.reshape(b, s, d)


def kernel(x, c, ctx, c_ctx, w_ada, b_ada, w_in, hg_lb_fwd, hg_lb_bwd, hg_norm_g, na_rpb, w_branch_a, w_branch_b, w_out, ln1_g, ln1_b, w_router, router_bias, w_e_gate, w_e_up, w_e_down, w_sh_gate, w_sh_up, w_sh_down, ln2_g, ln2_b):
    depth = w_ada.shape[0]
    assert depth == 1, "single-layer block"
    b, s, d = x.shape
    alpha = (2.0 * depth) ** 0.25
    l = 0
    lb_fwd = jnp.cumsum(jax.nn.softmax(hg_lb_fwd.astype(F32), axis=0), axis=0)[l]
    lb_bwd = jnp.cumsum(jax.nn.softmax(hg_lb_bwd.astype(F32), axis=0), axis=0)[l]

    cond_rows = jnp.concatenate([c, c_ctx[None, :], jnp.zeros((8 - b - 1, d), F32)], axis=0)
    mod = _ada(cond_rows, w_ada[l], b_ada[l])
    sh1, sc1, g1, sh2, sc2, g2 = [m[:b, None, :] for m in jnp.split(mod, 6, axis=-1)]
    csh1, csc1 = [jnp.broadcast_to(m[b:b + 1, None, :], (b, 1, d)) for m in jnp.split(mod, 6, axis=-1)[:2]]

    w_in_b = w_in[l].astype(BF16)
    p = _inproj(x, sh1, sc1, w_in_b)
    pc = _inproj(ctx, csh1, csc1, w_in_b)

    o_f, o_b = _hgrn(p, pc, lb_fwd, lb_bwd)
    y_na = _natten(p, pc, *_na_tables(na_rpb[l], s))

    x1, h2, topi, topw, cnt = _merge(o_f, o_b, p, y_na, x, g1, sh2, sc2, hg_norm_g[l], ln1_g[l], ln1_b[l],
                                     w_branch_a[l].astype(BF16), w_branch_b[l].astype(BF16),
                                     w_out[l].astype(BF16), w_router[l].T, router_bias[l], alpha)

    return _moe(h2, topi, topw, cnt, x1, g2,
                w_e_gate[l], w_e_up[l], w_e_down[l],
                w_sh_gate[l].astype(BF16), w_sh_up[l].astype(BF16), w_sh_down[l].astype(BF16),
                ln2_g[l], ln2_b[l], alpha)
```

```python
import functools

import numpy as np
import jax
import jax.numpy as jnp
from jax import lax
from jax.experimental import pallas as pl
from jax.experimental.pallas import tpu as pltpu
from jax.experimental.pallas import tpu_sc as plsc

F32 = jnp.float32
BF16 = jnp.bfloat16

D_MODEL = 1024
GRID_W = 64
HG_HEADS = 8
HG_DK = 128
HG_CHUNK = 64
NA_HEADS = 16
NA_HD = 64
NA_WIN_R = 8
NA_WIN_C = 16
ROPE_THETA = 10000.0
NEG_INF = -1e30
N_EXPERTS = 64
EXPERT_DIM = 256
TOP_K = 8
N_GROUPS = 8
TOPK_GROUPS = 4
ROUTED_SCALE = 2.5
LN_EPS = 1e-6
N_SECTIONS = 10
SEC_Q, SEC_FF, SEC_FB, SEC_I, SEC_OG, SEC_NQ, SEC_NK, SEC_NV, SEC_GA, SEC_GB = range(10)

VMEM_LIMIT = 56 * 1024 * 1024


def _cparams(sem):
    return pltpu.CompilerParams(dimension_semantics=sem, vmem_limit_bytes=VMEM_LIMIT)


def _normalize(x):
    mu = jnp.mean(x, axis=-1, keepdims=True)
    xc = x - mu
    var = jnp.mean(xc * xc, axis=-1, keepdims=True)
    return xc * lax.rsqrt(var + LN_EPS)


def _silu(x):
    return x * jax.nn.sigmoid(x)


def _dot(a, b):
    return jnp.dot(a, b, preferred_element_type=F32)


def _dot_nt(a, b):
    return lax.dot_general(a, b, (((1,), (1,)), ((), ())), preferred_element_type=F32)


def _dot_tn(a, b):
    return lax.dot_general(a, b, (((0,), (0,)), ((), ())), preferred_element_type=F32)


U32 = jnp.uint32


def _pack_words(x):
    half = x.shape[1] // 2
    lo = lax.bitcast_convert_type(x[:, :half].astype(BF16).astype(F32), U32) >> 16
    hi = lax.bitcast_convert_type(x[:, half:].astype(BF16).astype(F32), U32) & jnp.uint32(0xFFFF0000)
    return lo | hi


def _unpack_words(w):
    lo = lax.bitcast_convert_type(w << 16, F32)
    hi = lax.bitcast_convert_type(w & jnp.uint32(0xFFFF0000), F32)
    return jnp.concatenate([lo, hi], axis=-1)


def _split3(x):
    hi = x.astype(BF16)
    r1 = x - hi.astype(F32)
    mid = r1.astype(BF16)
    lo = (r1 - mid.astype(F32)).astype(BF16)
    return hi, mid, lo


def _ada_kernel(c_ref, w_ref, b_ref, o_ref):
    cond = _silu(c_ref[...])
    o_ref[...] = _dot(cond.astype(BF16), w_ref[...].astype(BF16)) + b_ref[...]


def _ada(cond_rows, w_ada, b_ada):
    r, d = cond_rows.shape
    n = w_ada.shape[1]
    tn = 1024
    return pl.pallas_call(
        _ada_kernel,
        grid=(n // tn,),
        in_specs=[pl.BlockSpec((r, d), lambda j: (0, 0)),
                  pl.BlockSpec((d, tn), lambda j: (0, j)),
                  pl.BlockSpec((1, tn), lambda j: (0, j))],
        out_specs=pl.BlockSpec((r, tn), lambda j: (0, j)),
        out_shape=jax.ShapeDtypeStruct((r, n), F32),
        compiler_params=_cparams(("arbitrary",)),
        name="ada",
    )(cond_rows, w_ada, b_ada.reshape(1, n))


INPROJ_TOK = 2048


def _inproj_kernel(x_ref, sh_ref, sc_ref, w_ref, o_ref, h_ref):
    @pl.when(pl.program_id(2) == 0)
    def _():
        h = _normalize(x_ref[...]) * (1.0 + sc_ref[...]) + sh_ref[...]
        h_ref[...] = h.astype(BF16)

    o_ref[...] = _dot(h_ref[...], w_ref[...])


def _inproj(x, shift, scale, w_in_bf16):
    b, s, d = x.shape
    tm = min(INPROJ_TOK, s)
    nj = w_in_bf16.shape[1] // d
    return pl.pallas_call(
        _inproj_kernel,
        grid=(b, s // tm, nj),
        in_specs=[pl.BlockSpec((None, tm, d), lambda bi, i, j: (bi, i, 0)),
                  pl.BlockSpec((None, 1, d), lambda bi, i, j: (bi, 0, 0)),
                  pl.BlockSpec((None, 1, d), lambda bi, i, j: (bi, 0, 0)),
                  pl.BlockSpec((d, d), lambda bi, i, j: (0, j))],
        out_specs=pl.BlockSpec((None, None, tm, d), lambda bi, i, j: (j, bi, i, 0)),
        out_shape=jax.ShapeDtypeStruct((nj, b, s, d), F32),
        scratch_shapes=[pltpu.VMEM((tm, d), BF16)],
        compiler_params=_cparams(("arbitrary", "arbitrary", "arbitrary")),
        name="inproj",
    )(x, shift, scale, w_in_bf16)


def _hgrn_gates(q, fraw, v, lb, tri_bf16, last_row):
    f = lb + (1.0 - lb) * jax.nn.sigmoid(fraw)
    k = 1.0 - f
    lf = jnp.log(f)
    hi, mid, lo = _split3(lf)
    a = _dot(tri_bf16, hi) + _dot(tri_bf16, mid) + _dot(tri_bf16, lo)
    a_last = a[last_row:last_row + 1, :]
    kd = (k * jnp.exp(a_last - a)).astype(BF16)
    decay = jnp.exp(a_last)
    qa = kb = None
    if q is not None:
        qa = (_silu(q) * jnp.exp(a)).astype(BF16)
        kb = (k * jnp.exp(-a)).astype(BF16)
    return qa, kb, kd, v.astype(BF16), decay


def _hgrn_chunks(chunks, st_ref):
    first = []
    for d, ((qa, kb, kd, vb, decay), keep) in enumerate(chunks):
        for h in range(HG_HEADS):
            sl = slice(h * HG_DK, (h + 1) * HG_DK)
            st = st_ref[d, h]
            if qa is not None:
                first.append((_dot_nt(qa[:, sl], kb[:, sl]), _dot_nt(qa[:, sl], st.astype(BF16))))
            st_ref[d, h] = st * decay[:, sl] + _dot_tn(vb[:, sl], kd[:, sl])
    results = []
    for d, ((qa, kb, kd, vb, decay), keep) in enumerate(chunks):
        if qa is None:
            results.append(None)
            continue
        outs = []
        for h in range(HG_HEADS):
            sl = slice(h * HG_DK, (h + 1) * HG_DK)
            s_qk, o_state = first.pop(0)
            outs.append(_dot(jnp.where(keep, s_qk, 0.0).astype(BF16), vb[:, sl]) + o_state)
        results.append(jnp.concatenate(outs, axis=-1))
    return results


def _hgrn_kernel(qf_ref, ff_ref, if_ref, qb_ref, fb_ref, ib_ref, cff_ref, cfb_ref, ci_ref,
                 lbf_ref, lbb_ref, of_ref, ob_ref, st_ref, *, n_sub, n_ctx_sub):
    n = pl.program_id(1)
    c = HG_CHUNK
    row = lax.broadcasted_iota(jnp.int32, (c, c), 0)
    col = lax.broadcasted_iota(jnp.int32, (c, c), 1)
    keep_f = col <= row
    keep_b = col >= row
    tri_f = keep_f.astype(F32).astype(BF16)
    tri_b = keep_b.astype(F32).astype(BF16)
    lbf = lbf_ref[...]
    lbb = lbb_ref[...]

    @pl.when(n == 0)
    def _():
        st_ref[...] = jnp.zeros_like(st_ref)

        def body(i, carry):
            r0 = pl.multiple_of(i * c, c)
            r1 = pl.multiple_of((n_ctx_sub - 1 - i) * c, c)
            gf = _hgrn_gates(None, cff_ref[pl.ds(r0, c), :], ci_ref[pl.ds(r0, c), :], lbf, tri_f, c - 1)
            gb = _hgrn_gates(None, cfb_ref[pl.ds(r1, c), :], ci_ref[pl.ds(r1, c), :], lbb, tri_b, 0)
            _hgrn_chunks([(gf, keep_f), (gb, keep_b)], st_ref)
            return carry

        lax.fori_loop(0, n_ctx_sub, body, 0)

    @pl.when(n > 0)
    def _():
        def body(i, carry):
            r0 = pl.multiple_of(i * c, c)
            r1 = pl.multiple_of((n_sub - 1 - i) * c, c)
            gf = _hgrn_gates(qf_ref[pl.ds(r0, c), :], ff_ref[pl.ds(r0, c), :], if_ref[pl.ds(r0, c), :],
                             lbf, tri_f, c - 1)
            gb = _hgrn_gates(qb_ref[pl.ds(r1, c), :], fb_ref[pl.ds(r1, c), :], ib_ref[pl.ds(r1, c), :],
                             lbb, tri_b, 0)
            o_f, o_b = _hgrn_chunks([(gf, keep_f), (gb, keep_b)], st_ref)
            of_ref[pl.ds(r0, c), :] = o_f
            ob_ref[pl.ds(r1, c), :] = o_b
            return carry

        lax.fori_loop(0, n_sub, body, 0, unroll=True)


def _hgrn(p, pc, lb_fwd, lb_bwd):
    _, b, s, w = p.shape
    ctx_len = pc.shape[2]
    tb = min(256, s)
    nb = s // tb
    fwd = lambda bi, n: jnp.maximum(n - 1, 0)
    bwd = lambda bi, n: nb - 1 - jnp.maximum(n - 1, 0)

    def sec(section, blk):
        return pl.BlockSpec((None, None, tb, w), lambda bi, n: (section, bi, blk(bi, n), 0))

    def csec(section):
        return pl.BlockSpec((None, None, ctx_len, w), lambda bi, n: (section, bi, 0, 0))

    vec = pl.BlockSpec((1, w), lambda bi, n: (0, 0))
    kern = functools.partial(_hgrn_kernel, n_sub=tb // HG_CHUNK, n_ctx_sub=ctx_len // HG_CHUNK)
    return pl.pallas_call(
        kern,
        grid=(b, nb + 1),
        in_specs=[sec(SEC_Q, fwd), sec(SEC_FF, fwd), sec(SEC_I, fwd),
                  sec(SEC_Q, bwd), sec(SEC_FB, bwd), sec(SEC_I, bwd),
                  csec(SEC_FF), csec(SEC_FB), csec(SEC_I), vec, vec],
        out_specs=[pl.BlockSpec((None, tb, w), lambda bi, n: (bi, fwd(bi, n), 0)),
                   pl.BlockSpec((None, tb, w), lambda bi, n: (bi, bwd(bi, n), 0))],
        out_shape=[jax.ShapeDtypeStruct((b, s, w), F32), jax.ShapeDtypeStruct((b, s, w), F32)],
        scratch_shapes=[pltpu.VMEM((2, HG_HEADS, HG_DK, HG_DK), F32)],
        compiler_params=_cparams(("arbitrary", "arbitrary")),
        name="hgrn",
    )(p, p, p, p, p, p, pc, pc, pc, lb_fwd.reshape(1, w), lb_bwd.reshape(1, w))


NA_ROWS_PER_STEP = 32
NA_PREP_ROWS = 512
NA_KEY_TILE = 128
NA_SPAN = (NA_WIN_R + 2) * GRID_W


def _rope(t, cos, sin_signed, first_half):
    w = t.shape[-1]
    partner = jnp.where(first_half, pltpu.roll(t, w - 16, 1), pltpu.roll(t, 16, 1))
    return t * cos + partner * sin_signed


def _fold_lanes(op, *arrays):
    tiles = [a[:, c:c + 128] for a in arrays for c in range(0, a.shape[-1], 128)]
    acc = tiles[0]
    for t in tiles[1:]:
        acc = op(acc, t)
    return acc


def _rope_tables(rowtab_ref, coltab_ref, row0, n_rows, row_lane):
    out = []
    for i in range(2):
        rt = rowtab_ref[i, pl.ds(row0, n_rows), :]
        by_row = jnp.concatenate([jnp.broadcast_to(rt[r:r + 1, :], (GRID_W, rt.shape[1])) for r in range(n_rows)],
                                 axis=0)
        by_col = jnp.concatenate([coltab_ref[i]] * n_rows, axis=0)
        out.append(jnp.where(row_lane, by_row, by_col))
    return out


def _natten_kernel(q_ref, k_ref, v_ref, kc_ref, vc_ref, rowtab_ref, coltab_ref, t2_ref, o_ref,
                   kt_s, v_s, kc_s, vc_s, bias_s, *, rows):
    rblk = pl.program_id(2)
    hd = NA_HD
    lane = lax.broadcasted_iota(jnp.int32, (1, 2 * hd), 1)
    first_half = (lane % 32) < 16
    row_lane = (lane % hd) < hd // 2
    scale = NA_HD ** -0.5

    def values_and_ones(v_pair, h):
        vh = v_pair if h == 0 else pltpu.roll(v_pair, hd, 1)
        return jnp.where(lane < hd, vh, jnp.where(lane == hd, 1.0, 0.0)).astype(BF16)

    @pl.when(rblk == 0)
    def _():
        kc = kc_ref[...].astype(BF16)
        qi = lax.broadcasted_iota(jnp.int32, (GRID_W, GRID_W), 0)
        ki = lax.broadcasted_iota(jnp.int32, (GRID_W, GRID_W), 1)
        cstart = jnp.clip(qi - NA_WIN_C // 2, 0, GRID_W - NA_WIN_C)
        in_win = (ki >= cstart) & (ki < cstart + NA_WIN_C)
        masked = jnp.full((GRID_W, GRID_W), NEG_INF, F32)
        s_len = k_ref.shape[0]
        for h in range(2):
            sl = slice(h * hd, (h + 1) * hd)
            kc_s[h] = kc[:, sl]
            vc_s[h] = values_and_ones(vc_ref[...], h)
            kt_s[h, s_len // NA_KEY_TILE] = jnp.zeros((hd, NA_KEY_TILE), BF16)
            v_s[h, s_len:s_len + NA_KEY_TILE, :] = jnp.zeros((NA_KEY_TILE, 2 * hd), BF16)
            tiles = [jnp.where(in_win, t2_ref[h, dr], NEG_INF) for dr in range(2 * NA_WIN_R - 1)]
            for bidx in range(NA_WIN_R + 1):
                v, par = (bidx, 0) if bidx < NA_WIN_R else (NA_WIN_R // 2, 1)
                for piece in range(NA_SPAN // GRID_W):
                    j = piece - par
                    tile = tiles[NA_WIN_R - 1 - v + j] if 0 <= j < NA_WIN_R else masked
                    bias_s[h, bidx, :, piece * GRID_W:(piece + 1) * GRID_W] = tile

        eye = (lax.broadcasted_iota(jnp.int32, (2 * hd, 2 * hd), 0)
               == lax.broadcasted_iota(jnp.int32, (2 * hd, 2 * hd), 1)).astype(F32).astype(BF16)

        def prep(i, carry):
            r0 = pl.multiple_of(i * NA_PREP_ROWS, NA_PREP_ROWS)
            rws = pl.ds(r0, NA_PREP_ROWS)
            cos, sin = _rope_tables(rowtab_ref, coltab_ref, i * (NA_PREP_ROWS // GRID_W), NA_PREP_ROWS // GRID_W,
                                    row_lane)
            kr = _rope(k_ref[rws, :], cos, sin, first_half)
            krt = _dot_nt(eye, kr.astype(BF16)).astype(BF16)
            vv = v_ref[rws, :]
            for h in range(2):
                sl = slice(h * hd, (h + 1) * hd)
                for c in range(NA_PREP_ROWS // NA_KEY_TILE):
                    kt_s[h, i * (NA_PREP_ROWS // NA_KEY_TILE) + c] = krt[sl, c * NA_KEY_TILE:(c + 1) * NA_KEY_TILE]
                v_s[h, rws, :] = values_and_ones(vv, h)
            return carry

        lax.fori_loop(0, s_len // NA_PREP_ROWS, prep, 0, unroll=4)

    tq = NA_ROWS_PER_STEP * GRID_W
    q = q_ref[...] * scale
    cos, sin = _rope_tables(rowtab_ref, coltab_ref, rblk * NA_ROWS_PER_STEP, NA_ROWS_PER_STEP, row_lane)
    qr = _rope(q, cos, sin, first_half)
    qb = q.astype(BF16)
    qrb = qr.astype(BF16)
    rws = [slice(rr * GRID_W, (rr + 1) * GRID_W) for rr in range(NA_ROWS_PER_STEP)]
    tile0, bidx = [], []
    for rr in range(NA_ROWS_PER_STEP):
        r = rblk * NA_ROWS_PER_STEP + rr
        rs = jnp.clip(r - NA_WIN_R // 2, 0, rows - NA_WIN_R)
        tile0.append(lax.shift_right_logical(rs, 1))
        bidx.append(jnp.where((rs & 1) == 1, NA_WIN_R, r - rs))

    def scores(h):
        sl = slice(h * hd, (h + 1) * hd)
        qrb_h = qrb[:, sl]
        s_ctx_all = _dot_nt(qb[:, sl], kc_s[h])
        s_win = []
        for rr in range(NA_ROWS_PER_STEP):
            kt = kt_s[h, pl.ds(tile0[rr], NA_SPAN // NA_KEY_TILE)]
            kt = jnp.concatenate([kt[c] for c in range(NA_SPAN // NA_KEY_TILE)], axis=-1)
            s_win.append(_dot(qrb_h[rws[rr]], kt))
        return s_win, s_ctx_all

    def softmax(h, s_win, s_ctx_all):
        e_win, e_ctx = [], []
        for rr in range(NA_ROWS_PER_STEP):
            sw = s_win[rr] + bias_s[h, bidx[rr]]
            sc = s_ctx_all[rws[rr]]
            m = jnp.max(_fold_lanes(jnp.maximum, sw, sc), axis=-1, keepdims=True)
            e_win.append(jnp.exp(sw - m).astype(BF16))
            e_ctx.append(jnp.exp(sc - m).astype(BF16))
        return e_win, e_ctx

    def values(h, e_win, e_ctx):
        o_win = []
        for rr in range(NA_ROWS_PER_STEP):
            k0 = pl.multiple_of(tile0[rr] * NA_KEY_TILE, NA_KEY_TILE)
            o_win.append(_dot(e_win[rr], v_s[h, pl.ds(k0, NA_SPAN), :]))
        o = jnp.concatenate(o_win, axis=0) + _dot(jnp.concatenate(e_ctx, axis=0), vc_s[h])
        return o[:, :hd] * (1.0 / o[:, hd:hd + 1])

    s0 = scores(0)
    s1 = scores(1)
    p0 = softmax(0, *s0)
    o0 = values(0, *p0)
    p1 = softmax(1, *s1)
    o1 = values(1, *p1)
    o_ref[...] = jnp.concatenate([o0, o1], axis=-1)


def _na_tables(rpb, s):
    half = NA_HD // 2
    inv = jnp.power(ROPE_THETA, -jnp.arange(0, half, 2, dtype=F32) / half)

    def tables(n):
        ang = jnp.arange(n, dtype=F32)[:, None] * inv[None, :]
        reps = 2 * NA_HD // half
        return jnp.stack([jnp.tile(jnp.cos(ang), (1, 2 * reps)),
                          jnp.tile(jnp.concatenate([-jnp.sin(ang), jnp.sin(ang)], axis=-1), (1, reps))])

    rowtab, coltab = tables(s // GRID_W), tables(GRID_W)

    pad = GRID_W - NA_WIN_C
    rp = jnp.pad(rpb.astype(F32), ((0, 0), (0, 0), (pad, pad)), mode="edge")
    t2 = jnp.stack([rp[:, :, GRID_W - 1 - qc:2 * GRID_W - 1 - qc] for qc in range(GRID_W)], axis=2)
    return rowtab, coltab, t2


def _natten(p, pc, rowtab, coltab, t2):
    _, b, s, w = p.shape
    ctx_len = pc.shape[2]
    rows = s // GRID_W
    assert rows >= NA_WIN_R and rows % NA_ROWS_PER_STEP == 0
    tq = NA_ROWS_PER_STEP * GRID_W
    hw = 2 * NA_HD
    nhp = w // hw
    kern = functools.partial(_natten_kernel, rows=rows)
    return pl.pallas_call(
        kern,
        grid=(b, nhp, rows // NA_ROWS_PER_STEP),
        in_specs=[pl.BlockSpec((None, None, tq, hw), lambda bi, hp, r: (SEC_NQ, bi, r, hp)),
                  pl.BlockSpec((None, None, s, hw), lambda bi, hp, r: (SEC_NK, bi, 0, hp)),
                  pl.BlockSpec((None, None, s, hw), lambda bi, hp, r: (SEC_NV, bi, 0, hp)),
                  pl.BlockSpec((None, None, ctx_len, hw), lambda bi, hp, r: (SEC_NK, bi, 0, hp)),
                  pl.BlockSpec((None, None, ctx_len, hw), lambda bi, hp, r: (SEC_NV, bi, 0, hp)),
                  pl.BlockSpec((2, rows, hw), lambda bi, hp, r: (0, 0, 0)),
                  pl.BlockSpec((2, GRID_W, hw), lambda bi, hp, r: (0, 0, 0)),
                  pl.BlockSpec((2, 2 * NA_WIN_R - 1, GRID_W, GRID_W), lambda bi, hp, r: (hp, 0, 0, 0))],
        out_specs=pl.BlockSpec((None, tq, hw), lambda bi, hp, r: (bi, r, hp)),
        out_shape=jax.ShapeDtypeStruct((b, s, w), F32),
        scratch_shapes=[pltpu.VMEM((2, s // NA_KEY_TILE + 1, NA_HD, NA_KEY_TILE), BF16),
                        pltpu.VMEM((2, s + NA_KEY_TILE, hw), BF16),
                        pltpu.VMEM((2, ctx_len, NA_HD), BF16), pltpu.VMEM((2, ctx_len, hw), BF16),
                        pltpu.VMEM((2, NA_WIN_R + 1, GRID_W, NA_SPAN), F32)],
        compiler_params=_cparams(("arbitrary", "arbitrary", "arbitrary")),
        name="natten",
    )(p, p, p, pc, pc, rowtab, coltab, t2)


def _route(logits_t, rbias):
    e, t = logits_t.shape
    gsz = e // N_GROUPS
    scores = jax.nn.sigmoid(logits_t)
    sel = scores + rbias
    neg = -jnp.inf
    sub = lax.broadcasted_iota(jnp.int32, (gsz, t), 0).astype(F32)
    gscore = []
    for g in range(N_GROUPS):
        grp = sel[g * gsz:(g + 1) * gsz, :]
        m1 = jnp.max(grp, axis=0, keepdims=True)
        first = jnp.min(jnp.where(grp == m1, sub, float(gsz)), axis=0, keepdims=True)
        m2 = jnp.max(jnp.where(sub == first, neg, grp), axis=0, keepdims=True)
        gscore.append(m1 + m2)
    masked = []
    for g in range(N_GROUPS):
        rank = jnp.zeros((1, t), F32)
        for g2 in range(N_GROUPS):
            if g2 == g:
                continue
            if g2 < g:
                ahead = gscore[g2] >= gscore[g]
            else:
                ahead = gscore[g2] > gscore[g]
            rank = rank + jnp.where(ahead, 1.0, 0.0)
        masked.append(jnp.where(rank < TOPK_GROUPS, sel[g * gsz:(g + 1) * gsz, :], neg))
    work = jnp.concatenate(masked, axis=0)
    eidx = lax.broadcasted_iota(jnp.int32, (e, t), 0).astype(F32)
    idxs, ws = [], []
    chosen = jnp.zeros((e, t), F32)
    for _ in range(TOP_K):
        m = jnp.max(work, axis=0, keepdims=True)
        first = jnp.min(jnp.where(work == m, eidx, float(e)), axis=0, keepdims=True)
        pick = eidx == first
        idxs.append(first)
        ws.append(jnp.sum(jnp.where(pick, scores, 0.0), axis=0, keepdims=True))
        chosen = jnp.where(pick, 1.0, chosen)
        work = jnp.where(pick, neg, work)
    w = jnp.concatenate(ws, axis=0)
    w = w / jnp.sum(w, axis=0, keepdims=True) * ROUTED_SCALE
    return jnp.concatenate(idxs, axis=0).astype(jnp.int32), w, chosen


MERGE_TOK = 512
MERGE_SUB = 256


def _merge_kernel(of_ref, ob_ref, og_ref, yna_ref, ga_ref, gb_ref, x_ref, g1_ref, sh2_ref, sc2_ref,
                  hgg_ref, ln1g_ref, ln1b_ref, wa_ref, wb_ref, wo_ref, wr_ref, rb_ref,
                  x1_ref, h2_ref, topi_ref, topw_ref, cnt_ref, *, alpha):
    tm = x_ref.shape[0]
    subs = [slice(i * MERGE_SUB, (i + 1) * MERGE_SUB) for i in range(tm // MERGE_SUB)]

    def branches(rows):
        o = of_ref[rows, :] + ob_ref[rows, :]
        parts = []
        for h in range(HG_HEADS):
            oh = o[:, h * HG_DK:(h + 1) * HG_DK]
            parts.append(oh * lax.rsqrt(jnp.mean(oh * oh, axis=-1, keepdims=True) + LN_EPS))
        y_hg = jnp.concatenate(parts, axis=-1) * hgg_ref[...] * _silu(og_ref[rows, :])
        return _dot(y_hg.astype(BF16), wa_ref[...]), _dot(yna_ref[rows, :].astype(BF16), wb_ref[...])

    def out_proj(rows, ya, yb):
        t = jax.nn.sigmoid(ga_ref[rows, :]) * ya + jax.nn.sigmoid(gb_ref[rows, :]) * yb
        return _dot(t.astype(BF16), wo_ref[...])

    def norms_router(rows, i, y):
        x1 = _normalize(alpha * x_ref[rows, :] + g1_ref[...] * y) * ln1g_ref[...] + ln1b_ref[...]
        x1_ref[rows, :] = x1
        h2 = _normalize(x1) * (1.0 + sc2_ref[...]) + sh2_ref[...]
        h2_ref[rows, :] = _pack_words(h2)
        hh, hm, hl = _split3(h2)
        wh, wm, wl = _split3(wr_ref[...])
        return (_dot_nt(wh, hh) + _dot_nt(wh, hm) + _dot_nt(wm, hh)
                + _dot_nt(wh, hl) + _dot_nt(wl, hh) + _dot_nt(wm, hm))

    ab = [branches(rows) for rows in subs]
    ys = [out_proj(rows, *ab[i]) for i, rows in enumerate(subs)]
    logits = [norms_router(rows, i, ys[i]) for i, rows in enumerate(subs)]

    @pl.when((pl.program_id(0) == 0) & (pl.program_id(1) == 0))
    def _():
        cnt_ref[...] = jnp.zeros_like(cnt_ref)

    for i, rows in enumerate(subs):
        topi, topw, chosen = _route(logits[i], rb_ref[...])
        topi_ref[:, rows] = topi
        topw_ref[:, rows] = topw
        cnt_ref[...] += jnp.sum(chosen, axis=1, keepdims=True)


def _merge(o_f, o_b, p, y_na, x, g1, sh2, sc2, hg_norm_g, ln1_g, ln1_b, w_a, w_b, w_o, w_router_t, router_bias,
           alpha):
    b, s, d = x.shape
    tm = min(MERGE_TOK, s)
    e = w_router_t.shape[0]
    tok = lambda bi, i: (bi, i, 0)
    blk = pl.BlockSpec((None, tm, d), tok)

    def sec(section):
        return pl.BlockSpec((None, None, tm, d), lambda bi, i: (section, bi, i, 0))

    mod = pl.BlockSpec((None, 1, d), lambda bi, i: (bi, 0, 0))
    vec = pl.BlockSpec((1, d), lambda bi, i: (0, 0))
    mat = pl.BlockSpec((d, d), lambda bi, i: (0, 0), pipeline_mode=pl.Buffered(1))
    return pl.pallas_call(
        functools.partial(_merge_kernel, alpha=alpha),
        grid=(b, s // tm),
        in_specs=[blk, blk, sec(SEC_OG), blk, sec(SEC_GA), sec(SEC_GB), blk, mod, mod, mod,
                  vec, vec, vec, mat, mat, mat,
                  pl.BlockSpec((e, d), lambda bi, i: (0, 0)),
                  pl.BlockSpec((e, 1), lambda bi, i: (0, 0))],
        out_specs=[blk,
                   pl.BlockSpec((tm, d // 2), lambda bi, i: (bi * (s // tm) + i, 0)),
                   pl.BlockSpec((None, TOP_K, tm), lambda bi, i: (bi, 0, i)),
                   pl.BlockSpec((None, TOP_K, tm), lambda bi, i: (bi, 0, i)),
                   pl.BlockSpec((e, 128), lambda bi, i: (0, 0))],
        out_shape=[jax.ShapeDtypeStruct((b, s, d), F32),
                   jax.ShapeDtypeStruct((b * s, d // 2), U32),
                   jax.ShapeDtypeStruct((b, TOP_K, s), jnp.int32), jax.ShapeDtypeStruct((b, TOP_K, s), F32),
                   jax.ShapeDtypeStruct((e, 128), F32)],
        compiler_params=_cparams(("arbitrary", "arbitrary")),
        name="merge",
    )(o_f, o_b, p, y_na, p, p, x, g1, sh2, sc2, hg_norm_g.reshape(1, d), ln1_g.reshape(1, d),
      ln1_b.reshape(1, d), w_a, w_b, w_o, w_router_t, router_bias.reshape(e, 1))


MOE_TILE = 512
MOE_TOK = 512


def _plan_kernel(topi_ref, off_ref, dest_ref, carry_ref):
    @pl.when(pl.program_id(0) == 0)
    def _():
        carry_ref[...] = jnp.zeros_like(carry_ref)

    topi = topi_ref[...]
    tok = topi.shape[1]
    eidx = lax.broadcasted_iota(jnp.int32, (N_EXPERTS, tok), 0)
    hits = [eidx == topi[k:k + 1, :] for k in range(TOP_K)]
    m = jnp.zeros((N_EXPERTS, tok), F32)
    for hit in hits:
        m = jnp.where(hit, 1.0, m)
    before = (lax.broadcasted_iota(jnp.int32, (tok, tok), 0)
              < lax.broadcasted_iota(jnp.int32, (tok, tok), 1)).astype(F32).astype(BF16)
    row = off_ref[...] + carry_ref[...] + _dot(m.astype(BF16), before)
    dest = [jnp.sum(jnp.where(hit, row, 0.0), axis=0, keepdims=True) for hit in hits]
    dest_ref[...] = jnp.concatenate(dest, axis=0).astype(jnp.int32)
    carry_ref[...] += jnp.sum(m, axis=1, keepdims=True)


def _plan(topi, seg_off):
    b, k, s = topi.shape
    per_b = s // MOE_TOK
    blk = pl.BlockSpec((None, k, MOE_TOK), lambda i: (i // per_b, 0, i % per_b))
    return pl.pallas_call(
        _plan_kernel,
        grid=(b * per_b,),
        in_specs=[blk, pl.BlockSpec((N_EXPERTS, 1), lambda i: (0, 0))],
        out_specs=blk,
        out_shape=jax.ShapeDtypeStruct((b, k, s), jnp.int32),
        scratch_shapes=[pltpu.VMEM((N_EXPERTS, 1), F32)],
        compiler_params=_cparams(("arbitrary",)),
        name="plan",
    )(topi, seg_off.astype(F32).reshape(N_EXPERTS, 1))


SC_WINDOW = 128


def _sc_workers():
    info = plsc.get_sparse_core_info()
    return info.num_cores, info.num_cores * info.num_subcores


def _scatter_rows(src, idx, zero_idx, n_rows):
    t, w = src.shape
    m, mz = idx.shape[0], zero_idx.shape[0]
    n_cores, n_workers = _sc_workers()
    per_worker, per_worker_z = t // n_workers, mz // n_workers
    assert m % t == 0
    assert per_worker * n_workers == t and per_worker % SC_WINDOW == 0
    assert per_worker_z * n_workers == mz and per_worker_z % SC_WINDOW == 0
    mesh = plsc.VectorSubcoreMesh(core_axis_name="core", subcore_axis_name="subcore")

    @functools.partial(
        pl.kernel, mesh=mesh, out_type=jax.ShapeDtypeStruct((n_rows, w), src.dtype),
        scratch_types=[pltpu.VMEM((SC_WINDOW,), jnp.int32), pltpu.VMEM((SC_WINDOW, w), src.dtype),
                       pltpu.SemaphoreType.DMA])
    def scatter(src_hbm, idx_hbm, zeros_hbm, zero_idx_hbm, out_hbm, idx_v, rows_v, sem):
        worker = lax.axis_index("subcore") * n_cores + lax.axis_index("core")

        @pl.loop(0, per_worker // SC_WINDOW)
        def _(step):
            first = pl.multiple_of(worker * per_worker + step * SC_WINDOW, SC_WINDOW)
            pltpu.sync_copy(src_hbm.at[pl.ds(first, SC_WINDOW)], rows_v)
            for copy in range(m // t):
                pltpu.sync_copy(idx_hbm.at[pl.ds(copy * t + first, SC_WINDOW)], idx_v)
                pltpu.async_copy(rows_v, out_hbm.at[idx_v], sem).wait()

        pltpu.sync_copy(zeros_hbm, rows_v)

        @pl.loop(0, per_worker_z // SC_WINDOW)
        def _(step):
            base = pl.multiple_of(worker * per_worker_z + step * SC_WINDOW, SC_WINDOW)
            pltpu.sync_copy(zero_idx_hbm.at[pl.ds(base, SC_WINDOW)], idx_v)
            pltpu.async_copy(rows_v, out_hbm.at[idx_v], sem).wait()

    return scatter(src, idx, jnp.zeros((SC_WINDOW, w), src.dtype), zero_idx)


EXPERT_RING = 3


def _experts_kernel(te_ref, tb_ref, nt_ref, xs_ref, wg_ref, wu_ref, wd_ref, ys_ref, xbuf, sem):
    i = pl.program_id(0)
    n_tiles = nt_ref[0]

    def tile_copy(j):
        slot = lax.rem(j, EXPERT_RING)
        row0 = pl.multiple_of(tb_ref[j] * MOE_TILE, MOE_TILE)
        return pltpu.make_async_copy(xs_ref.at[pl.ds(row0, MOE_TILE), :], xbuf.at[slot], sem.at[slot])

    @pl.when(i == 0)
    def _():
        for j in range(EXPERT_RING - 1):
            @pl.when(j < n_tiles)
            def _():
                tile_copy(j).start()

    ahead = i + (EXPERT_RING - 1)

    @pl.when(ahead < n_tiles)
    def _():
        tile_copy(ahead).start()

    @pl.when(i < n_tiles)
    def _():
        tile_copy(i).wait()
        x = _unpack_words(xbuf[lax.rem(i, EXPERT_RING)]).astype(BF16)
        act = _silu(_dot(x, wg_ref[...].astype(BF16))) * _dot(x, wu_ref[...].astype(BF16))
        ys_ref[...] = _pack_words(_dot(act.astype(BF16), wd_ref[...].astype(BF16)))


def _experts(xs, tile_expert, tile_block, n_tiles, wg, wu, wd):
    d, f = wg.shape[1], wg.shape[2]
    grid_spec = pltpu.PrefetchScalarGridSpec(
        num_scalar_prefetch=3,
        grid=(xs.shape[0] // MOE_TILE,),
        in_specs=[pl.BlockSpec(memory_space=pl.ANY),
                  pl.BlockSpec((None, d, f), lambda i, te, tb, nt: (te[i], 0, 0)),
                  pl.BlockSpec((None, d, f), lambda i, te, tb, nt: (te[i], 0, 0)),
                  pl.BlockSpec((None, f, d), lambda i, te, tb, nt: (te[i], 0, 0))],
        out_specs=pl.BlockSpec((MOE_TILE, d // 2), lambda i, te, tb, nt: (tb[i], 0)),
        scratch_shapes=[pltpu.VMEM((EXPERT_RING, MOE_TILE, d // 2), U32), pltpu.SemaphoreType.DMA((EXPERT_RING,))],
    )
    return pl.pallas_call(
        _experts_kernel,
        grid_spec=grid_spec,
        out_shape=jax.ShapeDtypeStruct(xs.shape, U32),
        compiler_params=_cparams(("arbitrary",)),
        name="experts",
    )(tile_expert, tile_block, n_tiles, xs, wg, wu, wd)


def _gather_rows(table, idx):
    m = idx.shape[0]
    w = table.shape[1]
    win = SC_WINDOW // 2
    n_cores, n_workers = _sc_workers()
    per_worker = m // n_workers
    n_pairs = per_worker // (2 * win)
    assert per_worker * n_workers == m and n_pairs * 2 * win == per_worker
    mesh = plsc.VectorSubcoreMesh(core_axis_name="core", subcore_axis_name="subcore")

    @functools.partial(
        pl.kernel, mesh=mesh, out_type=jax.ShapeDtypeStruct((m, w), table.dtype),
        scratch_types=[pltpu.VMEM((win,), jnp.int32), pltpu.VMEM((win,), jnp.int32),
                       pltpu.VMEM((win, w), table.dtype), pltpu.VMEM((win, w), table.dtype),
                       pltpu.SemaphoreType.DMA, pltpu.SemaphoreType.DMA])
    def gather(table_hbm, idx_hbm, out_hbm, idx_a, idx_b, rows_a, rows_b, sem_a, sem_b):
        worker = lax.axis_index("subcore") * n_cores + lax.axis_index("core")
        start = worker * per_worker

        def request(first, idx_v, rows_v, sem):
            pltpu.sync_copy(idx_hbm.at[pl.ds(first, win)], idx_v)
            pltpu.async_copy(table_hbm.at[idx_v], rows_v, sem)

        def deliver(first, idx_v, rows_v, sem):
            pltpu.make_async_copy(table_hbm.at[idx_v], rows_v, sem).wait()
            pltpu.sync_copy(rows_v, out_hbm.at[pl.ds(first, win)])

        request(pl.multiple_of(start, win), idx_a, rows_a, sem_a)

        @pl.loop(0, n_pairs)
        def _(pair):
            first_a = pl.multiple_of(start + pair * 2 * win, win)
            first_b = pl.multiple_of(first_a + win, win)
            request(first_b, idx_b, rows_b, sem_b)
            deliver(first_a, idx_a, rows_a, sem_a)

            @pl.when(pair + 1 < n_pairs)
            def _():
                request(pl.multiple_of(first_b + win, win), idx_a, rows_a, sem_a)

            deliver(first_b, idx_b, rows_b, sem_b)

    return gather(table, idx)


def _combine_kernel(rows_ref, topw_ref, h_ref, x1_ref, g2_ref, sg_ref, su_ref, sd_ref, ln2g_ref, ln2b_ref,
                    o_ref, *, alpha):
    h = _unpack_words(h_ref[...]).astype(BF16)
    act = _silu(_dot(h, sg_ref[...])) * _dot(h, su_ref[...])
    y = _dot(act.astype(BF16), sd_ref[...])
    w = topw_ref[...].T
    for k in range(TOP_K):
        y = y + w[:, k:k + 1] * _unpack_words(rows_ref[k])
    o_ref[...] = _normalize(alpha * x1_ref[...] + g2_ref[...] * y) * ln2g_ref[...] + ln2b_ref[...]


def _combine(gathered, topw, h2p, x1, g2, sg, su, sd, ln2_g, ln2_b, alpha):
    t, d = x1.shape
    b, k, s = topw.shape
    per_b = s // MOE_TOK
    fs = sg.shape[1]
    rows = pl.BlockSpec((MOE_TOK, d), lambda i: (i, 0))
    packed = pl.BlockSpec((MOE_TOK, d // 2), lambda i: (i, 0))
    vec = pl.BlockSpec((1, d), lambda i: (0, 0))
    return pl.pallas_call(
        functools.partial(_combine_kernel, alpha=alpha),
        grid=(t // MOE_TOK,),
        in_specs=[pl.BlockSpec((k, MOE_TOK, d // 2), lambda i: (0, i, 0)),
                  pl.BlockSpec((None, k, MOE_TOK), lambda i: (i // per_b, 0, i % per_b)),
                  packed, rows,
                  pl.BlockSpec((None, 1, d), lambda i: (i // per_b, 0, 0)),
                  pl.BlockSpec((d, fs), lambda i: (0, 0)),
                  pl.BlockSpec((d, fs), lambda i: (0, 0)),
                  pl.BlockSpec((fs, d), lambda i: (0, 0)),
                  vec, vec],
        out_specs=rows,
        out_shape=jax.ShapeDtypeStruct((t, d), F32),
        compiler_params=_cparams(("arbitrary",)),
        name="combine",
    )(gathered, topw, h2p, x1, g2, sg, su, sd, ln2_g.reshape(1, d), ln2_b.reshape(1, d))


def _moe(h2p, topi, topw, cnt, x1, g2, wg, wu, wd, sg, su, sd, ln2_g, ln2_b, alpha):
    b, s, d = x1.shape
    t = b * s
    cnt = cnt[:, 0].astype(jnp.int32)
    tiles_e = (cnt + (MOE_TILE - 1)) // MOE_TILE
    tiles_cum = jnp.cumsum(tiles_e)
    seg_off = (tiles_cum - tiles_e) * MOE_TILE
    n_tiles_max = t * TOP_K // MOE_TILE + N_EXPERTS
    tile_block = jnp.minimum(jnp.arange(n_tiles_max, dtype=jnp.int32), tiles_cum[-1] - 1)
    tile_expert = jnp.sum((tiles_cum[None, :] <= tile_block[:, None]).astype(jnp.int32), axis=1)
    n_tiles = tiles_cum[-1:].astype(jnp.int32)

    dest = jnp.transpose(_plan(topi, seg_off), (1, 0, 2)).reshape(TOP_K * t)
    j = jnp.arange(MOE_TILE, dtype=jnp.int32)[None, :]
    n_pad = (tiles_e * MOE_TILE - cnt)[:, None]
    spare = (n_tiles_max - 1) * MOE_TILE + j
    zero_idx = jnp.where(j < n_pad, (seg_off + cnt)[:, None] + j, spare).reshape(N_EXPERTS * MOE_TILE)
    xs = _scatter_rows(h2p, dest, zero_idx.astype(jnp.int32), n_tiles_max * MOE_TILE)
    ys = _experts(xs, tile_expert, tile_block, n_tiles, wg, wu, wd)
    gathered = _gather_rows(ys, dest)
    out = _combine(gathered.reshape(TOP_K, t, d // 2), topw, h2p, x1.reshape(t, d), g2, sg, su, sd,
                   ln2_g, ln2_b, alpha)
    return out.reshape(b, s, d)


def kernel(x, c, ctx, c_ctx, w_ada, b_ada, w_in, hg_lb_fwd, hg_lb_bwd, hg_norm_g, na_rpb, w_branch_a, w_branch_b, w_out, ln1_g, ln1_b, w_router, router_bias, w_e_gate, w_e_up, w_e_down, w_sh_gate, w_sh_up, w_sh_down, ln2_g, ln2_b):
    depth = w_ada.shape[0]
    assert depth == 1, "single-layer block"
    b, s, d = x.shape
    alpha = (2.0 * depth) ** 0.25
    l = 0
    lb_fwd = jnp.cumsum(jax.nn.softmax(hg_lb_fwd.astype(F32), axis=0), axis=0)[l]
    lb_bwd = jnp.cumsum(jax.nn.softmax(hg_lb_bwd.astype(F32), axis=0), axis=0)[l]

    cond_rows = jnp.concatenate([c, c_ctx[None, :], jnp.zeros((8 - b - 1, d), F32)], axis=0)
    mod = _ada(cond_rows, w_ada[l], b_ada[l])
    sh1, sc1, g1, sh2, sc2, g2 = [m[:b, None, :] for m in jnp.split(mod, 6, axis=-1)]
    csh1, csc1 = [jnp.broadcast_to(m[b:b + 1, None, :], (b, 1, d)) for m in jnp.split(mod, 6, axis=-1)[:2]]

    w_in_b = w_in[l].astype(BF16)
    p = _inproj(x, sh1, sc1, w_in_b)
    pc = _inproj(ctx, csh1, csc1, w_in_b)

    o_f, o_b = _hgrn(p, pc, lb_fwd, lb_bwd)
    y_na = _natten(p, pc, *_na_tables(na_rpb[l], s))

    x1, h2, topi, topw, cnt = _merge(o_f, o_b, p, y_na, x, g1, sh2, sc2, hg_norm_g[l], ln1_g[l], ln1_b[l],
                                     w_branch_a[l].astype(BF16), w_branch_b[l].astype(BF16),
                                     w_out[l].astype(BF16), w_router[l].T, router_bias[l], alpha)

    return _moe(h2, topi, topw, cnt, x1, g2,
                w_e_gate[l], w_e_up[l], w_e_down[l],
                w_sh_gate[l].astype(BF16), w_sh_up[l].astype(BF16), w_sh_down[l].astype(BF16),
                ln2_g[l], ln2_b[l], alpha)
```

```python
import functools

import numpy as np
import jax
import jax.numpy as jnp
from jax import lax
from jax.experimental import pallas as pl
from jax.experimental.pallas import tpu as pltpu
from jax.experimental.pallas import tpu_sc as plsc

F32 = jnp.float32
BF16 = jnp.bfloat16

D_MODEL = 1024
GRID_W = 64
HG_HEADS = 8
HG_DK = 128
HG_CHUNK = 64
NA_HEADS = 16
NA_HD = 64
NA_WIN_R = 8
NA_WIN_C = 16
ROPE_THETA = 10000.0
NEG_INF = -1e30
N_EXPERTS = 64
EXPERT_DIM = 256
TOP_K = 8
N_GROUPS = 8
TOPK_GROUPS = 4
ROUTED_SCALE = 2.5
LN_EPS = 1e-6
N_SECTIONS = 10
SEC_Q, SEC_FF, SEC_FB, SEC_I, SEC_OG, SEC_NQ, SEC_NK, SEC_NV, SEC_GA, SEC_GB = range(10)

VMEM_LIMIT = 56 * 1024 * 1024


def _cparams(sem):
    return pltpu.CompilerParams(dimension_semantics=sem, vmem_limit_bytes=VMEM_LIMIT)


def _normalize(x):
    mu = jnp.mean(x, axis=-1, keepdims=True)
    xc = x - mu
    var = jnp.mean(xc * xc, axis=-1, keepdims=True)
    return xc * lax.rsqrt(var + LN_EPS)


def _silu(x):
    return x * jax.nn.sigmoid(x)


def _dot(a, b):
    return jnp.dot(a, b, preferred_element_type=F32)


def _dot_nt(a, b):
    return lax.dot_general(a, b, (((1,), (1,)), ((), ())), preferred_element_type=F32)


def _dot_tn(a, b):
    return lax.dot_general(a, b, (((0,), (0,)), ((), ())), preferred_element_type=F32)


U32 = jnp.uint32


def _pack_words(x):
    half = x.shape[1] // 2
    lo = lax.bitcast_convert_type(x[:, :half].astype(BF16).astype(F32), U32) >> 16
    hi = lax.bitcast_convert_type(x[:, half:].astype(BF16).astype(F32), U32) & jnp.uint32(0xFFFF0000)
    return lo | hi


def _unpack_words(w):
    lo = lax.bitcast_convert_type(w << 16, F32)
    hi = lax.bitcast_convert_type(w & jnp.uint32(0xFFFF0000), F32)
    return jnp.concatenate([lo, hi], axis=-1)


def _split3(x):
    hi = x.astype(BF16)
    r1 = x - hi.astype(F32)
    mid = r1.astype(BF16)
    lo = (r1 - mid.astype(F32)).astype(BF16)
    return hi, mid, lo


def _ada_kernel(c_ref, w_ref, b_ref, o_ref):
    cond = _silu(c_ref[...])
    o_ref[...] = _dot(cond.astype(BF16), w_ref[...].astype(BF16)) + b_ref[...]


def _ada(cond_rows, w_ada, b_ada):
    r, d = cond_rows.shape
    n = w_ada.shape[1]
    tn = 1024
    return pl.pallas_call(
        _ada_kernel,
        grid=(n // tn,),
        in_specs=[pl.BlockSpec((r, d), lambda j: (0, 0)),
                  pl.BlockSpec((d, tn), lambda j: (0, j)),
                  pl.BlockSpec((1, tn), lambda j: (0, j))],
        out_specs=pl.BlockSpec((r, tn), lambda j: (0, j)),
        out_shape=jax.ShapeDtypeStruct((r, n), F32),
        compiler_params=_cparams(("arbitrary",)),
        name="ada",
    )(cond_rows, w_ada, b_ada.reshape(1, n))


INPROJ_TOK = 2048


def _inproj_kernel(x_ref, sh_ref, sc_ref, w_ref, o_ref, h_ref):
    @pl.when(pl.program_id(2) == 0)
    def _():
        h = _normalize(x_ref[...]) * (1.0 + sc_ref[...]) + sh_ref[...]
        h_ref[...] = h.astype(BF16)

    o_ref[...] = _dot(h_ref[...], w_ref[...])


def _inproj(x, shift, scale, w_in_bf16):
    b, s, d = x.shape
    tm = min(INPROJ_TOK, s)
    nj = w_in_bf16.shape[1] // d
    return pl.pallas_call(
        _inproj_kernel,
        grid=(b, s // tm, nj),
        in_specs=[pl.BlockSpec((None, tm, d), lambda bi, i, j: (bi, i, 0)),
                  pl.BlockSpec((None, 1, d), lambda bi, i, j: (bi, 0, 0)),
                  pl.BlockSpec((None, 1, d), lambda bi, i, j: (bi, 0, 0)),
                  pl.BlockSpec((d, d), lambda bi, i, j: (0, j))],
        out_specs=pl.BlockSpec((None, None, tm, d), lambda bi, i, j: (j, bi, i, 0)),
        out_shape=jax.ShapeDtypeStruct((nj, b, s, d), F32),
        scratch_shapes=[pltpu.VMEM((tm, d), BF16)],
        compiler_params=_cparams(("arbitrary", "arbitrary", "arbitrary")),
        name="inproj",
    )(x, shift, scale, w_in_bf16)


def _hgrn_gates(q, fraw, v, lb, tri_bf16, last_row):
    f = lb + (1.0 - lb) * jax.nn.sigmoid(fraw)
    k = 1.0 - f
    lf = jnp.log(f)
    hi, mid, lo = _split3(lf)
    a = _dot(tri_bf16, hi) + _dot(tri_bf16, mid) + _dot(tri_bf16, lo)
    a_last = a[last_row:last_row + 1, :]
    kd = (k * jnp.exp(a_last - a)).astype(BF16)
    decay = jnp.exp(a_last)
    qa = kb = None
    if q is not None:
        qa = (_silu(q) * jnp.exp(a)).astype(BF16)
        kb = (k * jnp.exp(-a)).astype(BF16)
    return qa, kb, kd, v.astype(BF16), decay


def _hgrn_chunks(chunks, st_ref):
    first = []
    for d, ((qa, kb, kd, vb, decay), keep) in enumerate(chunks):
        for h in range(HG_HEADS):
            sl = slice(h * HG_DK, (h + 1) * HG_DK)
            st = st_ref[d, h]
            if qa is not None:
                first.append((_dot_nt(qa[:, sl], kb[:, sl]), _dot_nt(qa[:, sl], st.astype(BF16))))
            st_ref[d, h] = st * decay[:, sl] + _dot_tn(vb[:, sl], kd[:, sl])
    results = []
    for d, ((qa, kb, kd, vb, decay), keep) in enumerate(chunks):
        if qa is None:
            results.append(None)
            continue
        outs = []
        for h in range(HG_HEADS):
            sl = slice(h * HG_DK, (h + 1) * HG_DK)
            s_qk, o_state = first.pop(0)
            outs.append(_dot(jnp.where(keep, s_qk, 0.0).astype(BF16), vb[:, sl]) + o_state)
        results.append(jnp.concatenate(outs, axis=-1))
    return results


def _hgrn_kernel(qf_ref, ff_ref, if_ref, qb_ref, fb_ref, ib_ref, cff_ref, cfb_ref, ci_ref,
                 lbf_ref, lbb_ref, of_ref, ob_ref, st_ref, *, n_sub, n_ctx_sub):
    n = pl.program_id(1)
    c = HG_CHUNK
    row = lax.broadcasted_iota(jnp.int32, (c, c), 0)
    col = lax.broadcasted_iota(jnp.int32, (c, c), 1)
    keep_f = col <= row
    keep_b = col >= row
    tri_f = keep_f.astype(F32).astype(BF16)
    tri_b = keep_b.astype(F32).astype(BF16)
    lbf = lbf_ref[...]
    lbb = lbb_ref[...]

    @pl.when(n == 0)
    def _():
        st_ref[...] = jnp.zeros_like(st_ref)

        def body(i, carry):
            r0 = pl.multiple_of(i * c, c)
            r1 = pl.multiple_of((n_ctx_sub - 1 - i) * c, c)
            gf = _hgrn_gates(None, cff_ref[pl.ds(r0, c), :], ci_ref[pl.ds(r0, c), :], lbf, tri_f, c - 1)
            gb = _hgrn_gates(None, cfb_ref[pl.ds(r1, c), :], ci_ref[pl.ds(r1, c), :], lbb, tri_b, 0)
            _hgrn_chunks([(gf, keep_f), (gb, keep_b)], st_ref)
            return carry

        lax.fori_loop(0, n_ctx_sub, body, 0)

    @pl.when(n > 0)
    def _():
        def body(i, carry):
            r0 = pl.multiple_of(i * c, c)
            r1 = pl.multiple_of((n_sub - 1 - i) * c, c)
            gf = _hgrn_gates(qf_ref[pl.ds(r0, c), :], ff_ref[pl.ds(r0, c), :], if_ref[pl.ds(r0, c), :],
                             lbf, tri_f, c - 1)
            gb = _hgrn_gates(qb_ref[pl.ds(r1, c), :], fb_ref[pl.ds(r1, c), :], ib_ref[pl.ds(r1, c), :],
                             lbb, tri_b, 0)
            o_f, o_b = _hgrn_chunks([(gf, keep_f), (gb, keep_b)], st_ref)
            of_ref[pl.ds(r0, c), :] = o_f
            ob_ref[pl.ds(r1, c), :] = o_b
            return carry

        lax.fori_loop(0, n_sub, body, 0, unroll=True)


def _hgrn(p, pc, lb_fwd, lb_bwd):
    _, b, s, w = p.shape
    ctx_len = pc.shape[2]
    tb = min(256, s)
    nb = s // tb
    fwd = lambda bi, n: jnp.maximum(n - 1, 0)
    bwd = lambda bi, n: nb - 1 - jnp.maximum(n - 1, 0)

    def sec(section, blk):
        return pl.BlockSpec((None, None, tb, w), lambda bi, n: (section, bi, blk(bi, n), 0))

    def csec(section):
        return pl.BlockSpec((None, None, ctx_len, w), lambda bi, n: (section, bi, 0, 0))

    vec = pl.BlockSpec((1, w), lambda bi, n: (0, 0))
    kern = functools.partial(_hgrn_kernel, n_sub=tb // HG_CHUNK, n_ctx_sub=ctx_len // HG_CHUNK)
    return pl.pallas_call(
        kern,
        grid=(b, nb + 1),
        in_specs=[sec(SEC_Q, fwd), sec(SEC_FF, fwd), sec(SEC_I, fwd),
                  sec(SEC_Q, bwd), sec(SEC_FB, bwd), sec(SEC_I, bwd),
                  csec(SEC_FF), csec(SEC_FB), csec(SEC_I), vec, vec],
        out_specs=[pl.BlockSpec((None, tb, w), lambda bi, n: (bi, fwd(bi, n), 0)),
                   pl.BlockSpec((None, tb, w), lambda bi, n: (bi, bwd(bi, n), 0))],
        out_shape=[jax.ShapeDtypeStruct((b, s, w), F32), jax.ShapeDtypeStruct((b, s, w), F32)],
        scratch_shapes=[pltpu.VMEM((2, HG_HEADS, HG_DK, HG_DK), F32)],
        compiler_params=_cparams(("arbitrary", "arbitrary")),
        name="hgrn",
    )(p, p, p, p, p, p, pc, pc, pc, lb_fwd.reshape(1, w), lb_bwd.reshape(1, w))


NA_ROWS_PER_STEP = 32
NA_PREP_ROWS = 512
NA_KEY_TILE = 128
NA_SPAN = NA_WIN_R * GRID_W


def _rope(t, cos, sin_signed, first_half):
    w = t.shape[-1]
    partner = jnp.where(first_half, pltpu.roll(t, w - 16, 1), pltpu.roll(t, 16, 1))
    return t * cos + partner * sin_signed


def _fold_lanes(op, *arrays):
    tiles = [a[:, c:c + 128] for a in arrays for c in range(0, a.shape[-1], 128)]
    acc = tiles[0]
    for t in tiles[1:]:
        acc = op(acc, t)
    return acc


def _rope_tables(rowtab_ref, coltab_ref, row0, n_rows, row_lane):
    out = []
    for i in range(2):
        rt = rowtab_ref[i, pl.ds(row0, n_rows), :]
        by_row = jnp.concatenate([jnp.broadcast_to(rt[r:r + 1, :], (GRID_W, rt.shape[1])) for r in range(n_rows)],
                                 axis=0)
        by_col = jnp.concatenate([coltab_ref[i]] * n_rows, axis=0)
        out.append(jnp.where(row_lane, by_row, by_col))
    return out


def _natten_kernel(q_ref, k_ref, v_ref, kc_ref, vc_ref, rowtab_ref, coltab_ref, t2_ref, o_ref,
                   kt_s, v_s, kc_s, vc_s, bias_s, tail_s, *, rows):
    rblk = pl.program_id(2)
    hd = NA_HD
    lane = lax.broadcasted_iota(jnp.int32, (1, 2 * hd), 1)
    first_half = (lane % 32) < 16
    row_lane = (lane % hd) < hd // 2
    scale = NA_HD ** -0.5

    def values_and_ones(v_pair, h):
        vh = v_pair if h == 0 else pltpu.roll(v_pair, hd, 1)
        return jnp.where(lane < hd, vh, jnp.where(lane == hd, 1.0, 0.0)).astype(BF16)

    @pl.when(rblk == 0)
    def _():
        kc = kc_ref[...].astype(BF16)
        qi = lax.broadcasted_iota(jnp.int32, (GRID_W, GRID_W), 0)
        ki = lax.broadcasted_iota(jnp.int32, (GRID_W, GRID_W), 1)
        cstart = jnp.clip(qi - NA_WIN_C // 2, 0, GRID_W - NA_WIN_C)
        in_win = (ki >= cstart) & (ki < cstart + NA_WIN_C)
        s_len = k_ref.shape[0]
        tail_s[...] = jnp.zeros_like(tail_s)
        for h in range(2):
            sl = slice(h * hd, (h + 1) * hd)
            kc_s[h] = kc[:, sl]
            vc_s[h] = values_and_ones(vc_ref[...], h)
            tiles = [jnp.where(in_win, t2_ref[h, dr], NEG_INF) for dr in range(2 * NA_WIN_R - 1)]
            for v in range(NA_WIN_R):
                for j in range(NA_WIN_R):
                    bias_s[h, v, :, j * GRID_W:(j + 1) * GRID_W] = tiles[NA_WIN_R - 1 - v + j]

        eye = (lax.broadcasted_iota(jnp.int32, (2 * hd, 2 * hd), 0)
               == lax.broadcasted_iota(jnp.int32, (2 * hd, 2 * hd), 1)).astype(F32).astype(BF16)

        def prep(i, carry):
            r0 = pl.multiple_of(i * NA_PREP_ROWS, NA_PREP_ROWS)
            rws = pl.ds(r0, NA_PREP_ROWS)
            cos, sin = _rope_tables(rowtab_ref, coltab_ref, i * (NA_PREP_ROWS // GRID_W), NA_PREP_ROWS // GRID_W,
                                    row_lane)
            kr = _rope(k_ref[rws, :], cos, sin, first_half)
            kr_odd = jnp.concatenate([tail_s[...], kr[:NA_PREP_ROWS - GRID_W]], axis=0)
            tail_s[...] = kr[NA_PREP_ROWS - GRID_W:]
            krt = [_dot_nt(eye, kr.astype(BF16)).astype(BF16),
                   _dot_nt(eye, kr_odd.astype(BF16)).astype(BF16)]
            vv = v_ref[rws, :]
            for h in range(2):
                sl = slice(h * hd, (h + 1) * hd)
                for par in range(2):
                    for c in range(NA_PREP_ROWS // NA_KEY_TILE):
                        kt_s[h, par, i * (NA_PREP_ROWS // NA_KEY_TILE) + c] = (
                            krt[par][sl, c * NA_KEY_TILE:(c + 1) * NA_KEY_TILE])
                v_s[h, rws, :] = values_and_ones(vv, h)
            return carry

        lax.fori_loop(0, s_len // NA_PREP_ROWS, prep, 0, unroll=4)

    tq = NA_ROWS_PER_STEP * GRID_W
    q = q_ref[...] * scale
    cos, sin = _rope_tables(rowtab_ref, coltab_ref, rblk * NA_ROWS_PER_STEP, NA_ROWS_PER_STEP, row_lane)
    qr = _rope(q, cos, sin, first_half)
    qb = q.astype(BF16)
    qrb = qr.astype(BF16)
    rws = [slice(rr * GRID_W, (rr + 1) * GRID_W) for rr in range(NA_ROWS_PER_STEP)]
    par, slot0, key0, bidx = [], [], [], []
    for rr in range(NA_ROWS_PER_STEP):
        r = rblk * NA_ROWS_PER_STEP + rr
        rs = jnp.clip(r - NA_WIN_R // 2, 0, rows - NA_WIN_R)
        par.append(rs & 1)
        slot0.append(lax.shift_right_logical(rs, 1) + (rs & 1))
        key0.append(pl.multiple_of(rs * GRID_W, GRID_W))
        bidx.append(r - rs)

    def scores(h):
        sl = slice(h * hd, (h + 1) * hd)
        qrb_h = qrb[:, sl]
        s_ctx_all = _dot_nt(qb[:, sl], kc_s[h])
        s_win = []
        for rr in range(NA_ROWS_PER_STEP):
            kt = kt_s[h, par[rr], pl.ds(slot0[rr], NA_SPAN // NA_KEY_TILE)]
            kt = jnp.concatenate([kt[c] for c in range(NA_SPAN // NA_KEY_TILE)], axis=-1)
            s_win.append(_dot(qrb_h[rws[rr]], kt))
        return s_win, s_ctx_all

    def softmax(h, s_win, s_ctx_all):
        e_win, e_ctx = [], []
        for rr in range(NA_ROWS_PER_STEP):
            sw = s_win[rr] + bias_s[h, bidx[rr]]
            sc = s_ctx_all[rws[rr]]
            m = jnp.max(_fold_lanes(jnp.maximum, sw, sc), axis=-1, keepdims=True)
            e_win.append(jnp.exp(sw - m).astype(BF16))
            e_ctx.append(jnp.exp(sc - m).astype(BF16))
        return e_win, e_ctx

    def values(h, e_win, e_ctx):
        o_win = []
        for rr in range(NA_ROWS_PER_STEP):
            o_win.append(_dot(e_win[rr], v_s[h, pl.ds(key0[rr], NA_SPAN), :]))
        o = jnp.concatenate(o_win, axis=0) + _dot(jnp.concatenate(e_ctx, axis=0), vc_s[h])
        return o[:, :hd] * (1.0 / o[:, hd:hd + 1])

    s0 = scores(0)
    s1 = scores(1)
    p0 = softmax(0, *s0)
    o0 = values(0, *p0)
    p1 = softmax(1, *s1)
    o1 = values(1, *p1)
    o_ref[...] = jnp.concatenate([o0, o1], axis=-1)


def _na_tables(rpb, s):
    half = NA_HD // 2
    inv = jnp.power(ROPE_THETA, -jnp.arange(0, half, 2, dtype=F32) / half)

    def tables(n):
        ang = jnp.arange(n, dtype=F32)[:, None] * inv[None, :]
        reps = 2 * NA_HD // half
        return jnp.stack([jnp.tile(jnp.cos(ang), (1, 2 * reps)),
                          jnp.tile(jnp.concatenate([-jnp.sin(ang), jnp.sin(ang)], axis=-1), (1, reps))])

    rowtab, coltab = tables(s // GRID_W), tables(GRID_W)

    pad = GRID_W - NA_WIN_C
    rp = jnp.pad(rpb.astype(F32), ((0, 0), (0, 0), (pad, pad)), mode="edge")
    t2 = jnp.stack([rp[:, :, GRID_W - 1 - qc:2 * GRID_W - 1 - qc] for qc in range(GRID_W)], axis=2)
    return rowtab, coltab, t2


def _natten(p, pc, rowtab, coltab, t2):
    _, b, s, w = p.shape
    ctx_len = pc.shape[2]
    rows = s // GRID_W
    assert rows >= NA_WIN_R and rows % NA_ROWS_PER_STEP == 0
    tq = NA_ROWS_PER_STEP * GRID_W
    hw = 2 * NA_HD
    nhp = w // hw
    kern = functools.partial(_natten_kernel, rows=rows)
    return pl.pallas_call(
        kern,
        grid=(b, nhp, rows // NA_ROWS_PER_STEP),
        in_specs=[pl.BlockSpec((None, None, tq, hw), lambda bi, hp, r: (SEC_NQ, bi, r, hp)),
                  pl.BlockSpec((None, None, s, hw), lambda bi, hp, r: (SEC_NK, bi, 0, hp)),
                  pl.BlockSpec((None, None, s, hw), lambda bi, hp, r: (SEC_NV, bi, 0, hp)),
                  pl.BlockSpec((None, None, ctx_len, hw), lambda bi, hp, r: (SEC_NK, bi, 0, hp)),
                  pl.BlockSpec((None, None, ctx_len, hw), lambda bi, hp, r: (SEC_NV, bi, 0, hp)),
                  pl.BlockSpec((2, rows, hw), lambda bi, hp, r: (0, 0, 0)),
                  pl.BlockSpec((2, GRID_W, hw), lambda bi, hp, r: (0, 0, 0)),
                  pl.BlockSpec((2, 2 * NA_WIN_R - 1, GRID_W, GRID_W), lambda bi, hp, r: (hp, 0, 0, 0))],
        out_specs=pl.BlockSpec((None, tq, hw), lambda bi, hp, r: (bi, r, hp)),
        out_shape=jax.ShapeDtypeStruct((b, s, w), F32),
        scratch_shapes=[pltpu.VMEM((2, 2, s // NA_KEY_TILE, NA_HD, NA_KEY_TILE), BF16),
                        pltpu.VMEM((2, s, hw), BF16),
                        pltpu.VMEM((2, ctx_len, NA_HD), BF16), pltpu.VMEM((2, ctx_len, hw), BF16),
                        pltpu.VMEM((2, NA_WIN_R, GRID_W, NA_SPAN), F32),
                        pltpu.VMEM((GRID_W, hw), F32)],
        compiler_params=_cparams(("arbitrary", "arbitrary", "arbitrary")),
        name="natten",
    )(p, p, p, pc, pc, rowtab, coltab, t2)


def _route(logits_t, rbias):
    e, t = logits_t.shape
    gsz = e // N_GROUPS
    scores = jax.nn.sigmoid(logits_t)
    sel = scores + rbias
    neg = -jnp.inf
    sub = lax.broadcasted_iota(jnp.int32, (gsz, t), 0).astype(F32)
    gscore = []
    for g in range(N_GROUPS):
        grp = sel[g * gsz:(g + 1) * gsz, :]
        m1 = jnp.max(grp, axis=0, keepdims=True)
        first = jnp.min(jnp.where(grp == m1, sub, float(gsz)), axis=0, keepdims=True)
        m2 = jnp.max(jnp.where(sub == first, neg, grp), axis=0, keepdims=True)
        gscore.append(m1 + m2)
    masked = []
    for g in range(N_GROUPS):
        rank = jnp.zeros((1, t), F32)
        for g2 in range(N_GROUPS):
            if g2 == g:
                continue
            if g2 < g:
                ahead = gscore[g2] >= gscore[g]
            else:
                ahead = gscore[g2] > gscore[g]
            rank = rank + jnp.where(ahead, 1.0, 0.0)
        masked.append(jnp.where(rank < TOPK_GROUPS, sel[g * gsz:(g + 1) * gsz, :], neg))
    work = jnp.concatenate(masked, axis=0)
    eidx = lax.broadcasted_iota(jnp.int32, (e, t), 0).astype(F32)
    idxs, ws = [], []
    chosen = jnp.zeros((e, t), F32)
    for _ in range(TOP_K):
        m = jnp.max(work, axis=0, keepdims=True)
        first = jnp.min(jnp.where(work == m, eidx, float(e)), axis=0, keepdims=True)
        pick = eidx == first
        idxs.append(first)
        ws.append(jnp.sum(jnp.where(pick, scores, 0.0), axis=0, keepdims=True))
        chosen = jnp.where(pick, 1.0, chosen)
        work = jnp.where(pick, neg, work)
    w = jnp.concatenate(ws, axis=0)
    w = w / jnp.sum(w, axis=0, keepdims=True) * ROUTED_SCALE
    return jnp.concatenate(idxs, axis=0).astype(jnp.int32), w, chosen


MERGE_TOK = 512
MERGE_SUB = 256


def _merge_kernel(of_ref, ob_ref, og_ref, yna_ref, ga_ref, gb_ref, x_ref, g1_ref, sh2_ref, sc2_ref,
                  hgg_ref, ln1g_ref, ln1b_ref, wa_ref, wb_ref, wo_ref, wr_ref, rb_ref,
                  x1_ref, h2_ref, topi_ref, topw_ref, cnt_ref, *, alpha):
    tm = x_ref.shape[0]
    subs = [slice(i * MERGE_SUB, (i + 1) * MERGE_SUB) for i in range(tm // MERGE_SUB)]

    def branches(rows):
        o = of_ref[rows, :] + ob_ref[rows, :]
        parts = []
        for h in range(HG_HEADS):
            oh = o[:, h * HG_DK:(h + 1) * HG_DK]
            parts.append(oh * lax.rsqrt(jnp.mean(oh * oh, axis=-1, keepdims=True) + LN_EPS))
        y_hg = jnp.concatenate(parts, axis=-1) * hgg_ref[...] * _silu(og_ref[rows, :])
        return _dot(y_hg.astype(BF16), wa_ref[...]), _dot(yna_ref[rows, :].astype(BF16), wb_ref[...])

    def out_proj(rows, ya, yb):
        t = jax.nn.sigmoid(ga_ref[rows, :]) * ya + jax.nn.sigmoid(gb_ref[rows, :]) * yb
        return _dot(t.astype(BF16), wo_ref[...])

    def norms_router(rows, i, y):
        x1 = _normalize(alpha * x_ref[rows, :] + g1_ref[...] * y) * ln1g_ref[...] + ln1b_ref[...]
        x1_ref[rows, :] = x1
        h2 = _normalize(x1) * (1.0 + sc2_ref[...]) + sh2_ref[...]
        h2_ref[rows, :] = _pack_words(h2)
        hh, hm, hl = _split3(h2)
        wh, wm, wl = _split3(wr_ref[...])
        return (_dot_nt(wh, hh) + _dot_nt(wh, hm) + _dot_nt(wm, hh)
                + _dot_nt(wh, hl) + _dot_nt(wl, hh) + _dot_nt(wm, hm))

    ab = [branches(rows) for rows in subs]
    ys = [out_proj(rows, *ab[i]) for i, rows in enumerate(subs)]
    logits = [norms_router(rows, i, ys[i]) for i, rows in enumerate(subs)]

    @pl.when((pl.program_id(0) == 0) & (pl.program_id(1) == 0))
    def _():
        cnt_ref[...] = jnp.zeros_like(cnt_ref)

    for i, rows in enumerate(subs):
        topi, topw, chosen = _route(logits[i], rb_ref[...])
        topi_ref[:, rows] = topi
        topw_ref[:, rows] = topw
        cnt_ref[...] += jnp.sum(chosen, axis=1, keepdims=True)


def _merge(o_f, o_b, p, y_na, x, g1, sh2, sc2, hg_norm_g, ln1_g, ln1_b, w_a, w_b, w_o, w_router_t, router_bias,
           alpha):
    b, s, d = x.shape
    tm = min(MERGE_TOK, s)
    e = w_router_t.shape[0]
    tok = lambda bi, i: (bi, i, 0)
    blk = pl.BlockSpec((None, tm, d), tok)

    def sec(section):
        return pl.BlockSpec((None, None, tm, d), lambda bi, i: (section, bi, i, 0))

    mod = pl.BlockSpec((None, 1, d), lambda bi, i: (bi, 0, 0))
    vec = pl.BlockSpec((1, d), lambda bi, i: (0, 0))
    mat = pl.BlockSpec((d, d), lambda bi, i: (0, 0), pipeline_mode=pl.Buffered(1))
    return pl.pallas_call(
        functools.partial(_merge_kernel, alpha=alpha),
        grid=(b, s // tm),
        in_specs=[blk, blk, sec(SEC_OG), blk, sec(SEC_GA), sec(SEC_GB), blk, mod, mod, mod,
                  vec, vec, vec, mat, mat, mat,
                  pl.BlockSpec((e, d), lambda bi, i: (0, 0)),
                  pl.BlockSpec((e, 1), lambda bi, i: (0, 0))],
        out_specs=[blk,
                   pl.BlockSpec((tm, d // 2), lambda bi, i: (bi * (s // tm) + i, 0)),
                   pl.BlockSpec((None, TOP_K, tm), lambda bi, i: (bi, 0, i)),
                   pl.BlockSpec((None, TOP_K, tm), lambda bi, i: (bi, 0, i)),
                   pl.BlockSpec((e, 128), lambda bi, i: (0, 0))],
        out_shape=[jax.ShapeDtypeStruct((b, s, d), F32),
                   jax.ShapeDtypeStruct((b * s, d // 2), U32),
                   jax.ShapeDtypeStruct((b, TOP_K, s), jnp.int32), jax.ShapeDtypeStruct((b, TOP_K, s), F32),
                   jax.ShapeDtypeStruct((e, 128), F32)],
        compiler_params=_cparams(("arbitrary", "arbitrary")),
        name="merge",
    )(o_f, o_b, p, y_na, p, p, x, g1, sh2, sc2, hg_norm_g.reshape(1, d), ln1_g.reshape(1, d),
      ln1_b.reshape(1, d), w_a, w_b, w_o, w_router_t, router_bias.reshape(e, 1))


MOE_TILE = 512
MOE_TOK = 512


def _plan_kernel(topi_ref, off_ref, dest_ref, carry_ref):
    @pl.when(pl.program_id(0) == 0)
    def _():
        carry_ref[...] = jnp.zeros_like(carry_ref)

    topi = topi_ref[...]
    tok = topi.shape[1]
    eidx = lax.broadcasted_iota(jnp.int32, (N_EXPERTS, tok), 0)
    hits = [eidx == topi[k:k + 1, :] for k in range(TOP_K)]
    m = jnp.zeros((N_EXPERTS, tok), F32)
    for hit in hits:
        m = jnp.where(hit, 1.0, m)
    before = (lax.broadcasted_iota(jnp.int32, (tok, tok), 0)
              < lax.broadcasted_iota(jnp.int32, (tok, tok), 1)).astype(F32).astype(BF16)
    row = off_ref[...] + carry_ref[...] + _dot(m.astype(BF16), before)
    dest = [jnp.sum(jnp.where(hit, row, 0.0), axis=0, keepdims=True) for hit in hits]
    dest_ref[...] = jnp.concatenate(dest, axis=0).astype(jnp.int32)
    carry_ref[...] += jnp.sum(m, axis=1, keepdims=True)


def _plan(topi, seg_off):
    b, k, s = topi.shape
    per_b = s // MOE_TOK
    blk = pl.BlockSpec((None, k, MOE_TOK), lambda i: (i // per_b, 0, i % per_b))
    return pl.pallas_call(
        _plan_kernel,
        grid=(b * per_b,),
        in_specs=[blk, pl.BlockSpec((N_EXPERTS, 1), lambda i: (0, 0))],
        out_specs=blk,
        out_shape=jax.ShapeDtypeStruct((b, k, s), jnp.int32),
        scratch_shapes=[pltpu.VMEM((N_EXPERTS, 1), F32)],
        compiler_params=_cparams(("arbitrary",)),
        name="plan",
    )(topi, seg_off.astype(F32).reshape(N_EXPERTS, 1))


SC_WINDOW = 128


def _sc_workers():
    info = plsc.get_sparse_core_info()
    return info.num_cores, info.num_cores * info.num_subcores


def _scatter_rows(src, idx, zero_idx, n_rows):
    t, w = src.shape
    m, mz = idx.shape[0], zero_idx.shape[0]
    n_cores, n_workers = _sc_workers()
    per_worker, per_worker_z = t // n_workers, mz // n_workers
    assert m % t == 0
    assert per_worker * n_workers == t and per_worker % SC_WINDOW == 0
    assert per_worker_z * n_workers == mz and per_worker_z % SC_WINDOW == 0
    mesh = plsc.VectorSubcoreMesh(core_axis_name="core", subcore_axis_name="subcore")

    @functools.partial(
        pl.kernel, mesh=mesh, out_type=jax.ShapeDtypeStruct((n_rows, w), src.dtype),
        scratch_types=[pltpu.VMEM((SC_WINDOW,), jnp.int32), pltpu.VMEM((SC_WINDOW, w), src.dtype),
                       pltpu.SemaphoreType.DMA])
    def scatter(src_hbm, idx_hbm, zeros_hbm, zero_idx_hbm, out_hbm, idx_v, rows_v, sem):
        worker = lax.axis_index("subcore") * n_cores + lax.axis_index("core")

        @pl.loop(0, per_worker // SC_WINDOW)
        def _(step):
            first = pl.multiple_of(worker * per_worker + step * SC_WINDOW, SC_WINDOW)
            pltpu.sync_copy(src_hbm.at[pl.ds(first, SC_WINDOW)], rows_v)
            for copy in range(m // t):
                pltpu.sync_copy(idx_hbm.at[pl.ds(copy * t + first, SC_WINDOW)], idx_v)
                pltpu.async_copy(rows_v, out_hbm.at[idx_v], sem).wait()

        pltpu.sync_copy(zeros_hbm, rows_v)

        @pl.loop(0, per_worker_z // SC_WINDOW)
        def _(step):
            base = pl.multiple_of(worker * per_worker_z + step * SC_WINDOW, SC_WINDOW)
            pltpu.sync_copy(zero_idx_hbm.at[pl.ds(base, SC_WINDOW)], idx_v)
            pltpu.async_copy(rows_v, out_hbm.at[idx_v], sem).wait()

    return scatter(src, idx, jnp.zeros((SC_WINDOW, w), src.dtype), zero_idx)


EXPERT_RING = 3


def _experts_kernel(te_ref, tb_ref, nt_ref, xs_ref, wg_ref, wu_ref, wd_ref, ys_ref, xbuf, sem):
    i = pl.program_id(0)
    n_tiles = nt_ref[0]

    def tile_copy(j):
        slot = lax.rem(j, EXPERT_RING)
        row0 = pl.multiple_of(tb_ref[j] * MOE_TILE, MOE_TILE)
        return pltpu.make_async_copy(xs_ref.at[pl.ds(row0, MOE_TILE), :], xbuf.at[slot], sem.at[slot])

    @pl.when(i == 0)
    def _():
        for j in range(EXPERT_RING - 1):
            @pl.when(j < n_tiles)
            def _():
                tile_copy(j).start()

    ahead = i + (EXPERT_RING - 1)

    @pl.when(ahead < n_tiles)
    def _():
        tile_copy(ahead).start()

    @pl.when(i < n_tiles)
    def _():
        tile_copy(i).wait()
        x = _unpack_words(xbuf[lax.rem(i, EXPERT_RING)]).astype(BF16)
        act = _silu(_dot(x, wg_ref[...].astype(BF16))) * _dot(x, wu_ref[...].astype(BF16))
        ys_ref[...] = _pack_words(_dot(act.astype(BF16), wd_ref[...].astype(BF16)))


def _experts(xs, tile_expert, tile_block, n_tiles, wg, wu, wd):
    d, f = wg.shape[1], wg.shape[2]
    grid_spec = pltpu.PrefetchScalarGridSpec(
        num_scalar_prefetch=3,
        grid=(xs.shape[0] // MOE_TILE,),
        in_specs=[pl.BlockSpec(memory_space=pl.ANY),
                  pl.BlockSpec((None, d, f), lambda i, te, tb, nt: (te[i], 0, 0)),
                  pl.BlockSpec((None, d, f), lambda i, te, tb, nt: (te[i], 0, 0)),
                  pl.BlockSpec((None, f, d), lambda i, te, tb, nt: (te[i], 0, 0))],
        out_specs=pl.BlockSpec((MOE_TILE, d // 2), lambda i, te, tb, nt: (tb[i], 0)),
        scratch_shapes=[pltpu.VMEM((EXPERT_RING, MOE_TILE, d // 2), U32), pltpu.SemaphoreType.DMA((EXPERT_RING,))],
    )
    return pl.pallas_call(
        _experts_kernel,
        grid_spec=grid_spec,
        out_shape=jax.ShapeDtypeStruct(xs.shape, U32),
        compiler_params=_cparams(("arbitrary",)),
        name="experts",
    )(tile_expert, tile_block, n_tiles, xs, wg, wu, wd)


def _gather_rows(table, idx):
    m = idx.shape[0]
    w = table.shape[1]
    win = SC_WINDOW // 2
    n_cores, n_workers = _sc_workers()
    per_worker = m // n_workers
    n_pairs = per_worker // (2 * win)
    assert per_worker * n_workers == m and n_pairs * 2 * win == per_worker
    mesh = plsc.VectorSubcoreMesh(core_axis_name="core", subcore_axis_name="subcore")

    @functools.partial(
        pl.kernel, mesh=mesh, out_type=jax.ShapeDtypeStruct((m, w), table.dtype),
        scratch_types=[pltpu.VMEM((win,), jnp.int32), pltpu.VMEM((win,), jnp.int32),
                       pltpu.VMEM((win, w), table.dtype), pltpu.VMEM((win, w), table.dtype),
                       pltpu.SemaphoreType.DMA, pltpu.SemaphoreType.DMA])
    def gather(table_hbm, idx_hbm, out_hbm, idx_a, idx_b, rows_a, rows_b, sem_a, sem_b):
        worker = lax.axis_index("subcore") * n_cores + lax.axis_index("core")
        start = worker * per_worker

        def request(first, idx_v, rows_v, sem):
            pltpu.sync_copy(idx_hbm.at[pl.ds(first, win)], idx_v)
            pltpu.async_copy(table_hbm.at[idx_v], rows_v, sem)

        def deliver(first, idx_v, rows_v, sem):
            pltpu.make_async_copy(table_hbm.at[idx_v], rows_v, sem).wait()
            pltpu.sync_copy(rows_v, out_hbm.at[pl.ds(first, win)])

        request(pl.multiple_of(start, win), idx_a, rows_a, sem_a)

        @pl.loop(0, n_pairs)
        def _(pair):
            first_a = pl.multiple_of(start + pair * 2 * win, win)
            first_b = pl.multiple_of(first_a + win, win)
            request(first_b, idx_b, rows_b, sem_b)
            deliver(first_a, idx_a, rows_a, sem_a)

            @pl.when(pair + 1 < n_pairs)
            def _():
                request(pl.multiple_of(first_b + win, win), idx_a, rows_a, sem_a)

            deliver(first_b, idx_b, rows_b, sem_b)

    return gather(table, idx)


def _combine_kernel(rows_ref, topw_ref, h_ref, x1_ref, g2_ref, sg_ref, su_ref, sd_ref, ln2g_ref, ln2b_ref,
                    o_ref, *, alpha):
    h = _unpack_words(h_ref[...]).astype(BF16)
    act = _silu(_dot(h, sg_ref[...])) * _dot(h, su_ref[...])
    y = _dot(act.astype(BF16), sd_ref[...])
    w = topw_ref[...].T
    for k in range(TOP_K):
        y = y + w[:, k:k + 1] * _unpack_words(rows_ref[k])
    o_ref[...] = _normalize(alpha * x1_ref[...] + g2_ref[...] * y) * ln2g_ref[...] + ln2b_ref[...]


def _combine(gathered, topw, h2p, x1, g2, sg, su, sd, ln2_g, ln2_b, alpha):
    t, d = x1.shape
    b, k, s = topw.shape
    per_b = s // MOE_TOK
    fs = sg.shape[1]
    rows = pl.BlockSpec((MOE_TOK, d), lambda i: (i, 0))
    packed = pl.BlockSpec((MOE_TOK, d // 2), lambda i: (i, 0))
    vec = pl.BlockSpec((1, d), lambda i: (0, 0))
    return pl.pallas_call(
        functools.partial(_combine_kernel, alpha=alpha),
        grid=(t // MOE_TOK,),
        in_specs=[pl.BlockSpec((k, MOE_TOK, d // 2), lambda i: (0, i, 0)),
                  pl.BlockSpec((None, k, MOE_TOK), lambda i: (i // per_b, 0, i % per_b)),
                  packed, rows,
                  pl.BlockSpec((None, 1, d), lambda i: (i // per_b, 0, 0)),
                  pl.BlockSpec((d, fs), lambda i: (0, 0)),
                  pl.BlockSpec((d, fs), lambda i: (0, 0)),
                  pl.BlockSpec((fs, d), lambda i: (0, 0)),
                  vec, vec],
        out_specs=rows,
        out_shape=jax.ShapeDtypeStruct((t, d), F32),
        compiler_params=_cparams(("arbitrary",)),
        name="combine",
    )(gathered, topw, h2p, x1, g2, sg, su, sd, ln2_g.reshape(1, d), ln2_b.reshape(1, d))


def _moe(h2p, topi, topw, cnt, x1, g2, wg, wu, wd, sg, su, sd, ln2_g, ln2_b, alpha):
    b, s, d = x1.shape
    t = b * s
    cnt = cnt[:, 0].astype(jnp.int32)
    tiles_e = (cnt + (MOE_TILE - 1)) // MOE_TILE
    tiles_cum = jnp.cumsum(tiles_e)
    seg_off = (tiles_cum - tiles_e) * MOE_TILE
    n_tiles_max = t * TOP_K // MOE_TILE + N_EXPERTS
    tile_block = jnp.minimum(jnp.arange(n_tiles_max, dtype=jnp.int32), tiles_cum[-1] - 1)
    tile_expert = jnp.sum((tiles_cum[None, :] <= tile_block[:, None]).astype(jnp.int32), axis=1)
    n_tiles = tiles_cum[-1:].astype(jnp.int32)

    dest = jnp.transpose(_plan(topi, seg_off), (1, 0, 2)).reshape(TOP_K * t)
    j = jnp.arange(MOE_TILE, dtype=jnp.int32)[None, :]
    n_pad = (tiles_e * MOE_TILE - cnt)[:, None]
    spare = (n_tiles_max - 1) * MOE_TILE + j
    zero_idx = jnp.where(j < n_pad, (seg_off + cnt)[:, None] + j, spare).reshape(N_EXPERTS * MOE_TILE)
    xs = _scatter_rows(h2p, dest, zero_idx.astype(jnp.int32), n_tiles_max * MOE_TILE)
    ys = _experts(xs, tile_expert, tile_block, n_tiles, wg, wu, wd)
    gathered = _gather_rows(ys, dest)
    out = _combine(gathered.reshape(TOP_K, t, d // 2), topw, h2p, x1.reshape(t, d), g2, sg, su, sd,
                   ln2_g, ln2_b, alpha)
    return out.reshape(b, s, d)


def kernel(x, c, ctx, c_ctx, w_ada, b_ada, w_in, hg_lb_fwd, hg_lb_bwd, hg_norm_g, na_rpb, w_branch_a, w_branch_b, w_out, ln1_g, ln1_b, w_router, router_bias, w_e_gate, w_e_up, w_e_down, w_sh_gate, w_sh_up, w_sh_down, ln2_g, ln2_b):
    depth = w_ada.shape[0]
    assert depth == 1, "single-layer block"
    b, s, d = x.shape
    alpha = (2.0 * depth) ** 0.25
    l = 0
    lb_fwd = jnp.cumsum(jax.nn.softmax(hg_lb_fwd.astype(F32), axis=0), axis=0)[l]
    lb_bwd = jnp.cumsum(jax.nn.softmax(hg_lb_bwd.astype(F32), axis=0), axis=0)[l]

    cond_rows = jnp.concatenate([c, c_ctx[None, :], jnp.zeros((8 - b - 1, d), F32)], axis=0)
    mod = _ada(cond_rows, w_ada[l], b_ada[l])
    sh1, sc1, g1, sh2, sc2, g2 = [m[:b, None, :] for m in jnp.split(mod, 6, axis=-1)]
    csh1, csc1 = [jnp.broadcast_to(m[b:b + 1, None, :], (b, 1, d)) for m in jnp.split(mod, 6, axis=-1)[:2]]

    w_in_b = w_in[l].astype(BF16)
    p = _inproj(x, sh1, sc1, w_in_b)
    pc = _inproj(ctx, csh1, csc1, w_in_b)

    o_f, o_b = _hgrn(p, pc, lb_fwd, lb_bwd)
    y_na = _natten(p, pc, *_na_tables(na_rpb[l], s))

    x1, h2, topi, topw, cnt = _merge(o_f, o_b, p, y_na, x, g1, sh2, sc2, hg_norm_g[l], ln1_g[l], ln1_b[l],
                                     w_branch_a[l].astype(BF16), w_branch_b[l].astype(BF16),
                                     w_out[l].astype(BF16), w_router[l].T, router_bias[l], alpha)

    return _moe(h2, topi, topw, cnt, x1, g2,
                w_e_gate[l], w_e_up[l], w_e_down[l],
                w_sh_gate[l].astype(BF16), w_sh_up[l].astype(BF16), w_sh_down[l].astype(BF16),
                ln2_g[l], ln2_b[l], alpha)
```

```python
import functools

import numpy as np
import jax
import jax.numpy as jnp
from jax import lax
from jax.experimental import pallas as pl
from jax.experimental.pallas import tpu as pltpu
from jax.experimental.pallas import tpu_sc as plsc

F32 = jnp.float32
BF16 = jnp.bfloat16

D_MODEL = 1024
GRID_W = 64
HG_HEADS = 8
HG_DK = 128
HG_CHUNK = 64
NA_HEADS = 16
NA_HD = 64
NA_WIN_R = 8
NA_WIN_C = 16
ROPE_THETA = 10000.0
NEG_INF = -1e30
N_EXPERTS = 64
EXPERT_DIM = 256
TOP_K = 8
N_GROUPS = 8
TOPK_GROUPS = 4
ROUTED_SCALE = 2.5
LN_EPS = 1e-6
N_SECTIONS = 10
SEC_Q, SEC_FF, SEC_FB, SEC_I, SEC_OG, SEC_NQ, SEC_NK, SEC_NV, SEC_GA, SEC_GB = range(10)

VMEM_LIMIT = 56 * 1024 * 1024


def _cparams(sem):
    return pltpu.CompilerParams(dimension_semantics=sem, vmem_limit_bytes=VMEM_LIMIT)


def _normalize(x):
    mu = jnp.mean(x, axis=-1, keepdims=True)
    xc = x - mu
    var = jnp.mean(xc * xc, axis=-1, keepdims=True)
    return xc * lax.rsqrt(var + LN_EPS)


def _silu(x):
    return x * jax.nn.sigmoid(x)


def _dot(a, b):
    return jnp.dot(a, b, preferred_element_type=F32)


def _dot_nt(a, b):
    return lax.dot_general(a, b, (((1,), (1,)), ((), ())), preferred_element_type=F32)


def _dot_tn(a, b):
    return lax.dot_general(a, b, (((0,), (0,)), ((), ())), preferred_element_type=F32)


U32 = jnp.uint32


def _pack_words(x):
    half = x.shape[1] // 2
    lo = lax.bitcast_convert_type(x[:, :half].astype(BF16).astype(F32), U32) >> 16
    hi = lax.bitcast_convert_type(x[:, half:].astype(BF16).astype(F32), U32) & jnp.uint32(0xFFFF0000)
    return lo | hi


def _unpack_words(w):
    lo = lax.bitcast_convert_type(w << 16, F32)
    hi = lax.bitcast_convert_type(w & jnp.uint32(0xFFFF0000), F32)
    return jnp.concatenate([lo, hi], axis=-1)


def _split3(x):
    hi = x.astype(BF16)
    r1 = x - hi.astype(F32)
    mid = r1.astype(BF16)
    lo = (r1 - mid.astype(F32)).astype(BF16)
    return hi, mid, lo


def _ada_kernel(c_ref, w_ref, b_ref, o_ref):
    cond = _silu(c_ref[...])
    o_ref[...] = _dot(cond.astype(BF16), w_ref[...].astype(BF16)) + b_ref[...]


def _ada(cond_rows, w_ada, b_ada):
    r, d = cond_rows.shape
    n = w_ada.shape[1]
    tn = 1024
    return pl.pallas_call(
        _ada_kernel,
        grid=(n // tn,),
        in_specs=[pl.BlockSpec((r, d), lambda j: (0, 0)),
                  pl.BlockSpec((d, tn), lambda j: (0, j)),
                  pl.BlockSpec((1, tn), lambda j: (0, j))],
        out_specs=pl.BlockSpec((r, tn), lambda j: (0, j)),
        out_shape=jax.ShapeDtypeStruct((r, n), F32),
        compiler_params=_cparams(("arbitrary",)),
        name="ada",
    )(cond_rows, w_ada, b_ada.reshape(1, n))


INPROJ_TOK = 2048


def _inproj_kernel(x_ref, sh_ref, sc_ref, w_ref, o_ref, h_ref):
    @pl.when(pl.program_id(2) == 0)
    def _():
        h = _normalize(x_ref[...]) * (1.0 + sc_ref[...]) + sh_ref[...]
        h_ref[...] = h.astype(BF16)

    o_ref[...] = _dot(h_ref[...], w_ref[...])


def _inproj(x, shift, scale, w_in_bf16):
    b, s, d = x.shape
    tm = min(INPROJ_TOK, s)
    nj = w_in_bf16.shape[1] // d
    return pl.pallas_call(
        _inproj_kernel,
        grid=(b, s // tm, nj),
        in_specs=[pl.BlockSpec((None, tm, d), lambda bi, i, j: (bi, i, 0)),
                  pl.BlockSpec((None, 1, d), lambda bi, i, j: (bi, 0, 0)),
                  pl.BlockSpec((None, 1, d), lambda bi, i, j: (bi, 0, 0)),
                  pl.BlockSpec((d, d), lambda bi, i, j: (0, j))],
        out_specs=pl.BlockSpec((None, None, tm, d), lambda bi, i, j: (j, bi, i, 0)),
        out_shape=jax.ShapeDtypeStruct((nj, b, s, d), F32),
        scratch_shapes=[pltpu.VMEM((tm, d), BF16)],
        compiler_params=_cparams(("arbitrary", "arbitrary", "arbitrary")),
        name="inproj",
    )(x, shift, scale, w_in_bf16)


def _hgrn_gates(q, fraw, v, lb, tri_bf16, last_row):
    f = lb + (1.0 - lb) * jax.nn.sigmoid(fraw)
    k = 1.0 - f
    lf = jnp.log(f)
    hi, mid, lo = _split3(lf)
    a = _dot(tri_bf16, hi) + _dot(tri_bf16, mid) + _dot(tri_bf16, lo)
    a_last = a[last_row:last_row + 1, :]
    kd = (k * jnp.exp(a_last - a)).astype(BF16)
    decay = jnp.exp(a_last)
    qa = kb = None
    if q is not None:
        qa = (_silu(q) * jnp.exp(a)).astype(BF16)
        kb = (k * jnp.exp(-a)).astype(BF16)
    return qa, kb, kd, v.astype(BF16), decay


def _hgrn_chunks(chunks, st_ref):
    first = []
    for d, ((qa, kb, kd, vb, decay), keep) in enumerate(chunks):
        for h in range(HG_HEADS):
            sl = slice(h * HG_DK, (h + 1) * HG_DK)
            st = st_ref[d, h]
            if qa is not None:
                first.append((_dot_nt(qa[:, sl], kb[:, sl]), _dot_nt(qa[:, sl], st.astype(BF16))))
            st_ref[d, h] = st * decay[:, sl] + _dot_tn(vb[:, sl], kd[:, sl])
    results = []
    for d, ((qa, kb, kd, vb, decay), keep) in enumerate(chunks):
        if qa is None:
            results.append(None)
            continue
        outs = []
        for h in range(HG_HEADS):
            sl = slice(h * HG_DK, (h + 1) * HG_DK)
            s_qk, o_state = first.pop(0)
            outs.append(_dot(jnp.where(keep, s_qk, 0.0).astype(BF16), vb[:, sl]) + o_state)
        results.append(jnp.concatenate(outs, axis=-1))
    return results


def _hgrn_kernel(qf_ref, ff_ref, if_ref, qb_ref, fb_ref, ib_ref, cff_ref, cfb_ref, ci_ref,
                 lbf_ref, lbb_ref, of_ref, ob_ref, st_ref, *, n_sub, n_ctx_sub):
    n = pl.program_id(1)
    c = HG_CHUNK
    row = lax.broadcasted_iota(jnp.int32, (c, c), 0)
    col = lax.broadcasted_iota(jnp.int32, (c, c), 1)
    keep_f = col <= row
    keep_b = col >= row
    tri_f = keep_f.astype(F32).astype(BF16)
    tri_b = keep_b.astype(F32).astype(BF16)
    lbf = lbf_ref[...]
    lbb = lbb_ref[...]

    @pl.when(n == 0)
    def _():
        st_ref[...] = jnp.zeros_like(st_ref)

        def body(i, carry):
            r0 = pl.multiple_of(i * c, c)
            r1 = pl.multiple_of((n_ctx_sub - 1 - i) * c, c)
            gf = _hgrn_gates(None, cff_ref[pl.ds(r0, c), :], ci_ref[pl.ds(r0, c), :], lbf, tri_f, c - 1)
            gb = _hgrn_gates(None, cfb_ref[pl.ds(r1, c), :], ci_ref[pl.ds(r1, c), :], lbb, tri_b, 0)
            _hgrn_chunks([(gf, keep_f), (gb, keep_b)], st_ref)
            return carry

        lax.fori_loop(0, n_ctx_sub, body, 0)

    @pl.when(n > 0)
    def _():
        def body(i, carry):
            r0 = pl.multiple_of(i * c, c)
            r1 = pl.multiple_of((n_sub - 1 - i) * c, c)
            gf = _hgrn_gates(qf_ref[pl.ds(r0, c), :], ff_ref[pl.ds(r0, c), :], if_ref[pl.ds(r0, c), :],
                             lbf, tri_f, c - 1)
            gb = _hgrn_gates(qb_ref[pl.ds(r1, c), :], fb_ref[pl.ds(r1, c), :], ib_ref[pl.ds(r1, c), :],
                             lbb, tri_b, 0)
            o_f, o_b = _hgrn_chunks([(gf, keep_f), (gb, keep_b)], st_ref)
            of_ref[pl.ds(r0, c), :] = o_f
            ob_ref[pl.ds(r1, c), :] = o_b
            return carry

        lax.fori_loop(0, n_sub, body, 0, unroll=True)


def _hgrn(p, pc, lb_fwd, lb_bwd):
    _, b, s, w = p.shape
    ctx_len = pc.shape[2]
    tb = min(256, s)
    nb = s // tb
    fwd = lambda bi, n: jnp.maximum(n - 1, 0)
    bwd = lambda bi, n: nb - 1 - jnp.maximum(n - 1, 0)

    def sec(section, blk):
        return pl.BlockSpec((None, None, tb, w), lambda bi, n: (section, bi, blk(bi, n), 0))

    def csec(section):
        return pl.BlockSpec((None, None, ctx_len, w), lambda bi, n: (section, bi, 0, 0))

    vec = pl.BlockSpec((1, w), lambda bi, n: (0, 0))
    kern = functools.partial(_hgrn_kernel, n_sub=tb // HG_CHUNK, n_ctx_sub=ctx_len // HG_CHUNK)
    return pl.pallas_call(
        kern,
        grid=(b, nb + 1),
        in_specs=[sec(SEC_Q, fwd), sec(SEC_FF, fwd), sec(SEC_I, fwd),
                  sec(SEC_Q, bwd), sec(SEC_FB, bwd), sec(SEC_I, bwd),
                  csec(SEC_FF), csec(SEC_FB), csec(SEC_I), vec, vec],
        out_specs=[pl.BlockSpec((None, tb, w), lambda bi, n: (bi, fwd(bi, n), 0)),
                   pl.BlockSpec((None, tb, w), lambda bi, n: (bi, bwd(bi, n), 0))],
        out_shape=[jax.ShapeDtypeStruct((b, s, w), F32), jax.ShapeDtypeStruct((b, s, w), F32)],
        scratch_shapes=[pltpu.VMEM((2, HG_HEADS, HG_DK, HG_DK), F32)],
        compiler_params=_cparams(("arbitrary", "arbitrary")),
        name="hgrn",
    )(p, p, p, p, p, p, pc, pc, pc, lb_fwd.reshape(1, w), lb_bwd.reshape(1, w))


NA_ROWS_PER_STEP = 32
NA_PREP_ROWS = 512
NA_KEY_TILE = 128
NA_SPAN = NA_WIN_R * GRID_W


def _rope(t, cos, sin_signed, first_half):
    w = t.shape[-1]
    partner = jnp.where(first_half, pltpu.roll(t, w - 16, 1), pltpu.roll(t, 16, 1))
    return t * cos + partner * sin_signed


def _fold_lanes(op, *arrays):
    tiles = [a[:, c:c + 128] for a in arrays for c in range(0, a.shape[-1], 128)]
    acc = tiles[0]
    for t in tiles[1:]:
        acc = op(acc, t)
    return acc


def _rope_tables(rowtab_ref, coltab_ref, row0, n_rows, row_lane):
    out = []
    for i in range(2):
        rt = rowtab_ref[i, pl.ds(row0, n_rows), :]
        by_row = jnp.concatenate([jnp.broadcast_to(rt[r:r + 1, :], (GRID_W, rt.shape[1])) for r in range(n_rows)],
                                 axis=0)
        by_col = jnp.concatenate([coltab_ref[i]] * n_rows, axis=0)
        out.append(jnp.where(row_lane, by_row, by_col))
    return out


def _natten_kernel(q_ref, k_ref, v_ref, kc_ref, vc_ref, rowtab_ref, coltab_ref, t2_ref, o_ref,
                   kt_s, v_s, kc_s, vc_s, bias_s, tail_s, *, rows):
    rblk = pl.program_id(2)
    hd = NA_HD
    lane = lax.broadcasted_iota(jnp.int32, (1, 2 * hd), 1)
    first_half = (lane % 32) < 16
    row_lane = (lane % hd) < hd // 2
    scale = NA_HD ** -0.5

    def values_and_ones(v_pair, h):
        vh = v_pair if h == 0 else pltpu.roll(v_pair, hd, 1)
        return jnp.where(lane < hd, vh, jnp.where(lane == hd, 1.0, 0.0)).astype(BF16)

    @pl.when(rblk == 0)
    def _():
        kc = kc_ref[...].astype(BF16)
        qi = lax.broadcasted_iota(jnp.int32, (GRID_W, GRID_W), 0)
        ki = lax.broadcasted_iota(jnp.int32, (GRID_W, GRID_W), 1)
        cstart = jnp.clip(qi - NA_WIN_C // 2, 0, GRID_W - NA_WIN_C)
        in_win = (ki >= cstart) & (ki < cstart + NA_WIN_C)
        s_len = k_ref.shape[0]
        tail_s[...] = jnp.zeros_like(tail_s)
        for h in range(2):
            sl = slice(h * hd, (h + 1) * hd)
            kc_s[h] = kc[:, sl]
            vc_s[h] = values_and_ones(vc_ref[...], h)
            tiles = [jnp.where(in_win, t2_ref[h, dr], NEG_INF) for dr in range(2 * NA_WIN_R - 1)]
            for v in range(NA_WIN_R):
                for j in range(NA_WIN_R):
                    bias_s[h, v, :, j * GRID_W:(j + 1) * GRID_W] = tiles[NA_WIN_R - 1 - v + j]

        eye = (lax.broadcasted_iota(jnp.int32, (2 * hd, 2 * hd), 0)
               == lax.broadcasted_iota(jnp.int32, (2 * hd, 2 * hd), 1)).astype(F32).astype(BF16)

        def prep(i, carry):
            r0 = pl.multiple_of(i * NA_PREP_ROWS, NA_PREP_ROWS)
            rws = pl.ds(r0, NA_PREP_ROWS)
            cos, sin = _rope_tables(rowtab_ref, coltab_ref, i * (NA_PREP_ROWS // GRID_W), NA_PREP_ROWS // GRID_W,
                                    row_lane)
            kr = _rope(k_ref[rws, :], cos, sin, first_half)
            kr_odd = jnp.concatenate([tail_s[...], kr[:NA_PREP_ROWS - GRID_W]], axis=0)
            tail_s[...] = kr[NA_PREP_ROWS - GRID_W:]
            krt = [_dot_nt(eye, kr.astype(BF16)).astype(BF16),
                   _dot_nt(eye, kr_odd.astype(BF16)).astype(BF16)]
            vv = v_ref[rws, :]
            for h in range(2):
                sl = slice(h * hd, (h + 1) * hd)
                for par in range(2):
                    for c in range(NA_PREP_ROWS // NA_KEY_TILE):
                        kt_s[h, par, i * (NA_PREP_ROWS // NA_KEY_TILE) + c] = (
                            krt[par][sl, c * NA_KEY_TILE:(c + 1) * NA_KEY_TILE])
                v_s[h, rws, :] = values_and_ones(vv, h)
            return carry

        lax.fori_loop(0, s_len // NA_PREP_ROWS, prep, 0, unroll=4)

    tq = NA_ROWS_PER_STEP * GRID_W
    q = q_ref[...] * scale
    cos, sin = _rope_tables(rowtab_ref, coltab_ref, rblk * NA_ROWS_PER_STEP, NA_ROWS_PER_STEP, row_lane)
    qr = _rope(q, cos, sin, first_half)
    qb = q.astype(BF16)
    qrb = qr.astype(BF16)
    rws = [slice(rr * GRID_W, (rr + 1) * GRID_W) for rr in range(NA_ROWS_PER_STEP)]
    par, slot0, key0, bidx = [], [], [], []
    for rr in range(NA_ROWS_PER_STEP):
        r = rblk * NA_ROWS_PER_STEP + rr
        rs = jnp.clip(r - NA_WIN_R // 2, 0, rows - NA_WIN_R)
        par.append(rs & 1)
        slot0.append(lax.shift_right_logical(rs, 1) + (rs & 1))
        key0.append(pl.multiple_of(rs * GRID_W, GRID_W))
        bidx.append(r - rs)

    def scores(h):
        sl = slice(h * hd, (h + 1) * hd)
        qrb_h = qrb[:, sl]
        s_ctx_all = _dot_nt(qb[:, sl], kc_s[h])
        s_win = []
        for rr in range(NA_ROWS_PER_STEP):
            kt = kt_s[h, par[rr], pl.ds(slot0[rr], NA_SPAN // NA_KEY_TILE)]
            kt = jnp.concatenate([kt[c] for c in range(NA_SPAN // NA_KEY_TILE)], axis=-1)
            s_win.append(_dot(qrb_h[rws[rr]], kt))
        return s_win, s_ctx_all

    def softmax(h, s_win, s_ctx_all):
        e_win, e_ctx = [], []
        for rr in range(NA_ROWS_PER_STEP):
            sw = s_win[rr] + bias_s[h, bidx[rr]]
            sc = s_ctx_all[rws[rr]]
            m = jnp.max(_fold_lanes(jnp.maximum, sw, sc), axis=-1, keepdims=True)
            e_win.append(jnp.exp(sw - m).astype(BF16))
            e_ctx.append(jnp.exp(sc - m).astype(BF16))
        return e_win, e_ctx

    def values(h, e_win, e_ctx):
        o_win = []
        for rr in range(NA_ROWS_PER_STEP):
            o_win.append(_dot(e_win[rr], v_s[h, pl.ds(key0[rr], NA_SPAN), :]))
        o = jnp.concatenate(o_win, axis=0) + _dot(jnp.concatenate(e_ctx, axis=0), vc_s[h])
        return o[:, :hd] * (1.0 / o[:, hd:hd + 1])

    s0 = scores(0)
    s1 = scores(1)
    p0 = softmax(0, *s0)
    o0 = values(0, *p0)
    p1 = softmax(1, *s1)
    o1 = values(1, *p1)
    o_ref[...] = jnp.concatenate([o0, o1], axis=-1)


def _na_tables(rpb, s):
    half = NA_HD // 2
    inv = jnp.power(ROPE_THETA, -jnp.arange(0, half, 2, dtype=F32) / half)

    def tables(n):
        ang = jnp.arange(n, dtype=F32)[:, None] * inv[None, :]
        reps = 2 * NA_HD // half
        return jnp.stack([jnp.tile(jnp.cos(ang), (1, 2 * reps)),
                          jnp.tile(jnp.concatenate([-jnp.sin(ang), jnp.sin(ang)], axis=-1), (1, reps))])

    rowtab, coltab = tables(s // GRID_W), tables(GRID_W)

    pad = GRID_W - NA_WIN_C
    rp = jnp.pad(rpb.astype(F32), ((0, 0), (0, 0), (pad, pad)), mode="edge")
    t2 = jnp.stack([rp[:, :, GRID_W - 1 - qc:2 * GRID_W - 1 - qc] for qc in range(GRID_W)], axis=2)
    return rowtab, coltab, t2


def _natten(p, pc, rowtab, coltab, t2):
    _, b, s, w = p.shape
    ctx_len = pc.shape[2]
    rows = s // GRID_W
    assert rows >= NA_WIN_R and rows % NA_ROWS_PER_STEP == 0
    tq = NA_ROWS_PER_STEP * GRID_W
    hw = 2 * NA_HD
    nhp = w // hw
    kern = functools.partial(_natten_kernel, rows=rows)
    return pl.pallas_call(
        kern,
        grid=(b, nhp, rows // NA_ROWS_PER_STEP),
        in_specs=[pl.BlockSpec((None, None, tq, hw), lambda bi, hp, r: (SEC_NQ, bi, r, hp)),
                  pl.BlockSpec((None, None, s, hw), lambda bi, hp, r: (SEC_NK, bi, 0, hp)),
                  pl.BlockSpec((None, None, s, hw), lambda bi, hp, r: (SEC_NV, bi, 0, hp)),
                  pl.BlockSpec((None, None, ctx_len, hw), lambda bi, hp, r: (SEC_NK, bi, 0, hp)),
                  pl.BlockSpec((None, None, ctx_len, hw), lambda bi, hp, r: (SEC_NV, bi, 0, hp)),
                  pl.BlockSpec((2, rows, hw), lambda bi, hp, r: (0, 0, 0)),
                  pl.BlockSpec((2, GRID_W, hw), lambda bi, hp, r: (0, 0, 0)),
                  pl.BlockSpec((2, 2 * NA_WIN_R - 1, GRID_W, GRID_W), lambda bi, hp, r: (hp, 0, 0, 0))],
        out_specs=pl.BlockSpec((None, tq, hw), lambda bi, hp, r: (bi, r, hp)),
        out_shape=jax.ShapeDtypeStruct((b, s, w), F32),
        scratch_shapes=[pltpu.VMEM((2, 2, s // NA_KEY_TILE, NA_HD, NA_KEY_TILE), BF16),
                        pltpu.VMEM((2, s, hw), BF16),
                        pltpu.VMEM((2, ctx_len, NA_HD), BF16), pltpu.VMEM((2, ctx_len, hw), BF16),
                        pltpu.VMEM((2, NA_WIN_R, GRID_W, NA_SPAN), F32),
                        pltpu.VMEM((GRID_W, hw), F32)],
        compiler_params=_cparams(("arbitrary", "arbitrary", "arbitrary")),
        name="natten",
    )(p, p, p, pc, pc, rowtab, coltab, t2)


def _route(logits_t, rbias):
    e, t = logits_t.shape
    gsz = e // N_GROUPS
    scores = jax.nn.sigmoid(logits_t)
    sel = scores + rbias
    neg = -jnp.inf
    sub = lax.broadcasted_iota(jnp.int32, (gsz, t), 0).astype(F32)
    gscore = []
    for g in range(N_GROUPS):
        grp = sel[g * gsz:(g + 1) * gsz, :]
        m1 = jnp.max(grp, axis=0, keepdims=True)
        first = jnp.min(jnp.where(grp == m1, sub, float(gsz)), axis=0, keepdims=True)
        m2 = jnp.max(jnp.where(sub == first, neg, grp), axis=0, keepdims=True)
        gscore.append(m1 + m2)
    masked = []
    for g in range(N_GROUPS):
        rank = jnp.zeros((1, t), F32)
        for g2 in range(N_GROUPS):
            if g2 == g:
                continue
            if g2 < g:
                ahead = gscore[g2] >= gscore[g]
            else:
                ahead = gscore[g2] > gscore[g]
            rank = rank + jnp.where(ahead, 1.0, 0.0)
        masked.append(jnp.where(rank < TOPK_GROUPS, sel[g * gsz:(g + 1) * gsz, :], neg))
    work = jnp.concatenate(masked, axis=0)
    eidx = lax.broadcasted_iota(jnp.int32, (e, t), 0).astype(F32)
    idxs, ws = [], []
    chosen = jnp.zeros((e, t), F32)
    for _ in range(TOP_K):
        m = jnp.max(work, axis=0, keepdims=True)
        first = jnp.min(jnp.where(work == m, eidx, float(e)), axis=0, keepdims=True)
        pick = eidx == first
        idxs.append(first)
        ws.append(jnp.sum(jnp.where(pick, scores, 0.0), axis=0, keepdims=True))
        chosen = jnp.where(pick, 1.0, chosen)
        work = jnp.where(pick, neg, work)
    w = jnp.concatenate(ws, axis=0)
    w = w / jnp.sum(w, axis=0, keepdims=True) * ROUTED_SCALE
    return jnp.concatenate(idxs, axis=0).astype(jnp.int32), w, chosen


MERGE_TOK = 512
MERGE_SUB = 256


def _merge_kernel(of_ref, ob_ref, og_ref, yna_ref, ga_ref, gb_ref, x_ref, g1_ref, sh2_ref, sc2_ref,
                  hgg_ref, ln1g_ref, ln1b_ref, wa_ref, wb_ref, wo_ref, wr_ref, rb_ref,
                  x1_ref, h2_ref, topi_ref, topw_ref, cnt_ref, *, alpha):
    tm = x_ref.shape[0]
    subs = [slice(i * MERGE_SUB, (i + 1) * MERGE_SUB) for i in range(tm // MERGE_SUB)]

    def branches(rows):
        o = of_ref[rows, :] + ob_ref[rows, :]
        parts = []
        for h in range(HG_HEADS):
            oh = o[:, h * HG_DK:(h + 1) * HG_DK]
            parts.append(oh * lax.rsqrt(jnp.mean(oh * oh, axis=-1, keepdims=True) + LN_EPS))
        y_hg = jnp.concatenate(parts, axis=-1) * hgg_ref[...] * _silu(og_ref[rows, :])
        return _dot(y_hg.astype(BF16), wa_ref[...]), _dot(yna_ref[rows, :].astype(BF16), wb_ref[...])

    def out_proj(rows, ya, yb):
        t = jax.nn.sigmoid(ga_ref[rows, :]) * ya + jax.nn.sigmoid(gb_ref[rows, :]) * yb
        return _dot(t.astype(BF16), wo_ref[...])

    def norms_router(rows, i, y):
        x1 = _normalize(alpha * x_ref[rows, :] + g1_ref[...] * y) * ln1g_ref[...] + ln1b_ref[...]
        x1_ref[rows, :] = x1
        h2 = _normalize(x1) * (1.0 + sc2_ref[...]) + sh2_ref[...]
        h2_ref[rows, :] = _pack_words(h2)
        hh, hm, hl = _split3(h2)
        wh, wm, wl = _split3(wr_ref[...])
        return (_dot_nt(wh, hh) + _dot_nt(wh, hm) + _dot_nt(wm, hh)
                + _dot_nt(wh, hl) + _dot_nt(wl, hh) + _dot_nt(wm, hm))

    ab = [branches(rows) for rows in subs]
    ys = [out_proj(rows, *ab[i]) for i, rows in enumerate(subs)]
    logits = [norms_router(rows, i, ys[i]) for i, rows in enumerate(subs)]

    @pl.when((pl.program_id(0) == 0) & (pl.program_id(1) == 0))
    def _():
        cnt_ref[...] = jnp.zeros_like(cnt_ref)

    for i, rows in enumerate(subs):
        topi, topw, chosen = _route(logits[i], rb_ref[...])
        topi_ref[:, rows] = topi
        topw_ref[:, rows] = topw
        cnt_ref[...] += jnp.sum(chosen, axis=1, keepdims=True)


def _merge(o_f, o_b, p, y_na, x, g1, sh2, sc2, hg_norm_g, ln1_g, ln1_b, w_a, w_b, w_o, w_router_t, router_bias,
           alpha):
    b, s, d = x.shape
    tm = min(MERGE_TOK, s)
    e = w_router_t.shape[0]
    tok = lambda bi, i: (bi, i, 0)
    blk = pl.BlockSpec((None, tm, d), tok)

    def sec(section):
        return pl.BlockSpec((None, None, tm, d), lambda bi, i: (section, bi, i, 0))

    mod = pl.BlockSpec((None, 1, d), lambda bi, i: (bi, 0, 0))
    vec = pl.BlockSpec((1, d), lambda bi, i: (0, 0))
    mat = pl.BlockSpec((d, d), lambda bi, i: (0, 0), pipeline_mode=pl.Buffered(1))
    return pl.pallas_call(
        functools.partial(_merge_kernel, alpha=alpha),
        grid=(b, s // tm),
        in_specs=[blk, blk, sec(SEC_OG), blk, sec(SEC_GA), sec(SEC_GB), blk, mod, mod, mod,
                  vec, vec, vec, mat, mat, mat,
                  pl.BlockSpec((e, d), lambda bi, i: (0, 0)),
                  pl.BlockSpec((e, 1), lambda bi, i: (0, 0))],
        out_specs=[blk,
                   pl.BlockSpec((tm, d // 2), lambda bi, i: (bi * (s // tm) + i, 0)),
                   pl.BlockSpec((None, TOP_K, tm), lambda bi, i: (bi, 0, i)),
                   pl.BlockSpec((None, TOP_K, tm), lambda bi, i: (bi, 0, i)),
                   pl.BlockSpec((e, 128), lambda bi, i: (0, 0))],
        out_shape=[jax.ShapeDtypeStruct((b, s, d), F32),
                   jax.ShapeDtypeStruct((b * s, d // 2), U32),
                   jax.ShapeDtypeStruct((b, TOP_K, s), jnp.int32), jax.ShapeDtypeStruct((b, TOP_K, s), F32),
                   jax.ShapeDtypeStruct((e, 128), F32)],
        compiler_params=_cparams(("arbitrary", "arbitrary")),
        name="merge",
    )(o_f, o_b, p, y_na, p, p, x, g1, sh2, sc2, hg_norm_g.reshape(1, d), ln1_g.reshape(1, d),
      ln1_b.reshape(1, d), w_a, w_b, w_o, w_router_t, router_bias.reshape(e, 1))


MOE_TILE = 512
MOE_TOK = 512


def _plan_kernel(topi_ref, off_ref, dest_ref, carry_ref):
    @pl.when(pl.program_id(0) == 0)
    def _():
        carry_ref[...] = jnp.zeros_like(carry_ref)

    topi = topi_ref[...]
    tok = topi.shape[1]
    eidx = lax.broadcasted_iota(jnp.int32, (N_EXPERTS, tok), 0)
    hits = [eidx == topi[k:k + 1, :] for k in range(TOP_K)]
    m = jnp.zeros((N_EXPERTS, tok), F32)
    for hit in hits:
        m = jnp.where(hit, 1.0, m)
    before = (lax.broadcasted_iota(jnp.int32, (tok, tok), 0)
              < lax.broadcasted_iota(jnp.int32, (tok, tok), 1)).astype(F32).astype(BF16)
    row = off_ref[...] + carry_ref[...] + _dot(m.astype(BF16), before)
    dest = [jnp.sum(jnp.where(hit, row, 0.0), axis=0, keepdims=True) for hit in hits]
    dest_ref[...] = jnp.concatenate(dest, axis=0).astype(jnp.int32)
    carry_ref[...] += jnp.sum(m, axis=1, keepdims=True)


def _plan(topi, seg_off):
    b, k, s = topi.shape
    per_b = s // MOE_TOK
    blk = pl.BlockSpec((None, k, MOE_TOK), lambda i: (i // per_b, 0, i % per_b))
    return pl.pallas_call(
        _plan_kernel,
        grid=(b * per_b,),
        in_specs=[blk, pl.BlockSpec((N_EXPERTS, 1), lambda i: (0, 0))],
        out_specs=blk,
        out_shape=jax.ShapeDtypeStruct((b, k, s), jnp.int32),
        scratch_shapes=[pltpu.VMEM((N_EXPERTS, 1), F32)],
        compiler_params=_cparams(("arbitrary",)),
        name="plan",
    )(topi, seg_off.astype(F32).reshape(N_EXPERTS, 1))


SC_WINDOW = 128


def _sc_workers():
    info = plsc.get_sparse_core_info()
    return info.num_cores, info.num_cores * info.num_subcores


def _scatter_rows(src, idx, zero_idx, n_rows):
    t, w = src.shape
    m, mz = idx.shape[0], zero_idx.shape[0]
    n_cores, n_workers = _sc_workers()
    per_worker, per_worker_z = t // n_workers, mz // n_workers
    assert m % t == 0
    assert per_worker * n_workers == t and per_worker % SC_WINDOW == 0
    assert per_worker_z * n_workers == mz and per_worker_z % SC_WINDOW == 0
    mesh = plsc.VectorSubcoreMesh(core_axis_name="core", subcore_axis_name="subcore")

    @functools.partial(
        pl.kernel, mesh=mesh, out_type=jax.ShapeDtypeStruct((n_rows, w), src.dtype),
        scratch_types=[pltpu.VMEM((SC_WINDOW,), jnp.int32), pltpu.VMEM((SC_WINDOW, w), src.dtype),
                       pltpu.SemaphoreType.DMA])
    def scatter(src_hbm, idx_hbm, zeros_hbm, zero_idx_hbm, out_hbm, idx_v, rows_v, sem):
        worker = lax.axis_index("subcore") * n_cores + lax.axis_index("core")

        @pl.loop(0, per_worker // SC_WINDOW)
        def _(step):
            first = pl.multiple_of(worker * per_worker + step * SC_WINDOW, SC_WINDOW)
            pltpu.sync_copy(src_hbm.at[pl.ds(first, SC_WINDOW)], rows_v)
            for copy in range(m // t):
                pltpu.sync_copy(idx_hbm.at[pl.ds(copy * t + first, SC_WINDOW)], idx_v)
                pltpu.async_copy(rows_v, out_hbm.at[idx_v], sem).wait()

        pltpu.sync_copy(zeros_hbm, rows_v)

        @pl.loop(0, per_worker_z // SC_WINDOW)
        def _(step):
            base = pl.multiple_of(worker * per_worker_z + step * SC_WINDOW, SC_WINDOW)
            pltpu.sync_copy(zero_idx_hbm.at[pl.ds(base, SC_WINDOW)], idx_v)
            pltpu.async_copy(rows_v, out_hbm.at[idx_v], sem).wait()

    return scatter(src, idx, jnp.zeros((SC_WINDOW, w), src.dtype), zero_idx)


EXPERT_RING = 3


def _experts_kernel(te_ref, tb_ref, nt_ref, seg_ref, nxt_ref, xs_ref, wg_ref, wu_ref, wd_ref, ys_ref,
                    xbuf, sem, wg_f32, wu_f32, wd_f32, wg_s, wu_s, wd_s, wsem):
    i = pl.program_id(0)
    n_tiles = nt_ref[0]

    def weight_copies(expert, slot):
        return [pltpu.make_async_copy(src.at[expert], dst.at[slot], wsem.at[slot, n])
                for n, (src, dst) in enumerate(((wg_ref, wg_f32), (wu_ref, wu_f32), (wd_ref, wd_f32)))]

    @pl.when(i == 0)
    def _():
        for c in weight_copies(te_ref[0], 0):
            c.start()

    first = (i == 0) | (seg_ref[i] != seg_ref[jnp.maximum(i - 1, 0)])

    @pl.when((i < n_tiles) & first)
    def _():
        slot = seg_ref[i] & 1
        for c in weight_copies(te_ref[i], slot):
            c.wait()
        wg_s[...] = wg_f32[slot].astype(BF16)
        wu_s[...] = wu_f32[slot].astype(BF16)
        wd_s[...] = wd_f32[slot].astype(BF16)

        @pl.when(nxt_ref[i] >= 0)
        def _():
            for c in weight_copies(nxt_ref[i], 1 - slot):
                c.start()

    def tile_copy(j):
        slot = lax.rem(j, EXPERT_RING)
        row0 = pl.multiple_of(tb_ref[j] * MOE_TILE, MOE_TILE)
        return pltpu.make_async_copy(xs_ref.at[pl.ds(row0, MOE_TILE), :], xbuf.at[slot], sem.at[slot])

    @pl.when(i == 0)
    def _():
        for j in range(EXPERT_RING - 1):
            @pl.when(j < n_tiles)
            def _():
                tile_copy(j).start()

    ahead = i + (EXPERT_RING - 1)

    @pl.when(ahead < n_tiles)
    def _():
        tile_copy(ahead).start()

    @pl.when(i < n_tiles)
    def _():
        tile_copy(i).wait()
        x = _unpack_words(xbuf[lax.rem(i, EXPERT_RING)]).astype(BF16)
        act = _silu(_dot(x, wg_s[...])) * _dot(x, wu_s[...])
        ys_ref[...] = _pack_words(_dot(act.astype(BF16), wd_s[...]))


def _experts(xs, tile_expert, tile_block, n_tiles, tile_segment, next_expert, wg, wu, wd):
    d, f = wg.shape[1], wg.shape[2]
    anywhere = pl.BlockSpec(memory_space=pl.ANY)
    grid_spec = pltpu.PrefetchScalarGridSpec(
        num_scalar_prefetch=5,
        grid=(xs.shape[0] // MOE_TILE,),
        in_specs=[anywhere, anywhere, anywhere, anywhere],
        out_specs=pl.BlockSpec((MOE_TILE, d // 2), lambda i, te, tb, nt, seg, nxt: (tb[i], 0)),
        scratch_shapes=[pltpu.VMEM((EXPERT_RING, MOE_TILE, d // 2), U32), pltpu.SemaphoreType.DMA((EXPERT_RING,)),
                        pltpu.VMEM((2, d, f), F32), pltpu.VMEM((2, d, f), F32), pltpu.VMEM((2, f, d), F32),
                        pltpu.VMEM((d, f), BF16), pltpu.VMEM((d, f), BF16), pltpu.VMEM((f, d), BF16),
                        pltpu.SemaphoreType.DMA((2, 3))],
    )
    return pl.pallas_call(
        _experts_kernel,
        grid_spec=grid_spec,
        out_shape=jax.ShapeDtypeStruct(xs.shape, U32),
        compiler_params=_cparams(("arbitrary",)),
        name="experts",
    )(tile_expert, tile_block, n_tiles, tile_segment, next_expert, xs, wg, wu, wd)


def _gather_rows(table, idx):
    m = idx.shape[0]
    w = table.shape[1]
    win = SC_WINDOW // 2
    n_cores, n_workers = _sc_workers()
    per_worker = m // n_workers
    n_pairs = per_worker // (2 * win)
    assert per_worker * n_workers == m and n_pairs * 2 * win == per_worker
    mesh = plsc.VectorSubcoreMesh(core_axis_name="core", subcore_axis_name="subcore")

    @functools.partial(
        pl.kernel, mesh=mesh, out_type=jax.ShapeDtypeStruct((m, w), table.dtype),
        scratch_types=[pltpu.VMEM((win,), jnp.int32), pltpu.VMEM((win,), jnp.int32),
                       pltpu.VMEM((win, w), table.dtype), pltpu.VMEM((win, w), table.dtype),
                       pltpu.SemaphoreType.DMA, pltpu.SemaphoreType.DMA])
    def gather(table_hbm, idx_hbm, out_hbm, idx_a, idx_b, rows_a, rows_b, sem_a, sem_b):
        worker = lax.axis_index("subcore") * n_cores + lax.axis_index("core")
        start = worker * per_worker

        def request(first, idx_v, rows_v, sem):
            pltpu.sync_copy(idx_hbm.at[pl.ds(first, win)], idx_v)
            pltpu.async_copy(table_hbm.at[idx_v], rows_v, sem)

        def deliver(first, idx_v, rows_v, sem):
            pltpu.make_async_copy(table_hbm.at[idx_v], rows_v, sem).wait()
            pltpu.sync_copy(rows_v, out_hbm.at[pl.ds(first, win)])

        request(pl.multiple_of(start, win), idx_a, rows_a, sem_a)

        @pl.loop(0, n_pairs)
        def _(pair):
            first_a = pl.multiple_of(start + pair * 2 * win, win)
            first_b = pl.multiple_of(first_a + win, win)
            request(first_b, idx_b, rows_b, sem_b)
            deliver(first_a, idx_a, rows_a, sem_a)

            @pl.when(pair + 1 < n_pairs)
            def _():
                request(pl.multiple_of(first_b + win, win), idx_a, rows_a, sem_a)

            deliver(first_b, idx_b, rows_b, sem_b)

    return gather(table, idx)


def _combine_kernel(rows_ref, topw_ref, h_ref, x1_ref, g2_ref, sg_ref, su_ref, sd_ref, ln2g_ref, ln2b_ref,
                    o_ref, *, alpha):
    h = _unpack_words(h_ref[...]).astype(BF16)
    act = _silu(_dot(h, sg_ref[...])) * _dot(h, su_ref[...])
    y = _dot(act.astype(BF16), sd_ref[...])
    w = topw_ref[...].T
    for k in range(TOP_K):
        y = y + w[:, k:k + 1] * _unpack_words(rows_ref[k])
    o_ref[...] = _normalize(alpha * x1_ref[...] + g2_ref[...] * y) * ln2g_ref[...] + ln2b_ref[...]


def _combine(gathered, topw, h2p, x1, g2, sg, su, sd, ln2_g, ln2_b, alpha):
    t, d = x1.shape
    b, k, s = topw.shape
    per_b = s // MOE_TOK
    fs = sg.shape[1]
    rows = pl.BlockSpec((MOE_TOK, d), lambda i: (i, 0))
    packed = pl.BlockSpec((MOE_TOK, d // 2), lambda i: (i, 0))
    vec = pl.BlockSpec((1, d), lambda i: (0, 0))
    return pl.pallas_call(
        functools.partial(_combine_kernel, alpha=alpha),
        grid=(t // MOE_TOK,),
        in_specs=[pl.BlockSpec((k, MOE_TOK, d // 2), lambda i: (0, i, 0)),
                  pl.BlockSpec((None, k, MOE_TOK), lambda i: (i // per_b, 0, i % per_b)),
                  packed, rows,
                  pl.BlockSpec((None, 1, d), lambda i: (i // per_b, 0, 0)),
                  pl.BlockSpec((d, fs), lambda i: (0, 0)),
                  pl.BlockSpec((d, fs), lambda i: (0, 0)),
                  pl.BlockSpec((fs, d), lambda i: (0, 0)),
                  vec, vec],
        out_specs=rows,
        out_shape=jax.ShapeDtypeStruct((t, d), F32),
        compiler_params=_cparams(("arbitrary",)),
        name="combine",
    )(gathered, topw, h2p, x1, g2, sg, su, sd, ln2_g.reshape(1, d), ln2_b.reshape(1, d))


def _moe(h2p, topi, topw, cnt, x1, g2, wg, wu, wd, sg, su, sd, ln2_g, ln2_b, alpha):
    b, s, d = x1.shape
    t = b * s
    cnt = cnt[:, 0].astype(jnp.int32)
    tiles_e = (cnt + (MOE_TILE - 1)) // MOE_TILE
    tiles_cum = jnp.cumsum(tiles_e)
    seg_off = (tiles_cum - tiles_e) * MOE_TILE
    n_tiles_max = t * TOP_K // MOE_TILE + N_EXPERTS
    tile_block = jnp.minimum(jnp.arange(n_tiles_max, dtype=jnp.int32), tiles_cum[-1] - 1)
    tile_expert = jnp.sum((tiles_cum[None, :] <= tile_block[:, None]).astype(jnp.int32), axis=1)
    n_tiles = tiles_cum[-1:].astype(jnp.int32)
    present = tiles_e > 0
    seg_of_expert = jnp.cumsum(present.astype(jnp.int32)) - 1
    later = jnp.where(present, jnp.arange(N_EXPERTS, dtype=jnp.int32), N_EXPERTS)
    next_present = jnp.concatenate([lax.cummin(later, reverse=True)[1:], jnp.full((1,), N_EXPERTS, jnp.int32)])
    next_present = jnp.where(next_present < N_EXPERTS, next_present, -1)
    tile_segment = seg_of_expert[tile_expert].astype(jnp.int32)
    next_expert = next_present[tile_expert].astype(jnp.int32)

    dest = jnp.transpose(_plan(topi, seg_off), (1, 0, 2)).reshape(TOP_K * t)
    j = jnp.arange(MOE_TILE, dtype=jnp.int32)[None, :]
    n_pad = (tiles_e * MOE_TILE - cnt)[:, None]
    spare = (n_tiles_max - 1) * MOE_TILE + j
    zero_idx = jnp.where(j < n_pad, (seg_off + cnt)[:, None] + j, spare).reshape(N_EXPERTS * MOE_TILE)
    xs = _scatter_rows(h2p, dest, zero_idx.astype(jnp.int32), n_tiles_max * MOE_TILE)
    ys = _experts(xs, tile_expert, tile_block, n_tiles, tile_segment, next_expert, wg, wu, wd)
    gathered = _gather_rows(ys, dest)
    out = _combine(gathered.reshape(TOP_K, t, d // 2), topw, h2p, x1.reshape(t, d), g2, sg, su, sd,
                   ln2_g, ln2_b, alpha)
    return out.reshape(b, s, d)


def kernel(x, c, ctx, c_ctx, w_ada, b_ada, w_in, hg_lb_fwd, hg_lb_bwd, hg_norm_g, na_rpb, w_branch_a, w_branch_b, w_out, ln1_g, ln1_b, w_router, router_bias, w_e_gate, w_e_up, w_e_down, w_sh_gate, w_sh_up, w_sh_down, ln2_g, ln2_b):
    depth = w_ada.shape[0]
    assert depth == 1, "single-layer block"
    b, s, d = x.shape
    alpha = (2.0 * depth) ** 0.25
    l = 0
    lb_fwd = jnp.cumsum(jax.nn.softmax(hg_lb_fwd.astype(F32), axis=0), axis=0)[l]
    lb_bwd = jnp.cumsum(jax.nn.softmax(hg_lb_bwd.astype(F32), axis=0), axis=0)[l]

    cond_rows = jnp.concatenate([c, c_ctx[None, :], jnp.zeros((8 - b - 1, d), F32)], axis=0)
    mod = _ada(cond_rows, w_ada[l], b_ada[l])
    sh1, sc1, g1, sh2, sc2, g2 = [m[:b, None, :] for m in jnp.split(mod, 6, axis=-1)]
    csh1, csc1 = [jnp.broadcast_to(m[b:b + 1, None, :], (b, 1, d)) for m in jnp.split(mod, 6, axis=-1)[:2]]

    w_in_b = w_in[l].astype(BF16)
    p = _inproj(x, sh1, sc1, w_in_b)
    pc = _inproj(ctx, csh1, csc1, w_in_b)

    o_f, o_b = _hgrn(p, pc, lb_fwd, lb_bwd)
    y_na = _natten(p, pc, *_na_tables(na_rpb[l], s))

    x1, h2, topi, topw, cnt = _merge(o_f, o_b, p, y_na, x, g1, sh2, sc2, hg_norm_g[l], ln1_g[l], ln1_b[l],
                                     w_branch_a[l].astype(BF16), w_branch_b[l].astype(BF16),
                                     w_out[l].astype(BF16), w_router[l].T, router_bias[l], alpha)

    return _moe(h2, topi, topw, cnt, x1, g2,
                w_e_gate[l], w_e_up[l], w_e_down[l],
                w_sh_gate[l].astype(BF16), w_sh_up[l].astype(BF16), w_sh_down[l].astype(BF16),
                ln2_g[l], ln2_b[l], alpha)
```

```python
import functools

import numpy as np
import jax
import jax.numpy as jnp
from jax import lax
from jax.experimental import pallas as pl
from jax.experimental.pallas import tpu as pltpu
from jax.experimental.pallas import tpu_sc as plsc

F32 = jnp.float32
BF16 = jnp.bfloat16

D_MODEL = 1024
GRID_W = 64
HG_HEADS = 8
HG_DK = 128
HG_CHUNK = 64
NA_HEADS = 16
NA_HD = 64
NA_WIN_R = 8
NA_WIN_C = 16
ROPE_THETA = 10000.0
NEG_INF = -1e30
N_EXPERTS = 64
EXPERT_DIM = 256
TOP_K = 8
N_GROUPS = 8
TOPK_GROUPS = 4
ROUTED_SCALE = 2.5
LN_EPS = 1e-6
N_SECTIONS = 10
SEC_Q, SEC_FF, SEC_FB, SEC_I, SEC_OG, SEC_NQ, SEC_NK, SEC_NV, SEC_GA, SEC_GB = range(10)
CTX_SECTIONS = (SEC_FF, SEC_FB, SEC_I, SEC_NK, SEC_NV)

VMEM_LIMIT = 56 * 1024 * 1024


def _cparams(sem):
    return pltpu.CompilerParams(dimension_semantics=sem, vmem_limit_bytes=VMEM_LIMIT)


def _normalize(x):
    mu = jnp.mean(x, axis=-1, keepdims=True)
    xc = x - mu
    var = jnp.mean(xc * xc, axis=-1, keepdims=True)
    return xc * lax.rsqrt(var + LN_EPS)


def _silu(x):
    return x * jax.nn.sigmoid(x)


def _dot(a, b):
    return jnp.dot(a, b, preferred_element_type=F32)


def _dot_nt(a, b):
    return lax.dot_general(a, b, (((1,), (1,)), ((), ())), preferred_element_type=F32)


def _dot_tn(a, b):
    return lax.dot_general(a, b, (((0,), (0,)), ((), ())), preferred_element_type=F32)


U32 = jnp.uint32


def _pack_words(x):
    half = x.shape[1] // 2
    lo = lax.bitcast_convert_type(x[:, :half].astype(BF16).astype(F32), U32) >> 16
    hi = lax.bitcast_convert_type(x[:, half:].astype(BF16).astype(F32), U32) & jnp.uint32(0xFFFF0000)
    return lo | hi


def _unpack_words(w):
    lo = lax.bitcast_convert_type(w << 16, F32)
    hi = lax.bitcast_convert_type(w & jnp.uint32(0xFFFF0000), F32)
    return jnp.concatenate([lo, hi], axis=-1)


def _split3(x):
    hi = x.astype(BF16)
    r1 = x - hi.astype(F32)
    mid = r1.astype(BF16)
    lo = (r1 - mid.astype(F32)).astype(BF16)
    return hi, mid, lo


def _ada_kernel(c_ref, w_ref, b_ref, o_ref):
    cond = _silu(c_ref[...])
    o_ref[...] = _dot(cond.astype(BF16), w_ref[...].astype(BF16)) + b_ref[...]


def _ada(cond_rows, w_ada, b_ada):
    r, d = cond_rows.shape
    n = w_ada.shape[1]
    tn = 1024
    return pl.pallas_call(
        _ada_kernel,
        grid=(n // tn,),
        in_specs=[pl.BlockSpec((r, d), lambda j: (0, 0)),
                  pl.BlockSpec((d, tn), lambda j: (0, j)),
                  pl.BlockSpec((1, tn), lambda j: (0, j))],
        out_specs=pl.BlockSpec((r, tn), lambda j: (0, j)),
        out_shape=jax.ShapeDtypeStruct((r, n), F32),
        compiler_params=_cparams(("arbitrary",)),
        name="ada",
    )(cond_rows, w_ada, b_ada.reshape(1, n))


INPROJ_TOK = 2048


def _inproj_kernel(x_ref, sh_ref, sc_ref, w_ref, o_ref, h_ref):
    @pl.when(pl.program_id(2) == 0)
    def _():
        h = _normalize(x_ref[...]) * (1.0 + sc_ref[...]) + sh_ref[...]
        h_ref[...] = h.astype(BF16)

    o_ref[...] = _dot(h_ref[...], w_ref[...])


def _inproj(x, shift, scale, w_in_bf16, sections):
    b, s, d = x.shape
    tm = min(INPROJ_TOK, s)
    nj = len(sections)

    def section(j):
        sec = sections[-1]
        for k in range(nj - 2, -1, -1):
            sec = jnp.where(j == k, sections[k], sec)
        return sec

    return pl.pallas_call(
        _inproj_kernel,
        grid=(b, s // tm, nj),
        in_specs=[pl.BlockSpec((None, tm, d), lambda bi, i, j: (bi, i, 0)),
                  pl.BlockSpec((None, 1, d), lambda bi, i, j: (bi, 0, 0)),
                  pl.BlockSpec((None, 1, d), lambda bi, i, j: (bi, 0, 0)),
                  pl.BlockSpec((d, d), lambda bi, i, j: (0, section(j)))],
        out_specs=pl.BlockSpec((None, None, tm, d), lambda bi, i, j: (j, bi, i, 0)),
        out_shape=jax.ShapeDtypeStruct((nj, b, s, d), F32),
        scratch_shapes=[pltpu.VMEM((tm, d), BF16)],
        compiler_params=_cparams(("arbitrary", "arbitrary", "arbitrary")),
        name="inproj",
    )(x, shift, scale, w_in_bf16)


def _hgrn_gates(q, fraw, v, lb, tri_bf16, last_row):
    f = lb + (1.0 - lb) * jax.nn.sigmoid(fraw)
    k = 1.0 - f
    lf = jnp.log(f)
    hi, mid, lo = _split3(lf)
    a = _dot(tri_bf16, hi) + _dot(tri_bf16, mid) + _dot(tri_bf16, lo)
    a_last = a[last_row:last_row + 1, :]
    kd = (k * jnp.exp(a_last - a)).astype(BF16)
    decay = jnp.exp(a_last)
    qa = kb = None
    if q is not None:
        qa = (_silu(q) * jnp.exp(a)).astype(BF16)
        kb = (k * jnp.exp(-a)).astype(BF16)
    return qa, kb, kd, v.astype(BF16), decay


def _hgrn_chunks(chunks, st_ref):
    first = []
    for d, ((qa, kb, kd, vb, decay), keep) in enumerate(chunks):
        for h in range(HG_HEADS):
            sl = slice(h * HG_DK, (h + 1) * HG_DK)
            st = st_ref[d, h]
            if qa is not None:
                first.append((_dot_nt(qa[:, sl], kb[:, sl]), _dot_nt(qa[:, sl], st.astype(BF16))))
            st_ref[d, h] = st * decay[:, sl] + _dot_tn(vb[:, sl], kd[:, sl])
    results = []
    for d, ((qa, kb, kd, vb, decay), keep) in enumerate(chunks):
        if qa is None:
            results.append(None)
            continue
        outs = []
        for h in range(HG_HEADS):
            sl = slice(h * HG_DK, (h + 1) * HG_DK)
            s_qk, o_state = first.pop(0)
            outs.append(_dot(jnp.where(keep, s_qk, 0.0).astype(BF16), vb[:, sl]) + o_state)
        results.append(jnp.concatenate(outs, axis=-1))
    return results


def _hgrn_kernel(qf_ref, ff_ref, if_ref, qb_ref, fb_ref, ib_ref, cff_ref, cfb_ref, ci_ref,
                 lbf_ref, lbb_ref, of_ref, ob_ref, st_ref, *, n_sub, n_ctx_sub):
    n = pl.program_id(1)
    c = HG_CHUNK
    row = lax.broadcasted_iota(jnp.int32, (c, c), 0)
    col = lax.broadcasted_iota(jnp.int32, (c, c), 1)
    keep_f = col <= row
    keep_b = col >= row
    tri_f = keep_f.astype(F32).astype(BF16)
    tri_b = keep_b.astype(F32).astype(BF16)
    lbf = lbf_ref[...]
    lbb = lbb_ref[...]

    @pl.when(n == 0)
    def _():
        st_ref[...] = jnp.zeros_like(st_ref)

        def body(i, carry):
            r0 = pl.multiple_of(i * c, c)
            r1 = pl.multiple_of((n_ctx_sub - 1 - i) * c, c)
            gf = _hgrn_gates(None, cff_ref[pl.ds(r0, c), :], ci_ref[pl.ds(r0, c), :], lbf, tri_f, c - 1)
            gb = _hgrn_gates(None, cfb_ref[pl.ds(r1, c), :], ci_ref[pl.ds(r1, c), :], lbb, tri_b, 0)
            _hgrn_chunks([(gf, keep_f), (gb, keep_b)], st_ref)
            return carry

        lax.fori_loop(0, n_ctx_sub, body, 0)

    @pl.when(n > 0)
    def _():
        def body(i, carry):
            r0 = pl.multiple_of(i * c, c)
            r1 = pl.multiple_of((n_sub - 1 - i) * c, c)
            gf = _hgrn_gates(qf_ref[pl.ds(r0, c), :], ff_ref[pl.ds(r0, c), :], if_ref[pl.ds(r0, c), :],
                             lbf, tri_f, c - 1)
            gb = _hgrn_gates(qb_ref[pl.ds(r1, c), :], fb_ref[pl.ds(r1, c), :], ib_ref[pl.ds(r1, c), :],
                             lbb, tri_b, 0)
            o_f, o_b = _hgrn_chunks([(gf, keep_f), (gb, keep_b)], st_ref)
            of_ref[pl.ds(r0, c), :] = o_f
            ob_ref[pl.ds(r1, c), :] = o_b
            return carry

        lax.fori_loop(0, n_sub, body, 0, unroll=True)


def _hgrn(p, pc, lb_fwd, lb_bwd):
    _, b, s, w = p.shape
    ctx_len = pc.shape[2]
    tb = min(256, s)
    nb = s // tb
    fwd = lambda bi, n: jnp.maximum(n - 1, 0)
    bwd = lambda bi, n: nb - 1 - jnp.maximum(n - 1, 0)

    def sec(section, blk):
        return pl.BlockSpec((None, None, tb, w), lambda bi, n: (section, bi, blk(bi, n), 0))

    def csec(section):
        return pl.BlockSpec((None, None, ctx_len, w), lambda bi, n: (CTX_SECTIONS.index(section), bi, 0, 0))

    vec = pl.BlockSpec((1, w), lambda bi, n: (0, 0))
    kern = functools.partial(_hgrn_kernel, n_sub=tb // HG_CHUNK, n_ctx_sub=ctx_len // HG_CHUNK)
    return pl.pallas_call(
        kern,
        grid=(b, nb + 1),
        in_specs=[sec(SEC_Q, fwd), sec(SEC_FF, fwd), sec(SEC_I, fwd),
                  sec(SEC_Q, bwd), sec(SEC_FB, bwd), sec(SEC_I, bwd),
                  csec(SEC_FF), csec(SEC_FB), csec(SEC_I), vec, vec],
        out_specs=[pl.BlockSpec((None, tb, w), lambda bi, n: (bi, fwd(bi, n), 0)),
                   pl.BlockSpec((None, tb, w), lambda bi, n: (bi, bwd(bi, n), 0))],
        out_shape=[jax.ShapeDtypeStruct((b, s, w), F32), jax.ShapeDtypeStruct((b, s, w), F32)],
        scratch_shapes=[pltpu.VMEM((2, HG_HEADS, HG_DK, HG_DK), F32)],
        compiler_params=_cparams(("arbitrary", "arbitrary")),
        name="hgrn",
    )(p, p, p, p, p, p, pc, pc, pc, lb_fwd.reshape(1, w), lb_bwd.reshape(1, w))


NA_ROWS_PER_STEP = 32
NA_PREP_ROWS = 512
NA_KEY_TILE = 128
NA_SPAN = NA_WIN_R * GRID_W


def _rope(t, cos, sin_signed, first_half):
    w = t.shape[-1]
    partner = jnp.where(first_half, pltpu.roll(t, w - 16, 1), pltpu.roll(t, 16, 1))
    return t * cos + partner * sin_signed


def _fold_lanes(op, *arrays):
    tiles = [a[:, c:c + 128] for a in arrays for c in range(0, a.shape[-1], 128)]
    acc = tiles[0]
    for t in tiles[1:]:
        acc = op(acc, t)
    return acc


def _rope_tables(rowtab_ref, coltab_ref, row0, n_rows, row_lane):
    out = []
    for i in range(2):
        rt = rowtab_ref[i, pl.ds(row0, n_rows), :]
        by_row = jnp.concatenate([jnp.broadcast_to(rt[r:r + 1, :], (GRID_W, rt.shape[1])) for r in range(n_rows)],
                                 axis=0)
        by_col = jnp.concatenate([coltab_ref[i]] * n_rows, axis=0)
        out.append(jnp.where(row_lane, by_row, by_col))
    return out


def _natten_kernel(q_ref, k_ref, v_ref, kc_ref, vc_ref, rowtab_ref, coltab_ref, t2_ref, o_ref,
                   kt_s, v_s, kc_s, vc_s, bias_s, tail_s, *, rows):
    rblk = pl.program_id(2)
    hd = NA_HD
    lane = lax.broadcasted_iota(jnp.int32, (1, 2 * hd), 1)
    first_half = (lane % 32) < 16
    row_lane = (lane % hd) < hd // 2
    scale = NA_HD ** -0.5

    def values_and_ones(v_pair, h):
        vh = v_pair if h == 0 else pltpu.roll(v_pair, hd, 1)
        return jnp.where(lane < hd, vh, jnp.where(lane == hd, 1.0, 0.0)).astype(BF16)

    @pl.when(rblk == 0)
    def _():
        kc = kc_ref[...].astype(BF16)
        qi = lax.broadcasted_iota(jnp.int32, (GRID_W, GRID_W), 0)
        ki = lax.broadcasted_iota(jnp.int32, (GRID_W, GRID_W), 1)
        cstart = jnp.clip(qi - NA_WIN_C // 2, 0, GRID_W - NA_WIN_C)
        in_win = (ki >= cstart) & (ki < cstart + NA_WIN_C)
        s_len = k_ref.shape[0]
        tail_s[...] = jnp.zeros_like(tail_s)
        for h in range(2):
            sl = slice(h * hd, (h + 1) * hd)
            kc_s[h] = kc[:, sl]
            vc_s[h] = values_and_ones(vc_ref[...], h)
            tiles = [jnp.where(in_win, t2_ref[h, dr], NEG_INF) for dr in range(2 * NA_WIN_R - 1)]
            for v in range(NA_WIN_R):
                for j in range(NA_WIN_R):
                    bias_s[h, v, :, j * GRID_W:(j + 1) * GRID_W] = tiles[NA_WIN_R - 1 - v + j]

        eye = (lax.broadcasted_iota(jnp.int32, (2 * hd, 2 * hd), 0)
               == lax.broadcasted_iota(jnp.int32, (2 * hd, 2 * hd), 1)).astype(F32).astype(BF16)

        def prep(i, carry):
            r0 = pl.multiple_of(i * NA_PREP_ROWS, NA_PREP_ROWS)
            rws = pl.ds(r0, NA_PREP_ROWS)
            cos, sin = _rope_tables(rowtab_ref, coltab_ref, i * (NA_PREP_ROWS // GRID_W), NA_PREP_ROWS // GRID_W,
                                    row_lane)
            kr = _rope(k_ref[rws, :], cos, sin, first_half)
            kr_odd = jnp.concatenate([tail_s[...], kr[:NA_PREP_ROWS - GRID_W]], axis=0)
            tail_s[...] = kr[NA_PREP_ROWS - GRID_W:]
            krt = [_dot_nt(eye, kr.astype(BF16)).astype(BF16),
                   _dot_nt(eye, kr_odd.astype(BF16)).astype(BF16)]
            vv = v_ref[rws, :]
            for h in range(2):
                sl = slice(h * hd, (h + 1) * hd)
                for par in range(2):
                    for c in range(NA_PREP_ROWS // NA_KEY_TILE):
                        kt_s[h, par, i * (NA_PREP_ROWS // NA_KEY_TILE) + c] = (
                            krt[par][sl, c * NA_KEY_TILE:(c + 1) * NA_KEY_TILE])
                v_s[h, rws, :] = values_and_ones(vv, h)
            return carry

        lax.fori_loop(0, s_len // NA_PREP_ROWS, prep, 0, unroll=4)

    tq = NA_ROWS_PER_STEP * GRID_W
    q = q_ref[...] * scale
    cos, sin = _rope_tables(rowtab_ref, coltab_ref, rblk * NA_ROWS_PER_STEP, NA_ROWS_PER_STEP, row_lane)
    qr = _rope(q, cos, sin, first_half)
    qb = q.astype(BF16)
    qrb = qr.astype(BF16)
    rws = [slice(rr * GRID_W, (rr + 1) * GRID_W) for rr in range(NA_ROWS_PER_STEP)]
    par, slot0, key0, bidx = [], [], [], []
    for rr in range(NA_ROWS_PER_STEP):
        r = rblk * NA_ROWS_PER_STEP + rr
        rs = jnp.clip(r - NA_WIN_R // 2, 0, rows - NA_WIN_R)
        par.append(rs & 1)
        slot0.append(lax.shift_right_logical(rs, 1) + (rs & 1))
        key0.append(pl.multiple_of(rs * GRID_W, GRID_W))
        bidx.append(r - rs)

    def scores(h):
        sl = slice(h * hd, (h + 1) * hd)
        qrb_h = qrb[:, sl]
        s_ctx_all = _dot_nt(qb[:, sl], kc_s[h])
        s_win = []
        for rr in range(NA_ROWS_PER_STEP):
            kt = kt_s[h, par[rr], pl.ds(slot0[rr], NA_SPAN // NA_KEY_TILE)]
            kt = jnp.concatenate([kt[c] for c in range(NA_SPAN // NA_KEY_TILE)], axis=-1)
            s_win.append(_dot(qrb_h[rws[rr]], kt))
        return s_win, s_ctx_all

    def softmax(h, s_win, s_ctx_all):
        e_win, e_ctx = [], []
        for rr in range(NA_ROWS_PER_STEP):
            sw = s_win[rr] + bias_s[h, bidx[rr]]
            sc = s_ctx_all[rws[rr]]
            m = jnp.max(_fold_lanes(jnp.maximum, sw, sc), axis=-1, keepdims=True)
            e_win.append(jnp.exp(sw - m).astype(BF16))
            e_ctx.append(jnp.exp(sc - m).astype(BF16))
        return e_win, e_ctx

    def values(h, e_win, e_ctx):
        o_win = []
        for rr in range(NA_ROWS_PER_STEP):
            o_win.append(_dot(e_win[rr], v_s[h, pl.ds(key0[rr], NA_SPAN), :]))
        o = jnp.concatenate(o_win, axis=0) + _dot(jnp.concatenate(e_ctx, axis=0), vc_s[h])
        return o[:, :hd] * (1.0 / o[:, hd:hd + 1])

    s0 = scores(0)
    s1 = scores(1)
    p0 = softmax(0, *s0)
    o0 = values(0, *p0)
    p1 = softmax(1, *s1)
    o1 = values(1, *p1)
    o_ref[...] = jnp.concatenate([o0, o1], axis=-1)


def _na_tables(rpb, s):
    half = NA_HD // 2
    inv = jnp.power(ROPE_THETA, -jnp.arange(0, half, 2, dtype=F32) / half)

    def tables(n):
        ang = jnp.arange(n, dtype=F32)[:, None] * inv[None, :]
        reps = 2 * NA_HD // half
        return jnp.stack([jnp.tile(jnp.cos(ang), (1, 2 * reps)),
                          jnp.tile(jnp.concatenate([-jnp.sin(ang), jnp.sin(ang)], axis=-1), (1, reps))])

    rowtab, coltab = tables(s // GRID_W), tables(GRID_W)

    pad = GRID_W - NA_WIN_C
    rp = jnp.pad(rpb.astype(F32), ((0, 0), (0, 0), (pad, pad)), mode="edge")
    t2 = jnp.stack([rp[:, :, GRID_W - 1 - qc:2 * GRID_W - 1 - qc] for qc in range(GRID_W)], axis=2)
    return rowtab, coltab, t2


def _natten(p, pc, rowtab, coltab, t2):
    _, b, s, w = p.shape
    ctx_len = pc.shape[2]
    rows = s // GRID_W
    assert rows >= NA_WIN_R and rows % NA_ROWS_PER_STEP == 0
    tq = NA_ROWS_PER_STEP * GRID_W
    hw = 2 * NA_HD
    nhp = w // hw
    kern = functools.partial(_natten_kernel, rows=rows)
    return pl.pallas_call(
        kern,
        grid=(b, nhp, rows // NA_ROWS_PER_STEP),
        in_specs=[pl.BlockSpec((None, None, tq, hw), lambda bi, hp, r: (SEC_NQ, bi, r, hp)),
                  pl.BlockSpec((None, None, s, hw), lambda bi, hp, r: (SEC_NK, bi, 0, hp)),
                  pl.BlockSpec((None, None, s, hw), lambda bi, hp, r: (SEC_NV, bi, 0, hp)),
                  pl.BlockSpec((None, None, ctx_len, hw), lambda bi, hp, r: (CTX_SECTIONS.index(SEC_NK), bi, 0, hp)),
                  pl.BlockSpec((None, None, ctx_len, hw), lambda bi, hp, r: (CTX_SECTIONS.index(SEC_NV), bi, 0, hp)),
                  pl.BlockSpec((2, rows, hw), lambda bi, hp, r: (0, 0, 0)),
                  pl.BlockSpec((2, GRID_W, hw), lambda bi, hp, r: (0, 0, 0)),
                  pl.BlockSpec((2, 2 * NA_WIN_R - 1, GRID_W, GRID_W), lambda bi, hp, r: (hp, 0, 0, 0))],
        out_specs=pl.BlockSpec((None, tq, hw), lambda bi, hp, r: (bi, r, hp)),
        out_shape=jax.ShapeDtypeStruct((b, s, w), F32),
        scratch_shapes=[pltpu.VMEM((2, 2, s // NA_KEY_TILE, NA_HD, NA_KEY_TILE), BF16),
                        pltpu.VMEM((2, s, hw), BF16),
                        pltpu.VMEM((2, ctx_len, NA_HD), BF16), pltpu.VMEM((2, ctx_len, hw), BF16),
                        pltpu.VMEM((2, NA_WIN_R, GRID_W, NA_SPAN), F32),
                        pltpu.VMEM((GRID_W, hw), F32)],
        compiler_params=_cparams(("arbitrary", "arbitrary", "arbitrary")),
        name="natten",
    )(p, p, p, pc, pc, rowtab, coltab, t2)


def _route(logits_t, rbias):
    e, t = logits_t.shape
    gsz = e // N_GROUPS
    scores = jax.nn.sigmoid(logits_t)
    sel = scores + rbias
    neg = -jnp.inf
    sub = lax.broadcasted_iota(jnp.int32, (gsz, t), 0).astype(F32)
    gscore = []
    for g in range(N_GROUPS):
        grp = sel[g * gsz:(g + 1) * gsz, :]
        m1 = jnp.max(grp, axis=0, keepdims=True)
        first = jnp.min(jnp.where(grp == m1, sub, float(gsz)), axis=0, keepdims=True)
        m2 = jnp.max(jnp.where(sub == first, neg, grp), axis=0, keepdims=True)
        gscore.append(m1 + m2)
    masked = []
    for g in range(N_GROUPS):
        rank = jnp.zeros((1, t), F32)
        for g2 in range(N_GROUPS):
            if g2 == g:
                continue
            if g2 < g:
                ahead = gscore[g2] >= gscore[g]
            else:
                ahead = gscore[g2] > gscore[g]
            rank = rank + jnp.where(ahead, 1.0, 0.0)
        masked.append(jnp.where(rank < TOPK_GROUPS, sel[g * gsz:(g + 1) * gsz, :], neg))
    work = jnp.concatenate(masked, axis=0)
    eidx = lax.broadcasted_iota(jnp.int32, (e, t), 0).astype(F32)
    idxs, ws = [], []
    chosen = jnp.zeros((e, t), F32)
    for _ in range(TOP_K):
        m = jnp.max(work, axis=0, keepdims=True)
        first = jnp.min(jnp.where(work == m, eidx, float(e)), axis=0, keepdims=True)
        pick = eidx == first
        idxs.append(first)
        ws.append(jnp.sum(jnp.where(pick, scores, 0.0), axis=0, keepdims=True))
        chosen = jnp.where(pick, 1.0, chosen)
        work = jnp.where(pick, neg, work)
    w = jnp.concatenate(ws, axis=0)
    w = w / jnp.sum(w, axis=0, keepdims=True) * ROUTED_SCALE
    return jnp.concatenate(idxs, axis=0).astype(jnp.int32), w, chosen


MERGE_TOK = 512
MERGE_SUB = 256


def _merge_kernel(of_ref, ob_ref, og_ref, yna_ref, ga_ref, gb_ref, x_ref, g1_ref, sh2_ref, sc2_ref,
                  hgg_ref, ln1g_ref, ln1b_ref, wa_ref, wb_ref, wo_ref, wr_ref, rb_ref,
                  x1_ref, h2_ref, topi_ref, topw_ref, cnt_ref, *, alpha):
    tm = x_ref.shape[0]
    subs = [slice(i * MERGE_SUB, (i + 1) * MERGE_SUB) for i in range(tm // MERGE_SUB)]

    def branches(rows):
        o = of_ref[rows, :] + ob_ref[rows, :]
        parts = []
        for h in range(HG_HEADS):
            oh = o[:, h * HG_DK:(h + 1) * HG_DK]
            parts.append(oh * lax.rsqrt(jnp.mean(oh * oh, axis=-1, keepdims=True) + LN_EPS))
        y_hg = jnp.concatenate(parts, axis=-1) * hgg_ref[...] * _silu(og_ref[rows, :])
        return _dot(y_hg.astype(BF16), wa_ref[...]), _dot(yna_ref[rows, :].astype(BF16), wb_ref[...])

    def out_proj(rows, ya, yb):
        t = jax.nn.sigmoid(ga_ref[rows, :]) * ya + jax.nn.sigmoid(gb_ref[rows, :]) * yb
        return _dot(t.astype(BF16), wo_ref[...])

    def norms_router(rows, i, y):
        x1 = _normalize(alpha * x_ref[rows, :] + g1_ref[...] * y) * ln1g_ref[...] + ln1b_ref[...]
        x1_ref[rows, :] = x1
        h2 = _normalize(x1) * (1.0 + sc2_ref[...]) + sh2_ref[...]
        h2_ref[rows, :] = _pack_words(h2)
        hh, hm, hl = _split3(h2)
        wh, wm, wl = _split3(wr_ref[...])
        return (_dot_nt(wh, hh) + _dot_nt(wh, hm) + _dot_nt(wm, hh)
                + _dot_nt(wh, hl) + _dot_nt(wl, hh) + _dot_nt(wm, hm))

    ab = [branches(rows) for rows in subs]
    ys = [out_proj(rows, *ab[i]) for i, rows in enumerate(subs)]
    logits = [norms_router(rows, i, ys[i]) for i, rows in enumerate(subs)]

    @pl.when((pl.program_id(0) == 0) & (pl.program_id(1) == 0))
    def _():
        cnt_ref[...] = jnp.zeros_like(cnt_ref)

    for i, rows in enumerate(subs):
        topi, topw, chosen = _route(logits[i], rb_ref[...])
        topi_ref[:, rows] = topi
        topw_ref[:, rows] = topw
        cnt_ref[...] += jnp.sum(chosen, axis=1, keepdims=True)


def _merge(o_f, o_b, p, y_na, x, g1, sh2, sc2, hg_norm_g, ln1_g, ln1_b, w_a, w_b, w_o, w_router_t, router_bias,
           alpha):
    b, s, d = x.shape
    tm = min(MERGE_TOK, s)
    e = w_router_t.shape[0]
    tok = lambda bi, i: (bi, i, 0)
    blk = pl.BlockSpec((None, tm, d), tok)

    def sec(section):
        return pl.BlockSpec((None, None, tm, d), lambda bi, i: (section, bi, i, 0))

    mod = pl.BlockSpec((None, 1, d), lambda bi, i: (bi, 0, 0))
    vec = pl.BlockSpec((1, d), lambda bi, i: (0, 0))
    mat = pl.BlockSpec((d, d), lambda bi, i: (0, 0), pipeline_mode=pl.Buffered(1))
    return pl.pallas_call(
        functools.partial(_merge_kernel, alpha=alpha),
        grid=(b, s // tm),
        in_specs=[blk, blk, sec(SEC_OG), blk, sec(SEC_GA), sec(SEC_GB), blk, mod, mod, mod,
                  vec, vec, vec, mat, mat, mat,
                  pl.BlockSpec((e, d), lambda bi, i: (0, 0)),
                  pl.BlockSpec((e, 1), lambda bi, i: (0, 0))],
        out_specs=[blk,
                   pl.BlockSpec((tm, d // 2), lambda bi, i: (bi * (s // tm) + i, 0)),
                   pl.BlockSpec((None, TOP_K, tm), lambda bi, i: (bi, 0, i)),
                   pl.BlockSpec((None, TOP_K, tm), lambda bi, i: (bi, 0, i)),
                   pl.BlockSpec((e, 128), lambda bi, i: (0, 0))],
        out_shape=[jax.ShapeDtypeStruct((b, s, d), F32),
                   jax.ShapeDtypeStruct((b * s, d // 2), U32),
                   jax.ShapeDtypeStruct((b, TOP_K, s), jnp.int32), jax.ShapeDtypeStruct((b, TOP_K, s), F32),
                   jax.ShapeDtypeStruct((e, 128), F32)],
        compiler_params=_cparams(("arbitrary", "arbitrary")),
        name="merge",
    )(o_f, o_b, p, y_na, p, p, x, g1, sh2, sc2, hg_norm_g.reshape(1, d), ln1_g.reshape(1, d),
      ln1_b.reshape(1, d), w_a, w_b, w_o, w_router_t, router_bias.reshape(e, 1))


MOE_TILE = 512
MOE_TOK = 512


def _plan_kernel(topi_ref, off_ref, dest_ref, carry_ref):
    @pl.when(pl.program_id(0) == 0)
    def _():
        carry_ref[...] = jnp.zeros_like(carry_ref)

    topi = topi_ref[...]
    tok = topi.shape[1]
    eidx = lax.broadcasted_iota(jnp.int32, (N_EXPERTS, tok), 0)
    hits = [eidx == topi[k:k + 1, :] for k in range(TOP_K)]
    m = jnp.zeros((N_EXPERTS, tok), F32)
    for hit in hits:
        m = jnp.where(hit, 1.0, m)
    before = (lax.broadcasted_iota(jnp.int32, (tok, tok), 0)
              < lax.broadcasted_iota(jnp.int32, (tok, tok), 1)).astype(F32).astype(BF16)
    row = off_ref[...] + carry_ref[...] + _dot(m.astype(BF16), before)
    dest = [jnp.sum(jnp.where(hit, row, 0.0), axis=0, keepdims=True) for hit in hits]
    dest_ref[...] = jnp.concatenate(dest, axis=0).astype(jnp.int32)
    carry_ref[...] += jnp.sum(m, axis=1, keepdims=True)


def _plan(topi, seg_off):
    b, k, s = topi.shape
    per_b = s // MOE_TOK
    blk = pl.BlockSpec((None, k, MOE_TOK), lambda i: (i // per_b, 0, i % per_b))
    return pl.pallas_call(
        _plan_kernel,
        grid=(b * per_b,),
        in_specs=[blk, pl.BlockSpec((N_EXPERTS, 1), lambda i: (0, 0))],
        out_specs=blk,
        out_shape=jax.ShapeDtypeStruct((b, k, s), jnp.int32),
        scratch_shapes=[pltpu.VMEM((N_EXPERTS, 1), F32)],
        compiler_params=_cparams(("arbitrary",)),
        name="plan",
    )(topi, seg_off.astype(F32).reshape(N_EXPERTS, 1))


SC_WINDOW = 128


def _sc_workers():
    info = plsc.get_sparse_core_info()
    return info.num_cores, info.num_cores * info.num_subcores


def _scatter_rows(src, idx, zero_idx, n_rows):
    t, w = src.shape
    m, mz = idx.shape[0], zero_idx.shape[0]
    n_cores, n_workers = _sc_workers()
    per_worker, per_worker_z = t // n_workers, mz // n_workers
    assert m % t == 0
    assert per_worker * n_workers == t and per_worker % SC_WINDOW == 0
    assert per_worker_z * n_workers == mz and per_worker_z % SC_WINDOW == 0
    mesh = plsc.VectorSubcoreMesh(core_axis_name="core", subcore_axis_name="subcore")

    @functools.partial(
        pl.kernel, mesh=mesh, out_type=jax.ShapeDtypeStruct((n_rows, w), src.dtype),
        scratch_types=[pltpu.VMEM((SC_WINDOW,), jnp.int32), pltpu.VMEM((SC_WINDOW, w), src.dtype),
                       pltpu.SemaphoreType.DMA])
    def scatter(src_hbm, idx_hbm, zeros_hbm, zero_idx_hbm, out_hbm, idx_v, rows_v, sem):
        worker = lax.axis_index("subcore") * n_cores + lax.axis_index("core")

        @pl.loop(0, per_worker // SC_WINDOW)
        def _(step):
            first = pl.multiple_of(worker * per_worker + step * SC_WINDOW, SC_WINDOW)
            pltpu.sync_copy(src_hbm.at[pl.ds(first, SC_WINDOW)], rows_v)
            for copy in range(m // t):
                pltpu.sync_copy(idx_hbm.at[pl.ds(copy * t + first, SC_WINDOW)], idx_v)
                pltpu.async_copy(rows_v, out_hbm.at[idx_v], sem).wait()

        pltpu.sync_copy(zeros_hbm, rows_v)

        @pl.loop(0, per_worker_z // SC_WINDOW)
        def _(step):
            base = pl.multiple_of(worker * per_worker_z + step * SC_WINDOW, SC_WINDOW)
            pltpu.sync_copy(zero_idx_hbm.at[pl.ds(base, SC_WINDOW)], idx_v)
            pltpu.async_copy(rows_v, out_hbm.at[idx_v], sem).wait()

    return scatter(src, idx, jnp.zeros((SC_WINDOW, w), src.dtype), zero_idx)


EXPERT_RING = 3


def _experts_kernel(te_ref, tb_ref, nt_ref, seg_ref, nxt_ref, xs_ref, wg_ref, wu_ref, wd_ref, ys_ref,
                    xbuf, sem, wg_f32, wu_f32, wd_f32, wg_s, wu_s, wd_s, wsem):
    i = pl.program_id(0)
    n_tiles = nt_ref[0]

    def weight_copies(expert, slot):
        return [pltpu.make_async_copy(src.at[expert], dst.at[slot], wsem.at[slot, n])
                for n, (src, dst) in enumerate(((wg_ref, wg_f32), (wu_ref, wu_f32), (wd_ref, wd_f32)))]

    @pl.when(i == 0)
    def _():
        for c in weight_copies(te_ref[0], 0):
            c.start()

    first = (i == 0) | (seg_ref[i] != seg_ref[jnp.maximum(i - 1, 0)])

    @pl.when((i < n_tiles) & first)
    def _():
        slot = seg_ref[i] & 1
        for c in weight_copies(te_ref[i], slot):
            c.wait()
        wg_s[...] = wg_f32[slot].astype(BF16)
        wu_s[...] = wu_f32[slot].astype(BF16)
        wd_s[...] = wd_f32[slot].astype(BF16)

        @pl.when(nxt_ref[i] >= 0)
        def _():
            for c in weight_copies(nxt_ref[i], 1 - slot):
                c.start()

    def tile_copy(j):
        slot = lax.rem(j, EXPERT_RING)
        row0 = pl.multiple_of(tb_ref[j] * MOE_TILE, MOE_TILE)
        return pltpu.make_async_copy(xs_ref.at[pl.ds(row0, MOE_TILE), :], xbuf.at[slot], sem.at[slot])

    @pl.when(i == 0)
    def _():
        for j in range(EXPERT_RING - 1):
            @pl.when(j < n_tiles)
            def _():
                tile_copy(j).start()

    ahead = i + (EXPERT_RING - 1)

    @pl.when(ahead < n_tiles)
    def _():
        tile_copy(ahead).start()

    @pl.when(i < n_tiles)
    def _():
        tile_copy(i).wait()
        x = _unpack_words(xbuf[lax.rem(i, EXPERT_RING)]).astype(BF16)
        act = _silu(_dot(x, wg_s[...])) * _dot(x, wu_s[...])
        ys_ref[...] = _pack_words(_dot(act.astype(BF16), wd_s[...]))


def _experts(xs, tile_expert, tile_block, n_tiles, tile_segment, next_expert, wg, wu, wd):
    d, f = wg.shape[1], wg.shape[2]
    anywhere = pl.BlockSpec(memory_space=pl.ANY)
    grid_spec = pltpu.PrefetchScalarGridSpec(
        num_scalar_prefetch=5,
        grid=(xs.shape[0] // MOE_TILE,),
        in_specs=[anywhere, anywhere, anywhere, anywhere],
        out_specs=pl.BlockSpec((MOE_TILE, d // 2), lambda i, te, tb, nt, seg, nxt: (tb[i], 0)),
        scratch_shapes=[pltpu.VMEM((EXPERT_RING, MOE_TILE, d // 2), U32), pltpu.SemaphoreType.DMA((EXPERT_RING,)),
                        pltpu.VMEM((2, d, f), F32), pltpu.VMEM((2, d, f), F32), pltpu.VMEM((2, f, d), F32),
                        pltpu.VMEM((d, f), BF16), pltpu.VMEM((d, f), BF16), pltpu.VMEM((f, d), BF16),
                        pltpu.SemaphoreType.DMA((2, 3))],
    )
    return pl.pallas_call(
        _experts_kernel,
        grid_spec=grid_spec,
        out_shape=jax.ShapeDtypeStruct(xs.shape, U32),
        compiler_params=_cparams(("arbitrary",)),
        name="experts",
    )(tile_expert, tile_block, n_tiles, tile_segment, next_expert, xs, wg, wu, wd)


def _gather_rows(table, idx):
    m = idx.shape[0]
    w = table.shape[1]
    win = SC_WINDOW // 2
    n_cores, n_workers = _sc_workers()
    per_worker = m // n_workers
    n_pairs = per_worker // (2 * win)
    assert per_worker * n_workers == m and n_pairs * 2 * win == per_worker
    mesh = plsc.VectorSubcoreMesh(core_axis_name="core", subcore_axis_name="subcore")

    @functools.partial(
        pl.kernel, mesh=mesh, out_type=jax.ShapeDtypeStruct((m, w), table.dtype),
        scratch_types=[pltpu.VMEM((win,), jnp.int32), pltpu.VMEM((win,), jnp.int32),
                       pltpu.VMEM((win, w), table.dtype), pltpu.VMEM((win, w), table.dtype),
                       pltpu.SemaphoreType.DMA, pltpu.SemaphoreType.DMA])
    def gather(table_hbm, idx_hbm, out_hbm, idx_a, idx_b, rows_a, rows_b, sem_a, sem_b):
        worker = lax.axis_index("subcore") * n_cores + lax.axis_index("core")
        start = worker * per_worker

        def request(first, idx_v, rows_v, sem):
            pltpu.sync_copy(idx_hbm.at[pl.ds(first, win)], idx_v)
            pltpu.async_copy(table_hbm.at[idx_v], rows_v, sem)

        def deliver(first, idx_v, rows_v, sem):
            pltpu.make_async_copy(table_hbm.at[idx_v], rows_v, sem).wait()
            pltpu.sync_copy(rows_v, out_hbm.at[pl.ds(first, win)])

        request(pl.multiple_of(start, win), idx_a, rows_a, sem_a)

        @pl.loop(0, n_pairs)
        def _(pair):
            first_a = pl.multiple_of(start + pair * 2 * win, win)
            first_b = pl.multiple_of(first_a + win, win)
            request(first_b, idx_b, rows_b, sem_b)
            deliver(first_a, idx_a, rows_a, sem_a)

            @pl.when(pair + 1 < n_pairs)
            def _():
                request(pl.multiple_of(first_b + win, win), idx_a, rows_a, sem_a)

            deliver(first_b, idx_b, rows_b, sem_b)

    return gather(table, idx)


def _combine_kernel(rows_ref, topw_ref, h_ref, x1_ref, g2_ref, sg_ref, su_ref, sd_ref, ln2g_ref, ln2b_ref,
                    o_ref, *, alpha):
    h = _unpack_words(h_ref[...]).astype(BF16)
    act = _silu(_dot(h, sg_ref[...])) * _dot(h, su_ref[...])
    y = _dot(act.astype(BF16), sd_ref[...])
    w = topw_ref[...].T
    for k in range(TOP_K):
        y = y + w[:, k:k + 1] * _unpack_words(rows_ref[k])
    o_ref[...] = _normalize(alpha * x1_ref[...] + g2_ref[...] * y) * ln2g_ref[...] + ln2b_ref[...]


def _combine(gathered, topw, h2p, x1, g2, sg, su, sd, ln2_g, ln2_b, alpha):
    t, d = x1.shape
    b, k, s = topw.shape
    per_b = s // MOE_TOK
    fs = sg.shape[1]
    rows = pl.BlockSpec((MOE_TOK, d), lambda i: (i, 0))
    packed = pl.BlockSpec((MOE_TOK, d // 2), lambda i: (i, 0))
    vec = pl.BlockSpec((1, d), lambda i: (0, 0))
    return pl.pallas_call(
        functools.partial(_combine_kernel, alpha=alpha),
        grid=(t // MOE_TOK,),
        in_specs=[pl.BlockSpec((k, MOE_TOK, d // 2), lambda i: (0, i, 0)),
                  pl.BlockSpec((None, k, MOE_TOK), lambda i: (i // per_b, 0, i % per_b)),
                  packed, rows,
                  pl.BlockSpec((None, 1, d), lambda i: (i // per_b, 0, 0)),
                  pl.BlockSpec((d, fs), lambda i: (0, 0)),
                  pl.BlockSpec((d, fs), lambda i: (0, 0)),
                  pl.BlockSpec((fs, d), lambda i: (0, 0)),
                  vec, vec],
        out_specs=rows,
        out_shape=jax.ShapeDtypeStruct((t, d), F32),
        compiler_params=_cparams(("arbitrary",)),
        name="combine",
    )(gathered, topw, h2p, x1, g2, sg, su, sd, ln2_g.reshape(1, d), ln2_b.reshape(1, d))


def _moe(h2p, topi, topw, cnt, x1, g2, wg, wu, wd, sg, su, sd, ln2_g, ln2_b, alpha):
    b, s, d = x1.shape
    t = b * s
    cnt = cnt[:, 0].astype(jnp.int32)
    tiles_e = (cnt + (MOE_TILE - 1)) // MOE_TILE
    tiles_cum = jnp.cumsum(tiles_e)
    seg_off = (tiles_cum - tiles_e) * MOE_TILE
    n_tiles_max = t * TOP_K // MOE_TILE + N_EXPERTS
    tile_block = jnp.minimum(jnp.arange(n_tiles_max, dtype=jnp.int32), tiles_cum[-1] - 1)
    tile_expert = jnp.sum((tiles_cum[None, :] <= tile_block[:, None]).astype(jnp.int32), axis=1)
    n_tiles = tiles_cum[-1:].astype(jnp.int32)
    present = tiles_e > 0
    seg_of_expert = jnp.cumsum(present.astype(jnp.int32)) - 1
    later = jnp.where(present, jnp.arange(N_EXPERTS, dtype=jnp.int32), N_EXPERTS)
    next_present = jnp.concatenate([lax.cummin(later, reverse=True)[1:], jnp.full((1,), N_EXPERTS, jnp.int32)])
    next_present = jnp.where(next_present < N_EXPERTS, next_present, -1)
    tile_segment = seg_of_expert[tile_expert].astype(jnp.int32)
    next_expert = next_present[tile_expert].astype(jnp.int32)

    dest = jnp.transpose(_plan(topi, seg_off), (1, 0, 2)).reshape(TOP_K * t)
    j = jnp.arange(MOE_TILE, dtype=jnp.int32)[None, :]
    n_pad = (tiles_e * MOE_TILE - cnt)[:, None]
    spare = (n_tiles_max - 1) * MOE_TILE + j
    zero_idx = jnp.where(j < n_pad, (seg_off + cnt)[:, None] + j, spare).reshape(N_EXPERTS * MOE_TILE)
    xs = _scatter_rows(h2p, dest, zero_idx.astype(jnp.int32), n_tiles_max * MOE_TILE)
    ys = _experts(xs, tile_expert, tile_block, n_tiles, tile_segment, next_expert, wg, wu, wd)
    gathered = _gather_rows(ys, dest)
    out = _combine(gathered.reshape(TOP_K, t, d // 2), topw, h2p, x1.reshape(t, d), g2, sg, su, sd,
                   ln2_g, ln2_b, alpha)
    return out.reshape(b, s, d)


def kernel(x, c, ctx, c_ctx, w_ada, b_ada, w_in, hg_lb_fwd, hg_lb_bwd, hg_norm_g, na_rpb, w_branch_a, w_branch_b, w_out, ln1_g, ln1_b, w_router, router_bias, w_e_gate, w_e_up, w_e_down, w_sh_gate, w_sh_up, w_sh_down, ln2_g, ln2_b):
    depth = w_ada.shape[0]
    assert depth == 1, "single-layer block"
    b, s, d = x.shape
    alpha = (2.0 * depth) ** 0.25
    l = 0
    lb_fwd = jnp.cumsum(jax.nn.softmax(hg_lb_fwd.astype(F32), axis=0), axis=0)[l]
    lb_bwd = jnp.cumsum(jax.nn.softmax(hg_lb_bwd.astype(F32), axis=0), axis=0)[l]

    cond_rows = jnp.concatenate([c, c_ctx[None, :], jnp.zeros((8 - b - 1, d), F32)], axis=0)
    mod = _ada(cond_rows, w_ada[l], b_ada[l])
    sh1, sc1, g1, sh2, sc2, g2 = [m[:b, None, :] for m in jnp.split(mod, 6, axis=-1)]
    csh1, csc1 = [jnp.broadcast_to(m[b:b + 1, None, :], (b, 1, d)) for m in jnp.split(mod, 6, axis=-1)[:2]]

    w_in_b = w_in[l].astype(BF16)
    p = _inproj(x, sh1, sc1, w_in_b, tuple(range(N_SECTIONS)))
    pc = _inproj(ctx, csh1, csc1, w_in_b, CTX_SECTIONS)

    o_f, o_b = _hgrn(p, pc, lb_fwd, lb_bwd)
    y_na = _natten(p, pc, *_na_tables(na_rpb[l], s))

    x1, h2, topi, topw, cnt = _merge(o_f, o_b, p, y_na, x, g1, sh2, sc2, hg_norm_g[l], ln1_g[l], ln1_b[l],
                                     w_branch_a[l].astype(BF16), w_branch_b[l].astype(BF16),
                                     w_out[l].astype(BF16), w_router[l].T, router_bias[l], alpha)

    return _moe(h2, topi, topw, cnt, x1, g2,
                w_e_gate[l], w_e_up[l], w_e_down[l],
                w_sh_gate[l].astype(BF16), w_sh_up[l].astype(BF16), w_sh_down[l].astype(BF16),
                ln2_g[l], ln2_b[l], alpha)
```

```python
import functools

import numpy as np
import jax
import jax.numpy as jnp
from jax import lax
from jax.experimental import pallas as pl
from jax.experimental.pallas import tpu as pltpu
from jax.experimental.pallas import tpu_sc as plsc

F32 = jnp.float32
BF16 = jnp.bfloat16

D_MODEL = 1024
GRID_W = 64
HG_HEADS = 8
HG_DK = 128
HG_CHUNK = 64
NA_HEADS = 16
NA_HD = 64
NA_WIN_R = 8
NA_WIN_C = 16
ROPE_THETA = 10000.0
NEG_INF = -1e30
N_EXPERTS = 64
EXPERT_DIM = 256
TOP_K = 8
N_GROUPS = 8
TOPK_GROUPS = 4
ROUTED_SCALE = 2.5
LN_EPS = 1e-6
N_SECTIONS = 10
SEC_Q, SEC_FF, SEC_FB, SEC_I, SEC_OG, SEC_NQ, SEC_NK, SEC_NV, SEC_GA, SEC_GB = range(10)
CTX_SECTIONS = (SEC_FF, SEC_FB, SEC_I, SEC_NK, SEC_NV)

VMEM_LIMIT = 56 * 1024 * 1024


def _cparams(sem):
    return pltpu.CompilerParams(dimension_semantics=sem, vmem_limit_bytes=VMEM_LIMIT)


def _normalize(x):
    mu = jnp.mean(x, axis=-1, keepdims=True)
    xc = x - mu
    var = jnp.mean(xc * xc, axis=-1, keepdims=True)
    return xc * lax.rsqrt(var + LN_EPS)


def _silu(x):
    return x * jax.nn.sigmoid(x)


def _dot(a, b):
    return jnp.dot(a, b, preferred_element_type=F32)


def _dot_nt(a, b):
    return lax.dot_general(a, b, (((1,), (1,)), ((), ())), preferred_element_type=F32)


def _dot_tn(a, b):
    return lax.dot_general(a, b, (((0,), (0,)), ((), ())), preferred_element_type=F32)


U32 = jnp.uint32


def _pack_words(x):
    half = x.shape[1] // 2
    lo = lax.bitcast_convert_type(x[:, :half].astype(BF16).astype(F32), U32) >> 16
    hi = lax.bitcast_convert_type(x[:, half:].astype(BF16).astype(F32), U32) & jnp.uint32(0xFFFF0000)
    return lo | hi


def _unpack_words(w):
    lo = lax.bitcast_convert_type(w << 16, F32)
    hi = lax.bitcast_convert_type(w & jnp.uint32(0xFFFF0000), F32)
    return jnp.concatenate([lo, hi], axis=-1)


def _split3(x):
    hi = x.astype(BF16)
    r1 = x - hi.astype(F32)
    mid = r1.astype(BF16)
    lo = (r1 - mid.astype(F32)).astype(BF16)
    return hi, mid, lo


def _ada_kernel(c_ref, w_ref, b_ref, o_ref):
    cond = _silu(c_ref[...])
    o_ref[...] = _dot(cond.astype(BF16), w_ref[...].astype(BF16)) + b_ref[...]


def _ada(cond_rows, w_ada, b_ada):
    r, d = cond_rows.shape
    n = w_ada.shape[1]
    tn = 1024
    return pl.pallas_call(
        _ada_kernel,
        grid=(n // tn,),
        in_specs=[pl.BlockSpec((r, d), lambda j: (0, 0)),
                  pl.BlockSpec((d, tn), lambda j: (0, j)),
                  pl.BlockSpec((1, tn), lambda j: (0, j))],
        out_specs=pl.BlockSpec((r, tn), lambda j: (0, j)),
        out_shape=jax.ShapeDtypeStruct((r, n), F32),
        compiler_params=_cparams(("arbitrary",)),
        name="ada",
    )(cond_rows, w_ada, b_ada.reshape(1, n))


INPROJ_TOK = 2048


def _inproj_kernel(x_ref, sh_ref, sc_ref, w_ref, o_ref, h_ref):
    @pl.when(pl.program_id(2) == 0)
    def _():
        h = _normalize(x_ref[...]) * (1.0 + sc_ref[...]) + sh_ref[...]
        h_ref[...] = h.astype(BF16)

    o_ref[...] = _dot(h_ref[...], w_ref[...])


def _inproj(x, shift, scale, w_in_bf16, sections):
    b, s, d = x.shape
    tm = min(INPROJ_TOK, s)
    nj = len(sections)

    def section(j):
        sec = sections[-1]
        for k in range(nj - 2, -1, -1):
            sec = jnp.where(j == k, sections[k], sec)
        return sec

    return pl.pallas_call(
        _inproj_kernel,
        grid=(b, s // tm, nj),
        in_specs=[pl.BlockSpec((None, tm, d), lambda bi, i, j: (bi, i, 0)),
                  pl.BlockSpec((None, 1, d), lambda bi, i, j: (bi, 0, 0)),
                  pl.BlockSpec((None, 1, d), lambda bi, i, j: (bi, 0, 0)),
                  pl.BlockSpec((d, d), lambda bi, i, j: (0, section(j)))],
        out_specs=pl.BlockSpec((None, None, tm, d), lambda bi, i, j: (j, bi, i, 0)),
        out_shape=jax.ShapeDtypeStruct((nj, b, s, d), F32),
        scratch_shapes=[pltpu.VMEM((tm, d), BF16)],
        compiler_params=_cparams(("arbitrary", "arbitrary", "arbitrary")),
        name="inproj",
    )(x, shift, scale, w_in_bf16)


def _hgrn_gates(q, fraw, v, lb, tri_bf16, last_row):
    f = lb + (1.0 - lb) * jax.nn.sigmoid(fraw)
    k = 1.0 - f
    lf = jnp.log(f)
    hi, mid, lo = _split3(lf)
    a = _dot(tri_bf16, hi) + _dot(tri_bf16, mid) + _dot(tri_bf16, lo)
    a_last = a[last_row:last_row + 1, :]
    kd = (k * jnp.exp(a_last - a)).astype(BF16)
    decay = jnp.exp(a_last)
    qa = kb = None
    if q is not None:
        qa = (_silu(q) * jnp.exp(a)).astype(BF16)
        kb = (k * jnp.exp(-a)).astype(BF16)
    return qa, kb, kd, v.astype(BF16), decay


def _hgrn_chunks(chunks, st_ref):
    first = []
    for d, ((qa, kb, kd, vb, decay), keep) in enumerate(chunks):
        for h in range(HG_HEADS):
            sl = slice(h * HG_DK, (h + 1) * HG_DK)
            st = st_ref[d, h]
            if qa is not None:
                first.append((_dot_nt(qa[:, sl], kb[:, sl]), _dot_nt(qa[:, sl], st.astype(BF16))))
            st_ref[d, h] = st * decay[:, sl] + _dot_tn(vb[:, sl], kd[:, sl])
    results = []
    for d, ((qa, kb, kd, vb, decay), keep) in enumerate(chunks):
        if qa is None:
            results.append(None)
            continue
        outs = []
        for h in range(HG_HEADS):
            sl = slice(h * HG_DK, (h + 1) * HG_DK)
            s_qk, o_state = first.pop(0)
            outs.append(_dot(jnp.where(keep, s_qk, 0.0).astype(BF16), vb[:, sl]) + o_state)
        results.append(jnp.concatenate(outs, axis=-1))
    return results


def _hgrn_kernel(qf_ref, ff_ref, if_ref, qb_ref, fb_ref, ib_ref, cff_ref, cfb_ref, ci_ref,
                 lbf_ref, lbb_ref, of_ref, ob_ref, st_ref, *, n_sub, n_ctx_sub):
    n = pl.program_id(1)
    c = HG_CHUNK
    row = lax.broadcasted_iota(jnp.int32, (c, c), 0)
    col = lax.broadcasted_iota(jnp.int32, (c, c), 1)
    keep_f = col <= row
    keep_b = col >= row
    tri_f = keep_f.astype(F32).astype(BF16)
    tri_b = keep_b.astype(F32).astype(BF16)
    lbf = lbf_ref[...]
    lbb = lbb_ref[...]

    @pl.when(n == 0)
    def _():
        st_ref[...] = jnp.zeros_like(st_ref)

        def body(i, carry):
            r0 = pl.multiple_of(i * c, c)
            r1 = pl.multiple_of((n_ctx_sub - 1 - i) * c, c)
            gf = _hgrn_gates(None, cff_ref[pl.ds(r0, c), :], ci_ref[pl.ds(r0, c), :], lbf, tri_f, c - 1)
            gb = _hgrn_gates(None, cfb_ref[pl.ds(r1, c), :], ci_ref[pl.ds(r1, c), :], lbb, tri_b, 0)
            _hgrn_chunks([(gf, keep_f), (gb, keep_b)], st_ref)
            return carry

        lax.fori_loop(0, n_ctx_sub, body, 0)

    @pl.when(n > 0)
    def _():
        def body(i, carry):
            r0 = pl.multiple_of(i * c, c)
            r1 = pl.multiple_of((n_sub - 1 - i) * c, c)
            gf = _hgrn_gates(qf_ref[pl.ds(r0, c), :], ff_ref[pl.ds(r0, c), :], if_ref[pl.ds(r0, c), :],
                             lbf, tri_f, c - 1)
            gb = _hgrn_gates(qb_ref[pl.ds(r1, c), :], fb_ref[pl.ds(r1, c), :], ib_ref[pl.ds(r1, c), :],
                             lbb, tri_b, 0)
            o_f, o_b = _hgrn_chunks([(gf, keep_f), (gb, keep_b)], st_ref)
            of_ref[pl.ds(r0, c), :] = o_f
            ob_ref[pl.ds(r1, c), :] = o_b
            return carry

        lax.fori_loop(0, n_sub, body, 0, unroll=True)


def _hgrn(p, pc, lb_fwd, lb_bwd):
    _, b, s, w = p.shape
    ctx_len = pc.shape[2]
    tb = min(256, s)
    nb = s // tb
    fwd = lambda bi, n: jnp.maximum(n - 1, 0)
    bwd = lambda bi, n: nb - 1 - jnp.maximum(n - 1, 0)

    def sec(section, blk):
        return pl.BlockSpec((None, None, tb, w), lambda bi, n: (section, bi, blk(bi, n), 0))

    def csec(section):
        return pl.BlockSpec((None, None, ctx_len, w), lambda bi, n: (CTX_SECTIONS.index(section), bi, 0, 0))

    vec = pl.BlockSpec((1, w), lambda bi, n: (0, 0))
    kern = functools.partial(_hgrn_kernel, n_sub=tb // HG_CHUNK, n_ctx_sub=ctx_len // HG_CHUNK)
    return pl.pallas_call(
        kern,
        grid=(b, nb + 1),
        in_specs=[sec(SEC_Q, fwd), sec(SEC_FF, fwd), sec(SEC_I, fwd),
                  sec(SEC_Q, bwd), sec(SEC_FB, bwd), sec(SEC_I, bwd),
                  csec(SEC_FF), csec(SEC_FB), csec(SEC_I), vec, vec],
        out_specs=[pl.BlockSpec((None, tb, w), lambda bi, n: (bi, fwd(bi, n), 0)),
                   pl.BlockSpec((None, tb, w), lambda bi, n: (bi, bwd(bi, n), 0))],
        out_shape=[jax.ShapeDtypeStruct((b, s, w), F32), jax.ShapeDtypeStruct((b, s, w), F32)],
        scratch_shapes=[pltpu.VMEM((2, HG_HEADS, HG_DK, HG_DK), F32)],
        compiler_params=_cparams(("arbitrary", "arbitrary")),
        name="hgrn",
    )(p, p, p, p, p, p, pc, pc, pc, lb_fwd.reshape(1, w), lb_bwd.reshape(1, w))


NA_ROWS_PER_STEP = 32
NA_PREP_ROWS = 512
NA_KEY_TILE = 128
NA_SPAN = NA_WIN_R * GRID_W


def _rope(t, cos, sin_signed, first_half):
    w = t.shape[-1]
    partner = jnp.where(first_half, pltpu.roll(t, w - 16, 1), pltpu.roll(t, 16, 1))
    return t * cos + partner * sin_signed


def _fold_lanes(op, *arrays):
    tiles = [a[:, c:c + 128] for a in arrays for c in range(0, a.shape[-1], 128)]
    acc = tiles[0]
    for t in tiles[1:]:
        acc = op(acc, t)
    return acc


def _rope_tables(rowtab_ref, coltab_ref, row0, n_rows, row_lane):
    out = []
    for i in range(2):
        rt = rowtab_ref[i, pl.ds(row0, n_rows), :]
        by_row = jnp.concatenate([jnp.broadcast_to(rt[r:r + 1, :], (GRID_W, rt.shape[1])) for r in range(n_rows)],
                                 axis=0)
        by_col = jnp.concatenate([coltab_ref[i]] * n_rows, axis=0)
        out.append(jnp.where(row_lane, by_row, by_col))
    return out


def _natten_kernel(q_ref, k_ref, v_ref, kc_ref, vc_ref, rowtab_ref, coltab_ref, t2_ref, o_ref,
                   kt_s, v_s, kc_s, vc_s, bias_s, tail_s, *, rows):
    rblk = pl.program_id(2)
    hd = NA_HD
    lane = lax.broadcasted_iota(jnp.int32, (1, 2 * hd), 1)
    first_half = (lane % 32) < 16
    row_lane = (lane % hd) < hd // 2
    scale = NA_HD ** -0.5

    def values_and_ones(v_pair, h):
        vh = v_pair if h == 0 else pltpu.roll(v_pair, hd, 1)
        return jnp.where(lane < hd, vh, jnp.where(lane == hd, 1.0, 0.0)).astype(BF16)

    @pl.when(rblk == 0)
    def _():
        kc = kc_ref[...].astype(BF16)
        qi = lax.broadcasted_iota(jnp.int32, (GRID_W, GRID_W), 0)
        ki = lax.broadcasted_iota(jnp.int32, (GRID_W, GRID_W), 1)
        cstart = jnp.clip(qi - NA_WIN_C // 2, 0, GRID_W - NA_WIN_C)
        in_win = (ki >= cstart) & (ki < cstart + NA_WIN_C)
        s_len = k_ref.shape[0]
        tail_s[...] = jnp.zeros_like(tail_s)
        for h in range(2):
            sl = slice(h * hd, (h + 1) * hd)
            kc_s[h] = kc[:, sl]
            vc_s[h] = values_and_ones(vc_ref[...], h)
            tiles = [jnp.where(in_win, t2_ref[h, dr], NEG_INF) for dr in range(2 * NA_WIN_R - 1)]
            for v in range(NA_WIN_R):
                for j in range(NA_WIN_R):
                    bias_s[h, v, :, j * GRID_W:(j + 1) * GRID_W] = tiles[NA_WIN_R - 1 - v + j]

        eye = (lax.broadcasted_iota(jnp.int32, (2 * hd, 2 * hd), 0)
               == lax.broadcasted_iota(jnp.int32, (2 * hd, 2 * hd), 1)).astype(F32).astype(BF16)

        def prep(i, carry):
            r0 = pl.multiple_of(i * NA_PREP_ROWS, NA_PREP_ROWS)
            rws = pl.ds(r0, NA_PREP_ROWS)
            cos, sin = _rope_tables(rowtab_ref, coltab_ref, i * (NA_PREP_ROWS // GRID_W), NA_PREP_ROWS // GRID_W,
                                    row_lane)
            kr = _rope(k_ref[rws, :], cos, sin, first_half)
            kr_odd = jnp.concatenate([tail_s[...], kr[:NA_PREP_ROWS - GRID_W]], axis=0)
            tail_s[...] = kr[NA_PREP_ROWS - GRID_W:]
            krt = [_dot_nt(eye, kr.astype(BF16)).astype(BF16),
                   _dot_nt(eye, kr_odd.astype(BF16)).astype(BF16)]
            vv = v_ref[rws, :]
            for h in range(2):
                sl = slice(h * hd, (h + 1) * hd)
                for par in range(2):
                    for c in range(NA_PREP_ROWS // NA_KEY_TILE):
                        kt_s[h, par, i * (NA_PREP_ROWS // NA_KEY_TILE) + c] = (
                            krt[par][sl, c * NA_KEY_TILE:(c + 1) * NA_KEY_TILE])
                v_s[h, rws, :] = values_and_ones(vv, h)
            return carry

        lax.fori_loop(0, s_len // NA_PREP_ROWS, prep, 0, unroll=4)

    tq = NA_ROWS_PER_STEP * GRID_W
    q = q_ref[...] * scale
    cos, sin = _rope_tables(rowtab_ref, coltab_ref, rblk * NA_ROWS_PER_STEP, NA_ROWS_PER_STEP, row_lane)
    qr = _rope(q, cos, sin, first_half)
    qb = q.astype(BF16)
    qrb = qr.astype(BF16)
    rws = [slice(rr * GRID_W, (rr + 1) * GRID_W) for rr in range(NA_ROWS_PER_STEP)]
    par, slot0, key0, bidx = [], [], [], []
    for rr in range(NA_ROWS_PER_STEP):
        r = rblk * NA_ROWS_PER_STEP + rr
        rs = jnp.clip(r - NA_WIN_R // 2, 0, rows - NA_WIN_R)
        par.append(rs & 1)
        slot0.append(lax.shift_right_logical(rs, 1) + (rs & 1))
        key0.append(pl.multiple_of(rs * GRID_W, GRID_W))
        bidx.append(r - rs)

    def scores(h):
        sl = slice(h * hd, (h + 1) * hd)
        qrb_h = qrb[:, sl]
        s_ctx_all = _dot_nt(qb[:, sl], kc_s[h])
        s_win = []
        for rr in range(NA_ROWS_PER_STEP):
            kt = kt_s[h, par[rr], pl.ds(slot0[rr], NA_SPAN // NA_KEY_TILE)]
            kt = jnp.concatenate([kt[c] for c in range(NA_SPAN // NA_KEY_TILE)], axis=-1)
            s_win.append(_dot(qrb_h[rws[rr]], kt))
        return s_win, s_ctx_all

    def softmax(h, s_win, s_ctx_all):
        e_win, e_ctx = [], []
        for rr in range(NA_ROWS_PER_STEP):
            sw = s_win[rr] + bias_s[h, bidx[rr]]
            sc = s_ctx_all[rws[rr]]
            m = jnp.max(_fold_lanes(jnp.maximum, sw, sc), axis=-1, keepdims=True)
            e_win.append(jnp.exp(sw - m).astype(BF16))
            e_ctx.append(jnp.exp(sc - m).astype(BF16))
        return e_win, e_ctx

    def values(h, e_win, e_ctx):
        o_win = []
        for rr in range(NA_ROWS_PER_STEP):
            o_win.append(_dot(e_win[rr], v_s[h, pl.ds(key0[rr], NA_SPAN), :]))
        o = jnp.concatenate(o_win, axis=0) + _dot(jnp.concatenate(e_ctx, axis=0), vc_s[h])
        return o[:, :hd] * (1.0 / o[:, hd:hd + 1])

    s0 = scores(0)
    s1 = scores(1)
    p0 = softmax(0, *s0)
    o0 = values(0, *p0)
    p1 = softmax(1, *s1)
    o1 = values(1, *p1)
    o_ref[...] = jnp.concatenate([o0, o1], axis=-1)


def _na_tables(rpb, s):
    half = NA_HD // 2
    inv = jnp.power(ROPE_THETA, -jnp.arange(0, half, 2, dtype=F32) / half)

    def tables(n):
        ang = jnp.arange(n, dtype=F32)[:, None] * inv[None, :]
        reps = 2 * NA_HD // half
        return jnp.stack([jnp.tile(jnp.cos(ang), (1, 2 * reps)),
                          jnp.tile(jnp.concatenate([-jnp.sin(ang), jnp.sin(ang)], axis=-1), (1, reps))])

    rowtab, coltab = tables(s // GRID_W), tables(GRID_W)

    pad = GRID_W - NA_WIN_C
    rp = jnp.pad(rpb.astype(F32), ((0, 0), (0, 0), (pad, pad)), mode="edge")
    period = 2 * GRID_W - 1
    skew = jnp.tile(rp, (1, 1, GRID_W + 1))[:, :, :GRID_W * (period + 1)]
    skew = skew.reshape(rp.shape[0], rp.shape[1], GRID_W, period + 1)
    t2 = skew[:, :, ::-1, :GRID_W]
    return rowtab, coltab, t2


def _natten(p, pc, rowtab, coltab, t2):
    _, b, s, w = p.shape
    ctx_len = pc.shape[2]
    rows = s // GRID_W
    assert rows >= NA_WIN_R and rows % NA_ROWS_PER_STEP == 0
    tq = NA_ROWS_PER_STEP * GRID_W
    hw = 2 * NA_HD
    nhp = w // hw
    kern = functools.partial(_natten_kernel, rows=rows)
    return pl.pallas_call(
        kern,
        grid=(b, nhp, rows // NA_ROWS_PER_STEP),
        in_specs=[pl.BlockSpec((None, None, tq, hw), lambda bi, hp, r: (SEC_NQ, bi, r, hp)),
                  pl.BlockSpec((None, None, s, hw), lambda bi, hp, r: (SEC_NK, bi, 0, hp)),
                  pl.BlockSpec((None, None, s, hw), lambda bi, hp, r: (SEC_NV, bi, 0, hp)),
                  pl.BlockSpec((None, None, ctx_len, hw), lambda bi, hp, r: (CTX_SECTIONS.index(SEC_NK), bi, 0, hp)),
                  pl.BlockSpec((None, None, ctx_len, hw), lambda bi, hp, r: (CTX_SECTIONS.index(SEC_NV), bi, 0, hp)),
                  pl.BlockSpec((2, rows, hw), lambda bi, hp, r: (0, 0, 0)),
                  pl.BlockSpec((2, GRID_W, hw), lambda bi, hp, r: (0, 0, 0)),
                  pl.BlockSpec((2, 2 * NA_WIN_R - 1, GRID_W, GRID_W), lambda bi, hp, r: (hp, 0, 0, 0))],
        out_specs=pl.BlockSpec((None, tq, hw), lambda bi, hp, r: (bi, r, hp)),
        out_shape=jax.ShapeDtypeStruct((b, s, w), F32),
        scratch_shapes=[pltpu.VMEM((2, 2, s // NA_KEY_TILE, NA_HD, NA_KEY_TILE), BF16),
                        pltpu.VMEM((2, s, hw), BF16),
                        pltpu.VMEM((2, ctx_len, NA_HD), BF16), pltpu.VMEM((2, ctx_len, hw), BF16),
                        pltpu.VMEM((2, NA_WIN_R, GRID_W, NA_SPAN), F32),
                        pltpu.VMEM((GRID_W, hw), F32)],
        compiler_params=_cparams(("arbitrary", "arbitrary", "arbitrary")),
        name="natten",
    )(p, p, p, pc, pc, rowtab, coltab, t2)


def _route(logits_t, rbias):
    e, t = logits_t.shape
    gsz = e // N_GROUPS
    scores = jax.nn.sigmoid(logits_t)
    sel = scores + rbias
    neg = -jnp.inf
    sub = lax.broadcasted_iota(jnp.int32, (gsz, t), 0).astype(F32)
    gscore = []
    for g in range(N_GROUPS):
        grp = sel[g * gsz:(g + 1) * gsz, :]
        m1 = jnp.max(grp, axis=0, keepdims=True)
        first = jnp.min(jnp.where(grp == m1, sub, float(gsz)), axis=0, keepdims=True)
        m2 = jnp.max(jnp.where(sub == first, neg, grp), axis=0, keepdims=True)
        gscore.append(m1 + m2)
    masked = []
    for g in range(N_GROUPS):
        rank = jnp.zeros((1, t), F32)
        for g2 in range(N_GROUPS):
            if g2 == g:
                continue
            if g2 < g:
                ahead = gscore[g2] >= gscore[g]
            else:
                ahead = gscore[g2] > gscore[g]
            rank = rank + jnp.where(ahead, 1.0, 0.0)
        masked.append(jnp.where(rank < TOPK_GROUPS, sel[g * gsz:(g + 1) * gsz, :], neg))
    work = jnp.concatenate(masked, axis=0)
    eidx = lax.broadcasted_iota(jnp.int32, (e, t), 0).astype(F32)
    idxs, ws = [], []
    chosen = jnp.zeros((e, t), F32)
    for _ in range(TOP_K):
        m = jnp.max(work, axis=0, keepdims=True)
        first = jnp.min(jnp.where(work == m, eidx, float(e)), axis=0, keepdims=True)
        pick = eidx == first
        idxs.append(first)
        ws.append(jnp.sum(jnp.where(pick, scores, 0.0), axis=0, keepdims=True))
        chosen = jnp.where(pick, 1.0, chosen)
        work = jnp.where(pick, neg, work)
    w = jnp.concatenate(ws, axis=0)
    w = w / jnp.sum(w, axis=0, keepdims=True) * ROUTED_SCALE
    return jnp.concatenate(idxs, axis=0).astype(jnp.int32), w, chosen


MERGE_TOK = 512
MERGE_SUB = 256


def _merge_kernel(of_ref, ob_ref, og_ref, yna_ref, ga_ref, gb_ref, x_ref, g1_ref, sh2_ref, sc2_ref,
                  hgg_ref, ln1g_ref, ln1b_ref, wa_ref, wb_ref, wo_ref, wr_ref, rb_ref,
                  x1_ref, h2_ref, topi_ref, topw_ref, cnt_ref, *, alpha):
    tm = x_ref.shape[0]
    subs = [slice(i * MERGE_SUB, (i + 1) * MERGE_SUB) for i in range(tm // MERGE_SUB)]

    def branches(rows):
        o = of_ref[rows, :] + ob_ref[rows, :]
        parts = []
        for h in range(HG_HEADS):
            oh = o[:, h * HG_DK:(h + 1) * HG_DK]
            parts.append(oh * lax.rsqrt(jnp.mean(oh * oh, axis=-1, keepdims=True) + LN_EPS))
        y_hg = jnp.concatenate(parts, axis=-1) * hgg_ref[...] * _silu(og_ref[rows, :])
        return _dot(y_hg.astype(BF16), wa_ref[...]), _dot(yna_ref[rows, :].astype(BF16), wb_ref[...])

    def out_proj(rows, ya, yb):
        t = jax.nn.sigmoid(ga_ref[rows, :]) * ya + jax.nn.sigmoid(gb_ref[rows, :]) * yb
        return _dot(t.astype(BF16), wo_ref[...])

    def norms_router(rows, i, y):
        x1 = _normalize(alpha * x_ref[rows, :] + g1_ref[...] * y) * ln1g_ref[...] + ln1b_ref[...]
        x1_ref[rows, :] = x1
        h2 = _normalize(x1) * (1.0 + sc2_ref[...]) + sh2_ref[...]
        h2_ref[rows, :] = _pack_words(h2)
        hh, hm, hl = _split3(h2)
        wh, wm, wl = _split3(wr_ref[...])
        return (_dot_nt(wh, hh) + _dot_nt(wh, hm) + _dot_nt(wm, hh)
                + _dot_nt(wh, hl) + _dot_nt(wl, hh) + _dot_nt(wm, hm))

    ab = [branches(rows) for rows in subs]
    ys = [out_proj(rows, *ab[i]) for i, rows in enumerate(subs)]
    logits = [norms_router(rows, i, ys[i]) for i, rows in enumerate(subs)]

    @pl.when((pl.program_id(0) == 0) & (pl.program_id(1) == 0))
    def _():
        cnt_ref[...] = jnp.zeros_like(cnt_ref)

    for i, rows in enumerate(subs):
        topi, topw, chosen = _route(logits[i], rb_ref[...])
        topi_ref[:, rows] = topi
        topw_ref[:, rows] = topw
        cnt_ref[...] += jnp.sum(chosen, axis=1, keepdims=True)


def _merge(o_f, o_b, p, y_na, x, g1, sh2, sc2, hg_norm_g, ln1_g, ln1_b, w_a, w_b, w_o, w_router_t, router_bias,
           alpha):
    b, s, d = x.shape
    tm = min(MERGE_TOK, s)
    e = w_router_t.shape[0]
    tok = lambda bi, i: (bi, i, 0)
    blk = pl.BlockSpec((None, tm, d), tok)

    def sec(section):
        return pl.BlockSpec((None, None, tm, d), lambda bi, i: (section, bi, i, 0))

    mod = pl.BlockSpec((None, 1, d), lambda bi, i: (bi, 0, 0))
    vec = pl.BlockSpec((1, d), lambda bi, i: (0, 0))
    mat = pl.BlockSpec((d, d), lambda bi, i: (0, 0), pipeline_mode=pl.Buffered(1))
    return pl.pallas_call(
        functools.partial(_merge_kernel, alpha=alpha),
        grid=(b, s // tm),
        in_specs=[blk, blk, sec(SEC_OG), blk, sec(SEC_GA), sec(SEC_GB), blk, mod, mod, mod,
                  vec, vec, vec, mat, mat, mat,
                  pl.BlockSpec((e, d), lambda bi, i: (0, 0)),
                  pl.BlockSpec((e, 1), lambda bi, i: (0, 0))],
        out_specs=[blk,
                   pl.BlockSpec((tm, d // 2), lambda bi, i: (bi * (s // tm) + i, 0)),
                   pl.BlockSpec((None, TOP_K, tm), lambda bi, i: (bi, 0, i)),
                   pl.BlockSpec((None, TOP_K, tm), lambda bi, i: (bi, 0, i)),
                   pl.BlockSpec((e, 128), lambda bi, i: (0, 0))],
        out_shape=[jax.ShapeDtypeStruct((b, s, d), F32),
                   jax.ShapeDtypeStruct((b * s, d // 2), U32),
                   jax.ShapeDtypeStruct((b, TOP_K, s), jnp.int32), jax.ShapeDtypeStruct((b, TOP_K, s), F32),
                   jax.ShapeDtypeStruct((e, 128), F32)],
        compiler_params=_cparams(("arbitrary", "arbitrary")),
        name="merge",
    )(o_f, o_b, p, y_na, p, p, x, g1, sh2, sc2, hg_norm_g.reshape(1, d), ln1_g.reshape(1, d),
      ln1_b.reshape(1, d), w_a, w_b, w_o, w_router_t, router_bias.reshape(e, 1))


MOE_TILE = 512
MOE_TOK = 512


def _plan_kernel(topi_ref, off_ref, dest_ref, carry_ref):
    @pl.when(pl.program_id(0) == 0)
    def _():
        carry_ref[...] = jnp.zeros_like(carry_ref)

    topi = topi_ref[...]
    tok = topi.shape[1]
    eidx = lax.broadcasted_iota(jnp.int32, (N_EXPERTS, tok), 0)
    hits = [eidx == topi[k:k + 1, :] for k in range(TOP_K)]
    m = jnp.zeros((N_EXPERTS, tok), F32)
    for hit in hits:
        m = jnp.where(hit, 1.0, m)
    before = (lax.broadcasted_iota(jnp.int32, (tok, tok), 0)
              < lax.broadcasted_iota(jnp.int32, (tok, tok), 1)).astype(F32).astype(BF16)
    row = off_ref[...] + carry_ref[...] + _dot(m.astype(BF16), before)
    dest = [jnp.sum(jnp.where(hit, row, 0.0), axis=0, keepdims=True) for hit in hits]
    dest_ref[...] = jnp.concatenate(dest, axis=0).astype(jnp.int32)
    carry_ref[...] += jnp.sum(m, axis=1, keepdims=True)


def _plan(topi, seg_off):
    b, k, s = topi.shape
    per_b = s // MOE_TOK
    blk = pl.BlockSpec((None, k, MOE_TOK), lambda i: (i // per_b, 0, i % per_b))
    return pl.pallas_call(
        _plan_kernel,
        grid=(b * per_b,),
        in_specs=[blk, pl.BlockSpec((N_EXPERTS, 1), lambda i: (0, 0))],
        out_specs=blk,
        out_shape=jax.ShapeDtypeStruct((b, k, s), jnp.int32),
        scratch_shapes=[pltpu.VMEM((N_EXPERTS, 1), F32)],
        compiler_params=_cparams(("arbitrary",)),
        name="plan",
    )(topi, seg_off.astype(F32).reshape(N_EXPERTS, 1))


SC_WINDOW = 128


def _sc_workers():
    info = plsc.get_sparse_core_info()
    return info.num_cores, info.num_cores * info.num_subcores


def _scatter_rows(src, idx, zero_idx, n_rows):
    t, w = src.shape
    m, mz = idx.shape[0], zero_idx.shape[0]
    n_cores, n_workers = _sc_workers()
    per_worker, per_worker_z = t // n_workers, mz // n_workers
    assert m % t == 0
    assert per_worker * n_workers == t and per_worker % SC_WINDOW == 0
    assert per_worker_z * n_workers == mz and per_worker_z % SC_WINDOW == 0
    mesh = plsc.VectorSubcoreMesh(core_axis_name="core", subcore_axis_name="subcore")

    @functools.partial(
        pl.kernel, mesh=mesh, out_type=jax.ShapeDtypeStruct((n_rows, w), src.dtype),
        scratch_types=[pltpu.VMEM((SC_WINDOW,), jnp.int32), pltpu.VMEM((SC_WINDOW, w), src.dtype),
                       pltpu.SemaphoreType.DMA])
    def scatter(src_hbm, idx_hbm, zeros_hbm, zero_idx_hbm, out_hbm, idx_v, rows_v, sem):
        worker = lax.axis_index("subcore") * n_cores + lax.axis_index("core")

        @pl.loop(0, per_worker // SC_WINDOW)
        def _(step):
            first = pl.multiple_of(worker * per_worker + step * SC_WINDOW, SC_WINDOW)
            pltpu.sync_copy(src_hbm.at[pl.ds(first, SC_WINDOW)], rows_v)
            for copy in range(m // t):
                pltpu.sync_copy(idx_hbm.at[pl.ds(copy * t + first, SC_WINDOW)], idx_v)
                pltpu.async_copy(rows_v, out_hbm.at[idx_v], sem).wait()

        pltpu.sync_copy(zeros_hbm, rows_v)

        @pl.loop(0, per_worker_z // SC_WINDOW)
        def _(step):
            base = pl.multiple_of(worker * per_worker_z + step * SC_WINDOW, SC_WINDOW)
            pltpu.sync_copy(zero_idx_hbm.at[pl.ds(base, SC_WINDOW)], idx_v)
            pltpu.async_copy(rows_v, out_hbm.at[idx_v], sem).wait()

    return scatter(src, idx, jnp.zeros((SC_WINDOW, w), src.dtype), zero_idx)


EXPERT_RING = 3


def _experts_kernel(te_ref, tb_ref, nt_ref, seg_ref, nxt_ref, xs_ref, wg_ref, wu_ref, wd_ref, ys_ref,
                    xbuf, sem, wg_f32, wu_f32, wd_f32, wg_s, wu_s, wd_s, wsem):
    i = pl.program_id(0)
    n_tiles = nt_ref[0]

    def weight_copies(expert, slot):
        return [pltpu.make_async_copy(src.at[expert], dst.at[slot], wsem.at[slot, n])
                for n, (src, dst) in enumerate(((wg_ref, wg_f32), (wu_ref, wu_f32), (wd_ref, wd_f32)))]

    @pl.when(i == 0)
    def _():
        for c in weight_copies(te_ref[0], 0):
            c.start()

    first = (i == 0) | (seg_ref[i] != seg_ref[jnp.maximum(i - 1, 0)])

    @pl.when((i < n_tiles) & first)
    def _():
        slot = seg_ref[i] & 1
        for c in weight_copies(te_ref[i], slot):
            c.wait()
        wg_s[...] = wg_f32[slot].astype(BF16)
        wu_s[...] = wu_f32[slot].astype(BF16)
        wd_s[...] = wd_f32[slot].astype(BF16)

        @pl.when(nxt_ref[i] >= 0)
        def _():
            for c in weight_copies(nxt_ref[i], 1 - slot):
                c.start()

    def tile_copy(j):
        slot = lax.rem(j, EXPERT_RING)
        row0 = pl.multiple_of(tb_ref[j] * MOE_TILE, MOE_TILE)
        return pltpu.make_async_copy(xs_ref.at[pl.ds(row0, MOE_TILE), :], xbuf.at[slot], sem.at[slot])

    @pl.when(i == 0)
    def _():
        for j in range(EXPERT_RING - 1):
            @pl.when(j < n_tiles)
            def _():
                tile_copy(j).start()

    ahead = i + (EXPERT_RING - 1)

    @pl.when(ahead < n_tiles)
    def _():
        tile_copy(ahead).start()

    @pl.when(i < n_tiles)
    def _():
        tile_copy(i).wait()
        x = _unpack_words(xbuf[lax.rem(i, EXPERT_RING)]).astype(BF16)
        act = _silu(_dot(x, wg_s[...])) * _dot(x, wu_s[...])
        ys_ref[...] = _pack_words(_dot(act.astype(BF16), wd_s[...]))


def _experts(xs, tile_expert, tile_block, n_tiles, tile_segment, next_expert, wg, wu, wd):
    d, f = wg.shape[1], wg.shape[2]
    anywhere = pl.BlockSpec(memory_space=pl.ANY)
    grid_spec = pltpu.PrefetchScalarGridSpec(
        num_scalar_prefetch=5,
        grid=(xs.shape[0] // MOE_TILE,),
        in_specs=[anywhere, anywhere, anywhere, anywhere],
        out_specs=pl.BlockSpec((MOE_TILE, d // 2), lambda i, te, tb, nt, seg, nxt: (tb[i], 0)),
        scratch_shapes=[pltpu.VMEM((EXPERT_RING, MOE_TILE, d // 2), U32), pltpu.SemaphoreType.DMA((EXPERT_RING,)),
                        pltpu.VMEM((2, d, f), F32), pltpu.VMEM((2, d, f), F32), pltpu.VMEM((2, f, d), F32),
                        pltpu.VMEM((d, f), BF16), pltpu.VMEM((d, f), BF16), pltpu.VMEM((f, d), BF16),
                        pltpu.SemaphoreType.DMA((2, 3))],
    )
    return pl.pallas_call(
        _experts_kernel,
        grid_spec=grid_spec,
        out_shape=jax.ShapeDtypeStruct(xs.shape, U32),
        compiler_params=_cparams(("arbitrary",)),
        name="experts",
    )(tile_expert, tile_block, n_tiles, tile_segment, next_expert, xs, wg, wu, wd)


def _gather_rows(table, idx):
    m = idx.shape[0]
    w = table.shape[1]
    win = SC_WINDOW // 2
    n_cores, n_workers = _sc_workers()
    per_worker = m // n_workers
    n_pairs = per_worker // (2 * win)
    assert per_worker * n_workers == m and n_pairs * 2 * win == per_worker
    mesh = plsc.VectorSubcoreMesh(core_axis_name="core", subcore_axis_name="subcore")

    @functools.partial(
        pl.kernel, mesh=mesh, out_type=jax.ShapeDtypeStruct((m, w), table.dtype),
        scratch_types=[pltpu.VMEM((win,), jnp.int32), pltpu.VMEM((win,), jnp.int32),
                       pltpu.VMEM((win, w), table.dtype), pltpu.VMEM((win, w), table.dtype),
                       pltpu.SemaphoreType.DMA, pltpu.SemaphoreType.DMA])
    def gather(table_hbm, idx_hbm, out_hbm, idx_a, idx_b, rows_a, rows_b, sem_a, sem_b):
        worker = lax.axis_index("subcore") * n_cores + lax.axis_index("core")
        start = worker * per_worker

        def request(first, idx_v, rows_v, sem):
            pltpu.sync_copy(idx_hbm.at[pl.ds(first, win)], idx_v)
            pltpu.async_copy(table_hbm.at[idx_v], rows_v, sem)

        def deliver(first, idx_v, rows_v, sem):
            pltpu.make_async_copy(table_hbm.at[idx_v], rows_v, sem).wait()
            pltpu.sync_copy(rows_v, out_hbm.at[pl.ds(first, win)])

        request(pl.multiple_of(start, win), idx_a, rows_a, sem_a)

        @pl.loop(0, n_pairs)
        def _(pair):
            first_a = pl.multiple_of(start + pair * 2 * win, win)
            first_b = pl.multiple_of(first_a + win, win)
            request(first_b, idx_b, rows_b, sem_b)
            deliver(first_a, idx_a, rows_a, sem_a)

            @pl.when(pair + 1 < n_pairs)
            def _():
                request(pl.multiple_of(first_b + win, win), idx_a, rows_a, sem_a)

            deliver(first_b, idx_b, rows_b, sem_b)

    return gather(table, idx)


def _combine_kernel(rows_ref, topw_ref, h_ref, x1_ref, g2_ref, sg_ref, su_ref, sd_ref, ln2g_ref, ln2b_ref,
                    o_ref, *, alpha):
    h = _unpack_words(h_ref[...]).astype(BF16)
    act = _silu(_dot(h, sg_ref[...])) * _dot(h, su_ref[...])
    y = _dot(act.astype(BF16), sd_ref[...])
    w = topw_ref[...].T
    for k in range(TOP_K):
        y = y + w[:, k:k + 1] * _unpack_words(rows_ref[k])
    o_ref[...] = _normalize(alpha * x1_ref[...] + g2_ref[...] * y) * ln2g_ref[...] + ln2b_ref[...]


def _combine(gathered, topw, h2p, x1, g2, sg, su, sd, ln2_g, ln2_b, alpha):
    t, d = x1.shape
    b, k, s = topw.shape
    per_b = s // MOE_TOK
    fs = sg.shape[1]
    rows = pl.BlockSpec((MOE_TOK, d), lambda i: (i, 0))
    packed = pl.BlockSpec((MOE_TOK, d // 2), lambda i: (i, 0))
    vec = pl.BlockSpec((1, d), lambda i: (0, 0))
    return pl.pallas_call(
        functools.partial(_combine_kernel, alpha=alpha),
        grid=(t // MOE_TOK,),
        in_specs=[pl.BlockSpec((k, MOE_TOK, d // 2), lambda i: (0, i, 0)),
                  pl.BlockSpec((None, k, MOE_TOK), lambda i: (i // per_b, 0, i % per_b)),
                  packed, rows,
                  pl.BlockSpec((None, 1, d), lambda i: (i // per_b, 0, 0)),
                  pl.BlockSpec((d, fs), lambda i: (0, 0)),
                  pl.BlockSpec((d, fs), lambda i: (0, 0)),
                  pl.BlockSpec((fs, d), lambda i: (0, 0)),
                  vec, vec],
        out_specs=rows,
        out_shape=jax.ShapeDtypeStruct((t, d), F32),
        compiler_params=_cparams(("arbitrary",)),
        name="combine",
    )(gathered, topw, h2p, x1, g2, sg, su, sd, ln2_g.reshape(1, d), ln2_b.reshape(1, d))


def _moe(h2p, topi, topw, cnt, x1, g2, wg, wu, wd, sg, su, sd, ln2_g, ln2_b, alpha):
    b, s, d = x1.shape
    t = b * s
    cnt = cnt[:, 0].astype(jnp.int32)
    tiles_e = (cnt + (MOE_TILE - 1)) // MOE_TILE
    tiles_cum = jnp.cumsum(tiles_e)
    seg_off = (tiles_cum - tiles_e) * MOE_TILE
    n_tiles_max = t * TOP_K // MOE_TILE + N_EXPERTS
    tile_block = jnp.minimum(jnp.arange(n_tiles_max, dtype=jnp.int32), tiles_cum[-1] - 1)
    tile_expert = jnp.sum((tiles_cum[None, :] <= tile_block[:, None]).astype(jnp.int32), axis=1)
    n_tiles = tiles_cum[-1:].astype(jnp.int32)
    present = tiles_e > 0
    seg_of_expert = jnp.cumsum(present.astype(jnp.int32)) - 1
    later = jnp.where(present, jnp.arange(N_EXPERTS, dtype=jnp.int32), N_EXPERTS)
    next_present = jnp.concatenate([lax.cummin(later, reverse=True)[1:], jnp.full((1,), N_EXPERTS, jnp.int32)])
    next_present = jnp.where(next_present < N_EXPERTS, next_present, -1)
    tile_segment = seg_of_expert[tile_expert].astype(jnp.int32)
    next_expert = next_present[tile_expert].astype(jnp.int32)

    dest = jnp.transpose(_plan(topi, seg_off), (1, 0, 2)).reshape(TOP_K * t)
    j = jnp.arange(MOE_TILE, dtype=jnp.int32)[None, :]
    n_pad = (tiles_e * MOE_TILE - cnt)[:, None]
    spare = (n_tiles_max - 1) * MOE_TILE + j
    zero_idx = jnp.where(j < n_pad, (seg_off + cnt)[:, None] + j, spare).reshape(N_EXPERTS * MOE_TILE)
    xs = _scatter_rows(h2p, dest, zero_idx.astype(jnp.int32), n_tiles_max * MOE_TILE)
    ys = _experts(xs, tile_expert, tile_block, n_tiles, tile_segment, next_expert, wg, wu, wd)
    gathered = _gather_rows(ys, dest)
    out = _combine(gathered.reshape(TOP_K, t, d // 2), topw, h2p, x1.reshape(t, d), g2, sg, su, sd,
                   ln2_g, ln2_b, alpha)
    return out.reshape(b, s, d)


def kernel(x, c, ctx, c_ctx, w_ada, b_ada, w_in, hg_lb_fwd, hg_lb_bwd, hg_norm_g, na_rpb, w_branch_a, w_branch_b, w_out, ln1_g, ln1_b, w_router, router_bias, w_e_gate, w_e_up, w_e_down, w_sh_gate, w_sh_up, w_sh_down, ln2_g, ln2_b):
    depth = w_ada.shape[0]
    assert depth == 1, "single-layer block"
    b, s, d = x.shape
    alpha = (2.0 * depth) ** 0.25
    l = 0
    lb_fwd = jnp.cumsum(jax.nn.softmax(hg_lb_fwd.astype(F32), axis=0), axis=0)[l]
    lb_bwd = jnp.cumsum(jax.nn.softmax(hg_lb_bwd.astype(F32), axis=0), axis=0)[l]

    cond_rows = jnp.concatenate([c, c_ctx[None, :], jnp.zeros((8 - b - 1, d), F32)], axis=0)
    mod = _ada(cond_rows, w_ada[l], b_ada[l])
    sh1, sc1, g1, sh2, sc2, g2 = [m[:b, None, :] for m in jnp.split(mod, 6, axis=-1)]
    csh1, csc1 = [jnp.broadcast_to(m[b:b + 1, None, :], (b, 1, d)) for m in jnp.split(mod, 6, axis=-1)[:2]]

    w_in_b = w_in[l].astype(BF16)
    p = _inproj(x, sh1, sc1, w_in_b, tuple(range(N_SECTIONS)))
    pc = _inproj(ctx, csh1, csc1, w_in_b, CTX_SECTIONS)

    o_f, o_b = _hgrn(p, pc, lb_fwd, lb_bwd)
    y_na = _natten(p, pc, *_na_tables(na_rpb[l], s))

    x1, h2, topi, topw, cnt = _merge(o_f, o_b, p, y_na, x, g1, sh2, sc2, hg_norm_g[l], ln1_g[l], ln1_b[l],
                                     w_branch_a[l].astype(BF16), w_branch_b[l].astype(BF16),
                                     w_out[l].astype(BF16), w_router[l].T, router_bias[l], alpha)

    return _moe(h2, topi, topw, cnt, x1, g2,
                w_e_gate[l], w_e_up[l], w_e_down[l],
                w_sh_gate[l].astype(BF16), w_sh_up[l].astype(BF16), w_sh_down[l].astype(BF16),
                ln2_g[l], ln2_b[l], alpha)
```

```python
import functools

import numpy as np
import jax
import jax.numpy as jnp
from jax import lax
from jax.experimental import pallas as pl
from jax.experimental.pallas import tpu as pltpu
from jax.experimental.pallas import tpu_sc as plsc

F32 = jnp.float32
BF16 = jnp.bfloat16

D_MODEL = 1024
GRID_W = 64
HG_HEADS = 8
HG_DK = 128
HG_CHUNK = 64
NA_HEADS = 16
NA_HD = 64
NA_WIN_R = 8
NA_WIN_C = 16
ROPE_THETA = 10000.0
NEG_INF = -1e30
N_EXPERTS = 64
EXPERT_DIM = 256
TOP_K = 8
N_GROUPS = 8
TOPK_GROUPS = 4
ROUTED_SCALE = 2.5
LN_EPS = 1e-6
N_SECTIONS = 10
SEC_Q, SEC_FF, SEC_FB, SEC_I, SEC_OG, SEC_NQ, SEC_NK, SEC_NV, SEC_GA, SEC_GB = range(10)
CTX_SECTIONS = (SEC_FF, SEC_FB, SEC_I, SEC_NK, SEC_NV)

VMEM_LIMIT = 56 * 1024 * 1024


def _cparams(sem):
    return pltpu.CompilerParams(dimension_semantics=sem, vmem_limit_bytes=VMEM_LIMIT)


def _normalize(x):
    mu = jnp.mean(x, axis=-1, keepdims=True)
    xc = x - mu
    var = jnp.mean(xc * xc, axis=-1, keepdims=True)
    return xc * lax.rsqrt(var + LN_EPS)


def _silu(x):
    return x * jax.nn.sigmoid(x)


def _dot(a, b):
    return jnp.dot(a, b, preferred_element_type=F32)


def _dot_nt(a, b):
    return lax.dot_general(a, b, (((1,), (1,)), ((), ())), preferred_element_type=F32)


def _dot_tn(a, b):
    return lax.dot_general(a, b, (((0,), (0,)), ((), ())), preferred_element_type=F32)


U32 = jnp.uint32


def _pack_words(x):
    half = x.shape[1] // 2
    lo = lax.bitcast_convert_type(x[:, :half].astype(BF16).astype(F32), U32) >> 16
    hi = lax.bitcast_convert_type(x[:, half:].astype(BF16).astype(F32), U32) & jnp.uint32(0xFFFF0000)
    return lo | hi


def _unpack_words(w):
    lo = lax.bitcast_convert_type(w << 16, F32)
    hi = lax.bitcast_convert_type(w & jnp.uint32(0xFFFF0000), F32)
    return jnp.concatenate([lo, hi], axis=-1)


def _split3(x):
    hi = x.astype(BF16)
    r1 = x - hi.astype(F32)
    mid = r1.astype(BF16)
    lo = (r1 - mid.astype(F32)).astype(BF16)
    return hi, mid, lo


def _ada_kernel(c_ref, w_ref, b_ref, o_ref):
    cond = _silu(c_ref[...])
    o_ref[...] = _dot(cond.astype(BF16), w_ref[...].astype(BF16)) + b_ref[...]


def _ada(cond_rows, w_ada, b_ada):
    r, d = cond_rows.shape
    n = w_ada.shape[1]
    tn = 1024
    return pl.pallas_call(
        _ada_kernel,
        grid=(n // tn,),
        in_specs=[pl.BlockSpec((r, d), lambda j: (0, 0)),
                  pl.BlockSpec((d, tn), lambda j: (0, j)),
                  pl.BlockSpec((1, tn), lambda j: (0, j))],
        out_specs=pl.BlockSpec((r, tn), lambda j: (0, j)),
        out_shape=jax.ShapeDtypeStruct((r, n), F32),
        compiler_params=_cparams(("arbitrary",)),
        name="ada",
    )(cond_rows, w_ada, b_ada.reshape(1, n))


INPROJ_TOK = 2048


def _inproj_kernel(x_ref, sh_ref, sc_ref, w_ref, o_ref, h_ref):
    @pl.when(pl.program_id(2) == 0)
    def _():
        h = _normalize(x_ref[...]) * (1.0 + sc_ref[...]) + sh_ref[...]
        h_ref[...] = h.astype(BF16)

    o_ref[...] = _dot(h_ref[...], w_ref[...])


def _inproj(x, shift, scale, w_in_bf16, sections):
    b, s, d = x.shape
    tm = min(INPROJ_TOK, s)
    nj = len(sections)

    def section(j):
        sec = sections[-1]
        for k in range(nj - 2, -1, -1):
            sec = jnp.where(j == k, sections[k], sec)
        return sec

    return pl.pallas_call(
        _inproj_kernel,
        grid=(b, s // tm, nj),
        in_specs=[pl.BlockSpec((None, tm, d), lambda bi, i, j: (bi, i, 0)),
                  pl.BlockSpec((None, 1, d), lambda bi, i, j: (bi, 0, 0)),
                  pl.BlockSpec((None, 1, d), lambda bi, i, j: (bi, 0, 0)),
                  pl.BlockSpec((d, d), lambda bi, i, j: (0, section(j)))],
        out_specs=pl.BlockSpec((None, None, tm, d), lambda bi, i, j: (j, bi, i, 0)),
        out_shape=jax.ShapeDtypeStruct((nj, b, s, d), F32),
        scratch_shapes=[pltpu.VMEM((tm, d), BF16)],
        compiler_params=_cparams(("arbitrary", "arbitrary", "arbitrary")),
        name="inproj",
    )(x, shift, scale, w_in_bf16)


def _hgrn_gates(q, fraw, v, lb, tri_bf16, last_row):
    f = lb + (1.0 - lb) * jax.nn.sigmoid(fraw)
    k = 1.0 - f
    lf = jnp.log(f)
    hi, mid, lo = _split3(lf)
    a = _dot(tri_bf16, hi) + _dot(tri_bf16, mid) + _dot(tri_bf16, lo)
    a_last = a[last_row:last_row + 1, :]
    kd = (k * jnp.exp(a_last - a)).astype(BF16)
    decay = jnp.exp(a_last)
    qa = kb = None
    if q is not None:
        qa = (_silu(q) * jnp.exp(a)).astype(BF16)
        kb = (k * jnp.exp(-a)).astype(BF16)
    return qa, kb, kd, v.astype(BF16), decay


def _hgrn_chunks(chunks, st_ref):
    first = []
    for d, ((qa, kb, kd, vb, decay), keep) in enumerate(chunks):
        for h in range(HG_HEADS):
            sl = slice(h * HG_DK, (h + 1) * HG_DK)
            st = st_ref[d, h]
            if qa is not None:
                first.append((_dot_nt(qa[:, sl], kb[:, sl]), _dot_nt(qa[:, sl], st.astype(BF16))))
            st_ref[d, h] = st * decay[:, sl] + _dot_tn(vb[:, sl], kd[:, sl])
    results = []
    for d, ((qa, kb, kd, vb, decay), keep) in enumerate(chunks):
        if qa is None:
            results.append(None)
            continue
        outs = []
        for h in range(HG_HEADS):
            sl = slice(h * HG_DK, (h + 1) * HG_DK)
            s_qk, o_state = first.pop(0)
            outs.append(_dot(jnp.where(keep, s_qk, 0.0).astype(BF16), vb[:, sl]) + o_state)
        results.append(jnp.concatenate(outs, axis=-1))
    return results


def _hgrn_kernel(qf_ref, ff_ref, if_ref, qb_ref, fb_ref, ib_ref, cff_ref, cfb_ref, ci_ref,
                 lbf_ref, lbb_ref, of_ref, ob_ref, st_ref, *, n_sub, n_ctx_sub):
    n = pl.program_id(1)
    c = HG_CHUNK
    row = lax.broadcasted_iota(jnp.int32, (c, c), 0)
    col = lax.broadcasted_iota(jnp.int32, (c, c), 1)
    keep_f = col <= row
    keep_b = col >= row
    tri_f = keep_f.astype(F32).astype(BF16)
    tri_b = keep_b.astype(F32).astype(BF16)
    lbf = lbf_ref[...]
    lbb = lbb_ref[...]

    @pl.when(n == 0)
    def _():
        st_ref[...] = jnp.zeros_like(st_ref)

        def body(i, carry):
            r0 = pl.multiple_of(i * c, c)
            r1 = pl.multiple_of((n_ctx_sub - 1 - i) * c, c)
            gf = _hgrn_gates(None, cff_ref[pl.ds(r0, c), :], ci_ref[pl.ds(r0, c), :], lbf, tri_f, c - 1)
            gb = _hgrn_gates(None, cfb_ref[pl.ds(r1, c), :], ci_ref[pl.ds(r1, c), :], lbb, tri_b, 0)
            _hgrn_chunks([(gf, keep_f), (gb, keep_b)], st_ref)
            return carry

        lax.fori_loop(0, n_ctx_sub, body, 0)

    @pl.when(n > 0)
    def _():
        def body(i, carry):
            r0 = pl.multiple_of(i * c, c)
            r1 = pl.multiple_of((n_sub - 1 - i) * c, c)
            gf = _hgrn_gates(qf_ref[pl.ds(r0, c), :], ff_ref[pl.ds(r0, c), :], if_ref[pl.ds(r0, c), :],
                             lbf, tri_f, c - 1)
            gb = _hgrn_gates(qb_ref[pl.ds(r1, c), :], fb_ref[pl.ds(r1, c), :], ib_ref[pl.ds(r1, c), :],
                             lbb, tri_b, 0)
            o_f, o_b = _hgrn_chunks([(gf, keep_f), (gb, keep_b)], st_ref)
            of_ref[pl.ds(r0, c), :] = o_f
            ob_ref[pl.ds(r1, c), :] = o_b
            return carry

        lax.fori_loop(0, n_sub, body, 0, unroll=True)


def _hgrn(p, pc, lb_fwd, lb_bwd):
    _, b, s, w = p.shape
    ctx_len = pc.shape[2]
    tb = min(256, s)
    nb = s // tb
    fwd = lambda bi, n: jnp.maximum(n - 1, 0)
    bwd = lambda bi, n: nb - 1 - jnp.maximum(n - 1, 0)

    def sec(section, blk):
        return pl.BlockSpec((None, None, tb, w), lambda bi, n: (section, bi, blk(bi, n), 0))

    def csec(section):
        return pl.BlockSpec((None, None, ctx_len, w), lambda bi, n: (CTX_SECTIONS.index(section), bi, 0, 0))

    vec = pl.BlockSpec((1, w), lambda bi, n: (0, 0))
    kern = functools.partial(_hgrn_kernel, n_sub=tb // HG_CHUNK, n_ctx_sub=ctx_len // HG_CHUNK)
    return pl.pallas_call(
        kern,
        grid=(b, nb + 1),
        in_specs=[sec(SEC_Q, fwd), sec(SEC_FF, fwd), sec(SEC_I, fwd),
                  sec(SEC_Q, bwd), sec(SEC_FB, bwd), sec(SEC_I, bwd),
                  csec(SEC_FF), csec(SEC_FB), csec(SEC_I), vec, vec],
        out_specs=[pl.BlockSpec((None, tb, w), lambda bi, n: (bi, fwd(bi, n), 0)),
                   pl.BlockSpec((None, tb, w), lambda bi, n: (bi, bwd(bi, n), 0))],
        out_shape=[jax.ShapeDtypeStruct((b, s, w), F32), jax.ShapeDtypeStruct((b, s, w), F32)],
        scratch_shapes=[pltpu.VMEM((2, HG_HEADS, HG_DK, HG_DK), F32)],
        compiler_params=_cparams(("arbitrary", "arbitrary")),
        name="hgrn",
    )(p, p, p, p, p, p, pc, pc, pc, lb_fwd.reshape(1, w), lb_bwd.reshape(1, w))


NA_ROWS_PER_STEP = 32
NA_PREP_ROWS = 512
NA_KEY_TILE = 128
NA_SPAN = NA_WIN_R * GRID_W


def _rope(t, cos, sin_signed, first_half):
    w = t.shape[-1]
    partner = jnp.where(first_half, pltpu.roll(t, w - 16, 1), pltpu.roll(t, 16, 1))
    return t * cos + partner * sin_signed


def _fold_lanes(op, *arrays):
    tiles = [a[:, c:c + 128] for a in arrays for c in range(0, a.shape[-1], 128)]
    acc = tiles[0]
    for t in tiles[1:]:
        acc = op(acc, t)
    return acc


def _rope_tables(rowtab_ref, coltab_ref, row0, n_rows, row_lane):
    out = []
    for i in range(2):
        rt = rowtab_ref[i, pl.ds(row0, n_rows), :]
        by_row = jnp.concatenate([jnp.broadcast_to(rt[r:r + 1, :], (GRID_W, rt.shape[1])) for r in range(n_rows)],
                                 axis=0)
        by_col = jnp.concatenate([coltab_ref[i]] * n_rows, axis=0)
        out.append(jnp.where(row_lane, by_row, by_col))
    return out


def _natten_kernel(q_ref, k_ref, v_ref, kc_ref, vc_ref, rowtab_ref, coltab_ref, t2_ref, o_ref,
                   kt_s, v_s, kc_s, vc_s, bias_s, tail_s, *, rows):
    rblk = pl.program_id(2)
    hd = NA_HD
    lane = lax.broadcasted_iota(jnp.int32, (1, 2 * hd), 1)
    first_half = (lane % 32) < 16
    row_lane = (lane % hd) < hd // 2
    scale = NA_HD ** -0.5

    def values_and_ones(v_pair, h):
        vh = v_pair if h == 0 else pltpu.roll(v_pair, hd, 1)
        return jnp.where(lane < hd, vh, jnp.where(lane == hd, 1.0, 0.0)).astype(BF16)

    @pl.when(rblk == 0)
    def _():
        kc = kc_ref[...].astype(BF16)
        qi = lax.broadcasted_iota(jnp.int32, (GRID_W, GRID_W), 0)
        ki = lax.broadcasted_iota(jnp.int32, (GRID_W, GRID_W), 1)
        cstart = jnp.clip(qi - NA_WIN_C // 2, 0, GRID_W - NA_WIN_C)
        in_win = (ki >= cstart) & (ki < cstart + NA_WIN_C)
        s_len = k_ref.shape[0]
        tail_s[...] = jnp.zeros_like(tail_s)
        for h in range(2):
            sl = slice(h * hd, (h + 1) * hd)
            kc_s[h] = kc[:, sl]
            vc_s[h] = values_and_ones(vc_ref[...], h)
            tiles = [jnp.where(in_win, t2_ref[h, dr], NEG_INF) for dr in range(2 * NA_WIN_R - 1)]
            for v in range(NA_WIN_R):
                for j in range(NA_WIN_R):
                    bias_s[h, v, :, j * GRID_W:(j + 1) * GRID_W] = tiles[NA_WIN_R - 1 - v + j]

        eye = (lax.broadcasted_iota(jnp.int32, (2 * hd, 2 * hd), 0)
               == lax.broadcasted_iota(jnp.int32, (2 * hd, 2 * hd), 1)).astype(F32).astype(BF16)

        def prep(i, carry):
            r0 = pl.multiple_of(i * NA_PREP_ROWS, NA_PREP_ROWS)
            rws = pl.ds(r0, NA_PREP_ROWS)
            cos, sin = _rope_tables(rowtab_ref, coltab_ref, i * (NA_PREP_ROWS // GRID_W), NA_PREP_ROWS // GRID_W,
                                    row_lane)
            kr = _rope(k_ref[rws, :], cos, sin, first_half)
            kr_odd = jnp.concatenate([tail_s[...], kr[:NA_PREP_ROWS - GRID_W]], axis=0)
            tail_s[...] = kr[NA_PREP_ROWS - GRID_W:]
            krt = [_dot_nt(eye, kr.astype(BF16)).astype(BF16),
                   _dot_nt(eye, kr_odd.astype(BF16)).astype(BF16)]
            vv = v_ref[rws, :]
            for h in range(2):
                sl = slice(h * hd, (h + 1) * hd)
                for par in range(2):
                    for c in range(NA_PREP_ROWS // NA_KEY_TILE):
                        kt_s[h, par, i * (NA_PREP_ROWS // NA_KEY_TILE) + c] = (
                            krt[par][sl, c * NA_KEY_TILE:(c + 1) * NA_KEY_TILE])
                v_s[h, rws, :] = values_and_ones(vv, h)
            return carry

        lax.fori_loop(0, s_len // NA_PREP_ROWS, prep, 0, unroll=4)

    tq = NA_ROWS_PER_STEP * GRID_W
    q = q_ref[...] * scale
    cos, sin = _rope_tables(rowtab_ref, coltab_ref, rblk * NA_ROWS_PER_STEP, NA_ROWS_PER_STEP, row_lane)
    qr = _rope(q, cos, sin, first_half)
    qb = q.astype(BF16)
    qrb = qr.astype(BF16)
    rws = [slice(rr * GRID_W, (rr + 1) * GRID_W) for rr in range(NA_ROWS_PER_STEP)]
    par, slot0, key0, bidx = [], [], [], []
    for rr in range(NA_ROWS_PER_STEP):
        r = rblk * NA_ROWS_PER_STEP + rr
        rs = jnp.clip(r - NA_WIN_R // 2, 0, rows - NA_WIN_R)
        par.append(rs & 1)
        slot0.append(lax.shift_right_logical(rs, 1) + (rs & 1))
        key0.append(pl.multiple_of(rs * GRID_W, GRID_W))
        bidx.append(r - rs)

    def scores(h):
        sl = slice(h * hd, (h + 1) * hd)
        qrb_h = qrb[:, sl]
        s_ctx_all = _dot_nt(qb[:, sl], kc_s[h])
        s_win = []
        for rr in range(NA_ROWS_PER_STEP):
            kt = kt_s[h, par[rr], pl.ds(slot0[rr], NA_SPAN // NA_KEY_TILE)]
            kt = jnp.concatenate([kt[c] for c in range(NA_SPAN // NA_KEY_TILE)], axis=-1)
            s_win.append(_dot(qrb_h[rws[rr]], kt))
        return s_win, s_ctx_all

    def softmax(h, s_win, s_ctx_all):
        e_win, e_ctx = [], []
        for rr in range(NA_ROWS_PER_STEP):
            sw = s_win[rr] + bias_s[h, bidx[rr]]
            sc = s_ctx_all[rws[rr]]
            m = jnp.max(_fold_lanes(jnp.maximum, sw, sc), axis=-1, keepdims=True)
            e_win.append(jnp.exp(sw - m).astype(BF16))
            e_ctx.append(jnp.exp(sc - m).astype(BF16))
        return e_win, e_ctx

    def values(h, e_win, e_ctx):
        o_win = []
        for rr in range(NA_ROWS_PER_STEP):
            o_win.append(_dot(e_win[rr], v_s[h, pl.ds(key0[rr], NA_SPAN), :]))
        o = jnp.concatenate(o_win, axis=0) + _dot(jnp.concatenate(e_ctx, axis=0), vc_s[h])
        return o[:, :hd] * (1.0 / o[:, hd:hd + 1])

    s0 = scores(0)
    s1 = scores(1)
    p0 = softmax(0, *s0)
    o0 = values(0, *p0)
    p1 = softmax(1, *s1)
    o1 = values(1, *p1)
    o_ref[...] = jnp.concatenate([o0, o1], axis=-1)


def _na_tables(rpb, s):
    half = NA_HD // 2
    inv = jnp.power(ROPE_THETA, -jnp.arange(0, half, 2, dtype=F32) / half)

    def tables(n):
        ang = jnp.arange(n, dtype=F32)[:, None] * inv[None, :]
        reps = 2 * NA_HD // half
        return jnp.stack([jnp.tile(jnp.cos(ang), (1, 2 * reps)),
                          jnp.tile(jnp.concatenate([-jnp.sin(ang), jnp.sin(ang)], axis=-1), (1, reps))])

    rowtab, coltab = tables(s // GRID_W), tables(GRID_W)

    pad = GRID_W - NA_WIN_C
    width = 2 * GRID_W
    rp = jnp.pad(rpb.astype(F32), ((0, 0), (0, 0), (pad, pad + 2)), mode="edge")
    skew = jnp.tile(rp, (1, 1, GRID_W + 1))[:, :, GRID_W - 1:GRID_W - 1 + GRID_W * width]
    t2 = skew.reshape(rp.shape[0], rp.shape[1], GRID_W, width)[:, :, :, :GRID_W]
    return rowtab, coltab, t2


def _natten(p, pc, rowtab, coltab, t2):
    _, b, s, w = p.shape
    ctx_len = pc.shape[2]
    rows = s // GRID_W
    assert rows >= NA_WIN_R and rows % NA_ROWS_PER_STEP == 0
    tq = NA_ROWS_PER_STEP * GRID_W
    hw = 2 * NA_HD
    nhp = w // hw
    kern = functools.partial(_natten_kernel, rows=rows)
    return pl.pallas_call(
        kern,
        grid=(b, nhp, rows // NA_ROWS_PER_STEP),
        in_specs=[pl.BlockSpec((None, None, tq, hw), lambda bi, hp, r: (SEC_NQ, bi, r, hp)),
                  pl.BlockSpec((None, None, s, hw), lambda bi, hp, r: (SEC_NK, bi, 0, hp)),
                  pl.BlockSpec((None, None, s, hw), lambda bi, hp, r: (SEC_NV, bi, 0, hp)),
                  pl.BlockSpec((None, None, ctx_len, hw), lambda bi, hp, r: (CTX_SECTIONS.index(SEC_NK), bi, 0, hp)),
                  pl.BlockSpec((None, None, ctx_len, hw), lambda bi, hp, r: (CTX_SECTIONS.index(SEC_NV), bi, 0, hp)),
                  pl.BlockSpec((2, rows, hw), lambda bi, hp, r: (0, 0, 0)),
                  pl.BlockSpec((2, GRID_W, hw), lambda bi, hp, r: (0, 0, 0)),
                  pl.BlockSpec((2, 2 * NA_WIN_R - 1, GRID_W, GRID_W), lambda bi, hp, r: (hp, 0, 0, 0))],
        out_specs=pl.BlockSpec((None, tq, hw), lambda bi, hp, r: (bi, r, hp)),
        out_shape=jax.ShapeDtypeStruct((b, s, w), F32),
        scratch_shapes=[pltpu.VMEM((2, 2, s // NA_KEY_TILE, NA_HD, NA_KEY_TILE), BF16),
                        pltpu.VMEM((2, s, hw), BF16),
                        pltpu.VMEM((2, ctx_len, NA_HD), BF16), pltpu.VMEM((2, ctx_len, hw), BF16),
                        pltpu.VMEM((2, NA_WIN_R, GRID_W, NA_SPAN), F32),
                        pltpu.VMEM((GRID_W, hw), F32)],
        compiler_params=_cparams(("arbitrary", "arbitrary", "arbitrary")),
        name="natten",
    )(p, p, p, pc, pc, rowtab, coltab, t2)


def _route(logits_t, rbias):
    e, t = logits_t.shape
    gsz = e // N_GROUPS
    scores = jax.nn.sigmoid(logits_t)
    sel = scores + rbias
    neg = -jnp.inf
    sub = lax.broadcasted_iota(jnp.int32, (gsz, t), 0).astype(F32)
    gscore = []
    for g in range(N_GROUPS):
        grp = sel[g * gsz:(g + 1) * gsz, :]
        m1 = jnp.max(grp, axis=0, keepdims=True)
        first = jnp.min(jnp.where(grp == m1, sub, float(gsz)), axis=0, keepdims=True)
        m2 = jnp.max(jnp.where(sub == first, neg, grp), axis=0, keepdims=True)
        gscore.append(m1 + m2)
    masked = []
    for g in range(N_GROUPS):
        rank = jnp.zeros((1, t), F32)
        for g2 in range(N_GROUPS):
            if g2 == g:
                continue
            if g2 < g:
                ahead = gscore[g2] >= gscore[g]
            else:
                ahead = gscore[g2] > gscore[g]
            rank = rank + jnp.where(ahead, 1.0, 0.0)
        masked.append(jnp.where(rank < TOPK_GROUPS, sel[g * gsz:(g + 1) * gsz, :], neg))
    work = jnp.concatenate(masked, axis=0)
    eidx = lax.broadcasted_iota(jnp.int32, (e, t), 0).astype(F32)
    idxs, ws = [], []
    chosen = jnp.zeros((e, t), F32)
    for _ in range(TOP_K):
        m = jnp.max(work, axis=0, keepdims=True)
        first = jnp.min(jnp.where(work == m, eidx, float(e)), axis=0, keepdims=True)
        pick = eidx == first
        idxs.append(first)
        ws.append(jnp.sum(jnp.where(pick, scores, 0.0), axis=0, keepdims=True))
        chosen = jnp.where(pick, 1.0, chosen)
        work = jnp.where(pick, neg, work)
    w = jnp.concatenate(ws, axis=0)
    w = w / jnp.sum(w, axis=0, keepdims=True) * ROUTED_SCALE
    return jnp.concatenate(idxs, axis=0).astype(jnp.int32), w, chosen


MERGE_TOK = 512
MERGE_SUB = 256


def _merge_kernel(of_ref, ob_ref, og_ref, yna_ref, ga_ref, gb_ref, x_ref, g1_ref, sh2_ref, sc2_ref,
                  hgg_ref, ln1g_ref, ln1b_ref, wa_ref, wb_ref, wo_ref, wr_ref, rb_ref,
                  x1_ref, h2_ref, topi_ref, topw_ref, cnt_ref, *, alpha):
    tm = x_ref.shape[0]
    subs = [slice(i * MERGE_SUB, (i + 1) * MERGE_SUB) for i in range(tm // MERGE_SUB)]

    def branches(rows):
        o = of_ref[rows, :] + ob_ref[rows, :]
        parts = []
        for h in range(HG_HEADS):
            oh = o[:, h * HG_DK:(h + 1) * HG_DK]
            parts.append(oh * lax.rsqrt(jnp.mean(oh * oh, axis=-1, keepdims=True) + LN_EPS))
        y_hg = jnp.concatenate(parts, axis=-1) * hgg_ref[...] * _silu(og_ref[rows, :])
        return _dot(y_hg.astype(BF16), wa_ref[...]), _dot(yna_ref[rows, :].astype(BF16), wb_ref[...])

    def out_proj(rows, ya, yb):
        t = jax.nn.sigmoid(ga_ref[rows, :]) * ya + jax.nn.sigmoid(gb_ref[rows, :]) * yb
        return _dot(t.astype(BF16), wo_ref[...])

    def norms_router(rows, i, y):
        x1 = _normalize(alpha * x_ref[rows, :] + g1_ref[...] * y) * ln1g_ref[...] + ln1b_ref[...]
        x1_ref[rows, :] = x1
        h2 = _normalize(x1) * (1.0 + sc2_ref[...]) + sh2_ref[...]
        h2_ref[rows, :] = _pack_words(h2)
        hh, hm, hl = _split3(h2)
        wh, wm, wl = _split3(wr_ref[...])
        return (_dot_nt(wh, hh) + _dot_nt(wh, hm) + _dot_nt(wm, hh)
                + _dot_nt(wh, hl) + _dot_nt(wl, hh) + _dot_nt(wm, hm))

    ab = [branches(rows) for rows in subs]
    ys = [out_proj(rows, *ab[i]) for i, rows in enumerate(subs)]
    logits = [norms_router(rows, i, ys[i]) for i, rows in enumerate(subs)]

    @pl.when((pl.program_id(0) == 0) & (pl.program_id(1) == 0))
    def _():
        cnt_ref[...] = jnp.zeros_like(cnt_ref)

    for i, rows in enumerate(subs):
        topi, topw, chosen = _route(logits[i], rb_ref[...])
        topi_ref[:, rows] = topi
        topw_ref[:, rows] = topw
        cnt_ref[...] += jnp.sum(chosen, axis=1, keepdims=True)


def _merge(o_f, o_b, p, y_na, x, g1, sh2, sc2, hg_norm_g, ln1_g, ln1_b, w_a, w_b, w_o, w_router_t, router_bias,
           alpha):
    b, s, d = x.shape
    tm = min(MERGE_TOK, s)
    e = w_router_t.shape[0]
    tok = lambda bi, i: (bi, i, 0)
    blk = pl.BlockSpec((None, tm, d), tok)

    def sec(section):
        return pl.BlockSpec((None, None, tm, d), lambda bi, i: (section, bi, i, 0))

    mod = pl.BlockSpec((None, 1, d), lambda bi, i: (bi, 0, 0))
    vec = pl.BlockSpec((1, d), lambda bi, i: (0, 0))
    mat = pl.BlockSpec((d, d), lambda bi, i: (0, 0), pipeline_mode=pl.Buffered(1))
    return pl.pallas_call(
        functools.partial(_merge_kernel, alpha=alpha),
        grid=(b, s // tm),
        in_specs=[blk, blk, sec(SEC_OG), blk, sec(SEC_GA), sec(SEC_GB), blk, mod, mod, mod,
                  vec, vec, vec, mat, mat, mat,
                  pl.BlockSpec((e, d), lambda bi, i: (0, 0)),
                  pl.BlockSpec((e, 1), lambda bi, i: (0, 0))],
        out_specs=[blk,
                   pl.BlockSpec((tm, d // 2), lambda bi, i: (bi * (s // tm) + i, 0)),
                   pl.BlockSpec((None, TOP_K, tm), lambda bi, i: (bi, 0, i)),
                   pl.BlockSpec((None, TOP_K, tm), lambda bi, i: (bi, 0, i)),
                   pl.BlockSpec((e, 128), lambda bi, i: (0, 0))],
        out_shape=[jax.ShapeDtypeStruct((b, s, d), F32),
                   jax.ShapeDtypeStruct((b * s, d // 2), U32),
                   jax.ShapeDtypeStruct((b, TOP_K, s), jnp.int32), jax.ShapeDtypeStruct((b, TOP_K, s), F32),
                   jax.ShapeDtypeStruct((e, 128), F32)],
        compiler_params=_cparams(("arbitrary", "arbitrary")),
        name="merge",
    )(o_f, o_b, p, y_na, p, p, x, g1, sh2, sc2, hg_norm_g.reshape(1, d), ln1_g.reshape(1, d),
      ln1_b.reshape(1, d), w_a, w_b, w_o, w_router_t, router_bias.reshape(e, 1))


MOE_TILE = 512
MOE_TOK = 512


def _plan_kernel(topi_ref, off_ref, dest_ref, carry_ref):
    @pl.when(pl.program_id(0) == 0)
    def _():
        carry_ref[...] = jnp.zeros_like(carry_ref)

    topi = topi_ref[...]
    tok = topi.shape[1]
    eidx = lax.broadcasted_iota(jnp.int32, (N_EXPERTS, tok), 0)
    hits = [eidx == topi[k:k + 1, :] for k in range(TOP_K)]
    m = jnp.zeros((N_EXPERTS, tok), F32)
    for hit in hits:
        m = jnp.where(hit, 1.0, m)
    before = (lax.broadcasted_iota(jnp.int32, (tok, tok), 0)
              < lax.broadcasted_iota(jnp.int32, (tok, tok), 1)).astype(F32).astype(BF16)
    row = off_ref[...] + carry_ref[...] + _dot(m.astype(BF16), before)
    dest = [jnp.sum(jnp.where(hit, row, 0.0), axis=0, keepdims=True) for hit in hits]
    dest_ref[...] = jnp.concatenate(dest, axis=0).astype(jnp.int32)
    carry_ref[...] += jnp.sum(m, axis=1, keepdims=True)


def _plan(topi, seg_off):
    b, k, s = topi.shape
    per_b = s // MOE_TOK
    blk = pl.BlockSpec((None, k, MOE_TOK), lambda i: (i // per_b, 0, i % per_b))
    return pl.pallas_call(
        _plan_kernel,
        grid=(b * per_b,),
        in_specs=[blk, pl.BlockSpec((N_EXPERTS, 1), lambda i: (0, 0))],
        out_specs=blk,
        out_shape=jax.ShapeDtypeStruct((b, k, s), jnp.int32),
        scratch_shapes=[pltpu.VMEM((N_EXPERTS, 1), F32)],
        compiler_params=_cparams(("arbitrary",)),
        name="plan",
    )(topi, seg_off.astype(F32).reshape(N_EXPERTS, 1))


SC_WINDOW = 128


def _sc_workers():
    info = plsc.get_sparse_core_info()
    return info.num_cores, info.num_cores * info.num_subcores


def _scatter_rows(src, idx, zero_idx, n_rows):
    t, w = src.shape
    m, mz = idx.shape[0], zero_idx.shape[0]
    n_cores, n_workers = _sc_workers()
    per_worker, per_worker_z = t // n_workers, mz // n_workers
    assert m % t == 0
    assert per_worker * n_workers == t and per_worker % SC_WINDOW == 0
    assert per_worker_z * n_workers == mz and per_worker_z % SC_WINDOW == 0
    mesh = plsc.VectorSubcoreMesh(core_axis_name="core", subcore_axis_name="subcore")

    @functools.partial(
        pl.kernel, mesh=mesh, out_type=jax.ShapeDtypeStruct((n_rows, w), src.dtype),
        scratch_types=[pltpu.VMEM((SC_WINDOW,), jnp.int32), pltpu.VMEM((SC_WINDOW, w), src.dtype),
                       pltpu.SemaphoreType.DMA])
    def scatter(src_hbm, idx_hbm, zeros_hbm, zero_idx_hbm, out_hbm, idx_v, rows_v, sem):
        worker = lax.axis_index("subcore") * n_cores + lax.axis_index("core")

        @pl.loop(0, per_worker // SC_WINDOW)
        def _(step):
            first = pl.multiple_of(worker * per_worker + step * SC_WINDOW, SC_WINDOW)
            pltpu.sync_copy(src_hbm.at[pl.ds(first, SC_WINDOW)], rows_v)
            for copy in range(m // t):
                pltpu.sync_copy(idx_hbm.at[pl.ds(copy * t + first, SC_WINDOW)], idx_v)
                pltpu.async_copy(rows_v, out_hbm.at[idx_v], sem).wait()

        pltpu.sync_copy(zeros_hbm, rows_v)

        @pl.loop(0, per_worker_z // SC_WINDOW)
        def _(step):
            base = pl.multiple_of(worker * per_worker_z + step * SC_WINDOW, SC_WINDOW)
            pltpu.sync_copy(zero_idx_hbm.at[pl.ds(base, SC_WINDOW)], idx_v)
            pltpu.async_copy(rows_v, out_hbm.at[idx_v], sem).wait()

    return scatter(src, idx, jnp.zeros((SC_WINDOW, w), src.dtype), zero_idx)


EXPERT_RING = 3


def _experts_kernel(te_ref, tb_ref, nt_ref, seg_ref, nxt_ref, xs_ref, wg_ref, wu_ref, wd_ref, ys_ref,
                    xbuf, sem, wg_f32, wu_f32, wd_f32, wg_s, wu_s, wd_s, wsem):
    i = pl.program_id(0)
    n_tiles = nt_ref[0]

    def weight_copies(expert, slot):
        return [pltpu.make_async_copy(src.at[expert], dst.at[slot], wsem.at[slot, n])
                for n, (src, dst) in enumerate(((wg_ref, wg_f32), (wu_ref, wu_f32), (wd_ref, wd_f32)))]

    @pl.when(i == 0)
    def _():
        for c in weight_copies(te_ref[0], 0):
            c.start()

    first = (i == 0) | (seg_ref[i] != seg_ref[jnp.maximum(i - 1, 0)])

    @pl.when((i < n_tiles) & first)
    def _():
        slot = seg_ref[i] & 1
        for c in weight_copies(te_ref[i], slot):
            c.wait()
        wg_s[...] = wg_f32[slot].astype(BF16)
        wu_s[...] = wu_f32[slot].astype(BF16)
        wd_s[...] = wd_f32[slot].astype(BF16)

        @pl.when(nxt_ref[i] >= 0)
        def _():
            for c in weight_copies(nxt_ref[i], 1 - slot):
                c.start()

    def tile_copy(j):
        slot = lax.rem(j, EXPERT_RING)
        row0 = pl.multiple_of(tb_ref[j] * MOE_TILE, MOE_TILE)
        return pltpu.make_async_copy(xs_ref.at[pl.ds(row0, MOE_TILE), :], xbuf.at[slot], sem.at[slot])

    @pl.when(i == 0)
    def _():
        for j in range(EXPERT_RING - 1):
            @pl.when(j < n_tiles)
            def _():
                tile_copy(j).start()

    ahead = i + (EXPERT_RING - 1)

    @pl.when(ahead < n_tiles)
    def _():
        tile_copy(ahead).start()

    @pl.when(i < n_tiles)
    def _():
        tile_copy(i).wait()
        x = _unpack_words(xbuf[lax.rem(i, EXPERT_RING)]).astype(BF16)
        act = _silu(_dot(x, wg_s[...])) * _dot(x, wu_s[...])
        ys_ref[...] = _pack_words(_dot(act.astype(BF16), wd_s[...]))


def _experts(xs, tile_expert, tile_block, n_tiles, tile_segment, next_expert, wg, wu, wd):
    d, f = wg.shape[1], wg.shape[2]
    anywhere = pl.BlockSpec(memory_space=pl.ANY)
    grid_spec = pltpu.PrefetchScalarGridSpec(
        num_scalar_prefetch=5,
        grid=(xs.shape[0] // MOE_TILE,),
        in_specs=[anywhere, anywhere, anywhere, anywhere],
        out_specs=pl.BlockSpec((MOE_TILE, d // 2), lambda i, te, tb, nt, seg, nxt: (tb[i], 0)),
        scratch_shapes=[pltpu.VMEM((EXPERT_RING, MOE_TILE, d // 2), U32), pltpu.SemaphoreType.DMA((EXPERT_RING,)),
                        pltpu.VMEM((2, d, f), F32), pltpu.VMEM((2, d, f), F32), pltpu.VMEM((2, f, d), F32),
                        pltpu.VMEM((d, f), BF16), pltpu.VMEM((d, f), BF16), pltpu.VMEM((f, d), BF16),
                        pltpu.SemaphoreType.DMA((2, 3))],
    )
    return pl.pallas_call(
        _experts_kernel,
        grid_spec=grid_spec,
        out_shape=jax.ShapeDtypeStruct(xs.shape, U32),
        compiler_params=_cparams(("arbitrary",)),
        name="experts",
    )(tile_expert, tile_block, n_tiles, tile_segment, next_expert, xs, wg, wu, wd)


def _gather_rows(table, idx):
    m = idx.shape[0]
    w = table.shape[1]
    win = SC_WINDOW // 2
    n_cores, n_workers = _sc_workers()
    per_worker = m // n_workers
    n_pairs = per_worker // (2 * win)
    assert per_worker * n_workers == m and n_pairs * 2 * win == per_worker
    mesh = plsc.VectorSubcoreMesh(core_axis_name="core", subcore_axis_name="subcore")

    @functools.partial(
        pl.kernel, mesh=mesh, out_type=jax.ShapeDtypeStruct((m, w), table.dtype),
        scratch_types=[pltpu.VMEM((win,), jnp.int32), pltpu.VMEM((win,), jnp.int32),
                       pltpu.VMEM((win, w), table.dtype), pltpu.VMEM((win, w), table.dtype),
                       pltpu.SemaphoreType.DMA, pltpu.SemaphoreType.DMA])
    def gather(table_hbm, idx_hbm, out_hbm, idx_a, idx_b, rows_a, rows_b, sem_a, sem_b):
        worker = lax.axis_index("subcore") * n_cores + lax.axis_index("core")
        start = worker * per_worker

        def request(first, idx_v, rows_v, sem):
            pltpu.sync_copy(idx_hbm.at[pl.ds(first, win)], idx_v)
            pltpu.async_copy(table_hbm.at[idx_v], rows_v, sem)

        def deliver(first, idx_v, rows_v, sem):
            pltpu.make_async_copy(table_hbm.at[idx_v], rows_v, sem).wait()
            pltpu.sync_copy(rows_v, out_hbm.at[pl.ds(first, win)])

        request(pl.multiple_of(start, win), idx_a, rows_a, sem_a)

        @pl.loop(0, n_pairs)
        def _(pair):
            first_a = pl.multiple_of(start + pair * 2 * win, win)
            first_b = pl.multiple_of(first_a + win, win)
            request(first_b, idx_b, rows_b, sem_b)
            deliver(first_a, idx_a, rows_a, sem_a)

            @pl.when(pair + 1 < n_pairs)
            def _():
                request(pl.multiple_of(first_b + win, win), idx_a, rows_a, sem_a)

            deliver(first_b, idx_b, rows_b, sem_b)

    return gather(table, idx)


def _combine_kernel(rows_ref, topw_ref, h_ref, x1_ref, g2_ref, sg_ref, su_ref, sd_ref, ln2g_ref, ln2b_ref,
                    *rest, alpha):
    o_ref = rest[-1]
    h = _unpack_words(h_ref[...]).astype(BF16)
    act = _silu(_dot(h, sg_ref[...])) * _dot(h, su_ref[...])
    y = _dot(act.astype(BF16), sd_ref[...])
    w = topw_ref[...].T
    for k in range(TOP_K):
        y = y + w[:, k:k + 1] * _unpack_words(rows_ref[k])
    o_ref[...] = _normalize(alpha * x1_ref[...] + g2_ref[...] * y) * ln2g_ref[...] + ln2b_ref[...]


def _combine(gathered, topw, h2p, x1, g2, sg, su, sd, ln2_g, ln2_b, alpha, batch, out_prev):
    t, d = x1.shape
    _, k, s = topw.shape
    per_b = s // MOE_TOK
    fs = sg.shape[1]
    rows = pl.BlockSpec((MOE_TOK, d), lambda i: (batch * per_b + i, 0))
    packed = pl.BlockSpec((MOE_TOK, d // 2), lambda i: (batch * per_b + i, 0))
    vec = pl.BlockSpec((1, d), lambda i: (0, 0))
    in_specs = [pl.BlockSpec((k, MOE_TOK, d // 2), lambda i: (0, i, 0)),
                pl.BlockSpec((None, k, MOE_TOK), lambda i: (batch, 0, i)),
                packed, rows,
                pl.BlockSpec((None, 1, d), lambda i: (batch, 0, 0)),
                pl.BlockSpec((d, fs), lambda i: (0, 0)),
                pl.BlockSpec((d, fs), lambda i: (0, 0)),
                pl.BlockSpec((fs, d), lambda i: (0, 0)),
                vec, vec]
    args = [gathered, topw, h2p, x1, g2, sg, su, sd, ln2_g.reshape(1, d), ln2_b.reshape(1, d)]
    aliases = {}
    if out_prev is not None:
        in_specs.append(pl.BlockSpec(memory_space=pl.ANY))
        args.append(out_prev)
        aliases = {len(args) - 1: 0}
    return pl.pallas_call(
        functools.partial(_combine_kernel, alpha=alpha),
        grid=(per_b,),
        in_specs=in_specs,
        out_specs=rows,
        out_shape=jax.ShapeDtypeStruct((t, d), F32),
        input_output_aliases=aliases,
        compiler_params=_cparams(("arbitrary",)),
        name="combine",
    )(*args)


def _moe(h2p, topi, topw, cnt, x1, g2, wg, wu, wd, sg, su, sd, ln2_g, ln2_b, alpha):
    b, s, d = x1.shape
    t = b * s
    cnt = cnt[:, 0].astype(jnp.int32)
    tiles_e = (cnt + (MOE_TILE - 1)) // MOE_TILE
    tiles_cum = jnp.cumsum(tiles_e)
    seg_off = (tiles_cum - tiles_e) * MOE_TILE
    n_tiles_max = t * TOP_K // MOE_TILE + N_EXPERTS
    tile_block = jnp.minimum(jnp.arange(n_tiles_max, dtype=jnp.int32), tiles_cum[-1] - 1)
    tile_expert = jnp.sum((tiles_cum[None, :] <= tile_block[:, None]).astype(jnp.int32), axis=1)
    n_tiles = tiles_cum[-1:].astype(jnp.int32)
    present = tiles_e > 0
    seg_of_expert = jnp.cumsum(present.astype(jnp.int32)) - 1
    later = jnp.where(present, jnp.arange(N_EXPERTS, dtype=jnp.int32), N_EXPERTS)
    next_present = jnp.concatenate([lax.cummin(later, reverse=True)[1:], jnp.full((1,), N_EXPERTS, jnp.int32)])
    next_present = jnp.where(next_present < N_EXPERTS, next_present, -1)
    tile_segment = seg_of_expert[tile_expert].astype(jnp.int32)
    next_expert = next_present[tile_expert].astype(jnp.int32)

    dest = jnp.transpose(_plan(topi, seg_off), (1, 0, 2)).reshape(TOP_K * t)
    j = jnp.arange(MOE_TILE, dtype=jnp.int32)[None, :]
    n_pad = (tiles_e * MOE_TILE - cnt)[:, None]
    spare = (n_tiles_max - 1) * MOE_TILE + j
    zero_idx = jnp.where(j < n_pad, (seg_off + cnt)[:, None] + j, spare).reshape(N_EXPERTS * MOE_TILE)
    xs = _scatter_rows(h2p, dest, zero_idx.astype(jnp.int32), n_tiles_max * MOE_TILE)
    ys = _experts(xs, tile_expert, tile_block, n_tiles, tile_segment, next_expert, wg, wu, wd)
    dest = dest.reshape(TOP_K, b, s)
    x1 = x1.reshape(t, d)
    out = None
    for bi in range(b):
        gathered = _gather_rows(ys, dest[:, bi, :].reshape(TOP_K * s)).reshape(TOP_K, s, d // 2)
        out = _combine(gathered, topw, h2p, x1, g2, sg, su, sd, ln2_g, ln2_b, alpha, bi, out)
    return out.reshape(b, s, d)


def kernel(x, c, ctx, c_ctx, w_ada, b_ada, w_in, hg_lb_fwd, hg_lb_bwd, hg_norm_g, na_rpb, w_branch_a, w_branch_b, w_out, ln1_g, ln1_b, w_router, router_bias, w_e_gate, w_e_up, w_e_down, w_sh_gate, w_sh_up, w_sh_down, ln2_g, ln2_b):
    depth = w_ada.shape[0]
    assert depth == 1, "single-layer block"
    b, s, d = x.shape
    alpha = (2.0 * depth) ** 0.25
    l = 0
    lb_fwd = jnp.cumsum(jax.nn.softmax(hg_lb_fwd.astype(F32), axis=0), axis=0)[l]
    lb_bwd = jnp.cumsum(jax.nn.softmax(hg_lb_bwd.astype(F32), axis=0), axis=0)[l]

    cond_rows = jnp.concatenate([c, c_ctx[None, :], jnp.zeros((8 - b - 1, d), F32)], axis=0)
    mod = _ada(cond_rows, w_ada[l], b_ada[l])
    sh1, sc1, g1, sh2, sc2, g2 = [m[:b, None, :] for m in jnp.split(mod, 6, axis=-1)]
    csh1, csc1 = [jnp.broadcast_to(m[b:b + 1, None, :], (b, 1, d)) for m in jnp.split(mod, 6, axis=-1)[:2]]

    w_in_b = w_in[l].astype(BF16)
    p = _inproj(x, sh1, sc1, w_in_b, tuple(range(N_SECTIONS)))
    pc = _inproj(ctx, csh1, csc1, w_in_b, CTX_SECTIONS)

    o_f, o_b = _hgrn(p, pc, lb_fwd, lb_bwd)
    y_na = _natten(p, pc, *_na_tables(na_rpb[l], s))

    x1, h2, topi, topw, cnt = _merge(o_f, o_b, p, y_na, x, g1, sh2, sc2, hg_norm_g[l], ln1_g[l], ln1_b[l],
                                     w_branch_a[l].astype(BF16), w_branch_b[l].astype(BF16),
                                     w_out[l].astype(BF16), w_router[l].T, router_bias[l], alpha)

    return _moe(h2, topi, topw, cnt, x1, g2,
                w_e_gate[l], w_e_up[l], w_e_down[l],
                w_sh_gate[l].astype(BF16), w_sh_up[l].astype(BF16), w_sh_down[l].astype(BF16),
                ln2_g[l], ln2_b[l], alpha)
```

```python
import functools

import numpy as np
import jax
import jax.numpy as jnp
from jax import lax
from jax.experimental import pallas as pl
from jax.experimental.pallas import tpu as pltpu
from jax.experimental.pallas import tpu_sc as plsc

F32 = jnp.float32
BF16 = jnp.bfloat16

D_MODEL = 1024
GRID_W = 64
HG_HEADS = 8
HG_DK = 128
HG_CHUNK = 64
NA_HEADS = 16
NA_HD = 64
NA_WIN_R = 8
NA_WIN_C = 16
ROPE_THETA = 10000.0
NEG_INF = -1e30
N_EXPERTS = 64
EXPERT_DIM = 256
TOP_K = 8
N_GROUPS = 8
TOPK_GROUPS = 4
ROUTED_SCALE = 2.5
LN_EPS = 1e-6
N_SECTIONS = 10
SEC_Q, SEC_FF, SEC_FB, SEC_I, SEC_OG, SEC_NQ, SEC_NK, SEC_NV, SEC_GA, SEC_GB = range(10)
CTX_SECTIONS = (SEC_FF, SEC_FB, SEC_I, SEC_NK, SEC_NV)

VMEM_LIMIT = 56 * 1024 * 1024


def _cparams(sem):
    return pltpu.CompilerParams(dimension_semantics=sem, vmem_limit_bytes=VMEM_LIMIT)


def _normalize(x):
    mu = jnp.mean(x, axis=-1, keepdims=True)
    xc = x - mu
    var = jnp.mean(xc * xc, axis=-1, keepdims=True)
    return xc * lax.rsqrt(var + LN_EPS)


def _silu(x):
    return x * jax.nn.sigmoid(x)


def _dot(a, b):
    return jnp.dot(a, b, preferred_element_type=F32)


def _dot_nt(a, b):
    return lax.dot_general(a, b, (((1,), (1,)), ((), ())), preferred_element_type=F32)


def _dot_tn(a, b):
    return lax.dot_general(a, b, (((0,), (0,)), ((), ())), preferred_element_type=F32)


U32 = jnp.uint32


def _pack_words(x):
    half = x.shape[1] // 2
    lo = lax.bitcast_convert_type(x[:, :half].astype(BF16).astype(F32), U32) >> 16
    hi = lax.bitcast_convert_type(x[:, half:].astype(BF16).astype(F32), U32) & jnp.uint32(0xFFFF0000)
    return lo | hi


def _unpack_words(w):
    lo = lax.bitcast_convert_type(w << 16, F32)
    hi = lax.bitcast_convert_type(w & jnp.uint32(0xFFFF0000), F32)
    return jnp.concatenate([lo, hi], axis=-1)


def _split3(x):
    hi = x.astype(BF16)
    r1 = x - hi.astype(F32)
    mid = r1.astype(BF16)
    lo = (r1 - mid.astype(F32)).astype(BF16)
    return hi, mid, lo


def _ada_kernel(c_ref, w_ref, b_ref, o_ref):
    cond = _silu(c_ref[...])
    o_ref[...] = _dot(cond.astype(BF16), w_ref[...].astype(BF16)) + b_ref[...]


def _ada(cond_rows, w_ada, b_ada):
    r, d = cond_rows.shape
    n = w_ada.shape[1]
    tn = 1024
    return pl.pallas_call(
        _ada_kernel,
        grid=(n // tn,),
        in_specs=[pl.BlockSpec((r, d), lambda j: (0, 0)),
                  pl.BlockSpec((d, tn), lambda j: (0, j)),
                  pl.BlockSpec((1, tn), lambda j: (0, j))],
        out_specs=pl.BlockSpec((r, tn), lambda j: (0, j)),
        out_shape=jax.ShapeDtypeStruct((r, n), F32),
        compiler_params=_cparams(("arbitrary",)),
        name="ada",
    )(cond_rows, w_ada, b_ada.reshape(1, n))


INPROJ_TOK = 2048


def _inproj_kernel(x_ref, sh_ref, sc_ref, w_ref, o_ref, h_ref):
    @pl.when(pl.program_id(2) == 0)
    def _():
        h = _normalize(x_ref[...]) * (1.0 + sc_ref[...]) + sh_ref[...]
        h_ref[...] = h.astype(BF16)

    o_ref[...] = _dot(h_ref[...], w_ref[...])


def _inproj(x, shift, scale, w_in_bf16, sections):
    b, s, d = x.shape
    tm = min(INPROJ_TOK, s)
    nj = len(sections)

    def section(j):
        sec = sections[-1]
        for k in range(nj - 2, -1, -1):
            sec = jnp.where(j == k, sections[k], sec)
        return sec

    return pl.pallas_call(
        _inproj_kernel,
        grid=(b, s // tm, nj),
        in_specs=[pl.BlockSpec((None, tm, d), lambda bi, i, j: (bi, i, 0)),
                  pl.BlockSpec((None, 1, d), lambda bi, i, j: (bi, 0, 0)),
                  pl.BlockSpec((None, 1, d), lambda bi, i, j: (bi, 0, 0)),
                  pl.BlockSpec((d, d), lambda bi, i, j: (0, section(j)))],
        out_specs=pl.BlockSpec((None, None, tm, d), lambda bi, i, j: (j, bi, i, 0)),
        out_shape=jax.ShapeDtypeStruct((nj, b, s, d), F32),
        scratch_shapes=[pltpu.VMEM((tm, d), BF16)],
        compiler_params=_cparams(("arbitrary", "arbitrary", "arbitrary")),
        name="inproj",
    )(x, shift, scale, w_in_bf16)


def _hgrn_gates(q, fraw, v, lb, tri_bf16, last_row):
    f = lb + (1.0 - lb) * jax.nn.sigmoid(fraw)
    k = 1.0 - f
    lf = jnp.log(f)
    hi, mid, lo = _split3(lf)
    a = _dot(tri_bf16, hi) + _dot(tri_bf16, mid) + _dot(tri_bf16, lo)
    a_last = a[last_row:last_row + 1, :]
    kd = (k * jnp.exp(a_last - a)).astype(BF16)
    decay = jnp.exp(a_last)
    qa = kb = None
    if q is not None:
        qa = (_silu(q) * jnp.exp(a)).astype(BF16)
        kb = (k * jnp.exp(-a)).astype(BF16)
    return qa, kb, kd, v.astype(BF16), decay


def _hgrn_chunks(chunks, st_ref):
    first = []
    for d, ((qa, kb, kd, vb, decay), keep) in enumerate(chunks):
        for h in range(HG_HEADS):
            sl = slice(h * HG_DK, (h + 1) * HG_DK)
            st = st_ref[d, h]
            if qa is not None:
                first.append((_dot_nt(qa[:, sl], kb[:, sl]), _dot_nt(qa[:, sl], st.astype(BF16))))
            st_ref[d, h] = st * decay[:, sl] + _dot_tn(vb[:, sl], kd[:, sl])
    results = []
    for d, ((qa, kb, kd, vb, decay), keep) in enumerate(chunks):
        if qa is None:
            results.append(None)
            continue
        outs = []
        for h in range(HG_HEADS):
            sl = slice(h * HG_DK, (h + 1) * HG_DK)
            s_qk, o_state = first.pop(0)
            outs.append(_dot(jnp.where(keep, s_qk, 0.0).astype(BF16), vb[:, sl]) + o_state)
        results.append(jnp.concatenate(outs, axis=-1))
    return results


def _hgrn_kernel(qf_ref, ff_ref, if_ref, qb_ref, fb_ref, ib_ref, cff_ref, cfb_ref, ci_ref,
                 lbf_ref, lbb_ref, of_ref, ob_ref, st_ref, *, n_sub, n_ctx_sub):
    n = pl.program_id(1)
    c = HG_CHUNK
    row = lax.broadcasted_iota(jnp.int32, (c, c), 0)
    col = lax.broadcasted_iota(jnp.int32, (c, c), 1)
    keep_f = col <= row
    keep_b = col >= row
    tri_f = keep_f.astype(F32).astype(BF16)
    tri_b = keep_b.astype(F32).astype(BF16)
    lbf = lbf_ref[...]
    lbb = lbb_ref[...]

    @pl.when(n == 0)
    def _():
        st_ref[...] = jnp.zeros_like(st_ref)

        def body(i, carry):
            r0 = pl.multiple_of(i * c, c)
            r1 = pl.multiple_of((n_ctx_sub - 1 - i) * c, c)
            gf = _hgrn_gates(None, cff_ref[pl.ds(r0, c), :], ci_ref[pl.ds(r0, c), :], lbf, tri_f, c - 1)
            gb = _hgrn_gates(None, cfb_ref[pl.ds(r1, c), :], ci_ref[pl.ds(r1, c), :], lbb, tri_b, 0)
            _hgrn_chunks([(gf, keep_f), (gb, keep_b)], st_ref)
            return carry

        lax.fori_loop(0, n_ctx_sub, body, 0)

    @pl.when(n > 0)
    def _():
        def body(i, carry):
            r0 = pl.multiple_of(i * c, c)
            r1 = pl.multiple_of((n_sub - 1 - i) * c, c)
            gf = _hgrn_gates(qf_ref[pl.ds(r0, c), :], ff_ref[pl.ds(r0, c), :], if_ref[pl.ds(r0, c), :],
                             lbf, tri_f, c - 1)
            gb = _hgrn_gates(qb_ref[pl.ds(r1, c), :], fb_ref[pl.ds(r1, c), :], ib_ref[pl.ds(r1, c), :],
                             lbb, tri_b, 0)
            o_f, o_b = _hgrn_chunks([(gf, keep_f), (gb, keep_b)], st_ref)
            of_ref[pl.ds(r0, c), :] = o_f
            ob_ref[pl.ds(r1, c), :] = o_b
            return carry

        lax.fori_loop(0, n_sub, body, 0, unroll=True)


def _hgrn(p, pc, lb_fwd, lb_bwd):
    _, b, s, w = p.shape
    ctx_len = pc.shape[2]
    tb = min(256, s)
    nb = s // tb
    fwd = lambda bi, n: jnp.maximum(n - 1, 0)
    bwd = lambda bi, n: nb - 1 - jnp.maximum(n - 1, 0)

    def sec(section, blk):
        return pl.BlockSpec((None, None, tb, w), lambda bi, n: (section, bi, blk(bi, n), 0))

    def csec(section):
        return pl.BlockSpec((None, None, ctx_len, w), lambda bi, n: (CTX_SECTIONS.index(section), bi, 0, 0))

    vec = pl.BlockSpec((1, w), lambda bi, n: (0, 0))
    kern = functools.partial(_hgrn_kernel, n_sub=tb // HG_CHUNK, n_ctx_sub=ctx_len // HG_CHUNK)
    return pl.pallas_call(
        kern,
        grid=(b, nb + 1),
        in_specs=[sec(SEC_Q, fwd), sec(SEC_FF, fwd), sec(SEC_I, fwd),
                  sec(SEC_Q, bwd), sec(SEC_FB, bwd), sec(SEC_I, bwd),
                  csec(SEC_FF), csec(SEC_FB), csec(SEC_I), vec, vec],
        out_specs=[pl.BlockSpec((None, tb, w), lambda bi, n: (bi, fwd(bi, n), 0)),
                   pl.BlockSpec((None, tb, w), lambda bi, n: (bi, bwd(bi, n), 0))],
        out_shape=[jax.ShapeDtypeStruct((b, s, w), F32), jax.ShapeDtypeStruct((b, s, w), F32)],
        scratch_shapes=[pltpu.VMEM((2, HG_HEADS, HG_DK, HG_DK), F32)],
        compiler_params=_cparams(("arbitrary", "arbitrary")),
        name="hgrn",
    )(p, p, p, p, p, p, pc, pc, pc, lb_fwd.reshape(1, w), lb_bwd.reshape(1, w))


NA_ROWS_PER_STEP = 32
NA_PREP_ROWS = 512
NA_KEY_TILE = 128
NA_SPAN = NA_WIN_R * GRID_W


def _rope(t, cos, sin_signed, first_half):
    w = t.shape[-1]
    partner = jnp.where(first_half, pltpu.roll(t, w - 16, 1), pltpu.roll(t, 16, 1))
    return t * cos + partner * sin_signed


def _fold_lanes(op, *arrays):
    tiles = [a[:, c:c + 128] for a in arrays for c in range(0, a.shape[-1], 128)]
    acc = tiles[0]
    for t in tiles[1:]:
        acc = op(acc, t)
    return acc


def _rope_tables(rowtab_ref, coltab_ref, row0, n_rows, row_lane):
    out = []
    for i in range(2):
        rt = rowtab_ref[i, pl.ds(row0, n_rows), :]
        by_row = jnp.concatenate([jnp.broadcast_to(rt[r:r + 1, :], (GRID_W, rt.shape[1])) for r in range(n_rows)],
                                 axis=0)
        by_col = jnp.concatenate([coltab_ref[i]] * n_rows, axis=0)
        out.append(jnp.where(row_lane, by_row, by_col))
    return out


def _natten_kernel(q_ref, k_ref, v_ref, kc_ref, vc_ref, rowtab_ref, coltab_ref, t2_ref, o_ref,
                   kt_s, v_s, kc_s, vc_s, bias_s, tail_s, *, rows):
    rblk = pl.program_id(2)
    hd = NA_HD
    lane = lax.broadcasted_iota(jnp.int32, (1, 2 * hd), 1)
    first_half = (lane % 32) < 16
    row_lane = (lane % hd) < hd // 2
    scale = NA_HD ** -0.5

    def values_and_ones(v_pair, h):
        vh = v_pair if h == 0 else pltpu.roll(v_pair, hd, 1)
        return jnp.where(lane < hd, vh, jnp.where(lane == hd, 1.0, 0.0)).astype(BF16)

    @pl.when(rblk == 0)
    def _():
        kc = kc_ref[...].astype(BF16)
        qi = lax.broadcasted_iota(jnp.int32, (GRID_W, GRID_W), 0)
        ki = lax.broadcasted_iota(jnp.int32, (GRID_W, GRID_W), 1)
        cstart = jnp.clip(qi - NA_WIN_C // 2, 0, GRID_W - NA_WIN_C)
        in_win = (ki >= cstart) & (ki < cstart + NA_WIN_C)
        s_len = k_ref.shape[0]
        tail_s[...] = jnp.zeros_like(tail_s)
        for h in range(2):
            sl = slice(h * hd, (h + 1) * hd)
            kc_s[h] = kc[:, sl]
            vc_s[h] = values_and_ones(vc_ref[...], h)
            tiles = [jnp.where(in_win, t2_ref[h, dr], NEG_INF) for dr in range(2 * NA_WIN_R - 1)]
            for v in range(NA_WIN_R):
                for j in range(NA_WIN_R):
                    bias_s[h, v, :, j * GRID_W:(j + 1) * GRID_W] = tiles[NA_WIN_R - 1 - v + j]

        eye = (lax.broadcasted_iota(jnp.int32, (2 * hd, 2 * hd), 0)
               == lax.broadcasted_iota(jnp.int32, (2 * hd, 2 * hd), 1)).astype(F32).astype(BF16)

        def prep(i, carry):
            r0 = pl.multiple_of(i * NA_PREP_ROWS, NA_PREP_ROWS)
            rws = pl.ds(r0, NA_PREP_ROWS)
            cos, sin = _rope_tables(rowtab_ref, coltab_ref, i * (NA_PREP_ROWS // GRID_W), NA_PREP_ROWS // GRID_W,
                                    row_lane)
            kr = _rope(k_ref[rws, :], cos, sin, first_half)
            kr_odd = jnp.concatenate([tail_s[...], kr[:NA_PREP_ROWS - GRID_W]], axis=0)
            tail_s[...] = kr[NA_PREP_ROWS - GRID_W:]
            krt = [_dot_nt(eye, kr.astype(BF16)).astype(BF16),
                   _dot_nt(eye, kr_odd.astype(BF16)).astype(BF16)]
            vv = v_ref[rws, :]
            for h in range(2):
                sl = slice(h * hd, (h + 1) * hd)
                for par in range(2):
                    for c in range(NA_PREP_ROWS // NA_KEY_TILE):
                        kt_s[h, par, i * (NA_PREP_ROWS // NA_KEY_TILE) + c] = (
                            krt[par][sl, c * NA_KEY_TILE:(c + 1) * NA_KEY_TILE])
                v_s[h, rws, :] = values_and_ones(vv, h)
            return carry

        lax.fori_loop(0, s_len // NA_PREP_ROWS, prep, 0, unroll=4)

    tq = NA_ROWS_PER_STEP * GRID_W
    q = q_ref[...] * scale
    cos, sin = _rope_tables(rowtab_ref, coltab_ref, rblk * NA_ROWS_PER_STEP, NA_ROWS_PER_STEP, row_lane)
    qr = _rope(q, cos, sin, first_half)
    qb = q.astype(BF16)
    qrb = qr.astype(BF16)
    rws = [slice(rr * GRID_W, (rr + 1) * GRID_W) for rr in range(NA_ROWS_PER_STEP)]
    par, slot0, key0, bidx = [], [], [], []
    for rr in range(NA_ROWS_PER_STEP):
        r = rblk * NA_ROWS_PER_STEP + rr
        rs = jnp.clip(r - NA_WIN_R // 2, 0, rows - NA_WIN_R)
        par.append(rs & 1)
        slot0.append(lax.shift_right_logical(rs, 1) + (rs & 1))
        key0.append(pl.multiple_of(rs * GRID_W, GRID_W))
        bidx.append(r - rs)

    def scores(h):
        sl = slice(h * hd, (h + 1) * hd)
        qrb_h = qrb[:, sl]
        s_ctx_all = _dot_nt(qb[:, sl], kc_s[h])
        s_win = []
        for rr in range(NA_ROWS_PER_STEP):
            kt = kt_s[h, par[rr], pl.ds(slot0[rr], NA_SPAN // NA_KEY_TILE)]
            kt = jnp.concatenate([kt[c] for c in range(NA_SPAN // NA_KEY_TILE)], axis=-1)
            s_win.append(_dot(qrb_h[rws[rr]], kt))
        return s_win, s_ctx_all

    def softmax(h, s_win, s_ctx_all):
        e_win, e_ctx = [], []
        for rr in range(NA_ROWS_PER_STEP):
            sw = s_win[rr] + bias_s[h, bidx[rr]]
            sc = s_ctx_all[rws[rr]]
            m = jnp.max(_fold_lanes(jnp.maximum, sw, sc), axis=-1, keepdims=True)
            e_win.append(jnp.exp(sw - m).astype(BF16))
            e_ctx.append(jnp.exp(sc - m).astype(BF16))
        return e_win, e_ctx

    def values(h, e_win, e_ctx):
        o_win = []
        for rr in range(NA_ROWS_PER_STEP):
            o_win.append(_dot(e_win[rr], v_s[h, pl.ds(key0[rr], NA_SPAN), :]))
        o = jnp.concatenate(o_win, axis=0) + _dot(jnp.concatenate(e_ctx, axis=0), vc_s[h])
        return o[:, :hd] * (1.0 / o[:, hd:hd + 1])

    s0 = scores(0)
    s1 = scores(1)
    p0 = softmax(0, *s0)
    o0 = values(0, *p0)
    p1 = softmax(1, *s1)
    o1 = values(1, *p1)
    o_ref[...] = jnp.concatenate([o0, o1], axis=-1).astype(o_ref.dtype)


def _na_tables(rpb, s):
    half = NA_HD // 2
    inv = jnp.power(ROPE_THETA, -jnp.arange(0, half, 2, dtype=F32) / half)

    def tables(n):
        ang = jnp.arange(n, dtype=F32)[:, None] * inv[None, :]
        reps = 2 * NA_HD // half
        return jnp.stack([jnp.tile(jnp.cos(ang), (1, 2 * reps)),
                          jnp.tile(jnp.concatenate([-jnp.sin(ang), jnp.sin(ang)], axis=-1), (1, reps))])

    rowtab, coltab = tables(s // GRID_W), tables(GRID_W)

    pad = GRID_W - NA_WIN_C
    width = 2 * GRID_W
    rp = jnp.pad(rpb.astype(F32), ((0, 0), (0, 0), (pad, pad + 2)), mode="edge")
    skew = jnp.tile(rp, (1, 1, GRID_W + 1))[:, :, GRID_W - 1:GRID_W - 1 + GRID_W * width]
    t2 = skew.reshape(rp.shape[0], rp.shape[1], GRID_W, width)[:, :, :, :GRID_W]
    return rowtab, coltab, t2


def _natten(p, pc, rowtab, coltab, t2):
    _, b, s, w = p.shape
    ctx_len = pc.shape[2]
    rows = s // GRID_W
    assert rows >= NA_WIN_R and rows % NA_ROWS_PER_STEP == 0
    tq = NA_ROWS_PER_STEP * GRID_W
    hw = 2 * NA_HD
    nhp = w // hw
    kern = functools.partial(_natten_kernel, rows=rows)
    return pl.pallas_call(
        kern,
        grid=(b, nhp, rows // NA_ROWS_PER_STEP),
        in_specs=[pl.BlockSpec((None, None, tq, hw), lambda bi, hp, r: (SEC_NQ, bi, r, hp)),
                  pl.BlockSpec((None, None, s, hw), lambda bi, hp, r: (SEC_NK, bi, 0, hp)),
                  pl.BlockSpec((None, None, s, hw), lambda bi, hp, r: (SEC_NV, bi, 0, hp)),
                  pl.BlockSpec((None, None, ctx_len, hw), lambda bi, hp, r: (CTX_SECTIONS.index(SEC_NK), bi, 0, hp)),
                  pl.BlockSpec((None, None, ctx_len, hw), lambda bi, hp, r: (CTX_SECTIONS.index(SEC_NV), bi, 0, hp)),
                  pl.BlockSpec((2, rows, hw), lambda bi, hp, r: (0, 0, 0)),
                  pl.BlockSpec((2, GRID_W, hw), lambda bi, hp, r: (0, 0, 0)),
                  pl.BlockSpec((2, 2 * NA_WIN_R - 1, GRID_W, GRID_W), lambda bi, hp, r: (hp, 0, 0, 0))],
        out_specs=pl.BlockSpec((None, tq, hw), lambda bi, hp, r: (bi, r, hp)),
        out_shape=jax.ShapeDtypeStruct((b, s, w), BF16),
        scratch_shapes=[pltpu.VMEM((2, 2, s // NA_KEY_TILE, NA_HD, NA_KEY_TILE), BF16),
                        pltpu.VMEM((2, s, hw), BF16),
                        pltpu.VMEM((2, ctx_len, NA_HD), BF16), pltpu.VMEM((2, ctx_len, hw), BF16),
                        pltpu.VMEM((2, NA_WIN_R, GRID_W, NA_SPAN), F32),
                        pltpu.VMEM((GRID_W, hw), F32)],
        compiler_params=_cparams(("arbitrary", "arbitrary", "arbitrary")),
        name="natten",
    )(p, p, p, pc, pc, rowtab, coltab, t2)


def _route(logits_t, rbias):
    e, t = logits_t.shape
    gsz = e // N_GROUPS
    scores = jax.nn.sigmoid(logits_t)
    sel = scores + rbias
    neg = -jnp.inf
    sub = lax.broadcasted_iota(jnp.int32, (gsz, t), 0).astype(F32)
    gscore = []
    for g in range(N_GROUPS):
        grp = sel[g * gsz:(g + 1) * gsz, :]
        m1 = jnp.max(grp, axis=0, keepdims=True)
        first = jnp.min(jnp.where(grp == m1, sub, float(gsz)), axis=0, keepdims=True)
        m2 = jnp.max(jnp.where(sub == first, neg, grp), axis=0, keepdims=True)
        gscore.append(m1 + m2)
    masked = []
    for g in range(N_GROUPS):
        rank = jnp.zeros((1, t), F32)
        for g2 in range(N_GROUPS):
            if g2 == g:
                continue
            if g2 < g:
                ahead = gscore[g2] >= gscore[g]
            else:
                ahead = gscore[g2] > gscore[g]
            rank = rank + jnp.where(ahead, 1.0, 0.0)
        masked.append(jnp.where(rank < TOPK_GROUPS, sel[g * gsz:(g + 1) * gsz, :], neg))
    work = jnp.concatenate(masked, axis=0)
    eidx = lax.broadcasted_iota(jnp.int32, (e, t), 0).astype(F32)
    idxs, ws = [], []
    chosen = jnp.zeros((e, t), F32)
    for _ in range(TOP_K):
        m = jnp.max(work, axis=0, keepdims=True)
        first = jnp.min(jnp.where(work == m, eidx, float(e)), axis=0, keepdims=True)
        pick = eidx == first
        idxs.append(first)
        ws.append(jnp.sum(jnp.where(pick, scores, 0.0), axis=0, keepdims=True))
        chosen = jnp.where(pick, 1.0, chosen)
        work = jnp.where(pick, neg, work)
    w = jnp.concatenate(ws, axis=0)
    w = w / jnp.sum(w, axis=0, keepdims=True) * ROUTED_SCALE
    return jnp.concatenate(idxs, axis=0).astype(jnp.int32), w, chosen


MERGE_TOK = 512
MERGE_SUB = 256


def _merge_kernel(of_ref, ob_ref, og_ref, yna_ref, ga_ref, gb_ref, x_ref, g1_ref, sh2_ref, sc2_ref,
                  hgg_ref, ln1g_ref, ln1b_ref, wa_ref, wb_ref, wo_ref, wr_ref, rb_ref,
                  x1_ref, h2_ref, topi_ref, topw_ref, cnt_ref, *, alpha):
    tm = x_ref.shape[0]
    subs = [slice(i * MERGE_SUB, (i + 1) * MERGE_SUB) for i in range(tm // MERGE_SUB)]

    def branches(rows):
        o = of_ref[rows, :] + ob_ref[rows, :]
        parts = []
        for h in range(HG_HEADS):
            oh = o[:, h * HG_DK:(h + 1) * HG_DK]
            parts.append(oh * lax.rsqrt(jnp.mean(oh * oh, axis=-1, keepdims=True) + LN_EPS))
        y_hg = jnp.concatenate(parts, axis=-1) * hgg_ref[...] * _silu(og_ref[rows, :])
        return _dot(y_hg.astype(BF16), wa_ref[...]), _dot(yna_ref[rows, :], wb_ref[...])

    def out_proj(rows, ya, yb):
        t = jax.nn.sigmoid(ga_ref[rows, :]) * ya + jax.nn.sigmoid(gb_ref[rows, :]) * yb
        return _dot(t.astype(BF16), wo_ref[...])

    def norms_router(rows, i, y):
        x1 = _normalize(alpha * x_ref[rows, :] + g1_ref[...] * y) * ln1g_ref[...] + ln1b_ref[...]
        x1_ref[rows, :] = x1
        h2 = _normalize(x1) * (1.0 + sc2_ref[...]) + sh2_ref[...]
        h2_ref[rows, :] = _pack_words(h2)
        hh, hm, hl = _split3(h2)
        wh, wm, wl = _split3(wr_ref[...])
        return (_dot_nt(wh, hh) + _dot_nt(wh, hm) + _dot_nt(wm, hh)
                + _dot_nt(wh, hl) + _dot_nt(wl, hh) + _dot_nt(wm, hm))

    ab = [branches(rows) for rows in subs]
    ys = [out_proj(rows, *ab[i]) for i, rows in enumerate(subs)]
    logits = [norms_router(rows, i, ys[i]) for i, rows in enumerate(subs)]

    @pl.when((pl.program_id(0) == 0) & (pl.program_id(1) == 0))
    def _():
        cnt_ref[...] = jnp.zeros_like(cnt_ref)

    for i, rows in enumerate(subs):
        topi, topw, chosen = _route(logits[i], rb_ref[...])
        topi_ref[:, rows] = topi
        topw_ref[:, rows] = topw
        cnt_ref[...] += jnp.sum(chosen, axis=1, keepdims=True)


def _merge(o_f, o_b, p, y_na, x, g1, sh2, sc2, hg_norm_g, ln1_g, ln1_b, w_a, w_b, w_o, w_router_t, router_bias,
           alpha):
    b, s, d = x.shape
    tm = min(MERGE_TOK, s)
    e = w_router_t.shape[0]
    tok = lambda bi, i: (bi, i, 0)
    blk = pl.BlockSpec((None, tm, d), tok)

    def sec(section):
        return pl.BlockSpec((None, None, tm, d), lambda bi, i: (section, bi, i, 0))

    mod = pl.BlockSpec((None, 1, d), lambda bi, i: (bi, 0, 0))
    vec = pl.BlockSpec((1, d), lambda bi, i: (0, 0))
    mat = pl.BlockSpec((d, d), lambda bi, i: (0, 0), pipeline_mode=pl.Buffered(1))
    return pl.pallas_call(
        functools.partial(_merge_kernel, alpha=alpha),
        grid=(b, s // tm),
        in_specs=[blk, blk, sec(SEC_OG), blk, sec(SEC_GA), sec(SEC_GB), blk, mod, mod, mod,
                  vec, vec, vec, mat, mat, mat,
                  pl.BlockSpec((e, d), lambda bi, i: (0, 0)),
                  pl.BlockSpec((e, 1), lambda bi, i: (0, 0))],
        out_specs=[blk,
                   pl.BlockSpec((tm, d // 2), lambda bi, i: (bi * (s // tm) + i, 0)),
                   pl.BlockSpec((None, TOP_K, tm), lambda bi, i: (bi, 0, i)),
                   pl.BlockSpec((None, TOP_K, tm), lambda bi, i: (bi, 0, i)),
                   pl.BlockSpec((e, 128), lambda bi, i: (0, 0))],
        out_shape=[jax.ShapeDtypeStruct((b, s, d), F32),
                   jax.ShapeDtypeStruct((b * s, d // 2), U32),
                   jax.ShapeDtypeStruct((b, TOP_K, s), jnp.int32), jax.ShapeDtypeStruct((b, TOP_K, s), F32),
                   jax.ShapeDtypeStruct((e, 128), F32)],
        compiler_params=_cparams(("arbitrary", "arbitrary")),
        name="merge",
    )(o_f, o_b, p, y_na, p, p, x, g1, sh2, sc2, hg_norm_g.reshape(1, d), ln1_g.reshape(1, d),
      ln1_b.reshape(1, d), w_a, w_b, w_o, w_router_t, router_bias.reshape(e, 1))


MOE_TILE = 512
MOE_TOK = 512


def _plan_kernel(topi_ref, off_ref, dest_ref, carry_ref):
    @pl.when(pl.program_id(0) == 0)
    def _():
        carry_ref[...] = jnp.zeros_like(carry_ref)

    topi = topi_ref[...]
    tok = topi.shape[1]
    eidx = lax.broadcasted_iota(jnp.int32, (N_EXPERTS, tok), 0)
    hits = [eidx == topi[k:k + 1, :] for k in range(TOP_K)]
    m = jnp.zeros((N_EXPERTS, tok), F32)
    for hit in hits:
        m = jnp.where(hit, 1.0, m)
    before = (lax.broadcasted_iota(jnp.int32, (tok, tok), 0)
              < lax.broadcasted_iota(jnp.int32, (tok, tok), 1)).astype(F32).astype(BF16)
    row = off_ref[...] + carry_ref[...] + _dot(m.astype(BF16), before)
    dest = [jnp.sum(jnp.where(hit, row, 0.0), axis=0, keepdims=True) for hit in hits]
    dest_ref[...] = jnp.concatenate(dest, axis=0).astype(jnp.int32)
    carry_ref[...] += jnp.sum(m, axis=1, keepdims=True)


def _plan(topi, seg_off):
    b, k, s = topi.shape
    per_b = s // MOE_TOK
    blk = pl.BlockSpec((None, k, MOE_TOK), lambda i: (i // per_b, 0, i % per_b))
    return pl.pallas_call(
        _plan_kernel,
        grid=(b * per_b,),
        in_specs=[blk, pl.BlockSpec((N_EXPERTS, 1), lambda i: (0, 0))],
        out_specs=blk,
        out_shape=jax.ShapeDtypeStruct((b, k, s), jnp.int32),
        scratch_shapes=[pltpu.VMEM((N_EXPERTS, 1), F32)],
        compiler_params=_cparams(("arbitrary",)),
        name="plan",
    )(topi, seg_off.astype(F32).reshape(N_EXPERTS, 1))


SC_WINDOW = 128


def _sc_workers():
    info = plsc.get_sparse_core_info()
    return info.num_cores, info.num_cores * info.num_subcores


def _scatter_rows(src, idx, zero_idx, n_rows):
    t, w = src.shape
    m, mz = idx.shape[0], zero_idx.shape[0]
    n_cores, n_workers = _sc_workers()
    per_worker, per_worker_z = t // n_workers, mz // n_workers
    assert m % t == 0
    assert per_worker * n_workers == t and per_worker % SC_WINDOW == 0
    assert per_worker_z * n_workers == mz and per_worker_z % SC_WINDOW == 0
    mesh = plsc.VectorSubcoreMesh(core_axis_name="core", subcore_axis_name="subcore")

    @functools.partial(
        pl.kernel, mesh=mesh, out_type=jax.ShapeDtypeStruct((n_rows, w), src.dtype),
        scratch_types=[pltpu.VMEM((SC_WINDOW,), jnp.int32), pltpu.VMEM((SC_WINDOW, w), src.dtype),
                       pltpu.SemaphoreType.DMA])
    def scatter(src_hbm, idx_hbm, zeros_hbm, zero_idx_hbm, out_hbm, idx_v, rows_v, sem):
        worker = lax.axis_index("subcore") * n_cores + lax.axis_index("core")

        @pl.loop(0, per_worker // SC_WINDOW)
        def _(step):
            first = pl.multiple_of(worker * per_worker + step * SC_WINDOW, SC_WINDOW)
            pltpu.sync_copy(src_hbm.at[pl.ds(first, SC_WINDOW)], rows_v)
            for copy in range(m // t):
                pltpu.sync_copy(idx_hbm.at[pl.ds(copy * t + first, SC_WINDOW)], idx_v)
                pltpu.async_copy(rows_v, out_hbm.at[idx_v], sem).wait()

        pltpu.sync_copy(zeros_hbm, rows_v)

        @pl.loop(0, per_worker_z // SC_WINDOW)
        def _(step):
            base = pl.multiple_of(worker * per_worker_z + step * SC_WINDOW, SC_WINDOW)
            pltpu.sync_copy(zero_idx_hbm.at[pl.ds(base, SC_WINDOW)], idx_v)
            pltpu.async_copy(rows_v, out_hbm.at[idx_v], sem).wait()

    return scatter(src, idx, jnp.zeros((SC_WINDOW, w), src.dtype), zero_idx)


EXPERT_RING = 3


def _experts_kernel(te_ref, tb_ref, nt_ref, seg_ref, nxt_ref, xs_ref, wg_ref, wu_ref, wd_ref, ys_ref,
                    xbuf, sem, wg_f32, wu_f32, wd_f32, wg_s, wu_s, wd_s, wsem):
    i = pl.program_id(0)
    n_tiles = nt_ref[0]

    def weight_copies(expert, slot):
        return [pltpu.make_async_copy(src.at[expert], dst.at[slot], wsem.at[slot, n])
                for n, (src, dst) in enumerate(((wg_ref, wg_f32), (wu_ref, wu_f32), (wd_ref, wd_f32)))]

    @pl.when(i == 0)
    def _():
        for c in weight_copies(te_ref[0], 0):
            c.start()

    first = (i == 0) | (seg_ref[i] != seg_ref[jnp.maximum(i - 1, 0)])

    @pl.when((i < n_tiles) & first)
    def _():
        slot = seg_ref[i] & 1
        for c in weight_copies(te_ref[i], slot):
            c.wait()
        wg_s[...] = wg_f32[slot].astype(BF16)
        wu_s[...] = wu_f32[slot].astype(BF16)
        wd_s[...] = wd_f32[slot].astype(BF16)

        @pl.when(nxt_ref[i] >= 0)
        def _():
            for c in weight_copies(nxt_ref[i], 1 - slot):
                c.start()

    def tile_copy(j):
        slot = lax.rem(j, EXPERT_RING)
        row0 = pl.multiple_of(tb_ref[j] * MOE_TILE, MOE_TILE)
        return pltpu.make_async_copy(xs_ref.at[pl.ds(row0, MOE_TILE), :], xbuf.at[slot], sem.at[slot])

    @pl.when(i == 0)
    def _():
        for j in range(EXPERT_RING - 1):
            @pl.when(j < n_tiles)
            def _():
                tile_copy(j).start()

    ahead = i + (EXPERT_RING - 1)

    @pl.when(ahead < n_tiles)
    def _():
        tile_copy(ahead).start()

    @pl.when(i < n_tiles)
    def _():
        tile_copy(i).wait()
        x = _unpack_words(xbuf[lax.rem(i, EXPERT_RING)]).astype(BF16)
        act = _silu(_dot(x, wg_s[...])) * _dot(x, wu_s[...])
        ys_ref[...] = _pack_words(_dot(act.astype(BF16), wd_s[...]))


def _experts(xs, tile_expert, tile_block, n_tiles, tile_segment, next_expert, wg, wu, wd):
    d, f = wg.shape[1], wg.shape[2]
    anywhere = pl.BlockSpec(memory_space=pl.ANY)
    grid_spec = pltpu.PrefetchScalarGridSpec(
        num_scalar_prefetch=5,
        grid=(xs.shape[0] // MOE_TILE,),
        in_specs=[anywhere, anywhere, anywhere, anywhere],
        out_specs=pl.BlockSpec((MOE_TILE, d // 2), lambda i, te, tb, nt, seg, nxt: (tb[i], 0)),
        scratch_shapes=[pltpu.VMEM((EXPERT_RING, MOE_TILE, d // 2), U32), pltpu.SemaphoreType.DMA((EXPERT_RING,)),
                        pltpu.VMEM((2, d, f), F32), pltpu.VMEM((2, d, f), F32), pltpu.VMEM((2, f, d), F32),
                        pltpu.VMEM((d, f), BF16), pltpu.VMEM((d, f), BF16), pltpu.VMEM((f, d), BF16),
                        pltpu.SemaphoreType.DMA((2, 3))],
    )
    return pl.pallas_call(
        _experts_kernel,
        grid_spec=grid_spec,
        out_shape=jax.ShapeDtypeStruct(xs.shape, U32),
        compiler_params=_cparams(("arbitrary",)),
        name="experts",
    )(tile_expert, tile_block, n_tiles, tile_segment, next_expert, xs, wg, wu, wd)


def _gather_rows(table, idx):
    m = idx.shape[0]
    w = table.shape[1]
    win = SC_WINDOW // 2
    n_cores, n_workers = _sc_workers()
    per_worker = m // n_workers
    n_pairs = per_worker // (2 * win)
    assert per_worker * n_workers == m and n_pairs * 2 * win == per_worker
    mesh = plsc.VectorSubcoreMesh(core_axis_name="core", subcore_axis_name="subcore")

    @functools.partial(
        pl.kernel, mesh=mesh, out_type=jax.ShapeDtypeStruct((m, w), table.dtype),
        scratch_types=[pltpu.VMEM((win,), jnp.int32), pltpu.VMEM((win,), jnp.int32),
                       pltpu.VMEM((win, w), table.dtype), pltpu.VMEM((win, w), table.dtype),
                       pltpu.SemaphoreType.DMA, pltpu.SemaphoreType.DMA])
    def gather(table_hbm, idx_hbm, out_hbm, idx_a, idx_b, rows_a, rows_b, sem_a, sem_b):
        worker = lax.axis_index("subcore") * n_cores + lax.axis_index("core")
        start = worker * per_worker

        def request(first, idx_v, rows_v, sem):
            pltpu.sync_copy(idx_hbm.at[pl.ds(first, win)], idx_v)
            pltpu.async_copy(table_hbm.at[idx_v], rows_v, sem)

        def deliver(first, idx_v, rows_v, sem):
            pltpu.make_async_copy(table_hbm.at[idx_v], rows_v, sem).wait()
            pltpu.sync_copy(rows_v, out_hbm.at[pl.ds(first, win)])

        request(pl.multiple_of(start, win), idx_a, rows_a, sem_a)

        @pl.loop(0, n_pairs)
        def _(pair):
            first_a = pl.multiple_of(start + pair * 2 * win, win)
            first_b = pl.multiple_of(first_a + win, win)
            request(first_b, idx_b, rows_b, sem_b)
            deliver(first_a, idx_a, rows_a, sem_a)

            @pl.when(pair + 1 < n_pairs)
            def _():
                request(pl.multiple_of(first_b + win, win), idx_a, rows_a, sem_a)

            deliver(first_b, idx_b, rows_b, sem_b)

    return gather(table, idx)


def _combine_kernel(rows_ref, topw_ref, h_ref, x1_ref, g2_ref, sg_ref, su_ref, sd_ref, ln2g_ref, ln2b_ref,
                    o_ref, *, alpha):
    h = _unpack_words(h_ref[...]).astype(BF16)
    act = _silu(_dot(h, sg_ref[...])) * _dot(h, su_ref[...])
    y = _dot(act.astype(BF16), sd_ref[...])
    w = topw_ref[...].T
    for k in range(TOP_K):
        y = y + w[:, k:k + 1] * _unpack_words(rows_ref[k])
    o_ref[...] = _normalize(alpha * x1_ref[...] + g2_ref[...] * y) * ln2g_ref[...] + ln2b_ref[...]


def _combine(gathered, topw, h2p, x1, g2, sg, su, sd, ln2_g, ln2_b, alpha):
    t, d = x1.shape
    b, k, s = topw.shape
    per_b = s // MOE_TOK
    fs = sg.shape[1]
    rows = pl.BlockSpec((MOE_TOK, d), lambda i: (i, 0))
    packed = pl.BlockSpec((MOE_TOK, d // 2), lambda i: (i, 0))
    vec = pl.BlockSpec((1, d), lambda i: (0, 0))
    return pl.pallas_call(
        functools.partial(_combine_kernel, alpha=alpha),
        grid=(t // MOE_TOK,),
        in_specs=[pl.BlockSpec((k, MOE_TOK, d // 2), lambda i: (0, i, 0)),
                  pl.BlockSpec((None, k, MOE_TOK), lambda i: (i // per_b, 0, i % per_b)),
                  packed, rows,
                  pl.BlockSpec((None, 1, d), lambda i: (i // per_b, 0, 0)),
                  pl.BlockSpec((d, fs), lambda i: (0, 0)),
                  pl.BlockSpec((d, fs), lambda i: (0, 0)),
                  pl.BlockSpec((fs, d), lambda i: (0, 0)),
                  vec, vec],
        out_specs=rows,
        out_shape=jax.ShapeDtypeStruct((t, d), F32),
        compiler_params=_cparams(("arbitrary",)),
        name="combine",
    )(gathered, topw, h2p, x1, g2, sg, su, sd, ln2_g.reshape(1, d), ln2_b.reshape(1, d))


def _moe(h2p, topi, topw, cnt, x1, g2, wg, wu, wd, sg, su, sd, ln2_g, ln2_b, alpha):
    b, s, d = x1.shape
    t = b * s
    cnt = cnt[:, 0].astype(jnp.int32)
    tiles_e = (cnt + (MOE_TILE - 1)) // MOE_TILE
    tiles_cum = jnp.cumsum(tiles_e)
    seg_off = (tiles_cum - tiles_e) * MOE_TILE
    n_tiles_max = t * TOP_K // MOE_TILE + N_EXPERTS
    tile_block = jnp.minimum(jnp.arange(n_tiles_max, dtype=jnp.int32), tiles_cum[-1] - 1)
    tile_expert = jnp.sum((tiles_cum[None, :] <= tile_block[:, None]).astype(jnp.int32), axis=1)
    n_tiles = tiles_cum[-1:].astype(jnp.int32)
    present = tiles_e > 0
    seg_of_expert = jnp.cumsum(present.astype(jnp.int32)) - 1
    later = jnp.where(present, jnp.arange(N_EXPERTS, dtype=jnp.int32), N_EXPERTS)
    next_present = jnp.concatenate([lax.cummin(later, reverse=True)[1:], jnp.full((1,), N_EXPERTS, jnp.int32)])
    next_present = jnp.where(next_present < N_EXPERTS, next_present, -1)
    tile_segment = seg_of_expert[tile_expert].astype(jnp.int32)
    next_expert = next_present[tile_expert].astype(jnp.int32)

    dest = jnp.transpose(_plan(topi, seg_off), (1, 0, 2)).reshape(TOP_K * t)
    j = jnp.arange(MOE_TILE, dtype=jnp.int32)[None, :]
    n_pad = (tiles_e * MOE_TILE - cnt)[:, None]
    spare = (n_tiles_max - 1) * MOE_TILE + j
    zero_idx = jnp.where(j < n_pad, (seg_off + cnt)[:, None] + j, spare).reshape(N_EXPERTS * MOE_TILE)
    xs = _scatter_rows(h2p, dest, zero_idx.astype(jnp.int32), n_tiles_max * MOE_TILE)
    ys = _experts(xs, tile_expert, tile_block, n_tiles, tile_segment, next_expert, wg, wu, wd)
    gathered = _gather_rows(ys, dest)
    out = _combine(gathered.reshape(TOP_K, t, d // 2), topw, h2p, x1.reshape(t, d), g2, sg, su, sd,
                   ln2_g, ln2_b, alpha)
    return out.reshape(b, s, d)


def kernel(x, c, ctx, c_ctx, w_ada, b_ada, w_in, hg_lb_fwd, hg_lb_bwd, hg_norm_g, na_rpb, w_branch_a, w_branch_b, w_out, ln1_g, ln1_b, w_router, router_bias, w_e_gate, w_e_up, w_e_down, w_sh_gate, w_sh_up, w_sh_down, ln2_g, ln2_b):
    depth = w_ada.shape[0]
    assert depth == 1, "single-layer block"
    b, s, d = x.shape
    alpha = (2.0 * depth) ** 0.25
    l = 0
    lb_fwd = jnp.cumsum(jax.nn.softmax(hg_lb_fwd.astype(F32), axis=0), axis=0)[l]
    lb_bwd = jnp.cumsum(jax.nn.softmax(hg_lb_bwd.astype(F32), axis=0), axis=0)[l]

    cond_rows = jnp.concatenate([c, c_ctx[None, :], jnp.zeros((8 - b - 1, d), F32)], axis=0)
    mod = _ada(cond_rows, w_ada[l], b_ada[l])
    sh1, sc1, g1, sh2, sc2, g2 = [m[:b, None, :] for m in jnp.split(mod, 6, axis=-1)]
    csh1, csc1 = [jnp.broadcast_to(m[b:b + 1, None, :], (b, 1, d)) for m in jnp.split(mod, 6, axis=-1)[:2]]

    w_in_b = w_in[l].astype(BF16)
    p = _inproj(x, sh1, sc1, w_in_b, tuple(range(N_SECTIONS)))
    pc = _inproj(ctx, csh1, csc1, w_in_b, CTX_SECTIONS)

    o_f, o_b = _hgrn(p, pc, lb_fwd, lb_bwd)
    y_na = _natten(p, pc, *_na_tables(na_rpb[l], s))

    x1, h2, topi, topw, cnt = _merge(o_f, o_b, p, y_na, x, g1, sh2, sc2, hg_norm_g[l], ln1_g[l], ln1_b[l],
                                     w_branch_a[l].astype(BF16), w_branch_b[l].astype(BF16),
                                     w_out[l].astype(BF16), w_router[l].T, router_bias[l], alpha)

    return _moe(h2, topi, topw, cnt, x1, g2,
                w_e_gate[l], w_e_up[l], w_e_down[l],
                w_sh_gate[l].astype(BF16), w_sh_up[l].astype(BF16), w_sh_down[l].astype(BF16),
                ln2_g[l], ln2_b[l], alpha)
```

```python
import functools

import numpy as np
import jax
import jax.numpy as jnp
from jax import lax
from jax.experimental import pallas as pl
from jax.experimental.pallas import tpu as pltpu
from jax.experimental.pallas import tpu_sc as plsc

F32 = jnp.float32
BF16 = jnp.bfloat16

D_MODEL = 1024
GRID_W = 64
HG_HEADS = 8
HG_DK = 128
HG_CHUNK = 64
NA_HEADS = 16
NA_HD = 64
NA_WIN_R = 8
NA_WIN_C = 16
ROPE_THETA = 10000.0
NEG_INF = -1e30
N_EXPERTS = 64
EXPERT_DIM = 256
TOP_K = 8
N_GROUPS = 8
TOPK_GROUPS = 4
ROUTED_SCALE = 2.5
LN_EPS = 1e-6
N_SECTIONS = 10
SEC_Q, SEC_FF, SEC_FB, SEC_I, SEC_OG, SEC_NQ, SEC_NK, SEC_NV, SEC_GA, SEC_GB = range(10)
CTX_SECTIONS = (SEC_FF, SEC_FB, SEC_I, SEC_NK, SEC_NV)

VMEM_LIMIT = 56 * 1024 * 1024


def _cparams(sem):
    return pltpu.CompilerParams(dimension_semantics=sem, vmem_limit_bytes=VMEM_LIMIT)


def _normalize(x):
    mu = jnp.mean(x, axis=-1, keepdims=True)
    xc = x - mu
    var = jnp.mean(xc * xc, axis=-1, keepdims=True)
    return xc * lax.rsqrt(var + LN_EPS)


def _silu(x):
    return x * jax.nn.sigmoid(x)


def _dot(a, b):
    return jnp.dot(a, b, preferred_element_type=F32)


def _dot_nt(a, b):
    return lax.dot_general(a, b, (((1,), (1,)), ((), ())), preferred_element_type=F32)


def _dot_tn(a, b):
    return lax.dot_general(a, b, (((0,), (0,)), ((), ())), preferred_element_type=F32)


U32 = jnp.uint32


def _pack_words(x):
    half = x.shape[1] // 2
    lo = lax.bitcast_convert_type(x[:, :half].astype(BF16).astype(F32), U32) >> 16
    hi = lax.bitcast_convert_type(x[:, half:].astype(BF16).astype(F32), U32) & jnp.uint32(0xFFFF0000)
    return lo | hi


def _unpack_words(w):
    lo = lax.bitcast_convert_type(w << 16, F32)
    hi = lax.bitcast_convert_type(w & jnp.uint32(0xFFFF0000), F32)
    return jnp.concatenate([lo, hi], axis=-1)


def _split3(x):
    hi = x.astype(BF16)
    r1 = x - hi.astype(F32)
    mid = r1.astype(BF16)
    lo = (r1 - mid.astype(F32)).astype(BF16)
    return hi, mid, lo


def _ada_kernel(c_ref, w_ref, b_ref, o_ref):
    cond = _silu(c_ref[...])
    o_ref[...] = _dot(cond.astype(BF16), w_ref[...].astype(BF16)) + b_ref[...]


def _ada(cond_rows, w_ada, b_ada):
    r, d = cond_rows.shape
    n = w_ada.shape[1]
    tn = 1024
    return pl.pallas_call(
        _ada_kernel,
        grid=(n // tn,),
        in_specs=[pl.BlockSpec((r, d), lambda j: (0, 0)),
                  pl.BlockSpec((d, tn), lambda j: (0, j)),
                  pl.BlockSpec((1, tn), lambda j: (0, j))],
        out_specs=pl.BlockSpec((r, tn), lambda j: (0, j)),
        out_shape=jax.ShapeDtypeStruct((r, n), F32),
        compiler_params=_cparams(("arbitrary",)),
        name="ada",
    )(cond_rows, w_ada, b_ada.reshape(1, n))


INPROJ_TOK = 2048


def _inproj_kernel(x_ref, sh_ref, sc_ref, w_ref, o_ref, h_ref):
    @pl.when(pl.program_id(2) == 0)
    def _():
        h = _normalize(x_ref[...]) * (1.0 + sc_ref[...]) + sh_ref[...]
        h_ref[...] = h.astype(BF16)

    o_ref[...] = _dot(h_ref[...], w_ref[...])


def _inproj(x, shift, scale, w_in_bf16, sections):
    b, s, d = x.shape
    tm = min(INPROJ_TOK, s)
    nj = len(sections)

    def section(j):
        sec = sections[-1]
        for k in range(nj - 2, -1, -1):
            sec = jnp.where(j == k, sections[k], sec)
        return sec

    return pl.pallas_call(
        _inproj_kernel,
        grid=(b, s // tm, nj),
        in_specs=[pl.BlockSpec((None, tm, d), lambda bi, i, j: (bi, i, 0)),
                  pl.BlockSpec((None, 1, d), lambda bi, i, j: (bi, 0, 0)),
                  pl.BlockSpec((None, 1, d), lambda bi, i, j: (bi, 0, 0)),
                  pl.BlockSpec((d, d), lambda bi, i, j: (0, section(j)))],
        out_specs=pl.BlockSpec((None, None, tm, d), lambda bi, i, j: (j, bi, i, 0)),
        out_shape=jax.ShapeDtypeStruct((nj, b, s, d), F32),
        scratch_shapes=[pltpu.VMEM((tm, d), BF16)],
        compiler_params=_cparams(("arbitrary", "arbitrary", "arbitrary")),
        name="inproj",
    )(x, shift, scale, w_in_bf16)


def _hgrn_gates(q, fraw, v, lb, tri_bf16, last_row):
    f = lb + (1.0 - lb) * jax.nn.sigmoid(fraw)
    k = 1.0 - f
    lf = jnp.log(f)
    hi, mid, lo = _split3(lf)
    a = _dot(tri_bf16, hi) + _dot(tri_bf16, mid) + _dot(tri_bf16, lo)
    a_last = a[last_row:last_row + 1, :]
    kd = (k * jnp.exp(a_last - a)).astype(BF16)
    decay = jnp.exp(a_last)
    qa = kb = None
    if q is not None:
        qa = (_silu(q) * jnp.exp(a)).astype(BF16)
        kb = (k * jnp.exp(-a)).astype(BF16)
    return qa, kb, kd, v.astype(BF16), decay


def _hgrn_chunks(chunks, st_ref):
    first = []
    for d, ((qa, kb, kd, vb, decay), keep) in enumerate(chunks):
        for h in range(HG_HEADS):
            sl = slice(h * HG_DK, (h + 1) * HG_DK)
            st = st_ref[d, h]
            if qa is not None:
                first.append((_dot_nt(qa[:, sl], kb[:, sl]), _dot_nt(qa[:, sl], st.astype(BF16))))
            st_ref[d, h] = st * decay[:, sl] + _dot_tn(vb[:, sl], kd[:, sl])
    results = []
    for d, ((qa, kb, kd, vb, decay), keep) in enumerate(chunks):
        if qa is None:
            results.append(None)
            continue
        outs = []
        for h in range(HG_HEADS):
            sl = slice(h * HG_DK, (h + 1) * HG_DK)
            s_qk, o_state = first.pop(0)
            outs.append(_dot(jnp.where(keep, s_qk, 0.0).astype(BF16), vb[:, sl]) + o_state)
        results.append(jnp.concatenate(outs, axis=-1))
    return results


def _hgrn_kernel(qf_ref, ff_ref, if_ref, qb_ref, fb_ref, ib_ref, cff_ref, cfb_ref, ci_ref,
                 lbf_ref, lbb_ref, of_ref, ob_ref, st_ref, *, n_sub, n_ctx_sub):
    n = pl.program_id(1)
    c = HG_CHUNK
    row = lax.broadcasted_iota(jnp.int32, (c, c), 0)
    col = lax.broadcasted_iota(jnp.int32, (c, c), 1)
    keep_f = col <= row
    keep_b = col >= row
    tri_f = keep_f.astype(F32).astype(BF16)
    tri_b = keep_b.astype(F32).astype(BF16)
    lbf = lbf_ref[...]
    lbb = lbb_ref[...]

    @pl.when(n == 0)
    def _():
        st_ref[...] = jnp.zeros_like(st_ref)

        def body(i, carry):
            r0 = pl.multiple_of(i * c, c)
            r1 = pl.multiple_of((n_ctx_sub - 1 - i) * c, c)
            gf = _hgrn_gates(None, cff_ref[pl.ds(r0, c), :], ci_ref[pl.ds(r0, c), :], lbf, tri_f, c - 1)
            gb = _hgrn_gates(None, cfb_ref[pl.ds(r1, c), :], ci_ref[pl.ds(r1, c), :], lbb, tri_b, 0)
            _hgrn_chunks([(gf, keep_f), (gb, keep_b)], st_ref)
            return carry

        lax.fori_loop(0, n_ctx_sub, body, 0)

    @pl.when(n > 0)
    def _():
        def body(i, carry):
            r0 = pl.multiple_of(i * c, c)
            r1 = pl.multiple_of((n_sub - 1 - i) * c, c)
            gf = _hgrn_gates(qf_ref[pl.ds(r0, c), :], ff_ref[pl.ds(r0, c), :], if_ref[pl.ds(r0, c), :],
                             lbf, tri_f, c - 1)
            gb = _hgrn_gates(qb_ref[pl.ds(r1, c), :], fb_ref[pl.ds(r1, c), :], ib_ref[pl.ds(r1, c), :],
                             lbb, tri_b, 0)
            o_f, o_b = _hgrn_chunks([(gf, keep_f), (gb, keep_b)], st_ref)
            of_ref[pl.ds(r0, c), :] = o_f
            ob_ref[pl.ds(r1, c), :] = o_b
            return carry

        lax.fori_loop(0, n_sub, body, 0, unroll=True)


def _hgrn(p, pc, lb_fwd, lb_bwd):
    _, b, s, w = p.shape
    ctx_len = pc.shape[2]
    tb = min(256, s)
    nb = s // tb
    fwd = lambda bi, n: jnp.maximum(n - 1, 0)
    bwd = lambda bi, n: nb - 1 - jnp.maximum(n - 1, 0)

    def sec(section, blk):
        return pl.BlockSpec((None, None, tb, w), lambda bi, n: (section, bi, blk(bi, n), 0))

    def csec(section):
        return pl.BlockSpec((None, None, ctx_len, w), lambda bi, n: (CTX_SECTIONS.index(section), bi, 0, 0))

    vec = pl.BlockSpec((1, w), lambda bi, n: (0, 0))
    kern = functools.partial(_hgrn_kernel, n_sub=tb // HG_CHUNK, n_ctx_sub=ctx_len // HG_CHUNK)
    return pl.pallas_call(
        kern,
        grid=(b, nb + 1),
        in_specs=[sec(SEC_Q, fwd), sec(SEC_FF, fwd), sec(SEC_I, fwd),
                  sec(SEC_Q, bwd), sec(SEC_FB, bwd), sec(SEC_I, bwd),
                  csec(SEC_FF), csec(SEC_FB), csec(SEC_I), vec, vec],
        out_specs=[pl.BlockSpec((None, tb, w), lambda bi, n: (bi, fwd(bi, n), 0)),
                   pl.BlockSpec((None, tb, w), lambda bi, n: (bi, bwd(bi, n), 0))],
        out_shape=[jax.ShapeDtypeStruct((b, s, w), F32), jax.ShapeDtypeStruct((b, s, w), F32)],
        scratch_shapes=[pltpu.VMEM((2, HG_HEADS, HG_DK, HG_DK), F32)],
        compiler_params=_cparams(("arbitrary", "arbitrary")),
        name="hgrn",
    )(p, p, p, p, p, p, pc, pc, pc, lb_fwd.reshape(1, w), lb_bwd.reshape(1, w))


NA_ROWS_PER_STEP = 32
NA_PREP_ROWS = 512
NA_KEY_TILE = 128
NA_SPAN = NA_WIN_R * GRID_W


def _rope(t, cos, sin_signed, first_half):
    w = t.shape[-1]
    partner = jnp.where(first_half, pltpu.roll(t, w - 16, 1), pltpu.roll(t, 16, 1))
    return t * cos + partner * sin_signed


def _fold_lanes(op, *arrays):
    tiles = [a[:, c:c + 128] for a in arrays for c in range(0, a.shape[-1], 128)]
    acc = tiles[0]
    for t in tiles[1:]:
        acc = op(acc, t)
    return acc


def _rope_tables(rowtab_ref, coltab_ref, row0, n_rows, row_lane):
    out = []
    for i in range(2):
        rt = rowtab_ref[i, pl.ds(row0, n_rows), :]
        by_row = jnp.concatenate([jnp.broadcast_to(rt[r:r + 1, :], (GRID_W, rt.shape[1])) for r in range(n_rows)],
                                 axis=0)
        by_col = jnp.concatenate([coltab_ref[i]] * n_rows, axis=0)
        out.append(jnp.where(row_lane, by_row, by_col))
    return out


def _natten_kernel(q_ref, k_ref, v_ref, kc_ref, vc_ref, rowtab_ref, coltab_ref, t2_ref, o_ref,
                   kt_s, v_s, kc_s, vc_s, bias_s, tail_s, *, rows):
    rblk = pl.program_id(2)
    hd = NA_HD
    lane = lax.broadcasted_iota(jnp.int32, (1, 2 * hd), 1)
    first_half = (lane % 32) < 16
    row_lane = (lane % hd) < hd // 2
    scale = NA_HD ** -0.5

    ones_lane = (hd, 0)

    def values_and_ones(v_pair, h):
        mine = (lane < hd) if h == 0 else (lane >= hd)
        return jnp.where(mine, v_pair, jnp.where(lane == ones_lane[h], 1.0, 0.0)).astype(BF16)

    @pl.when(rblk == 0)
    def _():
        kc = kc_ref[...].astype(BF16)
        qi = lax.broadcasted_iota(jnp.int32, (GRID_W, GRID_W), 0)
        ki = lax.broadcasted_iota(jnp.int32, (GRID_W, GRID_W), 1)
        cstart = jnp.clip(qi - NA_WIN_C // 2, 0, GRID_W - NA_WIN_C)
        in_win = (ki >= cstart) & (ki < cstart + NA_WIN_C)
        s_len = k_ref.shape[0]
        tail_s[...] = jnp.zeros_like(tail_s)
        for h in range(2):
            sl = slice(h * hd, (h + 1) * hd)
            kc_s[h] = kc[:, sl]
            vc_s[h] = values_and_ones(vc_ref[...], h)
            tiles = [jnp.where(in_win, t2_ref[h, dr], NEG_INF) for dr in range(2 * NA_WIN_R - 1)]
            for v in range(NA_WIN_R):
                for j in range(NA_WIN_R):
                    bias_s[h, v, :, j * GRID_W:(j + 1) * GRID_W] = tiles[NA_WIN_R - 1 - v + j]

        eye = (lax.broadcasted_iota(jnp.int32, (2 * hd, 2 * hd), 0)
               == lax.broadcasted_iota(jnp.int32, (2 * hd, 2 * hd), 1)).astype(F32).astype(BF16)

        def prep(i, carry):
            r0 = pl.multiple_of(i * NA_PREP_ROWS, NA_PREP_ROWS)
            rws = pl.ds(r0, NA_PREP_ROWS)
            cos, sin = _rope_tables(rowtab_ref, coltab_ref, i * (NA_PREP_ROWS // GRID_W), NA_PREP_ROWS // GRID_W,
                                    row_lane)
            kr = _rope(k_ref[rws, :], cos, sin, first_half)
            kr_odd = jnp.concatenate([tail_s[...], kr[:NA_PREP_ROWS - GRID_W]], axis=0)
            tail_s[...] = kr[NA_PREP_ROWS - GRID_W:]
            krt = [_dot_nt(eye, kr.astype(BF16)).astype(BF16),
                   _dot_nt(eye, kr_odd.astype(BF16)).astype(BF16)]
            vv = v_ref[rws, :]
            for h in range(2):
                sl = slice(h * hd, (h + 1) * hd)
                for par in range(2):
                    for c in range(NA_PREP_ROWS // NA_KEY_TILE):
                        kt_s[h, par, i * (NA_PREP_ROWS // NA_KEY_TILE) + c] = (
                            krt[par][sl, c * NA_KEY_TILE:(c + 1) * NA_KEY_TILE])
                v_s[h, rws, :] = values_and_ones(vv, h)
            return carry

        lax.fori_loop(0, s_len // NA_PREP_ROWS, prep, 0, unroll=4)

    tq = NA_ROWS_PER_STEP * GRID_W
    q = q_ref[...] * scale
    cos, sin = _rope_tables(rowtab_ref, coltab_ref, rblk * NA_ROWS_PER_STEP, NA_ROWS_PER_STEP, row_lane)
    qr = _rope(q, cos, sin, first_half)
    qb = q.astype(BF16)
    qrb = qr.astype(BF16)
    rws = [slice(rr * GRID_W, (rr + 1) * GRID_W) for rr in range(NA_ROWS_PER_STEP)]
    par, slot0, key0, bidx = [], [], [], []
    for rr in range(NA_ROWS_PER_STEP):
        r = rblk * NA_ROWS_PER_STEP + rr
        rs = jnp.clip(r - NA_WIN_R // 2, 0, rows - NA_WIN_R)
        par.append(rs & 1)
        slot0.append(lax.shift_right_logical(rs, 1) + (rs & 1))
        key0.append(pl.multiple_of(rs * GRID_W, GRID_W))
        bidx.append(r - rs)

    def scores(h):
        sl = slice(h * hd, (h + 1) * hd)
        qrb_h = qrb[:, sl]
        s_ctx_all = _dot_nt(qb[:, sl], kc_s[h])
        s_win = []
        for rr in range(NA_ROWS_PER_STEP):
            kt = kt_s[h, par[rr], pl.ds(slot0[rr], NA_SPAN // NA_KEY_TILE)]
            kt = jnp.concatenate([kt[c] for c in range(NA_SPAN // NA_KEY_TILE)], axis=-1)
            s_win.append(_dot(qrb_h[rws[rr]], kt))
        return s_win, s_ctx_all

    def softmax(h, s_win, s_ctx_all):
        e_win, e_ctx = [], []
        for rr in range(NA_ROWS_PER_STEP):
            sw = s_win[rr] + bias_s[h, bidx[rr]]
            sc = s_ctx_all[rws[rr]]
            m = jnp.max(_fold_lanes(jnp.maximum, sw, sc), axis=-1, keepdims=True)
            e_win.append(jnp.exp(sw - m).astype(BF16))
            e_ctx.append(jnp.exp(sc - m).astype(BF16))
        return e_win, e_ctx

    def values(h, e_win, e_ctx):
        o_win = []
        for rr in range(NA_ROWS_PER_STEP):
            o_win.append(_dot(e_win[rr], v_s[h, pl.ds(key0[rr], NA_SPAN), :]))
        o = jnp.concatenate(o_win, axis=0) + _dot(jnp.concatenate(e_ctx, axis=0), vc_s[h])
        return o * (1.0 / o[:, ones_lane[h]:ones_lane[h] + 1])

    s0 = scores(0)
    s1 = scores(1)
    p0 = softmax(0, *s0)
    o0 = values(0, *p0)
    p1 = softmax(1, *s1)
    o1 = values(1, *p1)
    o_ref[...] = jnp.where(lane < hd, o0, o1).astype(o_ref.dtype)


def _na_tables(rpb, s):
    half = NA_HD // 2
    inv = jnp.power(ROPE_THETA, -jnp.arange(0, half, 2, dtype=F32) / half)

    def tables(n):
        ang = jnp.arange(n, dtype=F32)[:, None] * inv[None, :]
        reps = 2 * NA_HD // half
        return jnp.stack([jnp.tile(jnp.cos(ang), (1, 2 * reps)),
                          jnp.tile(jnp.concatenate([-jnp.sin(ang), jnp.sin(ang)], axis=-1), (1, reps))])

    rowtab, coltab = tables(s // GRID_W), tables(GRID_W)

    pad = GRID_W - NA_WIN_C
    width = 2 * GRID_W
    rp = jnp.pad(rpb.astype(F32), ((0, 0), (0, 0), (pad, pad + 2)), mode="edge")
    skew = jnp.tile(rp, (1, 1, GRID_W + 1))[:, :, GRID_W - 1:GRID_W - 1 + GRID_W * width]
    t2 = skew.reshape(rp.shape[0], rp.shape[1], GRID_W, width)[:, :, :, :GRID_W]
    return rowtab, coltab, t2


def _natten(p, pc, rowtab, coltab, t2):
    _, b, s, w = p.shape
    ctx_len = pc.shape[2]
    rows = s // GRID_W
    assert rows >= NA_WIN_R and rows % NA_ROWS_PER_STEP == 0
    tq = NA_ROWS_PER_STEP * GRID_W
    hw = 2 * NA_HD
    nhp = w // hw
    kern = functools.partial(_natten_kernel, rows=rows)
    return pl.pallas_call(
        kern,
        grid=(b, nhp, rows // NA_ROWS_PER_STEP),
        in_specs=[pl.BlockSpec((None, None, tq, hw), lambda bi, hp, r: (SEC_NQ, bi, r, hp)),
                  pl.BlockSpec((None, None, s, hw), lambda bi, hp, r: (SEC_NK, bi, 0, hp)),
                  pl.BlockSpec((None, None, s, hw), lambda bi, hp, r: (SEC_NV, bi, 0, hp)),
                  pl.BlockSpec((None, None, ctx_len, hw), lambda bi, hp, r: (CTX_SECTIONS.index(SEC_NK), bi, 0, hp)),
                  pl.BlockSpec((None, None, ctx_len, hw), lambda bi, hp, r: (CTX_SECTIONS.index(SEC_NV), bi, 0, hp)),
                  pl.BlockSpec((2, rows, hw), lambda bi, hp, r: (0, 0, 0)),
                  pl.BlockSpec((2, GRID_W, hw), lambda bi, hp, r: (0, 0, 0)),
                  pl.BlockSpec((2, 2 * NA_WIN_R - 1, GRID_W, GRID_W), lambda bi, hp, r: (hp, 0, 0, 0))],
        out_specs=pl.BlockSpec((None, tq, hw), lambda bi, hp, r: (bi, r, hp)),
        out_shape=jax.ShapeDtypeStruct((b, s, w), BF16),
        scratch_shapes=[pltpu.VMEM((2, 2, s // NA_KEY_TILE, NA_HD, NA_KEY_TILE), BF16),
                        pltpu.VMEM((2, s, hw), BF16),
                        pltpu.VMEM((2, ctx_len, NA_HD), BF16), pltpu.VMEM((2, ctx_len, hw), BF16),
                        pltpu.VMEM((2, NA_WIN_R, GRID_W, NA_SPAN), F32),
                        pltpu.VMEM((GRID_W, hw), F32)],
        compiler_params=_cparams(("arbitrary", "arbitrary", "arbitrary")),
        name="natten",
    )(p, p, p, pc, pc, rowtab, coltab, t2)


def _route(logits_t, rbias):
    e, t = logits_t.shape
    gsz = e // N_GROUPS
    scores = jax.nn.sigmoid(logits_t)
    sel = scores + rbias
    neg = -jnp.inf
    sub = lax.broadcasted_iota(jnp.int32, (gsz, t), 0).astype(F32)
    gscore = []
    for g in range(N_GROUPS):
        grp = sel[g * gsz:(g + 1) * gsz, :]
        m1 = jnp.max(grp, axis=0, keepdims=True)
        first = jnp.min(jnp.where(grp == m1, sub, float(gsz)), axis=0, keepdims=True)
        m2 = jnp.max(jnp.where(sub == first, neg, grp), axis=0, keepdims=True)
        gscore.append(m1 + m2)
    masked = []
    for g in range(N_GROUPS):
        rank = jnp.zeros((1, t), F32)
        for g2 in range(N_GROUPS):
            if g2 == g:
                continue
            if g2 < g:
                ahead = gscore[g2] >= gscore[g]
            else:
                ahead = gscore[g2] > gscore[g]
            rank = rank + jnp.where(ahead, 1.0, 0.0)
        masked.append(jnp.where(rank < TOPK_GROUPS, sel[g * gsz:(g + 1) * gsz, :], neg))
    work = jnp.concatenate(masked, axis=0)
    eidx = lax.broadcasted_iota(jnp.int32, (e, t), 0).astype(F32)
    idxs, ws = [], []
    chosen = jnp.zeros((e, t), F32)
    for _ in range(TOP_K):
        m = jnp.max(work, axis=0, keepdims=True)
        first = jnp.min(jnp.where(work == m, eidx, float(e)), axis=0, keepdims=True)
        pick = eidx == first
        idxs.append(first)
        ws.append(jnp.sum(jnp.where(pick, scores, 0.0), axis=0, keepdims=True))
        chosen = jnp.where(pick, 1.0, chosen)
        work = jnp.where(pick, neg, work)
    w = jnp.concatenate(ws, axis=0)
    w = w / jnp.sum(w, axis=0, keepdims=True) * ROUTED_SCALE
    return jnp.concatenate(idxs, axis=0).astype(jnp.int32), w, chosen


MERGE_TOK = 512
MERGE_SUB = 256


def _merge_kernel(of_ref, ob_ref, og_ref, yna_ref, ga_ref, gb_ref, x_ref, g1_ref, sh2_ref, sc2_ref,
                  hgg_ref, ln1g_ref, ln1b_ref, wa_ref, wb_ref, wo_ref, wr_ref, rb_ref,
                  x1_ref, h2_ref, topi_ref, topw_ref, cnt_ref, *, alpha):
    tm = x_ref.shape[0]
    subs = [slice(i * MERGE_SUB, (i + 1) * MERGE_SUB) for i in range(tm // MERGE_SUB)]

    def branches(rows):
        o = of_ref[rows, :] + ob_ref[rows, :]
        parts = []
        for h in range(HG_HEADS):
            oh = o[:, h * HG_DK:(h + 1) * HG_DK]
            parts.append(oh * lax.rsqrt(jnp.mean(oh * oh, axis=-1, keepdims=True) + LN_EPS))
        y_hg = jnp.concatenate(parts, axis=-1) * hgg_ref[...] * _silu(og_ref[rows, :])
        return _dot(y_hg.astype(BF16), wa_ref[...]), _dot(yna_ref[rows, :], wb_ref[...])

    def out_proj(rows, ya, yb):
        t = jax.nn.sigmoid(ga_ref[rows, :]) * ya + jax.nn.sigmoid(gb_ref[rows, :]) * yb
        return _dot(t.astype(BF16), wo_ref[...])

    def norms_router(rows, i, y):
        x1 = _normalize(alpha * x_ref[rows, :] + g1_ref[...] * y) * ln1g_ref[...] + ln1b_ref[...]
        x1_ref[rows, :] = x1
        h2 = _normalize(x1) * (1.0 + sc2_ref[...]) + sh2_ref[...]
        h2_ref[rows, :] = _pack_words(h2)
        hh, hm, hl = _split3(h2)
        wh, wm, wl = _split3(wr_ref[...])
        return (_dot_nt(wh, hh) + _dot_nt(wh, hm) + _dot_nt(wm, hh)
                + _dot_nt(wh, hl) + _dot_nt(wl, hh) + _dot_nt(wm, hm))

    ab = [branches(rows) for rows in subs]
    ys = [out_proj(rows, *ab[i]) for i, rows in enumerate(subs)]
    logits = [norms_router(rows, i, ys[i]) for i, rows in enumerate(subs)]

    @pl.when((pl.program_id(0) == 0) & (pl.program_id(1) == 0))
    def _():
        cnt_ref[...] = jnp.zeros_like(cnt_ref)

    for i, rows in enumerate(subs):
        topi, topw, chosen = _route(logits[i], rb_ref[...])
        topi_ref[:, rows] = topi
        topw_ref[:, rows] = topw
        cnt_ref[...] += jnp.sum(chosen, axis=1, keepdims=True)


def _merge(o_f, o_b, p, y_na, x, g1, sh2, sc2, hg_norm_g, ln1_g, ln1_b, w_a, w_b, w_o, w_router_t, router_bias,
           alpha):
    b, s, d = x.shape
    tm = min(MERGE_TOK, s)
    e = w_router_t.shape[0]
    tok = lambda bi, i: (bi, i, 0)
    blk = pl.BlockSpec((None, tm, d), tok)

    def sec(section):
        return pl.BlockSpec((None, None, tm, d), lambda bi, i: (section, bi, i, 0))

    mod = pl.BlockSpec((None, 1, d), lambda bi, i: (bi, 0, 0))
    vec = pl.BlockSpec((1, d), lambda bi, i: (0, 0))
    mat = pl.BlockSpec((d, d), lambda bi, i: (0, 0), pipeline_mode=pl.Buffered(1))
    return pl.pallas_call(
        functools.partial(_merge_kernel, alpha=alpha),
        grid=(b, s // tm),
        in_specs=[blk, blk, sec(SEC_OG), blk, sec(SEC_GA), sec(SEC_GB), blk, mod, mod, mod,
                  vec, vec, vec, mat, mat, mat,
                  pl.BlockSpec((e, d), lambda bi, i: (0, 0)),
                  pl.BlockSpec((e, 1), lambda bi, i: (0, 0))],
        out_specs=[blk,
                   pl.BlockSpec((tm, d // 2), lambda bi, i: (bi * (s // tm) + i, 0)),
                   pl.BlockSpec((None, TOP_K, tm), lambda bi, i: (bi, 0, i)),
                   pl.BlockSpec((None, TOP_K, tm), lambda bi, i: (bi, 0, i)),
                   pl.BlockSpec((e, 128), lambda bi, i: (0, 0))],
        out_shape=[jax.ShapeDtypeStruct((b, s, d), F32),
                   jax.ShapeDtypeStruct((b * s, d // 2), U32),
                   jax.ShapeDtypeStruct((b, TOP_K, s), jnp.int32), jax.ShapeDtypeStruct((b, TOP_K, s), F32),
                   jax.ShapeDtypeStruct((e, 128), F32)],
        compiler_params=_cparams(("arbitrary", "arbitrary")),
        name="merge",
    )(o_f, o_b, p, y_na, p, p, x, g1, sh2, sc2, hg_norm_g.reshape(1, d), ln1_g.reshape(1, d),
      ln1_b.reshape(1, d), w_a, w_b, w_o, w_router_t, router_bias.reshape(e, 1))


MOE_TILE = 512
MOE_TOK = 512


def _plan_kernel(topi_ref, off_ref, dest_ref, carry_ref):
    @pl.when(pl.program_id(0) == 0)
    def _():
        carry_ref[...] = jnp.zeros_like(carry_ref)

    topi = topi_ref[...]
    tok = topi.shape[1]
    eidx = lax.broadcasted_iota(jnp.int32, (N_EXPERTS, tok), 0)
    hits = [eidx == topi[k:k + 1, :] for k in range(TOP_K)]
    m = jnp.zeros((N_EXPERTS, tok), F32)
    for hit in hits:
        m = jnp.where(hit, 1.0, m)
    before = (lax.broadcasted_iota(jnp.int32, (tok, tok), 0)
              < lax.broadcasted_iota(jnp.int32, (tok, tok), 1)).astype(F32).astype(BF16)
    row = off_ref[...] + carry_ref[...] + _dot(m.astype(BF16), before)
    dest = [jnp.sum(jnp.where(hit, row, 0.0), axis=0, keepdims=True) for hit in hits]
    dest_ref[...] = jnp.concatenate(dest, axis=0).astype(jnp.int32)
    carry_ref[...] += jnp.sum(m, axis=1, keepdims=True)


def _plan(topi, seg_off):
    b, k, s = topi.shape
    per_b = s // MOE_TOK
    blk = pl.BlockSpec((None, k, MOE_TOK), lambda i: (i // per_b, 0, i % per_b))
    return pl.pallas_call(
        _plan_kernel,
        grid=(b * per_b,),
        in_specs=[blk, pl.BlockSpec((N_EXPERTS, 1), lambda i: (0, 0))],
        out_specs=blk,
        out_shape=jax.ShapeDtypeStruct((b, k, s), jnp.int32),
        scratch_shapes=[pltpu.VMEM((N_EXPERTS, 1), F32)],
        compiler_params=_cparams(("arbitrary",)),
        name="plan",
    )(topi, seg_off.astype(F32).reshape(N_EXPERTS, 1))


SC_WINDOW = 128


def _sc_workers():
    info = plsc.get_sparse_core_info()
    return info.num_cores, info.num_cores * info.num_subcores


def _scatter_rows(src, idx, zero_idx, n_rows):
    t, w = src.shape
    m, mz = idx.shape[0], zero_idx.shape[0]
    n_cores, n_workers = _sc_workers()
    per_worker, per_worker_z = t // n_workers, mz // n_workers
    assert m % t == 0
    assert per_worker * n_workers == t and per_worker % SC_WINDOW == 0
    assert per_worker_z * n_workers == mz and per_worker_z % SC_WINDOW == 0
    mesh = plsc.VectorSubcoreMesh(core_axis_name="core", subcore_axis_name="subcore")

    @functools.partial(
        pl.kernel, mesh=mesh, out_type=jax.ShapeDtypeStruct((n_rows, w), src.dtype),
        scratch_types=[pltpu.VMEM((SC_WINDOW,), jnp.int32), pltpu.VMEM((SC_WINDOW, w), src.dtype),
                       pltpu.SemaphoreType.DMA])
    def scatter(src_hbm, idx_hbm, zeros_hbm, zero_idx_hbm, out_hbm, idx_v, rows_v, sem):
        worker = lax.axis_index("subcore") * n_cores + lax.axis_index("core")

        @pl.loop(0, per_worker // SC_WINDOW)
        def _(step):
            first = pl.multiple_of(worker * per_worker + step * SC_WINDOW, SC_WINDOW)
            pltpu.sync_copy(src_hbm.at[pl.ds(first, SC_WINDOW)], rows_v)
            for copy in range(m // t):
                pltpu.sync_copy(idx_hbm.at[pl.ds(copy * t + first, SC_WINDOW)], idx_v)
                pltpu.async_copy(rows_v, out_hbm.at[idx_v], sem).wait()

        pltpu.sync_copy(zeros_hbm, rows_v)

        @pl.loop(0, per_worker_z // SC_WINDOW)
        def _(step):
            base = pl.multiple_of(worker * per_worker_z + step * SC_WINDOW, SC_WINDOW)
            pltpu.sync_copy(zero_idx_hbm.at[pl.ds(base, SC_WINDOW)], idx_v)
            pltpu.async_copy(rows_v, out_hbm.at[idx_v], sem).wait()

    return scatter(src, idx, jnp.zeros((SC_WINDOW, w), src.dtype), zero_idx)


EXPERT_RING = 3


def _experts_kernel(te_ref, tb_ref, nt_ref, seg_ref, nxt_ref, xs_ref, wg_ref, wu_ref, wd_ref, ys_ref,
                    xbuf, sem, wg_f32, wu_f32, wd_f32, wg_s, wu_s, wd_s, wsem):
    i = pl.program_id(0)
    n_tiles = nt_ref[0]

    def weight_copies(expert, slot):
        return [pltpu.make_async_copy(src.at[expert], dst.at[slot], wsem.at[slot, n])
                for n, (src, dst) in enumerate(((wg_ref, wg_f32), (wu_ref, wu_f32), (wd_ref, wd_f32)))]

    @pl.when(i == 0)
    def _():
        for c in weight_copies(te_ref[0], 0):
            c.start()

    first = (i == 0) | (seg_ref[i] != seg_ref[jnp.maximum(i - 1, 0)])

    @pl.when((i < n_tiles) & first)
    def _():
        slot = seg_ref[i] & 1
        for c in weight_copies(te_ref[i], slot):
            c.wait()
        wg_s[...] = wg_f32[slot].astype(BF16)
        wu_s[...] = wu_f32[slot].astype(BF16)
        wd_s[...] = wd_f32[slot].astype(BF16)

        @pl.when(nxt_ref[i] >= 0)
        def _():
            for c in weight_copies(nxt_ref[i], 1 - slot):
                c.start()

    def tile_copy(j):
        slot = lax.rem(j, EXPERT_RING)
        row0 = pl.multiple_of(tb_ref[j] * MOE_TILE, MOE_TILE)
        return pltpu.make_async_copy(xs_ref.at[pl.ds(row0, MOE_TILE), :], xbuf.at[slot], sem.at[slot])

    @pl.when(i == 0)
    def _():
        for j in range(EXPERT_RING - 1):
            @pl.when(j < n_tiles)
            def _():
                tile_copy(j).start()

    ahead = i + (EXPERT_RING - 1)

    @pl.when(ahead < n_tiles)
    def _():
        tile_copy(ahead).start()

    @pl.when(i < n_tiles)
    def _():
        tile_copy(i).wait()
        x = _unpack_words(xbuf[lax.rem(i, EXPERT_RING)]).astype(BF16)
        act = _silu(_dot(x, wg_s[...])) * _dot(x, wu_s[...])
        ys_ref[...] = _pack_words(_dot(act.astype(BF16), wd_s[...]))


def _experts(xs, tile_expert, tile_block, n_tiles, tile_segment, next_expert, wg, wu, wd):
    d, f = wg.shape[1], wg.shape[2]
    anywhere = pl.BlockSpec(memory_space=pl.ANY)
    grid_spec = pltpu.PrefetchScalarGridSpec(
        num_scalar_prefetch=5,
        grid=(xs.shape[0] // MOE_TILE,),
        in_specs=[anywhere, anywhere, anywhere, anywhere],
        out_specs=pl.BlockSpec((MOE_TILE, d // 2), lambda i, te, tb, nt, seg, nxt: (tb[i], 0)),
        scratch_shapes=[pltpu.VMEM((EXPERT_RING, MOE_TILE, d // 2), U32), pltpu.SemaphoreType.DMA((EXPERT_RING,)),
                        pltpu.VMEM((2, d, f), F32), pltpu.VMEM((2, d, f), F32), pltpu.VMEM((2, f, d), F32),
                        pltpu.VMEM((d, f), BF16), pltpu.VMEM((d, f), BF16), pltpu.VMEM((f, d), BF16),
                        pltpu.SemaphoreType.DMA((2, 3))],
    )
    return pl.pallas_call(
        _experts_kernel,
        grid_spec=grid_spec,
        out_shape=jax.ShapeDtypeStruct(xs.shape, U32),
        compiler_params=_cparams(("arbitrary",)),
        name="experts",
    )(tile_expert, tile_block, n_tiles, tile_segment, next_expert, xs, wg, wu, wd)


def _gather_rows(table, idx):
    m = idx.shape[0]
    w = table.shape[1]
    win = SC_WINDOW // 2
    n_cores, n_workers = _sc_workers()
    per_worker = m // n_workers
    n_pairs = per_worker // (2 * win)
    assert per_worker * n_workers == m and n_pairs * 2 * win == per_worker
    mesh = plsc.VectorSubcoreMesh(core_axis_name="core", subcore_axis_name="subcore")

    @functools.partial(
        pl.kernel, mesh=mesh, out_type=jax.ShapeDtypeStruct((m, w), table.dtype),
        scratch_types=[pltpu.VMEM((win,), jnp.int32), pltpu.VMEM((win,), jnp.int32),
                       pltpu.VMEM((win, w), table.dtype), pltpu.VMEM((win, w), table.dtype),
                       pltpu.SemaphoreType.DMA, pltpu.SemaphoreType.DMA])
    def gather(table_hbm, idx_hbm, out_hbm, idx_a, idx_b, rows_a, rows_b, sem_a, sem_b):
        worker = lax.axis_index("subcore") * n_cores + lax.axis_index("core")
        start = worker * per_worker

        def request(first, idx_v, rows_v, sem):
            pltpu.sync_copy(idx_hbm.at[pl.ds(first, win)], idx_v)
            pltpu.async_copy(table_hbm.at[idx_v], rows_v, sem)

        def deliver(first, idx_v, rows_v, sem):
            pltpu.make_async_copy(table_hbm.at[idx_v], rows_v, sem).wait()
            pltpu.sync_copy(rows_v, out_hbm.at[pl.ds(first, win)])

        request(pl.multiple_of(start, win), idx_a, rows_a, sem_a)

        @pl.loop(0, n_pairs)
        def _(pair):
            first_a = pl.multiple_of(start + pair * 2 * win, win)
            first_b = pl.multiple_of(first_a + win, win)
            request(first_b, idx_b, rows_b, sem_b)
            deliver(first_a, idx_a, rows_a, sem_a)

            @pl.when(pair + 1 < n_pairs)
            def _():
                request(pl.multiple_of(first_b + win, win), idx_a, rows_a, sem_a)

            deliver(first_b, idx_b, rows_b, sem_b)

    return gather(table, idx)


def _combine_kernel(rows_ref, topw_ref, h_ref, x1_ref, g2_ref, sg_ref, su_ref, sd_ref, ln2g_ref, ln2b_ref,
                    o_ref, *, alpha):
    h = _unpack_words(h_ref[...]).astype(BF16)
    act = _silu(_dot(h, sg_ref[...])) * _dot(h, su_ref[...])
    y = _dot(act.astype(BF16), sd_ref[...])
    w = topw_ref[...].T
    for k in range(TOP_K):
        y = y + w[:, k:k + 1] * _unpack_words(rows_ref[k])
    o_ref[...] = _normalize(alpha * x1_ref[...] + g2_ref[...] * y) * ln2g_ref[...] + ln2b_ref[...]


def _combine(gathered, topw, h2p, x1, g2, sg, su, sd, ln2_g, ln2_b, alpha):
    t, d = x1.shape
    b, k, s = topw.shape
    per_b = s // MOE_TOK
    fs = sg.shape[1]
    rows = pl.BlockSpec((MOE_TOK, d), lambda i: (i, 0))
    packed = pl.BlockSpec((MOE_TOK, d // 2), lambda i: (i, 0))
    vec = pl.BlockSpec((1, d), lambda i: (0, 0))
    return pl.pallas_call(
        functools.partial(_combine_kernel, alpha=alpha),
        grid=(t // MOE_TOK,),
        in_specs=[pl.BlockSpec((k, MOE_TOK, d // 2), lambda i: (0, i, 0)),
                  pl.BlockSpec((None, k, MOE_TOK), lambda i: (i // per_b, 0, i % per_b)),
                  packed, rows,
                  pl.BlockSpec((None, 1, d), lambda i: (i // per_b, 0, 0)),
                  pl.BlockSpec((d, fs), lambda i: (0, 0)),
                  pl.BlockSpec((d, fs), lambda i: (0, 0)),
                  pl.BlockSpec((fs, d), lambda i: (0, 0)),
                  vec, vec],
        out_specs=rows,
        out_shape=jax.ShapeDtypeStruct((t, d), F32),
        compiler_params=_cparams(("arbitrary",)),
        name="combine",
    )(gathered, topw, h2p, x1, g2, sg, su, sd, ln2_g.reshape(1, d), ln2_b.reshape(1, d))


def _moe(h2p, topi, topw, cnt, x1, g2, wg, wu, wd, sg, su, sd, ln2_g, ln2_b, alpha):
    b, s, d = x1.shape
    t = b * s
    cnt = cnt[:, 0].astype(jnp.int32)
    tiles_e = (cnt + (MOE_TILE - 1)) // MOE_TILE
    tiles_cum = jnp.cumsum(tiles_e)
    seg_off = (tiles_cum - tiles_e) * MOE_TILE
    n_tiles_max = t * TOP_K // MOE_TILE + N_EXPERTS
    tile_block = jnp.minimum(jnp.arange(n_tiles_max, dtype=jnp.int32), tiles_cum[-1] - 1)
    tile_expert = jnp.sum((tiles_cum[None, :] <= tile_block[:, None]).astype(jnp.int32), axis=1)
    n_tiles = tiles_cum[-1:].astype(jnp.int32)
    present = tiles_e > 0
    seg_of_expert = jnp.cumsum(present.astype(jnp.int32)) - 1
    later = jnp.where(present, jnp.arange(N_EXPERTS, dtype=jnp.int32), N_EXPERTS)
    next_present = jnp.concatenate([lax.cummin(later, reverse=True)[1:], jnp.full((1,), N_EXPERTS, jnp.int32)])
    next_present = jnp.where(next_present < N_EXPERTS, next_present, -1)
    tile_segment = seg_of_expert[tile_expert].astype(jnp.int32)
    next_expert = next_present[tile_expert].astype(jnp.int32)

    dest = jnp.transpose(_plan(topi, seg_off), (1, 0, 2)).reshape(TOP_K * t)
    j = jnp.arange(MOE_TILE, dtype=jnp.int32)[None, :]
    n_pad = (tiles_e * MOE_TILE - cnt)[:, None]
    spare = (n_tiles_max - 1) * MOE_TILE + j
    zero_idx = jnp.where(j < n_pad, (seg_off + cnt)[:, None] + j, spare).reshape(N_EXPERTS * MOE_TILE)
    xs = _scatter_rows(h2p, dest, zero_idx.astype(jnp.int32), n_tiles_max * MOE_TILE)
    ys = _experts(xs, tile_expert, tile_block, n_tiles, tile_segment, next_expert, wg, wu, wd)
    gathered = _gather_rows(ys, dest)
    out = _combine(gathered.reshape(TOP_K, t, d // 2), topw, h2p, x1.reshape(t, d), g2, sg, su, sd,
                   ln2_g, ln2_b, alpha)
    return out.reshape(b, s, d)


def kernel(x, c, ctx, c_ctx, w_ada, b_ada, w_in, hg_lb_fwd, hg_lb_bwd, hg_norm_g, na_rpb, w_branch_a, w_branch_b, w_out, ln1_g, ln1_b, w_router, router_bias, w_e_gate, w_e_up, w_e_down, w_sh_gate, w_sh_up, w_sh_down, ln2_g, ln2_b):
    depth = w_ada.shape[0]
    assert depth == 1, "single-layer block"
    b, s, d = x.shape
    alpha = (2.0 * depth) ** 0.25
    l = 0
    lb_fwd = jnp.cumsum(jax.nn.softmax(hg_lb_fwd.astype(F32), axis=0), axis=0)[l]
    lb_bwd = jnp.cumsum(jax.nn.softmax(hg_lb_bwd.astype(F32), axis=0), axis=0)[l]

    cond_rows = jnp.concatenate([c, c_ctx[None, :], jnp.zeros((8 - b - 1, d), F32)], axis=0)
    mod = _ada(cond_rows, w_ada[l], b_ada[l])
    sh1, sc1, g1, sh2, sc2, g2 = [m[:b, None, :] for m in jnp.split(mod, 6, axis=-1)]
    csh1, csc1 = [jnp.broadcast_to(m[b:b + 1, None, :], (b, 1, d)) for m in jnp.split(mod, 6, axis=-1)[:2]]

    w_in_b = w_in[l].astype(BF16)
    p = _inproj(x, sh1, sc1, w_in_b, tuple(range(N_SECTIONS)))
    pc = _inproj(ctx, csh1, csc1, w_in_b, CTX_SECTIONS)

    o_f, o_b = _hgrn(p, pc, lb_fwd, lb_bwd)
    y_na = _natten(p, pc, *_na_tables(na_rpb[l], s))

    x1, h2, topi, topw, cnt = _merge(o_f, o_b, p, y_na, x, g1, sh2, sc2, hg_norm_g[l], ln1_g[l], ln1_b[l],
                                     w_branch_a[l].astype(BF16), w_branch_b[l].astype(BF16),
                                     w_out[l].astype(BF16), w_router[l].T, router_bias[l], alpha)

    return _moe(h2, topi, topw, cnt, x1, g2,
                w_e_gate[l], w_e_up[l], w_e_down[l],
                w_sh_gate[l].astype(BF16), w_sh_up[l].astype(BF16), w_sh_down[l].astype(BF16),
                ln2_g[l], ln2_b[l], alpha)
```

```python
import functools

import numpy as np
import jax
import jax.numpy as jnp
from jax import lax
from jax.experimental import pallas as pl
from jax.experimental.pallas import tpu as pltpu
from jax.experimental.pallas import tpu_sc as plsc

F32 = jnp.float32
BF16 = jnp.bfloat16

D_MODEL = 1024
GRID_W = 64
HG_HEADS = 8
HG_DK = 128
HG_CHUNK = 64
NA_HEADS = 16
NA_HD = 64
NA_WIN_R = 8
NA_WIN_C = 16
ROPE_THETA = 10000.0
NEG_INF = -1e30
N_EXPERTS = 64
EXPERT_DIM = 256
TOP_K = 8
N_GROUPS = 8
TOPK_GROUPS = 4
ROUTED_SCALE = 2.5
LN_EPS = 1e-6
N_SECTIONS = 10
SEC_Q, SEC_FF, SEC_FB, SEC_I, SEC_OG, SEC_NQ, SEC_NK, SEC_NV, SEC_GA, SEC_GB = range(10)
CTX_SECTIONS = (SEC_FF, SEC_FB, SEC_I, SEC_NK, SEC_NV)

VMEM_LIMIT = 56 * 1024 * 1024


def _cparams(sem):
    return pltpu.CompilerParams(dimension_semantics=sem, vmem_limit_bytes=VMEM_LIMIT)


def _normalize(x):
    mu = jnp.mean(x, axis=-1, keepdims=True)
    xc = x - mu
    var = jnp.mean(xc * xc, axis=-1, keepdims=True)
    return xc * lax.rsqrt(var + LN_EPS)


def _silu(x):
    return x * jax.nn.sigmoid(x)


def _dot(a, b):
    return jnp.dot(a, b, preferred_element_type=F32)


def _dot_nt(a, b):
    return lax.dot_general(a, b, (((1,), (1,)), ((), ())), preferred_element_type=F32)


def _dot_tn(a, b):
    return lax.dot_general(a, b, (((0,), (0,)), ((), ())), preferred_element_type=F32)


U32 = jnp.uint32


def _pack_words(x):
    half = x.shape[1] // 2
    lo = lax.bitcast_convert_type(x[:, :half].astype(BF16).astype(F32), U32) >> 16
    hi = lax.bitcast_convert_type(x[:, half:].astype(BF16).astype(F32), U32) & jnp.uint32(0xFFFF0000)
    return lo | hi


def _unpack_words(w):
    lo = lax.bitcast_convert_type(w << 16, F32)
    hi = lax.bitcast_convert_type(w & jnp.uint32(0xFFFF0000), F32)
    return jnp.concatenate([lo, hi], axis=-1)


def _split3(x):
    hi = x.astype(BF16)
    r1 = x - hi.astype(F32)
    mid = r1.astype(BF16)
    lo = (r1 - mid.astype(F32)).astype(BF16)
    return hi, mid, lo


def _ada_kernel(c_ref, w_ref, b_ref, o_ref):
    cond = _silu(c_ref[...])
    o_ref[...] = _dot(cond.astype(BF16), w_ref[...].astype(BF16)) + b_ref[...]


def _ada(cond_rows, w_ada, b_ada):
    r, d = cond_rows.shape
    n = w_ada.shape[1]
    tn = 1024
    return pl.pallas_call(
        _ada_kernel,
        grid=(n // tn,),
        in_specs=[pl.BlockSpec((r, d), lambda j: (0, 0)),
                  pl.BlockSpec((d, tn), lambda j: (0, j)),
                  pl.BlockSpec((1, tn), lambda j: (0, j))],
        out_specs=pl.BlockSpec((r, tn), lambda j: (0, j)),
        out_shape=jax.ShapeDtypeStruct((r, n), F32),
        compiler_params=_cparams(("arbitrary",)),
        name="ada",
    )(cond_rows, w_ada, b_ada.reshape(1, n))


INPROJ_TOK = 2048


def _inproj_kernel(x_ref, sh_ref, sc_ref, w_ref, o_ref, h_ref):
    @pl.when(pl.program_id(2) == 0)
    def _():
        h = _normalize(x_ref[...]) * (1.0 + sc_ref[...]) + sh_ref[...]
        h_ref[...] = h.astype(BF16)

    o_ref[...] = _dot(h_ref[...], w_ref[...])


def _inproj(x, shift, scale, w_in_bf16, sections):
    b, s, d = x.shape
    tm = min(INPROJ_TOK, s)
    nj = len(sections)

    def section(j):
        sec = sections[-1]
        for k in range(nj - 2, -1, -1):
            sec = jnp.where(j == k, sections[k], sec)
        return sec

    return pl.pallas_call(
        _inproj_kernel,
        grid=(b, s // tm, nj),
        in_specs=[pl.BlockSpec((None, tm, d), lambda bi, i, j: (bi, i, 0)),
                  pl.BlockSpec((None, 1, d), lambda bi, i, j: (bi, 0, 0)),
                  pl.BlockSpec((None, 1, d), lambda bi, i, j: (bi, 0, 0)),
                  pl.BlockSpec((d, d), lambda bi, i, j: (0, section(j)))],
        out_specs=pl.BlockSpec((None, None, tm, d), lambda bi, i, j: (j, bi, i, 0)),
        out_shape=jax.ShapeDtypeStruct((nj, b, s, d), F32),
        scratch_shapes=[pltpu.VMEM((tm, d), BF16)],
        compiler_params=_cparams(("arbitrary", "arbitrary", "arbitrary")),
        name="inproj",
    )(x, shift, scale, w_in_bf16)


def _hgrn_gates(q, fraw, v, lb, tri_bf16, last_row):
    f = lb + (1.0 - lb) * jax.nn.sigmoid(fraw)
    k = 1.0 - f
    lf = jnp.log(f)
    hi, mid, lo = _split3(lf)
    a = _dot(tri_bf16, hi) + _dot(tri_bf16, mid) + _dot(tri_bf16, lo)
    a_last = a[last_row:last_row + 1, :]
    kd = (k * jnp.exp(a_last - a)).astype(BF16)
    decay = jnp.exp(a_last)
    qa = kb = None
    if q is not None:
        qa = (_silu(q) * jnp.exp(a)).astype(BF16)
        kb = (k * jnp.exp(-a)).astype(BF16)
    return qa, kb, kd, v.astype(BF16), decay


def _hgrn_chunks(chunks, st_ref):
    first = []
    for d, ((qa, kb, kd, vb, decay), keep) in enumerate(chunks):
        for h in range(HG_HEADS):
            sl = slice(h * HG_DK, (h + 1) * HG_DK)
            st = st_ref[d, h]
            if qa is not None:
                first.append((_dot_nt(qa[:, sl], kb[:, sl]), _dot_nt(qa[:, sl], st.astype(BF16))))
            st_ref[d, h] = st * decay[:, sl] + _dot_tn(vb[:, sl], kd[:, sl])
    results = []
    for d, ((qa, kb, kd, vb, decay), keep) in enumerate(chunks):
        if qa is None:
            results.append(None)
            continue
        outs = []
        for h in range(HG_HEADS):
            sl = slice(h * HG_DK, (h + 1) * HG_DK)
            s_qk, o_state = first.pop(0)
            outs.append(_dot(jnp.where(keep, s_qk, 0.0).astype(BF16), vb[:, sl]) + o_state)
        results.append(jnp.concatenate(outs, axis=-1))
    return results


def _hgrn_kernel(qf_ref, ff_ref, if_ref, qb_ref, fb_ref, ib_ref, cff_ref, cfb_ref, ci_ref,
                 lbf_ref, lbb_ref, of_ref, ob_ref, st_ref, *, n_sub, n_ctx_sub):
    n = pl.program_id(1)
    c = HG_CHUNK
    row = lax.broadcasted_iota(jnp.int32, (c, c), 0)
    col = lax.broadcasted_iota(jnp.int32, (c, c), 1)
    keep_f = col <= row
    keep_b = col >= row
    tri_f = keep_f.astype(F32).astype(BF16)
    tri_b = keep_b.astype(F32).astype(BF16)
    lbf = lbf_ref[...]
    lbb = lbb_ref[...]

    @pl.when(n == 0)
    def _():
        st_ref[...] = jnp.zeros_like(st_ref)

        def body(i, carry):
            r0 = pl.multiple_of(i * c, c)
            r1 = pl.multiple_of((n_ctx_sub - 1 - i) * c, c)
            gf = _hgrn_gates(None, cff_ref[pl.ds(r0, c), :], ci_ref[pl.ds(r0, c), :], lbf, tri_f, c - 1)
            gb = _hgrn_gates(None, cfb_ref[pl.ds(r1, c), :], ci_ref[pl.ds(r1, c), :], lbb, tri_b, 0)
            _hgrn_chunks([(gf, keep_f), (gb, keep_b)], st_ref)
            return carry

        lax.fori_loop(0, n_ctx_sub, body, 0)

    @pl.when(n > 0)
    def _():
        def body(i, carry):
            r0 = pl.multiple_of(i * c, c)
            r1 = pl.multiple_of((n_sub - 1 - i) * c, c)
            gf = _hgrn_gates(qf_ref[pl.ds(r0, c), :], ff_ref[pl.ds(r0, c), :], if_ref[pl.ds(r0, c), :],
                             lbf, tri_f, c - 1)
            gb = _hgrn_gates(qb_ref[pl.ds(r1, c), :], fb_ref[pl.ds(r1, c), :], ib_ref[pl.ds(r1, c), :],
                             lbb, tri_b, 0)
            o_f, o_b = _hgrn_chunks([(gf, keep_f), (gb, keep_b)], st_ref)
            of_ref[pl.ds(r0, c), :] = o_f
            ob_ref[pl.ds(r1, c), :] = o_b
            return carry

        lax.fori_loop(0, n_sub, body, 0, unroll=True)


def _hgrn(p, pc, lb_fwd, lb_bwd):
    _, b, s, w = p.shape
    ctx_len = pc.shape[2]
    tb = min(512, s)
    nb = s // tb
    fwd = lambda bi, n: jnp.maximum(n - 1, 0)
    bwd = lambda bi, n: nb - 1 - jnp.maximum(n - 1, 0)

    def sec(section, blk):
        return pl.BlockSpec((None, None, tb, w), lambda bi, n: (section, bi, blk(bi, n), 0))

    def csec(section):
        return pl.BlockSpec((None, None, ctx_len, w), lambda bi, n: (CTX_SECTIONS.index(section), bi, 0, 0))

    vec = pl.BlockSpec((1, w), lambda bi, n: (0, 0))
    kern = functools.partial(_hgrn_kernel, n_sub=tb // HG_CHUNK, n_ctx_sub=ctx_len // HG_CHUNK)
    return pl.pallas_call(
        kern,
        grid=(b, nb + 1),
        in_specs=[sec(SEC_Q, fwd), sec(SEC_FF, fwd), sec(SEC_I, fwd),
                  sec(SEC_Q, bwd), sec(SEC_FB, bwd), sec(SEC_I, bwd),
                  csec(SEC_FF), csec(SEC_FB), csec(SEC_I), vec, vec],
        out_specs=[pl.BlockSpec((None, tb, w), lambda bi, n: (bi, fwd(bi, n), 0)),
                   pl.BlockSpec((None, tb, w), lambda bi, n: (bi, bwd(bi, n), 0))],
        out_shape=[jax.ShapeDtypeStruct((b, s, w), F32), jax.ShapeDtypeStruct((b, s, w), F32)],
        scratch_shapes=[pltpu.VMEM((2, HG_HEADS, HG_DK, HG_DK), F32)],
        compiler_params=_cparams(("arbitrary", "arbitrary")),
        name="hgrn",
    )(p, p, p, p, p, p, pc, pc, pc, lb_fwd.reshape(1, w), lb_bwd.reshape(1, w))


NA_ROWS_PER_STEP = 32
NA_PREP_ROWS = 512
NA_KEY_TILE = 128
NA_SPAN = NA_WIN_R * GRID_W


def _rope(t, cos, sin_signed, first_half):
    w = t.shape[-1]
    partner = jnp.where(first_half, pltpu.roll(t, w - 16, 1), pltpu.roll(t, 16, 1))
    return t * cos + partner * sin_signed


def _fold_lanes(op, *arrays):
    tiles = [a[:, c:c + 128] for a in arrays for c in range(0, a.shape[-1], 128)]
    acc = tiles[0]
    for t in tiles[1:]:
        acc = op(acc, t)
    return acc


def _rope_tables(rowtab_ref, coltab_ref, row0, n_rows, row_lane):
    out = []
    for i in range(2):
        rt = rowtab_ref[i, pl.ds(row0, n_rows), :]
        by_row = jnp.concatenate([jnp.broadcast_to(rt[r:r + 1, :], (GRID_W, rt.shape[1])) for r in range(n_rows)],
                                 axis=0)
        by_col = jnp.concatenate([coltab_ref[i]] * n_rows, axis=0)
        out.append(jnp.where(row_lane, by_row, by_col))
    return out


def _natten_kernel(q_ref, k_ref, v_ref, kc_ref, vc_ref, rowtab_ref, coltab_ref, t2_ref, o_ref,
                   kt_s, v_s, kc_s, vc_s, bias_s, tail_s, *, rows):
    rblk = pl.program_id(2)
    hd = NA_HD
    lane = lax.broadcasted_iota(jnp.int32, (1, 2 * hd), 1)
    first_half = (lane % 32) < 16
    row_lane = (lane % hd) < hd // 2
    scale = NA_HD ** -0.5

    def values_and_ones(v_pair, h):
        vh = v_pair if h == 0 else pltpu.roll(v_pair, hd, 1)
        return jnp.where(lane < hd, vh, jnp.where(lane == hd, 1.0, 0.0)).astype(BF16)

    @pl.when(rblk == 0)
    def _():
        kc = kc_ref[...].astype(BF16)
        qi = lax.broadcasted_iota(jnp.int32, (GRID_W, GRID_W), 0)
        ki = lax.broadcasted_iota(jnp.int32, (GRID_W, GRID_W), 1)
        cstart = jnp.clip(qi - NA_WIN_C // 2, 0, GRID_W - NA_WIN_C)
        in_win = (ki >= cstart) & (ki < cstart + NA_WIN_C)
        s_len = k_ref.shape[0]
        tail_s[...] = jnp.zeros_like(tail_s)
        for h in range(2):
            sl = slice(h * hd, (h + 1) * hd)
            kc_s[h] = kc[:, sl]
            vc_s[h] = values_and_ones(vc_ref[...], h)
            tiles = [jnp.where(in_win, t2_ref[h, dr], NEG_INF) for dr in range(2 * NA_WIN_R - 1)]
            for v in range(NA_WIN_R):
                for j in range(NA_WIN_R):
                    bias_s[h, v, :, j * GRID_W:(j + 1) * GRID_W] = tiles[NA_WIN_R - 1 - v + j]

        eye = (lax.broadcasted_iota(jnp.int32, (2 * hd, 2 * hd), 0)
               == lax.broadcasted_iota(jnp.int32, (2 * hd, 2 * hd), 1)).astype(F32).astype(BF16)

        def prep(i, carry):
            r0 = pl.multiple_of(i * NA_PREP_ROWS, NA_PREP_ROWS)
            rws = pl.ds(r0, NA_PREP_ROWS)
            cos, sin = _rope_tables(rowtab_ref, coltab_ref, i * (NA_PREP_ROWS // GRID_W), NA_PREP_ROWS // GRID_W,
                                    row_lane)
            kr = _rope(k_ref[rws, :], cos, sin, first_half)
            kr_odd = jnp.concatenate([tail_s[...], kr[:NA_PREP_ROWS - GRID_W]], axis=0)
            tail_s[...] = kr[NA_PREP_ROWS - GRID_W:]
            krt = [_dot_nt(eye, kr.astype(BF16)).astype(BF16),
                   _dot_nt(eye, kr_odd.astype(BF16)).astype(BF16)]
            vv = v_ref[rws, :]
            for h in range(2):
                sl = slice(h * hd, (h + 1) * hd)
                for par in range(2):
                    for c in range(NA_PREP_ROWS // NA_KEY_TILE):
                        kt_s[h, par, i * (NA_PREP_ROWS // NA_KEY_TILE) + c] = (
                            krt[par][sl, c * NA_KEY_TILE:(c + 1) * NA_KEY_TILE])
                v_s[h, rws, :] = values_and_ones(vv, h)
            return carry

        lax.fori_loop(0, s_len // NA_PREP_ROWS, prep, 0, unroll=4)

    tq = NA_ROWS_PER_STEP * GRID_W
    q = q_ref[...] * scale
    cos, sin = _rope_tables(rowtab_ref, coltab_ref, rblk * NA_ROWS_PER_STEP, NA_ROWS_PER_STEP, row_lane)
    qr = _rope(q, cos, sin, first_half)
    qb = q.astype(BF16)
    qrb = qr.astype(BF16)
    rws = [slice(rr * GRID_W, (rr + 1) * GRID_W) for rr in range(NA_ROWS_PER_STEP)]
    par, slot0, key0, bidx = [], [], [], []
    for rr in range(NA_ROWS_PER_STEP):
        r = rblk * NA_ROWS_PER_STEP + rr
        rs = jnp.clip(r - NA_WIN_R // 2, 0, rows - NA_WIN_R)
        par.append(rs & 1)
        slot0.append(lax.shift_right_logical(rs, 1) + (rs & 1))
        key0.append(pl.multiple_of(rs * GRID_W, GRID_W))
        bidx.append(r - rs)

    def scores(h):
        sl = slice(h * hd, (h + 1) * hd)
        qrb_h = qrb[:, sl]
        s_ctx_all = _dot_nt(qb[:, sl], kc_s[h])
        s_win = []
        for rr in range(NA_ROWS_PER_STEP):
            kt = kt_s[h, par[rr], pl.ds(slot0[rr], NA_SPAN // NA_KEY_TILE)]
            kt = jnp.concatenate([kt[c] for c in range(NA_SPAN // NA_KEY_TILE)], axis=-1)
            s_win.append(_dot(qrb_h[rws[rr]], kt))
        return s_win, s_ctx_all

    def softmax(h, s_win, s_ctx_all):
        e_win, e_ctx = [], []
        for rr in range(NA_ROWS_PER_STEP):
            sw = s_win[rr] + bias_s[h, bidx[rr]]
            sc = s_ctx_all[rws[rr]]
            m = jnp.max(_fold_lanes(jnp.maximum, sw, sc), axis=-1, keepdims=True)
            e_win.append(jnp.exp(sw - m).astype(BF16))
            e_ctx.append(jnp.exp(sc - m).astype(BF16))
        return e_win, e_ctx

    def values(h, e_win, e_ctx):
        o_win = []
        for rr in range(NA_ROWS_PER_STEP):
            o_win.append(_dot(e_win[rr], v_s[h, pl.ds(key0[rr], NA_SPAN), :]))
        o = jnp.concatenate(o_win, axis=0) + _dot(jnp.concatenate(e_ctx, axis=0), vc_s[h])
        return o[:, :hd] * (1.0 / o[:, hd:hd + 1])

    s0 = scores(0)
    s1 = scores(1)
    p0 = softmax(0, *s0)
    o0 = values(0, *p0)
    p1 = softmax(1, *s1)
    o1 = values(1, *p1)
    o_ref[...] = jnp.concatenate([o0, o1], axis=-1).astype(o_ref.dtype)


def _na_tables(rpb, s):
    half = NA_HD // 2
    inv = jnp.power(ROPE_THETA, -jnp.arange(0, half, 2, dtype=F32) / half)

    def tables(n):
        ang = jnp.arange(n, dtype=F32)[:, None] * inv[None, :]
        reps = 2 * NA_HD // half
        return jnp.stack([jnp.tile(jnp.cos(ang), (1, 2 * reps)),
                          jnp.tile(jnp.concatenate([-jnp.sin(ang), jnp.sin(ang)], axis=-1), (1, reps))])

    rowtab, coltab = tables(s // GRID_W), tables(GRID_W)

    pad = GRID_W - NA_WIN_C
    width = 2 * GRID_W
    rp = jnp.pad(rpb.astype(F32), ((0, 0), (0, 0), (pad, pad + 2)), mode="edge")
    skew = jnp.tile(rp, (1, 1, GRID_W + 1))[:, :, GRID_W - 1:GRID_W - 1 + GRID_W * width]
    t2 = skew.reshape(rp.shape[0], rp.shape[1], GRID_W, width)[:, :, :, :GRID_W]
    return rowtab, coltab, t2


def _natten(p, pc, rowtab, coltab, t2):
    _, b, s, w = p.shape
    ctx_len = pc.shape[2]
    rows = s // GRID_W
    assert rows >= NA_WIN_R and rows % NA_ROWS_PER_STEP == 0
    tq = NA_ROWS_PER_STEP * GRID_W
    hw = 2 * NA_HD
    nhp = w // hw
    kern = functools.partial(_natten_kernel, rows=rows)
    return pl.pallas_call(
        kern,
        grid=(b, nhp, rows // NA_ROWS_PER_STEP),
        in_specs=[pl.BlockSpec((None, None, tq, hw), lambda bi, hp, r: (SEC_NQ, bi, r, hp)),
                  pl.BlockSpec((None, None, s, hw), lambda bi, hp, r: (SEC_NK, bi, 0, hp)),
                  pl.BlockSpec((None, None, s, hw), lambda bi, hp, r: (SEC_NV, bi, 0, hp)),
                  pl.BlockSpec((None, None, ctx_len, hw), lambda bi, hp, r: (CTX_SECTIONS.index(SEC_NK), bi, 0, hp)),
                  pl.BlockSpec((None, None, ctx_len, hw), lambda bi, hp, r: (CTX_SECTIONS.index(SEC_NV), bi, 0, hp)),
                  pl.BlockSpec((2, rows, hw), lambda bi, hp, r: (0, 0, 0)),
                  pl.BlockSpec((2, GRID_W, hw), lambda bi, hp, r: (0, 0, 0)),
                  pl.BlockSpec((2, 2 * NA_WIN_R - 1, GRID_W, GRID_W), lambda bi, hp, r: (hp, 0, 0, 0))],
        out_specs=pl.BlockSpec((None, tq, hw), lambda bi, hp, r: (bi, r, hp)),
        out_shape=jax.ShapeDtypeStruct((b, s, w), BF16),
        scratch_shapes=[pltpu.VMEM((2, 2, s // NA_KEY_TILE, NA_HD, NA_KEY_TILE), BF16),
                        pltpu.VMEM((2, s, hw), BF16),
                        pltpu.VMEM((2, ctx_len, NA_HD), BF16), pltpu.VMEM((2, ctx_len, hw), BF16),
                        pltpu.VMEM((2, NA_WIN_R, GRID_W, NA_SPAN), F32),
                        pltpu.VMEM((GRID_W, hw), F32)],
        compiler_params=_cparams(("arbitrary", "arbitrary", "arbitrary")),
        name="natten",
    )(p, p, p, pc, pc, rowtab, coltab, t2)


def _route(logits_t, rbias):
    e, t = logits_t.shape
    gsz = e // N_GROUPS
    scores = jax.nn.sigmoid(logits_t)
    sel = scores + rbias
    neg = -jnp.inf
    sub = lax.broadcasted_iota(jnp.int32, (gsz, t), 0).astype(F32)
    gscore = []
    for g in range(N_GROUPS):
        grp = sel[g * gsz:(g + 1) * gsz, :]
        m1 = jnp.max(grp, axis=0, keepdims=True)
        first = jnp.min(jnp.where(grp == m1, sub, float(gsz)), axis=0, keepdims=True)
        m2 = jnp.max(jnp.where(sub == first, neg, grp), axis=0, keepdims=True)
        gscore.append(m1 + m2)
    masked = []
    for g in range(N_GROUPS):
        rank = jnp.zeros((1, t), F32)
        for g2 in range(N_GROUPS):
            if g2 == g:
                continue
            if g2 < g:
                ahead = gscore[g2] >= gscore[g]
            else:
                ahead = gscore[g2] > gscore[g]
            rank = rank + jnp.where(ahead, 1.0, 0.0)
        masked.append(jnp.where(rank < TOPK_GROUPS, sel[g * gsz:(g + 1) * gsz, :], neg))
    work = jnp.concatenate(masked, axis=0)
    eidx = lax.broadcasted_iota(jnp.int32, (e, t), 0).astype(F32)
    idxs, ws = [], []
    chosen = jnp.zeros((e, t), F32)
    for _ in range(TOP_K):
        m = jnp.max(work, axis=0, keepdims=True)
        first = jnp.min(jnp.where(work == m, eidx, float(e)), axis=0, keepdims=True)
        pick = eidx == first
        idxs.append(first)
        ws.append(jnp.sum(jnp.where(pick, scores, 0.0), axis=0, keepdims=True))
        chosen = jnp.where(pick, 1.0, chosen)
        work = jnp.where(pick, neg, work)
    w = jnp.concatenate(ws, axis=0)
    w = w / jnp.sum(w, axis=0, keepdims=True) * ROUTED_SCALE
    return jnp.concatenate(idxs, axis=0).astype(jnp.int32), w, chosen


MERGE_TOK = 512
MERGE_SUB = 128


def _merge_kernel(of_ref, ob_ref, og_ref, yna_ref, ga_ref, gb_ref, x_ref, g1_ref, sh2_ref, sc2_ref,
                  hgg_ref, ln1g_ref, ln1b_ref, wa_ref, wb_ref, wo_ref, wr_ref, rb_ref,
                  x1_ref, h2_ref, topi_ref, topw_ref, cnt_ref, *, alpha):
    tm = x_ref.shape[0]
    subs = [slice(i * MERGE_SUB, (i + 1) * MERGE_SUB) for i in range(tm // MERGE_SUB)]

    def branches(rows):
        o = of_ref[rows, :] + ob_ref[rows, :]
        parts = []
        for h in range(HG_HEADS):
            oh = o[:, h * HG_DK:(h + 1) * HG_DK]
            parts.append(oh * lax.rsqrt(jnp.mean(oh * oh, axis=-1, keepdims=True) + LN_EPS))
        y_hg = jnp.concatenate(parts, axis=-1) * hgg_ref[...] * _silu(og_ref[rows, :])
        return _dot(y_hg.astype(BF16), wa_ref[...]), _dot(yna_ref[rows, :], wb_ref[...])

    def out_proj(rows, ya, yb):
        t = jax.nn.sigmoid(ga_ref[rows, :]) * ya + jax.nn.sigmoid(gb_ref[rows, :]) * yb
        return _dot(t.astype(BF16), wo_ref[...])

    def norms_router(rows, i, y):
        x1 = _normalize(alpha * x_ref[rows, :] + g1_ref[...] * y) * ln1g_ref[...] + ln1b_ref[...]
        x1_ref[rows, :] = x1
        h2 = _normalize(x1) * (1.0 + sc2_ref[...]) + sh2_ref[...]
        h2_ref[rows, :] = _pack_words(h2)
        hh, hm, hl = _split3(h2)
        wh, wm, wl = _split3(wr_ref[...])
        return (_dot_nt(wh, hh) + _dot_nt(wh, hm) + _dot_nt(wm, hh)
                + _dot_nt(wh, hl) + _dot_nt(wl, hh) + _dot_nt(wm, hm))

    ab = [branches(rows) for rows in subs]
    ys = [out_proj(rows, *ab[i]) for i, rows in enumerate(subs)]
    logits = [norms_router(rows, i, ys[i]) for i, rows in enumerate(subs)]

    @pl.when((pl.program_id(0) == 0) & (pl.program_id(1) == 0))
    def _():
        cnt_ref[...] = jnp.zeros_like(cnt_ref)

    for i, rows in enumerate(subs):
        topi, topw, chosen = _route(logits[i], rb_ref[...])
        topi_ref[:, rows] = topi
        topw_ref[:, rows] = topw
        cnt_ref[...] += jnp.sum(chosen, axis=1, keepdims=True)


def _merge(o_f, o_b, p, y_na, x, g1, sh2, sc2, hg_norm_g, ln1_g, ln1_b, w_a, w_b, w_o, w_router_t, router_bias,
           alpha):
    b, s, d = x.shape
    tm = min(MERGE_TOK, s)
    e = w_router_t.shape[0]
    tok = lambda bi, i: (bi, i, 0)
    blk = pl.BlockSpec((None, tm, d), tok)

    def sec(section):
        return pl.BlockSpec((None, None, tm, d), lambda bi, i: (section, bi, i, 0))

    mod = pl.BlockSpec((None, 1, d), lambda bi, i: (bi, 0, 0))
    vec = pl.BlockSpec((1, d), lambda bi, i: (0, 0))
    mat = pl.BlockSpec((d, d), lambda bi, i: (0, 0), pipeline_mode=pl.Buffered(1))
    return pl.pallas_call(
        functools.partial(_merge_kernel, alpha=alpha),
        grid=(b, s // tm),
        in_specs=[blk, blk, sec(SEC_OG), blk, sec(SEC_GA), sec(SEC_GB), blk, mod, mod, mod,
                  vec, vec, vec, mat, mat, mat,
                  pl.BlockSpec((e, d), lambda bi, i: (0, 0)),
                  pl.BlockSpec((e, 1), lambda bi, i: (0, 0))],
        out_specs=[blk,
                   pl.BlockSpec((tm, d // 2), lambda bi, i: (bi * (s // tm) + i, 0)),
                   pl.BlockSpec((None, TOP_K, tm), lambda bi, i: (bi, 0, i)),
                   pl.BlockSpec((None, TOP_K, tm), lambda bi, i: (bi, 0, i)),
                   pl.BlockSpec((e, 128), lambda bi, i: (0, 0))],
        out_shape=[jax.ShapeDtypeStruct((b, s, d), F32),
                   jax.ShapeDtypeStruct((b * s, d // 2), U32),
                   jax.ShapeDtypeStruct((b, TOP_K, s), jnp.int32), jax.ShapeDtypeStruct((b, TOP_K, s), F32),
                   jax.ShapeDtypeStruct((e, 128), F32)],
        compiler_params=_cparams(("arbitrary", "arbitrary")),
        name="merge",
    )(o_f, o_b, p, y_na, p, p, x, g1, sh2, sc2, hg_norm_g.reshape(1, d), ln1_g.reshape(1, d),
      ln1_b.reshape(1, d), w_a, w_b, w_o, w_router_t, router_bias.reshape(e, 1))


MOE_TILE = 512
MOE_TOK = 512


def _plan_kernel(topi_ref, off_ref, dest_ref, carry_ref):
    @pl.when(pl.program_id(0) == 0)
    def _():
        carry_ref[...] = jnp.zeros_like(carry_ref)

    topi = topi_ref[...]
    tok = topi.shape[1]
    eidx = lax.broadcasted_iota(jnp.int32, (N_EXPERTS, tok), 0)
    hits = [eidx == topi[k:k + 1, :] for k in range(TOP_K)]
    m = jnp.zeros((N_EXPERTS, tok), F32)
    for hit in hits:
        m = jnp.where(hit, 1.0, m)
    before = (lax.broadcasted_iota(jnp.int32, (tok, tok), 0)
              < lax.broadcasted_iota(jnp.int32, (tok, tok), 1)).astype(F32).astype(BF16)
    row = off_ref[...] + carry_ref[...] + _dot(m.astype(BF16), before)
    dest = [jnp.sum(jnp.where(hit, row, 0.0), axis=0, keepdims=True) for hit in hits]
    dest_ref[...] = jnp.concatenate(dest, axis=0).astype(jnp.int32)
    carry_ref[...] += jnp.sum(m, axis=1, keepdims=True)


def _plan(topi, seg_off):
    b, k, s = topi.shape
    per_b = s // MOE_TOK
    blk = pl.BlockSpec((None, k, MOE_TOK), lambda i: (i // per_b, 0, i % per_b))
    return pl.pallas_call(
        _plan_kernel,
        grid=(b * per_b,),
        in_specs=[blk, pl.BlockSpec((N_EXPERTS, 1), lambda i: (0, 0))],
        out_specs=blk,
        out_shape=jax.ShapeDtypeStruct((b, k, s), jnp.int32),
        scratch_shapes=[pltpu.VMEM((N_EXPERTS, 1), F32)],
        compiler_params=_cparams(("arbitrary",)),
        name="plan",
    )(topi, seg_off.astype(F32).reshape(N_EXPERTS, 1))


SC_WINDOW = 128


def _sc_workers():
    info = plsc.get_sparse_core_info()
    return info.num_cores, info.num_cores * info.num_subcores


def _scatter_rows(src, idx, zero_idx, n_rows):
    t, w = src.shape
    m, mz = idx.shape[0], zero_idx.shape[0]
    n_cores, n_workers = _sc_workers()
    per_worker, per_worker_z = t // n_workers, mz // n_workers
    assert m % t == 0
    assert per_worker * n_workers == t and per_worker % SC_WINDOW == 0
    assert per_worker_z * n_workers == mz and per_worker_z % SC_WINDOW == 0
    mesh = plsc.VectorSubcoreMesh(core_axis_name="core", subcore_axis_name="subcore")

    @functools.partial(
        pl.kernel, mesh=mesh, out_type=jax.ShapeDtypeStruct((n_rows, w), src.dtype),
        scratch_types=[pltpu.VMEM((SC_WINDOW,), jnp.int32), pltpu.VMEM((SC_WINDOW, w), src.dtype),
                       pltpu.SemaphoreType.DMA])
    def scatter(src_hbm, idx_hbm, zeros_hbm, zero_idx_hbm, out_hbm, idx_v, rows_v, sem):
        worker = lax.axis_index("subcore") * n_cores + lax.axis_index("core")

        @pl.loop(0, per_worker // SC_WINDOW)
        def _(step):
            first = pl.multiple_of(worker * per_worker + step * SC_WINDOW, SC_WINDOW)
            pltpu.sync_copy(src_hbm.at[pl.ds(first, SC_WINDOW)], rows_v)
            for copy in range(m // t):
                pltpu.sync_copy(idx_hbm.at[pl.ds(copy * t + first, SC_WINDOW)], idx_v)
                pltpu.async_copy(rows_v, out_hbm.at[idx_v], sem).wait()

        pltpu.sync_copy(zeros_hbm, rows_v)

        @pl.loop(0, per_worker_z // SC_WINDOW)
        def _(step):
            base = pl.multiple_of(worker * per_worker_z + step * SC_WINDOW, SC_WINDOW)
            pltpu.sync_copy(zero_idx_hbm.at[pl.ds(base, SC_WINDOW)], idx_v)
            pltpu.async_copy(rows_v, out_hbm.at[idx_v], sem).wait()

    return scatter(src, idx, jnp.zeros((SC_WINDOW, w), src.dtype), zero_idx)


EXPERT_RING = 3


def _experts_kernel(te_ref, tb_ref, nt_ref, seg_ref, nxt_ref, xs_ref, wg_ref, wu_ref, wd_ref, ys_ref,
                    xbuf, sem, wg_f32, wu_f32, wd_f32, wg_s, wu_s, wd_s, wsem):
    i = pl.program_id(0)
    n_tiles = nt_ref[0]

    def weight_copies(expert, slot):
        return [pltpu.make_async_copy(src.at[expert], dst.at[slot], wsem.at[slot, n])
                for n, (src, dst) in enumerate(((wg_ref, wg_f32), (wu_ref, wu_f32), (wd_ref, wd_f32)))]

    @pl.when(i == 0)
    def _():
        for c in weight_copies(te_ref[0], 0):
            c.start()

    first = (i == 0) | (seg_ref[i] != seg_ref[jnp.maximum(i - 1, 0)])

    @pl.when((i < n_tiles) & first)
    def _():
        slot = seg_ref[i] & 1
        for c in weight_copies(te_ref[i], slot):
            c.wait()
        wg_s[...] = wg_f32[slot].astype(BF16)
        wu_s[...] = wu_f32[slot].astype(BF16)
        wd_s[...] = wd_f32[slot].astype(BF16)

        @pl.when(nxt_ref[i] >= 0)
        def _():
            for c in weight_copies(nxt_ref[i], 1 - slot):
                c.start()

    def tile_copy(j):
        slot = lax.rem(j, EXPERT_RING)
        row0 = pl.multiple_of(tb_ref[j] * MOE_TILE, MOE_TILE)
        return pltpu.make_async_copy(xs_ref.at[pl.ds(row0, MOE_TILE), :], xbuf.at[slot], sem.at[slot])

    @pl.when(i == 0)
    def _():
        for j in range(EXPERT_RING - 1):
            @pl.when(j < n_tiles)
            def _():
                tile_copy(j).start()

    ahead = i + (EXPERT_RING - 1)

    @pl.when(ahead < n_tiles)
    def _():
        tile_copy(ahead).start()

    @pl.when(i < n_tiles)
    def _():
        tile_copy(i).wait()
        x = _unpack_words(xbuf[lax.rem(i, EXPERT_RING)]).astype(BF16)
        act = _silu(_dot(x, wg_s[...])) * _dot(x, wu_s[...])
        ys_ref[...] = _pack_words(_dot(act.astype(BF16), wd_s[...]))


def _experts(xs, tile_expert, tile_block, n_tiles, tile_segment, next_expert, wg, wu, wd):
    d, f = wg.shape[1], wg.shape[2]
    anywhere = pl.BlockSpec(memory_space=pl.ANY)
    grid_spec = pltpu.PrefetchScalarGridSpec(
        num_scalar_prefetch=5,
        grid=(xs.shape[0] // MOE_TILE,),
        in_specs=[anywhere, anywhere, anywhere, anywhere],
        out_specs=pl.BlockSpec((MOE_TILE, d // 2), lambda i, te, tb, nt, seg, nxt: (tb[i], 0)),
        scratch_shapes=[pltpu.VMEM((EXPERT_RING, MOE_TILE, d // 2), U32), pltpu.SemaphoreType.DMA((EXPERT_RING,)),
                        pltpu.VMEM((2, d, f), F32), pltpu.VMEM((2, d, f), F32), pltpu.VMEM((2, f, d), F32),
                        pltpu.VMEM((d, f), BF16), pltpu.VMEM((d, f), BF16), pltpu.VMEM((f, d), BF16),
                        pltpu.SemaphoreType.DMA((2, 3))],
    )
    return pl.pallas_call(
        _experts_kernel,
        grid_spec=grid_spec,
        out_shape=jax.ShapeDtypeStruct(xs.shape, U32),
        compiler_params=_cparams(("arbitrary",)),
        name="experts",
    )(tile_expert, tile_block, n_tiles, tile_segment, next_expert, xs, wg, wu, wd)


def _gather_rows(table, idx):
    m = idx.shape[0]
    w = table.shape[1]
    win = SC_WINDOW // 2
    n_cores, n_workers = _sc_workers()
    per_worker = m // n_workers
    n_pairs = per_worker // (2 * win)
    assert per_worker * n_workers == m and n_pairs * 2 * win == per_worker
    mesh = plsc.VectorSubcoreMesh(core_axis_name="core", subcore_axis_name="subcore")

    @functools.partial(
        pl.kernel, mesh=mesh, out_type=jax.ShapeDtypeStruct((m, w), table.dtype),
        scratch_types=[pltpu.VMEM((win,), jnp.int32), pltpu.VMEM((win,), jnp.int32),
                       pltpu.VMEM((win, w), table.dtype), pltpu.VMEM((win, w), table.dtype),
                       pltpu.SemaphoreType.DMA, pltpu.SemaphoreType.DMA])
    def gather(table_hbm, idx_hbm, out_hbm, idx_a, idx_b, rows_a, rows_b, sem_a, sem_b):
        worker = lax.axis_index("subcore") * n_cores + lax.axis_index("core")
        start = worker * per_worker

        def request(first, idx_v, rows_v, sem):
            pltpu.sync_copy(idx_hbm.at[pl.ds(first, win)], idx_v)
            pltpu.async_copy(table_hbm.at[idx_v], rows_v, sem)

        def deliver(first, idx_v, rows_v, sem):
            pltpu.make_async_copy(table_hbm.at[idx_v], rows_v, sem).wait()
            pltpu.sync_copy(rows_v, out_hbm.at[pl.ds(first, win)])

        request(pl.multiple_of(start, win), idx_a, rows_a, sem_a)

        @pl.loop(0, n_pairs)
        def _(pair):
            first_a = pl.multiple_of(start + pair * 2 * win, win)
            first_b = pl.multiple_of(first_a + win, win)
            request(first_b, idx_b, rows_b, sem_b)
            deliver(first_a, idx_a, rows_a, sem_a)

            @pl.when(pair + 1 < n_pairs)
            def _():
                request(pl.multiple_of(first_b + win, win), idx_a, rows_a, sem_a)

            deliver(first_b, idx_b, rows_b, sem_b)

    return gather(table, idx)


def _combine_kernel(rows_ref, topw_ref, h_ref, x1_ref, g2_ref, sg_ref, su_ref, sd_ref, ln2g_ref, ln2b_ref,
                    o_ref, *, alpha):
    h = _unpack_words(h_ref[...]).astype(BF16)
    act = _silu(_dot(h, sg_ref[...])) * _dot(h, su_ref[...])
    y = _dot(act.astype(BF16), sd_ref[...])
    w = topw_ref[...].T
    for k in range(TOP_K):
        y = y + w[:, k:k + 1] * _unpack_words(rows_ref[k])
    o_ref[...] = _normalize(alpha * x1_ref[...] + g2_ref[...] * y) * ln2g_ref[...] + ln2b_ref[...]


def _combine(gathered, topw, h2p, x1, g2, sg, su, sd, ln2_g, ln2_b, alpha):
    t, d = x1.shape
    b, k, s = topw.shape
    per_b = s // MOE_TOK
    fs = sg.shape[1]
    rows = pl.BlockSpec((MOE_TOK, d), lambda i: (i, 0))
    packed = pl.BlockSpec((MOE_TOK, d // 2), lambda i: (i, 0))
    vec = pl.BlockSpec((1, d), lambda i: (0, 0))
    return pl.pallas_call(
        functools.partial(_combine_kernel, alpha=alpha),
        grid=(t // MOE_TOK,),
        in_specs=[pl.BlockSpec((k, MOE_TOK, d // 2), lambda i: (0, i, 0)),
                  pl.BlockSpec((None, k, MOE_TOK), lambda i: (i // per_b, 0, i % per_b)),
                  packed, rows,
                  pl.BlockSpec((None, 1, d), lambda i: (i // per_b, 0, 0)),
                  pl.BlockSpec((d, fs), lambda i: (0, 0)),
                  pl.BlockSpec((d, fs), lambda i: (0, 0)),
                  pl.BlockSpec((fs, d), lambda i: (0, 0)),
                  vec, vec],
        out_specs=rows,
        out_shape=jax.ShapeDtypeStruct((t, d), F32),
        compiler_params=_cparams(("arbitrary",)),
        name="combine",
    )(gathered, topw, h2p, x1, g2, sg, su, sd, ln2_g.reshape(1, d), ln2_b.reshape(1, d))


def _moe(h2p, topi, topw, cnt, x1, g2, wg, wu, wd, sg, su, sd, ln2_g, ln2_b, alpha):
    b, s, d = x1.shape
    t = b * s
    cnt = cnt[:, 0].astype(jnp.int32)
    tiles_e = (cnt + (MOE_TILE - 1)) // MOE_TILE
    tiles_cum = jnp.cumsum(tiles_e)
    seg_off = (tiles_cum - tiles_e) * MOE_TILE
    n_tiles_max = t * TOP_K // MOE_TILE + N_EXPERTS
    tile_block = jnp.minimum(jnp.arange(n_tiles_max, dtype=jnp.int32), tiles_cum[-1] - 1)
    tile_expert = jnp.sum((tiles_cum[None, :] <= tile_block[:, None]).astype(jnp.int32), axis=1)
    n_tiles = tiles_cum[-1:].astype(jnp.int32)
    present = tiles_e > 0
    seg_of_expert = jnp.cumsum(present.astype(jnp.int32)) - 1
    later = jnp.where(present, jnp.arange(N_EXPERTS, dtype=jnp.int32), N_EXPERTS)
    next_present = jnp.concatenate([lax.cummin(later, reverse=True)[1:], jnp.full((1,), N_EXPERTS, jnp.int32)])
    next_present = jnp.where(next_present < N_EXPERTS, next_present, -1)
    tile_segment = seg_of_expert[tile_expert].astype(jnp.int32)
    next_expert = next_present[tile_expert].astype(jnp.int32)

    dest = jnp.transpose(_plan(topi, seg_off), (1, 0, 2)).reshape(TOP_K * t)
    j = jnp.arange(MOE_TILE, dtype=jnp.int32)[None, :]
    n_pad = (tiles_e * MOE_TILE - cnt)[:, None]
    spare = (n_tiles_max - 1) * MOE_TILE + j
    zero_idx = jnp.where(j < n_pad, (seg_off + cnt)[:, None] + j, spare).reshape(N_EXPERTS * MOE_TILE)
    xs = _scatter_rows(h2p, dest, zero_idx.astype(jnp.int32), n_tiles_max * MOE_TILE)
    ys = _experts(xs, tile_expert, tile_block, n_tiles, tile_segment, next_expert, wg, wu, wd)
    gathered = _gather_rows(ys, dest)
    out = _combine(gathered.reshape(TOP_K, t, d // 2), topw, h2p, x1.reshape(t, d), g2, sg, su, sd,
                   ln2_g, ln2_b, alpha)
    return out.reshape(b, s, d)


def kernel(x, c, ctx, c_ctx, w_ada, b_ada, w_in, hg_lb_fwd, hg_lb_bwd, hg_norm_g, na_rpb, w_branch_a, w_branch_b, w_out, ln1_g, ln1_b, w_router, router_bias, w_e_gate, w_e_up, w_e_down, w_sh_gate, w_sh_up, w_sh_down, ln2_g, ln2_b):
    depth = w_ada.shape[0]
    assert depth == 1, "single-layer block"
    b, s, d = x.shape
    alpha = (2.0 * depth) ** 0.25
    l = 0
    lb_fwd = jnp.cumsum(jax.nn.softmax(hg_lb_fwd.astype(F32), axis=0), axis=0)[l]
    lb_bwd = jnp.cumsum(jax.nn.softmax(hg_lb_bwd.astype(F32), axis=0), axis=0)[l]

    cond_rows = jnp.concatenate([c, c_ctx[None, :], jnp.zeros((8 - b - 1, d), F32)], axis=0)
    mod = _ada(cond_rows, w_ada[l], b_ada[l])
    sh1, sc1, g1, sh2, sc2, g2 = [m[:b, None, :] for m in jnp.split(mod, 6, axis=-1)]
    csh1, csc1 = [jnp.broadcast_to(m[b:b + 1, None, :], (b, 1, d)) for m in jnp.split(mod, 6, axis=-1)[:2]]

    w_in_b = w_in[l].astype(BF16)
    p = _inproj(x, sh1, sc1, w_in_b, tuple(range(N_SECTIONS)))
    pc = _inproj(ctx, csh1, csc1, w_in_b, CTX_SECTIONS)

    o_f, o_b = _hgrn(p, pc, lb_fwd, lb_bwd)
    y_na = _natten(p, pc, *_na_tables(na_rpb[l], s))

    x1, h2, topi, topw, cnt = _merge(o_f, o_b, p, y_na, x, g1, sh2, sc2, hg_norm_g[l], ln1_g[l], ln1_b[l],
                                     w_branch_a[l].astype(BF16), w_branch_b[l].astype(BF16),
                                     w_out[l].astype(BF16), w_router[l].T, router_bias[l], alpha)

    return _moe(h2, topi, topw, cnt, x1, g2,
                w_e_gate[l], w_e_up[l], w_e_down[l],
                w_sh_gate[l].astype(BF16), w_sh_up[l].astype(BF16), w_sh_down[l].astype(BF16),
                ln2_g[l], ln2_b[l], alpha)
```

```python
import functools

import numpy as np
import jax
import jax.numpy as jnp
from jax import lax
from jax.experimental import pallas as pl
from jax.experimental.pallas import tpu as pltpu
from jax.experimental.pallas import tpu_sc as plsc

F32 = jnp.float32
BF16 = jnp.bfloat16

D_MODEL = 1024
GRID_W = 64
HG_HEADS = 8
HG_DK = 128
HG_CHUNK = 64
NA_HEADS = 16
NA_HD = 64
NA_WIN_R = 8
NA_WIN_C = 16
ROPE_THETA = 10000.0
NEG_INF = -1e30
N_EXPERTS = 64
EXPERT_DIM = 256
TOP_K = 8
N_GROUPS = 8
TOPK_GROUPS = 4
ROUTED_SCALE = 2.5
LN_EPS = 1e-6
N_SECTIONS = 10
SEC_Q, SEC_FF, SEC_FB, SEC_I, SEC_OG, SEC_NQ, SEC_NK, SEC_NV, SEC_GA, SEC_GB = range(10)
CTX_SECTIONS = (SEC_FF, SEC_FB, SEC_I, SEC_NK, SEC_NV)

VMEM_LIMIT = 56 * 1024 * 1024


def _cparams(sem):
    return pltpu.CompilerParams(dimension_semantics=sem, vmem_limit_bytes=VMEM_LIMIT)


def _normalize(x):
    mu = jnp.mean(x, axis=-1, keepdims=True)
    xc = x - mu
    var = jnp.mean(xc * xc, axis=-1, keepdims=True)
    return xc * lax.rsqrt(var + LN_EPS)


def _silu(x):
    return x * jax.nn.sigmoid(x)


def _dot(a, b):
    return jnp.dot(a, b, preferred_element_type=F32)


def _dot_nt(a, b):
    return lax.dot_general(a, b, (((1,), (1,)), ((), ())), preferred_element_type=F32)


def _dot_tn(a, b):
    return lax.dot_general(a, b, (((0,), (0,)), ((), ())), preferred_element_type=F32)


U32 = jnp.uint32


def _pack_words(x):
    half = x.shape[1] // 2
    lo = lax.bitcast_convert_type(x[:, :half].astype(BF16).astype(F32), U32) >> 16
    hi = lax.bitcast_convert_type(x[:, half:].astype(BF16).astype(F32), U32) & jnp.uint32(0xFFFF0000)
    return lo | hi


def _unpack_words(w):
    lo = lax.bitcast_convert_type(w << 16, F32)
    hi = lax.bitcast_convert_type(w & jnp.uint32(0xFFFF0000), F32)
    return jnp.concatenate([lo, hi], axis=-1)


def _split3(x):
    hi = x.astype(BF16)
    r1 = x - hi.astype(F32)
    mid = r1.astype(BF16)
    lo = (r1 - mid.astype(F32)).astype(BF16)
    return hi, mid, lo


def _ada_kernel(c_ref, w_ref, b_ref, o_ref):
    cond = _silu(c_ref[...])
    o_ref[...] = _dot(cond.astype(BF16), w_ref[...].astype(BF16)) + b_ref[...]


def _ada(cond_rows, w_ada, b_ada):
    r, d = cond_rows.shape
    n = w_ada.shape[1]
    tn = 1024
    return pl.pallas_call(
        _ada_kernel,
        grid=(n // tn,),
        in_specs=[pl.BlockSpec((r, d), lambda j: (0, 0)),
                  pl.BlockSpec((d, tn), lambda j: (0, j)),
                  pl.BlockSpec((1, tn), lambda j: (0, j))],
        out_specs=pl.BlockSpec((r, tn), lambda j: (0, j)),
        out_shape=jax.ShapeDtypeStruct((r, n), F32),
        compiler_params=_cparams(("arbitrary",)),
        name="ada",
    )(cond_rows, w_ada, b_ada.reshape(1, n))


INPROJ_TOK = 2048


def _inproj_kernel(x_ref, sh_ref, sc_ref, w_ref, o_ref, h_ref):
    @pl.when(pl.program_id(2) == 0)
    def _():
        h = _normalize(x_ref[...]) * (1.0 + sc_ref[...]) + sh_ref[...]
        h_ref[...] = h.astype(BF16)

    o_ref[...] = _dot(h_ref[...], w_ref[...])


def _inproj(x, shift, scale, w_in_bf16, sections):
    b, s, d = x.shape
    tm = min(INPROJ_TOK, s)
    nj = len(sections)

    def section(j):
        sec = sections[-1]
        for k in range(nj - 2, -1, -1):
            sec = jnp.where(j == k, sections[k], sec)
        return sec

    return pl.pallas_call(
        _inproj_kernel,
        grid=(b, s // tm, nj),
        in_specs=[pl.BlockSpec((None, tm, d), lambda bi, i, j: (bi, i, 0)),
                  pl.BlockSpec((None, 1, d), lambda bi, i, j: (bi, 0, 0)),
                  pl.BlockSpec((None, 1, d), lambda bi, i, j: (bi, 0, 0)),
                  pl.BlockSpec((d, d), lambda bi, i, j: (0, section(j)))],
        out_specs=pl.BlockSpec((None, None, tm, d), lambda bi, i, j: (j, bi, i, 0)),
        out_shape=jax.ShapeDtypeStruct((nj, b, s, d), F32),
        scratch_shapes=[pltpu.VMEM((tm, d), BF16)],
        compiler_params=_cparams(("arbitrary", "arbitrary", "arbitrary")),
        name="inproj",
    )(x, shift, scale, w_in_bf16)


def _hgrn_gates(q, fraw, v, lb, tri_bf16, last_row):
    f = lb + (1.0 - lb) * jax.nn.sigmoid(fraw)
    k = 1.0 - f
    lf = jnp.log(f)
    hi, mid, lo = _split3(lf)
    a = _dot(tri_bf16, hi) + _dot(tri_bf16, mid) + _dot(tri_bf16, lo)
    a_last = a[last_row:last_row + 1, :]
    kd = (k * jnp.exp(a_last - a)).astype(BF16)
    decay = jnp.exp(a_last)
    qa = kb = None
    if q is not None:
        qa = (_silu(q) * jnp.exp(a)).astype(BF16)
        kb = (k * jnp.exp(-a)).astype(BF16)
    return qa, kb, kd, v.astype(BF16), decay


def _hgrn_chunks(chunks, st_ref):
    first = []
    for d, ((qa, kb, kd, vb, decay), keep) in enumerate(chunks):
        for h in range(HG_HEADS):
            sl = slice(h * HG_DK, (h + 1) * HG_DK)
            st = st_ref[d, h]
            if qa is not None:
                first.append((_dot_nt(qa[:, sl], kb[:, sl]), _dot_nt(qa[:, sl], st.astype(BF16))))
            st_ref[d, h] = st * decay[:, sl] + _dot_tn(vb[:, sl], kd[:, sl])
    results = []
    for d, ((qa, kb, kd, vb, decay), keep) in enumerate(chunks):
        if qa is None:
            results.append(None)
            continue
        outs = []
        for h in range(HG_HEADS):
            sl = slice(h * HG_DK, (h + 1) * HG_DK)
            s_qk, o_state = first.pop(0)
            outs.append(_dot(jnp.where(keep, s_qk, 0.0).astype(BF16), vb[:, sl]) + o_state)
        results.append(jnp.concatenate(outs, axis=-1))
    return results


def _hgrn_kernel(qf_ref, ff_ref, if_ref, qb_ref, fb_ref, ib_ref, cff_ref, cfb_ref, ci_ref,
                 lbf_ref, lbb_ref, of_ref, ob_ref, st_ref, *, n_sub, n_ctx_sub):
    n = pl.program_id(1)
    c = HG_CHUNK
    row = lax.broadcasted_iota(jnp.int32, (c, c), 0)
    col = lax.broadcasted_iota(jnp.int32, (c, c), 1)
    keep_f = col <= row
    keep_b = col >= row
    tri_f = keep_f.astype(F32).astype(BF16)
    tri_b = keep_b.astype(F32).astype(BF16)
    lbf = lbf_ref[...]
    lbb = lbb_ref[...]

    @pl.when(n == 0)
    def _():
        st_ref[...] = jnp.zeros_like(st_ref)

        def body(i, carry):
            r0 = pl.multiple_of(i * c, c)
            r1 = pl.multiple_of((n_ctx_sub - 1 - i) * c, c)
            gf = _hgrn_gates(None, cff_ref[pl.ds(r0, c), :], ci_ref[pl.ds(r0, c), :], lbf, tri_f, c - 1)
            gb = _hgrn_gates(None, cfb_ref[pl.ds(r1, c), :], ci_ref[pl.ds(r1, c), :], lbb, tri_b, 0)
            _hgrn_chunks([(gf, keep_f), (gb, keep_b)], st_ref)
            return carry

        lax.fori_loop(0, n_ctx_sub, body, 0)

    @pl.when(n > 0)
    def _():
        def body(i, carry):
            r0 = pl.multiple_of(i * c, c)
            r1 = pl.multiple_of((n_sub - 1 - i) * c, c)
            gf = _hgrn_gates(qf_ref[pl.ds(r0, c), :], ff_ref[pl.ds(r0, c), :], if_ref[pl.ds(r0, c), :],
                             lbf, tri_f, c - 1)
            gb = _hgrn_gates(qb_ref[pl.ds(r1, c), :], fb_ref[pl.ds(r1, c), :], ib_ref[pl.ds(r1, c), :],
                             lbb, tri_b, 0)
            o_f, o_b = _hgrn_chunks([(gf, keep_f), (gb, keep_b)], st_ref)
            of_ref[pl.ds(r0, c), :] = o_f
            ob_ref[pl.ds(r1, c), :] = o_b
            return carry

        lax.fori_loop(0, n_sub, body, 0, unroll=True)


def _hgrn(p, pc, lb_fwd, lb_bwd):
    _, b, s, w = p.shape
    ctx_len = pc.shape[2]
    tb = min(512, s)
    nb = s // tb
    fwd = lambda bi, n: jnp.maximum(n - 1, 0)
    bwd = lambda bi, n: nb - 1 - jnp.maximum(n - 1, 0)

    def sec(section, blk):
        return pl.BlockSpec((None, None, tb, w), lambda bi, n: (section, bi, blk(bi, n), 0))

    def csec(section):
        return pl.BlockSpec((None, None, ctx_len, w), lambda bi, n: (CTX_SECTIONS.index(section), bi, 0, 0))

    vec = pl.BlockSpec((1, w), lambda bi, n: (0, 0))
    kern = functools.partial(_hgrn_kernel, n_sub=tb // HG_CHUNK, n_ctx_sub=ctx_len // HG_CHUNK)
    return pl.pallas_call(
        kern,
        grid=(b, nb + 1),
        in_specs=[sec(SEC_Q, fwd), sec(SEC_FF, fwd), sec(SEC_I, fwd),
                  sec(SEC_Q, bwd), sec(SEC_FB, bwd), sec(SEC_I, bwd),
                  csec(SEC_FF), csec(SEC_FB), csec(SEC_I), vec, vec],
        out_specs=[pl.BlockSpec((None, tb, w), lambda bi, n: (bi, fwd(bi, n), 0)),
                   pl.BlockSpec((None, tb, w), lambda bi, n: (bi, bwd(bi, n), 0))],
        out_shape=[jax.ShapeDtypeStruct((b, s, w), F32), jax.ShapeDtypeStruct((b, s, w), F32)],
        scratch_shapes=[pltpu.VMEM((2, HG_HEADS, HG_DK, HG_DK), F32)],
        compiler_params=_cparams(("arbitrary", "arbitrary")),
        name="hgrn",
    )(p, p, p, p, p, p, pc, pc, pc, lb_fwd.reshape(1, w), lb_bwd.reshape(1, w))


NA_ROWS_PER_STEP = 64
NA_PREP_ROWS = 512
NA_KEY_TILE = 128
NA_SPAN = NA_WIN_R * GRID_W


def _rope(t, cos, sin_signed, first_half):
    w = t.shape[-1]
    partner = jnp.where(first_half, pltpu.roll(t, w - 16, 1), pltpu.roll(t, 16, 1))
    return t * cos + partner * sin_signed


def _fold_lanes(op, *arrays):
    tiles = [a[:, c:c + 128] for a in arrays for c in range(0, a.shape[-1], 128)]
    acc = tiles[0]
    for t in tiles[1:]:
        acc = op(acc, t)
    return acc


def _rope_tables(rowtab_ref, coltab_ref, row0, n_rows, row_lane):
    out = []
    for i in range(2):
        rt = rowtab_ref[i, pl.ds(row0, n_rows), :]
        by_row = jnp.concatenate([jnp.broadcast_to(rt[r:r + 1, :], (GRID_W, rt.shape[1])) for r in range(n_rows)],
                                 axis=0)
        by_col = jnp.concatenate([coltab_ref[i]] * n_rows, axis=0)
        out.append(jnp.where(row_lane, by_row, by_col))
    return out


def _natten_kernel(q_ref, k_ref, v_ref, kc_ref, vc_ref, rowtab_ref, coltab_ref, t2_ref, o_ref,
                   kt_s, v_s, kc_s, vc_s, bias_s, tail_s, *, rows):
    rblk = pl.program_id(2)
    hd = NA_HD
    lane = lax.broadcasted_iota(jnp.int32, (1, 2 * hd), 1)
    first_half = (lane % 32) < 16
    row_lane = (lane % hd) < hd // 2
    scale = NA_HD ** -0.5

    def values_and_ones(v_pair, h):
        vh = v_pair if h == 0 else pltpu.roll(v_pair, hd, 1)
        return jnp.where(lane < hd, vh, jnp.where(lane == hd, 1.0, 0.0)).astype(BF16)

    @pl.when(rblk == 0)
    def _():
        kc = kc_ref[...].astype(BF16)
        qi = lax.broadcasted_iota(jnp.int32, (GRID_W, GRID_W), 0)
        ki = lax.broadcasted_iota(jnp.int32, (GRID_W, GRID_W), 1)
        cstart = jnp.clip(qi - NA_WIN_C // 2, 0, GRID_W - NA_WIN_C)
        in_win = (ki >= cstart) & (ki < cstart + NA_WIN_C)
        s_len = k_ref.shape[0]
        tail_s[...] = jnp.zeros_like(tail_s)
        for h in range(2):
            sl = slice(h * hd, (h + 1) * hd)
            kc_s[h] = kc[:, sl]
            vc_s[h] = values_and_ones(vc_ref[...], h)
            tiles = [jnp.where(in_win, t2_ref[h, dr], NEG_INF) for dr in range(2 * NA_WIN_R - 1)]
            for v in range(NA_WIN_R):
                for j in range(NA_WIN_R):
                    bias_s[h, v, :, j * GRID_W:(j + 1) * GRID_W] = tiles[NA_WIN_R - 1 - v + j]

        eye = (lax.broadcasted_iota(jnp.int32, (2 * hd, 2 * hd), 0)
               == lax.broadcasted_iota(jnp.int32, (2 * hd, 2 * hd), 1)).astype(F32).astype(BF16)

        def prep(i, carry):
            r0 = pl.multiple_of(i * NA_PREP_ROWS, NA_PREP_ROWS)
            rws = pl.ds(r0, NA_PREP_ROWS)
            cos, sin = _rope_tables(rowtab_ref, coltab_ref, i * (NA_PREP_ROWS // GRID_W), NA_PREP_ROWS // GRID_W,
                                    row_lane)
            kr = _rope(k_ref[rws, :], cos, sin, first_half)
            kr_odd = jnp.concatenate([tail_s[...], kr[:NA_PREP_ROWS - GRID_W]], axis=0)
            tail_s[...] = kr[NA_PREP_ROWS - GRID_W:]
            krt = [_dot_nt(eye, kr.astype(BF16)).astype(BF16),
                   _dot_nt(eye, kr_odd.astype(BF16)).astype(BF16)]
            vv = v_ref[rws, :]
            for h in range(2):
                sl = slice(h * hd, (h + 1) * hd)
                for par in range(2):
                    for c in range(NA_PREP_ROWS // NA_KEY_TILE):
                        kt_s[h, par, i * (NA_PREP_ROWS // NA_KEY_TILE) + c] = (
                            krt[par][sl, c * NA_KEY_TILE:(c + 1) * NA_KEY_TILE])
                v_s[h, rws, :] = values_and_ones(vv, h)
            return carry

        lax.fori_loop(0, s_len // NA_PREP_ROWS, prep, 0, unroll=4)

    tq = NA_ROWS_PER_STEP * GRID_W
    q = q_ref[...] * scale
    cos, sin = _rope_tables(rowtab_ref, coltab_ref, rblk * NA_ROWS_PER_STEP, NA_ROWS_PER_STEP, row_lane)
    qr = _rope(q, cos, sin, first_half)
    qb = q.astype(BF16)
    qrb = qr.astype(BF16)
    rws = [slice(rr * GRID_W, (rr + 1) * GRID_W) for rr in range(NA_ROWS_PER_STEP)]
    par, slot0, key0, bidx = [], [], [], []
    for rr in range(NA_ROWS_PER_STEP):
        r = rblk * NA_ROWS_PER_STEP + rr
        rs = jnp.clip(r - NA_WIN_R // 2, 0, rows - NA_WIN_R)
        par.append(rs & 1)
        slot0.append(lax.shift_right_logical(rs, 1) + (rs & 1))
        key0.append(pl.multiple_of(rs * GRID_W, GRID_W))
        bidx.append(r - rs)

    def scores(h):
        sl = slice(h * hd, (h + 1) * hd)
        qrb_h = qrb[:, sl]
        s_ctx_all = _dot_nt(qb[:, sl], kc_s[h])
        s_win = []
        for rr in range(NA_ROWS_PER_STEP):
            kt = kt_s[h, par[rr], pl.ds(slot0[rr], NA_SPAN // NA_KEY_TILE)]
            kt = jnp.concatenate([kt[c] for c in range(NA_SPAN // NA_KEY_TILE)], axis=-1)
            s_win.append(_dot(qrb_h[rws[rr]], kt))
        return s_win, s_ctx_all

    def softmax(h, s_win, s_ctx_all):
        e_win, e_ctx = [], []
        for rr in range(NA_ROWS_PER_STEP):
            sw = s_win[rr] + bias_s[h, bidx[rr]]
            sc = s_ctx_all[rws[rr]]
            m = jnp.max(_fold_lanes(jnp.maximum, sw, sc), axis=-1, keepdims=True)
            e_win.append(jnp.exp(sw - m).astype(BF16))
            e_ctx.append(jnp.exp(sc - m).astype(BF16))
        return e_win, e_ctx

    def values(h, e_win, e_ctx):
        o_win = []
        for rr in range(NA_ROWS_PER_STEP):
            o_win.append(_dot(e_win[rr], v_s[h, pl.ds(key0[rr], NA_SPAN), :]))
        o = jnp.concatenate(o_win, axis=0) + _dot(jnp.concatenate(e_ctx, axis=0), vc_s[h])
        return o[:, :hd] * (1.0 / o[:, hd:hd + 1])

    s0 = scores(0)
    s1 = scores(1)
    p0 = softmax(0, *s0)
    o0 = values(0, *p0)
    p1 = softmax(1, *s1)
    o1 = values(1, *p1)
    o_ref[...] = jnp.concatenate([o0, o1], axis=-1).astype(o_ref.dtype)


def _na_tables(rpb, s):
    half = NA_HD // 2
    inv = jnp.power(ROPE_THETA, -jnp.arange(0, half, 2, dtype=F32) / half)

    def tables(n):
        ang = jnp.arange(n, dtype=F32)[:, None] * inv[None, :]
        reps = 2 * NA_HD // half
        return jnp.stack([jnp.tile(jnp.cos(ang), (1, 2 * reps)),
                          jnp.tile(jnp.concatenate([-jnp.sin(ang), jnp.sin(ang)], axis=-1), (1, reps))])

    rowtab, coltab = tables(s // GRID_W), tables(GRID_W)

    pad = GRID_W - NA_WIN_C
    width = 2 * GRID_W
    rp = jnp.pad(rpb.astype(F32), ((0, 0), (0, 0), (pad, pad + 2)), mode="edge")
    skew = jnp.tile(rp, (1, 1, GRID_W + 1))[:, :, GRID_W - 1:GRID_W - 1 + GRID_W * width]
    t2 = skew.reshape(rp.shape[0], rp.shape[1], GRID_W, width)[:, :, :, :GRID_W]
    return rowtab, coltab, t2


def _natten(p, pc, rowtab, coltab, t2):
    _, b, s, w = p.shape
    ctx_len = pc.shape[2]
    rows = s // GRID_W
    assert rows >= NA_WIN_R and rows % NA_ROWS_PER_STEP == 0
    tq = NA_ROWS_PER_STEP * GRID_W
    hw = 2 * NA_HD
    nhp = w // hw
    kern = functools.partial(_natten_kernel, rows=rows)
    return pl.pallas_call(
        kern,
        grid=(b, nhp, rows // NA_ROWS_PER_STEP),
        in_specs=[pl.BlockSpec((None, None, tq, hw), lambda bi, hp, r: (SEC_NQ, bi, r, hp)),
                  pl.BlockSpec((None, None, s, hw), lambda bi, hp, r: (SEC_NK, bi, 0, hp)),
                  pl.BlockSpec((None, None, s, hw), lambda bi, hp, r: (SEC_NV, bi, 0, hp)),
                  pl.BlockSpec((None, None, ctx_len, hw), lambda bi, hp, r: (CTX_SECTIONS.index(SEC_NK), bi, 0, hp)),
                  pl.BlockSpec((None, None, ctx_len, hw), lambda bi, hp, r: (CTX_SECTIONS.index(SEC_NV), bi, 0, hp)),
                  pl.BlockSpec((2, rows, hw), lambda bi, hp, r: (0, 0, 0)),
                  pl.BlockSpec((2, GRID_W, hw), lambda bi, hp, r: (0, 0, 0)),
                  pl.BlockSpec((2, 2 * NA_WIN_R - 1, GRID_W, GRID_W), lambda bi, hp, r: (hp, 0, 0, 0))],
        out_specs=pl.BlockSpec((None, tq, hw), lambda bi, hp, r: (bi, r, hp)),
        out_shape=jax.ShapeDtypeStruct((b, s, w), BF16),
        scratch_shapes=[pltpu.VMEM((2, 2, s // NA_KEY_TILE, NA_HD, NA_KEY_TILE), BF16),
                        pltpu.VMEM((2, s, hw), BF16),
                        pltpu.VMEM((2, ctx_len, NA_HD), BF16), pltpu.VMEM((2, ctx_len, hw), BF16),
                        pltpu.VMEM((2, NA_WIN_R, GRID_W, NA_SPAN), F32),
                        pltpu.VMEM((GRID_W, hw), F32)],
        compiler_params=_cparams(("arbitrary", "arbitrary", "arbitrary")),
        name="natten",
    )(p, p, p, pc, pc, rowtab, coltab, t2)


def _route(logits_t, rbias):
    e, t = logits_t.shape
    gsz = e // N_GROUPS
    scores = jax.nn.sigmoid(logits_t)
    sel = scores + rbias
    neg = -jnp.inf
    sub = lax.broadcasted_iota(jnp.int32, (gsz, t), 0).astype(F32)
    gscore = []
    for g in range(N_GROUPS):
        grp = sel[g * gsz:(g + 1) * gsz, :]
        m1 = jnp.max(grp, axis=0, keepdims=True)
        first = jnp.min(jnp.where(grp == m1, sub, float(gsz)), axis=0, keepdims=True)
        m2 = jnp.max(jnp.where(sub == first, neg, grp), axis=0, keepdims=True)
        gscore.append(m1 + m2)
    masked = []
    for g in range(N_GROUPS):
        rank = jnp.zeros((1, t), F32)
        for g2 in range(N_GROUPS):
            if g2 == g:
                continue
            if g2 < g:
                ahead = gscore[g2] >= gscore[g]
            else:
                ahead = gscore[g2] > gscore[g]
            rank = rank + jnp.where(ahead, 1.0, 0.0)
        masked.append(jnp.where(rank < TOPK_GROUPS, sel[g * gsz:(g + 1) * gsz, :], neg))
    work = jnp.concatenate(masked, axis=0)
    eidx = lax.broadcasted_iota(jnp.int32, (e, t), 0).astype(F32)
    idxs, ws = [], []
    chosen = jnp.zeros((e, t), F32)
    for _ in range(TOP_K):
        m = jnp.max(work, axis=0, keepdims=True)
        first = jnp.min(jnp.where(work == m, eidx, float(e)), axis=0, keepdims=True)
        pick = eidx == first
        idxs.append(first)
        ws.append(jnp.sum(jnp.where(pick, scores, 0.0), axis=0, keepdims=True))
        chosen = jnp.where(pick, 1.0, chosen)
        work = jnp.where(pick, neg, work)
    w = jnp.concatenate(ws, axis=0)
    w = w / jnp.sum(w, axis=0, keepdims=True) * ROUTED_SCALE
    return jnp.concatenate(idxs, axis=0).astype(jnp.int32), w, chosen


MERGE_TOK = 512
MERGE_SUB = 256


def _merge_kernel(of_ref, ob_ref, og_ref, yna_ref, ga_ref, gb_ref, x_ref, g1_ref, sh2_ref, sc2_ref,
                  hgg_ref, ln1g_ref, ln1b_ref, wa_ref, wb_ref, wo_ref, wr_ref, rb_ref,
                  x1_ref, h2_ref, topi_ref, topw_ref, cnt_ref, *, alpha):
    tm = x_ref.shape[0]
    subs = [slice(i * MERGE_SUB, (i + 1) * MERGE_SUB) for i in range(tm // MERGE_SUB)]

    def branches(rows):
        o = of_ref[rows, :] + ob_ref[rows, :]
        parts = []
        for h in range(HG_HEADS):
            oh = o[:, h * HG_DK:(h + 1) * HG_DK]
            parts.append(oh * lax.rsqrt(jnp.mean(oh * oh, axis=-1, keepdims=True) + LN_EPS))
        y_hg = jnp.concatenate(parts, axis=-1) * hgg_ref[...] * _silu(og_ref[rows, :])
        return _dot(y_hg.astype(BF16), wa_ref[...]), _dot(yna_ref[rows, :], wb_ref[...])

    def out_proj(rows, ya, yb):
        t = jax.nn.sigmoid(ga_ref[rows, :]) * ya + jax.nn.sigmoid(gb_ref[rows, :]) * yb
        return _dot(t.astype(BF16), wo_ref[...])

    def norms_router(rows, i, y):
        x1 = _normalize(alpha * x_ref[rows, :] + g1_ref[...] * y) * ln1g_ref[...] + ln1b_ref[...]
        x1_ref[rows, :] = x1
        h2 = _normalize(x1) * (1.0 + sc2_ref[...]) + sh2_ref[...]
        h2_ref[rows, :] = _pack_words(h2)
        hh, hm, hl = _split3(h2)
        wh, wm, wl = _split3(wr_ref[...])
        return (_dot_nt(wh, hh) + _dot_nt(wh, hm) + _dot_nt(wm, hh)
                + _dot_nt(wh, hl) + _dot_nt(wl, hh) + _dot_nt(wm, hm))

    ab = [branches(rows) for rows in subs]
    ys = [out_proj(rows, *ab[i]) for i, rows in enumerate(subs)]
    logits = [norms_router(rows, i, ys[i]) for i, rows in enumerate(subs)]

    @pl.when((pl.program_id(0) == 0) & (pl.program_id(1) == 0))
    def _():
        cnt_ref[...] = jnp.zeros_like(cnt_ref)

    for i, rows in enumerate(subs):
        topi, topw, chosen = _route(logits[i], rb_ref[...])
        topi_ref[:, rows] = topi
        topw_ref[:, rows] = topw
        cnt_ref[...] += jnp.sum(chosen, axis=1, keepdims=True)


def _merge(o_f, o_b, p, y_na, x, g1, sh2, sc2, hg_norm_g, ln1_g, ln1_b, w_a, w_b, w_o, w_router_t, router_bias,
           alpha):
    b, s, d = x.shape
    tm = min(MERGE_TOK, s)
    e = w_router_t.shape[0]
    tok = lambda bi, i: (bi, i, 0)
    blk = pl.BlockSpec((None, tm, d), tok)

    def sec(section):
        return pl.BlockSpec((None, None, tm, d), lambda bi, i: (section, bi, i, 0))

    mod = pl.BlockSpec((None, 1, d), lambda bi, i: (bi, 0, 0))
    vec = pl.BlockSpec((1, d), lambda bi, i: (0, 0))
    mat = pl.BlockSpec((d, d), lambda bi, i: (0, 0), pipeline_mode=pl.Buffered(1))
    return pl.pallas_call(
        functools.partial(_merge_kernel, alpha=alpha),
        grid=(b, s // tm),
        in_specs=[blk, blk, sec(SEC_OG), blk, sec(SEC_GA), sec(SEC_GB), blk, mod, mod, mod,
                  vec, vec, vec, mat, mat, mat,
                  pl.BlockSpec((e, d), lambda bi, i: (0, 0)),
                  pl.BlockSpec((e, 1), lambda bi, i: (0, 0))],
        out_specs=[blk,
                   pl.BlockSpec((tm, d // 2), lambda bi, i: (bi * (s // tm) + i, 0)),
                   pl.BlockSpec((None, TOP_K, tm), lambda bi, i: (bi, 0, i)),
                   pl.BlockSpec((None, TOP_K, tm), lambda bi, i: (bi, 0, i)),
                   pl.BlockSpec((e, 128), lambda bi, i: (0, 0))],
        out_shape=[jax.ShapeDtypeStruct((b, s, d), F32),
                   jax.ShapeDtypeStruct((b * s, d // 2), U32),
                   jax.ShapeDtypeStruct((b, TOP_K, s), jnp.int32), jax.ShapeDtypeStruct((b, TOP_K, s), F32),
                   jax.ShapeDtypeStruct((e, 128), F32)],
        compiler_params=_cparams(("arbitrary", "arbitrary")),
        name="merge",
    )(o_f, o_b, p, y_na, p, p, x, g1, sh2, sc2, hg_norm_g.reshape(1, d), ln1_g.reshape(1, d),
      ln1_b.reshape(1, d), w_a, w_b, w_o, w_router_t, router_bias.reshape(e, 1))


MOE_TILE = 512
MOE_TOK = 512


def _plan_kernel(topi_ref, off_ref, dest_ref, carry_ref):
    @pl.when(pl.program_id(0) == 0)
    def _():
        carry_ref[...] = jnp.zeros_like(carry_ref)

    topi = topi_ref[...]
    tok = topi.shape[1]
    eidx = lax.broadcasted_iota(jnp.int32, (N_EXPERTS, tok), 0)
    hits = [eidx == topi[k:k + 1, :] for k in range(TOP_K)]
    m = jnp.zeros((N_EXPERTS, tok), F32)
    for hit in hits:
        m = jnp.where(hit, 1.0, m)
    before = (lax.broadcasted_iota(jnp.int32, (tok, tok), 0)
              < lax.broadcasted_iota(jnp.int32, (tok, tok), 1)).astype(F32).astype(BF16)
    row = off_ref[...] + carry_ref[...] + _dot(m.astype(BF16), before)
    dest = [jnp.sum(jnp.where(hit, row, 0.0), axis=0, keepdims=True) for hit in hits]
    dest_ref[...] = jnp.concatenate(dest, axis=0).astype(jnp.int32)
    carry_ref[...] += jnp.sum(m, axis=1, keepdims=True)


def _plan(topi, seg_off):
    b, k, s = topi.shape
    per_b = s // MOE_TOK
    blk = pl.BlockSpec((None, k, MOE_TOK), lambda i: (i // per_b, 0, i % per_b))
    return pl.pallas_call(
        _plan_kernel,
        grid=(b * per_b,),
        in_specs=[blk, pl.BlockSpec((N_EXPERTS, 1), lambda i: (0, 0))],
        out_specs=blk,
        out_shape=jax.ShapeDtypeStruct((b, k, s), jnp.int32),
        scratch_shapes=[pltpu.VMEM((N_EXPERTS, 1), F32)],
        compiler_params=_cparams(("arbitrary",)),
        name="plan",
    )(topi, seg_off.astype(F32).reshape(N_EXPERTS, 1))


SC_WINDOW = 128


def _sc_workers():
    info = plsc.get_sparse_core_info()
    return info.num_cores, info.num_cores * info.num_subcores


def _scatter_rows(src, idx, zero_idx, n_rows):
    t, w = src.shape
    m, mz = idx.shape[0], zero_idx.shape[0]
    n_cores, n_workers = _sc_workers()
    per_worker, per_worker_z = t // n_workers, mz // n_workers
    assert m % t == 0
    assert per_worker * n_workers == t and per_worker % SC_WINDOW == 0
    assert per_worker_z * n_workers == mz and per_worker_z % SC_WINDOW == 0
    mesh = plsc.VectorSubcoreMesh(core_axis_name="core", subcore_axis_name="subcore")

    @functools.partial(
        pl.kernel, mesh=mesh, out_type=jax.ShapeDtypeStruct((n_rows, w), src.dtype),
        scratch_types=[pltpu.VMEM((SC_WINDOW,), jnp.int32), pltpu.VMEM((SC_WINDOW, w), src.dtype),
                       pltpu.SemaphoreType.DMA])
    def scatter(src_hbm, idx_hbm, zeros_hbm, zero_idx_hbm, out_hbm, idx_v, rows_v, sem):
        worker = lax.axis_index("subcore") * n_cores + lax.axis_index("core")

        @pl.loop(0, per_worker // SC_WINDOW)
        def _(step):
            first = pl.multiple_of(worker * per_worker + step * SC_WINDOW, SC_WINDOW)
            pltpu.sync_copy(src_hbm.at[pl.ds(first, SC_WINDOW)], rows_v)
            for copy in range(m // t):
                pltpu.sync_copy(idx_hbm.at[pl.ds(copy * t + first, SC_WINDOW)], idx_v)
                pltpu.async_copy(rows_v, out_hbm.at[idx_v], sem).wait()

        pltpu.sync_copy(zeros_hbm, rows_v)

        @pl.loop(0, per_worker_z // SC_WINDOW)
        def _(step):
            base = pl.multiple_of(worker * per_worker_z + step * SC_WINDOW, SC_WINDOW)
            pltpu.sync_copy(zero_idx_hbm.at[pl.ds(base, SC_WINDOW)], idx_v)
            pltpu.async_copy(rows_v, out_hbm.at[idx_v], sem).wait()

    return scatter(src, idx, jnp.zeros((SC_WINDOW, w), src.dtype), zero_idx)


EXPERT_RING = 3


def _experts_kernel(te_ref, tb_ref, nt_ref, seg_ref, nxt_ref, xs_ref, wg_ref, wu_ref, wd_ref, ys_ref,
                    xbuf, sem, wg_f32, wu_f32, wd_f32, wg_s, wu_s, wd_s, wsem):
    i = pl.program_id(0)
    n_tiles = nt_ref[0]

    def weight_copies(expert, slot):
        return [pltpu.make_async_copy(src.at[expert], dst.at[slot], wsem.at[slot, n])
                for n, (src, dst) in enumerate(((wg_ref, wg_f32), (wu_ref, wu_f32), (wd_ref, wd_f32)))]

    @pl.when(i == 0)
    def _():
        for c in weight_copies(te_ref[0], 0):
            c.start()

    first = (i == 0) | (seg_ref[i] != seg_ref[jnp.maximum(i - 1, 0)])

    @pl.when((i < n_tiles) & first)
    def _():
        slot = seg_ref[i] & 1
        for c in weight_copies(te_ref[i], slot):
            c.wait()
        wg_s[...] = wg_f32[slot].astype(BF16)
        wu_s[...] = wu_f32[slot].astype(BF16)
        wd_s[...] = wd_f32[slot].astype(BF16)

        @pl.when(nxt_ref[i] >= 0)
        def _():
            for c in weight_copies(nxt_ref[i], 1 - slot):
                c.start()

    def tile_copy(j):
        slot = lax.rem(j, EXPERT_RING)
        row0 = pl.multiple_of(tb_ref[j] * MOE_TILE, MOE_TILE)
        return pltpu.make_async_copy(xs_ref.at[pl.ds(row0, MOE_TILE), :], xbuf.at[slot], sem.at[slot])

    @pl.when(i == 0)
    def _():
        for j in range(EXPERT_RING - 1):
            @pl.when(j < n_tiles)
            def _():
                tile_copy(j).start()

    ahead = i + (EXPERT_RING - 1)

    @pl.when(ahead < n_tiles)
    def _():
        tile_copy(ahead).start()

    @pl.when(i < n_tiles)
    def _():
        tile_copy(i).wait()
        x = _unpack_words(xbuf[lax.rem(i, EXPERT_RING)]).astype(BF16)
        act = _silu(_dot(x, wg_s[...])) * _dot(x, wu_s[...])
        ys_ref[...] = _pack_words(_dot(act.astype(BF16), wd_s[...]))


def _experts(xs, tile_expert, tile_block, n_tiles, tile_segment, next_expert, wg, wu, wd):
    d, f = wg.shape[1], wg.shape[2]
    anywhere = pl.BlockSpec(memory_space=pl.ANY)
    grid_spec = pltpu.PrefetchScalarGridSpec(
        num_scalar_prefetch=5,
        grid=(xs.shape[0] // MOE_TILE,),
        in_specs=[anywhere, anywhere, anywhere, anywhere],
        out_specs=pl.BlockSpec((MOE_TILE, d // 2), lambda i, te, tb, nt, seg, nxt: (tb[i], 0)),
        scratch_shapes=[pltpu.VMEM((EXPERT_RING, MOE_TILE, d // 2), U32), pltpu.SemaphoreType.DMA((EXPERT_RING,)),
                        pltpu.VMEM((2, d, f), F32), pltpu.VMEM((2, d, f), F32), pltpu.VMEM((2, f, d), F32),
                        pltpu.VMEM((d, f), BF16), pltpu.VMEM((d, f), BF16), pltpu.VMEM((f, d), BF16),
                        pltpu.SemaphoreType.DMA((2, 3))],
    )
    return pl.pallas_call(
        _experts_kernel,
        grid_spec=grid_spec,
        out_shape=jax.ShapeDtypeStruct(xs.shape, U32),
        compiler_params=_cparams(("arbitrary",)),
        name="experts",
    )(tile_expert, tile_block, n_tiles, tile_segment, next_expert, xs, wg, wu, wd)


def _gather_rows(table, idx):
    m = idx.shape[0]
    w = table.shape[1]
    win = SC_WINDOW // 2
    n_cores, n_workers = _sc_workers()
    per_worker = m // n_workers
    n_pairs = per_worker // (2 * win)
    assert per_worker * n_workers == m and n_pairs * 2 * win == per_worker
    mesh = plsc.VectorSubcoreMesh(core_axis_name="core", subcore_axis_name="subcore")

    @functools.partial(
        pl.kernel, mesh=mesh, out_type=jax.ShapeDtypeStruct((m, w), table.dtype),
        scratch_types=[pltpu.VMEM((win,), jnp.int32), pltpu.VMEM((win,), jnp.int32),
                       pltpu.VMEM((win, w), table.dtype), pltpu.VMEM((win, w), table.dtype),
                       pltpu.SemaphoreType.DMA, pltpu.SemaphoreType.DMA])
    def gather(table_hbm, idx_hbm, out_hbm, idx_a, idx_b, rows_a, rows_b, sem_a, sem_b):
        worker = lax.axis_index("subcore") * n_cores + lax.axis_index("core")
        start = worker * per_worker

        def request(first, idx_v, rows_v, sem):
            pltpu.sync_copy(idx_hbm.at[pl.ds(first, win)], idx_v)
            pltpu.async_copy(table_hbm.at[idx_v], rows_v, sem)

        def deliver(first, idx_v, rows_v, sem):
            pltpu.make_async_copy(table_hbm.at[idx_v], rows_v, sem).wait()
            pltpu.sync_copy(rows_v, out_hbm.at[pl.ds(first, win)])

        request(pl.multiple_of(start, win), idx_a, rows_a, sem_a)

        @pl.loop(0, n_pairs)
        def _(pair):
            first_a = pl.multiple_of(start + pair * 2 * win, win)
            first_b = pl.multiple_of(first_a + win, win)
            request(first_b, idx_b, rows_b, sem_b)
            deliver(first_a, idx_a, rows_a, sem_a)

            @pl.when(pair + 1 < n_pairs)
            def _():
                request(pl.multiple_of(first_b + win, win), idx_a, rows_a, sem_a)

            deliver(first_b, idx_b, rows_b, sem_b)

    return gather(table, idx)


def _combine_kernel(rows_ref, topw_ref, h_ref, x1_ref, g2_ref, sg_ref, su_ref, sd_ref, ln2g_ref, ln2b_ref,
                    o_ref, *, alpha):
    h = _unpack_words(h_ref[...]).astype(BF16)
    act = _silu(_dot(h, sg_ref[...])) * _dot(h, su_ref[...])
    y = _dot(act.astype(BF16), sd_ref[...])
    w = topw_ref[...].T
    for k in range(TOP_K):
        y = y + w[:, k:k + 1] * _unpack_words(rows_ref[k])
    o_ref[...] = _normalize(alpha * x1_ref[...] + g2_ref[...] * y) * ln2g_ref[...] + ln2b_ref[...]


def _combine(gathered, topw, h2p, x1, g2, sg, su, sd, ln2_g, ln2_b, alpha):
    t, d = x1.shape
    b, k, s = topw.shape
    per_b = s // MOE_TOK
    fs = sg.shape[1]
    rows = pl.BlockSpec((MOE_TOK, d), lambda i: (i, 0))
    packed = pl.BlockSpec((MOE_TOK, d // 2), lambda i: (i, 0))
    vec = pl.BlockSpec((1, d), lambda i: (0, 0))
    return pl.pallas_call(
        functools.partial(_combine_kernel, alpha=alpha),
        grid=(t // MOE_TOK,),
        in_specs=[pl.BlockSpec((k, MOE_TOK, d // 2), lambda i: (0, i, 0)),
                  pl.BlockSpec((None, k, MOE_TOK), lambda i: (i // per_b, 0, i % per_b)),
                  packed, rows,
                  pl.BlockSpec((None, 1, d), lambda i: (i // per_b, 0, 0)),
                  pl.BlockSpec((d, fs), lambda i: (0, 0)),
                  pl.BlockSpec((d, fs), lambda i: (0, 0)),
                  pl.BlockSpec((fs, d), lambda i: (0, 0)),
                  vec, vec],
        out_specs=rows,
        out_shape=jax.ShapeDtypeStruct((t, d), F32),
        compiler_params=_cparams(("arbitrary",)),
        name="combine",
    )(gathered, topw, h2p, x1, g2, sg, su, sd, ln2_g.reshape(1, d), ln2_b.reshape(1, d))


def _moe(h2p, topi, topw, cnt, x1, g2, wg, wu, wd, sg, su, sd, ln2_g, ln2_b, alpha):
    b, s, d = x1.shape
    t = b * s
    cnt = cnt[:, 0].astype(jnp.int32)
    tiles_e = (cnt + (MOE_TILE - 1)) // MOE_TILE
    tiles_cum = jnp.cumsum(tiles_e)
    seg_off = (tiles_cum - tiles_e) * MOE_TILE
    n_tiles_max = t * TOP_K // MOE_TILE + N_EXPERTS
    tile_block = jnp.minimum(jnp.arange(n_tiles_max, dtype=jnp.int32), tiles_cum[-1] - 1)
    tile_expert = jnp.sum((tiles_cum[None, :] <= tile_block[:, None]).astype(jnp.int32), axis=1)
    n_tiles = tiles_cum[-1:].astype(jnp.int32)
    present = tiles_e > 0
    seg_of_expert = jnp.cumsum(present.astype(jnp.int32)) - 1
    later = jnp.where(present, jnp.arange(N_EXPERTS, dtype=jnp.int32), N_EXPERTS)
    next_present = jnp.concatenate([lax.cummin(later, reverse=True)[1:], jnp.full((1,), N_EXPERTS, jnp.int32)])
    next_present = jnp.where(next_present < N_EXPERTS, next_present, -1)
    tile_segment = seg_of_expert[tile_expert].astype(jnp.int32)
    next_expert = next_present[tile_expert].astype(jnp.int32)

    dest = jnp.transpose(_plan(topi, seg_off), (1, 0, 2)).reshape(TOP_K * t)
    j = jnp.arange(MOE_TILE, dtype=jnp.int32)[None, :]
    n_pad = (tiles_e * MOE_TILE - cnt)[:, None]
    spare = (n_tiles_max - 1) * MOE_TILE + j
    zero_idx = jnp.where(j < n_pad, (seg_off + cnt)[:, None] + j, spare).reshape(N_EXPERTS * MOE_TILE)
    xs = _scatter_rows(h2p, dest, zero_idx.astype(jnp.int32), n_tiles_max * MOE_TILE)
    ys = _experts(xs, tile_expert, tile_block, n_tiles, tile_segment, next_expert, wg, wu, wd)
    gathered = _gather_rows(ys, dest)
    out = _combine(gathered.reshape(TOP_K, t, d // 2), topw, h2p, x1.reshape(t, d), g2, sg, su, sd,
                   ln2_g, ln2_b, alpha)
    return out.reshape(b, s, d)


def kernel(x, c, ctx, c_ctx, w_ada, b_ada, w_in, hg_lb_fwd, hg_lb_bwd, hg_norm_g, na_rpb, w_branch_a, w_branch_b, w_out, ln1_g, ln1_b, w_router, router_bias, w_e_gate, w_e_up, w_e_down, w_sh_gate, w_sh_up, w_sh_down, ln2_g, ln2_b):
    depth = w_ada.shape[0]
    assert depth == 1, "single-layer block"
    b, s, d = x.shape
    alpha = (2.0 * depth) ** 0.25
    l = 0
    lb_fwd = jnp.cumsum(jax.nn.softmax(hg_lb_fwd.astype(F32), axis=0), axis=0)[l]
    lb_bwd = jnp.cumsum(jax.nn.softmax(hg_lb_bwd.astype(F32), axis=0), axis=0)[l]

    cond_rows = jnp.concatenate([c, c_ctx[None, :], jnp.zeros((8 - b - 1, d), F32)], axis=0)
    mod = _ada(cond_rows, w_ada[l], b_ada[l])
    sh1, sc1, g1, sh2, sc2, g2 = [m[:b, None, :] for m in jnp.split(mod, 6, axis=-1)]
    csh1, csc1 = [jnp.broadcast_to(m[b:b + 1, None, :], (b, 1, d)) for m in jnp.split(mod, 6, axis=-1)[:2]]

    w_in_b = w_in[l].astype(BF16)
    p = _inproj(x, sh1, sc1, w_in_b, tuple(range(N_SECTIONS)))
    pc = _inproj(ctx, csh1, csc1, w_in_b, CTX_SECTIONS)

    o_f, o_b = _hgrn(p, pc, lb_fwd, lb_bwd)
    y_na = _natten(p, pc, *_na_tables(na_rpb[l], s))

    x1, h2, topi, topw, cnt = _merge(o_f, o_b, p, y_na, x, g1, sh2, sc2, hg_norm_g[l], ln1_g[l], ln1_b[l],
                                     w_branch_a[l].astype(BF16), w_branch_b[l].astype(BF16),
                                     w_out[l].astype(BF16), w_router[l].T, router_bias[l], alpha)

    return _moe(h2, topi, topw, cnt, x1, g2,
                w_e_gate[l], w_e_up[l], w_e_down[l],
                w_sh_gate[l].astype(BF16), w_sh_up[l].astype(BF16), w_sh_down[l].astype(BF16),
                ln2_g[l], ln2_b[l], alpha)
```

```python
import functools

import numpy as np
import jax
import jax.numpy as jnp
from jax import lax
from jax.experimental import pallas as pl
from jax.experimental.pallas import tpu as pltpu
from jax.experimental.pallas import tpu_sc as plsc

F32 = jnp.float32
BF16 = jnp.bfloat16

D_MODEL = 1024
GRID_W = 64
HG_HEADS = 8
HG_DK = 128
HG_CHUNK = 64
NA_HEADS = 16
NA_HD = 64
NA_WIN_R = 8
NA_WIN_C = 16
ROPE_THETA = 10000.0
NEG_INF = -1e30
N_EXPERTS = 64
EXPERT_DIM = 256
TOP_K = 8
N_GROUPS = 8
TOPK_GROUPS = 4
ROUTED_SCALE = 2.5
LN_EPS = 1e-6
N_SECTIONS = 10
SEC_Q, SEC_FF, SEC_FB, SEC_I, SEC_OG, SEC_NQ, SEC_NK, SEC_NV, SEC_GA, SEC_GB = range(10)
CTX_SECTIONS = (SEC_FF, SEC_FB, SEC_I, SEC_NK, SEC_NV)

VMEM_LIMIT = 56 * 1024 * 1024


def _cparams(sem):
    return pltpu.CompilerParams(dimension_semantics=sem, vmem_limit_bytes=VMEM_LIMIT)


def _normalize(x):
    mu = jnp.mean(x, axis=-1, keepdims=True)
    xc = x - mu
    var = jnp.mean(xc * xc, axis=-1, keepdims=True)
    return xc * lax.rsqrt(var + LN_EPS)


def _silu(x):
    return x * jax.nn.sigmoid(x)


def _dot(a, b):
    return jnp.dot(a, b, preferred_element_type=F32)


def _dot_nt(a, b):
    return lax.dot_general(a, b, (((1,), (1,)), ((), ())), preferred_element_type=F32)


def _dot_tn(a, b):
    return lax.dot_general(a, b, (((0,), (0,)), ((), ())), preferred_element_type=F32)


U32 = jnp.uint32


def _pack_words(x):
    half = x.shape[1] // 2
    lo = lax.bitcast_convert_type(x[:, :half].astype(BF16).astype(F32), U32) >> 16
    hi = lax.bitcast_convert_type(x[:, half:].astype(BF16).astype(F32), U32) & jnp.uint32(0xFFFF0000)
    return lo | hi


def _unpack_words(w):
    lo = lax.bitcast_convert_type(w << 16, F32)
    hi = lax.bitcast_convert_type(w & jnp.uint32(0xFFFF0000), F32)
    return jnp.concatenate([lo, hi], axis=-1)


def _split3(x):
    hi = x.astype(BF16)
    r1 = x - hi.astype(F32)
    mid = r1.astype(BF16)
    lo = (r1 - mid.astype(F32)).astype(BF16)
    return hi, mid, lo


def _ada_kernel(c_ref, w_ref, b_ref, o_ref):
    cond = _silu(c_ref[...])
    o_ref[...] = _dot(cond.astype(BF16), w_ref[...].astype(BF16)) + b_ref[...]


def _ada(cond_rows, w_ada, b_ada):
    r, d = cond_rows.shape
    n = w_ada.shape[1]
    tn = 1024
    return pl.pallas_call(
        _ada_kernel,
        grid=(n // tn,),
        in_specs=[pl.BlockSpec((r, d), lambda j: (0, 0)),
                  pl.BlockSpec((d, tn), lambda j: (0, j)),
                  pl.BlockSpec((1, tn), lambda j: (0, j))],
        out_specs=pl.BlockSpec((r, tn), lambda j: (0, j)),
        out_shape=jax.ShapeDtypeStruct((r, n), F32),
        compiler_params=_cparams(("arbitrary",)),
        name="ada",
    )(cond_rows, w_ada, b_ada.reshape(1, n))


INPROJ_TOK = 2048


def _inproj_kernel(x_ref, sh_ref, sc_ref, w_ref, o_ref, h_ref):
    @pl.when(pl.program_id(2) == 0)
    def _():
        h = _normalize(x_ref[...]) * (1.0 + sc_ref[...]) + sh_ref[...]
        h_ref[...] = h.astype(BF16)

    o_ref[...] = _dot(h_ref[...], w_ref[...])


def _inproj(x, shift, scale, w_in_bf16, sections):
    b, s, d = x.shape
    tm = min(INPROJ_TOK, s)
    nj = len(sections)

    def section(j):
        sec = sections[-1]
        for k in range(nj - 2, -1, -1):
            sec = jnp.where(j == k, sections[k], sec)
        return sec

    return pl.pallas_call(
        _inproj_kernel,
        grid=(b, s // tm, nj),
        in_specs=[pl.BlockSpec((None, tm, d), lambda bi, i, j: (bi, i, 0)),
                  pl.BlockSpec((None, 1, d), lambda bi, i, j: (bi, 0, 0)),
                  pl.BlockSpec((None, 1, d), lambda bi, i, j: (bi, 0, 0)),
                  pl.BlockSpec((d, d), lambda bi, i, j: (0, section(j)))],
        out_specs=pl.BlockSpec((None, None, tm, d), lambda bi, i, j: (j, bi, i, 0)),
        out_shape=jax.ShapeDtypeStruct((nj, b, s, d), F32),
        scratch_shapes=[pltpu.VMEM((tm, d), BF16)],
        compiler_params=_cparams(("arbitrary", "arbitrary", "arbitrary")),
        name="inproj",
    )(x, shift, scale, w_in_bf16)


def _hgrn_gates(q, fraw, v, lb, tri_bf16, last_row):
    f = lb + (1.0 - lb) * jax.nn.sigmoid(fraw)
    k = 1.0 - f
    lf = jnp.log(f)
    hi, mid, lo = _split3(lf)
    a = _dot(tri_bf16, hi) + _dot(tri_bf16, mid) + _dot(tri_bf16, lo)
    a_last = a[last_row:last_row + 1, :]
    kd = (k * jnp.exp(a_last - a)).astype(BF16)
    decay = jnp.exp(a_last)
    qa = kb = None
    if q is not None:
        qa = (_silu(q) * jnp.exp(a)).astype(BF16)
        kb = (k * jnp.exp(-a)).astype(BF16)
    return qa, kb, kd, v.astype(BF16), decay


def _hgrn_chunks(chunks, st_ref):
    first = []
    for d, ((qa, kb, kd, vb, decay), keep) in enumerate(chunks):
        for h in range(HG_HEADS):
            sl = slice(h * HG_DK, (h + 1) * HG_DK)
            st = st_ref[d, h]
            if qa is not None:
                first.append((_dot_nt(qa[:, sl], kb[:, sl]), _dot_nt(qa[:, sl], st.astype(BF16))))
            st_ref[d, h] = st * decay[:, sl] + _dot_tn(vb[:, sl], kd[:, sl])
    results = []
    for d, ((qa, kb, kd, vb, decay), keep) in enumerate(chunks):
        if qa is None:
            results.append(None)
            continue
        outs = []
        for h in range(HG_HEADS):
            sl = slice(h * HG_DK, (h + 1) * HG_DK)
            s_qk, o_state = first.pop(0)
            outs.append(_dot(jnp.where(keep, s_qk, 0.0).astype(BF16), vb[:, sl]) + o_state)
        results.append(jnp.concatenate(outs, axis=-1))
    return results


def _hgrn_kernel(qf_ref, ff_ref, if_ref, qb_ref, fb_ref, ib_ref, cff_ref, cfb_ref, ci_ref,
                 lbf_ref, lbb_ref, of_ref, ob_ref, st_ref, *, n_sub, n_ctx_sub):
    n = pl.program_id(1)
    c = HG_CHUNK
    row = lax.broadcasted_iota(jnp.int32, (c, c), 0)
    col = lax.broadcasted_iota(jnp.int32, (c, c), 1)
    keep_f = col <= row
    keep_b = col >= row
    tri_f = keep_f.astype(F32).astype(BF16)
    tri_b = keep_b.astype(F32).astype(BF16)
    lbf = lbf_ref[...]
    lbb = lbb_ref[...]

    @pl.when(n == 0)
    def _():
        st_ref[...] = jnp.zeros_like(st_ref)

        def body(i, carry):
            r0 = pl.multiple_of(i * c, c)
            r1 = pl.multiple_of((n_ctx_sub - 1 - i) * c, c)
            gf = _hgrn_gates(None, cff_ref[pl.ds(r0, c), :], ci_ref[pl.ds(r0, c), :], lbf, tri_f, c - 1)
            gb = _hgrn_gates(None, cfb_ref[pl.ds(r1, c), :], ci_ref[pl.ds(r1, c), :], lbb, tri_b, 0)
            _hgrn_chunks([(gf, keep_f), (gb, keep_b)], st_ref)
            return carry

        lax.fori_loop(0, n_ctx_sub, body, 0)

    @pl.when(n > 0)
    def _():
        def body(i, carry):
            r0 = pl.multiple_of(i * c, c)
            r1 = pl.multiple_of((n_sub - 1 - i) * c, c)
            gf = _hgrn_gates(qf_ref[pl.ds(r0, c), :], ff_ref[pl.ds(r0, c), :], if_ref[pl.ds(r0, c), :],
                             lbf, tri_f, c - 1)
            gb = _hgrn_gates(qb_ref[pl.ds(r1, c), :], fb_ref[pl.ds(r1, c), :], ib_ref[pl.ds(r1, c), :],
                             lbb, tri_b, 0)
            o_f, o_b = _hgrn_chunks([(gf, keep_f), (gb, keep_b)], st_ref)
            of_ref[pl.ds(r0, c), :] = o_f
            ob_ref[pl.ds(r1, c), :] = o_b
            return carry

        lax.fori_loop(0, n_sub, body, 0, unroll=True)


def _hgrn(p, pc, lb_fwd, lb_bwd):
    _, b, s, w = p.shape
    ctx_len = pc.shape[2]
    tb = min(512, s)
    nb = s // tb
    fwd = lambda bi, n: jnp.maximum(n - 1, 0)
    bwd = lambda bi, n: nb - 1 - jnp.maximum(n - 1, 0)

    def sec(section, blk):
        return pl.BlockSpec((None, None, tb, w), lambda bi, n: (section, bi, blk(bi, n), 0))

    def csec(section):
        return pl.BlockSpec((None, None, ctx_len, w), lambda bi, n: (CTX_SECTIONS.index(section), bi, 0, 0))

    vec = pl.BlockSpec((1, w), lambda bi, n: (0, 0))
    kern = functools.partial(_hgrn_kernel, n_sub=tb // HG_CHUNK, n_ctx_sub=ctx_len // HG_CHUNK)
    return pl.pallas_call(
        kern,
        grid=(b, nb + 1),
        in_specs=[sec(SEC_Q, fwd), sec(SEC_FF, fwd), sec(SEC_I, fwd),
                  sec(SEC_Q, bwd), sec(SEC_FB, bwd), sec(SEC_I, bwd),
                  csec(SEC_FF), csec(SEC_FB), csec(SEC_I), vec, vec],
        out_specs=[pl.BlockSpec((None, tb, w), lambda bi, n: (bi, fwd(bi, n), 0)),
                   pl.BlockSpec((None, tb, w), lambda bi, n: (bi, bwd(bi, n), 0))],
        out_shape=[jax.ShapeDtypeStruct((b, s, w), F32), jax.ShapeDtypeStruct((b, s, w), F32)],
        scratch_shapes=[pltpu.VMEM((2, HG_HEADS, HG_DK, HG_DK), F32)],
        compiler_params=_cparams(("arbitrary", "arbitrary")),
        name="hgrn",
    )(p, p, p, p, p, p, pc, pc, pc, lb_fwd.reshape(1, w), lb_bwd.reshape(1, w))


NA_ROWS_PER_STEP = 64
NA_PREP_ROWS = 512
NA_KEY_TILE = 128
NA_SPAN = NA_WIN_R * GRID_W


def _rope(t, cos, sin_signed, first_half):
    w = t.shape[-1]
    partner = jnp.where(first_half, pltpu.roll(t, w - 16, 1), pltpu.roll(t, 16, 1))
    return t * cos + partner * sin_signed


def _fold_lanes(op, *arrays):
    tiles = [a[:, c:c + 128] for a in arrays for c in range(0, a.shape[-1], 128)]
    acc = tiles[0]
    for t in tiles[1:]:
        acc = op(acc, t)
    return acc


def _rope_tables(rowtab_ref, coltab_ref, row0, n_rows, row_lane):
    out = []
    for i in range(2):
        rt = rowtab_ref[i, pl.ds(row0, n_rows), :]
        by_row = jnp.concatenate([jnp.broadcast_to(rt[r:r + 1, :], (GRID_W, rt.shape[1])) for r in range(n_rows)],
                                 axis=0)
        by_col = jnp.concatenate([coltab_ref[i]] * n_rows, axis=0)
        out.append(jnp.where(row_lane, by_row, by_col))
    return out


def _natten_kernel(q_ref, k_ref, v_ref, kc_ref, vc_ref, rowtab_ref, coltab_ref, rp_ref, o_ref,
                   kt_s, v_s, kc_s, vc_s, bias_s, tail_s, *, rows):
    rblk = pl.program_id(2)
    hd = NA_HD
    lane = lax.broadcasted_iota(jnp.int32, (1, 2 * hd), 1)
    first_half = (lane % 32) < 16
    row_lane = (lane % hd) < hd // 2
    scale = NA_HD ** -0.5

    def values_and_ones(v_pair, h):
        vh = v_pair if h == 0 else pltpu.roll(v_pair, hd, 1)
        return jnp.where(lane < hd, vh, jnp.where(lane == hd, 1.0, 0.0)).astype(BF16)

    @pl.when(rblk == 0)
    def _():
        kc = kc_ref[...].astype(BF16)
        qi = lax.broadcasted_iota(jnp.int32, (GRID_W, GRID_W), 0)
        ki = lax.broadcasted_iota(jnp.int32, (GRID_W, GRID_W), 1)
        cstart = jnp.clip(qi - NA_WIN_C // 2, 0, GRID_W - NA_WIN_C)
        in_win = (ki >= cstart) & (ki < cstart + NA_WIN_C)
        s_len = k_ref.shape[0]
        tail_s[...] = jnp.zeros_like(tail_s)
        for h in range(2):
            sl = slice(h * hd, (h + 1) * hd)
            kc_s[h] = kc[:, sl]
            vc_s[h] = values_and_ones(vc_ref[...], h)
            tiles = []
            for dr in range(2 * NA_WIN_R - 1):
                table = jnp.broadcast_to(rp_ref[h, dr:dr + 1, :], (GRID_W, 2 * GRID_W))
                skew = pltpu.roll(table, GRID_W + 1, 1, stride=1, stride_axis=0)
                tiles.append(jnp.where(in_win, skew[:, :GRID_W], NEG_INF))
            for v in range(NA_WIN_R):
                for j in range(NA_WIN_R):
                    bias_s[h, v, :, j * GRID_W:(j + 1) * GRID_W] = tiles[NA_WIN_R - 1 - v + j]

        eye = (lax.broadcasted_iota(jnp.int32, (2 * hd, 2 * hd), 0)
               == lax.broadcasted_iota(jnp.int32, (2 * hd, 2 * hd), 1)).astype(F32).astype(BF16)

        def prep(i, carry):
            r0 = pl.multiple_of(i * NA_PREP_ROWS, NA_PREP_ROWS)
            rws = pl.ds(r0, NA_PREP_ROWS)
            cos, sin = _rope_tables(rowtab_ref, coltab_ref, i * (NA_PREP_ROWS // GRID_W), NA_PREP_ROWS // GRID_W,
                                    row_lane)
            kr = _rope(k_ref[rws, :], cos, sin, first_half)
            kr_odd = jnp.concatenate([tail_s[...], kr[:NA_PREP_ROWS - GRID_W]], axis=0)
            tail_s[...] = kr[NA_PREP_ROWS - GRID_W:]
            krt = [_dot_nt(eye, kr.astype(BF16)).astype(BF16),
                   _dot_nt(eye, kr_odd.astype(BF16)).astype(BF16)]
            vv = v_ref[rws, :]
            for h in range(2):
                sl = slice(h * hd, (h + 1) * hd)
                for par in range(2):
                    for c in range(NA_PREP_ROWS // NA_KEY_TILE):
                        kt_s[h, par, i * (NA_PREP_ROWS // NA_KEY_TILE) + c] = (
                            krt[par][sl, c * NA_KEY_TILE:(c + 1) * NA_KEY_TILE])
                v_s[h, rws, :] = values_and_ones(vv, h)
            return carry

        lax.fori_loop(0, s_len // NA_PREP_ROWS, prep, 0, unroll=4)

    tq = NA_ROWS_PER_STEP * GRID_W
    q = q_ref[...] * scale
    cos, sin = _rope_tables(rowtab_ref, coltab_ref, rblk * NA_ROWS_PER_STEP, NA_ROWS_PER_STEP, row_lane)
    qr = _rope(q, cos, sin, first_half)
    qb = q.astype(BF16)
    qrb = qr.astype(BF16)
    rws = [slice(rr * GRID_W, (rr + 1) * GRID_W) for rr in range(NA_ROWS_PER_STEP)]
    par, slot0, key0, bidx = [], [], [], []
    for rr in range(NA_ROWS_PER_STEP):
        r = rblk * NA_ROWS_PER_STEP + rr
        rs = jnp.clip(r - NA_WIN_R // 2, 0, rows - NA_WIN_R)
        par.append(rs & 1)
        slot0.append(lax.shift_right_logical(rs, 1) + (rs & 1))
        key0.append(pl.multiple_of(rs * GRID_W, GRID_W))
        bidx.append(r - rs)

    def scores(h):
        sl = slice(h * hd, (h + 1) * hd)
        qrb_h = qrb[:, sl]
        s_ctx_all = _dot_nt(qb[:, sl], kc_s[h])
        s_win = []
        for rr in range(NA_ROWS_PER_STEP):
            kt = kt_s[h, par[rr], pl.ds(slot0[rr], NA_SPAN // NA_KEY_TILE)]
            kt = jnp.concatenate([kt[c] for c in range(NA_SPAN // NA_KEY_TILE)], axis=-1)
            s_win.append(_dot(qrb_h[rws[rr]], kt))
        return s_win, s_ctx_all

    def softmax(h, s_win, s_ctx_all):
        e_win, e_ctx = [], []
        for rr in range(NA_ROWS_PER_STEP):
            sw = s_win[rr] + bias_s[h, bidx[rr]]
            sc = s_ctx_all[rws[rr]]
            m = jnp.max(_fold_lanes(jnp.maximum, sw, sc), axis=-1, keepdims=True)
            e_win.append(jnp.exp(sw - m).astype(BF16))
            e_ctx.append(jnp.exp(sc - m).astype(BF16))
        return e_win, e_ctx

    def values(h, e_win, e_ctx):
        o_win = []
        for rr in range(NA_ROWS_PER_STEP):
            o_win.append(_dot(e_win[rr], v_s[h, pl.ds(key0[rr], NA_SPAN), :]))
        o = jnp.concatenate(o_win, axis=0) + _dot(jnp.concatenate(e_ctx, axis=0), vc_s[h])
        return o[:, :hd] * (1.0 / o[:, hd:hd + 1])

    s0 = scores(0)
    s1 = scores(1)
    p0 = softmax(0, *s0)
    o0 = values(0, *p0)
    p1 = softmax(1, *s1)
    o1 = values(1, *p1)
    o_ref[...] = jnp.concatenate([o0, o1], axis=-1).astype(o_ref.dtype)


def _na_tables(rpb, s):
    half = NA_HD // 2
    inv = jnp.power(ROPE_THETA, -jnp.arange(0, half, 2, dtype=F32) / half)

    def tables(n):
        ang = jnp.arange(n, dtype=F32)[:, None] * inv[None, :]
        reps = 2 * NA_HD // half
        return jnp.stack([jnp.tile(jnp.cos(ang), (1, 2 * reps)),
                          jnp.tile(jnp.concatenate([-jnp.sin(ang), jnp.sin(ang)], axis=-1), (1, reps))])

    rowtab, coltab = tables(s // GRID_W), tables(GRID_W)

    pad = GRID_W - NA_WIN_C
    rp = jnp.pad(rpb.astype(F32), ((0, 0), (0, 0), (pad, pad + 1)), mode="edge")
    return rowtab, coltab, rp


def _natten(p, pc, rowtab, coltab, rp):
    _, b, s, w = p.shape
    ctx_len = pc.shape[2]
    rows = s // GRID_W
    assert rows >= NA_WIN_R and rows % NA_ROWS_PER_STEP == 0
    tq = NA_ROWS_PER_STEP * GRID_W
    hw = 2 * NA_HD
    nhp = w // hw
    kern = functools.partial(_natten_kernel, rows=rows)
    return pl.pallas_call(
        kern,
        grid=(b, nhp, rows // NA_ROWS_PER_STEP),
        in_specs=[pl.BlockSpec((None, None, tq, hw), lambda bi, hp, r: (SEC_NQ, bi, r, hp)),
                  pl.BlockSpec((None, None, s, hw), lambda bi, hp, r: (SEC_NK, bi, 0, hp)),
                  pl.BlockSpec((None, None, s, hw), lambda bi, hp, r: (SEC_NV, bi, 0, hp)),
                  pl.BlockSpec((None, None, ctx_len, hw), lambda bi, hp, r: (CTX_SECTIONS.index(SEC_NK), bi, 0, hp)),
                  pl.BlockSpec((None, None, ctx_len, hw), lambda bi, hp, r: (CTX_SECTIONS.index(SEC_NV), bi, 0, hp)),
                  pl.BlockSpec((2, rows, hw), lambda bi, hp, r: (0, 0, 0)),
                  pl.BlockSpec((2, GRID_W, hw), lambda bi, hp, r: (0, 0, 0)),
                  pl.BlockSpec((2, 2 * NA_WIN_R - 1, 2 * GRID_W), lambda bi, hp, r: (hp, 0, 0))],
        out_specs=pl.BlockSpec((None, tq, hw), lambda bi, hp, r: (bi, r, hp)),
        out_shape=jax.ShapeDtypeStruct((b, s, w), BF16),
        scratch_shapes=[pltpu.VMEM((2, 2, s // NA_KEY_TILE, NA_HD, NA_KEY_TILE), BF16),
                        pltpu.VMEM((2, s, hw), BF16),
                        pltpu.VMEM((2, ctx_len, NA_HD), BF16), pltpu.VMEM((2, ctx_len, hw), BF16),
                        pltpu.VMEM((2, NA_WIN_R, GRID_W, NA_SPAN), F32),
                        pltpu.VMEM((GRID_W, hw), F32)],
        compiler_params=_cparams(("arbitrary", "arbitrary", "arbitrary")),
        name="natten",
    )(p, p, p, pc, pc, rowtab, coltab, rp)


def _route(logits_t, rbias):
    e, t = logits_t.shape
    gsz = e // N_GROUPS
    scores = jax.nn.sigmoid(logits_t)
    sel = scores + rbias
    neg = -jnp.inf
    sub = lax.broadcasted_iota(jnp.int32, (gsz, t), 0).astype(F32)
    gscore = []
    for g in range(N_GROUPS):
        grp = sel[g * gsz:(g + 1) * gsz, :]
        m1 = jnp.max(grp, axis=0, keepdims=True)
        first = jnp.min(jnp.where(grp == m1, sub, float(gsz)), axis=0, keepdims=True)
        m2 = jnp.max(jnp.where(sub == first, neg, grp), axis=0, keepdims=True)
        gscore.append(m1 + m2)
    masked = []
    for g in range(N_GROUPS):
        rank = jnp.zeros((1, t), F32)
        for g2 in range(N_GROUPS):
            if g2 == g:
                continue
            if g2 < g:
                ahead = gscore[g2] >= gscore[g]
            else:
                ahead = gscore[g2] > gscore[g]
            rank = rank + jnp.where(ahead, 1.0, 0.0)
        masked.append(jnp.where(rank < TOPK_GROUPS, sel[g * gsz:(g + 1) * gsz, :], neg))
    work = jnp.concatenate(masked, axis=0)
    eidx = lax.broadcasted_iota(jnp.int32, (e, t), 0).astype(F32)
    idxs, ws = [], []
    chosen = jnp.zeros((e, t), F32)
    for _ in range(TOP_K):
        m = jnp.max(work, axis=0, keepdims=True)
        first = jnp.min(jnp.where(work == m, eidx, float(e)), axis=0, keepdims=True)
        pick = eidx == first
        idxs.append(first)
        ws.append(jnp.sum(jnp.where(pick, scores, 0.0), axis=0, keepdims=True))
        chosen = jnp.where(pick, 1.0, chosen)
        work = jnp.where(pick, neg, work)
    w = jnp.concatenate(ws, axis=0)
    w = w / jnp.sum(w, axis=0, keepdims=True) * ROUTED_SCALE
    return jnp.concatenate(idxs, axis=0).astype(jnp.int32), w, chosen


MERGE_TOK = 512
MERGE_SUB = 256


def _merge_kernel(of_ref, ob_ref, og_ref, yna_ref, ga_ref, gb_ref, x_ref, g1_ref, sh2_ref, sc2_ref,
                  hgg_ref, ln1g_ref, ln1b_ref, wa_ref, wb_ref, wo_ref, wr_ref, rb_ref,
                  x1_ref, h2_ref, topi_ref, topw_ref, cnt_ref, *, alpha):
    tm = x_ref.shape[0]
    subs = [slice(i * MERGE_SUB, (i + 1) * MERGE_SUB) for i in range(tm // MERGE_SUB)]

    def branches(rows):
        o = of_ref[rows, :] + ob_ref[rows, :]
        parts = []
        for h in range(HG_HEADS):
            oh = o[:, h * HG_DK:(h + 1) * HG_DK]
            parts.append(oh * lax.rsqrt(jnp.mean(oh * oh, axis=-1, keepdims=True) + LN_EPS))
        y_hg = jnp.concatenate(parts, axis=-1) * hgg_ref[...] * _silu(og_ref[rows, :])
        return _dot(y_hg.astype(BF16), wa_ref[...]), _dot(yna_ref[rows, :], wb_ref[...])

    def out_proj(rows, ya, yb):
        t = jax.nn.sigmoid(ga_ref[rows, :]) * ya + jax.nn.sigmoid(gb_ref[rows, :]) * yb
        return _dot(t.astype(BF16), wo_ref[...])

    def norms_router(rows, i, y):
        x1 = _normalize(alpha * x_ref[rows, :] + g1_ref[...] * y) * ln1g_ref[...] + ln1b_ref[...]
        x1_ref[rows, :] = x1
        h2 = _normalize(x1) * (1.0 + sc2_ref[...]) + sh2_ref[...]
        h2_ref[rows, :] = _pack_words(h2)
        hh, hm, hl = _split3(h2)
        wh, wm, wl = _split3(wr_ref[...])
        return (_dot_nt(wh, hh) + _dot_nt(wh, hm) + _dot_nt(wm, hh)
                + _dot_nt(wh, hl) + _dot_nt(wl, hh) + _dot_nt(wm, hm))

    ab = [branches(rows) for rows in subs]
    ys = [out_proj(rows, *ab[i]) for i, rows in enumerate(subs)]
    logits = [norms_router(rows, i, ys[i]) for i, rows in enumerate(subs)]

    @pl.when((pl.program_id(0) == 0) & (pl.program_id(1) == 0))
    def _():
        cnt_ref[...] = jnp.zeros_like(cnt_ref)

    for i, rows in enumerate(subs):
        topi, topw, chosen = _route(logits[i], rb_ref[...])
        topi_ref[:, rows] = topi
        topw_ref[:, rows] = topw
        cnt_ref[...] += jnp.sum(chosen, axis=1, keepdims=True)


def _merge(o_f, o_b, p, y_na, x, g1, sh2, sc2, hg_norm_g, ln1_g, ln1_b, w_a, w_b, w_o, w_router_t, router_bias,
           alpha):
    b, s, d = x.shape
    tm = min(MERGE_TOK, s)
    e = w_router_t.shape[0]
    tok = lambda bi, i: (bi, i, 0)
    blk = pl.BlockSpec((None, tm, d), tok)

    def sec(section):
        return pl.BlockSpec((None, None, tm, d), lambda bi, i: (section, bi, i, 0))

    mod = pl.BlockSpec((None, 1, d), lambda bi, i: (bi, 0, 0))
    vec = pl.BlockSpec((1, d), lambda bi, i: (0, 0))
    mat = pl.BlockSpec((d, d), lambda bi, i: (0, 0), pipeline_mode=pl.Buffered(1))
    return pl.pallas_call(
        functools.partial(_merge_kernel, alpha=alpha),
        grid=(b, s // tm),
        in_specs=[blk, blk, sec(SEC_OG), blk, sec(SEC_GA), sec(SEC_GB), blk, mod, mod, mod,
                  vec, vec, vec, mat, mat, mat,
                  pl.BlockSpec((e, d), lambda bi, i: (0, 0)),
                  pl.BlockSpec((e, 1), lambda bi, i: (0, 0))],
        out_specs=[blk,
                   pl.BlockSpec((tm, d // 2), lambda bi, i: (bi * (s // tm) + i, 0)),
                   pl.BlockSpec((None, TOP_K, tm), lambda bi, i: (bi, 0, i)),
                   pl.BlockSpec((None, TOP_K, tm), lambda bi, i: (bi, 0, i)),
                   pl.BlockSpec((e, 128), lambda bi, i: (0, 0))],
        out_shape=[jax.ShapeDtypeStruct((b, s, d), F32),
                   jax.ShapeDtypeStruct((b * s, d // 2), U32),
                   jax.ShapeDtypeStruct((b, TOP_K, s), jnp.int32), jax.ShapeDtypeStruct((b, TOP_K, s), F32),
                   jax.ShapeDtypeStruct((e, 128), F32)],
        compiler_params=_cparams(("arbitrary", "arbitrary")),
        name="merge",
    )(o_f, o_b, p, y_na, p, p, x, g1, sh2, sc2, hg_norm_g.reshape(1, d), ln1_g.reshape(1, d),
      ln1_b.reshape(1, d), w_a, w_b, w_o, w_router_t, router_bias.reshape(e, 1))


MOE_TILE = 512
MOE_TOK = 512


def _plan_kernel(topi_ref, off_ref, dest_ref, carry_ref):
    @pl.when(pl.program_id(0) == 0)
    def _():
        carry_ref[...] = jnp.zeros_like(carry_ref)

    topi = topi_ref[...]
    tok = topi.shape[1]
    eidx = lax.broadcasted_iota(jnp.int32, (N_EXPERTS, tok), 0)
    hits = [eidx == topi[k:k + 1, :] for k in range(TOP_K)]
    m = jnp.zeros((N_EXPERTS, tok), F32)
    for hit in hits:
        m = jnp.where(hit, 1.0, m)
    before = (lax.broadcasted_iota(jnp.int32, (tok, tok), 0)
              < lax.broadcasted_iota(jnp.int32, (tok, tok), 1)).astype(F32).astype(BF16)
    row = off_ref[...] + carry_ref[...] + _dot(m.astype(BF16), before)
    dest = [jnp.sum(jnp.where(hit, row, 0.0), axis=0, keepdims=True) for hit in hits]
    dest_ref[...] = jnp.concatenate(dest, axis=0).astype(jnp.int32)
    carry_ref[...] += jnp.sum(m, axis=1, keepdims=True)


def _plan(topi, seg_off):
    b, k, s = topi.shape
    per_b = s // MOE_TOK
    blk = pl.BlockSpec((None, k, MOE_TOK), lambda i: (i // per_b, 0, i % per_b))
    return pl.pallas_call(
        _plan_kernel,
        grid=(b * per_b,),
        in_specs=[blk, pl.BlockSpec((N_EXPERTS, 1), lambda i: (0, 0))],
        out_specs=blk,
        out_shape=jax.ShapeDtypeStruct((b, k, s), jnp.int32),
        scratch_shapes=[pltpu.VMEM((N_EXPERTS, 1), F32)],
        compiler_params=_cparams(("arbitrary",)),
        name="plan",
    )(topi, seg_off.astype(F32).reshape(N_EXPERTS, 1))


SC_WINDOW = 128


def _sc_workers():
    info = plsc.get_sparse_core_info()
    return info.num_cores, info.num_cores * info.num_subcores


def _scatter_rows(src, idx, zero_idx, n_rows):
    t, w = src.shape
    m, mz = idx.shape[0], zero_idx.shape[0]
    n_cores, n_workers = _sc_workers()
    per_worker, per_worker_z = t // n_workers, mz // n_workers
    assert m % t == 0
    assert per_worker * n_workers == t and per_worker % SC_WINDOW == 0
    assert per_worker_z * n_workers == mz and per_worker_z % SC_WINDOW == 0
    mesh = plsc.VectorSubcoreMesh(core_axis_name="core", subcore_axis_name="subcore")

    @functools.partial(
        pl.kernel, mesh=mesh, out_type=jax.ShapeDtypeStruct((n_rows, w), src.dtype),
        scratch_types=[pltpu.VMEM((SC_WINDOW,), jnp.int32), pltpu.VMEM((SC_WINDOW, w), src.dtype),
                       pltpu.SemaphoreType.DMA])
    def scatter(src_hbm, idx_hbm, zeros_hbm, zero_idx_hbm, out_hbm, idx_v, rows_v, sem):
        worker = lax.axis_index("subcore") * n_cores + lax.axis_index("core")

        @pl.loop(0, per_worker // SC_WINDOW)
        def _(step):
            first = pl.multiple_of(worker * per_worker + step * SC_WINDOW, SC_WINDOW)
            pltpu.sync_copy(src_hbm.at[pl.ds(first, SC_WINDOW)], rows_v)
            for copy in range(m // t):
                pltpu.sync_copy(idx_hbm.at[pl.ds(copy * t + first, SC_WINDOW)], idx_v)
                pltpu.async_copy(rows_v, out_hbm.at[idx_v], sem).wait()

        pltpu.sync_copy(zeros_hbm, rows_v)

        @pl.loop(0, per_worker_z // SC_WINDOW)
        def _(step):
            base = pl.multiple_of(worker * per_worker_z + step * SC_WINDOW, SC_WINDOW)
            pltpu.sync_copy(zero_idx_hbm.at[pl.ds(base, SC_WINDOW)], idx_v)
            pltpu.async_copy(rows_v, out_hbm.at[idx_v], sem).wait()

    return scatter(src, idx, jnp.zeros((SC_WINDOW, w), src.dtype), zero_idx)


EXPERT_RING = 3


def _experts_kernel(te_ref, tb_ref, nt_ref, seg_ref, nxt_ref, xs_ref, wg_ref, wu_ref, wd_ref, ys_ref,
                    xbuf, sem, wg_f32, wu_f32, wd_f32, wg_s, wu_s, wd_s, wsem):
    i = pl.program_id(0)
    n_tiles = nt_ref[0]

    def weight_copies(expert, slot):
        return [pltpu.make_async_copy(src.at[expert], dst.at[slot], wsem.at[slot, n])
                for n, (src, dst) in enumerate(((wg_ref, wg_f32), (wu_ref, wu_f32), (wd_ref, wd_f32)))]

    @pl.when(i == 0)
    def _():
        for c in weight_copies(te_ref[0], 0):
            c.start()

    first = (i == 0) | (seg_ref[i] != seg_ref[jnp.maximum(i - 1, 0)])

    @pl.when((i < n_tiles) & first)
    def _():
        slot = seg_ref[i] & 1
        for c in weight_copies(te_ref[i], slot):
            c.wait()
        wg_s[...] = wg_f32[slot].astype(BF16)
        wu_s[...] = wu_f32[slot].astype(BF16)
        wd_s[...] = wd_f32[slot].astype(BF16)

        @pl.when(nxt_ref[i] >= 0)
        def _():
            for c in weight_copies(nxt_ref[i], 1 - slot):
                c.start()

    def tile_copy(j):
        slot = lax.rem(j, EXPERT_RING)
        row0 = pl.multiple_of(tb_ref[j] * MOE_TILE, MOE_TILE)
        return pltpu.make_async_copy(xs_ref.at[pl.ds(row0, MOE_TILE), :], xbuf.at[slot], sem.at[slot])

    @pl.when(i == 0)
    def _():
        for j in range(EXPERT_RING - 1):
            @pl.when(j < n_tiles)
            def _():
                tile_copy(j).start()

    ahead = i + (EXPERT_RING - 1)

    @pl.when(ahead < n_tiles)
    def _():
        tile_copy(ahead).start()

    @pl.when(i < n_tiles)
    def _():
        tile_copy(i).wait()
        x = _unpack_words(xbuf[lax.rem(i, EXPERT_RING)]).astype(BF16)
        act = _silu(_dot(x, wg_s[...])) * _dot(x, wu_s[...])
        ys_ref[...] = _pack_words(_dot(act.astype(BF16), wd_s[...]))


def _experts(xs, tile_expert, tile_block, n_tiles, tile_segment, next_expert, wg, wu, wd):
    d, f = wg.shape[1], wg.shape[2]
    anywhere = pl.BlockSpec(memory_space=pl.ANY)
    grid_spec = pltpu.PrefetchScalarGridSpec(
        num_scalar_prefetch=5,
        grid=(xs.shape[0] // MOE_TILE,),
        in_specs=[anywhere, anywhere, anywhere, anywhere],
        out_specs=pl.BlockSpec((MOE_TILE, d // 2), lambda i, te, tb, nt, seg, nxt: (tb[i], 0)),
        scratch_shapes=[pltpu.VMEM((EXPERT_RING, MOE_TILE, d // 2), U32), pltpu.SemaphoreType.DMA((EXPERT_RING,)),
                        pltpu.VMEM((2, d, f), F32), pltpu.VMEM((2, d, f), F32), pltpu.VMEM((2, f, d), F32),
                        pltpu.VMEM((d, f), BF16), pltpu.VMEM((d, f), BF16), pltpu.VMEM((f, d), BF16),
                        pltpu.SemaphoreType.DMA((2, 3))],
    )
    return pl.pallas_call(
        _experts_kernel,
        grid_spec=grid_spec,
        out_shape=jax.ShapeDtypeStruct(xs.shape, U32),
        compiler_params=_cparams(("arbitrary",)),
        name="experts",
    )(tile_expert, tile_block, n_tiles, tile_segment, next_expert, xs, wg, wu, wd)


def _gather_rows(table, idx):
    m = idx.shape[0]
    w = table.shape[1]
    win = SC_WINDOW // 2
    n_cores, n_workers = _sc_workers()
    per_worker = m // n_workers
    n_pairs = per_worker // (2 * win)
    assert per_worker * n_workers == m and n_pairs * 2 * win == per_worker
    mesh = plsc.VectorSubcoreMesh(core_axis_name="core", subcore_axis_name="subcore")

    @functools.partial(
        pl.kernel, mesh=mesh, out_type=jax.ShapeDtypeStruct((m, w), table.dtype),
        scratch_types=[pltpu.VMEM((win,), jnp.int32), pltpu.VMEM((win,), jnp.int32),
                       pltpu.VMEM((win, w), table.dtype), pltpu.VMEM((win, w), table.dtype),
                       pltpu.SemaphoreType.DMA, pltpu.SemaphoreType.DMA])
    def gather(table_hbm, idx_hbm, out_hbm, idx_a, idx_b, rows_a, rows_b, sem_a, sem_b):
        worker = lax.axis_index("subcore") * n_cores + lax.axis_index("core")
        start = worker * per_worker

        def request(first, idx_v, rows_v, sem):
            pltpu.sync_copy(idx_hbm.at[pl.ds(first, win)], idx_v)
            pltpu.async_copy(table_hbm.at[idx_v], rows_v, sem)

        def deliver(first, idx_v, rows_v, sem):
            pltpu.make_async_copy(table_hbm.at[idx_v], rows_v, sem).wait()
            pltpu.sync_copy(rows_v, out_hbm.at[pl.ds(first, win)])

        request(pl.multiple_of(start, win), idx_a, rows_a, sem_a)

        @pl.loop(0, n_pairs)
        def _(pair):
            first_a = pl.multiple_of(start + pair * 2 * win, win)
            first_b = pl.multiple_of(first_a + win, win)
            request(first_b, idx_b, rows_b, sem_b)
            deliver(first_a, idx_a, rows_a, sem_a)

            @pl.when(pair + 1 < n_pairs)
            def _():
                request(pl.multiple_of(first_b + win, win), idx_a, rows_a, sem_a)

            deliver(first_b, idx_b, rows_b, sem_b)

    return gather(table, idx)


def _combine_kernel(rows_ref, topw_ref, h_ref, x1_ref, g2_ref, sg_ref, su_ref, sd_ref, ln2g_ref, ln2b_ref,
                    o_ref, *, alpha):
    h = _unpack_words(h_ref[...]).astype(BF16)
    act = _silu(_dot(h, sg_ref[...])) * _dot(h, su_ref[...])
    y = _dot(act.astype(BF16), sd_ref[...])
    w = topw_ref[...].T
    for k in range(TOP_K):
        y = y + w[:, k:k + 1] * _unpack_words(rows_ref[k])
    o_ref[...] = _normalize(alpha * x1_ref[...] + g2_ref[...] * y) * ln2g_ref[...] + ln2b_ref[...]


def _combine(gathered, topw, h2p, x1, g2, sg, su, sd, ln2_g, ln2_b, alpha):
    t, d = x1.shape
    b, k, s = topw.shape
    per_b = s // MOE_TOK
    fs = sg.shape[1]
    rows = pl.BlockSpec((MOE_TOK, d), lambda i: (i, 0))
    packed = pl.BlockSpec((MOE_TOK, d // 2), lambda i: (i, 0))
    vec = pl.BlockSpec((1, d), lambda i: (0, 0))
    return pl.pallas_call(
        functools.partial(_combine_kernel, alpha=alpha),
        grid=(t // MOE_TOK,),
        in_specs=[pl.BlockSpec((k, MOE_TOK, d // 2), lambda i: (0, i, 0)),
                  pl.BlockSpec((None, k, MOE_TOK), lambda i: (i // per_b, 0, i % per_b)),
                  packed, rows,
                  pl.BlockSpec((None, 1, d), lambda i: (i // per_b, 0, 0)),
                  pl.BlockSpec((d, fs), lambda i: (0, 0)),
                  pl.BlockSpec((d, fs), lambda i: (0, 0)),
                  pl.BlockSpec((fs, d), lambda i: (0, 0)),
                  vec, vec],
        out_specs=rows,
        out_shape=jax.ShapeDtypeStruct((t, d), F32),
        compiler_params=_cparams(("arbitrary",)),
        name="combine",
    )(gathered, topw, h2p, x1, g2, sg, su, sd, ln2_g.reshape(1, d), ln2_b.reshape(1, d))


def _moe(h2p, topi, topw, cnt, x1, g2, wg, wu, wd, sg, su, sd, ln2_g, ln2_b, alpha):
    b, s, d = x1.shape
    t = b * s
    cnt = cnt[:, 0].astype(jnp.int32)
    tiles_e = (cnt + (MOE_TILE - 1)) // MOE_TILE
    tiles_cum = jnp.cumsum(tiles_e)
    seg_off = (tiles_cum - tiles_e) * MOE_TILE
    n_tiles_max = t * TOP_K // MOE_TILE + N_EXPERTS
    tile_block = jnp.minimum(jnp.arange(n_tiles_max, dtype=jnp.int32), tiles_cum[-1] - 1)
    tile_expert = jnp.sum((tiles_cum[None, :] <= tile_block[:, None]).astype(jnp.int32), axis=1)
    n_tiles = tiles_cum[-1:].astype(jnp.int32)
    present = tiles_e > 0
    seg_of_expert = jnp.cumsum(present.astype(jnp.int32)) - 1
    later = jnp.where(present, jnp.arange(N_EXPERTS, dtype=jnp.int32), N_EXPERTS)
    next_present = jnp.concatenate([lax.cummin(later, reverse=True)[1:], jnp.full((1,), N_EXPERTS, jnp.int32)])
    next_present = jnp.where(next_present < N_EXPERTS, next_present, -1)
    tile_segment = seg_of_expert[tile_expert].astype(jnp.int32)
    next_expert = next_present[tile_expert].astype(jnp.int32)

    dest = jnp.transpose(_plan(topi, seg_off), (1, 0, 2)).reshape(TOP_K * t)
    j = jnp.arange(MOE_TILE, dtype=jnp.int32)[None, :]
    n_pad = (tiles_e * MOE_TILE - cnt)[:, None]
    spare = (n_tiles_max - 1) * MOE_TILE + j
    zero_idx = jnp.where(j < n_pad, (seg_off + cnt)[:, None] + j, spare).reshape(N_EXPERTS * MOE_TILE)
    xs = _scatter_rows(h2p, dest, zero_idx.astype(jnp.int32), n_tiles_max * MOE_TILE)
    ys = _experts(xs, tile_expert, tile_block, n_tiles, tile_segment, next_expert, wg, wu, wd)
    gathered = _gather_rows(ys, dest)
    out = _combine(gathered.reshape(TOP_K, t, d // 2), topw, h2p, x1.reshape(t, d), g2, sg, su, sd,
                   ln2_g, ln2_b, alpha)
    return out.reshape(b, s, d)


def kernel(x, c, ctx, c_ctx, w_ada, b_ada, w_in, hg_lb_fwd, hg_lb_bwd, hg_norm_g, na_rpb, w_branch_a, w_branch_b, w_out, ln1_g, ln1_b, w_router, router_bias, w_e_gate, w_e_up, w_e_down, w_sh_gate, w_sh_up, w_sh_down, ln2_g, ln2_b):
    depth = w_ada.shape[0]
    assert depth == 1, "single-layer block"
    b, s, d = x.shape
    alpha = (2.0 * depth) ** 0.25
    l = 0
    lb_fwd = jnp.cumsum(jax.nn.softmax(hg_lb_fwd.astype(F32), axis=0), axis=0)[l]
    lb_bwd = jnp.cumsum(jax.nn.softmax(hg_lb_bwd.astype(F32), axis=0), axis=0)[l]

    cond_rows = jnp.concatenate([c, c_ctx[None, :], jnp.zeros((8 - b - 1, d), F32)], axis=0)
    mod = _ada(cond_rows, w_ada[l], b_ada[l])
    sh1, sc1, g1, sh2, sc2, g2 = [m[:b, None, :] for m in jnp.split(mod, 6, axis=-1)]
    csh1, csc1 = [jnp.broadcast_to(m[b:b + 1, None, :], (b, 1, d)) for m in jnp.split(mod, 6, axis=-1)[:2]]

    w_in_b = w_in[l].astype(BF16)
    p = _inproj(x, sh1, sc1, w_in_b, tuple(range(N_SECTIONS)))
    pc = _inproj(ctx, csh1, csc1, w_in_b, CTX_SECTIONS)

    o_f, o_b = _hgrn(p, pc, lb_fwd, lb_bwd)
    y_na = _natten(p, pc, *_na_tables(na_rpb[l], s))

    x1, h2, topi, topw, cnt = _merge(o_f, o_b, p, y_na, x, g1, sh2, sc2, hg_norm_g[l], ln1_g[l], ln1_b[l],
                                     w_branch_a[l].astype(BF16), w_branch_b[l].astype(BF16),
                                     w_out[l].astype(BF16), w_router[l].T, router_bias[l], alpha)

    return _moe(h2, topi, topw, cnt, x1, g2,
                w_e_gate[l], w_e_up[l], w_e_down[l],
                w_sh_gate[l].astype(BF16), w_sh_up[l].astype(BF16), w_sh_down[l].astype(BF16),
                ln2_g[l], ln2_b[l], alpha)
```

```python
import functools

import numpy as np
import jax
import jax.numpy as jnp
from jax import lax
from jax.experimental import pallas as pl
from jax.experimental.pallas import tpu as pltpu
from jax.experimental.pallas import tpu_sc as plsc

F32 = jnp.float32
BF16 = jnp.bfloat16

D_MODEL = 1024
GRID_W = 64
HG_HEADS = 8
HG_DK = 128
HG_CHUNK = 64
NA_HEADS = 16
NA_HD = 64
NA_WIN_R = 8
NA_WIN_C = 16
ROPE_THETA = 10000.0
NEG_INF = -1e30
N_EXPERTS = 64
EXPERT_DIM = 256
TOP_K = 8
N_GROUPS = 8
TOPK_GROUPS = 4
ROUTED_SCALE = 2.5
LN_EPS = 1e-6
N_SECTIONS = 10
SEC_Q, SEC_FF, SEC_FB, SEC_I, SEC_OG, SEC_NQ, SEC_NK, SEC_NV, SEC_GA, SEC_GB = range(10)
CTX_SECTIONS = (SEC_FF, SEC_FB, SEC_I, SEC_NK, SEC_NV)

VMEM_LIMIT = 56 * 1024 * 1024


def _cparams(sem):
    return pltpu.CompilerParams(dimension_semantics=sem, vmem_limit_bytes=VMEM_LIMIT)


def _normalize(x):
    mu = jnp.mean(x, axis=-1, keepdims=True)
    xc = x - mu
    var = jnp.mean(xc * xc, axis=-1, keepdims=True)
    return xc * lax.rsqrt(var + LN_EPS)


def _silu(x):
    return x * jax.nn.sigmoid(x)


def _dot(a, b):
    return jnp.dot(a, b, preferred_element_type=F32)


def _dot_nt(a, b):
    return lax.dot_general(a, b, (((1,), (1,)), ((), ())), preferred_element_type=F32)


def _dot_tn(a, b):
    return lax.dot_general(a, b, (((0,), (0,)), ((), ())), preferred_element_type=F32)


U32 = jnp.uint32


def _pack_words(x):
    half = x.shape[1] // 2
    lo = lax.bitcast_convert_type(x[:, :half].astype(BF16).astype(F32), U32) >> 16
    hi = lax.bitcast_convert_type(x[:, half:].astype(BF16).astype(F32), U32) & jnp.uint32(0xFFFF0000)
    return lo | hi


def _unpack_words(w):
    lo = lax.bitcast_convert_type(w << 16, F32)
    hi = lax.bitcast_convert_type(w & jnp.uint32(0xFFFF0000), F32)
    return jnp.concatenate([lo, hi], axis=-1)


def _split3(x):
    hi = x.astype(BF16)
    r1 = x - hi.astype(F32)
    mid = r1.astype(BF16)
    lo = (r1 - mid.astype(F32)).astype(BF16)
    return hi, mid, lo


def _ada_kernel(c_ref, w_ref, b_ref, o_ref):
    cond = _silu(c_ref[...])
    o_ref[...] = _dot(cond.astype(BF16), w_ref[...].astype(BF16)) + b_ref[...]


def _ada(cond_rows, w_ada, b_ada):
    r, d = cond_rows.shape
    n = w_ada.shape[1]
    tn = 1024
    return pl.pallas_call(
        _ada_kernel,
        grid=(n // tn,),
        in_specs=[pl.BlockSpec((r, d), lambda j: (0, 0)),
                  pl.BlockSpec((d, tn), lambda j: (0, j)),
                  pl.BlockSpec((1, tn), lambda j: (0, j))],
        out_specs=pl.BlockSpec((r, tn), lambda j: (0, j)),
        out_shape=jax.ShapeDtypeStruct((r, n), F32),
        compiler_params=_cparams(("arbitrary",)),
        name="ada",
    )(cond_rows, w_ada, b_ada.reshape(1, n))


INPROJ_TOK = 2048


def _inproj_kernel(x_ref, sh_ref, sc_ref, w_ref, o_ref, h_ref):
    @pl.when(pl.program_id(2) == 0)
    def _():
        h = _normalize(x_ref[...]) * (1.0 + sc_ref[...]) + sh_ref[...]
        h_ref[...] = h.astype(BF16)

    o_ref[...] = _dot(h_ref[...], w_ref[...])


def _inproj(x, shift, scale, w_in_bf16, sections):
    b, s, d = x.shape
    tm = min(INPROJ_TOK, s)
    nj = len(sections)

    def section(j):
        sec = sections[-1]
        for k in range(nj - 2, -1, -1):
            sec = jnp.where(j == k, sections[k], sec)
        return sec

    return pl.pallas_call(
        _inproj_kernel,
        grid=(b, s // tm, nj),
        in_specs=[pl.BlockSpec((None, tm, d), lambda bi, i, j: (bi, i, 0)),
                  pl.BlockSpec((None, 1, d), lambda bi, i, j: (bi, 0, 0)),
                  pl.BlockSpec((None, 1, d), lambda bi, i, j: (bi, 0, 0)),
                  pl.BlockSpec((d, d), lambda bi, i, j: (0, section(j)))],
        out_specs=pl.BlockSpec((None, None, tm, d), lambda bi, i, j: (j, bi, i, 0)),
        out_shape=jax.ShapeDtypeStruct((nj, b, s, d), F32),
        scratch_shapes=[pltpu.VMEM((tm, d), BF16)],
        compiler_params=_cparams(("arbitrary", "arbitrary", "arbitrary")),
        name="inproj",
    )(x, shift, scale, w_in_bf16)


def _hgrn_gates(q, fraw, v, lb, tri_bf16, last_row):
    f = lb + (1.0 - lb) * jax.nn.sigmoid(fraw)
    k = 1.0 - f
    lf = jnp.log(f)
    hi, mid, lo = _split3(lf)
    a = _dot(tri_bf16, hi) + _dot(tri_bf16, mid) + _dot(tri_bf16, lo)
    a_last = a[last_row:last_row + 1, :]
    kd = (k * jnp.exp(a_last - a)).astype(BF16)
    decay = jnp.exp(a_last)
    qa = kb = None
    if q is not None:
        qa = (_silu(q) * jnp.exp(a)).astype(BF16)
        kb = (k * jnp.exp(-a)).astype(BF16)
    return qa, kb, kd, v.astype(BF16), decay


def _hgrn_chunks(chunks, st_ref):
    first = []
    for d, ((qa, kb, kd, vb, decay), keep) in enumerate(chunks):
        for h in range(HG_HEADS):
            sl = slice(h * HG_DK, (h + 1) * HG_DK)
            st = st_ref[d, h]
            if qa is not None:
                first.append((_dot_nt(qa[:, sl], kb[:, sl]), _dot_nt(qa[:, sl], st.astype(BF16))))
            st_ref[d, h] = st * decay[:, sl] + _dot_tn(vb[:, sl], kd[:, sl])
    results = []
    for d, ((qa, kb, kd, vb, decay), keep) in enumerate(chunks):
        if qa is None:
            results.append(None)
            continue
        outs = []
        for h in range(HG_HEADS):
            sl = slice(h * HG_DK, (h + 1) * HG_DK)
            s_qk, o_state = first.pop(0)
            outs.append(_dot(jnp.where(keep, s_qk, 0.0).astype(BF16), vb[:, sl]) + o_state)
        results.append(jnp.concatenate(outs, axis=-1))
    return results


def _hgrn_kernel(qf_ref, ff_ref, if_ref, qb_ref, fb_ref, ib_ref, cff_ref, cfb_ref, ci_ref,
                 lbf_ref, lbb_ref, of_ref, ob_ref, st_ref, *, n_sub, n_ctx_sub):
    n = pl.program_id(1)
    c = HG_CHUNK
    row = lax.broadcasted_iota(jnp.int32, (c, c), 0)
    col = lax.broadcasted_iota(jnp.int32, (c, c), 1)
    keep_f = col <= row
    keep_b = col >= row
    tri_f = keep_f.astype(F32).astype(BF16)
    tri_b = keep_b.astype(F32).astype(BF16)
    lbf = lbf_ref[...]
    lbb = lbb_ref[...]

    @pl.when(n == 0)
    def _():
        st_ref[...] = jnp.zeros_like(st_ref)

        def body(i, carry):
            r0 = pl.multiple_of(i * c, c)
            r1 = pl.multiple_of((n_ctx_sub - 1 - i) * c, c)
            gf = _hgrn_gates(None, cff_ref[pl.ds(r0, c), :], ci_ref[pl.ds(r0, c), :], lbf, tri_f, c - 1)
            gb = _hgrn_gates(None, cfb_ref[pl.ds(r1, c), :], ci_ref[pl.ds(r1, c), :], lbb, tri_b, 0)
            _hgrn_chunks([(gf, keep_f), (gb, keep_b)], st_ref)
            return carry

        lax.fori_loop(0, n_ctx_sub, body, 0)

    @pl.when(n > 0)
    def _():
        def body(i, carry):
            r0 = pl.multiple_of(i * c, c)
            r1 = pl.multiple_of((n_sub - 1 - i) * c, c)
            gf = _hgrn_gates(qf_ref[pl.ds(r0, c), :], ff_ref[pl.ds(r0, c), :], if_ref[pl.ds(r0, c), :],
                             lbf, tri_f, c - 1)
            gb = _hgrn_gates(qb_ref[pl.ds(r1, c), :], fb_ref[pl.ds(r1, c), :], ib_ref[pl.ds(r1, c), :],
                             lbb, tri_b, 0)
            o_f, o_b = _hgrn_chunks([(gf, keep_f), (gb, keep_b)], st_ref)
            of_ref[pl.ds(r0, c), :] = o_f
            ob_ref[pl.ds(r1, c), :] = o_b
            return carry

        lax.fori_loop(0, n_sub, body, 0, unroll=True)


def _hgrn(p, pc, lb_fwd, lb_bwd):
    _, b, s, w = p.shape
    ctx_len = pc.shape[2]
    tb = min(512, s)
    nb = s // tb
    fwd = lambda bi, n: jnp.maximum(n - 1, 0)
    bwd = lambda bi, n: nb - 1 - jnp.maximum(n - 1, 0)

    def sec(section, blk):
        return pl.BlockSpec((None, None, tb, w), lambda bi, n: (section, bi, blk(bi, n), 0))

    def csec(section):
        return pl.BlockSpec((None, None, ctx_len, w), lambda bi, n: (CTX_SECTIONS.index(section), bi, 0, 0))

    vec = pl.BlockSpec((1, w), lambda bi, n: (0, 0))
    kern = functools.partial(_hgrn_kernel, n_sub=tb // HG_CHUNK, n_ctx_sub=ctx_len // HG_CHUNK)
    return pl.pallas_call(
        kern,
        grid=(b, nb + 1),
        in_specs=[sec(SEC_Q, fwd), sec(SEC_FF, fwd), sec(SEC_I, fwd),
                  sec(SEC_Q, bwd), sec(SEC_FB, bwd), sec(SEC_I, bwd),
                  csec(SEC_FF), csec(SEC_FB), csec(SEC_I), vec, vec],
        out_specs=[pl.BlockSpec((None, tb, w), lambda bi, n: (bi, fwd(bi, n), 0)),
                   pl.BlockSpec((None, tb, w), lambda bi, n: (bi, bwd(bi, n), 0))],
        out_shape=[jax.ShapeDtypeStruct((b, s, w), F32), jax.ShapeDtypeStruct((b, s, w), F32)],
        scratch_shapes=[pltpu.VMEM((2, HG_HEADS, HG_DK, HG_DK), F32)],
        compiler_params=_cparams(("arbitrary", "arbitrary")),
        name="hgrn",
    )(p, p, p, p, p, p, pc, pc, pc, lb_fwd.reshape(1, w), lb_bwd.reshape(1, w))


NA_ROWS_PER_STEP = 64
NA_PREP_ROWS = 512
NA_KEY_TILE = 128
NA_SPAN = NA_WIN_R * GRID_W


def _rope(t, cos, sin_signed, first_half):
    w = t.shape[-1]
    partner = jnp.where(first_half, pltpu.roll(t, w - 16, 1), pltpu.roll(t, 16, 1))
    return t * cos + partner * sin_signed


def _fold_lanes(op, *arrays):
    tiles = [a[:, c:c + 128] for a in arrays for c in range(0, a.shape[-1], 128)]
    acc = tiles[0]
    for t in tiles[1:]:
        acc = op(acc, t)
    return acc


def _rope_tables(rowtab_ref, coltab_ref, row0, n_rows, row_lane):
    out = []
    for i in range(2):
        rt = rowtab_ref[i, pl.ds(row0, n_rows), :]
        by_row = jnp.concatenate([jnp.broadcast_to(rt[r:r + 1, :], (GRID_W, rt.shape[1])) for r in range(n_rows)],
                                 axis=0)
        by_col = jnp.concatenate([coltab_ref[i]] * n_rows, axis=0)
        out.append(jnp.where(row_lane, by_row, by_col))
    return out


def _natten_kernel(q_ref, k_ref, v_ref, kc_ref, vc_ref, rowtab_ref, coltab_ref, rp_ref, o_ref,
                   kt_s, v_s, kc_s, vc_s, bias_s, tail_s, *, rows):
    rblk = pl.program_id(2)
    hd = NA_HD
    lane = lax.broadcasted_iota(jnp.int32, (1, 2 * hd), 1)
    first_half = (lane % 32) < 16
    row_lane = (lane % hd) < hd // 2
    scale = NA_HD ** -0.5

    def values_and_ones(v_pair, h):
        vh = v_pair if h == 0 else pltpu.roll(v_pair, hd, 1)
        return jnp.where(lane < hd, vh, jnp.where(lane == hd, 1.0, 0.0)).astype(BF16)

    @pl.when(rblk == 0)
    def _():
        kc = kc_ref[...].astype(BF16)
        qi = lax.broadcasted_iota(jnp.int32, (GRID_W, GRID_W), 0)
        ki = lax.broadcasted_iota(jnp.int32, (GRID_W, GRID_W), 1)
        cstart = jnp.clip(qi - NA_WIN_C // 2, 0, GRID_W - NA_WIN_C)
        in_win = (ki >= cstart) & (ki < cstart + NA_WIN_C)
        s_len = k_ref.shape[0]
        tail_s[...] = jnp.zeros_like(tail_s)
        for h in range(2):
            sl = slice(h * hd, (h + 1) * hd)
            kc_s[h] = kc[:, sl]
            vc_s[h] = values_and_ones(vc_ref[...], h)
            tiles = []
            for dr in range(2 * NA_WIN_R - 1):
                table = jnp.broadcast_to(rp_ref[h, dr:dr + 1, :], (GRID_W, 2 * GRID_W))
                skew = pltpu.roll(table, GRID_W + 1, 1, stride=1, stride_axis=0)
                tiles.append(jnp.where(in_win, skew[:, :GRID_W], NEG_INF))
            for v in range(NA_WIN_R):
                for j in range(NA_WIN_R):
                    bias_s[h, v, :, j * GRID_W:(j + 1) * GRID_W] = tiles[NA_WIN_R - 1 - v + j]

        eye = (lax.broadcasted_iota(jnp.int32, (2 * hd, 2 * hd), 0)
               == lax.broadcasted_iota(jnp.int32, (2 * hd, 2 * hd), 1)).astype(F32).astype(BF16)

        def prep(i, carry):
            r0 = pl.multiple_of(i * NA_PREP_ROWS, NA_PREP_ROWS)
            rws = pl.ds(r0, NA_PREP_ROWS)
            cos, sin = _rope_tables(rowtab_ref, coltab_ref, i * (NA_PREP_ROWS // GRID_W), NA_PREP_ROWS // GRID_W,
                                    row_lane)
            kr = _rope(k_ref[rws, :], cos, sin, first_half)
            kr_odd = jnp.concatenate([tail_s[...], kr[:NA_PREP_ROWS - GRID_W]], axis=0)
            tail_s[...] = kr[NA_PREP_ROWS - GRID_W:]
            krt = [_dot_nt(eye, kr.astype(BF16)).astype(BF16),
                   _dot_nt(eye, kr_odd.astype(BF16)).astype(BF16)]
            vv = v_ref[rws, :]
            for h in range(2):
                sl = slice(h * hd, (h + 1) * hd)
                for par in range(2):
                    for c in range(NA_PREP_ROWS // NA_KEY_TILE):
                        kt_s[h, par, i * (NA_PREP_ROWS // NA_KEY_TILE) + c] = (
                            krt[par][sl, c * NA_KEY_TILE:(c + 1) * NA_KEY_TILE])
                v_s[h, rws, :] = values_and_ones(vv, h)
            return carry

        lax.fori_loop(0, s_len // NA_PREP_ROWS, prep, 0, unroll=4)

    tq = NA_ROWS_PER_STEP * GRID_W
    q = q_ref[...] * scale
    cos, sin = _rope_tables(rowtab_ref, coltab_ref, rblk * NA_ROWS_PER_STEP, NA_ROWS_PER_STEP, row_lane)
    qr = _rope(q, cos, sin, first_half)
    qb = q.astype(BF16)
    qrb = qr.astype(BF16)
    rws = [slice(rr * GRID_W, (rr + 1) * GRID_W) for rr in range(NA_ROWS_PER_STEP)]
    par, slot0, key0, bidx = [], [], [], []
    for rr in range(NA_ROWS_PER_STEP):
        r = rblk * NA_ROWS_PER_STEP + rr
        rs = jnp.clip(r - NA_WIN_R // 2, 0, rows - NA_WIN_R)
        par.append(rs & 1)
        slot0.append(lax.shift_right_logical(rs, 1) + (rs & 1))
        key0.append(pl.multiple_of(rs * GRID_W, GRID_W))
        bidx.append(r - rs)

    def scores(h):
        sl = slice(h * hd, (h + 1) * hd)
        qrb_h = qrb[:, sl]
        s_ctx_all = _dot_nt(qb[:, sl], kc_s[h])
        s_win = []
        for rr in range(NA_ROWS_PER_STEP):
            kt = kt_s[h, par[rr], pl.ds(slot0[rr], NA_SPAN // NA_KEY_TILE)]
            kt = jnp.concatenate([kt[c] for c in range(NA_SPAN // NA_KEY_TILE)], axis=-1)
            s_win.append(_dot(qrb_h[rws[rr]], kt))
        return s_win, s_ctx_all

    def softmax(h, s_win, s_ctx_all):
        e_win, e_ctx = [], []
        for rr in range(NA_ROWS_PER_STEP):
            sw = s_win[rr] + bias_s[h, bidx[rr]]
            sc = s_ctx_all[rws[rr]]
            m = jnp.max(_fold_lanes(jnp.maximum, sw, sc), axis=-1, keepdims=True)
            e_win.append(jnp.exp(sw - m).astype(BF16))
            e_ctx.append(jnp.exp(sc - m).astype(BF16))
        return e_win, e_ctx

    def values(h, e_win, e_ctx):
        o_win = []
        for rr in range(NA_ROWS_PER_STEP):
            o_win.append(_dot(e_win[rr], v_s[h, pl.ds(key0[rr], NA_SPAN), :]))
        o = jnp.concatenate(o_win, axis=0) + _dot(jnp.concatenate(e_ctx, axis=0), vc_s[h])
        return o[:, :hd] * (1.0 / o[:, hd:hd + 1])

    s0 = scores(0)
    s1 = scores(1)
    p0 = softmax(0, *s0)
    o0 = values(0, *p0)
    p1 = softmax(1, *s1)
    o1 = values(1, *p1)
    o_ref[...] = jnp.concatenate([o0, o1], axis=-1).astype(o_ref.dtype)


def _na_tables(rpb, s):
    half = NA_HD // 2
    inv = jnp.power(ROPE_THETA, -jnp.arange(0, half, 2, dtype=F32) / half)

    def tables(n):
        ang = jnp.arange(n, dtype=F32)[:, None] * inv[None, :]
        reps = 2 * NA_HD // half
        return jnp.stack([jnp.tile(jnp.cos(ang), (1, 2 * reps)),
                          jnp.tile(jnp.concatenate([-jnp.sin(ang), jnp.sin(ang)], axis=-1), (1, reps))])

    rowtab, coltab = tables(s // GRID_W), tables(GRID_W)

    pad = GRID_W - NA_WIN_C
    rp = jnp.pad(rpb.astype(F32), ((0, 0), (0, 0), (pad, pad + 1)), mode="edge")
    return rowtab, coltab, rp


def _natten(p, pc, rowtab, coltab, rp):
    _, b, s, w = p.shape
    ctx_len = pc.shape[2]
    rows = s // GRID_W
    assert rows >= NA_WIN_R and rows % NA_ROWS_PER_STEP == 0
    tq = NA_ROWS_PER_STEP * GRID_W
    hw = 2 * NA_HD
    nhp = w // hw
    kern = functools.partial(_natten_kernel, rows=rows)
    return pl.pallas_call(
        kern,
        grid=(b, nhp, rows // NA_ROWS_PER_STEP),
        in_specs=[pl.BlockSpec((None, None, tq, hw), lambda bi, hp, r: (SEC_NQ, bi, r, hp)),
                  pl.BlockSpec((None, None, s, hw), lambda bi, hp, r: (SEC_NK, bi, 0, hp)),
                  pl.BlockSpec((None, None, s, hw), lambda bi, hp, r: (SEC_NV, bi, 0, hp)),
                  pl.BlockSpec((None, None, ctx_len, hw), lambda bi, hp, r: (CTX_SECTIONS.index(SEC_NK), bi, 0, hp)),
                  pl.BlockSpec((None, None, ctx_len, hw), lambda bi, hp, r: (CTX_SECTIONS.index(SEC_NV), bi, 0, hp)),
                  pl.BlockSpec((2, rows, hw), lambda bi, hp, r: (0, 0, 0)),
                  pl.BlockSpec((2, GRID_W, hw), lambda bi, hp, r: (0, 0, 0)),
                  pl.BlockSpec((2, 2 * NA_WIN_R - 1, 2 * GRID_W), lambda bi, hp, r: (hp, 0, 0))],
        out_specs=pl.BlockSpec((None, tq, hw), lambda bi, hp, r: (bi, r, hp)),
        out_shape=jax.ShapeDtypeStruct((b, s, w), BF16),
        scratch_shapes=[pltpu.VMEM((2, 2, s // NA_KEY_TILE, NA_HD, NA_KEY_TILE), BF16),
                        pltpu.VMEM((2, s, hw), BF16),
                        pltpu.VMEM((2, ctx_len, NA_HD), BF16), pltpu.VMEM((2, ctx_len, hw), BF16),
                        pltpu.VMEM((2, NA_WIN_R, GRID_W, NA_SPAN), F32),
                        pltpu.VMEM((GRID_W, hw), F32)],
        compiler_params=_cparams(("arbitrary", "arbitrary", "arbitrary")),
        name="natten",
    )(p, p, p, pc, pc, rowtab, coltab, rp)


def _route(logits_t, rbias):
    e, t = logits_t.shape
    gsz = e // N_GROUPS
    scores = jax.nn.sigmoid(logits_t)
    sel = scores + rbias
    neg = -jnp.inf
    sub = lax.broadcasted_iota(jnp.int32, (gsz, t), 0).astype(F32)
    gscore = []
    for g in range(N_GROUPS):
        grp = sel[g * gsz:(g + 1) * gsz, :]
        m1 = jnp.max(grp, axis=0, keepdims=True)
        first = jnp.min(jnp.where(grp == m1, sub, float(gsz)), axis=0, keepdims=True)
        m2 = jnp.max(jnp.where(sub == first, neg, grp), axis=0, keepdims=True)
        gscore.append(m1 + m2)
    masked = []
    for g in range(N_GROUPS):
        rank = jnp.zeros((1, t), F32)
        for g2 in range(N_GROUPS):
            if g2 == g:
                continue
            if g2 < g:
                ahead = gscore[g2] >= gscore[g]
            else:
                ahead = gscore[g2] > gscore[g]
            rank = rank + jnp.where(ahead, 1.0, 0.0)
        masked.append(jnp.where(rank < TOPK_GROUPS, sel[g * gsz:(g + 1) * gsz, :], neg))
    work = jnp.concatenate(masked, axis=0)
    eidx = lax.broadcasted_iota(jnp.int32, (e, t), 0).astype(F32)
    idxs, ws = [], []
    chosen = jnp.zeros((e, t), F32)
    for _ in range(TOP_K):
        m = jnp.max(work, axis=0, keepdims=True)
        first = jnp.min(jnp.where(work == m, eidx, float(e)), axis=0, keepdims=True)
        pick = eidx == first
        idxs.append(first)
        ws.append(jnp.sum(jnp.where(pick, scores, 0.0), axis=0, keepdims=True))
        chosen = jnp.where(pick, 1.0, chosen)
        work = jnp.where(pick, neg, work)
    w = jnp.concatenate(ws, axis=0)
    w = w / jnp.sum(w, axis=0, keepdims=True) * ROUTED_SCALE
    return jnp.concatenate(idxs, axis=0).astype(jnp.int32), w, chosen


MERGE_TOK = 512
MERGE_SUB = 256


def _merge_kernel(of_ref, ob_ref, og_ref, yna_ref, ga_ref, gb_ref, x_ref, g1_ref, sh2_ref, sc2_ref,
                  hgg_ref, ln1g_ref, ln1b_ref, wa_ref, wb_ref, wo_ref, wr_ref, rb_ref,
                  x1_ref, h2_ref, topi_ref, topw_ref, cnt_ref, *, alpha):
    tm = x_ref.shape[0]
    subs = [slice(i * MERGE_SUB, (i + 1) * MERGE_SUB) for i in range(tm // MERGE_SUB)]

    def branches(rows):
        o = of_ref[rows, :] + ob_ref[rows, :]
        parts = []
        for h in range(HG_HEADS):
            oh = o[:, h * HG_DK:(h + 1) * HG_DK]
            parts.append(oh * lax.rsqrt(jnp.mean(oh * oh, axis=-1, keepdims=True) + LN_EPS))
        y_hg = jnp.concatenate(parts, axis=-1) * hgg_ref[...] * _silu(og_ref[rows, :])
        return _dot(y_hg.astype(BF16), wa_ref[...]), _dot(yna_ref[rows, :], wb_ref[...])

    def out_proj(rows, ya, yb):
        t = jax.nn.sigmoid(ga_ref[rows, :]) * ya + jax.nn.sigmoid(gb_ref[rows, :]) * yb
        return _dot(t.astype(BF16), wo_ref[...])

    def norms_router(rows, i, y):
        x1 = _normalize(alpha * x_ref[rows, :] + g1_ref[...] * y) * ln1g_ref[...] + ln1b_ref[...]
        x1_ref[rows, :] = x1
        h2 = _normalize(x1) * (1.0 + sc2_ref[...]) + sh2_ref[...]
        h2_ref[rows, :] = _pack_words(h2)
        hh, hm, hl = _split3(h2)
        wh, wm, wl = _split3(wr_ref[...])
        return (_dot_nt(wh, hh) + _dot_nt(wh, hm) + _dot_nt(wm, hh)
                + _dot_nt(wh, hl) + _dot_nt(wl, hh) + _dot_nt(wm, hm))

    ab = [branches(rows) for rows in subs]
    ys = [out_proj(rows, *ab[i]) for i, rows in enumerate(subs)]
    logits = [norms_router(rows, i, ys[i]) for i, rows in enumerate(subs)]

    @pl.when((pl.program_id(0) == 0) & (pl.program_id(1) == 0))
    def _():
        cnt_ref[...] = jnp.zeros_like(cnt_ref)

    for i, rows in enumerate(subs):
        topi, topw, chosen = _route(logits[i], rb_ref[...])
        topi_ref[:, rows] = topi
        topw_ref[:, rows] = topw
        cnt_ref[...] += jnp.sum(chosen, axis=1, keepdims=True)


def _merge(o_f, o_b, p, y_na, x, g1, sh2, sc2, hg_norm_g, ln1_g, ln1_b, w_a, w_b, w_o, w_router_t, router_bias,
           alpha):
    b, s, d = x.shape
    tm = min(MERGE_TOK, s)
    e = w_router_t.shape[0]
    tok = lambda bi, i: (bi, i, 0)
    blk = pl.BlockSpec((None, tm, d), tok)

    def sec(section):
        return pl.BlockSpec((None, None, tm, d), lambda bi, i: (section, bi, i, 0))

    mod = pl.BlockSpec((None, 1, d), lambda bi, i: (bi, 0, 0))
    vec = pl.BlockSpec((1, d), lambda bi, i: (0, 0))
    mat = pl.BlockSpec((d, d), lambda bi, i: (0, 0), pipeline_mode=pl.Buffered(1))
    return pl.pallas_call(
        functools.partial(_merge_kernel, alpha=alpha),
        grid=(b, s // tm),
        in_specs=[blk, blk, sec(SEC_OG), blk, sec(SEC_GA), sec(SEC_GB), blk, mod, mod, mod,
                  vec, vec, vec, mat, mat, mat,
                  pl.BlockSpec((e, d), lambda bi, i: (0, 0)),
                  pl.BlockSpec((e, 1), lambda bi, i: (0, 0))],
        out_specs=[blk,
                   pl.BlockSpec((tm, d // 2), lambda bi, i: (bi * (s // tm) + i, 0)),
                   pl.BlockSpec((None, TOP_K, tm), lambda bi, i: (bi, 0, i)),
                   pl.BlockSpec((None, TOP_K, tm), lambda bi, i: (bi, 0, i)),
                   pl.BlockSpec((e, 128), lambda bi, i: (0, 0))],
        out_shape=[jax.ShapeDtypeStruct((b, s, d), F32),
                   jax.ShapeDtypeStruct((b * s, d // 2), U32),
                   jax.ShapeDtypeStruct((b, TOP_K, s), jnp.int32), jax.ShapeDtypeStruct((b, TOP_K, s), F32),
                   jax.ShapeDtypeStruct((e, 128), F32)],
        compiler_params=_cparams(("arbitrary", "arbitrary")),
        name="merge",
    )(o_f, o_b, p, y_na, p, p, x, g1, sh2, sc2, hg_norm_g.reshape(1, d), ln1_g.reshape(1, d),
      ln1_b.reshape(1, d), w_a, w_b, w_o, w_router_t, router_bias.reshape(e, 1))


MOE_TILE = 512
MOE_TOK = 512


def _plan_kernel(topi_ref, off_ref, dest_ref, carry_ref):
    @pl.when(pl.program_id(0) == 0)
    def _():
        carry_ref[...] = jnp.zeros_like(carry_ref)

    topi = topi_ref[...]
    tok = topi.shape[1]
    eidx = lax.broadcasted_iota(jnp.int32, (N_EXPERTS, tok), 0)
    hits = [eidx == topi[k:k + 1, :] for k in range(TOP_K)]
    m = jnp.zeros((N_EXPERTS, tok), F32)
    for hit in hits:
        m = jnp.where(hit, 1.0, m)
    before = (lax.broadcasted_iota(jnp.int32, (tok, tok), 0)
              < lax.broadcasted_iota(jnp.int32, (tok, tok), 1)).astype(F32).astype(BF16)
    row = off_ref[...] + carry_ref[...] + _dot(m.astype(BF16), before)
    dest = [jnp.sum(jnp.where(hit, row, 0.0), axis=0, keepdims=True) for hit in hits]
    dest_ref[...] = jnp.concatenate(dest, axis=0).astype(jnp.int32)
    carry_ref[...] += jnp.sum(m, axis=1, keepdims=True)


def _plan(topi, seg_off):
    b, k, s = topi.shape
    per_b = s // MOE_TOK
    blk = pl.BlockSpec((None, k, MOE_TOK), lambda i: (i // per_b, 0, i % per_b))
    return pl.pallas_call(
        _plan_kernel,
        grid=(b * per_b,),
        in_specs=[blk, pl.BlockSpec((N_EXPERTS, 1), lambda i: (0, 0))],
        out_specs=blk,
        out_shape=jax.ShapeDtypeStruct((b, k, s), jnp.int32),
        scratch_shapes=[pltpu.VMEM((N_EXPERTS, 1), F32)],
        compiler_params=_cparams(("arbitrary",)),
        name="plan",
    )(topi, seg_off.astype(F32).reshape(N_EXPERTS, 1))


SC_WINDOW = 128


def _sc_workers():
    info = plsc.get_sparse_core_info()
    return info.num_cores, info.num_cores * info.num_subcores


def _scatter_rows(src, idx, zero_idx, n_rows):
    t, w = src.shape
    m, mz = idx.shape[0], zero_idx.shape[0]
    n_cores, n_workers = _sc_workers()
    per_worker, per_worker_z = t // n_workers, mz // n_workers
    assert m % t == 0
    assert per_worker * n_workers == t and per_worker % SC_WINDOW == 0
    assert per_worker_z * n_workers == mz and per_worker_z % SC_WINDOW == 0
    mesh = plsc.VectorSubcoreMesh(core_axis_name="core", subcore_axis_name="subcore")

    copies = m // t
    zero_batch = per_worker_z // SC_WINDOW

    @functools.partial(
        pl.kernel, mesh=mesh, out_type=jax.ShapeDtypeStruct((n_rows, w), src.dtype),
        scratch_types=[pltpu.VMEM((copies, SC_WINDOW), jnp.int32), pltpu.VMEM((zero_batch, SC_WINDOW), jnp.int32),
                       pltpu.VMEM((SC_WINDOW, w), src.dtype), pltpu.SemaphoreType.DMA])
    def scatter(src_hbm, idx_hbm, zeros_hbm, zero_idx_hbm, out_hbm, idx_v, zidx_v, rows_v, sem):
        worker = lax.axis_index("subcore") * n_cores + lax.axis_index("core")

        def scatter_all(index_rows, n):
            for c in range(n):
                pltpu.async_copy(rows_v, out_hbm.at[index_rows.at[c]], sem)
            for c in range(n):
                pltpu.make_async_copy(rows_v, out_hbm.at[index_rows.at[c]], sem).wait()

        @pl.loop(0, per_worker // SC_WINDOW)
        def _(step):
            first = pl.multiple_of(worker * per_worker + step * SC_WINDOW, SC_WINDOW)
            pltpu.sync_copy(src_hbm.at[pl.ds(first, SC_WINDOW)], rows_v)
            pltpu.sync_copy(idx_hbm.at[:, pl.ds(first, SC_WINDOW)], idx_v)
            scatter_all(idx_v, copies)

        pltpu.sync_copy(zeros_hbm, rows_v)
        pltpu.sync_copy(zero_idx_hbm.at[worker], zidx_v)
        scatter_all(zidx_v, zero_batch)

    return scatter(src, idx.reshape(copies, t), jnp.zeros((SC_WINDOW, w), src.dtype),
                   zero_idx.reshape(n_workers, zero_batch, SC_WINDOW))


EXPERT_RING = 3


def _experts_kernel(te_ref, tb_ref, nt_ref, seg_ref, nxt_ref, xs_ref, wg_ref, wu_ref, wd_ref, ys_ref,
                    xbuf, sem, wg_f32, wu_f32, wd_f32, wg_s, wu_s, wd_s, wsem):
    i = pl.program_id(0)
    n_tiles = nt_ref[0]

    def weight_copies(expert, slot):
        return [pltpu.make_async_copy(src.at[expert], dst.at[slot], wsem.at[slot, n])
                for n, (src, dst) in enumerate(((wg_ref, wg_f32), (wu_ref, wu_f32), (wd_ref, wd_f32)))]

    @pl.when(i == 0)
    def _():
        for c in weight_copies(te_ref[0], 0):
            c.start()

    first = (i == 0) | (seg_ref[i] != seg_ref[jnp.maximum(i - 1, 0)])

    @pl.when((i < n_tiles) & first)
    def _():
        slot = seg_ref[i] & 1
        for c in weight_copies(te_ref[i], slot):
            c.wait()
        wg_s[...] = wg_f32[slot].astype(BF16)
        wu_s[...] = wu_f32[slot].astype(BF16)
        wd_s[...] = wd_f32[slot].astype(BF16)

        @pl.when(nxt_ref[i] >= 0)
        def _():
            for c in weight_copies(nxt_ref[i], 1 - slot):
                c.start()

    def tile_copy(j):
        slot = lax.rem(j, EXPERT_RING)
        row0 = pl.multiple_of(tb_ref[j] * MOE_TILE, MOE_TILE)
        return pltpu.make_async_copy(xs_ref.at[pl.ds(row0, MOE_TILE), :], xbuf.at[slot], sem.at[slot])

    @pl.when(i == 0)
    def _():
        for j in range(EXPERT_RING - 1):
            @pl.when(j < n_tiles)
            def _():
                tile_copy(j).start()

    ahead = i + (EXPERT_RING - 1)

    @pl.when(ahead < n_tiles)
    def _():
        tile_copy(ahead).start()

    @pl.when(i < n_tiles)
    def _():
        tile_copy(i).wait()
        x = _unpack_words(xbuf[lax.rem(i, EXPERT_RING)]).astype(BF16)
        act = _silu(_dot(x, wg_s[...])) * _dot(x, wu_s[...])
        ys_ref[...] = _pack_words(_dot(act.astype(BF16), wd_s[...]))


def _experts(xs, tile_expert, tile_block, n_tiles, tile_segment, next_expert, wg, wu, wd):
    d, f = wg.shape[1], wg.shape[2]
    anywhere = pl.BlockSpec(memory_space=pl.ANY)
    grid_spec = pltpu.PrefetchScalarGridSpec(
        num_scalar_prefetch=5,
        grid=(xs.shape[0] // MOE_TILE,),
        in_specs=[anywhere, anywhere, anywhere, anywhere],
        out_specs=pl.BlockSpec((MOE_TILE, d // 2), lambda i, te, tb, nt, seg, nxt: (tb[i], 0)),
        scratch_shapes=[pltpu.VMEM((EXPERT_RING, MOE_TILE, d // 2), U32), pltpu.SemaphoreType.DMA((EXPERT_RING,)),
                        pltpu.VMEM((2, d, f), F32), pltpu.VMEM((2, d, f), F32), pltpu.VMEM((2, f, d), F32),
                        pltpu.VMEM((d, f), BF16), pltpu.VMEM((d, f), BF16), pltpu.VMEM((f, d), BF16),
                        pltpu.SemaphoreType.DMA((2, 3))],
    )
    return pl.pallas_call(
        _experts_kernel,
        grid_spec=grid_spec,
        out_shape=jax.ShapeDtypeStruct(xs.shape, U32),
        compiler_params=_cparams(("arbitrary",)),
        name="experts",
    )(tile_expert, tile_block, n_tiles, tile_segment, next_expert, xs, wg, wu, wd)


def _gather_rows(table, idx):
    m = idx.shape[0]
    w = table.shape[1]
    win = SC_WINDOW // 2
    n_cores, n_workers = _sc_workers()
    per_worker = m // n_workers
    n_pairs = per_worker // (2 * win)
    assert per_worker * n_workers == m and n_pairs * 2 * win == per_worker
    mesh = plsc.VectorSubcoreMesh(core_axis_name="core", subcore_axis_name="subcore")

    @functools.partial(
        pl.kernel, mesh=mesh, out_type=jax.ShapeDtypeStruct((m, w), table.dtype),
        scratch_types=[pltpu.VMEM((win,), jnp.int32), pltpu.VMEM((win,), jnp.int32),
                       pltpu.VMEM((win, w), table.dtype), pltpu.VMEM((win, w), table.dtype),
                       pltpu.SemaphoreType.DMA, pltpu.SemaphoreType.DMA])
    def gather(table_hbm, idx_hbm, out_hbm, idx_a, idx_b, rows_a, rows_b, sem_a, sem_b):
        worker = lax.axis_index("subcore") * n_cores + lax.axis_index("core")
        start = worker * per_worker

        def request(first, idx_v, rows_v, sem):
            pltpu.sync_copy(idx_hbm.at[pl.ds(first, win)], idx_v)
            pltpu.async_copy(table_hbm.at[idx_v], rows_v, sem)

        def deliver(first, idx_v, rows_v, sem):
            pltpu.make_async_copy(table_hbm.at[idx_v], rows_v, sem).wait()
            pltpu.sync_copy(rows_v, out_hbm.at[pl.ds(first, win)])

        request(pl.multiple_of(start, win), idx_a, rows_a, sem_a)

        @pl.loop(0, n_pairs)
        def _(pair):
            first_a = pl.multiple_of(start + pair * 2 * win, win)
            first_b = pl.multiple_of(first_a + win, win)
            request(first_b, idx_b, rows_b, sem_b)
            deliver(first_a, idx_a, rows_a, sem_a)

            @pl.when(pair + 1 < n_pairs)
            def _():
                request(pl.multiple_of(first_b + win, win), idx_a, rows_a, sem_a)

            deliver(first_b, idx_b, rows_b, sem_b)

    return gather(table, idx)


def _combine_kernel(rows_ref, topw_ref, h_ref, x1_ref, g2_ref, sg_ref, su_ref, sd_ref, ln2g_ref, ln2b_ref,
                    o_ref, *, alpha):
    h = _unpack_words(h_ref[...]).astype(BF16)
    act = _silu(_dot(h, sg_ref[...])) * _dot(h, su_ref[...])
    y = _dot(act.astype(BF16), sd_ref[...])
    w = topw_ref[...].T
    for k in range(TOP_K):
        y = y + w[:, k:k + 1] * _unpack_words(rows_ref[k])
    o_ref[...] = _normalize(alpha * x1_ref[...] + g2_ref[...] * y) * ln2g_ref[...] + ln2b_ref[...]


def _combine(gathered, topw, h2p, x1, g2, sg, su, sd, ln2_g, ln2_b, alpha):
    t, d = x1.shape
    b, k, s = topw.shape
    per_b = s // MOE_TOK
    fs = sg.shape[1]
    rows = pl.BlockSpec((MOE_TOK, d), lambda i: (i, 0))
    packed = pl.BlockSpec((MOE_TOK, d // 2), lambda i: (i, 0))
    vec = pl.BlockSpec((1, d), lambda i: (0, 0))
    return pl.pallas_call(
        functools.partial(_combine_kernel, alpha=alpha),
        grid=(t // MOE_TOK,),
        in_specs=[pl.BlockSpec((k, MOE_TOK, d // 2), lambda i: (0, i, 0)),
                  pl.BlockSpec((None, k, MOE_TOK), lambda i: (i // per_b, 0, i % per_b)),
                  packed, rows,
                  pl.BlockSpec((None, 1, d), lambda i: (i // per_b, 0, 0)),
                  pl.BlockSpec((d, fs), lambda i: (0, 0)),
                  pl.BlockSpec((d, fs), lambda i: (0, 0)),
                  pl.BlockSpec((fs, d), lambda i: (0, 0)),
                  vec, vec],
        out_specs=rows,
        out_shape=jax.ShapeDtypeStruct((t, d), F32),
        compiler_params=_cparams(("arbitrary",)),
        name="combine",
    )(gathered, topw, h2p, x1, g2, sg, su, sd, ln2_g.reshape(1, d), ln2_b.reshape(1, d))


def _moe(h2p, topi, topw, cnt, x1, g2, wg, wu, wd, sg, su, sd, ln2_g, ln2_b, alpha):
    b, s, d = x1.shape
    t = b * s
    cnt = cnt[:, 0].astype(jnp.int32)
    tiles_e = (cnt + (MOE_TILE - 1)) // MOE_TILE
    tiles_cum = jnp.cumsum(tiles_e)
    seg_off = (tiles_cum - tiles_e) * MOE_TILE
    n_tiles_max = t * TOP_K // MOE_TILE + N_EXPERTS
    tile_block = jnp.minimum(jnp.arange(n_tiles_max, dtype=jnp.int32), tiles_cum[-1] - 1)
    tile_expert = jnp.sum((tiles_cum[None, :] <= tile_block[:, None]).astype(jnp.int32), axis=1)
    n_tiles = tiles_cum[-1:].astype(jnp.int32)
    present = tiles_e > 0
    seg_of_expert = jnp.cumsum(present.astype(jnp.int32)) - 1
    later = jnp.where(present, jnp.arange(N_EXPERTS, dtype=jnp.int32), N_EXPERTS)
    next_present = jnp.concatenate([lax.cummin(later, reverse=True)[1:], jnp.full((1,), N_EXPERTS, jnp.int32)])
    next_present = jnp.where(next_present < N_EXPERTS, next_present, -1)
    tile_segment = seg_of_expert[tile_expert].astype(jnp.int32)
    next_expert = next_present[tile_expert].astype(jnp.int32)

    dest = jnp.transpose(_plan(topi, seg_off), (1, 0, 2)).reshape(TOP_K * t)
    j = jnp.arange(MOE_TILE, dtype=jnp.int32)[None, :]
    n_pad = (tiles_e * MOE_TILE - cnt)[:, None]
    spare = (n_tiles_max - 1) * MOE_TILE + j
    zero_idx = jnp.where(j < n_pad, (seg_off + cnt)[:, None] + j, spare).reshape(N_EXPERTS * MOE_TILE)
    xs = _scatter_rows(h2p, dest, zero_idx.astype(jnp.int32), n_tiles_max * MOE_TILE)
    ys = _experts(xs, tile_expert, tile_block, n_tiles, tile_segment, next_expert, wg, wu, wd)
    gathered = _gather_rows(ys, dest)
    out = _combine(gathered.reshape(TOP_K, t, d // 2), topw, h2p, x1.reshape(t, d), g2, sg, su, sd,
                   ln2_g, ln2_b, alpha)
    return out.reshape(b, s, d)


def kernel(x, c, ctx, c_ctx, w_ada, b_ada, w_in, hg_lb_fwd, hg_lb_bwd, hg_norm_g, na_rpb, w_branch_a, w_branch_b, w_out, ln1_g, ln1_b, w_router, router_bias, w_e_gate, w_e_up, w_e_down, w_sh_gate, w_sh_up, w_sh_down, ln2_g, ln2_b):
    depth = w_ada.shape[0]
    assert depth == 1, "single-layer block"
    b, s, d = x.shape
    alpha = (2.0 * depth) ** 0.25
    l = 0
    lb_fwd = jnp.cumsum(jax.nn.softmax(hg_lb_fwd.astype(F32), axis=0), axis=0)[l]
    lb_bwd = jnp.cumsum(jax.nn.softmax(hg_lb_bwd.astype(F32), axis=0), axis=0)[l]

    cond_rows = jnp.concatenate([c, c_ctx[None, :], jnp.zeros((8 - b - 1, d), F32)], axis=0)
    mod = _ada(cond_rows, w_ada[l], b_ada[l])
    sh1, sc1, g1, sh2, sc2, g2 = [m[:b, None, :] for m in jnp.split(mod, 6, axis=-1)]
    csh1, csc1 = [jnp.broadcast_to(m[b:b + 1, None, :], (b, 1, d)) for m in jnp.split(mod, 6, axis=-1)[:2]]

    w_in_b = w_in[l].astype(BF16)
    p = _inproj(x, sh1, sc1, w_in_b, tuple(range(N_SECTIONS)))
    pc = _inproj(ctx, csh1, csc1, w_in_b, CTX_SECTIONS)

    o_f, o_b = _hgrn(p, pc, lb_fwd, lb_bwd)
    y_na = _natten(p, pc, *_na_tables(na_rpb[l], s))

    x1, h2, topi, topw, cnt = _merge(o_f, o_b, p, y_na, x, g1, sh2, sc2, hg_norm_g[l], ln1_g[l], ln1_b[l],
                                     w_branch_a[l].astype(BF16), w_branch_b[l].astype(BF16),
                                     w_out[l].astype(BF16), w_router[l].T, router_bias[l], alpha)

    return _moe(h2, topi, topw, cnt, x1, g2,
                w_e_gate[l], w_e_up[l], w_e_down[l],
                w_sh_gate[l].astype(BF16), w_sh_up[l].astype(BF16), w_sh_down[l].astype(BF16),
                ln2_g[l], ln2_b[l], alpha)
```

```python
import functools

import numpy as np
import jax
import jax.numpy as jnp
from jax import lax
from jax.experimental import pallas as pl
from jax.experimental.pallas import tpu as pltpu
from jax.experimental.pallas import tpu_sc as plsc

F32 = jnp.float32
BF16 = jnp.bfloat16

D_MODEL = 1024
GRID_W = 64
HG_HEADS = 8
HG_DK = 128
HG_CHUNK = 64
NA_HEADS = 16
NA_HD = 64
NA_WIN_R = 8
NA_WIN_C = 16
ROPE_THETA = 10000.0
NEG_INF = -1e30
N_EXPERTS = 64
EXPERT_DIM = 256
TOP_K = 8
N_GROUPS = 8
TOPK_GROUPS = 4
ROUTED_SCALE = 2.5
LN_EPS = 1e-6
N_SECTIONS = 10
SEC_Q, SEC_FF, SEC_FB, SEC_I, SEC_OG, SEC_NQ, SEC_NK, SEC_NV, SEC_GA, SEC_GB = range(10)
CTX_SECTIONS = (SEC_FF, SEC_FB, SEC_I, SEC_NK, SEC_NV)

VMEM_LIMIT = 56 * 1024 * 1024


def _cparams(sem):
    return pltpu.CompilerParams(dimension_semantics=sem, vmem_limit_bytes=VMEM_LIMIT)


def _normalize(x):
    mu = jnp.mean(x, axis=-1, keepdims=True)
    xc = x - mu
    var = jnp.mean(xc * xc, axis=-1, keepdims=True)
    return xc * lax.rsqrt(var + LN_EPS)


def _silu(x):
    return x * jax.nn.sigmoid(x)


def _dot(a, b):
    return jnp.dot(a, b, preferred_element_type=F32)


def _dot_nt(a, b):
    return lax.dot_general(a, b, (((1,), (1,)), ((), ())), preferred_element_type=F32)


def _dot_tn(a, b):
    return lax.dot_general(a, b, (((0,), (0,)), ((), ())), preferred_element_type=F32)


U32 = jnp.uint32


def _pack_words(x):
    half = x.shape[1] // 2
    lo = lax.bitcast_convert_type(x[:, :half].astype(BF16).astype(F32), U32) >> 16
    hi = lax.bitcast_convert_type(x[:, half:].astype(BF16).astype(F32), U32) & jnp.uint32(0xFFFF0000)
    return lo | hi


def _unpack_words(w):
    lo = lax.bitcast_convert_type(w << 16, F32)
    hi = lax.bitcast_convert_type(w & jnp.uint32(0xFFFF0000), F32)
    return jnp.concatenate([lo, hi], axis=-1)


def _split3(x):
    hi = x.astype(BF16)
    r1 = x - hi.astype(F32)
    mid = r1.astype(BF16)
    lo = (r1 - mid.astype(F32)).astype(BF16)
    return hi, mid, lo


def _ada_kernel(c_ref, w_ref, b_ref, o_ref):
    cond = _silu(c_ref[...])
    o_ref[...] = _dot(cond.astype(BF16), w_ref[...].astype(BF16)) + b_ref[...]


def _ada(cond_rows, w_ada, b_ada):
    r, d = cond_rows.shape
    n = w_ada.shape[1]
    tn = 1024
    return pl.pallas_call(
        _ada_kernel,
        grid=(n // tn,),
        in_specs=[pl.BlockSpec((r, d), lambda j: (0, 0)),
                  pl.BlockSpec((d, tn), lambda j: (0, j)),
                  pl.BlockSpec((1, tn), lambda j: (0, j))],
        out_specs=pl.BlockSpec((r, tn), lambda j: (0, j)),
        out_shape=jax.ShapeDtypeStruct((r, n), F32),
        compiler_params=_cparams(("arbitrary",)),
        name="ada",
    )(cond_rows, w_ada, b_ada.reshape(1, n))


INPROJ_TOK = 2048


def _inproj_kernel(x_ref, sh_ref, sc_ref, w_ref, o_ref, h_ref):
    @pl.when(pl.program_id(2) == 0)
    def _():
        h = _normalize(x_ref[...]) * (1.0 + sc_ref[...]) + sh_ref[...]
        h_ref[...] = h.astype(BF16)

    o_ref[...] = _dot(h_ref[...], w_ref[...])


def _inproj(x, shift, scale, w_in_bf16, sections):
    b, s, d = x.shape
    tm = min(INPROJ_TOK, s)
    nj = len(sections)

    def section(j):
        sec = sections[-1]
        for k in range(nj - 2, -1, -1):
            sec = jnp.where(j == k, sections[k], sec)
        return sec

    return pl.pallas_call(
        _inproj_kernel,
        grid=(b, s // tm, nj),
        in_specs=[pl.BlockSpec((None, tm, d), lambda bi, i, j: (bi, i, 0)),
                  pl.BlockSpec((None, 1, d), lambda bi, i, j: (bi, 0, 0)),
                  pl.BlockSpec((None, 1, d), lambda bi, i, j: (bi, 0, 0)),
                  pl.BlockSpec((d, d), lambda bi, i, j: (0, section(j)))],
        out_specs=pl.BlockSpec((None, None, tm, d), lambda bi, i, j: (j, bi, i, 0)),
        out_shape=jax.ShapeDtypeStruct((nj, b, s, d), F32),
        scratch_shapes=[pltpu.VMEM((tm, d), BF16)],
        compiler_params=_cparams(("arbitrary", "arbitrary", "arbitrary")),
        name="inproj",
    )(x, shift, scale, w_in_bf16)


def _hgrn_gates(q, fraw, v, lb, tri_bf16, last_row):
    f = lb + (1.0 - lb) * jax.nn.sigmoid(fraw)
    k = 1.0 - f
    lf = jnp.log(f)
    hi, mid, lo = _split3(lf)
    a = _dot(tri_bf16, hi) + _dot(tri_bf16, mid) + _dot(tri_bf16, lo)
    a_last = a[last_row:last_row + 1, :]
    kd = (k * jnp.exp(a_last - a)).astype(BF16)
    decay = jnp.exp(a_last)
    qa = kb = None
    if q is not None:
        qa = (_silu(q) * jnp.exp(a)).astype(BF16)
        kb = (k * jnp.exp(-a)).astype(BF16)
    return qa, kb, kd, v.astype(BF16), decay


def _hgrn_chunks(chunks, st_ref):
    first = []
    for d, ((qa, kb, kd, vb, decay), keep) in enumerate(chunks):
        for h in range(HG_HEADS):
            sl = slice(h * HG_DK, (h + 1) * HG_DK)
            st = st_ref[d, h]
            if qa is not None:
                first.append((_dot_nt(qa[:, sl], kb[:, sl]), _dot_nt(qa[:, sl], st.astype(BF16))))
            st_ref[d, h] = st * decay[:, sl] + _dot_tn(vb[:, sl], kd[:, sl])
    results = []
    for d, ((qa, kb, kd, vb, decay), keep) in enumerate(chunks):
        if qa is None:
            results.append(None)
            continue
        outs = []
        for h in range(HG_HEADS):
            sl = slice(h * HG_DK, (h + 1) * HG_DK)
            s_qk, o_state = first.pop(0)
            outs.append(_dot(jnp.where(keep, s_qk, 0.0).astype(BF16), vb[:, sl]) + o_state)
        results.append(jnp.concatenate(outs, axis=-1))
    return results


def _hgrn_kernel(qf_ref, ff_ref, if_ref, qb_ref, fb_ref, ib_ref, cff_ref, cfb_ref, ci_ref,
                 lbf_ref, lbb_ref, of_ref, ob_ref, st_ref, *, n_sub, n_ctx_sub):
    n = pl.program_id(1)
    c = HG_CHUNK
    row = lax.broadcasted_iota(jnp.int32, (c, c), 0)
    col = lax.broadcasted_iota(jnp.int32, (c, c), 1)
    keep_f = col <= row
    keep_b = col >= row
    tri_f = keep_f.astype(F32).astype(BF16)
    tri_b = keep_b.astype(F32).astype(BF16)
    lbf = lbf_ref[...]
    lbb = lbb_ref[...]

    @pl.when(n == 0)
    def _():
        st_ref[...] = jnp.zeros_like(st_ref)

        def body(i, carry):
            r0 = pl.multiple_of(i * c, c)
            r1 = pl.multiple_of((n_ctx_sub - 1 - i) * c, c)
            gf = _hgrn_gates(None, cff_ref[pl.ds(r0, c), :], ci_ref[pl.ds(r0, c), :], lbf, tri_f, c - 1)
            gb = _hgrn_gates(None, cfb_ref[pl.ds(r1, c), :], ci_ref[pl.ds(r1, c), :], lbb, tri_b, 0)
            _hgrn_chunks([(gf, keep_f), (gb, keep_b)], st_ref)
            return carry

        lax.fori_loop(0, n_ctx_sub, body, 0)

    @pl.when(n > 0)
    def _():
        def body(i, carry):
            r0 = pl.multiple_of(i * c, c)
            r1 = pl.multiple_of((n_sub - 1 - i) * c, c)
            gf = _hgrn_gates(qf_ref[pl.ds(r0, c), :], ff_ref[pl.ds(r0, c), :], if_ref[pl.ds(r0, c), :],
                             lbf, tri_f, c - 1)
            gb = _hgrn_gates(qb_ref[pl.ds(r1, c), :], fb_ref[pl.ds(r1, c), :], ib_ref[pl.ds(r1, c), :],
                             lbb, tri_b, 0)
            o_f, o_b = _hgrn_chunks([(gf, keep_f), (gb, keep_b)], st_ref)
            of_ref[pl.ds(r0, c), :] = o_f
            ob_ref[pl.ds(r1, c), :] = o_b
            return carry

        lax.fori_loop(0, n_sub, body, 0, unroll=True)


def _hgrn(p, pc, lb_fwd, lb_bwd):
    _, b, s, w = p.shape
    ctx_len = pc.shape[2]
    tb = min(512, s)
    nb = s // tb
    fwd = lambda bi, n: jnp.maximum(n - 1, 0)
    bwd = lambda bi, n: nb - 1 - jnp.maximum(n - 1, 0)

    def sec(section, blk):
        return pl.BlockSpec((None, None, tb, w), lambda bi, n: (section, bi, blk(bi, n), 0))

    def csec(section):
        return pl.BlockSpec((None, None, ctx_len, w), lambda bi, n: (CTX_SECTIONS.index(section), bi, 0, 0))

    vec = pl.BlockSpec((1, w), lambda bi, n: (0, 0))
    kern = functools.partial(_hgrn_kernel, n_sub=tb // HG_CHUNK, n_ctx_sub=ctx_len // HG_CHUNK)
    return pl.pallas_call(
        kern,
        grid=(b, nb + 1),
        in_specs=[sec(SEC_Q, fwd), sec(SEC_FF, fwd), sec(SEC_I, fwd),
                  sec(SEC_Q, bwd), sec(SEC_FB, bwd), sec(SEC_I, bwd),
                  csec(SEC_FF), csec(SEC_FB), csec(SEC_I), vec, vec],
        out_specs=[pl.BlockSpec((None, tb, w), lambda bi, n: (bi, fwd(bi, n), 0)),
                   pl.BlockSpec((None, tb, w), lambda bi, n: (bi, bwd(bi, n), 0))],
        out_shape=[jax.ShapeDtypeStruct((b, s, w), F32), jax.ShapeDtypeStruct((b, s, w), F32)],
        scratch_shapes=[pltpu.VMEM((2, HG_HEADS, HG_DK, HG_DK), F32)],
        compiler_params=_cparams(("arbitrary", "arbitrary")),
        name="hgrn",
    )(p, p, p, p, p, p, pc, pc, pc, lb_fwd.reshape(1, w), lb_bwd.reshape(1, w))


NA_ROWS_PER_STEP = 64
NA_PREP_ROWS = 512
NA_KEY_TILE = 128
NA_SPAN = NA_WIN_R * GRID_W


def _rope(t, cos, sin_signed, first_half):
    w = t.shape[-1]
    partner = jnp.where(first_half, pltpu.roll(t, w - 16, 1), pltpu.roll(t, 16, 1))
    return t * cos + partner * sin_signed


def _fold_lanes(op, *arrays):
    tiles = [a[:, c:c + 128] for a in arrays for c in range(0, a.shape[-1], 128)]
    acc = tiles[0]
    for t in tiles[1:]:
        acc = op(acc, t)
    return acc


def _rope_tables(rowtab_ref, coltab_ref, row0, n_rows, row_lane):
    out = []
    for i in range(2):
        rt = rowtab_ref[i, pl.ds(row0, n_rows), :]
        by_row = jnp.concatenate([jnp.broadcast_to(rt[r:r + 1, :], (GRID_W, rt.shape[1])) for r in range(n_rows)],
                                 axis=0)
        by_col = jnp.concatenate([coltab_ref[i]] * n_rows, axis=0)
        out.append(jnp.where(row_lane, by_row, by_col))
    return out


def _natten_kernel(q_ref, k_ref, v_ref, kc_ref, vc_ref, rowtab_ref, coltab_ref, rp_ref, o_ref,
                   kt_s, v_s, kc_s, vc_s, bias_s, tail_s, *, rows):
    rblk = pl.program_id(2)
    hd = NA_HD
    lane = lax.broadcasted_iota(jnp.int32, (1, 2 * hd), 1)
    first_half = (lane % 32) < 16
    row_lane = (lane % hd) < hd // 2
    scale = NA_HD ** -0.5

    def values_and_ones(v_pair, h):
        vh = v_pair if h == 0 else pltpu.roll(v_pair, hd, 1)
        return jnp.where(lane < hd, vh, jnp.where(lane == hd, 1.0, 0.0)).astype(BF16)

    @pl.when(rblk == 0)
    def _():
        kc = kc_ref[...].astype(BF16)
        qi = lax.broadcasted_iota(jnp.int32, (GRID_W, GRID_W), 0)
        ki = lax.broadcasted_iota(jnp.int32, (GRID_W, GRID_W), 1)
        cstart = jnp.clip(qi - NA_WIN_C // 2, 0, GRID_W - NA_WIN_C)
        in_win = (ki >= cstart) & (ki < cstart + NA_WIN_C)
        s_len = k_ref.shape[0]
        tail_s[...] = jnp.zeros_like(tail_s)
        for h in range(2):
            sl = slice(h * hd, (h + 1) * hd)
            kc_s[h] = kc[:, sl]
            vc_s[h] = values_and_ones(vc_ref[...], h)
            tiles = []
            for dr in range(2 * NA_WIN_R - 1):
                table = jnp.broadcast_to(rp_ref[h, dr:dr + 1, :], (GRID_W, 2 * GRID_W))
                skew = pltpu.roll(table, GRID_W + 1, 1, stride=1, stride_axis=0)
                tiles.append(jnp.where(in_win, skew[:, :GRID_W], NEG_INF))
            for v in range(NA_WIN_R):
                for j in range(NA_WIN_R):
                    bias_s[h, v, :, j * GRID_W:(j + 1) * GRID_W] = tiles[NA_WIN_R - 1 - v + j]

        eye = (lax.broadcasted_iota(jnp.int32, (2 * hd, 2 * hd), 0)
               == lax.broadcasted_iota(jnp.int32, (2 * hd, 2 * hd), 1)).astype(F32).astype(BF16)

        def prep(i, carry):
            r0 = pl.multiple_of(i * NA_PREP_ROWS, NA_PREP_ROWS)
            rws = pl.ds(r0, NA_PREP_ROWS)
            cos, sin = _rope_tables(rowtab_ref, coltab_ref, i * (NA_PREP_ROWS // GRID_W), NA_PREP_ROWS // GRID_W,
                                    row_lane)
            kr = _rope(k_ref[rws, :], cos, sin, first_half)
            kr_odd = jnp.concatenate([tail_s[...], kr[:NA_PREP_ROWS - GRID_W]], axis=0)
            tail_s[...] = kr[NA_PREP_ROWS - GRID_W:]
            krt = [_dot_nt(eye, kr.astype(BF16)).astype(BF16),
                   _dot_nt(eye, kr_odd.astype(BF16)).astype(BF16)]
            vv = v_ref[rws, :]
            for h in range(2):
                sl = slice(h * hd, (h + 1) * hd)
                for par in range(2):
                    for c in range(NA_PREP_ROWS // NA_KEY_TILE):
                        kt_s[h, par, i * (NA_PREP_ROWS // NA_KEY_TILE) + c] = (
                            krt[par][sl, c * NA_KEY_TILE:(c + 1) * NA_KEY_TILE])
                v_s[h, rws, :] = values_and_ones(vv, h)
            return carry

        lax.fori_loop(0, s_len // NA_PREP_ROWS, prep, 0, unroll=4)

    tq = NA_ROWS_PER_STEP * GRID_W
    q = q_ref[...] * scale
    cos, sin = _rope_tables(rowtab_ref, coltab_ref, rblk * NA_ROWS_PER_STEP, NA_ROWS_PER_STEP, row_lane)
    qr = _rope(q, cos, sin, first_half)
    qb = q.astype(BF16)
    qrb = qr.astype(BF16)
    rws = [slice(rr * GRID_W, (rr + 1) * GRID_W) for rr in range(NA_ROWS_PER_STEP)]
    par, slot0, key0, bidx = [], [], [], []
    for rr in range(NA_ROWS_PER_STEP):
        r = rblk * NA_ROWS_PER_STEP + rr
        rs = jnp.clip(r - NA_WIN_R // 2, 0, rows - NA_WIN_R)
        par.append(rs & 1)
        slot0.append(lax.shift_right_logical(rs, 1) + (rs & 1))
        key0.append(pl.multiple_of(rs * GRID_W, GRID_W))
        bidx.append(r - rs)

    def scores(h):
        sl = slice(h * hd, (h + 1) * hd)
        qrb_h = qrb[:, sl]
        s_ctx_all = _dot_nt(qb[:, sl], kc_s[h])
        s_win = []
        for rr in range(NA_ROWS_PER_STEP):
            kt = kt_s[h, par[rr], pl.ds(slot0[rr], NA_SPAN // NA_KEY_TILE)]
            kt = jnp.concatenate([kt[c] for c in range(NA_SPAN // NA_KEY_TILE)], axis=-1)
            s_win.append(_dot(qrb_h[rws[rr]], kt))
        return s_win, s_ctx_all

    def softmax(h, s_win, s_ctx_all):
        e_win, e_ctx = [], []
        for rr in range(NA_ROWS_PER_STEP):
            sw = s_win[rr] + bias_s[h, bidx[rr]]
            sc = s_ctx_all[rws[rr]]
            m = jnp.max(_fold_lanes(jnp.maximum, sw, sc), axis=-1, keepdims=True)
            e_win.append(jnp.exp(sw - m).astype(BF16))
            e_ctx.append(jnp.exp(sc - m).astype(BF16))
        return e_win, e_ctx

    def values(h, e_win, e_ctx):
        o_win = []
        for rr in range(NA_ROWS_PER_STEP):
            o_win.append(_dot(e_win[rr], v_s[h, pl.ds(key0[rr], NA_SPAN), :]))
        o = jnp.concatenate(o_win, axis=0) + _dot(jnp.concatenate(e_ctx, axis=0), vc_s[h])
        return o[:, :hd] * (1.0 / o[:, hd:hd + 1])

    s0 = scores(0)
    s1 = scores(1)
    p0 = softmax(0, *s0)
    o0 = values(0, *p0)
    p1 = softmax(1, *s1)
    o1 = values(1, *p1)
    o_ref[...] = jnp.concatenate([o0, o1], axis=-1).astype(o_ref.dtype)


def _na_tables(rpb, s):
    half = NA_HD // 2
    inv = jnp.power(ROPE_THETA, -jnp.arange(0, half, 2, dtype=F32) / half)

    def tables(n):
        ang = jnp.arange(n, dtype=F32)[:, None] * inv[None, :]
        reps = 2 * NA_HD // half
        return jnp.stack([jnp.tile(jnp.cos(ang), (1, 2 * reps)),
                          jnp.tile(jnp.concatenate([-jnp.sin(ang), jnp.sin(ang)], axis=-1), (1, reps))])

    rowtab, coltab = tables(s // GRID_W), tables(GRID_W)

    pad = GRID_W - NA_WIN_C
    rp = jnp.pad(rpb.astype(F32), ((0, 0), (0, 0), (pad, pad + 1)), mode="edge")
    return rowtab, coltab, rp


def _natten(p, pc, rowtab, coltab, rp):
    _, b, s, w = p.shape
    ctx_len = pc.shape[2]
    rows = s // GRID_W
    assert rows >= NA_WIN_R and rows % NA_ROWS_PER_STEP == 0
    tq = NA_ROWS_PER_STEP * GRID_W
    hw = 2 * NA_HD
    nhp = w // hw
    kern = functools.partial(_natten_kernel, rows=rows)
    return pl.pallas_call(
        kern,
        grid=(b, nhp, rows // NA_ROWS_PER_STEP),
        in_specs=[pl.BlockSpec((None, None, tq, hw), lambda bi, hp, r: (SEC_NQ, bi, r, hp)),
                  pl.BlockSpec((None, None, s, hw), lambda bi, hp, r: (SEC_NK, bi, 0, hp)),
                  pl.BlockSpec((None, None, s, hw), lambda bi, hp, r: (SEC_NV, bi, 0, hp)),
                  pl.BlockSpec((None, None, ctx_len, hw), lambda bi, hp, r: (CTX_SECTIONS.index(SEC_NK), bi, 0, hp)),
                  pl.BlockSpec((None, None, ctx_len, hw), lambda bi, hp, r: (CTX_SECTIONS.index(SEC_NV), bi, 0, hp)),
                  pl.BlockSpec((2, rows, hw), lambda bi, hp, r: (0, 0, 0)),
                  pl.BlockSpec((2, GRID_W, hw), lambda bi, hp, r: (0, 0, 0)),
                  pl.BlockSpec((2, 2 * NA_WIN_R - 1, 2 * GRID_W), lambda bi, hp, r: (hp, 0, 0))],
        out_specs=pl.BlockSpec((None, tq, hw), lambda bi, hp, r: (bi, r, hp)),
        out_shape=jax.ShapeDtypeStruct((b, s, w), BF16),
        scratch_shapes=[pltpu.VMEM((2, 2, s // NA_KEY_TILE, NA_HD, NA_KEY_TILE), BF16),
                        pltpu.VMEM((2, s, hw), BF16),
                        pltpu.VMEM((2, ctx_len, NA_HD), BF16), pltpu.VMEM((2, ctx_len, hw), BF16),
                        pltpu.VMEM((2, NA_WIN_R, GRID_W, NA_SPAN), F32),
                        pltpu.VMEM((GRID_W, hw), F32)],
        compiler_params=_cparams(("arbitrary", "arbitrary", "arbitrary")),
        name="natten",
    )(p, p, p, pc, pc, rowtab, coltab, rp)


def _route(logits_t, rbias):
    e, t = logits_t.shape
    gsz = e // N_GROUPS
    scores = jax.nn.sigmoid(logits_t)
    sel = scores + rbias
    neg = -jnp.inf
    sub = lax.broadcasted_iota(jnp.int32, (gsz, t), 0).astype(F32)
    gscore = []
    for g in range(N_GROUPS):
        grp = sel[g * gsz:(g + 1) * gsz, :]
        m1 = jnp.max(grp, axis=0, keepdims=True)
        first = jnp.min(jnp.where(grp == m1, sub, float(gsz)), axis=0, keepdims=True)
        m2 = jnp.max(jnp.where(sub == first, neg, grp), axis=0, keepdims=True)
        gscore.append(m1 + m2)
    masked = []
    for g in range(N_GROUPS):
        rank = jnp.zeros((1, t), F32)
        for g2 in range(N_GROUPS):
            if g2 == g:
                continue
            if g2 < g:
                ahead = gscore[g2] >= gscore[g]
            else:
                ahead = gscore[g2] > gscore[g]
            rank = rank + jnp.where(ahead, 1.0, 0.0)
        masked.append(jnp.where(rank < TOPK_GROUPS, sel[g * gsz:(g + 1) * gsz, :], neg))
    work = jnp.concatenate(masked, axis=0)
    eidx = lax.broadcasted_iota(jnp.int32, (e, t), 0).astype(F32)
    idxs, ws = [], []
    chosen = jnp.zeros((e, t), F32)
    for _ in range(TOP_K):
        m = jnp.max(work, axis=0, keepdims=True)
        first = jnp.min(jnp.where(work == m, eidx, float(e)), axis=0, keepdims=True)
        pick = eidx == first
        idxs.append(first)
        ws.append(jnp.sum(jnp.where(pick, scores, 0.0), axis=0, keepdims=True))
        chosen = jnp.where(pick, 1.0, chosen)
        work = jnp.where(pick, neg, work)
    w = jnp.concatenate(ws, axis=0)
    w = w / jnp.sum(w, axis=0, keepdims=True) * ROUTED_SCALE
    return jnp.concatenate(idxs, axis=0).astype(jnp.int32), w, chosen


MERGE_TOK = 512
MERGE_SUB = 256


def _merge_kernel(of_ref, ob_ref, og_ref, yna_ref, ga_ref, gb_ref, x_ref, g1_ref, sh2_ref, sc2_ref,
                  hgg_ref, ln1g_ref, ln1b_ref, wa_ref, wb_ref, wo_ref, wr_ref, rb_ref,
                  x1_ref, h2_ref, topi_ref, topw_ref, cnt_ref, *, alpha):
    tm = x_ref.shape[0]
    subs = [slice(i * MERGE_SUB, (i + 1) * MERGE_SUB) for i in range(tm // MERGE_SUB)]

    def branches(rows):
        o = of_ref[rows, :] + ob_ref[rows, :]
        parts = []
        for h in range(HG_HEADS):
            oh = o[:, h * HG_DK:(h + 1) * HG_DK]
            parts.append(oh * lax.rsqrt(jnp.mean(oh * oh, axis=-1, keepdims=True) + LN_EPS))
        y_hg = jnp.concatenate(parts, axis=-1) * hgg_ref[...] * _silu(og_ref[rows, :])
        return _dot(y_hg.astype(BF16), wa_ref[...]), _dot(yna_ref[rows, :], wb_ref[...])

    def out_proj(rows, ya, yb):
        t = jax.nn.sigmoid(ga_ref[rows, :]) * ya + jax.nn.sigmoid(gb_ref[rows, :]) * yb
        return _dot(t.astype(BF16), wo_ref[...])

    def norms_router(rows, i, y):
        x1 = _normalize(alpha * x_ref[rows, :] + g1_ref[...] * y) * ln1g_ref[...] + ln1b_ref[...]
        x1_ref[rows, :] = x1
        h2 = _normalize(x1) * (1.0 + sc2_ref[...]) + sh2_ref[...]
        h2_ref[rows, :] = _pack_words(h2)
        hh, hm, hl = _split3(h2)
        wh, wm, wl = _split3(wr_ref[...])
        return (_dot_nt(wh, hh) + _dot_nt(wh, hm) + _dot_nt(wm, hh)
                + _dot_nt(wh, hl) + _dot_nt(wl, hh) + _dot_nt(wm, hm))

    ab = [branches(rows) for rows in subs]
    ys = [out_proj(rows, *ab[i]) for i, rows in enumerate(subs)]
    logits = [norms_router(rows, i, ys[i]) for i, rows in enumerate(subs)]

    @pl.when((pl.program_id(0) == 0) & (pl.program_id(1) == 0))
    def _():
        cnt_ref[...] = jnp.zeros_like(cnt_ref)

    for i, rows in enumerate(subs):
        topi, topw, chosen = _route(logits[i], rb_ref[...])
        topi_ref[:, rows] = topi
        topw_ref[:, rows] = topw
        cnt_ref[...] += jnp.sum(chosen, axis=1, keepdims=True)


def _merge(o_f, o_b, p, y_na, x, g1, sh2, sc2, hg_norm_g, ln1_g, ln1_b, w_a, w_b, w_o, w_router_t, router_bias,
           alpha):
    b, s, d = x.shape
    tm = min(MERGE_TOK, s)
    e = w_router_t.shape[0]
    tok = lambda bi, i: (bi, i, 0)
    blk = pl.BlockSpec((None, tm, d), tok)

    def sec(section):
        return pl.BlockSpec((None, None, tm, d), lambda bi, i: (section, bi, i, 0))

    mod = pl.BlockSpec((None, 1, d), lambda bi, i: (bi, 0, 0))
    vec = pl.BlockSpec((1, d), lambda bi, i: (0, 0))
    mat = pl.BlockSpec((d, d), lambda bi, i: (0, 0), pipeline_mode=pl.Buffered(1))
    return pl.pallas_call(
        functools.partial(_merge_kernel, alpha=alpha),
        grid=(b, s // tm),
        in_specs=[blk, blk, sec(SEC_OG), blk, sec(SEC_GA), sec(SEC_GB), blk, mod, mod, mod,
                  vec, vec, vec, mat, mat, mat,
                  pl.BlockSpec((e, d), lambda bi, i: (0, 0)),
                  pl.BlockSpec((e, 1), lambda bi, i: (0, 0))],
        out_specs=[blk,
                   pl.BlockSpec((tm, d // 2), lambda bi, i: (bi * (s // tm) + i, 0)),
                   pl.BlockSpec((None, TOP_K, tm), lambda bi, i: (bi, 0, i)),
                   pl.BlockSpec((None, TOP_K, tm), lambda bi, i: (bi, 0, i)),
                   pl.BlockSpec((e, 128), lambda bi, i: (0, 0))],
        out_shape=[jax.ShapeDtypeStruct((b, s, d), F32),
                   jax.ShapeDtypeStruct((b * s, d // 2), U32),
                   jax.ShapeDtypeStruct((b, TOP_K, s), jnp.int32), jax.ShapeDtypeStruct((b, TOP_K, s), F32),
                   jax.ShapeDtypeStruct((e, 128), F32)],
        compiler_params=_cparams(("arbitrary", "arbitrary")),
        name="merge",
    )(o_f, o_b, p, y_na, p, p, x, g1, sh2, sc2, hg_norm_g.reshape(1, d), ln1_g.reshape(1, d),
      ln1_b.reshape(1, d), w_a, w_b, w_o, w_router_t, router_bias.reshape(e, 1))


MOE_TILE = 512
MOE_TOK = 512


def _plan_kernel(topi_ref, off_ref, dest_ref, carry_ref):
    @pl.when(pl.program_id(0) == 0)
    def _():
        carry_ref[...] = jnp.zeros_like(carry_ref)

    topi = topi_ref[...]
    tok = topi.shape[1]
    eidx = lax.broadcasted_iota(jnp.int32, (N_EXPERTS, tok), 0)
    hits = [eidx == topi[k:k + 1, :] for k in range(TOP_K)]
    m = jnp.zeros((N_EXPERTS, tok), F32)
    for hit in hits:
        m = jnp.where(hit, 1.0, m)
    before = (lax.broadcasted_iota(jnp.int32, (tok, tok), 0)
              < lax.broadcasted_iota(jnp.int32, (tok, tok), 1)).astype(F32).astype(BF16)
    row = off_ref[...] + carry_ref[...] + _dot(m.astype(BF16), before)
    dest = [jnp.sum(jnp.where(hit, row, 0.0), axis=0, keepdims=True) for hit in hits]
    dest_ref[...] = jnp.concatenate(dest, axis=0).astype(jnp.int32)
    carry_ref[...] += jnp.sum(m, axis=1, keepdims=True)


def _plan(topi, seg_off):
    b, k, s = topi.shape
    per_b = s // MOE_TOK
    blk = pl.BlockSpec((None, k, MOE_TOK), lambda i: (i // per_b, 0, i % per_b))
    return pl.pallas_call(
        _plan_kernel,
        grid=(b * per_b,),
        in_specs=[blk, pl.BlockSpec((N_EXPERTS, 1), lambda i: (0, 0))],
        out_specs=blk,
        out_shape=jax.ShapeDtypeStruct((b, k, s), jnp.int32),
        scratch_shapes=[pltpu.VMEM((N_EXPERTS, 1), F32)],
        compiler_params=_cparams(("arbitrary",)),
        name="plan",
    )(topi, seg_off.astype(F32).reshape(N_EXPERTS, 1))


SC_WINDOW = 128


def _sc_workers():
    info = plsc.get_sparse_core_info()
    return info.num_cores, info.num_cores * info.num_subcores


def _scatter_rows(src, idx, zero_idx, n_rows):
    t, w = src.shape
    m, mz = idx.shape[0], zero_idx.shape[0]
    n_cores, n_workers = _sc_workers()
    per_worker, per_worker_z = t // n_workers, mz // n_workers
    assert m % t == 0
    assert per_worker * n_workers == t and per_worker % SC_WINDOW == 0
    assert per_worker_z * n_workers == mz and per_worker_z % SC_WINDOW == 0
    mesh = plsc.VectorSubcoreMesh(core_axis_name="core", subcore_axis_name="subcore")

    copies = m // t
    zero_batch = per_worker_z // SC_WINDOW

    @functools.partial(
        pl.kernel, mesh=mesh, out_type=jax.ShapeDtypeStruct((n_rows, w), src.dtype),
        scratch_types=[pltpu.VMEM((copies, SC_WINDOW), jnp.int32), pltpu.VMEM((zero_batch, SC_WINDOW), jnp.int32),
                       pltpu.VMEM((SC_WINDOW, w), src.dtype), pltpu.SemaphoreType.DMA])
    def scatter(src_hbm, idx_hbm, zeros_hbm, zero_idx_hbm, out_hbm, idx_v, zidx_v, rows_v, sem):
        worker = lax.axis_index("subcore") * n_cores + lax.axis_index("core")

        def scatter_all(index_rows, n):
            for c in range(n):
                pltpu.async_copy(rows_v, out_hbm.at[index_rows.at[c]], sem)
            for c in range(n):
                pltpu.make_async_copy(rows_v, out_hbm.at[index_rows.at[c]], sem).wait()

        @pl.loop(0, per_worker // SC_WINDOW)
        def _(step):
            first = pl.multiple_of(worker * per_worker + step * SC_WINDOW, SC_WINDOW)
            pltpu.sync_copy(src_hbm.at[pl.ds(first, SC_WINDOW)], rows_v)
            pltpu.sync_copy(idx_hbm.at[:, pl.ds(first, SC_WINDOW)], idx_v)
            scatter_all(idx_v, copies)

        pltpu.sync_copy(zeros_hbm, rows_v)
        pltpu.sync_copy(zero_idx_hbm.at[worker], zidx_v)
        scatter_all(zidx_v, zero_batch)

    return scatter(src, idx.reshape(copies, t), jnp.zeros((SC_WINDOW, w), src.dtype),
                   zero_idx.reshape(n_workers, zero_batch, SC_WINDOW))


EXPERT_RING = 4


def _experts_kernel(te_ref, tb_ref, nt_ref, seg_ref, nxt_ref, xs_ref, wg_ref, wu_ref, wd_ref, ys_ref,
                    xbuf, sem, wg_f32, wu_f32, wd_f32, wg_s, wu_s, wd_s, wsem):
    i = pl.program_id(0)
    n_tiles = nt_ref[0]

    def weight_copies(expert, slot):
        return [pltpu.make_async_copy(src.at[expert], dst.at[slot], wsem.at[slot, n])
                for n, (src, dst) in enumerate(((wg_ref, wg_f32), (wu_ref, wu_f32), (wd_ref, wd_f32)))]

    @pl.when(i == 0)
    def _():
        for c in weight_copies(te_ref[0], 0):
            c.start()

    first = (i == 0) | (seg_ref[i] != seg_ref[jnp.maximum(i - 1, 0)])

    @pl.when((i < n_tiles) & first)
    def _():
        slot = seg_ref[i] & 1
        for c in weight_copies(te_ref[i], slot):
            c.wait()
        wg_s[...] = wg_f32[slot].astype(BF16)
        wu_s[...] = wu_f32[slot].astype(BF16)
        wd_s[...] = wd_f32[slot].astype(BF16)

        @pl.when(nxt_ref[i] >= 0)
        def _():
            for c in weight_copies(nxt_ref[i], 1 - slot):
                c.start()

    def tile_copy(j):
        slot = lax.rem(j, EXPERT_RING)
        row0 = pl.multiple_of(tb_ref[j] * MOE_TILE, MOE_TILE)
        return pltpu.make_async_copy(xs_ref.at[pl.ds(row0, MOE_TILE), :], xbuf.at[slot], sem.at[slot])

    @pl.when(i == 0)
    def _():
        for j in range(EXPERT_RING - 1):
            @pl.when(j < n_tiles)
            def _():
                tile_copy(j).start()

    ahead = i + (EXPERT_RING - 1)

    @pl.when(ahead < n_tiles)
    def _():
        tile_copy(ahead).start()

    @pl.when(i < n_tiles)
    def _():
        tile_copy(i).wait()
        x = _unpack_words(xbuf[lax.rem(i, EXPERT_RING)]).astype(BF16)
        act = _silu(_dot(x, wg_s[...])) * _dot(x, wu_s[...])
        ys_ref[...] = _pack_words(_dot(act.astype(BF16), wd_s[...]))


def _experts(xs, tile_expert, tile_block, n_tiles, tile_segment, next_expert, wg, wu, wd):
    d, f = wg.shape[1], wg.shape[2]
    anywhere = pl.BlockSpec(memory_space=pl.ANY)
    grid_spec = pltpu.PrefetchScalarGridSpec(
        num_scalar_prefetch=5,
        grid=(xs.shape[0] // MOE_TILE,),
        in_specs=[anywhere, anywhere, anywhere, anywhere],
        out_specs=pl.BlockSpec((MOE_TILE, d // 2), lambda i, te, tb, nt, seg, nxt: (tb[i], 0)),
        scratch_shapes=[pltpu.VMEM((EXPERT_RING, MOE_TILE, d // 2), U32), pltpu.SemaphoreType.DMA((EXPERT_RING,)),
                        pltpu.VMEM((2, d, f), F32), pltpu.VMEM((2, d, f), F32), pltpu.VMEM((2, f, d), F32),
                        pltpu.VMEM((d, f), BF16), pltpu.VMEM((d, f), BF16), pltpu.VMEM((f, d), BF16),
                        pltpu.SemaphoreType.DMA((2, 3))],
    )
    return pl.pallas_call(
        _experts_kernel,
        grid_spec=grid_spec,
        out_shape=jax.ShapeDtypeStruct(xs.shape, U32),
        compiler_params=_cparams(("arbitrary",)),
        name="experts",
    )(tile_expert, tile_block, n_tiles, tile_segment, next_expert, xs, wg, wu, wd)


def _gather_rows(table, idx):
    m = idx.shape[0]
    w = table.shape[1]
    win = SC_WINDOW // 2
    n_cores, n_workers = _sc_workers()
    per_worker = m // n_workers
    n_pairs = per_worker // (2 * win)
    assert per_worker * n_workers == m and n_pairs * 2 * win == per_worker
    mesh = plsc.VectorSubcoreMesh(core_axis_name="core", subcore_axis_name="subcore")

    @functools.partial(
        pl.kernel, mesh=mesh, out_type=jax.ShapeDtypeStruct((m, w), table.dtype),
        scratch_types=[pltpu.VMEM((win,), jnp.int32), pltpu.VMEM((win,), jnp.int32),
                       pltpu.VMEM((win, w), table.dtype), pltpu.VMEM((win, w), table.dtype),
                       pltpu.SemaphoreType.DMA, pltpu.SemaphoreType.DMA])
    def gather(table_hbm, idx_hbm, out_hbm, idx_a, idx_b, rows_a, rows_b, sem_a, sem_b):
        worker = lax.axis_index("subcore") * n_cores + lax.axis_index("core")
        start = worker * per_worker

        def request(first, idx_v, rows_v, sem):
            pltpu.sync_copy(idx_hbm.at[pl.ds(first, win)], idx_v)
            pltpu.async_copy(table_hbm.at[idx_v], rows_v, sem)

        def deliver(first, idx_v, rows_v, sem):
            pltpu.make_async_copy(table_hbm.at[idx_v], rows_v, sem).wait()
            pltpu.sync_copy(rows_v, out_hbm.at[pl.ds(first, win)])

        request(pl.multiple_of(start, win), idx_a, rows_a, sem_a)

        @pl.loop(0, n_pairs)
        def _(pair):
            first_a = pl.multiple_of(start + pair * 2 * win, win)
            first_b = pl.multiple_of(first_a + win, win)
            request(first_b, idx_b, rows_b, sem_b)
            deliver(first_a, idx_a, rows_a, sem_a)

            @pl.when(pair + 1 < n_pairs)
            def _():
                request(pl.multiple_of(first_b + win, win), idx_a, rows_a, sem_a)

            deliver(first_b, idx_b, rows_b, sem_b)

    return gather(table, idx)


def _combine_kernel(rows_ref, topw_ref, h_ref, x1_ref, g2_ref, sg_ref, su_ref, sd_ref, ln2g_ref, ln2b_ref,
                    o_ref, *, alpha):
    h = _unpack_words(h_ref[...]).astype(BF16)
    act = _silu(_dot(h, sg_ref[...])) * _dot(h, su_ref[...])
    y = _dot(act.astype(BF16), sd_ref[...])
    w = topw_ref[...].T
    for k in range(TOP_K):
        y = y + w[:, k:k + 1] * _unpack_words(rows_ref[k])
    o_ref[...] = _normalize(alpha * x1_ref[...] + g2_ref[...] * y) * ln2g_ref[...] + ln2b_ref[...]


def _combine(gathered, topw, h2p, x1, g2, sg, su, sd, ln2_g, ln2_b, alpha):
    t, d = x1.shape
    b, k, s = topw.shape
    per_b = s // MOE_TOK
    fs = sg.shape[1]
    rows = pl.BlockSpec((MOE_TOK, d), lambda i: (i, 0))
    packed = pl.BlockSpec((MOE_TOK, d // 2), lambda i: (i, 0))
    vec = pl.BlockSpec((1, d), lambda i: (0, 0))
    return pl.pallas_call(
        functools.partial(_combine_kernel, alpha=alpha),
        grid=(t // MOE_TOK,),
        in_specs=[pl.BlockSpec((k, MOE_TOK, d // 2), lambda i: (0, i, 0)),
                  pl.BlockSpec((None, k, MOE_TOK), lambda i: (i // per_b, 0, i % per_b)),
                  packed, rows,
                  pl.BlockSpec((None, 1, d), lambda i: (i // per_b, 0, 0)),
                  pl.BlockSpec((d, fs), lambda i: (0, 0)),
                  pl.BlockSpec((d, fs), lambda i: (0, 0)),
                  pl.BlockSpec((fs, d), lambda i: (0, 0)),
                  vec, vec],
        out_specs=rows,
        out_shape=jax.ShapeDtypeStruct((t, d), F32),
        compiler_params=_cparams(("arbitrary",)),
        name="combine",
    )(gathered, topw, h2p, x1, g2, sg, su, sd, ln2_g.reshape(1, d), ln2_b.reshape(1, d))


def _moe(h2p, topi, topw, cnt, x1, g2, wg, wu, wd, sg, su, sd, ln2_g, ln2_b, alpha):
    b, s, d = x1.shape
    t = b * s
    cnt = cnt[:, 0].astype(jnp.int32)
    tiles_e = (cnt + (MOE_TILE - 1)) // MOE_TILE
    tiles_cum = jnp.cumsum(tiles_e)
    seg_off = (tiles_cum - tiles_e) * MOE_TILE
    n_tiles_max = t * TOP_K // MOE_TILE + N_EXPERTS
    tile_block = jnp.minimum(jnp.arange(n_tiles_max, dtype=jnp.int32), tiles_cum[-1] - 1)
    tile_expert = jnp.sum((tiles_cum[None, :] <= tile_block[:, None]).astype(jnp.int32), axis=1)
    n_tiles = tiles_cum[-1:].astype(jnp.int32)
    present = tiles_e > 0
    seg_of_expert = jnp.cumsum(present.astype(jnp.int32)) - 1
    later = jnp.where(present, jnp.arange(N_EXPERTS, dtype=jnp.int32), N_EXPERTS)
    next_present = jnp.concatenate([lax.cummin(later, reverse=True)[1:], jnp.full((1,), N_EXPERTS, jnp.int32)])
    next_present = jnp.where(next_present < N_EXPERTS, next_present, -1)
    tile_segment = seg_of_expert[tile_expert].astype(jnp.int32)
    next_expert = next_present[tile_expert].astype(jnp.int32)

    dest = jnp.transpose(_plan(topi, seg_off), (1, 0, 2)).reshape(TOP_K * t)
    j = jnp.arange(MOE_TILE, dtype=jnp.int32)[None, :]
    n_pad = (tiles_e * MOE_TILE - cnt)[:, None]
    spare = (n_tiles_max - 1) * MOE_TILE + j
    zero_idx = jnp.where(j < n_pad, (seg_off + cnt)[:, None] + j, spare).reshape(N_EXPERTS * MOE_TILE)
    xs = _scatter_rows(h2p, dest, zero_idx.astype(jnp.int32), n_tiles_max * MOE_TILE)
    ys = _experts(xs, tile_expert, tile_block, n_tiles, tile_segment, next_expert, wg, wu, wd)
    gathered = _gather_rows(ys, dest)
    out = _combine(gathered.reshape(TOP_K, t, d // 2), topw, h2p, x1.reshape(t, d), g2, sg, su, sd,
                   ln2_g, ln2_b, alpha)
    return out.reshape(b, s, d)


def kernel(x, c, ctx, c_ctx, w_ada, b_ada, w_in, hg_lb_fwd, hg_lb_bwd, hg_norm_g, na_rpb, w_branch_a, w_branch_b, w_out, ln1_g, ln1_b, w_router, router_bias, w_e_gate, w_e_up, w_e_down, w_sh_gate, w_sh_up, w_sh_down, ln2_g, ln2_b):
    depth = w_ada.shape[0]
    assert depth == 1, "single-layer block"
    b, s, d = x.shape
    alpha = (2.0 * depth) ** 0.25
    l = 0
    lb_fwd = jnp.cumsum(jax.nn.softmax(hg_lb_fwd.astype(F32), axis=0), axis=0)[l]
    lb_bwd = jnp.cumsum(jax.nn.softmax(hg_lb_bwd.astype(F32), axis=0), axis=0)[l]

    cond_rows = jnp.concatenate([c, c_ctx[None, :], jnp.zeros((8 - b - 1, d), F32)], axis=0)
    mod = _ada(cond_rows, w_ada[l], b_ada[l])
    sh1, sc1, g1, sh2, sc2, g2 = [m[:b, None, :] for m in jnp.split(mod, 6, axis=-1)]
    csh1, csc1 = [jnp.broadcast_to(m[b:b + 1, None, :], (b, 1, d)) for m in jnp.split(mod, 6, axis=-1)[:2]]

    w_in_b = w_in[l].astype(BF16)
    p = _inproj(x, sh1, sc1, w_in_b, tuple(range(N_SECTIONS)))
    pc = _inproj(ctx, csh1, csc1, w_in_b, CTX_SECTIONS)

    o_f, o_b = _hgrn(p, pc, lb_fwd, lb_bwd)
    y_na = _natten(p, pc, *_na_tables(na_rpb[l], s))

    x1, h2, topi, topw, cnt = _merge(o_f, o_b, p, y_na, x, g1, sh2, sc2, hg_norm_g[l], ln1_g[l], ln1_b[l],
                                     w_branch_a[l].astype(BF16), w_branch_b[l].astype(BF16),
                                     w_out[l].astype(BF16), w_router[l].T, router_bias[l], alpha)

    return _moe(h2, topi, topw, cnt, x1, g2,
                w_e_gate[l], w_e_up[l], w_e_down[l],
                w_sh_gate[l].astype(BF16), w_sh_up[l].astype(BF16), w_sh_down[l].astype(BF16),
                ln2_g[l], ln2_b[l], alpha)
```

```python
import functools

import numpy as np
import jax
import jax.numpy as jnp
from jax import lax
from jax.experimental import pallas as pl
from jax.experimental.pallas import tpu as pltpu
from jax.experimental.pallas import tpu_sc as plsc

F32 = jnp.float32
BF16 = jnp.bfloat16

D_MODEL = 1024
GRID_W = 64
HG_HEADS = 8
HG_DK = 128
HG_CHUNK = 64
NA_HEADS = 16
NA_HD = 64
NA_WIN_R = 8
NA_WIN_C = 16
ROPE_THETA = 10000.0
NEG_INF = -1e30
N_EXPERTS = 64
EXPERT_DIM = 256
TOP_K = 8
N_GROUPS = 8
TOPK_GROUPS = 4
ROUTED_SCALE = 2.5
LN_EPS = 1e-6
N_SECTIONS = 10
SEC_Q, SEC_FF, SEC_FB, SEC_I, SEC_OG, SEC_NQ, SEC_NK, SEC_NV, SEC_GA, SEC_GB = range(10)
CTX_SECTIONS = (SEC_FF, SEC_FB, SEC_I, SEC_NK, SEC_NV)

VMEM_LIMIT = 56 * 1024 * 1024


def _cparams(sem):
    return pltpu.CompilerParams(dimension_semantics=sem, vmem_limit_bytes=VMEM_LIMIT)


def _normalize(x):
    mu = jnp.mean(x, axis=-1, keepdims=True)
    xc = x - mu
    var = jnp.mean(xc * xc, axis=-1, keepdims=True)
    return xc * lax.rsqrt(var + LN_EPS)


def _silu(x):
    return x * jax.nn.sigmoid(x)


def _dot(a, b):
    return jnp.dot(a, b, preferred_element_type=F32)


def _dot_nt(a, b):
    return lax.dot_general(a, b, (((1,), (1,)), ((), ())), preferred_element_type=F32)


def _dot_tn(a, b):
    return lax.dot_general(a, b, (((0,), (0,)), ((), ())), preferred_element_type=F32)


U32 = jnp.uint32


def _pack_words(x):
    half = x.shape[1] // 2
    lo = lax.bitcast_convert_type(x[:, :half].astype(BF16).astype(F32), U32) >> 16
    hi = lax.bitcast_convert_type(x[:, half:].astype(BF16).astype(F32), U32) & jnp.uint32(0xFFFF0000)
    return lo | hi


def _unpack_words(w):
    lo = lax.bitcast_convert_type(w << 16, F32)
    hi = lax.bitcast_convert_type(w & jnp.uint32(0xFFFF0000), F32)
    return jnp.concatenate([lo, hi], axis=-1)


def _split3(x):
    hi = x.astype(BF16)
    r1 = x - hi.astype(F32)
    mid = r1.astype(BF16)
    lo = (r1 - mid.astype(F32)).astype(BF16)
    return hi, mid, lo


def _ada_kernel(c_ref, w_ref, b_ref, o_ref):
    cond = _silu(c_ref[...])
    o_ref[...] = _dot(cond.astype(BF16), w_ref[...].astype(BF16)) + b_ref[...]


def _ada(cond_rows, w_ada, b_ada):
    r, d = cond_rows.shape
    n = w_ada.shape[1]
    tn = 1024
    return pl.pallas_call(
        _ada_kernel,
        grid=(n // tn,),
        in_specs=[pl.BlockSpec((r, d), lambda j: (0, 0)),
                  pl.BlockSpec((d, tn), lambda j: (0, j)),
                  pl.BlockSpec((1, tn), lambda j: (0, j))],
        out_specs=pl.BlockSpec((r, tn), lambda j: (0, j)),
        out_shape=jax.ShapeDtypeStruct((r, n), F32),
        compiler_params=_cparams(("arbitrary",)),
        name="ada",
    )(cond_rows, w_ada, b_ada.reshape(1, n))


INPROJ_TOK = 2048


def _inproj_kernel(x_ref, sh_ref, sc_ref, w_ref, o_ref, h_ref):
    @pl.when(pl.program_id(2) == 0)
    def _():
        h = _normalize(x_ref[...]) * (1.0 + sc_ref[...]) + sh_ref[...]
        h_ref[...] = h.astype(BF16)

    o_ref[...] = _dot(h_ref[...], w_ref[...])


def _inproj(x, shift, scale, w_in_bf16, sections):
    b, s, d = x.shape
    tm = min(INPROJ_TOK, s)
    nj = len(sections)

    def section(j):
        sec = sections[-1]
        for k in range(nj - 2, -1, -1):
            sec = jnp.where(j == k, sections[k], sec)
        return sec

    return pl.pallas_call(
        _inproj_kernel,
        grid=(b, s // tm, nj),
        in_specs=[pl.BlockSpec((None, tm, d), lambda bi, i, j: (bi, i, 0)),
                  pl.BlockSpec((None, 1, d), lambda bi, i, j: (bi, 0, 0)),
                  pl.BlockSpec((None, 1, d), lambda bi, i, j: (bi, 0, 0)),
                  pl.BlockSpec((d, d), lambda bi, i, j: (0, section(j)))],
        out_specs=pl.BlockSpec((None, None, tm, d), lambda bi, i, j: (j, bi, i, 0)),
        out_shape=jax.ShapeDtypeStruct((nj, b, s, d), F32),
        scratch_shapes=[pltpu.VMEM((tm, d), BF16)],
        compiler_params=_cparams(("arbitrary", "arbitrary", "arbitrary")),
        name="inproj",
    )(x, shift, scale, w_in_bf16)


def _hgrn_gates(q, fraw, v, lb, tri_bf16, last_row):
    f = lb + (1.0 - lb) * jax.nn.sigmoid(fraw)
    k = 1.0 - f
    lf = jnp.log(f)
    hi, mid, lo = _split3(lf)
    a = _dot(tri_bf16, hi) + _dot(tri_bf16, mid) + _dot(tri_bf16, lo)
    a_last = a[last_row:last_row + 1, :]
    kd = (k * jnp.exp(a_last - a)).astype(BF16)
    decay = jnp.exp(a_last)
    qa = kb = None
    if q is not None:
        qa = (_silu(q) * jnp.exp(a)).astype(BF16)
        kb = (k * jnp.exp(-a)).astype(BF16)
    return qa, kb, kd, v.astype(BF16), decay


def _hgrn_chunks(chunks, st_ref):
    first = []
    for d, ((qa, kb, kd, vb, decay), keep) in enumerate(chunks):
        for h in range(HG_HEADS):
            sl = slice(h * HG_DK, (h + 1) * HG_DK)
            st = st_ref[d, h]
            if qa is not None:
                first.append((_dot_nt(qa[:, sl], kb[:, sl]), _dot_nt(qa[:, sl], st.astype(BF16))))
            st_ref[d, h] = st * decay[:, sl] + _dot_tn(vb[:, sl], kd[:, sl])
    results = []
    for d, ((qa, kb, kd, vb, decay), keep) in enumerate(chunks):
        if qa is None:
            results.append(None)
            continue
        outs = []
        for h in range(HG_HEADS):
            sl = slice(h * HG_DK, (h + 1) * HG_DK)
            s_qk, o_state = first.pop(0)
            outs.append(_dot(jnp.where(keep, s_qk, 0.0).astype(BF16), vb[:, sl]) + o_state)
        results.append(jnp.concatenate(outs, axis=-1))
    return results


def _hgrn_kernel(qf_ref, ff_ref, if_ref, qb_ref, fb_ref, ib_ref, cff_ref, cfb_ref, ci_ref,
                 lbf_ref, lbb_ref, of_ref, ob_ref, st_ref, *, n_sub, n_ctx_sub):
    n = pl.program_id(1)
    c = HG_CHUNK
    row = lax.broadcasted_iota(jnp.int32, (c, c), 0)
    col = lax.broadcasted_iota(jnp.int32, (c, c), 1)
    keep_f = col <= row
    keep_b = col >= row
    tri_f = keep_f.astype(F32).astype(BF16)
    tri_b = keep_b.astype(F32).astype(BF16)
    lbf = lbf_ref[...]
    lbb = lbb_ref[...]

    @pl.when(n == 0)
    def _():
        st_ref[...] = jnp.zeros_like(st_ref)

        def body(i, carry):
            r0 = pl.multiple_of(i * c, c)
            r1 = pl.multiple_of((n_ctx_sub - 1 - i) * c, c)
            gf = _hgrn_gates(None, cff_ref[pl.ds(r0, c), :], ci_ref[pl.ds(r0, c), :], lbf, tri_f, c - 1)
            gb = _hgrn_gates(None, cfb_ref[pl.ds(r1, c), :], ci_ref[pl.ds(r1, c), :], lbb, tri_b, 0)
            _hgrn_chunks([(gf, keep_f), (gb, keep_b)], st_ref)
            return carry

        lax.fori_loop(0, n_ctx_sub, body, 0)

    @pl.when(n > 0)
    def _():
        def body(i, carry):
            r0 = pl.multiple_of(i * c, c)
            r1 = pl.multiple_of((n_sub - 1 - i) * c, c)
            gf = _hgrn_gates(qf_ref[pl.ds(r0, c), :], ff_ref[pl.ds(r0, c), :], if_ref[pl.ds(r0, c), :],
                             lbf, tri_f, c - 1)
            gb = _hgrn_gates(qb_ref[pl.ds(r1, c), :], fb_ref[pl.ds(r1, c), :], ib_ref[pl.ds(r1, c), :],
                             lbb, tri_b, 0)
            o_f, o_b = _hgrn_chunks([(gf, keep_f), (gb, keep_b)], st_ref)
            of_ref[pl.ds(r0, c), :] = o_f
            ob_ref[pl.ds(r1, c), :] = o_b
            return carry

        lax.fori_loop(0, n_sub, body, 0, unroll=True)


def _hgrn(p, pc, lb_fwd, lb_bwd):
    _, b, s, w = p.shape
    ctx_len = pc.shape[2]
    tb = min(512, s)
    nb = s // tb
    fwd = lambda bi, n: jnp.maximum(n - 1, 0)
    bwd = lambda bi, n: nb - 1 - jnp.maximum(n - 1, 0)

    def sec(section, blk):
        return pl.BlockSpec((None, None, tb, w), lambda bi, n: (section, bi, blk(bi, n), 0))

    def csec(section):
        return pl.BlockSpec((None, None, ctx_len, w), lambda bi, n: (CTX_SECTIONS.index(section), bi, 0, 0))

    vec = pl.BlockSpec((1, w), lambda bi, n: (0, 0))
    kern = functools.partial(_hgrn_kernel, n_sub=tb // HG_CHUNK, n_ctx_sub=ctx_len // HG_CHUNK)
    return pl.pallas_call(
        kern,
        grid=(b, nb + 1),
        in_specs=[sec(SEC_Q, fwd), sec(SEC_FF, fwd), sec(SEC_I, fwd),
                  sec(SEC_Q, bwd), sec(SEC_FB, bwd), sec(SEC_I, bwd),
                  csec(SEC_FF), csec(SEC_FB), csec(SEC_I), vec, vec],
        out_specs=[pl.BlockSpec((None, tb, w), lambda bi, n: (bi, fwd(bi, n), 0)),
                   pl.BlockSpec((None, tb, w), lambda bi, n: (bi, bwd(bi, n), 0))],
        out_shape=[jax.ShapeDtypeStruct((b, s, w), F32), jax.ShapeDtypeStruct((b, s, w), F32)],
        scratch_shapes=[pltpu.VMEM((2, HG_HEADS, HG_DK, HG_DK), F32)],
        compiler_params=_cparams(("arbitrary", "arbitrary")),
        name="hgrn",
    )(p, p, p, p, p, p, pc, pc, pc, lb_fwd.reshape(1, w), lb_bwd.reshape(1, w))


NA_ROWS_PER_STEP = 64
NA_PREP_ROWS = 512
NA_KEY_TILE = 128
NA_SPAN = NA_WIN_R * GRID_W


def _rope(t, cos, sin_signed, first_half):
    w = t.shape[-1]
    partner = jnp.where(first_half, pltpu.roll(t, w - 16, 1), pltpu.roll(t, 16, 1))
    return t * cos + partner * sin_signed


def _fold_lanes(op, *arrays):
    tiles = [a[:, c:c + 128] for a in arrays for c in range(0, a.shape[-1], 128)]
    acc = tiles[0]
    for t in tiles[1:]:
        acc = op(acc, t)
    return acc


def _rope_tables(rowtab_ref, coltab_ref, row0, n_rows, row_lane):
    out = []
    for i in range(2):
        rt = rowtab_ref[i, pl.ds(row0, n_rows), :]
        by_row = jnp.concatenate([jnp.broadcast_to(rt[r:r + 1, :], (GRID_W, rt.shape[1])) for r in range(n_rows)],
                                 axis=0)
        by_col = jnp.concatenate([coltab_ref[i]] * n_rows, axis=0)
        out.append(jnp.where(row_lane, by_row, by_col))
    return out


def _natten_kernel(q_ref, k_ref, v_ref, kc_ref, vc_ref, rowtab_ref, coltab_ref, rp_ref, o_ref,
                   kt_s, v_s, kc_s, vc_s, bias_s, tail_s, *, rows):
    rblk = pl.program_id(2)
    hd = NA_HD
    lane = lax.broadcasted_iota(jnp.int32, (1, 2 * hd), 1)
    first_half = (lane % 32) < 16
    row_lane = (lane % hd) < hd // 2
    scale = NA_HD ** -0.5

    def values_and_ones(v_pair, h):
        vh = v_pair if h == 0 else pltpu.roll(v_pair, hd, 1)
        return jnp.where(lane < hd, vh, jnp.where(lane == hd, 1.0, 0.0)).astype(BF16)

    @pl.when(rblk == 0)
    def _():
        kc = kc_ref[...].astype(BF16)
        qi = lax.broadcasted_iota(jnp.int32, (GRID_W, GRID_W), 0)
        ki = lax.broadcasted_iota(jnp.int32, (GRID_W, GRID_W), 1)
        cstart = jnp.clip(qi - NA_WIN_C // 2, 0, GRID_W - NA_WIN_C)
        in_win = (ki >= cstart) & (ki < cstart + NA_WIN_C)
        s_len = k_ref.shape[0]
        tail_s[...] = jnp.zeros_like(tail_s)
        for h in range(2):
            sl = slice(h * hd, (h + 1) * hd)
            kc_s[h] = kc[:, sl]
            vc_s[h] = values_and_ones(vc_ref[...], h)
            tiles = []
            for dr in range(2 * NA_WIN_R - 1):
                table = jnp.broadcast_to(rp_ref[h, dr:dr + 1, :], (GRID_W, 2 * GRID_W))
                skew = pltpu.roll(table, GRID_W + 1, 1, stride=1, stride_axis=0)
                tiles.append(jnp.where(in_win, skew[:, :GRID_W], NEG_INF))
            for v in range(NA_WIN_R):
                for j in range(NA_WIN_R):
                    bias_s[h, v, :, j * GRID_W:(j + 1) * GRID_W] = tiles[NA_WIN_R - 1 - v + j]

        eye = (lax.broadcasted_iota(jnp.int32, (2 * hd, 2 * hd), 0)
               == lax.broadcasted_iota(jnp.int32, (2 * hd, 2 * hd), 1)).astype(F32).astype(BF16)

        def prep(i, carry):
            r0 = pl.multiple_of(i * NA_PREP_ROWS, NA_PREP_ROWS)
            rws = pl.ds(r0, NA_PREP_ROWS)
            cos, sin = _rope_tables(rowtab_ref, coltab_ref, i * (NA_PREP_ROWS // GRID_W), NA_PREP_ROWS // GRID_W,
                                    row_lane)
            kr = _rope(k_ref[rws, :], cos, sin, first_half)
            kr_odd = jnp.concatenate([tail_s[...], kr[:NA_PREP_ROWS - GRID_W]], axis=0)
            tail_s[...] = kr[NA_PREP_ROWS - GRID_W:]
            krt = [_dot_nt(eye, kr.astype(BF16)).astype(BF16),
                   _dot_nt(eye, kr_odd.astype(BF16)).astype(BF16)]
            vv = v_ref[rws, :]
            for h in range(2):
                sl = slice(h * hd, (h + 1) * hd)
                for par in range(2):
                    for c in range(NA_PREP_ROWS // NA_KEY_TILE):
                        kt_s[h, par, i * (NA_PREP_ROWS // NA_KEY_TILE) + c] = (
                            krt[par][sl, c * NA_KEY_TILE:(c + 1) * NA_KEY_TILE])
                v_s[h, rws, :] = values_and_ones(vv, h)
            return carry

        lax.fori_loop(0, s_len // NA_PREP_ROWS, prep, 0, unroll=4)

    tq = NA_ROWS_PER_STEP * GRID_W
    q = q_ref[...] * scale
    cos, sin = _rope_tables(rowtab_ref, coltab_ref, rblk * NA_ROWS_PER_STEP, NA_ROWS_PER_STEP, row_lane)
    qr = _rope(q, cos, sin, first_half)
    qb = q.astype(BF16)
    qrb = qr.astype(BF16)
    rws = [slice(rr * GRID_W, (rr + 1) * GRID_W) for rr in range(NA_ROWS_PER_STEP)]
    par, slot0, key0, bidx = [], [], [], []
    for rr in range(NA_ROWS_PER_STEP):
        r = rblk * NA_ROWS_PER_STEP + rr
        rs = jnp.clip(r - NA_WIN_R // 2, 0, rows - NA_WIN_R)
        par.append(rs & 1)
        slot0.append(lax.shift_right_logical(rs, 1) + (rs & 1))
        key0.append(pl.multiple_of(rs * GRID_W, GRID_W))
        bidx.append(r - rs)

    def scores(h):
        sl = slice(h * hd, (h + 1) * hd)
        qrb_h = qrb[:, sl]
        s_ctx_all = _dot_nt(qb[:, sl], kc_s[h])
        s_win = []
        for rr in range(NA_ROWS_PER_STEP):
            kt = kt_s[h, par[rr], pl.ds(slot0[rr], NA_SPAN // NA_KEY_TILE)]
            kt = jnp.concatenate([kt[c] for c in range(NA_SPAN // NA_KEY_TILE)], axis=-1)
            s_win.append(_dot(qrb_h[rws[rr]], kt))
        return s_win, s_ctx_all

    def softmax(h, s_win, s_ctx_all):
        e_win, e_ctx = [], []
        for rr in range(NA_ROWS_PER_STEP):
            sw = s_win[rr] + bias_s[h, bidx[rr]]
            sc = s_ctx_all[rws[rr]]
            m = jnp.max(_fold_lanes(jnp.maximum, sw, sc), axis=-1, keepdims=True)
            e_win.append(jnp.exp(sw - m).astype(BF16))
            e_ctx.append(jnp.exp(sc - m).astype(BF16))
        return e_win, e_ctx

    def values(h, e_win, e_ctx):
        o_win = []
        for rr in range(NA_ROWS_PER_STEP):
            o_win.append(_dot(e_win[rr], v_s[h, pl.ds(key0[rr], NA_SPAN), :]))
        o = jnp.concatenate(o_win, axis=0) + _dot(jnp.concatenate(e_ctx, axis=0), vc_s[h])
        return o[:, :hd] * (1.0 / o[:, hd:hd + 1])

    s0 = scores(0)
    s1 = scores(1)
    p0 = softmax(0, *s0)
    o0 = values(0, *p0)
    p1 = softmax(1, *s1)
    o1 = values(1, *p1)
    o_ref[...] = jnp.concatenate([o0, o1], axis=-1).astype(o_ref.dtype)


def _na_tables(rpb, s):
    half = NA_HD // 2
    inv = jnp.power(ROPE_THETA, -jnp.arange(0, half, 2, dtype=F32) / half)

    def tables(n):
        ang = jnp.arange(n, dtype=F32)[:, None] * inv[None, :]
        reps = 2 * NA_HD // half
        return jnp.stack([jnp.tile(jnp.cos(ang), (1, 2 * reps)),
                          jnp.tile(jnp.concatenate([-jnp.sin(ang), jnp.sin(ang)], axis=-1), (1, reps))])

    rowtab, coltab = tables(s // GRID_W), tables(GRID_W)

    pad = GRID_W - NA_WIN_C
    rp = jnp.pad(rpb.astype(F32), ((0, 0), (0, 0), (pad, pad + 1)), mode="edge")
    return rowtab, coltab, rp


def _natten(p, pc, rowtab, coltab, rp):
    _, b, s, w = p.shape
    ctx_len = pc.shape[2]
    rows = s // GRID_W
    assert rows >= NA_WIN_R and rows % NA_ROWS_PER_STEP == 0
    tq = NA_ROWS_PER_STEP * GRID_W
    hw = 2 * NA_HD
    nhp = w // hw
    kern = functools.partial(_natten_kernel, rows=rows)
    return pl.pallas_call(
        kern,
        grid=(b, nhp, rows // NA_ROWS_PER_STEP),
        in_specs=[pl.BlockSpec((None, None, tq, hw), lambda bi, hp, r: (SEC_NQ, bi, r, hp)),
                  pl.BlockSpec((None, None, s, hw), lambda bi, hp, r: (SEC_NK, bi, 0, hp)),
                  pl.BlockSpec((None, None, s, hw), lambda bi, hp, r: (SEC_NV, bi, 0, hp)),
                  pl.BlockSpec((None, None, ctx_len, hw), lambda bi, hp, r: (CTX_SECTIONS.index(SEC_NK), bi, 0, hp)),
                  pl.BlockSpec((None, None, ctx_len, hw), lambda bi, hp, r: (CTX_SECTIONS.index(SEC_NV), bi, 0, hp)),
                  pl.BlockSpec((2, rows, hw), lambda bi, hp, r: (0, 0, 0)),
                  pl.BlockSpec((2, GRID_W, hw), lambda bi, hp, r: (0, 0, 0)),
                  pl.BlockSpec((2, 2 * NA_WIN_R - 1, 2 * GRID_W), lambda bi, hp, r: (hp, 0, 0))],
        out_specs=pl.BlockSpec((None, tq, hw), lambda bi, hp, r: (bi, r, hp)),
        out_shape=jax.ShapeDtypeStruct((b, s, w), BF16),
        scratch_shapes=[pltpu.VMEM((2, 2, s // NA_KEY_TILE, NA_HD, NA_KEY_TILE), BF16),
                        pltpu.VMEM((2, s, hw), BF16),
                        pltpu.VMEM((2, ctx_len, NA_HD), BF16), pltpu.VMEM((2, ctx_len, hw), BF16),
                        pltpu.VMEM((2, NA_WIN_R, GRID_W, NA_SPAN), F32),
                        pltpu.VMEM((GRID_W, hw), F32)],
        compiler_params=_cparams(("arbitrary", "arbitrary", "arbitrary")),
        name="natten",
    )(p, p, p, pc, pc, rowtab, coltab, rp)


def _route(logits_t, rbias):
    e, t = logits_t.shape
    gsz = e // N_GROUPS
    scores = jax.nn.sigmoid(logits_t)
    sel = scores + rbias
    neg = -jnp.inf
    sub = lax.broadcasted_iota(jnp.int32, (gsz, t), 0).astype(F32)
    gscore = []
    for g in range(N_GROUPS):
        grp = sel[g * gsz:(g + 1) * gsz, :]
        m1 = jnp.max(grp, axis=0, keepdims=True)
        first = jnp.min(jnp.where(grp == m1, sub, float(gsz)), axis=0, keepdims=True)
        m2 = jnp.max(jnp.where(sub == first, neg, grp), axis=0, keepdims=True)
        gscore.append(m1 + m2)
    masked = []
    for g in range(N_GROUPS):
        rank = jnp.zeros((1, t), F32)
        for g2 in range(N_GROUPS):
            if g2 == g:
                continue
            if g2 < g:
                ahead = gscore[g2] >= gscore[g]
            else:
                ahead = gscore[g2] > gscore[g]
            rank = rank + jnp.where(ahead, 1.0, 0.0)
        masked.append(jnp.where(rank < TOPK_GROUPS, sel[g * gsz:(g + 1) * gsz, :], neg))
    work = jnp.concatenate(masked, axis=0)
    eidx = lax.broadcasted_iota(jnp.int32, (e, t), 0).astype(F32)
    idxs, ws = [], []
    chosen = jnp.zeros((e, t), F32)
    for _ in range(TOP_K):
        m = jnp.max(work, axis=0, keepdims=True)
        first = jnp.min(jnp.where(work == m, eidx, float(e)), axis=0, keepdims=True)
        pick = eidx == first
        idxs.append(first)
        ws.append(jnp.sum(jnp.where(pick, scores, 0.0), axis=0, keepdims=True))
        chosen = jnp.where(pick, 1.0, chosen)
        work = jnp.where(pick, neg, work)
    w = jnp.concatenate(ws, axis=0)
    w = w / jnp.sum(w, axis=0, keepdims=True) * ROUTED_SCALE
    return jnp.concatenate(idxs, axis=0).astype(jnp.int32), w, chosen


MERGE_TOK = 512
MERGE_SUB = 256


def _merge_kernel(of_ref, ob_ref, og_ref, yna_ref, ga_ref, gb_ref, x_ref, g1_ref, sh2_ref, sc2_ref,
                  hgg_ref, ln1g_ref, ln1b_ref, wa_ref, wb_ref, wo_ref, wr_ref, rb_ref,
                  x1_ref, h2_ref, topi_ref, topw_ref, cnt_ref, *, alpha):
    tm = x_ref.shape[0]
    subs = [slice(i * MERGE_SUB, (i + 1) * MERGE_SUB) for i in range(tm // MERGE_SUB)]

    def branches(rows):
        o = of_ref[rows, :] + ob_ref[rows, :]
        parts = []
        for h in range(HG_HEADS):
            oh = o[:, h * HG_DK:(h + 1) * HG_DK]
            parts.append(oh * lax.rsqrt(jnp.mean(oh * oh, axis=-1, keepdims=True) + LN_EPS))
        y_hg = jnp.concatenate(parts, axis=-1) * hgg_ref[...] * _silu(og_ref[rows, :])
        return _dot(y_hg.astype(BF16), wa_ref[...]), _dot(yna_ref[rows, :], wb_ref[...])

    def out_proj(rows, ya, yb):
        t = jax.nn.sigmoid(ga_ref[rows, :]) * ya + jax.nn.sigmoid(gb_ref[rows, :]) * yb
        return _dot(t.astype(BF16), wo_ref[...])

    def norms_router(rows, i, y):
        x1 = _normalize(alpha * x_ref[rows, :] + g1_ref[...] * y) * ln1g_ref[...] + ln1b_ref[...]
        x1_ref[rows, :] = x1
        h2 = _normalize(x1) * (1.0 + sc2_ref[...]) + sh2_ref[...]
        h2_ref[rows, :] = _pack_words(h2)
        hh, hm, hl = _split3(h2)
        wh, wm, wl = _split3(wr_ref[...])
        return (_dot_nt(wh, hh) + _dot_nt(wh, hm) + _dot_nt(wm, hh)
                + _dot_nt(wh, hl) + _dot_nt(wl, hh) + _dot_nt(wm, hm))

    ab = [branches(rows) for rows in subs]
    ys = [out_proj(rows, *ab[i]) for i, rows in enumerate(subs)]
    logits = [norms_router(rows, i, ys[i]) for i, rows in enumerate(subs)]

    @pl.when((pl.program_id(0) == 0) & (pl.program_id(1) == 0))
    def _():
        cnt_ref[...] = jnp.zeros_like(cnt_ref)

    for i, rows in enumerate(subs):
        topi, topw, chosen = _route(logits[i], rb_ref[...])
        topi_ref[:, rows] = topi
        topw_ref[:, rows] = topw
        cnt_ref[...] += jnp.sum(chosen, axis=1, keepdims=True)


def _merge(o_f, o_b, p, y_na, x, g1, sh2, sc2, hg_norm_g, ln1_g, ln1_b, w_a, w_b, w_o, w_router_t, router_bias,
           alpha):
    b, s, d = x.shape
    tm = min(MERGE_TOK, s)
    e = w_router_t.shape[0]
    tok = lambda bi, i: (bi, i, 0)
    blk = pl.BlockSpec((None, tm, d), tok)

    def sec(section):
        return pl.BlockSpec((None, None, tm, d), lambda bi, i: (section, bi, i, 0))

    mod = pl.BlockSpec((None, 1, d), lambda bi, i: (bi, 0, 0))
    vec = pl.BlockSpec((1, d), lambda bi, i: (0, 0))
    mat = pl.BlockSpec((d, d), lambda bi, i: (0, 0), pipeline_mode=pl.Buffered(1))
    return pl.pallas_call(
        functools.partial(_merge_kernel, alpha=alpha),
        grid=(b, s // tm),
        in_specs=[blk, blk, sec(SEC_OG), blk, sec(SEC_GA), sec(SEC_GB), blk, mod, mod, mod,
                  vec, vec, vec, mat, mat, mat,
                  pl.BlockSpec((e, d), lambda bi, i: (0, 0)),
                  pl.BlockSpec((e, 1), lambda bi, i: (0, 0))],
        out_specs=[blk,
                   pl.BlockSpec((tm, d // 2), lambda bi, i: (bi * (s // tm) + i, 0)),
                   pl.BlockSpec((None, TOP_K, tm), lambda bi, i: (bi, 0, i)),
                   pl.BlockSpec((None, TOP_K, tm), lambda bi, i: (bi, 0, i)),
                   pl.BlockSpec((e, 128), lambda bi, i: (0, 0))],
        out_shape=[jax.ShapeDtypeStruct((b, s, d), F32),
                   jax.ShapeDtypeStruct((b * s, d // 2), U32),
                   jax.ShapeDtypeStruct((b, TOP_K, s), jnp.int32), jax.ShapeDtypeStruct((b, TOP_K, s), F32),
                   jax.ShapeDtypeStruct((e, 128), F32)],
        compiler_params=_cparams(("arbitrary", "arbitrary")),
        name="merge",
    )(o_f, o_b, p, y_na, p, p, x, g1, sh2, sc2, hg_norm_g.reshape(1, d), ln1_g.reshape(1, d),
      ln1_b.reshape(1, d), w_a, w_b, w_o, w_router_t, router_bias.reshape(e, 1))


MOE_TILE = 512
MOE_TOK = 512


def _plan_kernel(topi_ref, off_ref, dest_ref, carry_ref):
    @pl.when(pl.program_id(0) == 0)
    def _():
        carry_ref[...] = jnp.zeros_like(carry_ref)

    topi = topi_ref[...]
    tok = topi.shape[1]
    eidx = lax.broadcasted_iota(jnp.int32, (N_EXPERTS, tok), 0)
    hits = [eidx == topi[k:k + 1, :] for k in range(TOP_K)]
    m = jnp.zeros((N_EXPERTS, tok), F32)
    for hit in hits:
        m = jnp.where(hit, 1.0, m)
    before = (lax.broadcasted_iota(jnp.int32, (tok, tok), 0)
              < lax.broadcasted_iota(jnp.int32, (tok, tok), 1)).astype(F32).astype(BF16)
    row = off_ref[...] + carry_ref[...] + _dot(m.astype(BF16), before)
    dest = [jnp.sum(jnp.where(hit, row, 0.0), axis=0, keepdims=True) for hit in hits]
    dest_ref[...] = jnp.concatenate(dest, axis=0).astype(jnp.int32)
    carry_ref[...] += jnp.sum(m, axis=1, keepdims=True)


def _plan(topi, seg_off):
    b, k, s = topi.shape
    per_b = s // MOE_TOK
    blk = pl.BlockSpec((None, k, MOE_TOK), lambda i: (i // per_b, 0, i % per_b))
    return pl.pallas_call(
        _plan_kernel,
        grid=(b * per_b,),
        in_specs=[blk, pl.BlockSpec((N_EXPERTS, 1), lambda i: (0, 0))],
        out_specs=blk,
        out_shape=jax.ShapeDtypeStruct((b, k, s), jnp.int32),
        scratch_shapes=[pltpu.VMEM((N_EXPERTS, 1), F32)],
        compiler_params=_cparams(("arbitrary",)),
        name="plan",
    )(topi, seg_off.astype(F32).reshape(N_EXPERTS, 1))


SC_WINDOW = 128


def _sc_workers():
    info = plsc.get_sparse_core_info()
    return info.num_cores, info.num_cores * info.num_subcores


def _scatter_rows(src, idx, zero_idx, n_rows):
    t, w = src.shape
    m, mz = idx.shape[0], zero_idx.shape[0]
    n_cores, n_workers = _sc_workers()
    per_worker, per_worker_z = t // n_workers, mz // n_workers
    assert m % t == 0
    assert per_worker * n_workers == t and per_worker % SC_WINDOW == 0
    assert per_worker_z * n_workers == mz and per_worker_z % SC_WINDOW == 0
    mesh = plsc.VectorSubcoreMesh(core_axis_name="core", subcore_axis_name="subcore")

    copies = m // t
    zero_batch = per_worker_z // SC_WINDOW

    @functools.partial(
        pl.kernel, mesh=mesh, out_type=jax.ShapeDtypeStruct((n_rows, w), src.dtype),
        scratch_types=[pltpu.VMEM((copies, SC_WINDOW), jnp.int32), pltpu.VMEM((zero_batch, SC_WINDOW), jnp.int32),
                       pltpu.VMEM((SC_WINDOW, w), src.dtype), pltpu.SemaphoreType.DMA])
    def scatter(src_hbm, idx_hbm, zeros_hbm, zero_idx_hbm, out_hbm, idx_v, zidx_v, rows_v, sem):
        worker = lax.axis_index("subcore") * n_cores + lax.axis_index("core")

        def scatter_all(index_rows, n):
            for c in range(n):
                pltpu.async_copy(rows_v, out_hbm.at[index_rows.at[c]], sem)
            for c in range(n):
                pltpu.make_async_copy(rows_v, out_hbm.at[index_rows.at[c]], sem).wait()

        @pl.loop(0, per_worker // SC_WINDOW)
        def _(step):
            first = pl.multiple_of(worker * per_worker + step * SC_WINDOW, SC_WINDOW)
            pltpu.sync_copy(src_hbm.at[pl.ds(first, SC_WINDOW)], rows_v)
            pltpu.sync_copy(idx_hbm.at[:, pl.ds(first, SC_WINDOW)], idx_v)
            scatter_all(idx_v, copies)

        pltpu.sync_copy(zeros_hbm, rows_v)
        pltpu.sync_copy(zero_idx_hbm.at[worker], zidx_v)
        scatter_all(zidx_v, zero_batch)

    return scatter(src, idx.reshape(copies, t), jnp.zeros((SC_WINDOW, w), src.dtype),
                   zero_idx.reshape(n_workers, zero_batch, SC_WINDOW))


EXPERT_RING = 6


def _experts_kernel(te_ref, tb_ref, nt_ref, seg_ref, nxt_ref, xs_ref, wg_ref, wu_ref, wd_ref, ys_ref,
                    xbuf, sem, wg_f32, wu_f32, wd_f32, wg_s, wu_s, wd_s, wsem):
    i = pl.program_id(0)
    n_tiles = nt_ref[0]

    def weight_copies(expert, slot):
        return [pltpu.make_async_copy(src.at[expert], dst.at[slot], wsem.at[slot, n])
                for n, (src, dst) in enumerate(((wg_ref, wg_f32), (wu_ref, wu_f32), (wd_ref, wd_f32)))]

    @pl.when(i == 0)
    def _():
        for c in weight_copies(te_ref[0], 0):
            c.start()

    first = (i == 0) | (seg_ref[i] != seg_ref[jnp.maximum(i - 1, 0)])

    @pl.when((i < n_tiles) & first)
    def _():
        slot = seg_ref[i] & 1
        for c in weight_copies(te_ref[i], slot):
            c.wait()
        wg_s[...] = wg_f32[slot].astype(BF16)
        wu_s[...] = wu_f32[slot].astype(BF16)
        wd_s[...] = wd_f32[slot].astype(BF16)

        @pl.when(nxt_ref[i] >= 0)
        def _():
            for c in weight_copies(nxt_ref[i], 1 - slot):
                c.start()

    def tile_copy(j):
        slot = lax.rem(j, EXPERT_RING)
        row0 = pl.multiple_of(tb_ref[j] * MOE_TILE, MOE_TILE)
        return pltpu.make_async_copy(xs_ref.at[pl.ds(row0, MOE_TILE), :], xbuf.at[slot], sem.at[slot])

    @pl.when(i == 0)
    def _():
        for j in range(EXPERT_RING - 1):
            @pl.when(j < n_tiles)
            def _():
                tile_copy(j).start()

    ahead = i + (EXPERT_RING - 1)

    @pl.when(ahead < n_tiles)
    def _():
        tile_copy(ahead).start()

    @pl.when(i < n_tiles)
    def _():
        tile_copy(i).wait()
        x = _unpack_words(xbuf[lax.rem(i, EXPERT_RING)]).astype(BF16)
        act = _silu(_dot(x, wg_s[...])) * _dot(x, wu_s[...])
        ys_ref[...] = _pack_words(_dot(act.astype(BF16), wd_s[...]))


def _experts(xs, tile_expert, tile_block, n_tiles, tile_segment, next_expert, wg, wu, wd):
    d, f = wg.shape[1], wg.shape[2]
    anywhere = pl.BlockSpec(memory_space=pl.ANY)
    grid_spec = pltpu.PrefetchScalarGridSpec(
        num_scalar_prefetch=5,
        grid=(xs.shape[0] // MOE_TILE,),
        in_specs=[anywhere, anywhere, anywhere, anywhere],
        out_specs=pl.BlockSpec((MOE_TILE, d // 2), lambda i, te, tb, nt, seg, nxt: (tb[i], 0)),
        scratch_shapes=[pltpu.VMEM((EXPERT_RING, MOE_TILE, d // 2), U32), pltpu.SemaphoreType.DMA((EXPERT_RING,)),
                        pltpu.VMEM((2, d, f), F32), pltpu.VMEM((2, d, f), F32), pltpu.VMEM((2, f, d), F32),
                        pltpu.VMEM((d, f), BF16), pltpu.VMEM((d, f), BF16), pltpu.VMEM((f, d), BF16),
                        pltpu.SemaphoreType.DMA((2, 3))],
    )
    return pl.pallas_call(
        _experts_kernel,
        grid_spec=grid_spec,
        out_shape=jax.ShapeDtypeStruct(xs.shape, U32),
        compiler_params=_cparams(("arbitrary",)),
        name="experts",
    )(tile_expert, tile_block, n_tiles, tile_segment, next_expert, xs, wg, wu, wd)


def _gather_rows(table, idx):
    m = idx.shape[0]
    w = table.shape[1]
    win = SC_WINDOW // 2
    n_cores, n_workers = _sc_workers()
    per_worker = m // n_workers
    n_pairs = per_worker // (2 * win)
    assert per_worker * n_workers == m and n_pairs * 2 * win == per_worker
    mesh = plsc.VectorSubcoreMesh(core_axis_name="core", subcore_axis_name="subcore")

    @functools.partial(
        pl.kernel, mesh=mesh, out_type=jax.ShapeDtypeStruct((m, w), table.dtype),
        scratch_types=[pltpu.VMEM((win,), jnp.int32), pltpu.VMEM((win,), jnp.int32),
                       pltpu.VMEM((win, w), table.dtype), pltpu.VMEM((win, w), table.dtype),
                       pltpu.SemaphoreType.DMA, pltpu.SemaphoreType.DMA])
    def gather(table_hbm, idx_hbm, out_hbm, idx_a, idx_b, rows_a, rows_b, sem_a, sem_b):
        worker = lax.axis_index("subcore") * n_cores + lax.axis_index("core")
        start = worker * per_worker

        def request(first, idx_v, rows_v, sem):
            pltpu.sync_copy(idx_hbm.at[pl.ds(first, win)], idx_v)
            pltpu.async_copy(table_hbm.at[idx_v], rows_v, sem)

        def deliver(first, idx_v, rows_v, sem):
            pltpu.make_async_copy(table_hbm.at[idx_v], rows_v, sem).wait()
            pltpu.sync_copy(rows_v, out_hbm.at[pl.ds(first, win)])

        request(pl.multiple_of(start, win), idx_a, rows_a, sem_a)

        @pl.loop(0, n_pairs)
        def _(pair):
            first_a = pl.multiple_of(start + pair * 2 * win, win)
            first_b = pl.multiple_of(first_a + win, win)
            request(first_b, idx_b, rows_b, sem_b)
            deliver(first_a, idx_a, rows_a, sem_a)

            @pl.when(pair + 1 < n_pairs)
            def _():
                request(pl.multiple_of(first_b + win, win), idx_a, rows_a, sem_a)

            deliver(first_b, idx_b, rows_b, sem_b)

    return gather(table, idx)


def _combine_kernel(rows_ref, topw_ref, h_ref, x1_ref, g2_ref, sg_ref, su_ref, sd_ref, ln2g_ref, ln2b_ref,
                    o_ref, *, alpha):
    h = _unpack_words(h_ref[...]).astype(BF16)
    act = _silu(_dot(h, sg_ref[...])) * _dot(h, su_ref[...])
    y = _dot(act.astype(BF16), sd_ref[...])
    w = topw_ref[...].T
    for k in range(TOP_K):
        y = y + w[:, k:k + 1] * _unpack_words(rows_ref[k])
    o_ref[...] = _normalize(alpha * x1_ref[...] + g2_ref[...] * y) * ln2g_ref[...] + ln2b_ref[...]


def _combine(gathered, topw, h2p, x1, g2, sg, su, sd, ln2_g, ln2_b, alpha):
    t, d = x1.shape
    b, k, s = topw.shape
    per_b = s // MOE_TOK
    fs = sg.shape[1]
    rows = pl.BlockSpec((MOE_TOK, d), lambda i: (i, 0))
    packed = pl.BlockSpec((MOE_TOK, d // 2), lambda i: (i, 0))
    vec = pl.BlockSpec((1, d), lambda i: (0, 0))
    return pl.pallas_call(
        functools.partial(_combine_kernel, alpha=alpha),
        grid=(t // MOE_TOK,),
        in_specs=[pl.BlockSpec((k, MOE_TOK, d // 2), lambda i: (0, i, 0)),
                  pl.BlockSpec((None, k, MOE_TOK), lambda i: (i // per_b, 0, i % per_b)),
                  packed, rows,
                  pl.BlockSpec((None, 1, d), lambda i: (i // per_b, 0, 0)),
                  pl.BlockSpec((d, fs), lambda i: (0, 0)),
                  pl.BlockSpec((d, fs), lambda i: (0, 0)),
                  pl.BlockSpec((fs, d), lambda i: (0, 0)),
                  vec, vec],
        out_specs=rows,
        out_shape=jax.ShapeDtypeStruct((t, d), F32),
        compiler_params=_cparams(("arbitrary",)),
        name="combine",
    )(gathered, topw, h2p, x1, g2, sg, su, sd, ln2_g.reshape(1, d), ln2_b.reshape(1, d))


def _moe(h2p, topi, topw, cnt, x1, g2, wg, wu, wd, sg, su, sd, ln2_g, ln2_b, alpha):
    b, s, d = x1.shape
    t = b * s
    cnt = cnt[:, 0].astype(jnp.int32)
    tiles_e = (cnt + (MOE_TILE - 1)) // MOE_TILE
    tiles_cum = jnp.cumsum(tiles_e)
    seg_off = (tiles_cum - tiles_e) * MOE_TILE
    n_tiles_max = t * TOP_K // MOE_TILE + N_EXPERTS
    tile_block = jnp.minimum(jnp.arange(n_tiles_max, dtype=jnp.int32), tiles_cum[-1] - 1)
    tile_expert = jnp.sum((tiles_cum[None, :] <= tile_block[:, None]).astype(jnp.int32), axis=1)
    n_tiles = tiles_cum[-1:].astype(jnp.int32)
    present = tiles_e > 0
    seg_of_expert = jnp.cumsum(present.astype(jnp.int32)) - 1
    later = jnp.where(present, jnp.arange(N_EXPERTS, dtype=jnp.int32), N_EXPERTS)
    next_present = jnp.concatenate([lax.cummin(later, reverse=True)[1:], jnp.full((1,), N_EXPERTS, jnp.int32)])
    next_present = jnp.where(next_present < N_EXPERTS, next_present, -1)
    tile_segment = seg_of_expert[tile_expert].astype(jnp.int32)
    next_expert = next_present[tile_expert].astype(jnp.int32)

    dest = jnp.transpose(_plan(topi, seg_off), (1, 0, 2)).reshape(TOP_K * t)
    j = jnp.arange(MOE_TILE, dtype=jnp.int32)[None, :]
    n_pad = (tiles_e * MOE_TILE - cnt)[:, None]
    spare = (n_tiles_max - 1) * MOE_TILE + j
    zero_idx = jnp.where(j < n_pad, (seg_off + cnt)[:, None] + j, spare).reshape(N_EXPERTS * MOE_TILE)
    xs = _scatter_rows(h2p, dest, zero_idx.astype(jnp.int32), n_tiles_max * MOE_TILE)
    ys = _experts(xs, tile_expert, tile_block, n_tiles, tile_segment, next_expert, wg, wu, wd)
    gathered = _gather_rows(ys, dest)
    out = _combine(gathered.reshape(TOP_K, t, d // 2), topw, h2p, x1.reshape(t, d), g2, sg, su, sd,
                   ln2_g, ln2_b, alpha)
    return out.reshape(b, s, d)


def kernel(x, c, ctx, c_ctx, w_ada, b_ada, w_in, hg_lb_fwd, hg_lb_bwd, hg_norm_g, na_rpb, w_branch_a, w_branch_b, w_out, ln1_g, ln1_b, w_router, router_bias, w_e_gate, w_e_up, w_e_down, w_sh_gate, w_sh_up, w_sh_down, ln2_g, ln2_b):
    depth = w_ada.shape[0]
    assert depth == 1, "single-layer block"
    b, s, d = x.shape
    alpha = (2.0 * depth) ** 0.25
    l = 0
    lb_fwd = jnp.cumsum(jax.nn.softmax(hg_lb_fwd.astype(F32), axis=0), axis=0)[l]
    lb_bwd = jnp.cumsum(jax.nn.softmax(hg_lb_bwd.astype(F32), axis=0), axis=0)[l]

    cond_rows = jnp.concatenate([c, c_ctx[None, :], jnp.zeros((8 - b - 1, d), F32)], axis=0)
    mod = _ada(cond_rows, w_ada[l], b_ada[l])
    sh1, sc1, g1, sh2, sc2, g2 = [m[:b, None, :] for m in jnp.split(mod, 6, axis=-1)]
    csh1, csc1 = [jnp.broadcast_to(m[b:b + 1, None, :], (b, 1, d)) for m in jnp.split(mod, 6, axis=-1)[:2]]

    w_in_b = w_in[l].astype(BF16)
    p = _inproj(x, sh1, sc1, w_in_b, tuple(range(N_SECTIONS)))
    pc = _inproj(ctx, csh1, csc1, w_in_b, CTX_SECTIONS)

    o_f, o_b = _hgrn(p, pc, lb_fwd, lb_bwd)
    y_na = _natten(p, pc, *_na_tables(na_rpb[l], s))

    x1, h2, topi, topw, cnt = _merge(o_f, o_b, p, y_na, x, g1, sh2, sc2, hg_norm_g[l], ln1_g[l], ln1_b[l],
                                     w_branch_a[l].astype(BF16), w_branch_b[l].astype(BF16),
                                     w_out[l].astype(BF16), w_router[l].T, router_bias[l], alpha)

    return _moe(h2, topi, topw, cnt, x1, g2,
                w_e_gate[l], w_e_up[l], w_e_down[l],
                w_sh_gate[l].astype(BF16), w_sh_up[l].astype(BF16), w_sh_down[l].astype(BF16),
                ln2_g[l], ln2_b[l], alpha)
```
